```python
import jax, jax.numpy as jnp
from jax import lax
import numpy as np

D_MODEL = 1024
BATCH = 8
SEQ = 2048
DEPTH = 1

HG_HEADS = 8
HG_KEY = 128
HG_VAL = D_MODEL // HG_HEADS
HG_FDIM = HG_HEADS * HG_KEY
HG_VDIM = HG_HEADS * HG_VAL
HG_CHUNK = 64
ATT_GROUPS = ((128, 1), (512, 4), (2048, 16))
N_GROUPS = 3
ATT_HEADS = 8
ATT_HEAD_DIM = 64
ATT_DIM = ATT_HEADS * ATT_HEAD_DIM
ROPE_THETA = 10000.0
N_BRANCH = 2
NORM_EPS = 1e-6
IN_SIZES = (HG_FDIM, HG_FDIM, HG_VDIM, HG_VDIM, 3 * N_GROUPS * ATT_DIM, ATT_DIM, N_BRANCH * D_MODEL)
IN_COLS = HG_FDIM * 2 + HG_VDIM * 2 + 3 * N_GROUPS * ATT_DIM + ATT_DIM + N_BRANCH * D_MODEL
IN_SPLITS = (HG_FDIM, 2 * HG_FDIM, 2 * HG_FDIM + HG_VDIM, 2 * HG_FDIM + 2 * HG_VDIM,
             2 * HG_FDIM + 2 * HG_VDIM + 3 * N_GROUPS * ATT_DIM,
             2 * HG_FDIM + 2 * HG_VDIM + 3 * N_GROUPS * ATT_DIM + ATT_DIM)

kernel_name = "hybrid_hgrn2_dilated_attn_gated_merge"


def rmsnorm(x, w):
    xf = x.astype(jnp.float32)
    y = xf * lax.rsqrt(jnp.mean(xf * xf, axis=-1, keepdims=True) + NORM_EPS)
    return (y * w.astype(jnp.float32)).astype(x.dtype)


def rope_tables(positions, dim):
    inv_freq = ROPE_THETA ** (-jnp.arange(0, dim, 2, dtype=jnp.float32) / dim)
    ang = positions.astype(jnp.float32)[..., None] * inv_freq
    return jnp.cos(ang), jnp.sin(ang)


def apply_rope(t, cos, sin):
    c = cos[:, :, None, None, :]
    s = sin[:, :, None, None, :]
    t1, t2 = jnp.split(t, 2, axis=-1)
    return jnp.concatenate([t1 * c - t2 * s, t2 * c + t1 * s], axis=-1)


def hgrn2_chunked(q, k, logf, v):
    B, S, H, K = q.shape
    V = v.shape[-1]
    C = HG_CHUNK
    nC = S // C

    def chunks(t):
        return t.reshape(B, nC, C, H, t.shape[-1]).transpose(1, 0, 3, 2, 4)

    causal = jnp.tril(jnp.ones((C, C), dtype=bool))

    def step(state, xs):
        qc, kc, gc, vc = xs
        b = jnp.cumsum(gc, axis=2)
        diff = b[:, :, :, None, :] - b[:, :, None, :, :]
        decay = jnp.exp(jnp.where(causal[:, :, None], diff, -jnp.inf))
        scores = jnp.einsum('bhtk,bhtsk,bhsk->bhts', qc, decay, kc)
        o = (jnp.einsum('bhts,bhsv->bhtv', scores, vc)
             + jnp.einsum('bhtk,bhkv->bhtv', qc * jnp.exp(b), state))
        b_last = b[:, :, -1:, :]
        state = (jnp.exp(b_last)[:, :, 0, :, None] * state
                 + jnp.einsum('bhsk,bhsv->bhkv', kc * jnp.exp(b_last - b), vc))
        return state, o

    s0 = jnp.zeros((B, H, K, V), jnp.float32)
    _, o = lax.scan(step, s0, (chunks(q), chunks(k), chunks(logf), chunks(v)))
    return o.transpose(1, 0, 3, 2, 4).reshape(B, S, H, V)


def dilated_window_attention(q, k, v, window, dilation):
    B, S, H, E = q.shape
    n = window // dilation
    M = S // dilation
    nblk = -(-M // n)
    Mp = nblk * n

    def to_residue(t):
        t = t.reshape(B, M, dilation, H, E).transpose(0, 2, 3, 1, 4)
        return jnp.pad(t, ((0, 0), (0, 0), (0, 0), (0, Mp - M), (0, 0)))

    def key_blocks(t):
        t = jnp.pad(t, ((0, 0), (0, 0), (0, 0), (n, 0), (0, 0))).reshape(B, dilation, H, nblk + 1, n, E)
        return jnp.concatenate([t[:, :, :, :-1], t[:, :, :, 1:]], axis=-2)

    qb = to_residue(q).reshape(B, dilation, H, nblk, n, E)
    kb = key_blocks(to_residue(k))
    vb = key_blocks(to_residue(v))
    s = jnp.einsum('brhcqe,brhcke->brhcqk', qb, kb) * (E ** -0.5)
    i = jnp.arange(n)[:, None]
    j = jnp.arange(2 * n)[None, :]
    dist = i + n - j
    c = jnp.arange(nblk)[:, None, None]
    valid = (dist >= 0) & (dist <= n) & (c * n - n + j >= 0)
    s = jnp.where(valid, s, -jnp.inf)
    m = jnp.max(s, axis=-1)
    p = jnp.exp(s - m[..., None])
    l = jnp.sum(p, axis=-1)
    o = jnp.einsum('brhcqk,brhcke->brhcqe', p, vb)

    def back(t):
        rest = t.shape[5:]
        t = t.reshape(B, dilation, H, Mp, *rest)[:, :, :, :M]
        t = jnp.moveaxis(t, 3, 1)
        return t.reshape(B, S, H, *rest)

    return back(o), back(m), back(l)


def hybrid_layer(x, cos, sin, norm_w, w_in, lb, hgrn_norm_w, w_branch_a, w_branch_b, w_out):
    B, S, _ = x.shape
    h = rmsnorm(x, norm_w)
    z = h @ w_in
    hq, hf, hi, hg, aqkv, ag, gates = jnp.split(z, IN_SPLITS, axis=-1)

    f = lb + (1.0 - lb) * jax.nn.sigmoid(hf.astype(jnp.float32))
    shp = (B, S, HG_HEADS, HG_KEY)
    q_a = jax.nn.silu(hq.astype(jnp.float32)).reshape(shp)
    o_a = hgrn2_chunked(q_a, (1.0 - f).reshape(shp), jnp.log(f).reshape(shp),
                        hi.astype(jnp.float32).reshape(B, S, HG_HEADS, HG_VAL))
    o_a = rmsnorm(o_a, hgrn_norm_w) * jax.nn.silu(hg.astype(jnp.float32)).reshape(B, S, HG_HEADS, HG_VAL)
    y_a = o_a.reshape(B, S, HG_VDIM).astype(x.dtype) @ w_branch_a

    aqkv = aqkv.astype(jnp.float32).reshape(B, S, 3, N_GROUPS, ATT_HEADS, ATT_HEAD_DIM)
    q_b = apply_rope(aqkv[:, :, 0], cos, sin)
    k_b = apply_rope(aqkv[:, :, 1], cos, sin)
    v_b = aqkv[:, :, 2]
    outs, maxes, dens = [], [], []
    for g, (window, dilation) in enumerate(ATT_GROUPS):
        o_g, m_g, l_g = dilated_window_attention(q_b[:, :, g], k_b[:, :, g], v_b[:, :, g], window, dilation)
        outs.append(o_g)
        maxes.append(m_g)
        dens.append(l_g)
    ms = jnp.stack(maxes)
    wts = jnp.exp(ms - jnp.max(ms, axis=0))
    den = jnp.sum(wts * jnp.stack(dens), axis=0)
    num = jnp.sum(wts[..., None] * jnp.stack(outs), axis=0)
    o_b = (num / den[..., None]).reshape(B, S, ATT_DIM) * jax.nn.silu(ag.astype(jnp.float32))
    y_b = o_b.astype(x.dtype) @ w_branch_b

    g_a, g_b = jnp.split(jax.nn.sigmoid(gates), N_BRANCH, axis=-1)
    merged = g_a * y_a + g_b * y_b
    return x + merged @ w_out


def _fwd_setup_inputs(seed: int = 0) -> dict:
    key = jax.random.key(seed)
    ks = jax.random.split(key, 12)
    x = jax.random.normal(ks[0], (BATCH, SEQ, D_MODEL), jnp.float32)
    offsets = jax.random.randint(ks[1], (BATCH, 1), 0, 4096, dtype=jnp.int32)
    positions = offsets + jnp.arange(SEQ, dtype=jnp.int32)[None, :]
    norm_w = 1.0 + 0.02 * jax.random.normal(ks[2], (DEPTH, D_MODEL), jnp.float32)
    w_in = jax.random.normal(ks[3], (DEPTH, D_MODEL, IN_COLS), jnp.float32) * D_MODEL ** -0.5
    lb_logits = 0.5 * jax.random.normal(ks[4], (DEPTH + 1, HG_FDIM), jnp.float32)
    hgrn_norm_w = 1.0 + 0.02 * jax.random.normal(ks[5], (DEPTH, HG_VAL), jnp.float32)
    w_branch_a = jax.random.normal(ks[6], (DEPTH, HG_VDIM, D_MODEL), jnp.float32) * HG_VDIM ** -0.5
    w_branch_b = jax.random.normal(ks[7], (DEPTH, ATT_DIM, D_MODEL), jnp.float32) * ATT_DIM ** -0.5
    w_out = jax.random.normal(ks[8], (DEPTH, D_MODEL, D_MODEL), jnp.float32) * D_MODEL ** -0.5
    final_norm_w = 1.0 + 0.02 * jax.random.normal(ks[9], (D_MODEL,), jnp.float32)
    return {"x": x, "positions": positions, "norm_w": norm_w, "w_in": w_in, "lb_logits": lb_logits,
            "hgrn_norm_w": hgrn_norm_w, "w_branch_a": w_branch_a, "w_branch_b": w_branch_b,
            "w_out": w_out, "final_norm_w": final_norm_w}


def _fwd_reference(x, positions, norm_w, w_in, lb_logits, hgrn_norm_w, w_branch_a, w_branch_b, w_out, final_norm_w):
    cos, sin = rope_tables(positions, ATT_HEAD_DIM)
    lower_bounds = jnp.cumsum(jax.nn.softmax(lb_logits.astype(jnp.float32), axis=0), axis=0)
    for layer in range(DEPTH):
        x = hybrid_layer(x, cos, sin, norm_w[layer], w_in[layer], lower_bounds[layer], hgrn_norm_w[layer],
                         w_branch_a[layer], w_branch_b[layer], w_out[layer])
    return rmsnorm(x, final_norm_w)


import jax as _jax
import jax.numpy as _jnp

TWIN_FORMAT = 'train_step'
FWD_PARAMS = ['x', 'positions', 'norm_w', 'w_in', 'lb_logits', 'hgrn_norm_w', 'w_branch_a', 'w_branch_b', 'w_out', 'final_norm_w']
TWIN_WEIGHTS = ['norm_w', 'w_in', 'lb_logits', 'hgrn_norm_w', 'w_branch_a', 'w_branch_b', 'w_out', 'final_norm_w']
TWIN_DIFF_INPUT = 'x'
TWIN_INPUTS = ['x', 'positions', 'norm_w', 'w_in', 'lb_logits', 'hgrn_norm_w', 'w_branch_a', 'w_branch_b', 'w_out', 'final_norm_w', 'loss_target', 'm_norm_w', 'm_w_in', 'm_lb_logits', 'm_hgrn_norm_w', 'm_w_branch_a', 'm_w_branch_b', 'm_w_out', 'm_final_norm_w', 'v_norm_w', 'v_w_in', 'v_lb_logits', 'v_hgrn_norm_w', 'v_w_branch_a', 'v_w_branch_b', 'v_w_out', 'v_final_norm_w']
TWIN_OUTPUTS = ['loss', 'grad_x', 'grad_norm_w', 'grad_w_in', 'grad_lb_logits', 'grad_hgrn_norm_w', 'grad_w_branch_a', 'grad_w_branch_b', 'grad_w_out', 'grad_final_norm_w', 'delta_norm_w', 'delta_w_in', 'delta_lb_logits', 'delta_hgrn_norm_w', 'delta_w_branch_a', 'delta_w_branch_b', 'delta_w_out', 'delta_final_norm_w', 'new_m_norm_w', 'new_m_w_in', 'new_m_lb_logits', 'new_m_hgrn_norm_w', 'new_m_w_branch_a', 'new_m_w_branch_b', 'new_m_w_out', 'new_m_final_norm_w', 'new_v_norm_w', 'new_v_w_in', 'new_v_lb_logits', 'new_v_hgrn_norm_w', 'new_v_w_branch_a', 'new_v_w_branch_b', 'new_v_w_out', 'new_v_final_norm_w']
TWIN_LEAF_KINDS = {'loss': 'loss', 'grad_x': 'grad_x', 'grad_norm_w': 'grad_w', 'grad_w_in': 'grad_w', 'grad_lb_logits': 'grad_w', 'grad_hgrn_norm_w': 'grad_w', 'grad_w_branch_a': 'grad_w', 'grad_w_branch_b': 'grad_w', 'grad_w_out': 'grad_w', 'grad_final_norm_w': 'grad_w', 'delta_norm_w': 'delta_w', 'delta_w_in': 'delta_w', 'delta_lb_logits': 'delta_w', 'delta_hgrn_norm_w': 'delta_w', 'delta_w_branch_a': 'delta_w', 'delta_w_branch_b': 'delta_w', 'delta_w_out': 'delta_w', 'delta_final_norm_w': 'delta_w', 'new_m_norm_w': 'new_m', 'new_m_w_in': 'new_m', 'new_m_lb_logits': 'new_m', 'new_m_hgrn_norm_w': 'new_m', 'new_m_w_branch_a': 'new_m', 'new_m_w_branch_b': 'new_m', 'new_m_w_out': 'new_m', 'new_m_final_norm_w': 'new_m', 'new_v_norm_w': 'new_v', 'new_v_w_in': 'new_v', 'new_v_lb_logits': 'new_v', 'new_v_hgrn_norm_w': 'new_v', 'new_v_w_branch_a': 'new_v', 'new_v_w_branch_b': 'new_v', 'new_v_w_out': 'new_v', 'new_v_final_norm_w': 'new_v'}


def _forward(args):
    return _fwd_reference(*[args[k] for k in FWD_PARAMS])


def _output_shape():
    out = _jax.eval_shape(lambda: _forward(_fwd_setup_inputs(0)))
    return out.shape, out.dtype

N_MICROBATCH = 1
ADAM_LR = 0.001
ADAM_B1 = 0.9
ADAM_B2 = 0.999
ADAM_EPS = 1e-08
ADAM_WD = 0.01
ADAM_STEP = 10
PER_EXAMPLE_BATCH_AXIS = {'x': 0, 'positions': 0, 'loss_target': 0}
SHARED_INPUTS = []
_WEIGHT_DTYPES = {'norm_w': _jnp.float32, 'w_in': _jnp.float32, 'lb_logits': _jnp.float32, 'hgrn_norm_w': _jnp.float32, 'w_branch_a': _jnp.float32, 'w_branch_b': _jnp.float32, 'w_out': _jnp.float32, 'final_norm_w': _jnp.float32}
MOMENT_SCALE = {'norm_w': 5.835805e-02, 'w_in': 1.785727e-02, 'lb_logits': 3.329464e-03, 'hgrn_norm_w': 1.044816e-01, 'w_branch_a': 3.848277e-02, 'w_branch_b': 7.154914e-03, 'w_out': 3.912642e-02, 'final_norm_w': 1.599607e+01}


def _to_microbatches(a, axis):
    t = _jnp.moveaxis(a, axis, 0)
    t = t.reshape((N_MICROBATCH, t.shape[0] // N_MICROBATCH) + t.shape[1:])
    return _jnp.moveaxis(t, 1, axis + 1)


def setup_inputs(seed: int = 0) -> dict:
    inp = _fwd_setup_inputs(seed)
    key = _jax.random.fold_in(_jax.random.key(seed), 7919)
    shape, _ = _output_shape()
    out = dict(inp)
    out["loss_target"] = _jax.random.normal(_jax.random.fold_in(key, 0), shape, _jnp.float32)
    for i, name in enumerate(TWIN_WEIGHTS):
        w = inp[name].astype(_jnp.float32)
        if MOMENT_SCALE is None:
            s = _jnp.sqrt(_jnp.mean(_jnp.square(w)) + 1e-30)
        else:
            s = MOMENT_SCALE[name]
        km, kv = _jax.random.split(_jax.random.fold_in(key, i + 1))
        out[name] = w
        out["m_" + name] = s * _jax.random.normal(km, w.shape, _jnp.float32)
        out["v_" + name] = (s * s) * _jax.random.uniform(kv, w.shape, _jnp.float32, 0.5, 1.5)
    if N_MICROBATCH > 1:
        for name, axis in PER_EXAMPLE_BATCH_AXIS.items():
            out[name] = _to_microbatches(out[name], axis)
    return {'x': out['x'], 'positions': out['positions'], 'norm_w': out['norm_w'], 'w_in': out['w_in'], 'lb_logits': out['lb_logits'], 'hgrn_norm_w': out['hgrn_norm_w'], 'w_branch_a': out['w_branch_a'], 'w_branch_b': out['w_branch_b'], 'w_out': out['w_out'], 'final_norm_w': out['final_norm_w'], 'loss_target': out['loss_target'], 'm_norm_w': out['m_norm_w'], 'm_w_in': out['m_w_in'], 'm_lb_logits': out['m_lb_logits'], 'm_hgrn_norm_w': out['m_hgrn_norm_w'], 'm_w_branch_a': out['m_w_branch_a'], 'm_w_branch_b': out['m_w_branch_b'], 'm_w_out': out['m_w_out'], 'm_final_norm_w': out['m_final_norm_w'], 'v_norm_w': out['v_norm_w'], 'v_w_in': out['v_w_in'], 'v_lb_logits': out['v_lb_logits'], 'v_hgrn_norm_w': out['v_hgrn_norm_w'], 'v_w_branch_a': out['v_w_branch_a'], 'v_w_branch_b': out['v_w_branch_b'], 'v_w_out': out['v_w_out'], 'v_final_norm_w': out['v_final_norm_w']}


def _loss(weights, diff, rest, loss_target):
    with _jax.named_scope("forward"):
        args = {**rest, TWIN_DIFF_INPUT: diff, **{k: w.astype(_WEIGHT_DTYPES[k]) for k, w in weights.items()}}
        y = _forward(args)
    with _jax.named_scope("loss_head"):
        err = _jnp.square(y.astype(_jnp.float32) - loss_target)
        return 0.5 * _jnp.sum(_jnp.mean(err, axis=-1)) if err.ndim else 0.5 * err


def _adamw(w, g, m, v):
    m = ADAM_B1 * m + (1.0 - ADAM_B1) * g
    v = ADAM_B2 * v + (1.0 - ADAM_B2) * _jnp.square(g)
    m_hat = m / (1.0 - ADAM_B1 ** ADAM_STEP)
    v_hat = v / (1.0 - ADAM_B2 ** ADAM_STEP)
    delta = -ADAM_LR * (m_hat / (_jnp.sqrt(v_hat) + ADAM_EPS) + ADAM_WD * w)
    return delta, m, v


def reference(x, positions, norm_w, w_in, lb_logits, hgrn_norm_w, w_branch_a, w_branch_b, w_out, final_norm_w, loss_target, m_norm_w, m_w_in, m_lb_logits, m_hgrn_norm_w, m_w_branch_a, m_w_branch_b, m_w_out, m_final_norm_w, v_norm_w, v_w_in, v_lb_logits, v_hgrn_norm_w, v_w_branch_a, v_w_branch_b, v_w_out, v_final_norm_w):
    given = dict(x=x, positions=positions, norm_w=norm_w, w_in=w_in, lb_logits=lb_logits, hgrn_norm_w=hgrn_norm_w, w_branch_a=w_branch_a, w_branch_b=w_branch_b, w_out=w_out, final_norm_w=final_norm_w, loss_target=loss_target, m_norm_w=m_norm_w, m_w_in=m_w_in, m_lb_logits=m_lb_logits, m_hgrn_norm_w=m_hgrn_norm_w, m_w_branch_a=m_w_branch_a, m_w_branch_b=m_w_branch_b, m_w_out=m_w_out, m_final_norm_w=m_final_norm_w, v_norm_w=v_norm_w, v_w_in=v_w_in, v_lb_logits=v_lb_logits, v_hgrn_norm_w=v_hgrn_norm_w, v_w_branch_a=v_w_branch_a, v_w_branch_b=v_w_branch_b, v_w_out=v_w_out, v_final_norm_w=v_final_norm_w)
    weights = {n: given[n] for n in TWIN_WEIGHTS}
    shared = {n: given[n] for n in SHARED_INPUTS}
    per_example = {n: given[n] for n in ['x', 'positions']}
    grad_fn = _jax.value_and_grad(_loss, argnums=(0, 1))

    def one_microbatch(ex, loss_target):
        ex = dict(ex)
        diff = ex.pop(TWIN_DIFF_INPUT)
        return grad_fn(weights, diff, {**shared, **ex}, loss_target)

    if N_MICROBATCH == 1:
        loss, (grad_w, grad_x) = one_microbatch(per_example, given["loss_target"])
    else:
        def body(carry, xs):
            loss_sum, grad_sum = carry
            l_k, (gw_k, gx_k) = one_microbatch(xs[0], xs[1])
            with _jax.named_scope("update"):
                return (loss_sum + l_k, _jax.tree.map(_jnp.add, grad_sum, gw_k)), gx_k

        init = (_jnp.zeros((), _jnp.float32), _jax.tree.map(_jnp.zeros_like, weights))
        (loss, grad_w), grad_x = _jax.lax.scan(body, init, (per_example, given["loss_target"]))
    with _jax.named_scope("update"):
        delta_w, new_m, new_v = {}, {}, {}
        for n in TWIN_WEIGHTS:
            delta_w[n], new_m[n], new_v[n] = _adamw(weights[n], grad_w[n], given["m_" + n], given["v_" + n])
    return (loss, grad_x, *[grad_w[n] for n in TWIN_WEIGHTS], *[delta_w[n] for n in TWIN_WEIGHTS],
            *[new_m[n] for n in TWIN_WEIGHTS], *[new_v[n] for n in TWIN_WEIGHTS])
```

```python
import functools

import jax
import jax.numpy as jnp
from jax import lax
from jax.experimental import pallas as pl
from jax.experimental.pallas import tpu as pltpu

F32 = jnp.float32
BF16 = jnp.bfloat16
MESH = pl.DeviceIdType.MESH

S = 2048
D = 1024
NDEV = 8
HEADS = 8
CHUNK = 64
SUB = 16
TBLK = 256
EXP_CLAMP = 80.0
EPS = 1e-6
IN_COLS = 11264
SHARD_COLS = IN_COLS // NDEV
ATT_DILS = (1, 4, 16)
ATT_SCALE = 64 ** -0.5
LANES = 128

ADAM_LR, ADAM_B1, ADAM_B2, ADAM_EPS, ADAM_WD, ADAM_STEP = 0.001, 0.9, 0.999, 1e-08, 0.01, 10

VMEM_LIMIT = 56 * 1024 * 1024


def _cp(sem=None, **kw):
    return pltpu.CompilerParams(dimension_semantics=sem, vmem_limit_bytes=VMEM_LIMIT, **kw)


def _dot(a, b):
    return jnp.dot(a, b, preferred_element_type=F32)


def _dot_nt(a, b):
    return lax.dot_general(a, b, (((1,), (1,)), ((), ())), preferred_element_type=F32)


def _dot_tn(a, b):
    return lax.dot_general(a, b, (((0,), (0,)), ((), ())), preferred_element_type=F32)


def _split2(x):
    hi = x.astype(BF16)
    lo = (x - hi.astype(F32)).astype(BF16)
    return hi, lo


def _split3(x):
    hi = x.astype(BF16)
    r = x - hi.astype(F32)
    mid = r.astype(BF16)
    lo = (r - mid.astype(F32)).astype(BF16)
    return hi, mid, lo


def _dot_ones(ones_bf16, x):
    hi, mid, lo = _split3(x)
    return _dot(ones_bf16, hi) + _dot(ones_bf16, mid) + _dot(ones_bf16, lo)


def _dot3(dotfn, a, b):
    ah, al = _split2(a)
    bh, bl = _split2(b)
    return dotfn(ah, bh) + dotfn(ah, bl) + dotfn(al, bh)


def _silu(x):
    return x * jax.nn.sigmoid(x)


def _dsilu(x):
    s = jax.nn.sigmoid(x)
    return s * (1.0 + x * (1.0 - s))


def _mesh_pos():
    return lax.axis_index("x"), lax.axis_index("y"), lax.axis_index("c")


def _shard_of(ref, a, d):
    if a == 0:
        return ref.at[:, pl.ds(pl.multiple_of(d * SHARD_COLS, LANES), SHARD_COLS)]
    if a == 2:
        return ref.at[:, pl.ds(pl.multiple_of(d * LANES, LANES), LANES)]
    return ref.at[pl.ds(pl.multiple_of(d * 128, 128), 128), :]


FULL_SHAPES = ((D, IN_COLS), (D, D), (512, D), (D, D))
SHARD_SHAPES = ((D, SHARD_COLS), (128, D), (512, 128), (128, D))


def _allgather_weights(shards):
    n = len(shards)

    def body(*refs):
        ins, outs = refs[:n], refs[n:2 * n]
        send_sems, recv_sems, local_sems = refs[2 * n:]
        x, y, c = _mesh_pos()
        me, sibling = (x, y, c), (x, y, 1 - c)
        chips = [(1 - x, y), (x, 1 - y), (1 - x, 1 - y)]

        def blk(a, p):
            return _shard_of(outs[a], a, 4 * p[0] + 2 * p[1] + p[2])

        def copy(a, k, block, to, src=None):
            return pltpu.make_async_remote_copy(
                src_ref=blk(a, block) if src is None else src, dst_ref=blk(a, block),
                send_sem=send_sems.at[a * 7 + k], recv_sem=recv_sems.at[a * 7 + k],
                device_id=to, device_id_type=MESH)

        mine = [pltpu.make_async_copy(ins[a], blk(a, me), local_sems.at[a]) for a in range(n)]
        for cp in mine:
            cp.start()
        first = []
        for a in range(n):
            first += [copy(a, 1 + j, me, (*chip, c), src=ins[a]) for j, chip in enumerate(chips)]
        for a in range(n):
            first.append(copy(a, 0, me, sibling, src=ins[a]))
        for cp in first:
            cp.start()
        passed = []
        for j, chip in enumerate(chips):
            for a in range(n):
                copy(a, 1 + j, (*chip, c), me).wait_recv()
                fwd = copy(a, 4 + j, (*chip, c), sibling)
                fwd.start()
                passed.append(fwd)
        for a in range(n):
            copy(a, 0, sibling, me).wait_recv()
        for j, chip in enumerate(chips):
            for a in range(n):
                copy(a, 4 + j, (*chip, 1 - c), me).wait_recv()
        for cp in first + passed:
            cp.wait_send()
        for cp in mine:
            cp.wait()

    any_spec = pl.BlockSpec(memory_space=pl.ANY)
    return pl.pallas_call(
        body, name="allgather_weights",
        out_shape=[jax.ShapeDtypeStruct(FULL_SHAPES[a], BF16) for a in range(n)],
        in_specs=[any_spec] * n, out_specs=[any_spec] * n,
        scratch_shapes=[pltpu.SemaphoreType.DMA((7 * n,)), pltpu.SemaphoreType.DMA((7 * n,)),
                        pltpu.SemaphoreType.DMA((n,))],
    )(*shards)


def _exchange_sibling(gb):
    n = len(gb)

    def body(*refs):
        ins, outs = refs[:n], refs[n:2 * n]
        send_sems, recv_sems = refs[2 * n:]
        x, y, c = _mesh_pos()
        sibling = (x, y, 1 - c)
        copies = []
        for a in range(n):
            for q in range(4):
                copies.append(pltpu.make_async_remote_copy(
                    src_ref=_shard_of(ins[a], a, 2 * q + (1 - c)), dst_ref=outs[a].at[q],
                    send_sem=send_sems.at[a * 4 + q], recv_sem=recv_sems.at[a * 4 + q],
                    device_id=sibling, device_id_type=MESH))
        for cp in copies:
            cp.start()
        for cp in copies:
            cp.wait()

    any_spec = pl.BlockSpec(memory_space=pl.ANY)
    return pl.pallas_call(
        body, name="grads_to_sibling",
        out_shape=[jax.ShapeDtypeStruct((4,) + SHARD_SHAPES[a], BF16) for a in range(n)],
        in_specs=[any_spec] * n, out_specs=[any_spec] * n,
        scratch_shapes=[pltpu.SemaphoreType.DMA((4 * n,)), pltpu.SemaphoreType.DMA((4 * n,))],
    )(*gb)


def _exchange_chips(pb, small):
    n = len(pb)

    def body(*refs):
        ins, small_ref = refs[:n], refs[n]
        outs, small_out = refs[n + 1:2 * n + 1], refs[2 * n + 1]
        send_sems, recv_sems, ssend, srecv, local_sem = refs[2 * n + 2:]
        x, y, c = _mesh_pos()
        chips = [(1 - x, y), (x, 1 - y), (1 - x, 1 - y)]
        me = 4 * x + 2 * y + c
        copies = []
        for a in range(n):
            for k, chip in enumerate(chips):
                copies.append(pltpu.make_async_remote_copy(
                    src_ref=ins[a].at[2 * chip[0] + chip[1]], dst_ref=outs[a].at[k],
                    send_sem=send_sems.at[a * 3 + k], recv_sem=recv_sems.at[a * 3 + k],
                    device_id=(*chip, c), device_id_type=MESH))
        for r in range(1, NDEV):
            peer = (1 - x if r & 4 else x, 1 - y if r & 2 else y, 1 - c if r & 1 else c)
            copies.append(pltpu.make_async_remote_copy(
                src_ref=small_ref, dst_ref=small_out.at[me],
                send_sem=ssend.at[r - 1], recv_sem=srecv.at[r - 1],
                device_id=peer, device_id_type=MESH))
        own = pltpu.make_async_copy(small_ref, small_out.at[me], local_sem)
        own.start()
        for cp in copies:
            cp.start()
        for cp in copies:
            cp.wait()
        own.wait()

    any_spec = pl.BlockSpec(memory_space=pl.ANY)
    return pl.pallas_call(
        body, name="grads_between_chips",
        out_shape=[jax.ShapeDtypeStruct((3,) + SHARD_SHAPES[a], BF16) for a in range(n)]
        + [jax.ShapeDtypeStruct((NDEV,) + small.shape, F32)],
        in_specs=[any_spec] * (n + 1), out_specs=[any_spec] * (n + 1),
        scratch_shapes=[pltpu.SemaphoreType.DMA((3 * n,)), pltpu.SemaphoreType.DMA((3 * n,)),
                        pltpu.SemaphoreType.DMA((NDEV - 1,)), pltpu.SemaphoreType.DMA((NDEV - 1,)),
                        pltpu.SemaphoreType.DMA],
    )(*pb, small)


def _shard_tiles(a):
    rows, cols = SHARD_SHAPES[a]
    tr = min(rows, 256)
    return (tr, cols), rows // tr


def _full_index(a, d, i):
    (tr, _), nt = _shard_tiles(a)
    if a in (0, 2):
        return (i, d)
    return (d * nt + i, 0)


def _cast_bf16(x, name):
    rows, cols = x.shape
    tr = min(rows, 256)

    def body(x_ref, o_ref):
        o_ref[...] = x_ref[...].astype(BF16)

    return pl.pallas_call(
        body, name=name, out_shape=jax.ShapeDtypeStruct(x.shape, BF16), grid=(rows // tr,),
        in_specs=[pl.BlockSpec((tr, cols), lambda i: (i, 0))],
        out_specs=pl.BlockSpec((tr, cols), lambda i: (i, 0)),
        compiler_params=_cp(("parallel",)),
    )(x)


def _chip_partials(a, g_full, r1, core):
    tile, nt = _shard_tiles(a)

    def body(c_ref, g_ref, r_ref, o_ref):
        o_ref[0] = (g_ref[...] + r_ref[0].astype(F32)).astype(BF16)

    grid_spec = pltpu.PrefetchScalarGridSpec(
        num_scalar_prefetch=1, grid=(4, nt),
        in_specs=[pl.BlockSpec(tile, lambda q, i, c: _full_index(a, 2 * q + c[0], i)),
                  pl.BlockSpec((1,) + tile, lambda q, i, c: (q, i, 0))],
        out_specs=pl.BlockSpec((1,) + tile, lambda q, i, c: (q, i, 0)))
    return pl.pallas_call(
        body, name=f"chip_partials_{a}", grid_spec=grid_spec,
        out_shape=jax.ShapeDtypeStruct((4,) + SHARD_SHAPES[a], BF16),
        compiler_params=_cp(("parallel", "parallel")),
    )(core, g_full, r1)


def _adam(w, g, m, v):
    m = ADAM_B1 * m + (1.0 - ADAM_B1) * g
    v = ADAM_B2 * v + (1.0 - ADAM_B2) * (g * g)
    m_hat = m / (1.0 - ADAM_B1 ** ADAM_STEP)
    v_hat = v / (1.0 - ADAM_B2 ** ADAM_STEP)
    delta = -ADAM_LR * (m_hat / (jnp.sqrt(v_hat) + ADAM_EPS) + ADAM_WD * w)
    return delta, m, v


def _reduce_and_update(a, w, m, v, g_full, r1, r2, pos):
    tile, nt = _shard_tiles(a)

    def body(p_ref, w_ref, m_ref, v_ref, g_ref, r1_ref, r2_ref, go_ref, do_ref, mo_ref, vo_ref):
        g = g_ref[...] + r1_ref[0].astype(F32)
        g = g + r2_ref[0].astype(F32)
        g = g + r2_ref[1].astype(F32)
        g = g + r2_ref[2].astype(F32)
        delta, m_new, v_new = _adam(w_ref[...], g, m_ref[...], v_ref[...])
        go_ref[...] = g
        do_ref[...] = delta
        mo_ref[...] = m_new
        vo_ref[...] = v_new

    own = pl.BlockSpec(tile, lambda i, p: (i, 0))
    grid_spec = pltpu.PrefetchScalarGridSpec(
        num_scalar_prefetch=1, grid=(nt,),
        in_specs=[own, own, own,
                  pl.BlockSpec(tile, lambda i, p: _full_index(a, p[0], i)),
                  pl.BlockSpec((1,) + tile, lambda i, p: (p[1], i, 0)),
                  pl.BlockSpec((3,) + tile, lambda i, p: (0, i, 0))],
        out_specs=[own] * 4)
    shp = jax.ShapeDtypeStruct(w.shape, F32)
    return pl.pallas_call(
        body, name=f"reduce_update_{a}", grid_spec=grid_spec, out_shape=[shp] * 4,
        compiler_params=_cp(("parallel",)),
    )(pos, w, m, v, g_full, r1, r2)


def _small_update(gathered, norm_w, lb_logits, hnw, fnw, moments):
    m_nw, m_lb, m_hn, m_fn, v_nw, v_lb, v_hn, v_fn = moments

    def body(g_ref, nw, lb, hn, fn, mnw, mlb, mhn, mfn, vnw, vlb, vhn, vfn,
             loss_o, g_nw, g_lb, g_hn, g_fn, d_nw, d_lb, d_hn, d_fn,
             mo_nw, mo_lb, mo_hn, mo_fn, vo_nw, vo_lb, vo_hn, vo_fn):
        tot = g_ref[0]
        for d in range(1, NDEV):
            tot = tot + g_ref[d]
        loss_o[...] = tot[4:5, 0:LANES]
        logits = lb[...]
        lbv = jax.nn.sigmoid(logits[0:1] - logits[1:2])
        chain = tot[1:2] * lbv * (1.0 - lbv)
        grads = (tot[0:1], jnp.concatenate([chain, -chain], axis=0), tot[2:3, 0:LANES], tot[3:4])
        outs = ((nw, mnw, vnw, g_nw, d_nw, mo_nw, vo_nw), (lb, mlb, vlb, g_lb, d_lb, mo_lb, vo_lb),
                (hn, mhn, vhn, g_hn, d_hn, mo_hn, vo_hn), (fn, mfn, vfn, g_fn, d_fn, mo_fn, vo_fn))
        for g, (w_r, m_r, v_r, g_o, d_o, m_o, v_o) in zip(grads, outs):
            delta, m_new, v_new = _adam(w_r[...], g, m_r[...], v_r[...])
            g_o[...] = g
            d_o[...] = delta
            m_o[...] = m_new
            v_o[...] = v_new

    shapes = [norm_w.shape, lb_logits.shape, hnw.shape, fnw.shape]
    out_shape = [jax.ShapeDtypeStruct((1, LANES), F32)] + [jax.ShapeDtypeStruct(s, F32) for s in shapes] * 4
    return pl.pallas_call(body, name="small_update", out_shape=out_shape, compiler_params=_cp())(
        gathered, norm_w, lb_logits, hnw, fnw, m_nw, m_lb, m_hn, m_fn, v_nw, v_lb, v_hn, v_fn)


def _rmsnorm_in(x, norm_w):
    tr = 512

    def body(x_ref, w_ref, h_ref):
        xv = x_ref[...]
        r = lax.rsqrt(jnp.mean(xv * xv, axis=-1, keepdims=True) + EPS)
        h_ref[...] = (xv * r * w_ref[...]).astype(BF16)

    return pl.pallas_call(
        body, name="rmsnorm_in", out_shape=jax.ShapeDtypeStruct((S, D), BF16), grid=(S // tr,),
        in_specs=[pl.BlockSpec((tr, D), lambda i: (i, 0)), pl.BlockSpec((1, D), lambda i: (0, 0))],
        out_specs=pl.BlockSpec((tr, D), lambda i: (i, 0)),
        compiler_params=_cp(("parallel",)),
    )(x, norm_w)


def _in_proj(h, w_in):
    tn = 1024

    def body(h_ref, w_ref, z_ref):
        z_ref[...] = _dot(h_ref[...], w_ref[...])

    return pl.pallas_call(
        body, name="in_proj", out_shape=jax.ShapeDtypeStruct((S, IN_COLS), F32), grid=(IN_COLS // tn,),
        in_specs=[pl.BlockSpec((S, D), lambda j: (0, 0)), pl.BlockSpec((D, tn), lambda j: (0, j))],
        out_specs=pl.BlockSpec((S, tn), lambda j: (0, j)),
        compiler_params=_cp(("parallel",)),
    )(h, w_in)


def _block_tri(n, block, upper=False):
    r = lax.broadcasted_iota(jnp.int32, (n, n), 0)
    c = lax.broadcasted_iota(jnp.int32, (n, n), 1)
    keep = (c >= r) if upper else (c <= r)
    return jnp.where(keep & ((r // block) == (c // block)), 1.0, 0.0).astype(BF16)


def _tril_mask(n):
    r = lax.broadcasted_iota(jnp.int32, (n, n), 0)
    c = lax.broadcasted_iota(jnp.int32, (n, n), 1)
    return c <= r


def _hgrn_inputs(hq, hf, hi, lb):
    sg = jax.nn.sigmoid(hf)
    f = lb + (1.0 - lb) * sg
    g = jnp.log(f)
    b = _dot_ones(_block_tri(TBLK, CHUNK), g)
    return _silu(hq), 1.0 - f, g, hi, sg, f, b


def _chunk_scores(q, k, b, bex, r0, mask):
    rows = slice(r0, r0 + CHUNK)
    parts, qs_l, ek_l, eq_l = [], [], [], []
    for i in range(CHUNK // SUB):
        ri = slice(r0 + SUB * i, r0 + SUB * (i + 1))
        base = bex[r0 + SUB * i:r0 + SUB * i + 1]
        eq = jnp.exp(b[ri] - base)
        ek = jnp.exp(jnp.minimum(base - b[rows], EXP_CLAMP))
        qs = q[ri] * eq
        parts.append(_dot_nt(qs.astype(BF16), (k[rows] * ek).astype(BF16)))
        qs_l.append(qs)
        ek_l.append(ek)
        eq_l.append(eq)
    return jnp.where(mask, jnp.concatenate(parts, axis=0), 0.0), qs_l, ek_l, eq_l


def _hgrn_fwd(z, lbv, hnw):
    ntb, nch = S // TBLK, TBLK // CHUNK

    def body(hq_ref, hf_ref, hi_ref, hg_ref, lb_ref, hnw_ref, o_ref, oa_ref, st_ref, state):
        @pl.when(pl.program_id(1) == 0)
        def _():
            state[...] = jnp.zeros_like(state)

        q, k, g, v, _, _, b = _hgrn_inputs(hq_ref[...], hf_ref[...], hi_ref[...], lb_ref[...])
        bex = b - g
        eb = jnp.exp(b)
        mask = _tril_mask(CHUNK)
        st = state[...]
        outs = []
        for c in range(nch):
            r0 = c * CHUNK
            rows = slice(r0, r0 + CHUNK)
            a, _, _, _ = _chunk_scores(q, k, b, bex, r0, mask)
            vb = v[rows].astype(BF16)
            b_last = b[r0 + CHUNK - 1:r0 + CHUNK]
            qe = (q[rows] * eb[rows]).astype(BF16)
            outs.append(_dot(a.astype(BF16), vb) + _dot_nt(qe, st.astype(BF16)))
            st_ref[0, c] = st
            ke = (k[rows] * jnp.exp(b_last - b[rows])).astype(BF16)
            st = st * jnp.exp(b_last) + _dot_tn(vb, ke)
        state[...] = st
        o = jnp.concatenate(outs, axis=0)
        o_ref[...] = o
        r = lax.rsqrt(jnp.mean(o * o, axis=-1, keepdims=True) + EPS)
        oa_ref[...] = (o * r * hnw_ref[...] * _silu(hg_ref[...])).astype(BF16)

    def zcol(off):
        return pl.BlockSpec((TBLK, 128), lambda h, t: (t, off + h))

    out_blk = pl.BlockSpec((TBLK, 128), lambda h, t: (t, h))
    return pl.pallas_call(
        body, name="hgrn_fwd", grid=(HEADS, ntb),
        out_shape=[jax.ShapeDtypeStruct((S, D), F32), jax.ShapeDtypeStruct((S, D), BF16),
                   jax.ShapeDtypeStruct((HEADS, S // CHUNK, 128, 128), F32)],
        in_specs=[zcol(0), zcol(8), zcol(16), zcol(24),
                  pl.BlockSpec((1, 128), lambda h, t: (0, h)), pl.BlockSpec((1, 128), lambda h, t: (0, 0))],
        out_specs=[out_blk, out_blk, pl.BlockSpec((1, nch, 128, 128), lambda h, t: (h, t, 0, 0))],
        scratch_shapes=[pltpu.VMEM((128, 128), F32)],
        compiler_params=_cp(("parallel", "arbitrary")),
    )(z, z, z, z, lbv, hnw)


def _half_mask():
    lane = lax.broadcasted_iota(jnp.int32, (1, LANES), 1)
    return (lane % 64) < 32


def _rope(t, cc, ss, first_half):
    partner = jnp.where(first_half, pltpu.roll(t, 96, 1), pltpu.roll(t, 32, 1))
    return t * cc + partner * ss


def _to_residues(dst, src, dil):
    m = S // dil
    if dil == 1:
        dst[...] = src[...]
        return
    for r in range(dil):
        dst[r * m:(r + 1) * m, :] = src[pl.ds(r, m, stride=dil), :]


def _from_residues(dst, src, dil):
    m = S // dil
    if dil == 1:
        dst[...] = src[...]
        return
    for r in range(dil):
        dst[pl.ds(r, m, stride=dil), :] = src[r * m:(r + 1) * m, :]


def _attn_masks():
    i = lax.broadcasted_iota(jnp.int32, (128, 128), 0)
    j = lax.broadcasted_iota(jnp.int32, (128, 128), 1)
    return j >= i, j <= i


def _wide(col, width=64):
    return jnp.broadcast_to(col, (col.shape[0], width))


def _attn_zspecs(extra):
    def zcol(off):
        return pl.BlockSpec((S, LANES), lambda p: (0, off + p))

    specs = [zcol(32 + 12 * j + 4 * g) for j in range(3) for g in range(3)] + [zcol(68)]
    return specs + [pl.BlockSpec((S, LANES), lambda p: (0, 0))] * 2 + [pl.BlockSpec((S, LANES), lambda p: (0, p))] * extra


def _attn_fwd(z, cc, ss):
    def body(q0, q1, q2, k0, k1, k2, v0, v1, v2, ag_ref, cc_ref, ss_ref, ob_ref, lse_ref, obg_ref,
             tmp, qs, ks, vs, og, mg, lg, o_t, m_t, l_t):
        first_half = _half_mask()
        prev_ok, cur_ok = _attn_masks()
        for g, (q_ref, k_ref, v_ref) in enumerate(((q0, k0, v0), (q1, k1, v1), (q2, k2, v2))):
            dil = ATT_DILS[g]
            nblk = 16 // dil
            tmp[...] = _rope(q_ref[...], cc_ref[...], ss_ref[...], first_half) * ATT_SCALE
            _to_residues(qs, tmp, dil)
            tmp[...] = _rope(k_ref[...], cc_ref[...], ss_ref[...], first_half)
            _to_residues(ks, tmp, dil)
            _to_residues(vs, v_ref, dil)

            def unit(u, carry):
                start = pl.multiple_of(u * 128, 128)
                qu = qs[pl.ds(start, 128), :].astype(BF16)
                kc = ks[pl.ds(start, 128), :].astype(BF16)
                vc = vs[pl.ds(start, 128), :].astype(BF16)
                if nblk > 1:
                    has_prev = (u % nblk) > 0
                    pstart = pl.multiple_of(jnp.where(has_prev, u - 1, u) * 128, 128)
                    kp = ks[pl.ds(pstart, 128), :].astype(BF16)
                    vp = vs[pl.ds(pstart, 128), :].astype(BF16)
                    pmask = prev_ok & has_prev
                o_h, m_h, l_h = [], [], []
                for hh in range(2):
                    hs = slice(64 * hh, 64 * hh + 64)
                    sc = jnp.where(cur_ok, _dot_nt(qu[:, hs], kc[:, hs]), -jnp.inf)
                    m = jnp.max(sc, axis=-1, keepdims=True)
                    if nblk > 1:
                        sp = jnp.where(pmask, _dot_nt(qu[:, hs], kp[:, hs]), -jnp.inf)
                        m = jnp.maximum(m, jnp.max(sp, axis=-1, keepdims=True))
                        pp = jnp.exp(sp - m)
                    pc = jnp.exp(sc - m)
                    l = jnp.sum(pc, axis=-1, keepdims=True)
                    o = _dot(pc.astype(BF16), vc[:, hs])
                    if nblk > 1:
                        l = l + jnp.sum(pp, axis=-1, keepdims=True)
                        o = o + _dot(pp.astype(BF16), vp[:, hs])
                    o_h.append(o)
                    m_h.append(_wide(m))
                    l_h.append(_wide(l))
                og[pl.ds(start, 128), :] = jnp.concatenate(o_h, axis=1)
                mg[pl.ds(start, 128), :] = jnp.concatenate(m_h, axis=1)
                lg[pl.ds(start, 128), :] = jnp.concatenate(l_h, axis=1)
                return carry

            lax.fori_loop(0, 16, unit, 0)
            _from_residues(o_t.at[g], og, dil)
            _from_residues(m_t.at[g], mg, dil)
            _from_residues(l_t.at[g], lg, dil)

        m = jnp.maximum(jnp.maximum(m_t[0], m_t[1]), m_t[2])
        w0, w1, w2 = jnp.exp(m_t[0] - m), jnp.exp(m_t[1] - m), jnp.exp(m_t[2] - m)
        den = w0 * l_t[0] + w1 * l_t[1] + w2 * l_t[2]
        num = w0 * o_t[0] + w1 * o_t[1] + w2 * o_t[2]
        ob = num / den
        ob_ref[...] = ob
        lse_ref[...] = m + jnp.log(den)
        obg_ref[...] = (ob * _silu(ag_ref[...])).astype(BF16)

    blk = pl.BlockSpec((S, LANES), lambda p: (0, p))
    buf = pltpu.VMEM((S, LANES), F32)
    buf3 = pltpu.VMEM((3, S, LANES), F32)
    return pl.pallas_call(
        body, name="attn_fwd", grid=(4,),
        out_shape=[jax.ShapeDtypeStruct((S, 512), F32), jax.ShapeDtypeStruct((S, 512), F32),
                   jax.ShapeDtypeStruct((S, 512), BF16)],
        in_specs=_attn_zspecs(0), out_specs=[blk, blk, blk],
        scratch_shapes=[buf] * 7 + [buf3] * 3,
        compiler_params=_cp(("parallel",)),
    )(*([z] * 10), cc, ss)


def _tail(x, o_a, o_bg, z, target, w_a, w_b, w_out, fnw):
    tm = 256

    def body(x_ref, oa_ref, ob_ref, gpa_ref, gpb_ref, t_ref, wa_ref, wb_ref, wo_ref, fnw_ref,
             dx2_ref, dx2b_ref, dgp_ref, doa_ref, dob_ref, mg_ref, dya_ref, dyb_ref, small_ref):
        @pl.when(pl.program_id(0) == 0)
        def _():
            small_ref[...] = jnp.zeros_like(small_ref)

        wa, wb, wo = wa_ref[...], wb_ref[...], wo_ref[...]
        y_a = _dot(oa_ref[...], wa)
        y_b = _dot(ob_ref[...], wb)
        ga = jax.nn.sigmoid(gpa_ref[...])
        gb = jax.nn.sigmoid(gpb_ref[...])
        merged = (ga * y_a + gb * y_b).astype(BF16)
        x2 = x_ref[...] + _dot(merged, wo)
        r2 = lax.rsqrt(jnp.mean(x2 * x2, axis=-1, keepdims=True) + EPS)
        n2 = x2 * r2
        fw = fnw_ref[...]
        err = n2 * fw - t_ref[...]
        loss = 0.5 * jnp.sum(jnp.sum(err * err, axis=-1, keepdims=True), axis=0, keepdims=True) / D
        dy = err * (1.0 / D)
        g_fnw = jnp.sum(dy * n2, axis=0, keepdims=True)
        dn = dy * fw
        dx2 = r2 * (dn - n2 * jnp.mean(dn * n2, axis=-1, keepdims=True))
        dx2b = dx2.astype(BF16)
        dmerged = _dot_nt(dx2b, wo)
        dy_a = (dmerged * ga).astype(BF16)
        dy_b = (dmerged * gb).astype(BF16)
        dx2_ref[...] = dx2
        dx2b_ref[...] = dx2b
        dgp_ref[:, 0:D] = (dmerged * y_a * ga * (1.0 - ga)).astype(BF16)
        dgp_ref[:, D:2 * D] = (dmerged * y_b * gb * (1.0 - gb)).astype(BF16)
        doa_ref[...] = _dot_nt(dy_a, wa)
        dob_ref[...] = _dot_nt(dy_b, wb)
        mg_ref[...] = merged
        dya_ref[...] = dy_a
        dyb_ref[...] = dy_b
        small_ref[0:1, :] += g_fnw
        small_ref[1:2, :] += jnp.broadcast_to(loss, (1, D))

    def rows(cols, off=0):
        return pl.BlockSpec((tm, cols), lambda i: (i, off))

    def whole(shape):
        return pl.BlockSpec(shape, lambda i: (0, 0))

    return pl.pallas_call(
        body, name="tail", grid=(S // tm,),
        out_shape=[jax.ShapeDtypeStruct((S, D), F32), jax.ShapeDtypeStruct((S, D), BF16),
                   jax.ShapeDtypeStruct((S, 2 * D), BF16), jax.ShapeDtypeStruct((S, D), F32),
                   jax.ShapeDtypeStruct((S, 512), F32), jax.ShapeDtypeStruct((S, D), BF16),
                   jax.ShapeDtypeStruct((S, D), BF16), jax.ShapeDtypeStruct((S, D), BF16),
                   jax.ShapeDtypeStruct((8, D), F32)],
        in_specs=[rows(D), rows(D), rows(512), rows(D, 9), rows(D, 10), rows(D),
                  whole((D, D)), whole((512, D)), whole((D, D)), whole((1, D))],
        out_specs=[rows(D), rows(D), rows(2 * D), rows(D), rows(512), rows(D), rows(D), rows(D), whole((8, D))],
        compiler_params=_cp(("arbitrary",)),
    )(x, o_a, o_bg, z, z, target, w_a, w_b, w_out, fnw)


def _tn_matmul(a, b, name):
    m, n = a.shape[1], b.shape[1]
    tn = 512

    def body(a_ref, b_ref, o_ref):
        o_ref[...] = _dot_tn(a_ref[...], b_ref[...])

    return pl.pallas_call(
        body, name=name, out_shape=jax.ShapeDtypeStruct((m, n), F32), grid=(n // tn,),
        in_specs=[pl.BlockSpec((S, m), lambda j: (0, 0)), pl.BlockSpec((S, tn), lambda j: (0, j))],
        out_specs=pl.BlockSpec((m, tn), lambda j: (0, j)),
        compiler_params=_cp(("parallel",)),
    )(a, b)


def _hgrn_bwd(z, o, do_a, states, lbv, hnw):
    ntb, nch = S // TBLK, TBLK // CHUNK

    def body(hq_ref, hf_ref, hi_ref, hg_ref, o_ref, doa_ref, st_ref, lb_ref, hnw_ref,
             dhq_ref, dhf_ref, dhi_ref, dhg_ref, glb_ref, ghn_ref, dstate):
        @pl.when(pl.program_id(1) == 0)
        def _():
            dstate[...] = jnp.zeros_like(dstate)
            glb_ref[...] = jnp.zeros_like(glb_ref)
            ghn_ref[...] = jnp.zeros_like(ghn_ref)

        lb = lb_ref[...]
        hq, hg = hq_ref[...], hg_ref[...]
        q, k, g, v, sg, f, b = _hgrn_inputs(hq, hf_ref[...], hi_ref[...], lb)
        bex = b - g
        eb = jnp.exp(b)
        ov, doa, w = o_ref[...], doa_ref[...], hnw_ref[...]
        r = lax.rsqrt(jnp.mean(ov * ov, axis=-1, keepdims=True) + EPS)
        n = ov * r
        sil = _silu(hg)
        dhg_ref[...] = (doa * n * w * _dsilu(hg)).astype(BF16)
        ghn_ref[0] += jnp.sum(doa * sil * n, axis=0, keepdims=True)
        dn = doa * sil * w
        do = r * (dn - n * jnp.mean(dn * n, axis=-1, keepdims=True))

        mask = _tril_mask(CHUNK)
        upper = _block_tri(CHUNK, CHUNK, upper=True)
        dst = dstate[...]
        dq_l, dk_l, dv_l, dg_l = [None] * nch, [None] * nch, [None] * nch, [None] * nch
        for c in reversed(range(nch)):
            r0 = c * CHUNK
            rows = slice(r0, r0 + CHUNK)
            st = st_ref[0, c]
            bc, kc, qc = b[rows], k[rows], q[rows]
            vb, dob = v[rows].astype(BF16), do[rows].astype(BF16)
            b_last = bc[CHUNK - 1:CHUNK]
            e_last = jnp.exp(b_last)
            ekl = jnp.exp(b_last - bc)
            dstb = dst.astype(BF16)
            a, qs_l, ek_l, eq_l = _chunk_scores(q, k, b, bex, r0, mask)
            da = jnp.where(mask, _dot_nt(dob, vb), 0.0)
            dv_l[c] = _dot_tn(a.astype(BF16), dob) + _dot_nt((kc * ekl).astype(BF16), dstb)
            dq_inter = _dot(dob, st.astype(BF16)) * eb[rows]
            dk_state = _dot(vb, dstb) * ekl
            dq_parts, dk_intra = [], jnp.zeros((CHUNK, 128), F32)
            for i in range(CHUNK // SUB):
                da_i = da[SUB * i:SUB * (i + 1)]
                dq_parts.append(_dot3(_dot, da_i, kc * ek_l[i]) * eq_l[i])
                dk_intra = dk_intra + _dot3(_dot_tn, da_i, qs_l[i]) * ek_l[i]
            dq = jnp.concatenate(dq_parts, axis=0) + dq_inter
            dk = dk_intra + dk_state
            last = e_last * jnp.sum(st * dst, axis=0, keepdims=True) + jnp.sum(kc * dk_state, axis=0, keepdims=True)
            dg_l[c] = _dot_ones(upper, qc * dq - kc * dk) + last
            dq_l[c], dk_l[c] = dq, dk
            dst = dst * e_last + _dot_tn(dob, (qc * eb[rows]).astype(BF16))
        dstate[...] = dst
        dq, dk = jnp.concatenate(dq_l, axis=0), jnp.concatenate(dk_l, axis=0)
        dg, dv = jnp.concatenate(dg_l, axis=0), jnp.concatenate(dv_l, axis=0)
        dhq_ref[...] = (dq * _dsilu(hq)).astype(BF16)
        dhi_ref[...] = dv.astype(BF16)
        df = dg / f - dk
        dhf_ref[...] = (df * (1.0 - lb) * sg * (1.0 - sg)).astype(BF16)
        glb_ref[...] += jnp.sum(df * (1.0 - sg), axis=0, keepdims=True)

    def rev(t):
        return ntb - 1 - t

    def zcol(off):
        return pl.BlockSpec((TBLK, 128), lambda h, t: (rev(t), off + h))

    blk = pl.BlockSpec((TBLK, 128), lambda h, t: (rev(t), h))
    vec = pl.BlockSpec((1, 128), lambda h, t: (0, h))
    return pl.pallas_call(
        body, name="hgrn_bwd", grid=(HEADS, ntb),
        out_shape=[jax.ShapeDtypeStruct((S, D), BF16)] * 4
        + [jax.ShapeDtypeStruct((1, D), F32), jax.ShapeDtypeStruct((HEADS, 1, 128), F32)],
        in_specs=[zcol(0), zcol(8), zcol(16), zcol(24), blk, blk,
                  pl.BlockSpec((1, nch, 128, 128), lambda h, t: (h, rev(t), 0, 0)),
                  vec, pl.BlockSpec((1, 128), lambda h, t: (0, 0))],
        out_specs=[blk] * 4 + [vec, pl.BlockSpec((1, 1, 128), lambda h, t: (h, 0, 0))],
        scratch_shapes=[pltpu.VMEM((128, 128), F32)],
        compiler_params=_cp(("parallel", "arbitrary")),
    )(z, z, z, z, o, do_a, states, lbv, hnw)


def _attn_bwd(z, cc, ss, ob, lse, do_bg):
    def body(q0, q1, q2, k0, k1, k2, v0, v1, v2, ag_ref, cc_ref, ss_ref, ob_ref, lse_ref, dobg_ref,
             dq0, dq1, dq2, dk0, dk1, dk2, dv0, dv1, dv2, dag_ref,
             tmp, qs, ks, vs, dos, lses, dls, dqs, dks, dvs, do_t, dl_t):
        first_half = _half_mask()
        prev_ok, cur_ok = _attn_masks()
        ag, obv, dobg = ag_ref[...], ob_ref[...], dobg_ref[...]
        dag_ref[...] = (dobg * obv * _dsilu(ag)).astype(BF16)
        dob = dobg * _silu(ag)
        do_t[...] = dob
        prod = dob * obv
        dl_t[...] = jnp.concatenate(
            [_wide(jnp.sum(prod[:, 0:64], axis=-1, keepdims=True)),
             _wide(jnp.sum(prod[:, 64:128], axis=-1, keepdims=True))], axis=1)
        cc_v, ss_v = cc_ref[...], ss_ref[...]
        groups = ((q0, k0, v0, dq0, dk0, dv0), (q1, k1, v1, dq1, dk1, dv1), (q2, k2, v2, dq2, dk2, dv2))
        for g, (q_ref, k_ref, v_ref, dq_ref, dk_ref, dv_ref) in enumerate(groups):
            dil = ATT_DILS[g]
            nblk = 16 // dil
            tmp[...] = _rope(q_ref[...], cc_v, ss_v, first_half) * ATT_SCALE
            _to_residues(qs, tmp, dil)
            tmp[...] = _rope(k_ref[...], cc_v, ss_v, first_half)
            _to_residues(ks, tmp, dil)
            _to_residues(vs, v_ref, dil)
            _to_residues(dos, do_t, dil)
            _to_residues(lses, lse_ref, dil)
            _to_residues(dls, dl_t, dil)
            dks[...] = jnp.zeros_like(dks)
            dvs[...] = jnp.zeros_like(dvs)

            def unit(u, carry):
                start = pl.multiple_of(u * 128, 128)
                cur = pl.ds(start, 128)
                qu = qs[cur, :].astype(BF16)
                kc = ks[cur, :].astype(BF16)
                vc = vs[cur, :].astype(BF16)
                dou = dos[cur, :].astype(BF16)
                lse_u, dl_u = lses[cur, :], dls[cur, :]
                if nblk > 1:
                    has_prev = (u % nblk) > 0
                    prev = pl.ds(pl.multiple_of(jnp.where(has_prev, u - 1, u) * 128, 128), 128)
                    kp = ks[prev, :].astype(BF16)
                    vp = vs[prev, :].astype(BF16)
                    pmask = prev_ok & has_prev
                dq_h, dkc_h, dvc_h, dkp_h, dvp_h = [], [], [], [], []
                for hh in range(2):
                    hs = slice(64 * hh, 64 * hh + 64)
                    lse_c, dl_c = lse_u[:, 64 * hh:64 * hh + 1], dl_u[:, 64 * hh:64 * hh + 1]
                    pc = jnp.where(cur_ok, jnp.exp(_dot_nt(qu[:, hs], kc[:, hs]) - lse_c), 0.0)
                    dsc = (pc * (_dot_nt(dou[:, hs], vc[:, hs]) - dl_c)).astype(BF16)
                    dq = _dot(dsc, kc[:, hs])
                    dkc_h.append(_dot_tn(dsc, qu[:, hs]))
                    dvc_h.append(_dot_tn(pc.astype(BF16), dou[:, hs]))
                    if nblk > 1:
                        pp = jnp.where(pmask, jnp.exp(_dot_nt(qu[:, hs], kp[:, hs]) - lse_c), 0.0)
                        dsp = (pp * (_dot_nt(dou[:, hs], vp[:, hs]) - dl_c)).astype(BF16)
                        dq = dq + _dot(dsp, kp[:, hs])
                        dkp_h.append(_dot_tn(dsp, qu[:, hs]))
                        dvp_h.append(_dot_tn(pp.astype(BF16), dou[:, hs]))
                    dq_h.append(dq)
                dqs[cur, :] = jnp.concatenate(dq_h, axis=1)
                if nblk > 1:
                    dks[prev, :] += jnp.concatenate(dkp_h, axis=1)
                    dvs[prev, :] += jnp.concatenate(dvp_h, axis=1)
                dks[cur, :] += jnp.concatenate(dkc_h, axis=1)
                dvs[cur, :] += jnp.concatenate(dvc_h, axis=1)
                return carry

            lax.fori_loop(0, 16, unit, 0)
            _from_residues(tmp, dqs, dil)
            dq_ref[...] = (_rope(tmp[...], cc_v, -ss_v, first_half) * ATT_SCALE).astype(BF16)
            _from_residues(tmp, dks, dil)
            dk_ref[...] = _rope(tmp[...], cc_v, -ss_v, first_half).astype(BF16)
            _from_residues(tmp, dvs, dil)
            dv_ref[...] = tmp[...].astype(BF16)

    blk = pl.BlockSpec((S, LANES), lambda p: (0, p))
    buf = pltpu.VMEM((S, LANES), F32)
    return pl.pallas_call(
        body, name="attn_bwd", grid=(4,),
        out_shape=[jax.ShapeDtypeStruct((S, 512), BF16)] * 10,
        in_specs=_attn_zspecs(3), out_specs=[blk] * 10,
        scratch_shapes=[buf] * 12,
        compiler_params=_cp(("parallel",)),
    )(*([z] * 10), cc, ss, ob, lse, do_bg)


def _in_proj_bwd(dz, h, w_in):
    tk = 512
    nk = IN_COLS // tk

    def body(dz_ref, h_ref, w_ref, dh_ref, gw_ref):
        @pl.when(pl.program_id(0) == 0)
        def _():
            dh_ref[...] = jnp.zeros_like(dh_ref)

        dzv = dz_ref[...]
        gw_ref[...] = _dot_tn(h_ref[...], dzv)
        dh_ref[...] += _dot_nt(dzv, w_ref[...])

    return pl.pallas_call(
        body, name="in_proj_bwd", grid=(nk,),
        out_shape=[jax.ShapeDtypeStruct((S, D), F32), jax.ShapeDtypeStruct((D, IN_COLS), F32)],
        in_specs=[pl.BlockSpec((S, tk), lambda j: (0, j)), pl.BlockSpec((S, D), lambda j: (0, 0)),
                  pl.BlockSpec((D, tk), lambda j: (0, j))],
        out_specs=[pl.BlockSpec((S, D), lambda j: (0, 0)), pl.BlockSpec((D, tk), lambda j: (0, j))],
        compiler_params=_cp(("arbitrary",)),
    )(dz, h, w_in)


def _grad_x(x, norm_w, dh, dx2):
    tr = 256

    def body(x_ref, w_ref, dh_ref, dx2_ref, gx_ref, gnw_ref):
        @pl.when(pl.program_id(0) == 0)
        def _():
            gnw_ref[...] = jnp.zeros_like(gnw_ref)

        xv, dhv = x_ref[...], dh_ref[...]
        r = lax.rsqrt(jnp.mean(xv * xv, axis=-1, keepdims=True) + EPS)
        n = xv * r
        gnw_ref[...] += jnp.sum(dhv * n, axis=0, keepdims=True)
        dn = dhv * w_ref[...]
        gx_ref[...] = dx2_ref[...] + r * (dn - n * jnp.mean(dn * n, axis=-1, keepdims=True))

    row = pl.BlockSpec((tr, D), lambda i: (i, 0))
    vec = pl.BlockSpec((1, D), lambda i: (0, 0))
    return pl.pallas_call(
        body, name="grad_x", grid=(S // tr,),
        out_shape=[jax.ShapeDtypeStruct((S, D), F32), jax.ShapeDtypeStruct((1, D), F32)],
        in_specs=[row, vec, row, row], out_specs=[row, vec],
        compiler_params=_cp(("arbitrary",)),
    )(x, norm_w, dh, dx2)


def _rope_tables(positions):
    inv_freq = 10000.0 ** (-jnp.arange(0, 64, 2, dtype=F32) / 64)
    ang = positions.astype(F32)[:, None] * inv_freq[None, :]
    cos, sin = jnp.cos(ang), jnp.sin(ang)
    return jnp.tile(cos, (1, 4)), jnp.tile(jnp.concatenate([-sin, sin], axis=1), (1, 2))


def _local_step(x, positions, norm_w, lb_logits, hnw, fnw, target, w_in, w_a, w_b, w_out):
    cc, ss = _rope_tables(positions)
    lbv = jax.nn.sigmoid(lb_logits[0:1] - lb_logits[1:2])
    h = _rmsnorm_in(x, norm_w)
    z = _in_proj(h, w_in)
    o, o_a, states = _hgrn_fwd(z, lbv, hnw)
    ob, lse, o_bg = _attn_fwd(z, cc, ss)
    dx2, dx2b, dgp, do_a, do_bg, merged, dy_a, dy_b, tail_small = _tail(x, o_a, o_bg, z, target, w_a, w_b, w_out, fnw)
    g_out = _tn_matmul(merged, dx2b, "grad_w_out")
    g_a = _tn_matmul(o_a, dy_a, "grad_w_a")
    g_b = _tn_matmul(o_bg, dy_b, "grad_w_b")
    dhq, dhf, dhi, dhg, glb, ghn = _hgrn_bwd(z, o, do_a, states, lbv, hnw)
    datt = _attn_bwd(z, cc, ss, ob, lse, do_bg)
    dz = jnp.concatenate([dhq, dhf, dhi, dhg, *datt, dgp], axis=1)
    dh, g_in = _in_proj_bwd(dz, h, w_in)
    grad_x, gnw = _grad_x(x, norm_w, dh, dx2)
    ghn_row = jnp.pad(jnp.sum(ghn, axis=0), ((0, 0), (0, D - 128)))
    small = jnp.concatenate([gnw, glb, ghn_row, tail_small[0:2], jnp.zeros((3, D), F32)], axis=0)
    return grad_x, (g_in, g_a, g_b, g_out), small


def kernel(x, positions, norm_w, w_in, lb_logits, hgrn_norm_w, w_branch_a, w_branch_b, w_out, final_norm_w, loss_target, m_norm_w, m_w_in, m_lb_logits, m_hgrn_norm_w, m_w_branch_a, m_w_branch_b, m_w_out, m_final_norm_w, v_norm_w, v_w_in, v_lb_logits, v_hgrn_norm_w, v_w_branch_a, v_w_branch_b, v_w_out, v_final_norm_w):
    ix, iy, ic = _mesh_pos()
    core = jnp.reshape(ic, (1,)).astype(jnp.int32)
    pos = jnp.stack([4 * ix + 2 * iy + ic, 2 * ix + iy]).astype(jnp.int32)

    shards = [w_in[0], w_branch_a[0], w_branch_b[0], w_out[0]]
    moments_m = [m_w_in[0], m_w_branch_a[0], m_w_branch_b[0], m_w_out[0]]
    moments_v = [v_w_in[0], v_w_branch_a[0], v_w_branch_b[0], v_w_out[0]]
    names = ("w_in", "w_a", "w_b", "w_out")
    full = _allgather_weights([_cast_bf16(w, f"cast_{nm}") for w, nm in zip(shards, names)])

    fnw2 = final_norm_w.reshape(1, D)
    grad_x, grads, small = _local_step(x[0], positions[0], norm_w, lb_logits, hgrn_norm_w, fnw2,
                                       loss_target[0], *full)

    gb = [_cast_bf16(g, f"cast_grad_{nm}") for g, nm in zip(grads, names)]
    r1 = _exchange_sibling(gb)
    pb = [_chip_partials(a, grads[a], r1[a], core) for a in range(4)]
    *r2, gathered = _exchange_chips(pb, small)
    big = [_reduce_and_update(a, shards[a], moments_m[a], moments_v[a], grads[a], r1[a], r2[a], pos)
           for a in range(4)]
    sm = _small_update(gathered, norm_w, lb_logits, hgrn_norm_w, fnw2,
                       (m_norm_w, m_lb_logits, m_hgrn_norm_w, m_final_norm_w.reshape(1, D),
                        v_norm_w, v_lb_logits, v_hgrn_norm_w, v_final_norm_w.reshape(1, D)))
    loss = sm[0][0, 0]
    outs = [loss, grad_x[None]]
    for kind in range(4):
        s_nw, s_lb, s_hn, s_fn = sm[1 + 4 * kind:5 + 4 * kind]
        outs += [s_nw, big[0][kind][None], s_lb, s_hn, big[1][kind][None], big[2][kind][None],
                 big[3][kind][None], s_fn.reshape(D)]
    return tuple(outs)
```

```python
import functools

import jax
import jax.numpy as jnp
from jax import lax
from jax.experimental import pallas as pl
from jax.experimental.pallas import tpu as pltpu

F32 = jnp.float32
BF16 = jnp.bfloat16
MESH = pl.DeviceIdType.MESH

S = 2048
D = 1024
NDEV = 8
HEADS = 8
CHUNK = 64
SUB = 16
HBLK = 128
ATT_PAD = 128
ATT_UNROLL = 2
EXP_CLAMP = 80.0
EPS = 1e-6
IN_COLS = 11264
SHARD_COLS = IN_COLS // NDEV
ATT_DILS = (1, 4, 16)
ATT_SCALE = 64 ** -0.5
LANES = 128

ADAM_LR, ADAM_B1, ADAM_B2, ADAM_EPS, ADAM_WD, ADAM_STEP = 0.001, 0.9, 0.999, 1e-08, 0.01, 10

VMEM_LIMIT = 56 * 1024 * 1024


def _cp(sem=None, **kw):
    return pltpu.CompilerParams(dimension_semantics=sem, vmem_limit_bytes=VMEM_LIMIT, **kw)


def _dot(a, b):
    return jnp.dot(a, b, preferred_element_type=F32)


def _dot_nt(a, b):
    return lax.dot_general(a, b, (((1,), (1,)), ((), ())), preferred_element_type=F32)


def _dot_tn(a, b):
    return lax.dot_general(a, b, (((0,), (0,)), ((), ())), preferred_element_type=F32)


def _split2(x):
    hi = x.astype(BF16)
    lo = (x - hi.astype(F32)).astype(BF16)
    return hi, lo


def _split3(x):
    hi = x.astype(BF16)
    r = x - hi.astype(F32)
    mid = r.astype(BF16)
    lo = (r - mid.astype(F32)).astype(BF16)
    return hi, mid, lo


def _dot_ones(ones_bf16, x):
    hi, mid, lo = _split3(x)
    return _dot(ones_bf16, hi) + _dot(ones_bf16, mid) + _dot(ones_bf16, lo)


def _dot3(dotfn, a, b):
    ah, al = _split2(a)
    bh, bl = _split2(b)
    return dotfn(ah, bh) + dotfn(ah, bl) + dotfn(al, bh)


def _silu(x):
    return x * jax.nn.sigmoid(x)


def _dsilu(x):
    s = jax.nn.sigmoid(x)
    return s * (1.0 + x * (1.0 - s))


def _mesh_pos():
    return lax.axis_index("x"), lax.axis_index("y"), lax.axis_index("c")


def _shard_of(ref, a, d):
    if a == 0:
        return ref.at[:, pl.ds(pl.multiple_of(d * SHARD_COLS, LANES), SHARD_COLS)]
    if a == 2:
        return ref.at[:, pl.ds(pl.multiple_of(d * LANES, LANES), LANES)]
    return ref.at[pl.ds(pl.multiple_of(d * 128, 128), 128), :]


FULL_SHAPES = ((D, IN_COLS), (D, D), (512, D), (D, D))
SHARD_SHAPES = ((D, SHARD_COLS), (128, D), (512, 128), (128, D))


def _allgather_weights(shards):
    n = len(shards)

    def body(*refs):
        ins, outs = refs[:n], refs[n:2 * n]
        send_sems, recv_sems, local_sems = refs[2 * n:]
        x, y, c = _mesh_pos()
        me, sibling = (x, y, c), (x, y, 1 - c)
        chips = [(1 - x, y), (x, 1 - y), (1 - x, 1 - y)]

        def blk(a, p):
            return _shard_of(outs[a], a, 4 * p[0] + 2 * p[1] + p[2])

        def copy(a, k, block, to, src=None):
            return pltpu.make_async_remote_copy(
                src_ref=blk(a, block) if src is None else src, dst_ref=blk(a, block),
                send_sem=send_sems.at[a * 7 + k], recv_sem=recv_sems.at[a * 7 + k],
                device_id=to, device_id_type=MESH)

        mine = [pltpu.make_async_copy(ins[a], blk(a, me), local_sems.at[a]) for a in range(n)]
        for cp in mine:
            cp.start()
        first = []
        for a in range(n):
            first += [copy(a, 1 + j, me, (*chip, c), src=ins[a]) for j, chip in enumerate(chips)]
        for a in range(n):
            first.append(copy(a, 0, me, sibling, src=ins[a]))
        for cp in first:
            cp.start()
        passed = []
        for j, chip in enumerate(chips):
            for a in range(n):
                copy(a, 1 + j, (*chip, c), me).wait_recv()
                fwd = copy(a, 4 + j, (*chip, c), sibling)
                fwd.start()
                passed.append(fwd)
        for a in range(n):
            copy(a, 0, sibling, me).wait_recv()
        for j, chip in enumerate(chips):
            for a in range(n):
                copy(a, 4 + j, (*chip, 1 - c), me).wait_recv()
        for cp in first + passed:
            cp.wait_send()
        for cp in mine:
            cp.wait()

    any_spec = pl.BlockSpec(memory_space=pl.ANY)
    return pl.pallas_call(
        body, name="allgather_weights",
        out_shape=[jax.ShapeDtypeStruct(FULL_SHAPES[a], BF16) for a in range(n)],
        in_specs=[any_spec] * n, out_specs=[any_spec] * n,
        scratch_shapes=[pltpu.SemaphoreType.DMA((7 * n,)), pltpu.SemaphoreType.DMA((7 * n,)),
                        pltpu.SemaphoreType.DMA((n,))],
    )(*shards)


def _exchange_sibling(gb):
    n = len(gb)

    def body(*refs):
        ins, outs = refs[:n], refs[n:2 * n]
        send_sems, recv_sems = refs[2 * n:]
        x, y, c = _mesh_pos()
        sibling = (x, y, 1 - c)
        copies = []
        for a in range(n):
            for q in range(4):
                copies.append(pltpu.make_async_remote_copy(
                    src_ref=_shard_of(ins[a], a, 2 * q + (1 - c)), dst_ref=outs[a].at[q],
                    send_sem=send_sems.at[a * 4 + q], recv_sem=recv_sems.at[a * 4 + q],
                    device_id=sibling, device_id_type=MESH))
        for cp in copies:
            cp.start()
        for cp in copies:
            cp.wait()

    any_spec = pl.BlockSpec(memory_space=pl.ANY)
    return pl.pallas_call(
        body, name="grads_to_sibling",
        out_shape=[jax.ShapeDtypeStruct((4,) + SHARD_SHAPES[a], BF16) for a in range(n)],
        in_specs=[any_spec] * n, out_specs=[any_spec] * n,
        scratch_shapes=[pltpu.SemaphoreType.DMA((4 * n,)), pltpu.SemaphoreType.DMA((4 * n,))],
    )(*gb)


def _exchange_chips(pb, small):
    n = len(pb)

    def body(*refs):
        ins, small_ref = refs[:n], refs[n]
        outs, small_out = refs[n + 1:2 * n + 1], refs[2 * n + 1]
        send_sems, recv_sems, ssend, srecv, local_sem = refs[2 * n + 2:]
        x, y, c = _mesh_pos()
        chips = [(1 - x, y), (x, 1 - y), (1 - x, 1 - y)]
        me = 4 * x + 2 * y + c
        copies = []
        for a in range(n):
            for k, chip in enumerate(chips):
                copies.append(pltpu.make_async_remote_copy(
                    src_ref=ins[a].at[2 * chip[0] + chip[1]], dst_ref=outs[a].at[k],
                    send_sem=send_sems.at[a * 3 + k], recv_sem=recv_sems.at[a * 3 + k],
                    device_id=(*chip, c), device_id_type=MESH))
        for r in range(1, NDEV):
            peer = (1 - x if r & 4 else x, 1 - y if r & 2 else y, 1 - c if r & 1 else c)
            copies.append(pltpu.make_async_remote_copy(
                src_ref=small_ref, dst_ref=small_out.at[me],
                send_sem=ssend.at[r - 1], recv_sem=srecv.at[r - 1],
                device_id=peer, device_id_type=MESH))
        own = pltpu.make_async_copy(small_ref, small_out.at[me], local_sem)
        own.start()
        for cp in copies:
            cp.start()
        for cp in copies:
            cp.wait()
        own.wait()

    any_spec = pl.BlockSpec(memory_space=pl.ANY)
    return pl.pallas_call(
        body, name="grads_between_chips",
        out_shape=[jax.ShapeDtypeStruct((3,) + SHARD_SHAPES[a], BF16) for a in range(n)]
        + [jax.ShapeDtypeStruct((NDEV,) + small.shape, F32)],
        in_specs=[any_spec] * (n + 1), out_specs=[any_spec] * (n + 1),
        scratch_shapes=[pltpu.SemaphoreType.DMA((3 * n,)), pltpu.SemaphoreType.DMA((3 * n,)),
                        pltpu.SemaphoreType.DMA((NDEV - 1,)), pltpu.SemaphoreType.DMA((NDEV - 1,)),
                        pltpu.SemaphoreType.DMA],
    )(*pb, small)


def _shard_tiles(a):
    rows, cols = SHARD_SHAPES[a]
    tr = min(rows, 256)
    return (tr, cols), rows // tr


def _full_index(a, d, i):
    (tr, _), nt = _shard_tiles(a)
    if a in (0, 2):
        return (i, d)
    return (d * nt + i, 0)


def _cast_bf16(x, name):
    rows, cols = x.shape
    tr = min(rows, 256)

    def body(x_ref, o_ref):
        o_ref[...] = x_ref[...].astype(BF16)

    return pl.pallas_call(
        body, name=name, out_shape=jax.ShapeDtypeStruct(x.shape, BF16), grid=(rows // tr,),
        in_specs=[pl.BlockSpec((tr, cols), lambda i: (i, 0))],
        out_specs=pl.BlockSpec((tr, cols), lambda i: (i, 0)),
        compiler_params=_cp(("parallel",)),
    )(x)


def _chip_partials(a, g_full, r1, core):
    tile, nt = _shard_tiles(a)

    def body(c_ref, g_ref, r_ref, o_ref):
        o_ref[0] = (g_ref[...] + r_ref[0].astype(F32)).astype(BF16)

    grid_spec = pltpu.PrefetchScalarGridSpec(
        num_scalar_prefetch=1, grid=(4, nt),
        in_specs=[pl.BlockSpec(tile, lambda q, i, c: _full_index(a, 2 * q + c[0], i)),
                  pl.BlockSpec((1,) + tile, lambda q, i, c: (q, i, 0))],
        out_specs=pl.BlockSpec((1,) + tile, lambda q, i, c: (q, i, 0)))
    return pl.pallas_call(
        body, name=f"chip_partials_{a}", grid_spec=grid_spec,
        out_shape=jax.ShapeDtypeStruct((4,) + SHARD_SHAPES[a], BF16),
        compiler_params=_cp(("parallel", "parallel")),
    )(core, g_full, r1)


def _adam(w, g, m, v):
    m = ADAM_B1 * m + (1.0 - ADAM_B1) * g
    v = ADAM_B2 * v + (1.0 - ADAM_B2) * (g * g)
    m_hat = m / (1.0 - ADAM_B1 ** ADAM_STEP)
    v_hat = v / (1.0 - ADAM_B2 ** ADAM_STEP)
    delta = -ADAM_LR * (m_hat / (jnp.sqrt(v_hat) + ADAM_EPS) + ADAM_WD * w)
    return delta, m, v


def _reduce_and_update(a, w, m, v, g_full, r1, r2, pos):
    tile, nt = _shard_tiles(a)

    def body(p_ref, w_ref, m_ref, v_ref, g_ref, r1_ref, r2_ref, go_ref, do_ref, mo_ref, vo_ref):
        g = g_ref[...] + r1_ref[0].astype(F32)
        g = g + r2_ref[0].astype(F32)
        g = g + r2_ref[1].astype(F32)
        g = g + r2_ref[2].astype(F32)
        delta, m_new, v_new = _adam(w_ref[...], g, m_ref[...], v_ref[...])
        go_ref[...] = g
        do_ref[...] = delta
        mo_ref[...] = m_new
        vo_ref[...] = v_new

    own = pl.BlockSpec(tile, lambda i, p: (i, 0))
    grid_spec = pltpu.PrefetchScalarGridSpec(
        num_scalar_prefetch=1, grid=(nt,),
        in_specs=[own, own, own,
                  pl.BlockSpec(tile, lambda i, p: _full_index(a, p[0], i)),
                  pl.BlockSpec((1,) + tile, lambda i, p: (p[1], i, 0)),
                  pl.BlockSpec((3,) + tile, lambda i, p: (0, i, 0))],
        out_specs=[own] * 4)
    shp = jax.ShapeDtypeStruct(w.shape, F32)
    return pl.pallas_call(
        body, name=f"reduce_update_{a}", grid_spec=grid_spec, out_shape=[shp] * 4,
        compiler_params=_cp(("parallel",)),
    )(pos, w, m, v, g_full, r1, r2)


def _small_update(gathered, norm_w, lb_logits, hnw, fnw, moments):
    m_nw, m_lb, m_hn, m_fn, v_nw, v_lb, v_hn, v_fn = moments

    def body(g_ref, nw, lb, hn, fn, mnw, mlb, mhn, mfn, vnw, vlb, vhn, vfn,
             loss_o, g_nw, g_lb, g_hn, g_fn, d_nw, d_lb, d_hn, d_fn,
             mo_nw, mo_lb, mo_hn, mo_fn, vo_nw, vo_lb, vo_hn, vo_fn):
        tot = g_ref[0]
        for d in range(1, NDEV):
            tot = tot + g_ref[d]
        loss_o[...] = tot[4:5, 0:LANES]
        logits = lb[...]
        lbv = jax.nn.sigmoid(logits[0:1] - logits[1:2])
        chain = tot[1:2] * lbv * (1.0 - lbv)
        grads = (tot[0:1], jnp.concatenate([chain, -chain], axis=0), tot[2:3, 0:LANES], tot[3:4])
        outs = ((nw, mnw, vnw, g_nw, d_nw, mo_nw, vo_nw), (lb, mlb, vlb, g_lb, d_lb, mo_lb, vo_lb),
                (hn, mhn, vhn, g_hn, d_hn, mo_hn, vo_hn), (fn, mfn, vfn, g_fn, d_fn, mo_fn, vo_fn))
        for g, (w_r, m_r, v_r, g_o, d_o, m_o, v_o) in zip(grads, outs):
            delta, m_new, v_new = _adam(w_r[...], g, m_r[...], v_r[...])
            g_o[...] = g
            d_o[...] = delta
            m_o[...] = m_new
            v_o[...] = v_new

    shapes = [norm_w.shape, lb_logits.shape, hnw.shape, fnw.shape]
    out_shape = [jax.ShapeDtypeStruct((1, LANES), F32)] + [jax.ShapeDtypeStruct(s, F32) for s in shapes] * 4
    return pl.pallas_call(body, name="small_update", out_shape=out_shape, compiler_params=_cp())(
        gathered, norm_w, lb_logits, hnw, fnw, m_nw, m_lb, m_hn, m_fn, v_nw, v_lb, v_hn, v_fn)


def _rmsnorm_in(x, norm_w):
    tr = 512

    def body(x_ref, w_ref, h_ref):
        xv = x_ref[...]
        r = lax.rsqrt(jnp.mean(xv * xv, axis=-1, keepdims=True) + EPS)
        h_ref[...] = (xv * r * w_ref[...]).astype(BF16)

    return pl.pallas_call(
        body, name="rmsnorm_in", out_shape=jax.ShapeDtypeStruct((S, D), BF16), grid=(S // tr,),
        in_specs=[pl.BlockSpec((tr, D), lambda i: (i, 0)), pl.BlockSpec((1, D), lambda i: (0, 0))],
        out_specs=pl.BlockSpec((tr, D), lambda i: (i, 0)),
        compiler_params=_cp(("parallel",)),
    )(x, norm_w)


def _in_proj(h, w_in):
    tn = 1024

    def body(h_ref, w_ref, z_ref):
        z_ref[...] = _dot(h_ref[...], w_ref[...])

    return pl.pallas_call(
        body, name="in_proj", out_shape=jax.ShapeDtypeStruct((S, IN_COLS), F32), grid=(IN_COLS // tn,),
        in_specs=[pl.BlockSpec((S, D), lambda j: (0, 0)), pl.BlockSpec((D, tn), lambda j: (0, j))],
        out_specs=pl.BlockSpec((S, tn), lambda j: (0, j)),
        compiler_params=_cp(("parallel",)),
    )(h, w_in)


def _block_tri(n, block, upper=False):
    r = lax.broadcasted_iota(jnp.int32, (n, n), 0)
    c = lax.broadcasted_iota(jnp.int32, (n, n), 1)
    keep = (c >= r) if upper else (c <= r)
    return jnp.where(keep & ((r // block) == (c // block)), 1.0, 0.0).astype(BF16)


def _tril_mask(n):
    r = lax.broadcasted_iota(jnp.int32, (n, n), 0)
    c = lax.broadcasted_iota(jnp.int32, (n, n), 1)
    return c <= r


def _chunk_scores(q, k, b, bex, r0, mask):
    rows = slice(r0, r0 + CHUNK)
    parts, qs_l, ek_l, eq_l = [], [], [], []
    for i in range(CHUNK // SUB):
        ri = slice(r0 + SUB * i, r0 + SUB * (i + 1))
        base = bex[r0 + SUB * i:r0 + SUB * i + 1]
        eq = jnp.exp(b[ri] - base)
        ek = jnp.exp(jnp.minimum(base - b[rows], EXP_CLAMP))
        qs = q[ri] * eq
        parts.append(_dot_nt(qs.astype(BF16), (k[rows] * ek).astype(BF16)))
        qs_l.append(qs)
        ek_l.append(ek)
        eq_l.append(eq)
    return jnp.where(mask, jnp.concatenate(parts, axis=0), 0.0), qs_l, ek_l, eq_l


def _hgrn_cols(hq, hf, hi, lb):
    sg = jax.nn.sigmoid(hf)
    f = lb + (1.0 - lb) * sg
    g = jnp.log(f)
    b = _dot_ones(_block_tri(HBLK, CHUNK), g)
    return _silu(hq), 1.0 - f, g, hi, sg, f, b


def _hgrn_fwd(z, lbv, hnw):
    ntb, nch = S // HBLK, HBLK // CHUNK

    def body(hq_ref, hf_ref, hi_ref, hg_ref, lb_ref, hnw_ref, o_ref, oa_ref, st_ref, state):
        @pl.when(pl.program_id(0) == 0)
        def _():
            state[...] = jnp.zeros_like(state)

        q_a, k_a, g_a, v_a, _, _, b_a = _hgrn_cols(hq_ref[...], hf_ref[...], hi_ref[...], lb_ref[...])
        bex_a = b_a - g_a
        eb_a = jnp.exp(b_a)
        mask = _tril_mask(CHUNK)
        hg = hg_ref[...]
        w = hnw_ref[...]
        for h in range(HEADS):
            cols = slice(128 * h, 128 * h + 128)
            q, k, v, b, bex, eb = q_a[:, cols], k_a[:, cols], v_a[:, cols], b_a[:, cols], bex_a[:, cols], eb_a[:, cols]
            st = state[h]
            outs = []
            for c in range(nch):
                r0 = c * CHUNK
                rows = slice(r0, r0 + CHUNK)
                a, _, _, _ = _chunk_scores(q, k, b, bex, r0, mask)
                vb = v[rows].astype(BF16)
                b_last = b[r0 + CHUNK - 1:r0 + CHUNK]
                qe = (q[rows] * eb[rows]).astype(BF16)
                outs.append(_dot(a.astype(BF16), vb) + _dot_nt(qe, st.astype(BF16)))
                st_ref[h, c] = st
                ke = (k[rows] * jnp.exp(b_last - b[rows])).astype(BF16)
                st = st * jnp.exp(b_last) + _dot_tn(vb, ke)
            state[h] = st
            o = jnp.concatenate(outs, axis=0)
            o_ref[:, cols] = o
            r = lax.rsqrt(jnp.mean(o * o, axis=-1, keepdims=True) + EPS)
            oa_ref[:, cols] = (o * r * w * _silu(hg[:, cols])).astype(BF16)

    def zcol(j):
        return pl.BlockSpec((HBLK, D), lambda t: (t, j))

    out_blk = pl.BlockSpec((HBLK, D), lambda t: (t, 0))
    return pl.pallas_call(
        body, name="hgrn_fwd", grid=(ntb,),
        out_shape=[jax.ShapeDtypeStruct((S, D), F32), jax.ShapeDtypeStruct((S, D), BF16),
                   jax.ShapeDtypeStruct((HEADS, S // CHUNK, 128, 128), F32)],
        in_specs=[zcol(0), zcol(1), zcol(2), zcol(3),
                  pl.BlockSpec((1, D), lambda t: (0, 0)), pl.BlockSpec((1, 128), lambda t: (0, 0))],
        out_specs=[out_blk, out_blk, pl.BlockSpec((HEADS, nch, 128, 128), lambda t: (0, t, 0, 0))],
        scratch_shapes=[pltpu.VMEM((HEADS, 128, 128), F32)],
        compiler_params=_cp(("arbitrary",)),
    )(z, z, z, z, lbv, hnw)


def _half_mask():
    lane = lax.broadcasted_iota(jnp.int32, (1, LANES), 1)
    return (lane % 64) < 32


def _rope(t, cc, ss, first_half):
    partner = jnp.where(first_half, pltpu.roll(t, 96, 1), pltpu.roll(t, 32, 1))
    return t * cc + partner * ss


def _attn_masks():
    i = lax.broadcasted_iota(jnp.int32, (128, 128), 0)
    j = lax.broadcasted_iota(jnp.int32, (128, 128), 1)
    return j >= i, j <= i


def _to_residues_dyn(g, dst, src, row0=0, dtype=None):
    for gi, dil in enumerate((1, 4, 16)):
        m = S // dil

        @pl.when(g == gi)
        def _(dil=dil, m=m):
            for r in range(dil):
                v = src[...] if dil == 1 else src[pl.ds(r, m, stride=dil), :]
                if dtype is not None:
                    v = v.astype(dtype)
                dst[row0 + r * m:row0 + (r + 1) * m, 0:LANES] = v


def _from_residues_dyn(g, dst, src, row0=0):
    for gi, dil in enumerate((1, 4, 16)):
        m = S // dil

        @pl.when(g == gi)
        def _(dil=dil, m=m):
            for r in range(dil):
                v = src[row0 + r * m:row0 + (r + 1) * m, :]
                if dil == 1:
                    dst[...] = v
                else:
                    dst[pl.ds(r, m, stride=dil), :] = v


def _group_blocks(g):
    return jnp.where(g == 0, 16, jnp.where(g == 1, 4, 1))


def _attn_in_specs(extra):
    def zcol(off):
        return pl.BlockSpec((S, LANES), lambda p, g: (0, off + 4 * g + p))

    per_pair = pl.BlockSpec((S, LANES), lambda p, g: (0, p))
    const = pl.BlockSpec((S, LANES), lambda p, g: (0, 0))
    return [zcol(32), zcol(44), zcol(56), pl.BlockSpec((S, LANES), lambda p, g: (0, 68 + p)), const, const] + [per_pair] * extra


def _attn_fwd(z, cc, ss):
    def body(q_ref, k_ref, v_ref, ag_ref, cc_ref, ss_ref, ob_ref, lse_ref, obg_ref,
             tmp, qs, ks, vx, og, mg, lg, o_t, m_t, l_t, o_acc, m_acc, l_acc):
        g = pl.program_id(1)
        first_half = _half_mask()
        prev_ok, cur_ok = _attn_masks()
        lane = lax.broadcasted_iota(jnp.int32, (1, LANES), 1)
        heads = (lane < 64, lane >= 64)
        nblk = _group_blocks(g)

        @pl.when(g == 0)
        def _():
            ks[0:ATT_PAD, :] = jnp.zeros((ATT_PAD, LANES), BF16)
            vx[0:ATT_PAD, 0:LANES] = jnp.zeros((ATT_PAD, LANES), BF16)
            vx[:, LANES:2 * LANES] = jnp.ones((ATT_PAD + S, LANES), BF16)

        tmp[...] = _rope(q_ref[...], cc_ref[...], ss_ref[...], first_half) * ATT_SCALE
        _to_residues_dyn(g, qs, tmp)
        tmp[...] = _rope(k_ref[...], cc_ref[...], ss_ref[...], first_half)
        _to_residues_dyn(g, ks, tmp, ATT_PAD, BF16)
        _to_residues_dyn(g, vx, v_ref, ATT_PAD, BF16)

        def unit(u, carry):
            start = pl.multiple_of(u * 128, 128)
            cur = pl.ds(start, 128)
            pm = prev_ok & ((u & (nblk - 1)) != 0)
            qu = qs[cur, :]
            kcat = ks[pl.ds(start, 256), :]
            vext = vx[pl.ds(start, 256), :]
            o_u = m_u = l_u = None
            for hh in range(2):
                s = _dot_nt(jnp.where(heads[hh], qu, 0.0).astype(BF16), kcat)
                sp = jnp.where(pm, s[:, 0:128], -jnp.inf)
                sc = jnp.where(cur_ok, s[:, 128:256], -jnp.inf)
                m = jnp.max(jnp.maximum(sp, sc), axis=-1, keepdims=True)
                p = jnp.concatenate([jnp.exp(sp - m), jnp.exp(sc - m)], axis=1).astype(BF16)
                ol = _dot(p, vext)
                mb = jnp.broadcast_to(m, (128, LANES))
                if hh == 0:
                    o_u, l_u, m_u = ol[:, 0:128], ol[:, 128:256], mb
                else:
                    o_u = jnp.where(heads[1], ol[:, 0:128], o_u)
                    l_u = jnp.where(heads[1], ol[:, 128:256], l_u)
                    m_u = jnp.where(heads[1], mb, m_u)
            og[cur, :] = o_u
            mg[cur, :] = m_u
            lg[cur, :] = l_u
            return carry

        lax.fori_loop(0, 16, unit, 0, unroll=ATT_UNROLL)
        _from_residues_dyn(g, o_t, og)
        _from_residues_dyn(g, m_t, mg)
        _from_residues_dyn(g, l_t, lg)

        @pl.when(g == 0)
        def _():
            o_acc[...] = o_t[...]
            m_acc[...] = m_t[...]
            l_acc[...] = l_t[...]

        @pl.when(g > 0)
        def _():
            m_new = jnp.maximum(m_acc[...], m_t[...])
            wa, wb = jnp.exp(m_acc[...] - m_new), jnp.exp(m_t[...] - m_new)
            o_acc[...] = o_acc[...] * wa + o_t[...] * wb
            l_acc[...] = l_acc[...] * wa + l_t[...] * wb
            m_acc[...] = m_new

        @pl.when(g == 2)
        def _():
            ob = o_acc[...] / l_acc[...]
            ob_ref[...] = ob
            lse_ref[...] = m_acc[...] + jnp.log(l_acc[...])
            obg_ref[...] = (ob * _silu(ag_ref[...])).astype(BF16)

    blk = pl.BlockSpec((S, LANES), lambda p, g: (0, p))
    buf = pltpu.VMEM((S, LANES), F32)
    return pl.pallas_call(
        body, name="attn_fwd", grid=(4, 3),
        out_shape=[jax.ShapeDtypeStruct((S, 512), F32), jax.ShapeDtypeStruct((S, 512), F32),
                   jax.ShapeDtypeStruct((S, 512), BF16)],
        in_specs=_attn_in_specs(0), out_specs=[blk, blk, blk],
        scratch_shapes=[buf, buf, pltpu.VMEM((ATT_PAD + S, LANES), BF16), pltpu.VMEM((ATT_PAD + S, 2 * LANES), BF16)] + [buf] * 9,
        compiler_params=_cp(("parallel", "arbitrary")),
    )(z, z, z, z, cc, ss)


def _tail(x, o_a, o_bg, z, target, w_a, w_b, w_out, fnw):
    tm = 256

    def body(x_ref, oa_ref, ob_ref, gpa_ref, gpb_ref, t_ref, wa_ref, wb_ref, wo_ref, fnw_ref,
             dx2_ref, dx2b_ref, dgp_ref, doa_ref, dob_ref, mg_ref, dya_ref, dyb_ref, small_ref):
        @pl.when(pl.program_id(0) == 0)
        def _():
            small_ref[...] = jnp.zeros_like(small_ref)

        wa, wb, wo = wa_ref[...], wb_ref[...], wo_ref[...]
        y_a = _dot(oa_ref[...], wa)
        y_b = _dot(ob_ref[...], wb)
        ga = jax.nn.sigmoid(gpa_ref[...])
        gb = jax.nn.sigmoid(gpb_ref[...])
        merged = (ga * y_a + gb * y_b).astype(BF16)
        x2 = x_ref[...] + _dot(merged, wo)
        r2 = lax.rsqrt(jnp.mean(x2 * x2, axis=-1, keepdims=True) + EPS)
        n2 = x2 * r2
        fw = fnw_ref[...]
        err = n2 * fw - t_ref[...]
        loss = 0.5 * jnp.sum(jnp.sum(err * err, axis=-1, keepdims=True), axis=0, keepdims=True) / D
        dy = err * (1.0 / D)
        g_fnw = jnp.sum(dy * n2, axis=0, keepdims=True)
        dn = dy * fw
        dx2 = r2 * (dn - n2 * jnp.mean(dn * n2, axis=-1, keepdims=True))
        dx2b = dx2.astype(BF16)
        dmerged = _dot_nt(dx2b, wo)
        dy_a = (dmerged * ga).astype(BF16)
        dy_b = (dmerged * gb).astype(BF16)
        dx2_ref[...] = dx2
        dx2b_ref[...] = dx2b
        dgp_ref[:, 0:D] = (dmerged * y_a * ga * (1.0 - ga)).astype(BF16)
        dgp_ref[:, D:2 * D] = (dmerged * y_b * gb * (1.0 - gb)).astype(BF16)
        doa_ref[...] = _dot_nt(dy_a, wa)
        dob_ref[...] = _dot_nt(dy_b, wb)
        mg_ref[...] = merged
        dya_ref[...] = dy_a
        dyb_ref[...] = dy_b
        small_ref[0:1, :] += g_fnw
        small_ref[1:2, :] += jnp.broadcast_to(loss, (1, D))

    def rows(cols, off=0):
        return pl.BlockSpec((tm, cols), lambda i: (i, off))

    def whole(shape):
        return pl.BlockSpec(shape, lambda i: (0, 0))

    return pl.pallas_call(
        body, name="tail", grid=(S // tm,),
        out_shape=[jax.ShapeDtypeStruct((S, D), F32), jax.ShapeDtypeStruct((S, D), BF16),
                   jax.ShapeDtypeStruct((S, 2 * D), BF16), jax.ShapeDtypeStruct((S, D), F32),
                   jax.ShapeDtypeStruct((S, 512), F32), jax.ShapeDtypeStruct((S, D), BF16),
                   jax.ShapeDtypeStruct((S, D), BF16), jax.ShapeDtypeStruct((S, D), BF16),
                   jax.ShapeDtypeStruct((8, D), F32)],
        in_specs=[rows(D), rows(D), rows(512), rows(D, 9), rows(D, 10), rows(D),
                  whole((D, D)), whole((512, D)), whole((D, D)), whole((1, D))],
        out_specs=[rows(D), rows(D), rows(2 * D), rows(D), rows(512), rows(D), rows(D), rows(D), whole((8, D))],
        compiler_params=_cp(("arbitrary",)),
    )(x, o_a, o_bg, z, z, target, w_a, w_b, w_out, fnw)


def _tn_matmul(a, b, name):
    m, n = a.shape[1], b.shape[1]
    tn = 512

    def body(a_ref, b_ref, o_ref, ob_ref):
        acc = _dot_tn(a_ref[...], b_ref[...])
        o_ref[...] = acc
        ob_ref[...] = acc.astype(BF16)

    out_blk = pl.BlockSpec((m, tn), lambda j: (0, j))
    return pl.pallas_call(
        body, name=name, grid=(n // tn,),
        out_shape=[jax.ShapeDtypeStruct((m, n), F32), jax.ShapeDtypeStruct((m, n), BF16)],
        in_specs=[pl.BlockSpec((S, m), lambda j: (0, 0)), pl.BlockSpec((S, tn), lambda j: (0, j))],
        out_specs=[out_blk, out_blk],
        compiler_params=_cp(("parallel",)),
    )(a, b)


def _hgrn_bwd(z, o, do_a, states, lbv, hnw):
    ntb, nch = S // HBLK, HBLK // CHUNK

    def body(hq_ref, hf_ref, hi_ref, hg_ref, o_ref, doa_ref, st_ref, lb_ref, hnw_ref,
             dhq_ref, dhf_ref, dhi_ref, dhg_ref, glb_ref, ghn_ref, dstate):
        @pl.when(pl.program_id(0) == 0)
        def _():
            dstate[...] = jnp.zeros_like(dstate)
            glb_ref[...] = jnp.zeros_like(glb_ref)
            ghn_ref[...] = jnp.zeros_like(ghn_ref)

        lb_a = lb_ref[...]
        hq_a, hg_a = hq_ref[...], hg_ref[...]
        q_a, k_a, g_a, v_a, sg_a, f_a, b_a = _hgrn_cols(hq_a, hf_ref[...], hi_ref[...], lb_a)
        bex_a = b_a - g_a
        eb_a = jnp.exp(b_a)
        w = hnw_ref[...]
        mask = _tril_mask(CHUNK)
        upper = _block_tri(CHUNK, CHUNK, upper=True)
        for h in range(HEADS):
            cols = slice(128 * h, 128 * h + 128)
            q, k, v, b, bex, eb = q_a[:, cols], k_a[:, cols], v_a[:, cols], b_a[:, cols], bex_a[:, cols], eb_a[:, cols]
            hq, hg, sg, f, lb = hq_a[:, cols], hg_a[:, cols], sg_a[:, cols], f_a[:, cols], lb_a[:, cols]
            ov, doa = o_ref[:, cols], doa_ref[:, cols]
            r = lax.rsqrt(jnp.mean(ov * ov, axis=-1, keepdims=True) + EPS)
            n = ov * r
            sil = _silu(hg)
            dhg_ref[:, cols] = (doa * n * w * _dsilu(hg)).astype(BF16)
            ghn_ref[h] += jnp.sum(doa * sil * n, axis=0, keepdims=True)
            dn = doa * sil * w
            do = r * (dn - n * jnp.mean(dn * n, axis=-1, keepdims=True))

            dst = dstate[h]
            dq_l, dk_l, dv_l, dg_l = [None] * nch, [None] * nch, [None] * nch, [None] * nch
            for c in reversed(range(nch)):
                r0 = c * CHUNK
                rows = slice(r0, r0 + CHUNK)
                st = st_ref[h, c]
                bc, kc, qc = b[rows], k[rows], q[rows]
                vb, dob = v[rows].astype(BF16), do[rows].astype(BF16)
                b_last = bc[CHUNK - 1:CHUNK]
                e_last = jnp.exp(b_last)
                ekl = jnp.exp(b_last - bc)
                dstb = dst.astype(BF16)
                a, qs_l, ek_l, eq_l = _chunk_scores(q, k, b, bex, r0, mask)
                da = jnp.where(mask, _dot_nt(dob, vb), 0.0)
                dv_l[c] = _dot_tn(a.astype(BF16), dob) + _dot_nt((kc * ekl).astype(BF16), dstb)
                dq_inter = _dot(dob, st.astype(BF16)) * eb[rows]
                dk_state = _dot(vb, dstb) * ekl
                dq_parts, dk_intra = [], jnp.zeros((CHUNK, 128), F32)
                for i in range(CHUNK // SUB):
                    da_i = da[SUB * i:SUB * (i + 1)]
                    dq_parts.append(_dot3(_dot, da_i, kc * ek_l[i]) * eq_l[i])
                    dk_intra = dk_intra + _dot3(_dot_tn, da_i, qs_l[i]) * ek_l[i]
                dq = jnp.concatenate(dq_parts, axis=0) + dq_inter
                dk = dk_intra + dk_state
                last = (e_last * jnp.sum(st * dst, axis=0, keepdims=True)
                        + jnp.sum(kc * dk_state, axis=0, keepdims=True))
                dg_l[c] = _dot_ones(upper, qc * dq - kc * dk) + last
                dq_l[c], dk_l[c] = dq, dk
                dst = dst * e_last + _dot_tn(dob, (qc * eb[rows]).astype(BF16))
            dstate[h] = dst
            dq, dk = jnp.concatenate(dq_l, axis=0), jnp.concatenate(dk_l, axis=0)
            dg, dv = jnp.concatenate(dg_l, axis=0), jnp.concatenate(dv_l, axis=0)
            dhq_ref[:, cols] = (dq * _dsilu(hq)).astype(BF16)
            dhi_ref[:, cols] = dv.astype(BF16)
            df = dg / f - dk
            dhf_ref[:, cols] = (df * (1.0 - lb) * sg * (1.0 - sg)).astype(BF16)
            glb_ref[:, cols] += jnp.sum(df * (1.0 - sg), axis=0, keepdims=True)

    def rev(t):
        return ntb - 1 - t

    def zcol(j):
        return pl.BlockSpec((HBLK, D), lambda t: (rev(t), j))

    blk = pl.BlockSpec((HBLK, D), lambda t: (rev(t), 0))
    return pl.pallas_call(
        body, name="hgrn_bwd", grid=(ntb,),
        out_shape=[jax.ShapeDtypeStruct((S, D), BF16)] * 4
        + [jax.ShapeDtypeStruct((1, D), F32), jax.ShapeDtypeStruct((HEADS, 1, 128), F32)],
        in_specs=[zcol(0), zcol(1), zcol(2), zcol(3), blk, blk,
                  pl.BlockSpec((HEADS, nch, 128, 128), lambda t: (0, rev(t), 0, 0)),
                  pl.BlockSpec((1, D), lambda t: (0, 0)), pl.BlockSpec((1, 128), lambda t: (0, 0))],
        out_specs=[blk] * 4 + [pl.BlockSpec((1, D), lambda t: (0, 0)),
                               pl.BlockSpec((HEADS, 1, 128), lambda t: (0, 0, 0))],
        scratch_shapes=[pltpu.VMEM((HEADS, 128, 128), F32)],
        compiler_params=_cp(("arbitrary",)),
    )(z, z, z, z, o, do_a, states, lbv, hnw)


def _attn_bwd(z, cc, ss, ob, lse, do_bg):
    def body(q_ref, k_ref, v_ref, ag_ref, cc_ref, ss_ref, ob_ref, lse_ref, dobg_ref,
             dq_ref, dk_ref, dv_ref, dag_ref,
             tmp, qs, ks, vs, dos, dqs, dks, dvs, do_t, ls0_t, ls1_t, dl0_t, dl1_t, ls0, ls1, dl0, dl1):
        g = pl.program_id(1)
        first_half = _half_mask()
        prev_ok, cur_ok = _attn_masks()
        lane = lax.broadcasted_iota(jnp.int32, (1, LANES), 1)
        heads = (lane < 64, lane >= 64)
        nblk = _group_blocks(g)
        cc_v, ss_v = cc_ref[...], ss_ref[...]

        @pl.when(g == 0)
        def _():
            ag, obv, dobg = ag_ref[...], ob_ref[...], dobg_ref[...]
            dag_ref[...] = (dobg * obv * _dsilu(ag)).astype(BF16)
            dob = dobg * _silu(ag)
            do_t[...] = dob
            prod = dob * obv
            dl = jnp.concatenate(
                [jnp.broadcast_to(jnp.sum(prod[:, 0:64], axis=-1, keepdims=True), (S, 64)),
                 jnp.broadcast_to(jnp.sum(prod[:, 64:128], axis=-1, keepdims=True), (S, 64))], axis=1)
            dl_sw = pltpu.roll(dl, 64, 1)
            dl0_t[...] = jnp.where(heads[0], dl, dl_sw)
            dl1_t[...] = jnp.where(heads[0], dl_sw, dl)
            ls = lse_ref[...]
            ls_sw = pltpu.roll(ls, 64, 1)
            ls0_t[...] = jnp.where(heads[0], ls, ls_sw)
            ls1_t[...] = jnp.where(heads[0], ls_sw, ls)
            ks[0:ATT_PAD, :] = jnp.zeros((ATT_PAD, LANES), BF16)
            vs[0:ATT_PAD, :] = jnp.zeros((ATT_PAD, LANES), BF16)

        tmp[...] = _rope(q_ref[...], cc_v, ss_v, first_half) * ATT_SCALE
        _to_residues_dyn(g, qs, tmp)
        tmp[...] = _rope(k_ref[...], cc_v, ss_v, first_half)
        _to_residues_dyn(g, ks, tmp, ATT_PAD, BF16)
        _to_residues_dyn(g, vs, v_ref, ATT_PAD, BF16)
        _to_residues_dyn(g, dos, do_t)
        _to_residues_dyn(g, ls0, ls0_t)
        _to_residues_dyn(g, ls1, ls1_t)
        _to_residues_dyn(g, dl0, dl0_t)
        _to_residues_dyn(g, dl1, dl1_t)
        dks[...] = jnp.zeros_like(dks)
        dvs[...] = jnp.zeros_like(dvs)
        lss, dls = (ls0, ls1), (dl0, dl1)

        def unit(u, carry):
            start = pl.multiple_of(u * 128, 128)
            cur = pl.ds(start, 128)
            both = pl.ds(start, 256)
            pm = prev_ok & ((u & (nblk - 1)) != 0)
            qu, dou = qs[cur, :], dos[cur, :]
            kcat, vcat = ks[both, :], vs[both, :]
            dq_u = None
            q_l, do_l, ds_l, p_l = [], [], [], []
            for hh in range(2):
                q_h = jnp.where(heads[hh], qu, 0.0).astype(BF16)
                do_h = jnp.where(heads[hh], dou, 0.0).astype(BF16)
                s = _dot_nt(q_h, kcat)
                dp = _dot_nt(do_h, vcat)
                lse_h, dl_h = lss[hh][cur, :], dls[hh][cur, :]
                pp = jnp.where(pm, jnp.exp(s[:, 0:128] - lse_h), 0.0)
                pc = jnp.where(cur_ok, jnp.exp(s[:, 128:256] - lse_h), 0.0)
                ds = jnp.concatenate([pp * (dp[:, 0:128] - dl_h), pc * (dp[:, 128:256] - dl_h)], axis=1).astype(BF16)
                dq = _dot(ds, kcat)
                dq_u = dq if hh == 0 else jnp.where(heads[1], dq, dq_u)
                q_l.append(q_h)
                do_l.append(do_h)
                ds_l.append(ds)
                p_l.append(jnp.concatenate([pp, pc], axis=1).astype(BF16))
            dks[both, :] += _dot_tn(jnp.concatenate(ds_l, axis=0), jnp.concatenate(q_l, axis=0))
            dvs[both, :] += _dot_tn(jnp.concatenate(p_l, axis=0), jnp.concatenate(do_l, axis=0))
            dqs[cur, :] = dq_u
            return carry

        lax.fori_loop(0, 16, unit, 0, unroll=ATT_UNROLL)
        _from_residues_dyn(g, tmp, dqs)
        dq_ref[0] = (_rope(tmp[...], cc_v, -ss_v, first_half) * ATT_SCALE).astype(BF16)
        _from_residues_dyn(g, tmp, dks, ATT_PAD)
        dk_ref[0] = _rope(tmp[...], cc_v, -ss_v, first_half).astype(BF16)
        _from_residues_dyn(g, tmp, dvs, ATT_PAD)
        dv_ref[0] = tmp[...].astype(BF16)

    grp = pl.BlockSpec((1, S, LANES), lambda p, g: (g, 0, p))
    buf = pltpu.VMEM((S, LANES), F32)
    padded = pltpu.VMEM((ATT_PAD + S, LANES), F32)
    padded_b = pltpu.VMEM((ATT_PAD + S, LANES), BF16)
    return pl.pallas_call(
        body, name="attn_bwd", grid=(4, 3),
        out_shape=[jax.ShapeDtypeStruct((3, S, 512), BF16)] * 3 + [jax.ShapeDtypeStruct((S, 512), BF16)],
        in_specs=_attn_in_specs(3), out_specs=[grp, grp, grp, pl.BlockSpec((S, LANES), lambda p, g: (0, p))],
        scratch_shapes=[buf, buf, padded_b, padded_b, buf, buf, padded, padded] + [buf] * 9,
        compiler_params=_cp(("parallel", "arbitrary")),
    )(z, z, z, z, cc, ss, ob, lse, do_bg)


def _in_proj_bwd(dz, h, w_in):
    tk = 512
    nk = IN_COLS // tk

    def body(dz_ref, h_ref, w_ref, dh_ref, gw_ref, gwb_ref):
        @pl.when(pl.program_id(0) == 0)
        def _():
            dh_ref[...] = jnp.zeros_like(dh_ref)

        dzv = dz_ref[...]
        gw = _dot_tn(h_ref[...], dzv)
        gw_ref[...] = gw
        gwb_ref[...] = gw.astype(BF16)
        dh_ref[...] += _dot_nt(dzv, w_ref[...])

    return pl.pallas_call(
        body, name="in_proj_bwd", grid=(nk,),
        out_shape=[jax.ShapeDtypeStruct((S, D), F32), jax.ShapeDtypeStruct((D, IN_COLS), F32),
                   jax.ShapeDtypeStruct((D, IN_COLS), BF16)],
        in_specs=[pl.BlockSpec((S, tk), lambda j: (0, j)), pl.BlockSpec((S, D), lambda j: (0, 0)),
                  pl.BlockSpec((D, tk), lambda j: (0, j))],
        out_specs=[pl.BlockSpec((S, D), lambda j: (0, 0)), pl.BlockSpec((D, tk), lambda j: (0, j)),
                   pl.BlockSpec((D, tk), lambda j: (0, j))],
        compiler_params=_cp(("arbitrary",)),
    )(dz, h, w_in)


def _grad_x(x, norm_w, dh, dx2):
    tr = 256

    def body(x_ref, w_ref, dh_ref, dx2_ref, gx_ref, gnw_ref):
        @pl.when(pl.program_id(0) == 0)
        def _():
            gnw_ref[...] = jnp.zeros_like(gnw_ref)

        xv, dhv = x_ref[...], dh_ref[...]
        r = lax.rsqrt(jnp.mean(xv * xv, axis=-1, keepdims=True) + EPS)
        n = xv * r
        gnw_ref[...] += jnp.sum(dhv * n, axis=0, keepdims=True)
        dn = dhv * w_ref[...]
        gx_ref[...] = dx2_ref[...] + r * (dn - n * jnp.mean(dn * n, axis=-1, keepdims=True))

    row = pl.BlockSpec((tr, D), lambda i: (i, 0))
    vec = pl.BlockSpec((1, D), lambda i: (0, 0))
    return pl.pallas_call(
        body, name="grad_x", grid=(S // tr,),
        out_shape=[jax.ShapeDtypeStruct((S, D), F32), jax.ShapeDtypeStruct((1, D), F32)],
        in_specs=[row, vec, row, row], out_specs=[row, vec],
        compiler_params=_cp(("arbitrary",)),
    )(x, norm_w, dh, dx2)


def _rope_tables(positions):
    inv_freq = 10000.0 ** (-jnp.arange(0, 64, 2, dtype=F32) / 64)
    ang = positions.astype(F32)[:, None] * inv_freq[None, :]
    cos, sin = jnp.cos(ang), jnp.sin(ang)
    return jnp.tile(cos, (1, 4)), jnp.tile(jnp.concatenate([-sin, sin], axis=1), (1, 2))


def _local_step(x, positions, norm_w, lb_logits, hnw, fnw, target, w_in, w_a, w_b, w_out):
    cc, ss = _rope_tables(positions)
    lbv = jax.nn.sigmoid(lb_logits[0:1] - lb_logits[1:2])
    h = _rmsnorm_in(x, norm_w)
    z = _in_proj(h, w_in)
    o, o_a, states = _hgrn_fwd(z, lbv, hnw)
    ob, lse, o_bg = _attn_fwd(z, cc, ss)
    dx2, dx2b, dgp, do_a, do_bg, merged, dy_a, dy_b, tail_small = _tail(x, o_a, o_bg, z, target, w_a, w_b, w_out, fnw)
    g_out, gb_out = _tn_matmul(merged, dx2b, "grad_w_out")
    g_a, gb_a = _tn_matmul(o_a, dy_a, "grad_w_a")
    g_b, gb_b = _tn_matmul(o_bg, dy_b, "grad_w_b")
    dhq, dhf, dhi, dhg, glb, ghn = _hgrn_bwd(z, o, do_a, states, lbv, hnw)
    dq, dk, dv, dag = _attn_bwd(z, cc, ss, ob, lse, do_bg)
    dz = jnp.concatenate([dhq, dhf, dhi, dhg, dq[0], dq[1], dq[2], dk[0], dk[1], dk[2], dv[0], dv[1], dv[2], dag, dgp],
                         axis=1)
    dh, g_in, gb_in = _in_proj_bwd(dz, h, w_in)
    grad_x, gnw = _grad_x(x, norm_w, dh, dx2)
    ghn_row = jnp.pad(jnp.sum(ghn, axis=0), ((0, 0), (0, D - 128)))
    small = jnp.concatenate([gnw, glb, ghn_row, tail_small[0:2], jnp.zeros((3, D), F32)], axis=0)
    return grad_x, (g_in, g_a, g_b, g_out), (gb_in, gb_a, gb_b, gb_out), small


def kernel(x, positions, norm_w, w_in, lb_logits, hgrn_norm_w, w_branch_a, w_branch_b, w_out, final_norm_w, loss_target, m_norm_w, m_w_in, m_lb_logits, m_hgrn_norm_w, m_w_branch_a, m_w_branch_b, m_w_out, m_final_norm_w, v_norm_w, v_w_in, v_lb_logits, v_hgrn_norm_w, v_w_branch_a, v_w_branch_b, v_w_out, v_final_norm_w):
    ix, iy, ic = _mesh_pos()
    core = jnp.reshape(ic, (1,)).astype(jnp.int32)
    pos = jnp.stack([4 * ix + 2 * iy + ic, 2 * ix + iy]).astype(jnp.int32)

    shards = [w_in[0], w_branch_a[0], w_branch_b[0], w_out[0]]
    moments_m = [m_w_in[0], m_w_branch_a[0], m_w_branch_b[0], m_w_out[0]]
    moments_v = [v_w_in[0], v_w_branch_a[0], v_w_branch_b[0], v_w_out[0]]
    names = ("w_in", "w_a", "w_b", "w_out")
    full = _allgather_weights([_cast_bf16(w, f"cast_{nm}") for w, nm in zip(shards, names)])

    fnw2 = final_norm_w.reshape(1, D)
    grad_x, grads, gb, small = _local_step(x[0], positions[0], norm_w, lb_logits, hgrn_norm_w, fnw2,
                                           loss_target[0], *full)

    r1 = _exchange_sibling(gb)
    pb = [_chip_partials(a, grads[a], r1[a], core) for a in range(4)]
    *r2, gathered = _exchange_chips(pb, small)
    big = [_reduce_and_update(a, shards[a], moments_m[a], moments_v[a], grads[a], r1[a], r2[a], pos)
           for a in range(4)]
    sm = _small_update(gathered, norm_w, lb_logits, hgrn_norm_w, fnw2,
                       (m_norm_w, m_lb_logits, m_hgrn_norm_w, m_final_norm_w.reshape(1, D),
                        v_norm_w, v_lb_logits, v_hgrn_norm_w, v_final_norm_w.reshape(1, D)))
    loss = sm[0][0, 0]
    outs = [loss, grad_x[None]]
    for kind in range(4):
        s_nw, s_lb, s_hn, s_fn = sm[1 + 4 * kind:5 + 4 * kind]
        outs += [s_nw, big[0][kind][None], s_lb, s_hn, big[1][kind][None], big[2][kind][None],
                 big[3][kind][None], s_fn.reshape(D)]
    return tuple(outs)
```

```python
import functools

import jax
import jax.numpy as jnp
from jax import lax
from jax.experimental import pallas as pl
from jax.experimental.pallas import tpu as pltpu

F32 = jnp.float32
BF16 = jnp.bfloat16
MESH = pl.DeviceIdType.MESH

S = 2048
D = 1024
NDEV = 8
HEADS = 8
CHUNK = 64
SUB = 16
HBLK = 128
ATT_PAD = 128
ATT_UNROLL = 4
EXP_CLAMP = 80.0
EPS = 1e-6
IN_COLS = 11264
SHARD_COLS = IN_COLS // NDEV
ATT_DILS = (1, 4, 16)
ATT_SCALE = 64 ** -0.5
LANES = 128

ADAM_LR, ADAM_B1, ADAM_B2, ADAM_EPS, ADAM_WD, ADAM_STEP = 0.001, 0.9, 0.999, 1e-08, 0.01, 10

VMEM_LIMIT = 56 * 1024 * 1024


def _cp(sem=None, **kw):
    return pltpu.CompilerParams(dimension_semantics=sem, vmem_limit_bytes=VMEM_LIMIT, **kw)


def _dot(a, b):
    return jnp.dot(a, b, preferred_element_type=F32)


def _dot_nt(a, b):
    return lax.dot_general(a, b, (((1,), (1,)), ((), ())), preferred_element_type=F32)


def _dot_tn(a, b):
    return lax.dot_general(a, b, (((0,), (0,)), ((), ())), preferred_element_type=F32)


def _split2(x):
    hi = x.astype(BF16)
    lo = (x - hi.astype(F32)).astype(BF16)
    return hi, lo


def _split3(x):
    hi = x.astype(BF16)
    r = x - hi.astype(F32)
    mid = r.astype(BF16)
    lo = (r - mid.astype(F32)).astype(BF16)
    return hi, mid, lo


def _dot_ones(ones_bf16, x):
    hi, mid, lo = _split3(x)
    return _dot(ones_bf16, hi) + _dot(ones_bf16, mid) + _dot(ones_bf16, lo)


def _dot3(dotfn, a, b):
    ah, al = _split2(a)
    bh, bl = _split2(b)
    return dotfn(ah, bh) + dotfn(ah, bl) + dotfn(al, bh)


def _silu(x):
    return x * jax.nn.sigmoid(x)


def _dsilu(x):
    s = jax.nn.sigmoid(x)
    return s * (1.0 + x * (1.0 - s))


def _mesh_pos():
    return lax.axis_index("x"), lax.axis_index("y"), lax.axis_index("c")


def _shard_of(ref, a, d):
    if a == 0:
        return ref.at[:, pl.ds(pl.multiple_of(d * SHARD_COLS, LANES), SHARD_COLS)]
    if a == 2:
        return ref.at[:, pl.ds(pl.multiple_of(d * LANES, LANES), LANES)]
    return ref.at[pl.ds(pl.multiple_of(d * 128, 128), 128), :]


FULL_SHAPES = ((D, IN_COLS), (D, D), (512, D), (D, D))
SHARD_SHAPES = ((D, SHARD_COLS), (128, D), (512, 128), (128, D))


def _allgather_weights(shards):
    n = len(shards)

    def body(*refs):
        ins, outs = refs[:n], refs[n:2 * n]
        send_sems, recv_sems, local_sems = refs[2 * n:]
        x, y, c = _mesh_pos()
        me, sibling = (x, y, c), (x, y, 1 - c)
        chips = [(1 - x, y), (x, 1 - y), (1 - x, 1 - y)]

        def blk(a, p):
            return _shard_of(outs[a], a, 4 * p[0] + 2 * p[1] + p[2])

        def copy(a, k, block, to, src=None):
            return pltpu.make_async_remote_copy(
                src_ref=blk(a, block) if src is None else src, dst_ref=blk(a, block),
                send_sem=send_sems.at[a * 7 + k], recv_sem=recv_sems.at[a * 7 + k],
                device_id=to, device_id_type=MESH)

        mine = [pltpu.make_async_copy(ins[a], blk(a, me), local_sems.at[a]) for a in range(n)]
        for cp in mine:
            cp.start()
        first = []
        for a in range(n):
            first += [copy(a, 1 + j, me, (*chip, c), src=ins[a]) for j, chip in enumerate(chips)]
        for a in range(n):
            first.append(copy(a, 0, me, sibling, src=ins[a]))
        for cp in first:
            cp.start()
        passed = []
        for j, chip in enumerate(chips):
            for a in range(n):
                copy(a, 1 + j, (*chip, c), me).wait_recv()
                fwd = copy(a, 4 + j, (*chip, c), sibling)
                fwd.start()
                passed.append(fwd)
        for a in range(n):
            copy(a, 0, sibling, me).wait_recv()
        for j, chip in enumerate(chips):
            for a in range(n):
                copy(a, 4 + j, (*chip, 1 - c), me).wait_recv()
        for cp in first + passed:
            cp.wait_send()
        for cp in mine:
            cp.wait()

    any_spec = pl.BlockSpec(memory_space=pl.ANY)
    return pl.pallas_call(
        body, name="allgather_weights",
        out_shape=[jax.ShapeDtypeStruct(FULL_SHAPES[a], BF16) for a in range(n)],
        in_specs=[any_spec] * n, out_specs=[any_spec] * n,
        scratch_shapes=[pltpu.SemaphoreType.DMA((7 * n,)), pltpu.SemaphoreType.DMA((7 * n,)),
                        pltpu.SemaphoreType.DMA((n,))],
    )(*shards)


def _exchange_sibling(gb):
    n = len(gb)

    def body(*refs):
        ins, outs = refs[:n], refs[n:2 * n]
        send_sems, recv_sems = refs[2 * n:]
        x, y, c = _mesh_pos()
        sibling = (x, y, 1 - c)
        copies = []
        for a in range(n):
            for q in range(4):
                copies.append(pltpu.make_async_remote_copy(
                    src_ref=_shard_of(ins[a], a, 2 * q + (1 - c)), dst_ref=outs[a].at[q],
                    send_sem=send_sems.at[a * 4 + q], recv_sem=recv_sems.at[a * 4 + q],
                    device_id=sibling, device_id_type=MESH))
        for cp in copies:
            cp.start()
        for cp in copies:
            cp.wait()

    any_spec = pl.BlockSpec(memory_space=pl.ANY)
    return pl.pallas_call(
        body, name="grads_to_sibling",
        out_shape=[jax.ShapeDtypeStruct((4,) + SHARD_SHAPES[a], BF16) for a in range(n)],
        in_specs=[any_spec] * n, out_specs=[any_spec] * n,
        scratch_shapes=[pltpu.SemaphoreType.DMA((4 * n,)), pltpu.SemaphoreType.DMA((4 * n,))],
    )(*gb)


def _exchange_chips(pb, small):
    n = len(pb)

    def body(*refs):
        ins, small_ref = refs[:n], refs[n]
        outs, small_out = refs[n + 1:2 * n + 1], refs[2 * n + 1]
        send_sems, recv_sems, ssend, srecv, local_sem = refs[2 * n + 2:]
        x, y, c = _mesh_pos()
        chips = [(1 - x, y), (x, 1 - y), (1 - x, 1 - y)]
        me = 4 * x + 2 * y + c
        copies = []
        for a in range(n):
            for k, chip in enumerate(chips):
                copies.append(pltpu.make_async_remote_copy(
                    src_ref=ins[a].at[2 * chip[0] + chip[1]], dst_ref=outs[a].at[k],
                    send_sem=send_sems.at[a * 3 + k], recv_sem=recv_sems.at[a * 3 + k],
                    device_id=(*chip, c), device_id_type=MESH))
        for r in range(1, NDEV):
            peer = (1 - x if r & 4 else x, 1 - y if r & 2 else y, 1 - c if r & 1 else c)
            copies.append(pltpu.make_async_remote_copy(
                src_ref=small_ref, dst_ref=small_out.at[me],
                send_sem=ssend.at[r - 1], recv_sem=srecv.at[r - 1],
                device_id=peer, device_id_type=MESH))
        own = pltpu.make_async_copy(small_ref, small_out.at[me], local_sem)
        own.start()
        for cp in copies:
            cp.start()
        for cp in copies:
            cp.wait()
        own.wait()

    any_spec = pl.BlockSpec(memory_space=pl.ANY)
    return pl.pallas_call(
        body, name="grads_between_chips",
        out_shape=[jax.ShapeDtypeStruct((3,) + SHARD_SHAPES[a], BF16) for a in range(n)]
        + [jax.ShapeDtypeStruct((NDEV,) + small.shape, F32)],
        in_specs=[any_spec] * (n + 1), out_specs=[any_spec] * (n + 1),
        scratch_shapes=[pltpu.SemaphoreType.DMA((3 * n,)), pltpu.SemaphoreType.DMA((3 * n,)),
                        pltpu.SemaphoreType.DMA((NDEV - 1,)), pltpu.SemaphoreType.DMA((NDEV - 1,)),
                        pltpu.SemaphoreType.DMA],
    )(*pb, small)


def _shard_tiles(a):
    rows, cols = SHARD_SHAPES[a]
    tr = min(rows, 256)
    return (tr, cols), rows // tr


def _full_index(a, d, i):
    (tr, _), nt = _shard_tiles(a)
    if a in (0, 2):
        return (i, d)
    return (d * nt + i, 0)


def _cast_bf16(x, name):
    rows, cols = x.shape
    tr = min(rows, 256)

    def body(x_ref, o_ref):
        o_ref[...] = x_ref[...].astype(BF16)

    return pl.pallas_call(
        body, name=name, out_shape=jax.ShapeDtypeStruct(x.shape, BF16), grid=(rows // tr,),
        in_specs=[pl.BlockSpec((tr, cols), lambda i: (i, 0))],
        out_specs=pl.BlockSpec((tr, cols), lambda i: (i, 0)),
        compiler_params=_cp(("parallel",)),
    )(x)


def _chip_partials(a, g_full, r1, core):
    tile, nt = _shard_tiles(a)

    def body(c_ref, g_ref, r_ref, o_ref):
        o_ref[0] = (g_ref[...] + r_ref[0].astype(F32)).astype(BF16)

    grid_spec = pltpu.PrefetchScalarGridSpec(
        num_scalar_prefetch=1, grid=(4, nt),
        in_specs=[pl.BlockSpec(tile, lambda q, i, c: _full_index(a, 2 * q + c[0], i)),
                  pl.BlockSpec((1,) + tile, lambda q, i, c: (q, i, 0))],
        out_specs=pl.BlockSpec((1,) + tile, lambda q, i, c: (q, i, 0)))
    return pl.pallas_call(
        body, name=f"chip_partials_{a}", grid_spec=grid_spec,
        out_shape=jax.ShapeDtypeStruct((4,) + SHARD_SHAPES[a], BF16),
        compiler_params=_cp(("parallel", "parallel")),
    )(core, g_full, r1)


def _adam(w, g, m, v):
    m = ADAM_B1 * m + (1.0 - ADAM_B1) * g
    v = ADAM_B2 * v + (1.0 - ADAM_B2) * (g * g)
    m_hat = m / (1.0 - ADAM_B1 ** ADAM_STEP)
    v_hat = v / (1.0 - ADAM_B2 ** ADAM_STEP)
    delta = -ADAM_LR * (m_hat / (jnp.sqrt(v_hat) + ADAM_EPS) + ADAM_WD * w)
    return delta, m, v


def _reduce_and_update(a, w, m, v, g_full, r1, r2, pos):
    tile, nt = _shard_tiles(a)

    def body(p_ref, w_ref, m_ref, v_ref, g_ref, r1_ref, r2_ref, go_ref, do_ref, mo_ref, vo_ref):
        g = g_ref[...] + r1_ref[0].astype(F32)
        g = g + r2_ref[0].astype(F32)
        g = g + r2_ref[1].astype(F32)
        g = g + r2_ref[2].astype(F32)
        delta, m_new, v_new = _adam(w_ref[...], g, m_ref[...], v_ref[...])
        go_ref[...] = g
        do_ref[...] = delta
        mo_ref[...] = m_new
        vo_ref[...] = v_new

    own = pl.BlockSpec(tile, lambda i, p: (i, 0))
    grid_spec = pltpu.PrefetchScalarGridSpec(
        num_scalar_prefetch=1, grid=(nt,),
        in_specs=[own, own, own,
                  pl.BlockSpec(tile, lambda i, p: _full_index(a, p[0], i)),
                  pl.BlockSpec((1,) + tile, lambda i, p: (p[1], i, 0)),
                  pl.BlockSpec((3,) + tile, lambda i, p: (0, i, 0))],
        out_specs=[own] * 4)
    shp = jax.ShapeDtypeStruct(w.shape, F32)
    return pl.pallas_call(
        body, name=f"reduce_update_{a}", grid_spec=grid_spec, out_shape=[shp] * 4,
        compiler_params=_cp(("parallel",)),
    )(pos, w, m, v, g_full, r1, r2)


def _small_update(gathered, norm_w, lb_logits, hnw, fnw, moments):
    m_nw, m_lb, m_hn, m_fn, v_nw, v_lb, v_hn, v_fn = moments

    def body(g_ref, nw, lb, hn, fn, mnw, mlb, mhn, mfn, vnw, vlb, vhn, vfn,
             loss_o, g_nw, g_lb, g_hn, g_fn, d_nw, d_lb, d_hn, d_fn,
             mo_nw, mo_lb, mo_hn, mo_fn, vo_nw, vo_lb, vo_hn, vo_fn):
        tot = g_ref[0]
        for d in range(1, NDEV):
            tot = tot + g_ref[d]
        loss_o[...] = tot[4:5, 0:LANES]
        logits = lb[...]
        lbv = jax.nn.sigmoid(logits[0:1] - logits[1:2])
        chain = tot[1:2] * lbv * (1.0 - lbv)
        grads = (tot[0:1], jnp.concatenate([chain, -chain], axis=0), tot[2:3, 0:LANES], tot[3:4])
        outs = ((nw, mnw, vnw, g_nw, d_nw, mo_nw, vo_nw), (lb, mlb, vlb, g_lb, d_lb, mo_lb, vo_lb),
                (hn, mhn, vhn, g_hn, d_hn, mo_hn, vo_hn), (fn, mfn, vfn, g_fn, d_fn, mo_fn, vo_fn))
        for g, (w_r, m_r, v_r, g_o, d_o, m_o, v_o) in zip(grads, outs):
            delta, m_new, v_new = _adam(w_r[...], g, m_r[...], v_r[...])
            g_o[...] = g
            d_o[...] = delta
            m_o[...] = m_new
            v_o[...] = v_new

    shapes = [norm_w.shape, lb_logits.shape, hnw.shape, fnw.shape]
    out_shape = [jax.ShapeDtypeStruct((1, LANES), F32)] + [jax.ShapeDtypeStruct(s, F32) for s in shapes] * 4
    return pl.pallas_call(body, name="small_update", out_shape=out_shape, compiler_params=_cp())(
        gathered, norm_w, lb_logits, hnw, fnw, m_nw, m_lb, m_hn, m_fn, v_nw, v_lb, v_hn, v_fn)


def _rmsnorm_in(x, norm_w):
    tr = 512

    def body(x_ref, w_ref, h_ref):
        xv = x_ref[...]
        r = lax.rsqrt(jnp.mean(xv * xv, axis=-1, keepdims=True) + EPS)
        h_ref[...] = (xv * r * w_ref[...]).astype(BF16)

    return pl.pallas_call(
        body, name="rmsnorm_in", out_shape=jax.ShapeDtypeStruct((S, D), BF16), grid=(S // tr,),
        in_specs=[pl.BlockSpec((tr, D), lambda i: (i, 0)), pl.BlockSpec((1, D), lambda i: (0, 0))],
        out_specs=pl.BlockSpec((tr, D), lambda i: (i, 0)),
        compiler_params=_cp(("parallel",)),
    )(x, norm_w)


def _in_proj(h, w_in):
    tn = 1024

    def body(h_ref, w_ref, z_ref):
        z_ref[...] = _dot(h_ref[...], w_ref[...])

    return pl.pallas_call(
        body, name="in_proj", out_shape=jax.ShapeDtypeStruct((S, IN_COLS), F32), grid=(IN_COLS // tn,),
        in_specs=[pl.BlockSpec((S, D), lambda j: (0, 0)), pl.BlockSpec((D, tn), lambda j: (0, j))],
        out_specs=pl.BlockSpec((S, tn), lambda j: (0, j)),
        compiler_params=_cp(("parallel",)),
    )(h, w_in)


def _block_tri(n, block, upper=False):
    r = lax.broadcasted_iota(jnp.int32, (n, n), 0)
    c = lax.broadcasted_iota(jnp.int32, (n, n), 1)
    keep = (c >= r) if upper else (c <= r)
    return jnp.where(keep & ((r // block) == (c // block)), 1.0, 0.0).astype(BF16)


def _tril_mask(n):
    r = lax.broadcasted_iota(jnp.int32, (n, n), 0)
    c = lax.broadcasted_iota(jnp.int32, (n, n), 1)
    return c <= r


def _chunk_scores(q, k, b, bex, r0, mask):
    rows = slice(r0, r0 + CHUNK)
    parts, qs_l, ek_l, eq_l = [], [], [], []
    for i in range(CHUNK // SUB):
        ri = slice(r0 + SUB * i, r0 + SUB * (i + 1))
        base = bex[r0 + SUB * i:r0 + SUB * i + 1]
        eq = jnp.exp(b[ri] - base)
        ek = jnp.exp(jnp.minimum(base - b[rows], EXP_CLAMP))
        qs = q[ri] * eq
        parts.append(_dot_nt(qs.astype(BF16), (k[rows] * ek).astype(BF16)))
        qs_l.append(qs)
        ek_l.append(ek)
        eq_l.append(eq)
    return jnp.where(mask, jnp.concatenate(parts, axis=0), 0.0), qs_l, ek_l, eq_l


def _hgrn_cols(hq, hf, hi, lb):
    sg = jax.nn.sigmoid(hf)
    f = lb + (1.0 - lb) * sg
    g = jnp.log(f)
    b = _dot_ones(_block_tri(HBLK, CHUNK), g)
    return _silu(hq), 1.0 - f, g, hi, sg, f, b


def _hgrn_fwd(z, lbv, hnw):
    ntb, nch = S // HBLK, HBLK // CHUNK

    def body(hq_ref, hf_ref, hi_ref, hg_ref, lb_ref, hnw_ref, o_ref, oa_ref, st_ref, state):
        @pl.when(pl.program_id(0) == 0)
        def _():
            state[...] = jnp.zeros_like(state)

        q_a, k_a, g_a, v_a, _, _, b_a = _hgrn_cols(hq_ref[...], hf_ref[...], hi_ref[...], lb_ref[...])
        bex_a = b_a - g_a
        eb_a = jnp.exp(b_a)
        mask = _tril_mask(CHUNK)
        hg = hg_ref[...]
        w = hnw_ref[...]
        for h in range(HEADS):
            cols = slice(128 * h, 128 * h + 128)
            q, k, v, b, bex, eb = q_a[:, cols], k_a[:, cols], v_a[:, cols], b_a[:, cols], bex_a[:, cols], eb_a[:, cols]
            st = state[h]
            outs = []
            for c in range(nch):
                r0 = c * CHUNK
                rows = slice(r0, r0 + CHUNK)
                a, _, _, _ = _chunk_scores(q, k, b, bex, r0, mask)
                vb = v[rows].astype(BF16)
                b_last = b[r0 + CHUNK - 1:r0 + CHUNK]
                qe = (q[rows] * eb[rows]).astype(BF16)
                outs.append(_dot(a.astype(BF16), vb) + _dot_nt(qe, st.astype(BF16)))
                st_ref[h, c] = st
                ke = (k[rows] * jnp.exp(b_last - b[rows])).astype(BF16)
                st = st * jnp.exp(b_last) + _dot_tn(vb, ke)
            state[h] = st
            o = jnp.concatenate(outs, axis=0)
            o_ref[:, cols] = o
            r = lax.rsqrt(jnp.mean(o * o, axis=-1, keepdims=True) + EPS)
            oa_ref[:, cols] = (o * r * w * _silu(hg[:, cols])).astype(BF16)

    def zcol(j):
        return pl.BlockSpec((HBLK, D), lambda t: (t, j))

    out_blk = pl.BlockSpec((HBLK, D), lambda t: (t, 0))
    return pl.pallas_call(
        body, name="hgrn_fwd", grid=(ntb,),
        out_shape=[jax.ShapeDtypeStruct((S, D), F32), jax.ShapeDtypeStruct((S, D), BF16),
                   jax.ShapeDtypeStruct((HEADS, S // CHUNK, 128, 128), F32)],
        in_specs=[zcol(0), zcol(1), zcol(2), zcol(3),
                  pl.BlockSpec((1, D), lambda t: (0, 0)), pl.BlockSpec((1, 128), lambda t: (0, 0))],
        out_specs=[out_blk, out_blk, pl.BlockSpec((HEADS, nch, 128, 128), lambda t: (0, t, 0, 0))],
        scratch_shapes=[pltpu.VMEM((HEADS, 128, 128), F32)],
        compiler_params=_cp(("arbitrary",)),
    )(z, z, z, z, lbv, hnw)


def _half_mask():
    lane = lax.broadcasted_iota(jnp.int32, (1, LANES), 1)
    return (lane % 64) < 32


def _rope(t, cc, ss, first_half):
    partner = jnp.where(first_half, pltpu.roll(t, 96, 1), pltpu.roll(t, 32, 1))
    return t * cc + partner * ss


def _attn_masks():
    i = lax.broadcasted_iota(jnp.int32, (128, 128), 0)
    j = lax.broadcasted_iota(jnp.int32, (128, 128), 1)
    return j >= i, j <= i


def _to_residues_dyn(g, dst, src, row0=0, dtype=None):
    for gi, dil in enumerate((1, 4, 16)):
        m = S // dil

        @pl.when(g == gi)
        def _(dil=dil, m=m):
            for r in range(dil):
                v = src[...] if dil == 1 else src[pl.ds(r, m, stride=dil), :]
                if dtype is not None:
                    v = v.astype(dtype)
                dst[row0 + r * m:row0 + (r + 1) * m, 0:LANES] = v


def _from_residues_dyn(g, dst, src, row0=0):
    for gi, dil in enumerate((1, 4, 16)):
        m = S // dil

        @pl.when(g == gi)
        def _(dil=dil, m=m):
            for r in range(dil):
                v = src[row0 + r * m:row0 + (r + 1) * m, :]
                if dil == 1:
                    dst[...] = v
                else:
                    dst[pl.ds(r, m, stride=dil), :] = v


def _group_blocks(g):
    return jnp.where(g == 0, 16, jnp.where(g == 1, 4, 1))


def _attn_in_specs(extra):
    def zcol(off):
        return pl.BlockSpec((S, LANES), lambda p, g: (0, off + 4 * g + p))

    per_pair = pl.BlockSpec((S, LANES), lambda p, g: (0, p))
    const = pl.BlockSpec((S, LANES), lambda p, g: (0, 0))
    return [zcol(32), zcol(44), zcol(56), pl.BlockSpec((S, LANES), lambda p, g: (0, 68 + p)), const, const] + [per_pair] * extra


def _attn_fwd(z, cc, ss):
    def body(q_ref, k_ref, v_ref, ag_ref, cc_ref, ss_ref, ob_ref, lse_ref, obg_ref,
             tmp, qs, ks, vx, og, mg, lg, o_t, m_t, l_t, o_acc, m_acc, l_acc):
        g = pl.program_id(1)
        first_half = _half_mask()
        prev_ok, cur_ok = _attn_masks()
        lane = lax.broadcasted_iota(jnp.int32, (1, LANES), 1)
        heads = (lane < 64, lane >= 64)
        nblk = _group_blocks(g)

        @pl.when(g == 0)
        def _():
            ks[0:ATT_PAD, :] = jnp.zeros((ATT_PAD, LANES), BF16)
            vx[0:ATT_PAD, 0:LANES] = jnp.zeros((ATT_PAD, LANES), BF16)
            vx[:, LANES:2 * LANES] = jnp.ones((ATT_PAD + S, LANES), BF16)

        tmp[...] = _rope(q_ref[...], cc_ref[...], ss_ref[...], first_half) * ATT_SCALE
        _to_residues_dyn(g, qs, tmp)
        tmp[...] = _rope(k_ref[...], cc_ref[...], ss_ref[...], first_half)
        _to_residues_dyn(g, ks, tmp, ATT_PAD, BF16)
        _to_residues_dyn(g, vx, v_ref, ATT_PAD, BF16)

        def unit(u, carry):
            start = pl.multiple_of(u * 128, 128)
            cur = pl.ds(start, 128)
            pm = prev_ok & ((u & (nblk - 1)) != 0)
            qu = qs[cur, :]
            kcat = ks[pl.ds(start, 256), :]
            vext = vx[pl.ds(start, 256), :]
            o_u = m_u = l_u = None
            for hh in range(2):
                s = _dot_nt(jnp.where(heads[hh], qu, 0.0).astype(BF16), kcat)
                sp = jnp.where(pm, s[:, 0:128], -jnp.inf)
                sc = jnp.where(cur_ok, s[:, 128:256], -jnp.inf)
                m = jnp.max(jnp.maximum(sp, sc), axis=-1, keepdims=True)
                p = jnp.concatenate([jnp.exp(sp - m), jnp.exp(sc - m)], axis=1).astype(BF16)
                ol = _dot(p, vext)
                mb = jnp.broadcast_to(m, (128, LANES))
                if hh == 0:
                    o_u, l_u, m_u = ol[:, 0:128], ol[:, 128:256], mb
                else:
                    o_u = jnp.where(heads[1], ol[:, 0:128], o_u)
                    l_u = jnp.where(heads[1], ol[:, 128:256], l_u)
                    m_u = jnp.where(heads[1], mb, m_u)
            og[cur, :] = o_u
            mg[cur, :] = m_u
            lg[cur, :] = l_u
            return carry

        lax.fori_loop(0, 16, unit, 0, unroll=ATT_UNROLL)
        _from_residues_dyn(g, o_t, og)
        _from_residues_dyn(g, m_t, mg)
        _from_residues_dyn(g, l_t, lg)

        @pl.when(g == 0)
        def _():
            o_acc[...] = o_t[...]
            m_acc[...] = m_t[...]
            l_acc[...] = l_t[...]

        @pl.when(g > 0)
        def _():
            m_new = jnp.maximum(m_acc[...], m_t[...])
            wa, wb = jnp.exp(m_acc[...] - m_new), jnp.exp(m_t[...] - m_new)
            o_acc[...] = o_acc[...] * wa + o_t[...] * wb
            l_acc[...] = l_acc[...] * wa + l_t[...] * wb
            m_acc[...] = m_new

        @pl.when(g == 2)
        def _():
            ob = o_acc[...] / l_acc[...]
            ob_ref[...] = ob
            lse_ref[...] = m_acc[...] + jnp.log(l_acc[...])
            obg_ref[...] = (ob * _silu(ag_ref[...])).astype(BF16)

    blk = pl.BlockSpec((S, LANES), lambda p, g: (0, p))
    buf = pltpu.VMEM((S, LANES), F32)
    return pl.pallas_call(
        body, name="attn_fwd", grid=(4, 3),
        out_shape=[jax.ShapeDtypeStruct((S, 512), F32), jax.ShapeDtypeStruct((S, 512), F32),
                   jax.ShapeDtypeStruct((S, 512), BF16)],
        in_specs=_attn_in_specs(0), out_specs=[blk, blk, blk],
        scratch_shapes=[buf, buf, pltpu.VMEM((ATT_PAD + S, LANES), BF16), pltpu.VMEM((ATT_PAD + S, 2 * LANES), BF16)] + [buf] * 9,
        compiler_params=_cp(("parallel", "arbitrary")),
    )(z, z, z, z, cc, ss)


def _tail(x, o_a, o_bg, z, target, w_a, w_b, w_out, fnw):
    tm = 256

    def body(x_ref, oa_ref, ob_ref, gpa_ref, gpb_ref, t_ref, wa_ref, wb_ref, wo_ref, fnw_ref,
             dx2_ref, dx2b_ref, dgp_ref, doa_ref, dob_ref, mg_ref, dya_ref, dyb_ref, small_ref):
        @pl.when(pl.program_id(0) == 0)
        def _():
            small_ref[...] = jnp.zeros_like(small_ref)

        wa, wb, wo = wa_ref[...], wb_ref[...], wo_ref[...]
        y_a = _dot(oa_ref[...], wa)
        y_b = _dot(ob_ref[...], wb)
        ga = jax.nn.sigmoid(gpa_ref[...])
        gb = jax.nn.sigmoid(gpb_ref[...])
        merged = (ga * y_a + gb * y_b).astype(BF16)
        x2 = x_ref[...] + _dot(merged, wo)
        r2 = lax.rsqrt(jnp.mean(x2 * x2, axis=-1, keepdims=True) + EPS)
        n2 = x2 * r2
        fw = fnw_ref[...]
        err = n2 * fw - t_ref[...]
        loss = 0.5 * jnp.sum(jnp.sum(err * err, axis=-1, keepdims=True), axis=0, keepdims=True) / D
        dy = err * (1.0 / D)
        g_fnw = jnp.sum(dy * n2, axis=0, keepdims=True)
        dn = dy * fw
        dx2 = r2 * (dn - n2 * jnp.mean(dn * n2, axis=-1, keepdims=True))
        dx2b = dx2.astype(BF16)
        dmerged = _dot_nt(dx2b, wo)
        dy_a = (dmerged * ga).astype(BF16)
        dy_b = (dmerged * gb).astype(BF16)
        dx2_ref[...] = dx2
        dx2b_ref[...] = dx2b
        dgp_ref[:, 0:D] = (dmerged * y_a * ga * (1.0 - ga)).astype(BF16)
        dgp_ref[:, D:2 * D] = (dmerged * y_b * gb * (1.0 - gb)).astype(BF16)
        doa_ref[...] = _dot_nt(dy_a, wa)
        dob_ref[...] = _dot_nt(dy_b, wb)
        mg_ref[...] = merged
        dya_ref[...] = dy_a
        dyb_ref[...] = dy_b
        small_ref[0:1, :] += g_fnw
        small_ref[1:2, :] += jnp.broadcast_to(loss, (1, D))

    def rows(cols, off=0):
        return pl.BlockSpec((tm, cols), lambda i: (i, off))

    def whole(shape):
        return pl.BlockSpec(shape, lambda i: (0, 0))

    return pl.pallas_call(
        body, name="tail", grid=(S // tm,),
        out_shape=[jax.ShapeDtypeStruct((S, D), F32), jax.ShapeDtypeStruct((S, D), BF16),
                   jax.ShapeDtypeStruct((S, 2 * D), BF16), jax.ShapeDtypeStruct((S, D), F32),
                   jax.ShapeDtypeStruct((S, 512), F32), jax.ShapeDtypeStruct((S, D), BF16),
                   jax.ShapeDtypeStruct((S, D), BF16), jax.ShapeDtypeStruct((S, D), BF16),
                   jax.ShapeDtypeStruct((8, D), F32)],
        in_specs=[rows(D), rows(D), rows(512), rows(D, 9), rows(D, 10), rows(D),
                  whole((D, D)), whole((512, D)), whole((D, D)), whole((1, D))],
        out_specs=[rows(D), rows(D), rows(2 * D), rows(D), rows(512), rows(D), rows(D), rows(D), whole((8, D))],
        compiler_params=_cp(("arbitrary",)),
    )(x, o_a, o_bg, z, z, target, w_a, w_b, w_out, fnw)


def _tn_matmul(a, b, name):
    m, n = a.shape[1], b.shape[1]
    tn = 512

    def body(a_ref, b_ref, o_ref, ob_ref):
        acc = _dot_tn(a_ref[...], b_ref[...])
        o_ref[...] = acc
        ob_ref[...] = acc.astype(BF16)

    out_blk = pl.BlockSpec((m, tn), lambda j: (0, j))
    return pl.pallas_call(
        body, name=name, grid=(n // tn,),
        out_shape=[jax.ShapeDtypeStruct((m, n), F32), jax.ShapeDtypeStruct((m, n), BF16)],
        in_specs=[pl.BlockSpec((S, m), lambda j: (0, 0)), pl.BlockSpec((S, tn), lambda j: (0, j))],
        out_specs=[out_blk, out_blk],
        compiler_params=_cp(("parallel",)),
    )(a, b)


def _hgrn_bwd(z, o, do_a, states, lbv, hnw):
    ntb, nch = S // HBLK, HBLK // CHUNK

    def body(hq_ref, hf_ref, hi_ref, hg_ref, o_ref, doa_ref, st_ref, lb_ref, hnw_ref,
             dhq_ref, dhf_ref, dhi_ref, dhg_ref, glb_ref, ghn_ref, dstate):
        @pl.when(pl.program_id(0) == 0)
        def _():
            dstate[...] = jnp.zeros_like(dstate)
            glb_ref[...] = jnp.zeros_like(glb_ref)
            ghn_ref[...] = jnp.zeros_like(ghn_ref)

        lb_a = lb_ref[...]
        hq_a, hg_a = hq_ref[...], hg_ref[...]
        q_a, k_a, g_a, v_a, sg_a, f_a, b_a = _hgrn_cols(hq_a, hf_ref[...], hi_ref[...], lb_a)
        bex_a = b_a - g_a
        eb_a = jnp.exp(b_a)
        w = hnw_ref[...]
        mask = _tril_mask(CHUNK)
        upper = _block_tri(CHUNK, CHUNK, upper=True)
        for h in range(HEADS):
            cols = slice(128 * h, 128 * h + 128)
            q, k, v, b, bex, eb = q_a[:, cols], k_a[:, cols], v_a[:, cols], b_a[:, cols], bex_a[:, cols], eb_a[:, cols]
            hq, hg, sg, f, lb = hq_a[:, cols], hg_a[:, cols], sg_a[:, cols], f_a[:, cols], lb_a[:, cols]
            ov, doa = o_ref[:, cols], doa_ref[:, cols]
            r = lax.rsqrt(jnp.mean(ov * ov, axis=-1, keepdims=True) + EPS)
            n = ov * r
            sil = _silu(hg)
            dhg_ref[:, cols] = (doa * n * w * _dsilu(hg)).astype(BF16)
            ghn_ref[h] += jnp.sum(doa * sil * n, axis=0, keepdims=True)
            dn = doa * sil * w
            do = r * (dn - n * jnp.mean(dn * n, axis=-1, keepdims=True))

            dst = dstate[h]
            dq_l, dk_l, dv_l, dg_l = [None] * nch, [None] * nch, [None] * nch, [None] * nch
            for c in reversed(range(nch)):
                r0 = c * CHUNK
                rows = slice(r0, r0 + CHUNK)
                st = st_ref[h, c]
                bc, kc, qc = b[rows], k[rows], q[rows]
                vb, dob = v[rows].astype(BF16), do[rows].astype(BF16)
                b_last = bc[CHUNK - 1:CHUNK]
                e_last = jnp.exp(b_last)
                ekl = jnp.exp(b_last - bc)
                dstb = dst.astype(BF16)
                a, qs_l, ek_l, eq_l = _chunk_scores(q, k, b, bex, r0, mask)
                da = jnp.where(mask, _dot_nt(dob, vb), 0.0)
                dv_l[c] = _dot_tn(a.astype(BF16), dob) + _dot_nt((kc * ekl).astype(BF16), dstb)
                dq_inter = _dot(dob, st.astype(BF16)) * eb[rows]
                dk_state = _dot(vb, dstb) * ekl
                dq_parts, dk_intra = [], jnp.zeros((CHUNK, 128), F32)
                for i in range(CHUNK // SUB):
                    da_i = da[SUB * i:SUB * (i + 1)]
                    dq_parts.append(_dot3(_dot, da_i, kc * ek_l[i]) * eq_l[i])
                    dk_intra = dk_intra + _dot3(_dot_tn, da_i, qs_l[i]) * ek_l[i]
                dq = jnp.concatenate(dq_parts, axis=0) + dq_inter
                dk = dk_intra + dk_state
                last = (e_last * jnp.sum(st * dst, axis=0, keepdims=True)
                        + jnp.sum(kc * dk_state, axis=0, keepdims=True))
                dg_l[c] = _dot_ones(upper, qc * dq - kc * dk) + last
                dq_l[c], dk_l[c] = dq, dk
                dst = dst * e_last + _dot_tn(dob, (qc * eb[rows]).astype(BF16))
            dstate[h] = dst
            dq, dk = jnp.concatenate(dq_l, axis=0), jnp.concatenate(dk_l, axis=0)
            dg, dv = jnp.concatenate(dg_l, axis=0), jnp.concatenate(dv_l, axis=0)
            dhq_ref[:, cols] = (dq * _dsilu(hq)).astype(BF16)
            dhi_ref[:, cols] = dv.astype(BF16)
            df = dg / f - dk
            dhf_ref[:, cols] = (df * (1.0 - lb) * sg * (1.0 - sg)).astype(BF16)
            glb_ref[:, cols] += jnp.sum(df * (1.0 - sg), axis=0, keepdims=True)

    def rev(t):
        return ntb - 1 - t

    def zcol(j):
        return pl.BlockSpec((HBLK, D), lambda t: (rev(t), j))

    blk = pl.BlockSpec((HBLK, D), lambda t: (rev(t), 0))
    return pl.pallas_call(
        body, name="hgrn_bwd", grid=(ntb,),
        out_shape=[jax.ShapeDtypeStruct((S, D), BF16)] * 4
        + [jax.ShapeDtypeStruct((1, D), F32), jax.ShapeDtypeStruct((HEADS, 1, 128), F32)],
        in_specs=[zcol(0), zcol(1), zcol(2), zcol(3), blk, blk,
                  pl.BlockSpec((HEADS, nch, 128, 128), lambda t: (0, rev(t), 0, 0)),
                  pl.BlockSpec((1, D), lambda t: (0, 0)), pl.BlockSpec((1, 128), lambda t: (0, 0))],
        out_specs=[blk] * 4 + [pl.BlockSpec((1, D), lambda t: (0, 0)),
                               pl.BlockSpec((HEADS, 1, 128), lambda t: (0, 0, 0))],
        scratch_shapes=[pltpu.VMEM((HEADS, 128, 128), F32)],
        compiler_params=_cp(("arbitrary",)),
    )(z, z, z, z, o, do_a, states, lbv, hnw)


def _attn_bwd(z, cc, ss, ob, lse, do_bg):
    def body(q_ref, k_ref, v_ref, ag_ref, cc_ref, ss_ref, ob_ref, lse_ref, dobg_ref,
             dq_ref, dk_ref, dv_ref, dag_ref,
             tmp, qs, ks, vs, dos, dqs, dks, dvs, dkp, dvp, do_t, ls0_t, ls1_t, dl0_t, dl1_t, ls0, ls1, dl0, dl1):
        g = pl.program_id(1)
        first_half = _half_mask()
        prev_ok, cur_ok = _attn_masks()
        lane = lax.broadcasted_iota(jnp.int32, (1, LANES), 1)
        heads = (lane < 64, lane >= 64)
        nblk = _group_blocks(g)
        cc_v, ss_v = cc_ref[...], ss_ref[...]

        @pl.when(g == 0)
        def _():
            ag, obv, dobg = ag_ref[...], ob_ref[...], dobg_ref[...]
            dag_ref[...] = (dobg * obv * _dsilu(ag)).astype(BF16)
            dob = dobg * _silu(ag)
            do_t[...] = dob
            prod = dob * obv
            dl = jnp.concatenate(
                [jnp.broadcast_to(jnp.sum(prod[:, 0:64], axis=-1, keepdims=True), (S, 64)),
                 jnp.broadcast_to(jnp.sum(prod[:, 64:128], axis=-1, keepdims=True), (S, 64))], axis=1)
            dl_sw = pltpu.roll(dl, 64, 1)
            dl0_t[...] = jnp.where(heads[0], dl, dl_sw)
            dl1_t[...] = jnp.where(heads[0], dl_sw, dl)
            ls = lse_ref[...]
            ls_sw = pltpu.roll(ls, 64, 1)
            ls0_t[...] = jnp.where(heads[0], ls, ls_sw)
            ls1_t[...] = jnp.where(heads[0], ls_sw, ls)
            ks[0:ATT_PAD, :] = jnp.zeros((ATT_PAD, LANES), BF16)
            vs[0:ATT_PAD, :] = jnp.zeros((ATT_PAD, LANES), BF16)

        tmp[...] = _rope(q_ref[...], cc_v, ss_v, first_half) * ATT_SCALE
        _to_residues_dyn(g, qs, tmp)
        tmp[...] = _rope(k_ref[...], cc_v, ss_v, first_half)
        _to_residues_dyn(g, ks, tmp, ATT_PAD, BF16)
        _to_residues_dyn(g, vs, v_ref, ATT_PAD, BF16)
        _to_residues_dyn(g, dos, do_t)
        _to_residues_dyn(g, ls0, ls0_t)
        _to_residues_dyn(g, ls1, ls1_t)
        _to_residues_dyn(g, dl0, dl0_t)
        _to_residues_dyn(g, dl1, dl1_t)
        lss, dls = (ls0, ls1), (dl0, dl1)

        def unit(u, carry):
            start = pl.multiple_of(u * 128, 128)
            cur = pl.ds(start, 128)
            both = pl.ds(start, 256)
            pm = prev_ok & ((u & (nblk - 1)) != 0)
            qu, dou = qs[cur, :], dos[cur, :]
            kcat, vcat = ks[both, :], vs[both, :]
            dq_u = None
            q_l, do_l, ds_l, p_l = [], [], [], []
            for hh in range(2):
                q_h = jnp.where(heads[hh], qu, 0.0).astype(BF16)
                do_h = jnp.where(heads[hh], dou, 0.0).astype(BF16)
                s = _dot_nt(q_h, kcat)
                dp = _dot_nt(do_h, vcat)
                lse_h, dl_h = lss[hh][cur, :], dls[hh][cur, :]
                pp = jnp.where(pm, jnp.exp(s[:, 0:128] - lse_h), 0.0)
                pc = jnp.where(cur_ok, jnp.exp(s[:, 128:256] - lse_h), 0.0)
                ds = jnp.concatenate([pp * (dp[:, 0:128] - dl_h), pc * (dp[:, 128:256] - dl_h)], axis=1).astype(BF16)
                dq = _dot(ds, kcat)
                dq_u = dq if hh == 0 else jnp.where(heads[1], dq, dq_u)
                q_l.append(q_h)
                do_l.append(do_h)
                ds_l.append(ds)
                p_l.append(jnp.concatenate([pp, pc], axis=1).astype(BF16))
            dkcat = _dot_tn(jnp.concatenate(ds_l, axis=0), jnp.concatenate(q_l, axis=0))
            dvcat = _dot_tn(jnp.concatenate(p_l, axis=0), jnp.concatenate(do_l, axis=0))
            dkp[cur, :] = dkcat[0:128]
            dks[cur, :] = dkcat[128:256]
            dvp[cur, :] = dvcat[0:128]
            dvs[cur, :] = dvcat[128:256]
            dqs[cur, :] = dq_u
            return carry

        lax.fori_loop(0, 16, unit, 0, unroll=ATT_UNROLL)
        dks[0:S - 128, :] += dkp[128:S, :]
        dvs[0:S - 128, :] += dvp[128:S, :]
        _from_residues_dyn(g, tmp, dqs)
        dq_ref[0] = (_rope(tmp[...], cc_v, -ss_v, first_half) * ATT_SCALE).astype(BF16)
        _from_residues_dyn(g, tmp, dks)
        dk_ref[0] = _rope(tmp[...], cc_v, -ss_v, first_half).astype(BF16)
        _from_residues_dyn(g, tmp, dvs)
        dv_ref[0] = tmp[...].astype(BF16)

    grp = pl.BlockSpec((1, S, LANES), lambda p, g: (g, 0, p))
    buf = pltpu.VMEM((S, LANES), F32)
    padded_b = pltpu.VMEM((ATT_PAD + S, LANES), BF16)
    return pl.pallas_call(
        body, name="attn_bwd", grid=(4, 3),
        out_shape=[jax.ShapeDtypeStruct((3, S, 512), BF16)] * 3 + [jax.ShapeDtypeStruct((S, 512), BF16)],
        in_specs=_attn_in_specs(3), out_specs=[grp, grp, grp, pl.BlockSpec((S, LANES), lambda p, g: (0, p))],
        scratch_shapes=[buf, buf, padded_b, padded_b] + [buf] * 15,
        compiler_params=_cp(("parallel", "arbitrary")),
    )(z, z, z, z, cc, ss, ob, lse, do_bg)


def _in_proj_bwd(dz, h, w_in):
    tk = 512
    nk = IN_COLS // tk

    def body(dz_ref, h_ref, w_ref, dh_ref, gw_ref, gwb_ref):
        @pl.when(pl.program_id(0) == 0)
        def _():
            dh_ref[...] = jnp.zeros_like(dh_ref)

        dzv = dz_ref[...]
        gw = _dot_tn(h_ref[...], dzv)
        gw_ref[...] = gw
        gwb_ref[...] = gw.astype(BF16)
        dh_ref[...] += _dot_nt(dzv, w_ref[...])

    return pl.pallas_call(
        body, name="in_proj_bwd", grid=(nk,),
        out_shape=[jax.ShapeDtypeStruct((S, D), F32), jax.ShapeDtypeStruct((D, IN_COLS), F32),
                   jax.ShapeDtypeStruct((D, IN_COLS), BF16)],
        in_specs=[pl.BlockSpec((S, tk), lambda j: (0, j)), pl.BlockSpec((S, D), lambda j: (0, 0)),
                  pl.BlockSpec((D, tk), lambda j: (0, j))],
        out_specs=[pl.BlockSpec((S, D), lambda j: (0, 0)), pl.BlockSpec((D, tk), lambda j: (0, j)),
                   pl.BlockSpec((D, tk), lambda j: (0, j))],
        compiler_params=_cp(("arbitrary",)),
    )(dz, h, w_in)


def _grad_x(x, norm_w, dh, dx2):
    tr = 256

    def body(x_ref, w_ref, dh_ref, dx2_ref, gx_ref, gnw_ref):
        @pl.when(pl.program_id(0) == 0)
        def _():
            gnw_ref[...] = jnp.zeros_like(gnw_ref)

        xv, dhv = x_ref[...], dh_ref[...]
        r = lax.rsqrt(jnp.mean(xv * xv, axis=-1, keepdims=True) + EPS)
        n = xv * r
        gnw_ref[...] += jnp.sum(dhv * n, axis=0, keepdims=True)
        dn = dhv * w_ref[...]
        gx_ref[...] = dx2_ref[...] + r * (dn - n * jnp.mean(dn * n, axis=-1, keepdims=True))

    row = pl.BlockSpec((tr, D), lambda i: (i, 0))
    vec = pl.BlockSpec((1, D), lambda i: (0, 0))
    return pl.pallas_call(
        body, name="grad_x", grid=(S // tr,),
        out_shape=[jax.ShapeDtypeStruct((S, D), F32), jax.ShapeDtypeStruct((1, D), F32)],
        in_specs=[row, vec, row, row], out_specs=[row, vec],
        compiler_params=_cp(("arbitrary",)),
    )(x, norm_w, dh, dx2)


def _rope_tables(positions):
    inv_freq = 10000.0 ** (-jnp.arange(0, 64, 2, dtype=F32) / 64)
    ang = positions.astype(F32)[:, None] * inv_freq[None, :]
    cos, sin = jnp.cos(ang), jnp.sin(ang)
    return jnp.tile(cos, (1, 4)), jnp.tile(jnp.concatenate([-sin, sin], axis=1), (1, 2))


def _local_step(x, positions, norm_w, lb_logits, hnw, fnw, target, w_in, w_a, w_b, w_out):
    cc, ss = _rope_tables(positions)
    lbv = jax.nn.sigmoid(lb_logits[0:1] - lb_logits[1:2])
    h = _rmsnorm_in(x, norm_w)
    z = _in_proj(h, w_in)
    o, o_a, states = _hgrn_fwd(z, lbv, hnw)
    ob, lse, o_bg = _attn_fwd(z, cc, ss)
    dx2, dx2b, dgp, do_a, do_bg, merged, dy_a, dy_b, tail_small = _tail(x, o_a, o_bg, z, target, w_a, w_b, w_out, fnw)
    g_out, gb_out = _tn_matmul(merged, dx2b, "grad_w_out")
    g_a, gb_a = _tn_matmul(o_a, dy_a, "grad_w_a")
    g_b, gb_b = _tn_matmul(o_bg, dy_b, "grad_w_b")
    dhq, dhf, dhi, dhg, glb, ghn = _hgrn_bwd(z, o, do_a, states, lbv, hnw)
    dq, dk, dv, dag = _attn_bwd(z, cc, ss, ob, lse, do_bg)
    dz = jnp.concatenate([dhq, dhf, dhi, dhg, dq[0], dq[1], dq[2], dk[0], dk[1], dk[2], dv[0], dv[1], dv[2], dag, dgp],
                         axis=1)
    dh, g_in, gb_in = _in_proj_bwd(dz, h, w_in)
    grad_x, gnw = _grad_x(x, norm_w, dh, dx2)
    ghn_row = jnp.pad(jnp.sum(ghn, axis=0), ((0, 0), (0, D - 128)))
    small = jnp.concatenate([gnw, glb, ghn_row, tail_small[0:2], jnp.zeros((3, D), F32)], axis=0)
    return grad_x, (g_in, g_a, g_b, g_out), (gb_in, gb_a, gb_b, gb_out), small


def kernel(x, positions, norm_w, w_in, lb_logits, hgrn_norm_w, w_branch_a, w_branch_b, w_out, final_norm_w, loss_target, m_norm_w, m_w_in, m_lb_logits, m_hgrn_norm_w, m_w_branch_a, m_w_branch_b, m_w_out, m_final_norm_w, v_norm_w, v_w_in, v_lb_logits, v_hgrn_norm_w, v_w_branch_a, v_w_branch_b, v_w_out, v_final_norm_w):
    ix, iy, ic = _mesh_pos()
    core = jnp.reshape(ic, (1,)).astype(jnp.int32)
    pos = jnp.stack([4 * ix + 2 * iy + ic, 2 * ix + iy]).astype(jnp.int32)

    shards = [w_in[0], w_branch_a[0], w_branch_b[0], w_out[0]]
    moments_m = [m_w_in[0], m_w_branch_a[0], m_w_branch_b[0], m_w_out[0]]
    moments_v = [v_w_in[0], v_w_branch_a[0], v_w_branch_b[0], v_w_out[0]]
    names = ("w_in", "w_a", "w_b", "w_out")
    full = _allgather_weights([_cast_bf16(w, f"cast_{nm}") for w, nm in zip(shards, names)])

    fnw2 = final_norm_w.reshape(1, D)
    grad_x, grads, gb, small = _local_step(x[0], positions[0], norm_w, lb_logits, hgrn_norm_w, fnw2,
                                           loss_target[0], *full)

    r1 = _exchange_sibling(gb)
    pb = [_chip_partials(a, grads[a], r1[a], core) for a in range(4)]
    *r2, gathered = _exchange_chips(pb, small)
    big = [_reduce_and_update(a, shards[a], moments_m[a], moments_v[a], grads[a], r1[a], r2[a], pos)
           for a in range(4)]
    sm = _small_update(gathered, norm_w, lb_logits, hgrn_norm_w, fnw2,
                       (m_norm_w, m_lb_logits, m_hgrn_norm_w, m_final_norm_w.reshape(1, D),
                        v_norm_w, v_lb_logits, v_hgrn_norm_w, v_final_norm_w.reshape(1, D)))
    loss = sm[0][0, 0]
    outs = [loss, grad_x[None]]
    for kind in range(4):
        s_nw, s_lb, s_hn, s_fn = sm[1 + 4 * kind:5 + 4 * kind]
        outs += [s_nw, big[0][kind][None], s_lb, s_hn, big[1][kind][None], big[2][kind][None],
                 big[3][kind][None], s_fn.reshape(D)]
    return tuple(outs)
```

```python
import functools

import jax
import jax.numpy as jnp
from jax import lax
from jax.experimental import pallas as pl
from jax.experimental.pallas import tpu as pltpu

F32 = jnp.float32
BF16 = jnp.bfloat16
MESH = pl.DeviceIdType.MESH

S = 2048
D = 1024
NDEV = 8
HEADS = 8
CHUNK = 64
SUB = 16
HBLK = 128
ATT_PAD = 128
ATT_UNROLL = 4
EXP_CLAMP = 80.0
EPS = 1e-6
IN_COLS = 11264
SHARD_COLS = IN_COLS // NDEV
ATT_DILS = (1, 4, 16)
ATT_SCALE = 64 ** -0.5
LANES = 128

ADAM_LR, ADAM_B1, ADAM_B2, ADAM_EPS, ADAM_WD, ADAM_STEP = 0.001, 0.9, 0.999, 1e-08, 0.01, 10

VMEM_LIMIT = 56 * 1024 * 1024


def _cp(sem=None, **kw):
    return pltpu.CompilerParams(dimension_semantics=sem, vmem_limit_bytes=VMEM_LIMIT, **kw)


def _dot(a, b):
    return jnp.dot(a, b, preferred_element_type=F32)


def _dot_nt(a, b):
    return lax.dot_general(a, b, (((1,), (1,)), ((), ())), preferred_element_type=F32)


def _dot_tn(a, b):
    return lax.dot_general(a, b, (((0,), (0,)), ((), ())), preferred_element_type=F32)


def _split2(x):
    hi = x.astype(BF16)
    lo = (x - hi.astype(F32)).astype(BF16)
    return hi, lo


def _split3(x):
    hi = x.astype(BF16)
    r = x - hi.astype(F32)
    mid = r.astype(BF16)
    lo = (r - mid.astype(F32)).astype(BF16)
    return hi, mid, lo


def _dot_ones(ones_bf16, x):
    hi, mid, lo = _split3(x)
    return _dot(ones_bf16, hi) + _dot(ones_bf16, mid) + _dot(ones_bf16, lo)


def _dot3(dotfn, a, b):
    ah, al = _split2(a)
    bh, bl = _split2(b)
    return dotfn(ah, bh) + dotfn(ah, bl) + dotfn(al, bh)


def _silu(x):
    return x * jax.nn.sigmoid(x)


def _dsilu(x):
    s = jax.nn.sigmoid(x)
    return s * (1.0 + x * (1.0 - s))


def _mesh_pos():
    return lax.axis_index("x"), lax.axis_index("y"), lax.axis_index("c")


def _shard_of(ref, a, d):
    if a == 0:
        return ref.at[:, pl.ds(pl.multiple_of(d * SHARD_COLS, LANES), SHARD_COLS)]
    if a == 2:
        return ref.at[:, pl.ds(pl.multiple_of(d * LANES, LANES), LANES)]
    return ref.at[pl.ds(pl.multiple_of(d * 128, 128), 128), :]


FULL_SHAPES = ((D, IN_COLS), (D, D), (512, D), (D, D))
SHARD_SHAPES = ((D, SHARD_COLS), (128, D), (512, 128), (128, D))


def _allgather_weights(shards):
    n = len(shards)

    def body(*refs):
        ins, outs = refs[:n], refs[n:2 * n]
        send_sems, recv_sems, local_sems = refs[2 * n:]
        x, y, c = _mesh_pos()
        me, sibling = (x, y, c), (x, y, 1 - c)
        chips = [(1 - x, y), (x, 1 - y), (1 - x, 1 - y)]

        def blk(a, p):
            return _shard_of(outs[a], a, 4 * p[0] + 2 * p[1] + p[2])

        def copy(a, k, block, to, src=None):
            return pltpu.make_async_remote_copy(
                src_ref=blk(a, block) if src is None else src, dst_ref=blk(a, block),
                send_sem=send_sems.at[a * 7 + k], recv_sem=recv_sems.at[a * 7 + k],
                device_id=to, device_id_type=MESH)

        mine = [pltpu.make_async_copy(ins[a], blk(a, me), local_sems.at[a]) for a in range(n)]
        for cp in mine:
            cp.start()
        first = []
        for a in range(n):
            first += [copy(a, 1 + j, me, (*chip, c), src=ins[a]) for j, chip in enumerate(chips)]
        for a in range(n):
            first.append(copy(a, 0, me, sibling, src=ins[a]))
        for cp in first:
            cp.start()
        passed = []
        for j, chip in enumerate(chips):
            for a in range(n):
                copy(a, 1 + j, (*chip, c), me).wait_recv()
                fwd = copy(a, 4 + j, (*chip, c), sibling)
                fwd.start()
                passed.append(fwd)
        for a in range(n):
            copy(a, 0, sibling, me).wait_recv()
        for j, chip in enumerate(chips):
            for a in range(n):
                copy(a, 4 + j, (*chip, 1 - c), me).wait_recv()
        for cp in first + passed:
            cp.wait_send()
        for cp in mine:
            cp.wait()

    any_spec = pl.BlockSpec(memory_space=pl.ANY)
    return pl.pallas_call(
        body, name="allgather_weights",
        out_shape=[jax.ShapeDtypeStruct(FULL_SHAPES[a], BF16) for a in range(n)],
        in_specs=[any_spec] * n, out_specs=[any_spec] * n,
        scratch_shapes=[pltpu.SemaphoreType.DMA((7 * n,)), pltpu.SemaphoreType.DMA((7 * n,)),
                        pltpu.SemaphoreType.DMA((n,))],
    )(*shards)


def _exchange_sibling(ids, gb):
    n = len(gb)

    def body(*refs):
        ins, outs = refs[:n], refs[n:2 * n]
        send_sems, recv_sems = refs[2 * n:]
        x, y, c = _mesh_pos()
        sibling = (x, y, 1 - c)
        copies = []
        for i, a in enumerate(ids):
            for q in range(4):
                copies.append(pltpu.make_async_remote_copy(
                    src_ref=_shard_of(ins[i], a, 2 * q + (1 - c)), dst_ref=outs[i].at[q],
                    send_sem=send_sems.at[i * 4 + q], recv_sem=recv_sems.at[i * 4 + q],
                    device_id=sibling, device_id_type=MESH))
        for cp in copies:
            cp.start()
        for cp in copies:
            cp.wait()

    any_spec = pl.BlockSpec(memory_space=pl.ANY)
    return pl.pallas_call(
        body, name="grads_to_sibling",
        out_shape=[jax.ShapeDtypeStruct((4,) + SHARD_SHAPES[a], BF16) for a in ids],
        in_specs=[any_spec] * n, out_specs=[any_spec] * n,
        scratch_shapes=[pltpu.SemaphoreType.DMA((4 * n,)), pltpu.SemaphoreType.DMA((4 * n,))],
    )(*gb)


def _exchange_chips(ids, pb, small):
    n = len(pb)

    def body(*refs):
        ins, small_ref = refs[:n], refs[n]
        outs, small_out = refs[n + 1:2 * n + 1], refs[2 * n + 1]
        send_sems, recv_sems, ssend, srecv, local_sem = refs[2 * n + 2:]
        x, y, c = _mesh_pos()
        chips = [(1 - x, y), (x, 1 - y), (1 - x, 1 - y)]
        me = 4 * x + 2 * y + c
        copies = []
        for a in range(n):
            for k, chip in enumerate(chips):
                copies.append(pltpu.make_async_remote_copy(
                    src_ref=ins[a].at[2 * chip[0] + chip[1]], dst_ref=outs[a].at[k],
                    send_sem=send_sems.at[a * 3 + k], recv_sem=recv_sems.at[a * 3 + k],
                    device_id=(*chip, c), device_id_type=MESH))
        for r in range(1, NDEV):
            peer = (1 - x if r & 4 else x, 1 - y if r & 2 else y, 1 - c if r & 1 else c)
            copies.append(pltpu.make_async_remote_copy(
                src_ref=small_ref, dst_ref=small_out.at[me],
                send_sem=ssend.at[r - 1], recv_sem=srecv.at[r - 1],
                device_id=peer, device_id_type=MESH))
        own = pltpu.make_async_copy(small_ref, small_out.at[me], local_sem)
        own.start()
        for cp in copies:
            cp.start()
        for cp in copies:
            cp.wait()
        own.wait()

    any_spec = pl.BlockSpec(memory_space=pl.ANY)
    return pl.pallas_call(
        body, name="grads_between_chips",
        out_shape=[jax.ShapeDtypeStruct((3,) + SHARD_SHAPES[a], BF16) for a in ids]
        + [jax.ShapeDtypeStruct((NDEV,) + small.shape, F32)],
        in_specs=[any_spec] * (n + 1), out_specs=[any_spec] * (n + 1),
        scratch_shapes=[pltpu.SemaphoreType.DMA((3 * n,)), pltpu.SemaphoreType.DMA((3 * n,)),
                        pltpu.SemaphoreType.DMA((NDEV - 1,)), pltpu.SemaphoreType.DMA((NDEV - 1,)),
                        pltpu.SemaphoreType.DMA],
    )(*pb, small)


def _shard_tiles(a):
    rows, cols = SHARD_SHAPES[a]
    tr = min(rows, 256)
    return (tr, cols), rows // tr


def _full_index(a, d, i):
    (tr, _), nt = _shard_tiles(a)
    if a in (0, 2):
        return (i, d)
    return (d * nt + i, 0)


def _cast_bf16(x, name):
    rows, cols = x.shape
    tr = min(rows, 256)

    def body(x_ref, o_ref):
        o_ref[...] = x_ref[...].astype(BF16)

    return pl.pallas_call(
        body, name=name, out_shape=jax.ShapeDtypeStruct(x.shape, BF16), grid=(rows // tr,),
        in_specs=[pl.BlockSpec((tr, cols), lambda i: (i, 0))],
        out_specs=pl.BlockSpec((tr, cols), lambda i: (i, 0)),
        compiler_params=_cp(("parallel",)),
    )(x)


def _chip_partials(a, g_full, r1, core):
    tile, nt = _shard_tiles(a)

    def body(c_ref, g_ref, r_ref, o_ref):
        o_ref[0] = (g_ref[...] + r_ref[0].astype(F32)).astype(BF16)

    grid_spec = pltpu.PrefetchScalarGridSpec(
        num_scalar_prefetch=1, grid=(4, nt),
        in_specs=[pl.BlockSpec(tile, lambda q, i, c: _full_index(a, 2 * q + c[0], i)),
                  pl.BlockSpec((1,) + tile, lambda q, i, c: (q, i, 0))],
        out_specs=pl.BlockSpec((1,) + tile, lambda q, i, c: (q, i, 0)))
    return pl.pallas_call(
        body, name=f"chip_partials_{a}", grid_spec=grid_spec,
        out_shape=jax.ShapeDtypeStruct((4,) + SHARD_SHAPES[a], BF16),
        compiler_params=_cp(("parallel", "parallel")),
    )(core, g_full, r1)


def _adam(w, g, m, v):
    m = ADAM_B1 * m + (1.0 - ADAM_B1) * g
    v = ADAM_B2 * v + (1.0 - ADAM_B2) * (g * g)
    m_hat = m / (1.0 - ADAM_B1 ** ADAM_STEP)
    v_hat = v / (1.0 - ADAM_B2 ** ADAM_STEP)
    delta = -ADAM_LR * (m_hat / (jnp.sqrt(v_hat) + ADAM_EPS) + ADAM_WD * w)
    return delta, m, v


def _reduce_and_update(a, w, m, v, g_full, r1, r2, pos):
    tile, nt = _shard_tiles(a)

    def body(p_ref, w_ref, m_ref, v_ref, g_ref, r1_ref, r2_ref, go_ref, do_ref, mo_ref, vo_ref):
        g = g_ref[...] + r1_ref[0].astype(F32)
        g = g + r2_ref[0].astype(F32)
        g = g + r2_ref[1].astype(F32)
        g = g + r2_ref[2].astype(F32)
        delta, m_new, v_new = _adam(w_ref[...], g, m_ref[...], v_ref[...])
        go_ref[...] = g
        do_ref[...] = delta
        mo_ref[...] = m_new
        vo_ref[...] = v_new

    own = pl.BlockSpec(tile, lambda i, p: (i, 0))
    grid_spec = pltpu.PrefetchScalarGridSpec(
        num_scalar_prefetch=1, grid=(nt,),
        in_specs=[own, own, own,
                  pl.BlockSpec(tile, lambda i, p: _full_index(a, p[0], i)),
                  pl.BlockSpec((1,) + tile, lambda i, p: (p[1], i, 0)),
                  pl.BlockSpec((3,) + tile, lambda i, p: (0, i, 0))],
        out_specs=[own] * 4)
    shp = jax.ShapeDtypeStruct(w.shape, F32)
    return pl.pallas_call(
        body, name=f"reduce_update_{a}", grid_spec=grid_spec, out_shape=[shp] * 4,
        compiler_params=_cp(("parallel",)),
    )(pos, w, m, v, g_full, r1, r2)


def _reduce_own_and_update(w, m, v, g_chip, r2):
    tile, nt = _shard_tiles(0)

    def body(w_ref, m_ref, v_ref, g_ref, r2_ref, go_ref, do_ref, mo_ref, vo_ref):
        g = g_ref[...] + r2_ref[0].astype(F32)
        g = g + r2_ref[1].astype(F32)
        g = g + r2_ref[2].astype(F32)
        delta, m_new, v_new = _adam(w_ref[...], g, m_ref[...], v_ref[...])
        go_ref[...] = g
        do_ref[...] = delta
        mo_ref[...] = m_new
        vo_ref[...] = v_new

    own = pl.BlockSpec(tile, lambda i: (i, 0))
    shp = jax.ShapeDtypeStruct(w.shape, F32)
    return pl.pallas_call(
        body, name="reduce_update_0", grid=(nt,), out_shape=[shp] * 4,
        in_specs=[own, own, own, own, pl.BlockSpec((3,) + tile, lambda i: (0, i, 0))], out_specs=[own] * 4,
        compiler_params=_cp(("parallel",)),
    )(w, m, v, g_chip, r2)


def _small_update(gathered, norm_w, lb_logits, hnw, fnw, moments):
    m_nw, m_lb, m_hn, m_fn, v_nw, v_lb, v_hn, v_fn = moments

    def body(g_ref, nw, lb, hn, fn, mnw, mlb, mhn, mfn, vnw, vlb, vhn, vfn,
             loss_o, g_nw, g_lb, g_hn, g_fn, d_nw, d_lb, d_hn, d_fn,
             mo_nw, mo_lb, mo_hn, mo_fn, vo_nw, vo_lb, vo_hn, vo_fn):
        tot = g_ref[0]
        for d in range(1, NDEV):
            tot = tot + g_ref[d]
        loss_o[...] = tot[4:5, 0:LANES]
        logits = lb[...]
        lbv = jax.nn.sigmoid(logits[0:1] - logits[1:2])
        chain = tot[1:2] * lbv * (1.0 - lbv)
        grads = (tot[0:1], jnp.concatenate([chain, -chain], axis=0), tot[2:3, 0:LANES], tot[3:4])
        outs = ((nw, mnw, vnw, g_nw, d_nw, mo_nw, vo_nw), (lb, mlb, vlb, g_lb, d_lb, mo_lb, vo_lb),
                (hn, mhn, vhn, g_hn, d_hn, mo_hn, vo_hn), (fn, mfn, vfn, g_fn, d_fn, mo_fn, vo_fn))
        for g, (w_r, m_r, v_r, g_o, d_o, m_o, v_o) in zip(grads, outs):
            delta, m_new, v_new = _adam(w_r[...], g, m_r[...], v_r[...])
            g_o[...] = g
            d_o[...] = delta
            m_o[...] = m_new
            v_o[...] = v_new

    shapes = [norm_w.shape, lb_logits.shape, hnw.shape, fnw.shape]
    out_shape = [jax.ShapeDtypeStruct((1, LANES), F32)] + [jax.ShapeDtypeStruct(s, F32) for s in shapes] * 4
    return pl.pallas_call(body, name="small_update", out_shape=out_shape, compiler_params=_cp())(
        gathered, norm_w, lb_logits, hnw, fnw, m_nw, m_lb, m_hn, m_fn, v_nw, v_lb, v_hn, v_fn)


def _rmsnorm_in(x, norm_w):
    tr = 512

    def body(x_ref, w_ref, h_ref):
        xv = x_ref[...]
        r = lax.rsqrt(jnp.mean(xv * xv, axis=-1, keepdims=True) + EPS)
        h_ref[...] = (xv * r * w_ref[...]).astype(BF16)

    return pl.pallas_call(
        body, name="rmsnorm_in", out_shape=jax.ShapeDtypeStruct((S, D), BF16), grid=(S // tr,),
        in_specs=[pl.BlockSpec((tr, D), lambda i: (i, 0)), pl.BlockSpec((1, D), lambda i: (0, 0))],
        out_specs=pl.BlockSpec((tr, D), lambda i: (i, 0)),
        compiler_params=_cp(("parallel",)),
    )(x, norm_w)


def _in_proj(h, w_in):
    tn = 1024

    def body(h_ref, w_ref, z_ref):
        z_ref[...] = _dot(h_ref[...], w_ref[...])

    return pl.pallas_call(
        body, name="in_proj", out_shape=jax.ShapeDtypeStruct((S, IN_COLS), F32), grid=(IN_COLS // tn,),
        in_specs=[pl.BlockSpec((S, D), lambda j: (0, 0)), pl.BlockSpec((D, tn), lambda j: (0, j))],
        out_specs=pl.BlockSpec((S, tn), lambda j: (0, j)),
        compiler_params=_cp(("parallel",)),
    )(h, w_in)


def _block_tri(n, block, upper=False):
    r = lax.broadcasted_iota(jnp.int32, (n, n), 0)
    c = lax.broadcasted_iota(jnp.int32, (n, n), 1)
    keep = (c >= r) if upper else (c <= r)
    return jnp.where(keep & ((r // block) == (c // block)), 1.0, 0.0).astype(BF16)


def _tril_mask(n):
    r = lax.broadcasted_iota(jnp.int32, (n, n), 0)
    c = lax.broadcasted_iota(jnp.int32, (n, n), 1)
    return c <= r


def _chunk_scores(q, k, b, bex, r0, mask):
    rows = slice(r0, r0 + CHUNK)
    parts, qs_l, ek_l, eq_l = [], [], [], []
    for i in range(CHUNK // SUB):
        ri = slice(r0 + SUB * i, r0 + SUB * (i + 1))
        base = bex[r0 + SUB * i:r0 + SUB * i + 1]
        eq = jnp.exp(b[ri] - base)
        ek = jnp.exp(jnp.minimum(base - b[rows], EXP_CLAMP))
        qs = q[ri] * eq
        parts.append(_dot_nt(qs.astype(BF16), (k[rows] * ek).astype(BF16)))
        qs_l.append(qs)
        ek_l.append(ek)
        eq_l.append(eq)
    return jnp.where(mask, jnp.concatenate(parts, axis=0), 0.0), qs_l, ek_l, eq_l


def _hgrn_cols(hq, hf, hi, lb):
    sg = jax.nn.sigmoid(hf)
    f = lb + (1.0 - lb) * sg
    g = jnp.log(f)
    b = _dot_ones(_block_tri(HBLK, CHUNK), g)
    return _silu(hq), 1.0 - f, g, hi, sg, f, b


def _hgrn_fwd(z, lbv, hnw):
    ntb, nch = S // HBLK, HBLK // CHUNK

    def body(hq_ref, hf_ref, hi_ref, hg_ref, lb_ref, hnw_ref, o_ref, oa_ref, st_ref, state):
        @pl.when(pl.program_id(0) == 0)
        def _():
            state[...] = jnp.zeros_like(state)

        q_a, k_a, g_a, v_a, _, _, b_a = _hgrn_cols(hq_ref[...], hf_ref[...], hi_ref[...], lb_ref[...])
        bex_a = b_a - g_a
        eb_a = jnp.exp(b_a)
        mask = _tril_mask(CHUNK)
        hg = hg_ref[...]
        w = hnw_ref[...]
        for h in range(HEADS):
            cols = slice(128 * h, 128 * h + 128)
            q, k, v, b, bex, eb = q_a[:, cols], k_a[:, cols], v_a[:, cols], b_a[:, cols], bex_a[:, cols], eb_a[:, cols]
            st = state[h]
            outs = []
            for c in range(nch):
                r0 = c * CHUNK
                rows = slice(r0, r0 + CHUNK)
                a, _, _, _ = _chunk_scores(q, k, b, bex, r0, mask)
                vb = v[rows].astype(BF16)
                b_last = b[r0 + CHUNK - 1:r0 + CHUNK]
                qe = (q[rows] * eb[rows]).astype(BF16)
                outs.append(_dot(a.astype(BF16), vb) + _dot_nt(qe, st.astype(BF16)))
                st_ref[h, c] = st
                ke = (k[rows] * jnp.exp(b_last - b[rows])).astype(BF16)
                st = st * jnp.exp(b_last) + _dot_tn(vb, ke)
            state[h] = st
            o = jnp.concatenate(outs, axis=0)
            o_ref[:, cols] = o
            r = lax.rsqrt(jnp.mean(o * o, axis=-1, keepdims=True) + EPS)
            oa_ref[:, cols] = (o * r * w * _silu(hg[:, cols])).astype(BF16)

    def zcol(j):
        return pl.BlockSpec((HBLK, D), lambda t: (t, j))

    out_blk = pl.BlockSpec((HBLK, D), lambda t: (t, 0))
    return pl.pallas_call(
        body, name="hgrn_fwd", grid=(ntb,),
        out_shape=[jax.ShapeDtypeStruct((S, D), F32), jax.ShapeDtypeStruct((S, D), BF16),
                   jax.ShapeDtypeStruct((HEADS, S // CHUNK, 128, 128), F32)],
        in_specs=[zcol(0), zcol(1), zcol(2), zcol(3),
                  pl.BlockSpec((1, D), lambda t: (0, 0)), pl.BlockSpec((1, 128), lambda t: (0, 0))],
        out_specs=[out_blk, out_blk, pl.BlockSpec((HEADS, nch, 128, 128), lambda t: (0, t, 0, 0))],
        scratch_shapes=[pltpu.VMEM((HEADS, 128, 128), F32)],
        compiler_params=_cp(("arbitrary",)),
    )(z, z, z, z, lbv, hnw)


def _half_mask():
    lane = lax.broadcasted_iota(jnp.int32, (1, LANES), 1)
    return (lane % 64) < 32


def _rope(t, cc, ss, first_half):
    partner = jnp.where(first_half, pltpu.roll(t, 96, 1), pltpu.roll(t, 32, 1))
    return t * cc + partner * ss


def _attn_masks():
    i = lax.broadcasted_iota(jnp.int32, (128, 128), 0)
    j = lax.broadcasted_iota(jnp.int32, (128, 128), 1)
    return j >= i, j <= i


def _to_residues_dyn(g, dst, src, row0=0, dtype=None):
    for gi, dil in enumerate((1, 4, 16)):
        m = S // dil

        @pl.when(g == gi)
        def _(dil=dil, m=m):
            for r in range(dil):
                v = src[...] if dil == 1 else src[pl.ds(r, m, stride=dil), :]
                if dtype is not None:
                    v = v.astype(dtype)
                dst[row0 + r * m:row0 + (r + 1) * m, 0:LANES] = v


def _from_residues_dyn(g, dst, src, row0=0):
    for gi, dil in enumerate((1, 4, 16)):
        m = S // dil

        @pl.when(g == gi)
        def _(dil=dil, m=m):
            for r in range(dil):
                v = src[row0 + r * m:row0 + (r + 1) * m, :]
                if dil == 1:
                    dst[...] = v
                else:
                    dst[pl.ds(r, m, stride=dil), :] = v


def _group_blocks(g):
    return jnp.where(g == 0, 16, jnp.where(g == 1, 4, 1))


def _attn_in_specs(extra):
    def zcol(off):
        return pl.BlockSpec((S, LANES), lambda p, g: (0, off + 4 * g + p))

    per_pair = pl.BlockSpec((S, LANES), lambda p, g: (0, p))
    const = pl.BlockSpec((S, LANES), lambda p, g: (0, 0))
    return [zcol(32), zcol(44), zcol(56), pl.BlockSpec((S, LANES), lambda p, g: (0, 68 + p)), const, const] + [per_pair] * extra


def _attn_fwd(z, cc, ss):
    def body(q_ref, k_ref, v_ref, ag_ref, cc_ref, ss_ref, ob_ref, lse_ref, obg_ref,
             tmp, qs, ks, vx, og, mg, lg, o_t, m_t, l_t, o_acc, m_acc, l_acc):
        g = pl.program_id(1)
        first_half = _half_mask()
        prev_ok, cur_ok = _attn_masks()
        lane = lax.broadcasted_iota(jnp.int32, (1, LANES), 1)
        heads = (lane < 64, lane >= 64)
        nblk = _group_blocks(g)

        @pl.when(g == 0)
        def _():
            ks[0:ATT_PAD, :] = jnp.zeros((ATT_PAD, LANES), BF16)
            vx[0:ATT_PAD, 0:LANES] = jnp.zeros((ATT_PAD, LANES), BF16)
            vx[:, LANES:2 * LANES] = jnp.ones((ATT_PAD + S, LANES), BF16)

        tmp[...] = _rope(q_ref[...], cc_ref[...], ss_ref[...], first_half) * ATT_SCALE
        _to_residues_dyn(g, qs, tmp)
        tmp[...] = _rope(k_ref[...], cc_ref[...], ss_ref[...], first_half)
        _to_residues_dyn(g, ks, tmp, ATT_PAD, BF16)
        _to_residues_dyn(g, vx, v_ref, ATT_PAD, BF16)

        def unit(u, carry):
            start = pl.multiple_of(u * 128, 128)
            cur = pl.ds(start, 128)
            pm = prev_ok & ((u & (nblk - 1)) != 0)
            qu = qs[cur, :]
            kcat = ks[pl.ds(start, 256), :]
            vext = vx[pl.ds(start, 256), :]
            o_u = m_u = l_u = None
            for hh in range(2):
                s = _dot_nt(jnp.where(heads[hh], qu, 0.0).astype(BF16), kcat)
                sp = jnp.where(pm, s[:, 0:128], -jnp.inf)
                sc = jnp.where(cur_ok, s[:, 128:256], -jnp.inf)
                m = jnp.max(jnp.maximum(sp, sc), axis=-1, keepdims=True)
                p = jnp.concatenate([jnp.exp(sp - m), jnp.exp(sc - m)], axis=1).astype(BF16)
                ol = _dot(p, vext)
                mb = jnp.broadcast_to(m, (128, LANES))
                if hh == 0:
                    o_u, l_u, m_u = ol[:, 0:128], ol[:, 128:256], mb
                else:
                    o_u = jnp.where(heads[1], ol[:, 0:128], o_u)
                    l_u = jnp.where(heads[1], ol[:, 128:256], l_u)
                    m_u = jnp.where(heads[1], mb, m_u)
            og[cur, :] = o_u
            mg[cur, :] = m_u
            lg[cur, :] = l_u
            return carry

        lax.fori_loop(0, 16, unit, 0, unroll=ATT_UNROLL)
        _from_residues_dyn(g, o_t, og)
        _from_residues_dyn(g, m_t, mg)
        _from_residues_dyn(g, l_t, lg)

        @pl.when(g == 0)
        def _():
            o_acc[...] = o_t[...]
            m_acc[...] = m_t[...]
            l_acc[...] = l_t[...]

        @pl.when(g > 0)
        def _():
            m_new = jnp.maximum(m_acc[...], m_t[...])
            wa, wb = jnp.exp(m_acc[...] - m_new), jnp.exp(m_t[...] - m_new)
            o_acc[...] = o_acc[...] * wa + o_t[...] * wb
            l_acc[...] = l_acc[...] * wa + l_t[...] * wb
            m_acc[...] = m_new

        @pl.when(g == 2)
        def _():
            ob = o_acc[...] / l_acc[...]
            ob_ref[...] = ob
            lse_ref[...] = m_acc[...] + jnp.log(l_acc[...])
            obg_ref[...] = (ob * _silu(ag_ref[...])).astype(BF16)

    blk = pl.BlockSpec((S, LANES), lambda p, g: (0, p))
    buf = pltpu.VMEM((S, LANES), F32)
    return pl.pallas_call(
        body, name="attn_fwd", grid=(4, 3),
        out_shape=[jax.ShapeDtypeStruct((S, 512), F32), jax.ShapeDtypeStruct((S, 512), F32),
                   jax.ShapeDtypeStruct((S, 512), BF16)],
        in_specs=_attn_in_specs(0), out_specs=[blk, blk, blk],
        scratch_shapes=[buf, buf, pltpu.VMEM((ATT_PAD + S, LANES), BF16), pltpu.VMEM((ATT_PAD + S, 2 * LANES), BF16)] + [buf] * 9,
        compiler_params=_cp(("parallel", "arbitrary")),
    )(z, z, z, z, cc, ss)


def _tail(x, o_a, o_bg, z, target, w_a, w_b, w_out, fnw):
    tm = 256

    def body(x_ref, oa_ref, ob_ref, gpa_ref, gpb_ref, t_ref, wa_ref, wb_ref, wo_ref, fnw_ref,
             dx2_ref, dx2b_ref, dgp_ref, doa_ref, dob_ref, mg_ref, dya_ref, dyb_ref, small_ref):
        @pl.when(pl.program_id(0) == 0)
        def _():
            small_ref[...] = jnp.zeros_like(small_ref)

        wa, wb, wo = wa_ref[...], wb_ref[...], wo_ref[...]
        y_a = _dot(oa_ref[...], wa)
        y_b = _dot(ob_ref[...], wb)
        ga = jax.nn.sigmoid(gpa_ref[...])
        gb = jax.nn.sigmoid(gpb_ref[...])
        merged = (ga * y_a + gb * y_b).astype(BF16)
        x2 = x_ref[...] + _dot(merged, wo)
        r2 = lax.rsqrt(jnp.mean(x2 * x2, axis=-1, keepdims=True) + EPS)
        n2 = x2 * r2
        fw = fnw_ref[...]
        err = n2 * fw - t_ref[...]
        loss = 0.5 * jnp.sum(jnp.sum(err * err, axis=-1, keepdims=True), axis=0, keepdims=True) / D
        dy = err * (1.0 / D)
        g_fnw = jnp.sum(dy * n2, axis=0, keepdims=True)
        dn = dy * fw
        dx2 = r2 * (dn - n2 * jnp.mean(dn * n2, axis=-1, keepdims=True))
        dx2b = dx2.astype(BF16)
        dmerged = _dot_nt(dx2b, wo)
        dy_a = (dmerged * ga).astype(BF16)
        dy_b = (dmerged * gb).astype(BF16)
        dx2_ref[...] = dx2
        dx2b_ref[...] = dx2b
        dgp_ref[:, 0:D] = (dmerged * y_a * ga * (1.0 - ga)).astype(BF16)
        dgp_ref[:, D:2 * D] = (dmerged * y_b * gb * (1.0 - gb)).astype(BF16)
        doa_ref[...] = _dot_nt(dy_a, wa)
        dob_ref[...] = _dot_nt(dy_b, wb)
        mg_ref[...] = merged
        dya_ref[...] = dy_a
        dyb_ref[...] = dy_b
        small_ref[0:1, :] += g_fnw
        small_ref[1:2, :] += jnp.broadcast_to(loss, (1, D))

    def rows(cols, off=0):
        return pl.BlockSpec((tm, cols), lambda i: (i, off))

    def whole(shape):
        return pl.BlockSpec(shape, lambda i: (0, 0))

    return pl.pallas_call(
        body, name="tail", grid=(S // tm,),
        out_shape=[jax.ShapeDtypeStruct((S, D), F32), jax.ShapeDtypeStruct((S, D), BF16),
                   jax.ShapeDtypeStruct((S, 2 * D), BF16), jax.ShapeDtypeStruct((S, D), F32),
                   jax.ShapeDtypeStruct((S, 512), F32), jax.ShapeDtypeStruct((S, D), BF16),
                   jax.ShapeDtypeStruct((S, D), BF16), jax.ShapeDtypeStruct((S, D), BF16),
                   jax.ShapeDtypeStruct((8, D), F32)],
        in_specs=[rows(D), rows(D), rows(512), rows(D, 9), rows(D, 10), rows(D),
                  whole((D, D)), whole((512, D)), whole((D, D)), whole((1, D))],
        out_specs=[rows(D), rows(D), rows(2 * D), rows(D), rows(512), rows(D), rows(D), rows(D), whole((8, D))],
        compiler_params=_cp(("arbitrary",)),
    )(x, o_a, o_bg, z, z, target, w_a, w_b, w_out, fnw)


def _tn_matmul(a, b, name):
    m, n = a.shape[1], b.shape[1]
    tn = 512

    def body(a_ref, b_ref, o_ref, ob_ref):
        acc = _dot_tn(a_ref[...], b_ref[...])
        o_ref[...] = acc
        ob_ref[...] = acc.astype(BF16)

    out_blk = pl.BlockSpec((m, tn), lambda j: (0, j))
    return pl.pallas_call(
        body, name=name, grid=(n // tn,),
        out_shape=[jax.ShapeDtypeStruct((m, n), F32), jax.ShapeDtypeStruct((m, n), BF16)],
        in_specs=[pl.BlockSpec((S, m), lambda j: (0, 0)), pl.BlockSpec((S, tn), lambda j: (0, j))],
        out_specs=[out_blk, out_blk],
        compiler_params=_cp(("parallel",)),
    )(a, b)


def _hgrn_bwd(z, o, do_a, states, lbv, hnw):
    ntb, nch = S // HBLK, HBLK // CHUNK

    def body(hq_ref, hf_ref, hi_ref, hg_ref, o_ref, doa_ref, st_ref, lb_ref, hnw_ref,
             dhq_ref, dhf_ref, dhi_ref, dhg_ref, glb_ref, ghn_ref, dstate):
        @pl.when(pl.program_id(0) == 0)
        def _():
            dstate[...] = jnp.zeros_like(dstate)
            glb_ref[...] = jnp.zeros_like(glb_ref)
            ghn_ref[...] = jnp.zeros_like(ghn_ref)

        lb_a = lb_ref[...]
        hq_a, hg_a = hq_ref[...], hg_ref[...]
        q_a, k_a, g_a, v_a, sg_a, f_a, b_a = _hgrn_cols(hq_a, hf_ref[...], hi_ref[...], lb_a)
        bex_a = b_a - g_a
        eb_a = jnp.exp(b_a)
        w = hnw_ref[...]
        mask = _tril_mask(CHUNK)
        upper = _block_tri(CHUNK, CHUNK, upper=True)
        for h in range(HEADS):
            cols = slice(128 * h, 128 * h + 128)
            q, k, v, b, bex, eb = q_a[:, cols], k_a[:, cols], v_a[:, cols], b_a[:, cols], bex_a[:, cols], eb_a[:, cols]
            hq, hg, sg, f, lb = hq_a[:, cols], hg_a[:, cols], sg_a[:, cols], f_a[:, cols], lb_a[:, cols]
            ov, doa = o_ref[:, cols], doa_ref[:, cols]
            r = lax.rsqrt(jnp.mean(ov * ov, axis=-1, keepdims=True) + EPS)
            n = ov * r
            sil = _silu(hg)
            dhg_ref[:, cols] = (doa * n * w * _dsilu(hg)).astype(BF16)
            ghn_ref[h] += jnp.sum(doa * sil * n, axis=0, keepdims=True)
            dn = doa * sil * w
            do = r * (dn - n * jnp.mean(dn * n, axis=-1, keepdims=True))

            dst = dstate[h]
            dq_l, dk_l, dv_l, dg_l = [None] * nch, [None] * nch, [None] * nch, [None] * nch
            for c in reversed(range(nch)):
                r0 = c * CHUNK
                rows = slice(r0, r0 + CHUNK)
                st = st_ref[h, c]
                bc, kc, qc = b[rows], k[rows], q[rows]
                vb, dob = v[rows].astype(BF16), do[rows].astype(BF16)
                b_last = bc[CHUNK - 1:CHUNK]
                e_last = jnp.exp(b_last)
                ekl = jnp.exp(b_last - bc)
                dstb = dst.astype(BF16)
                a, qs_l, ek_l, eq_l = _chunk_scores(q, k, b, bex, r0, mask)
                da = jnp.where(mask, _dot_nt(dob, vb), 0.0)
                dv_l[c] = _dot_tn(a.astype(BF16), dob) + _dot_nt((kc * ekl).astype(BF16), dstb)
                dq_inter = _dot(dob, st.astype(BF16)) * eb[rows]
                dk_state = _dot(vb, dstb) * ekl
                dq_parts, dk_intra = [], jnp.zeros((CHUNK, 128), F32)
                for i in range(CHUNK // SUB):
                    da_i = da[SUB * i:SUB * (i + 1)]
                    dq_parts.append(_dot3(_dot, da_i, kc * ek_l[i]) * eq_l[i])
                    dk_intra = dk_intra + _dot3(_dot_tn, da_i, qs_l[i]) * ek_l[i]
                dq = jnp.concatenate(dq_parts, axis=0) + dq_inter
                dk = dk_intra + dk_state
                last = (e_last * jnp.sum(st * dst, axis=0, keepdims=True)
                        + jnp.sum(kc * dk_state, axis=0, keepdims=True))
                dg_l[c] = _dot_ones(upper, qc * dq - kc * dk) + last
                dq_l[c], dk_l[c] = dq, dk
                dst = dst * e_last + _dot_tn(dob, (qc * eb[rows]).astype(BF16))
            dstate[h] = dst
            dq, dk = jnp.concatenate(dq_l, axis=0), jnp.concatenate(dk_l, axis=0)
            dg, dv = jnp.concatenate(dg_l, axis=0), jnp.concatenate(dv_l, axis=0)
            dhq_ref[:, cols] = (dq * _dsilu(hq)).astype(BF16)
            dhi_ref[:, cols] = dv.astype(BF16)
            df = dg / f - dk
            dhf_ref[:, cols] = (df * (1.0 - lb) * sg * (1.0 - sg)).astype(BF16)
            glb_ref[:, cols] += jnp.sum(df * (1.0 - sg), axis=0, keepdims=True)

    def rev(t):
        return ntb - 1 - t

    def zcol(j):
        return pl.BlockSpec((HBLK, D), lambda t: (rev(t), j))

    blk = pl.BlockSpec((HBLK, D), lambda t: (rev(t), 0))
    return pl.pallas_call(
        body, name="hgrn_bwd", grid=(ntb,),
        out_shape=[jax.ShapeDtypeStruct((S, D), BF16)] * 4
        + [jax.ShapeDtypeStruct((1, D), F32), jax.ShapeDtypeStruct((HEADS, 1, 128), F32)],
        in_specs=[zcol(0), zcol(1), zcol(2), zcol(3), blk, blk,
                  pl.BlockSpec((HEADS, nch, 128, 128), lambda t: (0, rev(t), 0, 0)),
                  pl.BlockSpec((1, D), lambda t: (0, 0)), pl.BlockSpec((1, 128), lambda t: (0, 0))],
        out_specs=[blk] * 4 + [pl.BlockSpec((1, D), lambda t: (0, 0)),
                               pl.BlockSpec((HEADS, 1, 128), lambda t: (0, 0, 0))],
        scratch_shapes=[pltpu.VMEM((HEADS, 128, 128), F32)],
        compiler_params=_cp(("arbitrary",)),
    )(z, z, z, z, o, do_a, states, lbv, hnw)


def _attn_bwd(z, cc, ss, ob, lse, do_bg):
    def body(q_ref, k_ref, v_ref, ag_ref, cc_ref, ss_ref, ob_ref, lse_ref, dobg_ref,
             dq_ref, dk_ref, dv_ref, dag_ref,
             tmp, qs, ks, vs, dos, dqs, dks, dvs, dkp, dvp, do_t, ls0_t, ls1_t, dl0_t, dl1_t, ls0, ls1, dl0, dl1):
        g = pl.program_id(1)
        first_half = _half_mask()
        prev_ok, cur_ok = _attn_masks()
        lane = lax.broadcasted_iota(jnp.int32, (1, LANES), 1)
        heads = (lane < 64, lane >= 64)
        nblk = _group_blocks(g)
        cc_v, ss_v = cc_ref[...], ss_ref[...]

        @pl.when(g == 0)
        def _():
            ag, obv, dobg = ag_ref[...], ob_ref[...], dobg_ref[...]
            dag_ref[...] = (dobg * obv * _dsilu(ag)).astype(BF16)
            dob = dobg * _silu(ag)
            do_t[...] = dob
            prod = dob * obv
            dl = jnp.concatenate(
                [jnp.broadcast_to(jnp.sum(prod[:, 0:64], axis=-1, keepdims=True), (S, 64)),
                 jnp.broadcast_to(jnp.sum(prod[:, 64:128], axis=-1, keepdims=True), (S, 64))], axis=1)
            dl_sw = pltpu.roll(dl, 64, 1)
            dl0_t[...] = jnp.where(heads[0], dl, dl_sw)
            dl1_t[...] = jnp.where(heads[0], dl_sw, dl)
            ls = lse_ref[...]
            ls_sw = pltpu.roll(ls, 64, 1)
            ls0_t[...] = jnp.where(heads[0], ls, ls_sw)
            ls1_t[...] = jnp.where(heads[0], ls_sw, ls)
            ks[0:ATT_PAD, :] = jnp.zeros((ATT_PAD, LANES), BF16)
            vs[0:ATT_PAD, :] = jnp.zeros((ATT_PAD, LANES), BF16)

        tmp[...] = _rope(q_ref[...], cc_v, ss_v, first_half) * ATT_SCALE
        _to_residues_dyn(g, qs, tmp)
        tmp[...] = _rope(k_ref[...], cc_v, ss_v, first_half)
        _to_residues_dyn(g, ks, tmp, ATT_PAD, BF16)
        _to_residues_dyn(g, vs, v_ref, ATT_PAD, BF16)
        _to_residues_dyn(g, dos, do_t)
        _to_residues_dyn(g, ls0, ls0_t)
        _to_residues_dyn(g, ls1, ls1_t)
        _to_residues_dyn(g, dl0, dl0_t)
        _to_residues_dyn(g, dl1, dl1_t)
        lss, dls = (ls0, ls1), (dl0, dl1)

        def unit(u, carry):
            start = pl.multiple_of(u * 128, 128)
            cur = pl.ds(start, 128)
            both = pl.ds(start, 256)
            pm = prev_ok & ((u & (nblk - 1)) != 0)
            qu, dou = qs[cur, :], dos[cur, :]
            kcat, vcat = ks[both, :], vs[both, :]
            dq_u = None
            q_l, do_l, ds_l, p_l = [], [], [], []
            for hh in range(2):
                q_h = jnp.where(heads[hh], qu, 0.0).astype(BF16)
                do_h = jnp.where(heads[hh], dou, 0.0).astype(BF16)
                s = _dot_nt(q_h, kcat)
                dp = _dot_nt(do_h, vcat)
                lse_h, dl_h = lss[hh][cur, :], dls[hh][cur, :]
                pp = jnp.where(pm, jnp.exp(s[:, 0:128] - lse_h), 0.0)
                pc = jnp.where(cur_ok, jnp.exp(s[:, 128:256] - lse_h), 0.0)
                ds = jnp.concatenate([pp * (dp[:, 0:128] - dl_h), pc * (dp[:, 128:256] - dl_h)], axis=1).astype(BF16)
                dq = _dot(ds, kcat)
                dq_u = dq if hh == 0 else jnp.where(heads[1], dq, dq_u)
                q_l.append(q_h)
                do_l.append(do_h)
                ds_l.append(ds)
                p_l.append(jnp.concatenate([pp, pc], axis=1).astype(BF16))
            dkcat = _dot_tn(jnp.concatenate(ds_l, axis=0), jnp.concatenate(q_l, axis=0))
            dvcat = _dot_tn(jnp.concatenate(p_l, axis=0), jnp.concatenate(do_l, axis=0))
            dkp[cur, :] = dkcat[0:128]
            dks[cur, :] = dkcat[128:256]
            dvp[cur, :] = dvcat[0:128]
            dvs[cur, :] = dvcat[128:256]
            dqs[cur, :] = dq_u
            return carry

        lax.fori_loop(0, 16, unit, 0, unroll=ATT_UNROLL)
        dks[0:S - 128, :] += dkp[128:S, :]
        dvs[0:S - 128, :] += dvp[128:S, :]
        _from_residues_dyn(g, tmp, dqs)
        dq_ref[0] = (_rope(tmp[...], cc_v, -ss_v, first_half) * ATT_SCALE).astype(BF16)
        _from_residues_dyn(g, tmp, dks)
        dk_ref[0] = _rope(tmp[...], cc_v, -ss_v, first_half).astype(BF16)
        _from_residues_dyn(g, tmp, dvs)
        dv_ref[0] = tmp[...].astype(BF16)

    grp = pl.BlockSpec((1, S, LANES), lambda p, g: (g, 0, p))
    buf = pltpu.VMEM((S, LANES), F32)
    padded_b = pltpu.VMEM((ATT_PAD + S, LANES), BF16)
    return pl.pallas_call(
        body, name="attn_bwd", grid=(4, 3),
        out_shape=[jax.ShapeDtypeStruct((3, S, 512), BF16)] * 3 + [jax.ShapeDtypeStruct((S, 512), BF16)],
        in_specs=_attn_in_specs(3), out_specs=[grp, grp, grp, pl.BlockSpec((S, LANES), lambda p, g: (0, p))],
        scratch_shapes=[buf, buf, padded_b, padded_b] + [buf] * 15,
        compiler_params=_cp(("parallel", "arbitrary")),
    )(z, z, z, z, cc, ss, ob, lse, do_bg)


def _in_proj_bwd(dz, h, w_in):
    half = S // 2
    slab = (D, SHARD_COLS)

    def body(dz_hbm, h_hbm, w_hbm, dh_hbm, g_chip, r1_hbm, r2_hbm,
             h_buf, dz_buf, stage_d, r1_buf, stage_i, acc,
             dz_sem, w_sem, h_sem, r1_sem, out_sem, send_d, recv_d, send_i, recv_i):
        x, y, c = _mesh_pos()
        sibling = (x, y, 1 - c)
        chips = [(1 - x, y), (x, 1 - y), (1 - x, 1 - y), (x, y)]

        def cols(d):
            return pl.ds(pl.multiple_of(d * SHARD_COLS, LANES), SHARD_COLS)

        blocks = []
        for q in chips:
            blocks += [4 * q[0] + 2 * q[1] + (1 - c), 4 * q[0] + 2 * q[1] + c]

        def dz_tile(t):
            return pltpu.make_async_copy(dz_hbm.at[pl.ds((t % 2) * half, half), cols(blocks[t // 2])],
                                         dz_buf.at[t % 2], dz_sem.at[t % 2])

        def to_sibling(i):
            return pltpu.make_async_remote_copy(
                src_ref=stage_d.at[i % 2], dst_ref=r1_hbm.at[i], send_sem=send_d.at[i], recv_sem=recv_d.at[i],
                device_id=sibling, device_id_type=MESH)

        def to_owner(i):
            return pltpu.make_async_remote_copy(
                src_ref=stage_i.at[i], dst_ref=r2_hbm.at[i], send_sem=send_i.at[i], recv_sem=recv_i.at[i],
                device_id=(*chips[i], c), device_id_type=MESH)

        h_copy = pltpu.make_async_copy(h_hbm, h_buf, h_sem)
        h_copy.start()
        dz_tile(0).start()
        h_copy.wait()
        for b in range(8):
            i = b // 2
            if b % 2 == 1:
                to_sibling(i).wait_recv()
                r1_copy = pltpu.make_async_copy(r1_hbm.at[i], r1_buf, r1_sem)
                r1_copy.start()
            g = None
            for r in range(2):
                t = 2 * b + r
                if t + 1 < 16:
                    dz_tile(t + 1).start()
                dz_tile(t).wait()
                part = _dot_tn(h_buf[r * half:(r + 1) * half, :], dz_buf[t % 2])
                g = part if g is None else g + part
            if b % 2 == 0:
                if i >= 2:
                    to_sibling(i - 2).wait_send()
                stage_d[i % 2] = g.astype(BF16)
                to_sibling(i).start()
            else:
                r1_copy.wait()
                g = g + r1_buf[...].astype(F32)
                if i < 3:
                    stage_i[i] = g.astype(BF16)
                    to_owner(i).start()
                else:
                    g_chip[...] = g
        to_sibling(2).wait_send()
        to_sibling(3).wait_send()

        def dz2(t):
            return pltpu.make_async_copy(
                dz_hbm.at[pl.ds((t // 8) * half, half), pl.ds((t % 8) * SHARD_COLS, SHARD_COLS)],
                dz_buf.at[t % 2], dz_sem.at[t % 2])

        def w2(t):
            return pltpu.make_async_copy(w_hbm.at[:, pl.ds((t % 8) * SHARD_COLS, SHARD_COLS)],
                                         stage_d.at[t % 2], w_sem.at[t % 2])

        def dh_out(r):
            return pltpu.make_async_copy(acc, dh_hbm.at[pl.ds(r * half, half), :], out_sem)

        dz2(0).start()
        w2(0).start()
        for t in range(16):
            if t + 1 < 16:
                dz2(t + 1).start()
                w2(t + 1).start()
            dz2(t).wait()
            w2(t).wait()
            part = _dot_nt(dz_buf[t % 2], stage_d[t % 2])
            if t % 8 == 0:
                if t > 0:
                    dh_out(0).wait()
                acc[...] = part
            else:
                acc[...] += part
            if t % 8 == 7:
                dh_out(t // 8).start()
        dh_out(1).wait()
        for i in range(3):
            to_owner(i).wait_send()
        for i in range(3):
            to_owner(i).wait_recv()

    any_spec = pl.BlockSpec(memory_space=pl.ANY)
    return pl.pallas_call(
        body, name="in_proj_bwd",
        out_shape=[jax.ShapeDtypeStruct((S, D), F32), jax.ShapeDtypeStruct(slab, F32),
                   jax.ShapeDtypeStruct((4,) + slab, BF16), jax.ShapeDtypeStruct((3,) + slab, BF16)],
        in_specs=[any_spec] * 3,
        out_specs=[any_spec, pl.BlockSpec(memory_space=pltpu.VMEM), any_spec, any_spec],
        scratch_shapes=[pltpu.VMEM((S, D), BF16), pltpu.VMEM((2, half, SHARD_COLS), BF16),
                        pltpu.VMEM((2,) + slab, BF16), pltpu.VMEM(slab, BF16), pltpu.VMEM((3,) + slab, BF16),
                        pltpu.VMEM((half, D), F32),
                        pltpu.SemaphoreType.DMA((2,)), pltpu.SemaphoreType.DMA((2,)), pltpu.SemaphoreType.DMA,
                        pltpu.SemaphoreType.DMA, pltpu.SemaphoreType.DMA,
                        pltpu.SemaphoreType.DMA((4,)), pltpu.SemaphoreType.DMA((4,)),
                        pltpu.SemaphoreType.DMA((3,)), pltpu.SemaphoreType.DMA((3,))],
        compiler_params=_cp(),
    )(dz, h, w_in)


def _grad_x(x, norm_w, dh, dx2):
    tr = 256

    def body(x_ref, w_ref, dh_ref, dx2_ref, gx_ref, gnw_ref):
        @pl.when(pl.program_id(0) == 0)
        def _():
            gnw_ref[...] = jnp.zeros_like(gnw_ref)

        xv, dhv = x_ref[...], dh_ref[...]
        r = lax.rsqrt(jnp.mean(xv * xv, axis=-1, keepdims=True) + EPS)
        n = xv * r
        gnw_ref[...] += jnp.sum(dhv * n, axis=0, keepdims=True)
        dn = dhv * w_ref[...]
        gx_ref[...] = dx2_ref[...] + r * (dn - n * jnp.mean(dn * n, axis=-1, keepdims=True))

    row = pl.BlockSpec((tr, D), lambda i: (i, 0))
    vec = pl.BlockSpec((1, D), lambda i: (0, 0))
    return pl.pallas_call(
        body, name="grad_x", grid=(S // tr,),
        out_shape=[jax.ShapeDtypeStruct((S, D), F32), jax.ShapeDtypeStruct((1, D), F32)],
        in_specs=[row, vec, row, row], out_specs=[row, vec],
        compiler_params=_cp(("arbitrary",)),
    )(x, norm_w, dh, dx2)


def _rope_tables(positions):
    inv_freq = 10000.0 ** (-jnp.arange(0, 64, 2, dtype=F32) / 64)
    ang = positions.astype(F32)[:, None] * inv_freq[None, :]
    cos, sin = jnp.cos(ang), jnp.sin(ang)
    return jnp.tile(cos, (1, 4)), jnp.tile(jnp.concatenate([-sin, sin], axis=1), (1, 2))


def _local_step(x, positions, norm_w, lb_logits, hnw, fnw, target, w_in, w_a, w_b, w_out):
    cc, ss = _rope_tables(positions)
    lbv = jax.nn.sigmoid(lb_logits[0:1] - lb_logits[1:2])
    h = _rmsnorm_in(x, norm_w)
    z = _in_proj(h, w_in)
    o, o_a, states = _hgrn_fwd(z, lbv, hnw)
    ob, lse, o_bg = _attn_fwd(z, cc, ss)
    dx2, dx2b, dgp, do_a, do_bg, merged, dy_a, dy_b, tail_small = _tail(x, o_a, o_bg, z, target, w_a, w_b, w_out, fnw)
    g_out, gb_out = _tn_matmul(merged, dx2b, "grad_w_out")
    g_a, gb_a = _tn_matmul(o_a, dy_a, "grad_w_a")
    g_b, gb_b = _tn_matmul(o_bg, dy_b, "grad_w_b")
    dhq, dhf, dhi, dhg, glb, ghn = _hgrn_bwd(z, o, do_a, states, lbv, hnw)
    dq, dk, dv, dag = _attn_bwd(z, cc, ss, ob, lse, do_bg)
    dz = jnp.concatenate([dhq, dhf, dhi, dhg, dq[0], dq[1], dq[2], dk[0], dk[1], dk[2], dv[0], dv[1], dv[2], dag, dgp],
                         axis=1)
    dh, g_chip_in, _, r2_in = _in_proj_bwd(dz, h, w_in)
    grad_x, gnw = _grad_x(x, norm_w, dh, dx2)
    ghn_row = jnp.pad(jnp.sum(ghn, axis=0), ((0, 0), (0, D - 128)))
    small = jnp.concatenate([gnw, glb, ghn_row, tail_small[0:2], jnp.zeros((3, D), F32)], axis=0)
    return grad_x, (g_chip_in, r2_in), (g_a, g_b, g_out), (gb_a, gb_b, gb_out), small


def kernel(x, positions, norm_w, w_in, lb_logits, hgrn_norm_w, w_branch_a, w_branch_b, w_out, final_norm_w, loss_target, m_norm_w, m_w_in, m_lb_logits, m_hgrn_norm_w, m_w_branch_a, m_w_branch_b, m_w_out, m_final_norm_w, v_norm_w, v_w_in, v_lb_logits, v_hgrn_norm_w, v_w_branch_a, v_w_branch_b, v_w_out, v_final_norm_w):
    ix, iy, ic = _mesh_pos()
    core = jnp.reshape(ic, (1,)).astype(jnp.int32)
    pos = jnp.stack([4 * ix + 2 * iy + ic, 2 * ix + iy]).astype(jnp.int32)

    shards = [w_in[0], w_branch_a[0], w_branch_b[0], w_out[0]]
    moments_m = [m_w_in[0], m_w_branch_a[0], m_w_branch_b[0], m_w_out[0]]
    moments_v = [v_w_in[0], v_w_branch_a[0], v_w_branch_b[0], v_w_out[0]]
    names = ("w_in", "w_a", "w_b", "w_out")
    full = _allgather_weights([_cast_bf16(w, f"cast_{nm}") for w, nm in zip(shards, names)])

    fnw2 = final_norm_w.reshape(1, D)
    grad_x, (g_chip_in, r2_in), grads, gb, small = _local_step(
        x[0], positions[0], norm_w, lb_logits, hgrn_norm_w, fnw2, loss_target[0], *full)

    ids = (1, 2, 3)
    r1 = _exchange_sibling(ids, gb)
    pb = [_chip_partials(a, grads[i], r1[i], core) for i, a in enumerate(ids)]
    *r2, gathered = _exchange_chips(ids, pb, small)
    big = [_reduce_own_and_update(shards[0], moments_m[0], moments_v[0], g_chip_in, r2_in)]
    big += [_reduce_and_update(a, shards[a], moments_m[a], moments_v[a], grads[i], r1[i], r2[i], pos)
            for i, a in enumerate(ids)]
    sm = _small_update(gathered, norm_w, lb_logits, hgrn_norm_w, fnw2,
                       (m_norm_w, m_lb_logits, m_hgrn_norm_w, m_final_norm_w.reshape(1, D),
                        v_norm_w, v_lb_logits, v_hgrn_norm_w, v_final_norm_w.reshape(1, D)))
    loss = sm[0][0, 0]
    outs = [loss, grad_x[None]]
    for kind in range(4):
        s_nw, s_lb, s_hn, s_fn = sm[1 + 4 * kind:5 + 4 * kind]
        outs += [s_nw, big[0][kind][None], s_lb, s_hn, big[1][kind][None], big[2][kind][None],
                 big[3][kind][None], s_fn.reshape(D)]
    return tuple(outs)
```

```python
import functools

import jax
import jax.numpy as jnp
from jax import lax
from jax.experimental import pallas as pl
from jax.experimental.pallas import tpu as pltpu

F32 = jnp.float32
BF16 = jnp.bfloat16
MESH = pl.DeviceIdType.MESH

S = 2048
D = 1024
NDEV = 8
HEADS = 8
CHUNK = 64
SUB = 16
HBLK = 128
ATT_PAD = 128
ATT_UNROLL = 4
EXP_CLAMP = 80.0
EPS = 1e-6
IN_COLS = 11264
SHARD_COLS = IN_COLS // NDEV
ATT_DILS = (1, 4, 16)
ATT_SCALE = 64 ** -0.5
LANES = 128

ADAM_LR, ADAM_B1, ADAM_B2, ADAM_EPS, ADAM_WD, ADAM_STEP = 0.001, 0.9, 0.999, 1e-08, 0.01, 10

VMEM_LIMIT = 56 * 1024 * 1024


def _cp(sem=None, **kw):
    return pltpu.CompilerParams(dimension_semantics=sem, vmem_limit_bytes=VMEM_LIMIT, **kw)


def _dot(a, b):
    return jnp.dot(a, b, preferred_element_type=F32)


def _dot_nt(a, b):
    return lax.dot_general(a, b, (((1,), (1,)), ((), ())), preferred_element_type=F32)


def _dot_tn(a, b):
    return lax.dot_general(a, b, (((0,), (0,)), ((), ())), preferred_element_type=F32)


def _split2(x):
    hi = x.astype(BF16)
    lo = (x - hi.astype(F32)).astype(BF16)
    return hi, lo


def _split3(x):
    hi = x.astype(BF16)
    r = x - hi.astype(F32)
    mid = r.astype(BF16)
    lo = (r - mid.astype(F32)).astype(BF16)
    return hi, mid, lo


def _dot_ones(ones_bf16, x):
    hi, mid, lo = _split3(x)
    return _dot(ones_bf16, hi) + _dot(ones_bf16, mid) + _dot(ones_bf16, lo)


def _dot3(dotfn, a, b):
    ah, al = _split2(a)
    bh, bl = _split2(b)
    return dotfn(ah, bh) + dotfn(ah, bl) + dotfn(al, bh)


def _silu(x):
    return x * jax.nn.sigmoid(x)


def _dsilu(x):
    s = jax.nn.sigmoid(x)
    return s * (1.0 + x * (1.0 - s))


def _mesh_pos():
    return lax.axis_index("x"), lax.axis_index("y"), lax.axis_index("c")


def _shard_of(ref, a, d):
    if a == 0:
        return ref.at[:, pl.ds(pl.multiple_of(d * SHARD_COLS, LANES), SHARD_COLS)]
    if a == 2:
        return ref.at[:, pl.ds(pl.multiple_of(d * LANES, LANES), LANES)]
    return ref.at[pl.ds(pl.multiple_of(d * 128, 128), 128), :]


FULL_SHAPES = ((D, IN_COLS), (D, D), (512, D), (D, D))
SHARD_SHAPES = ((D, SHARD_COLS), (128, D), (512, 128), (128, D))


def _allgather_weights(ids, shards):
    n = len(shards)

    def body(*refs):
        ins, outs = refs[:n], refs[n:2 * n]
        send_sems, recv_sems, local_sems = refs[2 * n:]
        x, y, c = _mesh_pos()
        me, sibling = (x, y, c), (x, y, 1 - c)
        chips = [(1 - x, y), (x, 1 - y), (1 - x, 1 - y)]

        def blk(a, p):
            return _shard_of(outs[a], ids[a], 4 * p[0] + 2 * p[1] + p[2])

        def copy(a, k, block, to, src=None):
            return pltpu.make_async_remote_copy(
                src_ref=blk(a, block) if src is None else src, dst_ref=blk(a, block),
                send_sem=send_sems.at[a * 7 + k], recv_sem=recv_sems.at[a * 7 + k],
                device_id=to, device_id_type=MESH)

        mine = [pltpu.make_async_copy(ins[a], blk(a, me), local_sems.at[a]) for a in range(n)]
        for cp in mine:
            cp.start()
        first = []
        for a in range(n):
            first += [copy(a, 1 + j, me, (*chip, c), src=ins[a]) for j, chip in enumerate(chips)]
        for a in range(n):
            first.append(copy(a, 0, me, sibling, src=ins[a]))
        for cp in first:
            cp.start()
        passed = []
        for j, chip in enumerate(chips):
            for a in range(n):
                copy(a, 1 + j, (*chip, c), me).wait_recv()
                fwd = copy(a, 4 + j, (*chip, c), sibling)
                fwd.start()
                passed.append(fwd)
        for a in range(n):
            copy(a, 0, sibling, me).wait_recv()
        for j, chip in enumerate(chips):
            for a in range(n):
                copy(a, 4 + j, (*chip, 1 - c), me).wait_recv()
        for cp in first + passed:
            cp.wait_send()
        for cp in mine:
            cp.wait()

    any_spec = pl.BlockSpec(memory_space=pl.ANY)
    return pl.pallas_call(
        body, name="allgather_weights",
        out_shape=[jax.ShapeDtypeStruct(FULL_SHAPES[a], BF16) for a in ids],
        in_specs=[any_spec] * n, out_specs=[any_spec] * n,
        scratch_shapes=[pltpu.SemaphoreType.DMA((7 * n,)), pltpu.SemaphoreType.DMA((7 * n,)),
                        pltpu.SemaphoreType.DMA((n,))],
    )(*shards)


def _in_proj_gather(h, w_shard):
    half = S // 2
    slab = (D, SHARD_COLS)

    def body(h_hbm, w_hbm, z_hbm, wfull_hbm, h_buf, land, zstage,
             h_sem, own_sem, z_sem, wout_sem, send_sems, recv_sems):
        x, y, c = _mesh_pos()
        sibling = (x, y, 1 - c)
        north = c == 1

        def chips_of(first_x):
            near = (jnp.where(first_x, 1 - x, x), jnp.where(first_x, y, 1 - y))
            far = (jnp.where(first_x, x, 1 - x), jnp.where(first_x, 1 - y, y))
            return [near, far, (1 - x, 1 - y)]

        mine, theirs = chips_of(north), chips_of(jnp.logical_not(north))

        def dev(chip, core):
            return 4 * chip[0] + 2 * chip[1] + core

        block_of = ([dev((x, y), c), dev((x, y), 1 - c)] + [dev(q, c) for q in mine]
                    + [dev(q, 1 - c) for q in theirs])

        def cols(d):
            if isinstance(d, int):
                return pl.ds(d * SHARD_COLS, SHARD_COLS)
            return pl.ds(pl.multiple_of(d * SHARD_COLS, LANES), SHARD_COLS)

        def send(k, src, dst_slot, to):
            return pltpu.make_async_remote_copy(
                src_ref=src, dst_ref=land.at[dst_slot], send_sem=send_sems.at[k], recv_sem=recv_sems.at[k],
                device_id=to, device_id_type=MESH)

        def to_sibling():
            return send(0, w_hbm, 1, sibling)

        def to_chip(j):
            return send(1 + j, w_hbm, 2 + j, (*mine[j], c))

        def pass_on(j):
            return send(4 + j, land.at[2 + j], 5 + j, sibling)

        own = pltpu.make_async_copy(w_hbm, land.at[0], own_sem)
        h_copy = pltpu.make_async_copy(h_hbm, h_buf, h_sem)
        own.start()
        h_copy.start()
        to_sibling().start()
        to_chip(0).start()
        h_copy.wait()
        own.wait()

        def multiply(slot, n_done):
            d = block_of[slot]
            out = pltpu.make_async_copy(land.at[slot], wfull_hbm.at[:, cols(d)], wout_sem.at[slot])
            out.start()
            for r in range(2):
                rows = pl.ds(r * half, half)
                zc = pltpu.make_async_copy(zstage.at[r], z_hbm.at[rows, cols(d)], z_sem.at[r])
                if n_done > 0:
                    zc.wait()
                zstage[r] = _dot(h_buf[r * half:(r + 1) * half, :], land[slot])
                zc.start()
            return out

        outs = [multiply(0, 0)]
        to_sibling().wait_recv()
        outs.append(multiply(1, 1))
        done = 2
        for j in range(3):
            to_chip(j).wait_recv()
            pass_on(j).start()
            to_chip(j).wait_send()
            if j < 2:
                to_chip(j + 1).start()
            outs.append(multiply(2 + j, done))
            pass_on(j).wait_recv()
            outs.append(multiply(5 + j, done + 1))
            done += 2
        for r in range(2):
            pltpu.make_async_copy(zstage.at[r], z_hbm.at[pl.ds(r * half, half), cols(0)], z_sem.at[r]).wait()
        for out in outs:
            out.wait()
        to_sibling().wait_send()
        for j in range(3):
            pass_on(j).wait_send()

    any_spec = pl.BlockSpec(memory_space=pl.ANY)
    return pl.pallas_call(
        body, name="in_proj_gather",
        out_shape=[jax.ShapeDtypeStruct((S, IN_COLS), F32), jax.ShapeDtypeStruct((D, IN_COLS), BF16)],
        in_specs=[any_spec] * 2, out_specs=[any_spec] * 2,
        scratch_shapes=[pltpu.VMEM((S, D), BF16), pltpu.VMEM((8,) + slab, BF16), pltpu.VMEM((2, half, SHARD_COLS), F32),
                        pltpu.SemaphoreType.DMA, pltpu.SemaphoreType.DMA, pltpu.SemaphoreType.DMA((2,)),
                        pltpu.SemaphoreType.DMA((8,)), pltpu.SemaphoreType.DMA((7,)), pltpu.SemaphoreType.DMA((7,))],
        compiler_params=_cp(),
    )(h, w_shard)


def _exchange_sibling(ids, gb):
    n = len(gb)

    def body(*refs):
        ins, outs = refs[:n], refs[n:2 * n]
        send_sems, recv_sems = refs[2 * n:]
        x, y, c = _mesh_pos()
        sibling = (x, y, 1 - c)
        copies = []
        for i, a in enumerate(ids):
            for q in range(4):
                copies.append(pltpu.make_async_remote_copy(
                    src_ref=_shard_of(ins[i], a, 2 * q + (1 - c)), dst_ref=outs[i].at[q],
                    send_sem=send_sems.at[i * 4 + q], recv_sem=recv_sems.at[i * 4 + q],
                    device_id=sibling, device_id_type=MESH))
        for cp in copies:
            cp.start()
        for cp in copies:
            cp.wait()

    any_spec = pl.BlockSpec(memory_space=pl.ANY)
    return pl.pallas_call(
        body, name="grads_to_sibling",
        out_shape=[jax.ShapeDtypeStruct((4,) + SHARD_SHAPES[a], BF16) for a in ids],
        in_specs=[any_spec] * n, out_specs=[any_spec] * n,
        scratch_shapes=[pltpu.SemaphoreType.DMA((4 * n,)), pltpu.SemaphoreType.DMA((4 * n,))],
    )(*gb)


def _exchange_chips(ids, pb, small):
    n = len(pb)

    def body(*refs):
        ins, small_ref = refs[:n], refs[n]
        outs, small_out = refs[n + 1:2 * n + 1], refs[2 * n + 1]
        send_sems, recv_sems, ssend, srecv, local_sem = refs[2 * n + 2:]
        x, y, c = _mesh_pos()
        chips = [(1 - x, y), (x, 1 - y), (1 - x, 1 - y)]
        me = 4 * x + 2 * y + c
        copies = []
        for a in range(n):
            for k, chip in enumerate(chips):
                copies.append(pltpu.make_async_remote_copy(
                    src_ref=ins[a].at[2 * chip[0] + chip[1]], dst_ref=outs[a].at[k],
                    send_sem=send_sems.at[a * 3 + k], recv_sem=recv_sems.at[a * 3 + k],
                    device_id=(*chip, c), device_id_type=MESH))
        for r in range(1, NDEV):
            peer = (1 - x if r & 4 else x, 1 - y if r & 2 else y, 1 - c if r & 1 else c)
            copies.append(pltpu.make_async_remote_copy(
                src_ref=small_ref, dst_ref=small_out.at[me],
                send_sem=ssend.at[r - 1], recv_sem=srecv.at[r - 1],
                device_id=peer, device_id_type=MESH))
        own = pltpu.make_async_copy(small_ref, small_out.at[me], local_sem)
        own.start()
        for cp in copies:
            cp.start()
        for cp in copies:
            cp.wait()
        own.wait()

    any_spec = pl.BlockSpec(memory_space=pl.ANY)
    return pl.pallas_call(
        body, name="grads_between_chips",
        out_shape=[jax.ShapeDtypeStruct((3,) + SHARD_SHAPES[a], BF16) for a in ids]
        + [jax.ShapeDtypeStruct((NDEV,) + small.shape, F32)],
        in_specs=[any_spec] * (n + 1), out_specs=[any_spec] * (n + 1),
        scratch_shapes=[pltpu.SemaphoreType.DMA((3 * n,)), pltpu.SemaphoreType.DMA((3 * n,)),
                        pltpu.SemaphoreType.DMA((NDEV - 1,)), pltpu.SemaphoreType.DMA((NDEV - 1,)),
                        pltpu.SemaphoreType.DMA],
    )(*pb, small)


def _shard_tiles(a):
    rows, cols = SHARD_SHAPES[a]
    tr = min(rows, 256)
    return (tr, cols), rows // tr


def _full_index(a, d, i):
    (tr, _), nt = _shard_tiles(a)
    if a in (0, 2):
        return (i, d)
    return (d * nt + i, 0)


def _cast_bf16(x, name):
    rows, cols = x.shape
    tr = min(rows, 256)

    def body(x_ref, o_ref):
        o_ref[...] = x_ref[...].astype(BF16)

    return pl.pallas_call(
        body, name=name, out_shape=jax.ShapeDtypeStruct(x.shape, BF16), grid=(rows // tr,),
        in_specs=[pl.BlockSpec((tr, cols), lambda i: (i, 0))],
        out_specs=pl.BlockSpec((tr, cols), lambda i: (i, 0)),
        compiler_params=_cp(("parallel",)),
    )(x)


def _chip_partials(a, g_full, r1, core):
    tile, nt = _shard_tiles(a)

    def body(c_ref, g_ref, r_ref, o_ref):
        o_ref[0] = (g_ref[...] + r_ref[0].astype(F32)).astype(BF16)

    grid_spec = pltpu.PrefetchScalarGridSpec(
        num_scalar_prefetch=1, grid=(4, nt),
        in_specs=[pl.BlockSpec(tile, lambda q, i, c: _full_index(a, 2 * q + c[0], i)),
                  pl.BlockSpec((1,) + tile, lambda q, i, c: (q, i, 0))],
        out_specs=pl.BlockSpec((1,) + tile, lambda q, i, c: (q, i, 0)))
    return pl.pallas_call(
        body, name=f"chip_partials_{a}", grid_spec=grid_spec,
        out_shape=jax.ShapeDtypeStruct((4,) + SHARD_SHAPES[a], BF16),
        compiler_params=_cp(("parallel", "parallel")),
    )(core, g_full, r1)


def _adam(w, g, m, v):
    m = ADAM_B1 * m + (1.0 - ADAM_B1) * g
    v = ADAM_B2 * v + (1.0 - ADAM_B2) * (g * g)
    m_hat = m / (1.0 - ADAM_B1 ** ADAM_STEP)
    v_hat = v / (1.0 - ADAM_B2 ** ADAM_STEP)
    delta = -ADAM_LR * (m_hat / (jnp.sqrt(v_hat) + ADAM_EPS) + ADAM_WD * w)
    return delta, m, v


def _reduce_and_update(a, w, m, v, g_full, r1, r2, pos):
    tile, nt = _shard_tiles(a)

    def body(p_ref, w_ref, m_ref, v_ref, g_ref, r1_ref, r2_ref, go_ref, do_ref, mo_ref, vo_ref):
        g = g_ref[...] + r1_ref[0].astype(F32)
        g = g + r2_ref[0].astype(F32)
        g = g + r2_ref[1].astype(F32)
        g = g + r2_ref[2].astype(F32)
        delta, m_new, v_new = _adam(w_ref[...], g, m_ref[...], v_ref[...])
        go_ref[...] = g
        do_ref[...] = delta
        mo_ref[...] = m_new
        vo_ref[...] = v_new

    own = pl.BlockSpec(tile, lambda i, p: (i, 0))
    grid_spec = pltpu.PrefetchScalarGridSpec(
        num_scalar_prefetch=1, grid=(nt,),
        in_specs=[own, own, own,
                  pl.BlockSpec(tile, lambda i, p: _full_index(a, p[0], i)),
                  pl.BlockSpec((1,) + tile, lambda i, p: (p[1], i, 0)),
                  pl.BlockSpec((3,) + tile, lambda i, p: (0, i, 0))],
        out_specs=[own] * 4)
    shp = jax.ShapeDtypeStruct(w.shape, F32)
    return pl.pallas_call(
        body, name=f"reduce_update_{a}", grid_spec=grid_spec, out_shape=[shp] * 4,
        compiler_params=_cp(("parallel",)),
    )(pos, w, m, v, g_full, r1, r2)


def _reduce_own_and_update(w, m, v, g_chip, r2):
    tile, nt = _shard_tiles(0)

    def body(w_ref, m_ref, v_ref, g_ref, r2_ref, go_ref, do_ref, mo_ref, vo_ref):
        g = g_ref[...] + r2_ref[0].astype(F32)
        g = g + r2_ref[1].astype(F32)
        g = g + r2_ref[2].astype(F32)
        delta, m_new, v_new = _adam(w_ref[...], g, m_ref[...], v_ref[...])
        go_ref[...] = g
        do_ref[...] = delta
        mo_ref[...] = m_new
        vo_ref[...] = v_new

    own = pl.BlockSpec(tile, lambda i: (i, 0))
    shp = jax.ShapeDtypeStruct(w.shape, F32)
    return pl.pallas_call(
        body, name="reduce_update_0", grid=(nt,), out_shape=[shp] * 4,
        in_specs=[own, own, own, own, pl.BlockSpec((3,) + tile, lambda i: (0, i, 0))], out_specs=[own] * 4,
        compiler_params=_cp(("parallel",)),
    )(w, m, v, g_chip, r2)


def _small_update(gathered, norm_w, lb_logits, hnw, fnw, moments):
    m_nw, m_lb, m_hn, m_fn, v_nw, v_lb, v_hn, v_fn = moments

    def body(g_ref, nw, lb, hn, fn, mnw, mlb, mhn, mfn, vnw, vlb, vhn, vfn,
             loss_o, g_nw, g_lb, g_hn, g_fn, d_nw, d_lb, d_hn, d_fn,
             mo_nw, mo_lb, mo_hn, mo_fn, vo_nw, vo_lb, vo_hn, vo_fn):
        tot = g_ref[0]
        for d in range(1, NDEV):
            tot = tot + g_ref[d]
        loss_o[...] = tot[4:5, 0:LANES]
        logits = lb[...]
        lbv = jax.nn.sigmoid(logits[0:1] - logits[1:2])
        chain = tot[1:2] * lbv * (1.0 - lbv)
        grads = (tot[0:1], jnp.concatenate([chain, -chain], axis=0), tot[2:3, 0:LANES], tot[3:4])
        outs = ((nw, mnw, vnw, g_nw, d_nw, mo_nw, vo_nw), (lb, mlb, vlb, g_lb, d_lb, mo_lb, vo_lb),
                (hn, mhn, vhn, g_hn, d_hn, mo_hn, vo_hn), (fn, mfn, vfn, g_fn, d_fn, mo_fn, vo_fn))
        for g, (w_r, m_r, v_r, g_o, d_o, m_o, v_o) in zip(grads, outs):
            delta, m_new, v_new = _adam(w_r[...], g, m_r[...], v_r[...])
            g_o[...] = g
            d_o[...] = delta
            m_o[...] = m_new
            v_o[...] = v_new

    shapes = [norm_w.shape, lb_logits.shape, hnw.shape, fnw.shape]
    out_shape = [jax.ShapeDtypeStruct((1, LANES), F32)] + [jax.ShapeDtypeStruct(s, F32) for s in shapes] * 4
    return pl.pallas_call(body, name="small_update", out_shape=out_shape, compiler_params=_cp())(
        gathered, norm_w, lb_logits, hnw, fnw, m_nw, m_lb, m_hn, m_fn, v_nw, v_lb, v_hn, v_fn)


def _rmsnorm_in(x, norm_w):
    tr = 512

    def body(x_ref, w_ref, h_ref):
        xv = x_ref[...]
        r = lax.rsqrt(jnp.mean(xv * xv, axis=-1, keepdims=True) + EPS)
        h_ref[...] = (xv * r * w_ref[...]).astype(BF16)

    return pl.pallas_call(
        body, name="rmsnorm_in", out_shape=jax.ShapeDtypeStruct((S, D), BF16), grid=(S // tr,),
        in_specs=[pl.BlockSpec((tr, D), lambda i: (i, 0)), pl.BlockSpec((1, D), lambda i: (0, 0))],
        out_specs=pl.BlockSpec((tr, D), lambda i: (i, 0)),
        compiler_params=_cp(("parallel",)),
    )(x, norm_w)


def _in_proj(h, w_in):
    tn = 1024

    def body(h_ref, w_ref, z_ref):
        z_ref[...] = _dot(h_ref[...], w_ref[...])

    return pl.pallas_call(
        body, name="in_proj", out_shape=jax.ShapeDtypeStruct((S, IN_COLS), F32), grid=(IN_COLS // tn,),
        in_specs=[pl.BlockSpec((S, D), lambda j: (0, 0)), pl.BlockSpec((D, tn), lambda j: (0, j))],
        out_specs=pl.BlockSpec((S, tn), lambda j: (0, j)),
        compiler_params=_cp(("parallel",)),
    )(h, w_in)


def _block_tri(n, block, upper=False):
    r = lax.broadcasted_iota(jnp.int32, (n, n), 0)
    c = lax.broadcasted_iota(jnp.int32, (n, n), 1)
    keep = (c >= r) if upper else (c <= r)
    return jnp.where(keep & ((r // block) == (c // block)), 1.0, 0.0).astype(BF16)


def _tril_mask(n):
    r = lax.broadcasted_iota(jnp.int32, (n, n), 0)
    c = lax.broadcasted_iota(jnp.int32, (n, n), 1)
    return c <= r


def _chunk_scores(q, k, b, bex, r0, mask):
    rows = slice(r0, r0 + CHUNK)
    parts, qs_l, ek_l, eq_l = [], [], [], []
    for i in range(CHUNK // SUB):
        ri = slice(r0 + SUB * i, r0 + SUB * (i + 1))
        base = bex[r0 + SUB * i:r0 + SUB * i + 1]
        eq = jnp.exp(b[ri] - base)
        ek = jnp.exp(jnp.minimum(base - b[rows], EXP_CLAMP))
        qs = q[ri] * eq
        parts.append(_dot_nt(qs.astype(BF16), (k[rows] * ek).astype(BF16)))
        qs_l.append(qs)
        ek_l.append(ek)
        eq_l.append(eq)
    return jnp.where(mask, jnp.concatenate(parts, axis=0), 0.0), qs_l, ek_l, eq_l


def _hgrn_cols(hq, hf, hi, lb):
    sg = jax.nn.sigmoid(hf)
    f = lb + (1.0 - lb) * sg
    g = jnp.log(f)
    b = _dot_ones(_block_tri(HBLK, CHUNK), g)
    return _silu(hq), 1.0 - f, g, hi, sg, f, b


def _hgrn_fwd(z, lbv, hnw):
    ntb, nch = S // HBLK, HBLK // CHUNK

    def body(hq_ref, hf_ref, hi_ref, hg_ref, lb_ref, hnw_ref, o_ref, oa_ref, st_ref, state):
        @pl.when(pl.program_id(0) == 0)
        def _():
            state[...] = jnp.zeros_like(state)

        q_a, k_a, g_a, v_a, _, _, b_a = _hgrn_cols(hq_ref[...], hf_ref[...], hi_ref[...], lb_ref[...])
        bex_a = b_a - g_a
        eb_a = jnp.exp(b_a)
        mask = _tril_mask(CHUNK)
        hg = hg_ref[...]
        w = hnw_ref[...]
        for h in range(HEADS):
            cols = slice(128 * h, 128 * h + 128)
            q, k, v, b, bex, eb = q_a[:, cols], k_a[:, cols], v_a[:, cols], b_a[:, cols], bex_a[:, cols], eb_a[:, cols]
            st = state[h]
            outs = []
            for c in range(nch):
                r0 = c * CHUNK
                rows = slice(r0, r0 + CHUNK)
                a, _, _, _ = _chunk_scores(q, k, b, bex, r0, mask)
                vb = v[rows].astype(BF16)
                b_last = b[r0 + CHUNK - 1:r0 + CHUNK]
                qe = (q[rows] * eb[rows]).astype(BF16)
                outs.append(_dot(a.astype(BF16), vb) + _dot_nt(qe, st.astype(BF16)))
                st_ref[h, c] = st
                ke = (k[rows] * jnp.exp(b_last - b[rows])).astype(BF16)
                st = st * jnp.exp(b_last) + _dot_tn(vb, ke)
            state[h] = st
            o = jnp.concatenate(outs, axis=0)
            o_ref[:, cols] = o
            r = lax.rsqrt(jnp.mean(o * o, axis=-1, keepdims=True) + EPS)
            oa_ref[:, cols] = (o * r * w * _silu(hg[:, cols])).astype(BF16)

    def zcol(j):
        return pl.BlockSpec((HBLK, D), lambda t: (t, j))

    out_blk = pl.BlockSpec((HBLK, D), lambda t: (t, 0))
    return pl.pallas_call(
        body, name="hgrn_fwd", grid=(ntb,),
        out_shape=[jax.ShapeDtypeStruct((S, D), F32), jax.ShapeDtypeStruct((S, D), BF16),
                   jax.ShapeDtypeStruct((HEADS, S // CHUNK, 128, 128), F32)],
        in_specs=[zcol(0), zcol(1), zcol(2), zcol(3),
                  pl.BlockSpec((1, D), lambda t: (0, 0)), pl.BlockSpec((1, 128), lambda t: (0, 0))],
        out_specs=[out_blk, out_blk, pl.BlockSpec((HEADS, nch, 128, 128), lambda t: (0, t, 0, 0))],
        scratch_shapes=[pltpu.VMEM((HEADS, 128, 128), F32)],
        compiler_params=_cp(("arbitrary",)),
    )(z, z, z, z, lbv, hnw)


def _half_mask():
    lane = lax.broadcasted_iota(jnp.int32, (1, LANES), 1)
    return (lane % 64) < 32


def _rope(t, cc, ss, first_half):
    partner = jnp.where(first_half, pltpu.roll(t, 96, 1), pltpu.roll(t, 32, 1))
    return t * cc + partner * ss


def _attn_masks():
    i = lax.broadcasted_iota(jnp.int32, (128, 128), 0)
    j = lax.broadcasted_iota(jnp.int32, (128, 128), 1)
    return j >= i, j <= i


def _to_residues_dyn(g, dst, src, row0=0, dtype=None):
    for gi, dil in enumerate((1, 4, 16)):
        m = S // dil

        @pl.when(g == gi)
        def _(dil=dil, m=m):
            for r in range(dil):
                v = src[...] if dil == 1 else src[pl.ds(r, m, stride=dil), :]
                if dtype is not None:
                    v = v.astype(dtype)
                dst[row0 + r * m:row0 + (r + 1) * m, 0:LANES] = v


def _from_residues_dyn(g, dst, src, row0=0):
    for gi, dil in enumerate((1, 4, 16)):
        m = S // dil

        @pl.when(g == gi)
        def _(dil=dil, m=m):
            for r in range(dil):
                v = src[row0 + r * m:row0 + (r + 1) * m, :]
                if dil == 1:
                    dst[...] = v
                else:
                    dst[pl.ds(r, m, stride=dil), :] = v


def _group_blocks(g):
    return jnp.where(g == 0, 16, jnp.where(g == 1, 4, 1))


def _attn_in_specs(extra):
    def zcol(off):
        return pl.BlockSpec((S, LANES), lambda p, g: (0, off + 4 * g + p))

    per_pair = pl.BlockSpec((S, LANES), lambda p, g: (0, p))
    const = pl.BlockSpec((S, LANES), lambda p, g: (0, 0))
    return [zcol(32), zcol(44), zcol(56), pl.BlockSpec((S, LANES), lambda p, g: (0, 68 + p)), const, const] + [per_pair] * extra


def _attn_fwd(z, cc, ss):
    def body(q_ref, k_ref, v_ref, ag_ref, cc_ref, ss_ref, ob_ref, lse_ref, obg_ref,
             tmp, qs, ks, vx, og, mg, lg, o_t, m_t, l_t, o_acc, m_acc, l_acc):
        g = pl.program_id(1)
        first_half = _half_mask()
        prev_ok, cur_ok = _attn_masks()
        lane = lax.broadcasted_iota(jnp.int32, (1, LANES), 1)
        heads = (lane < 64, lane >= 64)
        nblk = _group_blocks(g)

        @pl.when(g == 0)
        def _():
            ks[0:ATT_PAD, :] = jnp.zeros((ATT_PAD, LANES), BF16)
            vx[0:ATT_PAD, 0:LANES] = jnp.zeros((ATT_PAD, LANES), BF16)
            vx[:, LANES:2 * LANES] = jnp.ones((ATT_PAD + S, LANES), BF16)

        tmp[...] = _rope(q_ref[...], cc_ref[...], ss_ref[...], first_half) * ATT_SCALE
        _to_residues_dyn(g, qs, tmp)
        tmp[...] = _rope(k_ref[...], cc_ref[...], ss_ref[...], first_half)
        _to_residues_dyn(g, ks, tmp, ATT_PAD, BF16)
        _to_residues_dyn(g, vx, v_ref, ATT_PAD, BF16)

        def unit(u, carry):
            start = pl.multiple_of(u * 128, 128)
            cur = pl.ds(start, 128)
            pm = prev_ok & ((u & (nblk - 1)) != 0)
            qu = qs[cur, :]
            kcat = ks[pl.ds(start, 256), :]
            vext = vx[pl.ds(start, 256), :]
            o_u = m_u = l_u = None
            for hh in range(2):
                s = _dot_nt(jnp.where(heads[hh], qu, 0.0).astype(BF16), kcat)
                sp = jnp.where(pm, s[:, 0:128], -jnp.inf)
                sc = jnp.where(cur_ok, s[:, 128:256], -jnp.inf)
                m = jnp.max(jnp.maximum(sp, sc), axis=-1, keepdims=True)
                p = jnp.concatenate([jnp.exp(sp - m), jnp.exp(sc - m)], axis=1).astype(BF16)
                ol = _dot(p, vext)
                mb = jnp.broadcast_to(m, (128, LANES))
                if hh == 0:
                    o_u, l_u, m_u = ol[:, 0:128], ol[:, 128:256], mb
                else:
                    o_u = jnp.where(heads[1], ol[:, 0:128], o_u)
                    l_u = jnp.where(heads[1], ol[:, 128:256], l_u)
                    m_u = jnp.where(heads[1], mb, m_u)
            og[cur, :] = o_u
            mg[cur, :] = m_u
            lg[cur, :] = l_u
            return carry

        lax.fori_loop(0, 16, unit, 0, unroll=ATT_UNROLL)
        _from_residues_dyn(g, o_t, og)
        _from_residues_dyn(g, m_t, mg)
        _from_residues_dyn(g, l_t, lg)

        @pl.when(g == 0)
        def _():
            o_acc[...] = o_t[...]
            m_acc[...] = m_t[...]
            l_acc[...] = l_t[...]

        @pl.when(g > 0)
        def _():
            m_new = jnp.maximum(m_acc[...], m_t[...])
            wa, wb = jnp.exp(m_acc[...] - m_new), jnp.exp(m_t[...] - m_new)
            o_acc[...] = o_acc[...] * wa + o_t[...] * wb
            l_acc[...] = l_acc[...] * wa + l_t[...] * wb
            m_acc[...] = m_new

        @pl.when(g == 2)
        def _():
            ob = o_acc[...] / l_acc[...]
            ob_ref[...] = ob
            lse_ref[...] = m_acc[...] + jnp.log(l_acc[...])
            obg_ref[...] = (ob * _silu(ag_ref[...])).astype(BF16)

    blk = pl.BlockSpec((S, LANES), lambda p, g: (0, p))
    buf = pltpu.VMEM((S, LANES), F32)
    return pl.pallas_call(
        body, name="attn_fwd", grid=(4, 3),
        out_shape=[jax.ShapeDtypeStruct((S, 512), F32), jax.ShapeDtypeStruct((S, 512), F32),
                   jax.ShapeDtypeStruct((S, 512), BF16)],
        in_specs=_attn_in_specs(0), out_specs=[blk, blk, blk],
        scratch_shapes=[buf, buf, pltpu.VMEM((ATT_PAD + S, LANES), BF16), pltpu.VMEM((ATT_PAD + S, 2 * LANES), BF16)] + [buf] * 9,
        compiler_params=_cp(("parallel", "arbitrary")),
    )(z, z, z, z, cc, ss)


def _tail(x, o_a, o_bg, z, target, w_a, w_b, w_out, fnw):
    tm = 256

    def body(x_ref, oa_ref, ob_ref, gpa_ref, gpb_ref, t_ref, wa_ref, wb_ref, wo_ref, fnw_ref,
             dx2_ref, dx2b_ref, dgp_ref, doa_ref, dob_ref, mg_ref, dya_ref, dyb_ref, small_ref):
        @pl.when(pl.program_id(0) == 0)
        def _():
            small_ref[...] = jnp.zeros_like(small_ref)

        wa, wb, wo = wa_ref[...], wb_ref[...], wo_ref[...]
        y_a = _dot(oa_ref[...], wa)
        y_b = _dot(ob_ref[...], wb)
        ga = jax.nn.sigmoid(gpa_ref[...])
        gb = jax.nn.sigmoid(gpb_ref[...])
        merged = (ga * y_a + gb * y_b).astype(BF16)
        x2 = x_ref[...] + _dot(merged, wo)
        r2 = lax.rsqrt(jnp.mean(x2 * x2, axis=-1, keepdims=True) + EPS)
        n2 = x2 * r2
        fw = fnw_ref[...]
        err = n2 * fw - t_ref[...]
        loss = 0.5 * jnp.sum(jnp.sum(err * err, axis=-1, keepdims=True), axis=0, keepdims=True) / D
        dy = err * (1.0 / D)
        g_fnw = jnp.sum(dy * n2, axis=0, keepdims=True)
        dn = dy * fw
        dx2 = r2 * (dn - n2 * jnp.mean(dn * n2, axis=-1, keepdims=True))
        dx2b = dx2.astype(BF16)
        dmerged = _dot_nt(dx2b, wo)
        dy_a = (dmerged * ga).astype(BF16)
        dy_b = (dmerged * gb).astype(BF16)
        dx2_ref[...] = dx2
        dx2b_ref[...] = dx2b
        dgp_ref[:, 0:D] = (dmerged * y_a * ga * (1.0 - ga)).astype(BF16)
        dgp_ref[:, D:2 * D] = (dmerged * y_b * gb * (1.0 - gb)).astype(BF16)
        doa_ref[...] = _dot_nt(dy_a, wa)
        dob_ref[...] = _dot_nt(dy_b, wb)
        mg_ref[...] = merged
        dya_ref[...] = dy_a
        dyb_ref[...] = dy_b
        small_ref[0:1, :] += g_fnw
        small_ref[1:2, :] += jnp.broadcast_to(loss, (1, D))

    def rows(cols, off=0):
        return pl.BlockSpec((tm, cols), lambda i: (i, off))

    def whole(shape):
        return pl.BlockSpec(shape, lambda i: (0, 0))

    return pl.pallas_call(
        body, name="tail", grid=(S // tm,),
        out_shape=[jax.ShapeDtypeStruct((S, D), F32), jax.ShapeDtypeStruct((S, D), BF16),
                   jax.ShapeDtypeStruct((S, 2 * D), BF16), jax.ShapeDtypeStruct((S, D), F32),
                   jax.ShapeDtypeStruct((S, 512), F32), jax.ShapeDtypeStruct((S, D), BF16),
                   jax.ShapeDtypeStruct((S, D), BF16), jax.ShapeDtypeStruct((S, D), BF16),
                   jax.ShapeDtypeStruct((8, D), F32)],
        in_specs=[rows(D), rows(D), rows(512), rows(D, 9), rows(D, 10), rows(D),
                  whole((D, D)), whole((512, D)), whole((D, D)), whole((1, D))],
        out_specs=[rows(D), rows(D), rows(2 * D), rows(D), rows(512), rows(D), rows(D), rows(D), whole((8, D))],
        compiler_params=_cp(("arbitrary",)),
    )(x, o_a, o_bg, z, z, target, w_a, w_b, w_out, fnw)


def _tn_matmul(a, b, name):
    m, n = a.shape[1], b.shape[1]
    tn = 512

    def body(a_ref, b_ref, o_ref, ob_ref):
        acc = _dot_tn(a_ref[...], b_ref[...])
        o_ref[...] = acc
        ob_ref[...] = acc.astype(BF16)

    out_blk = pl.BlockSpec((m, tn), lambda j: (0, j))
    return pl.pallas_call(
        body, name=name, grid=(n // tn,),
        out_shape=[jax.ShapeDtypeStruct((m, n), F32), jax.ShapeDtypeStruct((m, n), BF16)],
        in_specs=[pl.BlockSpec((S, m), lambda j: (0, 0)), pl.BlockSpec((S, tn), lambda j: (0, j))],
        out_specs=[out_blk, out_blk],
        compiler_params=_cp(("parallel",)),
    )(a, b)


def _hgrn_bwd(z, o, do_a, states, lbv, hnw):
    ntb, nch = S // HBLK, HBLK // CHUNK

    def body(hq_ref, hf_ref, hi_ref, hg_ref, o_ref, doa_ref, st_ref, lb_ref, hnw_ref,
             dhq_ref, dhf_ref, dhi_ref, dhg_ref, glb_ref, ghn_ref, dstate):
        @pl.when(pl.program_id(0) == 0)
        def _():
            dstate[...] = jnp.zeros_like(dstate)
            glb_ref[...] = jnp.zeros_like(glb_ref)
            ghn_ref[...] = jnp.zeros_like(ghn_ref)

        lb_a = lb_ref[...]
        hq_a, hg_a = hq_ref[...], hg_ref[...]
        q_a, k_a, g_a, v_a, sg_a, f_a, b_a = _hgrn_cols(hq_a, hf_ref[...], hi_ref[...], lb_a)
        bex_a = b_a - g_a
        eb_a = jnp.exp(b_a)
        w = hnw_ref[...]
        mask = _tril_mask(CHUNK)
        upper = _block_tri(CHUNK, CHUNK, upper=True)
        for h in range(HEADS):
            cols = slice(128 * h, 128 * h + 128)
            q, k, v, b, bex, eb = q_a[:, cols], k_a[:, cols], v_a[:, cols], b_a[:, cols], bex_a[:, cols], eb_a[:, cols]
            hq, hg, sg, f, lb = hq_a[:, cols], hg_a[:, cols], sg_a[:, cols], f_a[:, cols], lb_a[:, cols]
            ov, doa = o_ref[:, cols], doa_ref[:, cols]
            r = lax.rsqrt(jnp.mean(ov * ov, axis=-1, keepdims=True) + EPS)
            n = ov * r
            sil = _silu(hg)
            dhg_ref[:, cols] = (doa * n * w * _dsilu(hg)).astype(BF16)
            ghn_ref[h] += jnp.sum(doa * sil * n, axis=0, keepdims=True)
            dn = doa * sil * w
            do = r * (dn - n * jnp.mean(dn * n, axis=-1, keepdims=True))

            dst = dstate[h]
            dq_l, dk_l, dv_l, dg_l = [None] * nch, [None] * nch, [None] * nch, [None] * nch
            for c in reversed(range(nch)):
                r0 = c * CHUNK
                rows = slice(r0, r0 + CHUNK)
                st = st_ref[h, c]
                bc, kc, qc = b[rows], k[rows], q[rows]
                vb, dob = v[rows].astype(BF16), do[rows].astype(BF16)
                b_last = bc[CHUNK - 1:CHUNK]
                e_last = jnp.exp(b_last)
                ekl = jnp.exp(b_last - bc)
                dstb = dst.astype(BF16)
                a, qs_l, ek_l, eq_l = _chunk_scores(q, k, b, bex, r0, mask)
                da = jnp.where(mask, _dot_nt(dob, vb), 0.0)
                dv_l[c] = _dot_tn(a.astype(BF16), dob) + _dot_nt((kc * ekl).astype(BF16), dstb)
                dq_inter = _dot(dob, st.astype(BF16)) * eb[rows]
                dk_state = _dot(vb, dstb) * ekl
                dq_parts, dk_intra = [], jnp.zeros((CHUNK, 128), F32)
                for i in range(CHUNK // SUB):
                    da_i = da[SUB * i:SUB * (i + 1)]
                    dq_parts.append(_dot3(_dot, da_i, kc * ek_l[i]) * eq_l[i])
                    dk_intra = dk_intra + _dot3(_dot_tn, da_i, qs_l[i]) * ek_l[i]
                dq = jnp.concatenate(dq_parts, axis=0) + dq_inter
                dk = dk_intra + dk_state
                last = (e_last * jnp.sum(st * dst, axis=0, keepdims=True)
                        + jnp.sum(kc * dk_state, axis=0, keepdims=True))
                dg_l[c] = _dot_ones(upper, qc * dq - kc * dk) + last
                dq_l[c], dk_l[c] = dq, dk
                dst = dst * e_last + _dot_tn(dob, (qc * eb[rows]).astype(BF16))
            dstate[h] = dst
            dq, dk = jnp.concatenate(dq_l, axis=0), jnp.concatenate(dk_l, axis=0)
            dg, dv = jnp.concatenate(dg_l, axis=0), jnp.concatenate(dv_l, axis=0)
            dhq_ref[:, cols] = (dq * _dsilu(hq)).astype(BF16)
            dhi_ref[:, cols] = dv.astype(BF16)
            df = dg / f - dk
            dhf_ref[:, cols] = (df * (1.0 - lb) * sg * (1.0 - sg)).astype(BF16)
            glb_ref[:, cols] += jnp.sum(df * (1.0 - sg), axis=0, keepdims=True)

    def rev(t):
        return ntb - 1 - t

    def zcol(j):
        return pl.BlockSpec((HBLK, D), lambda t: (rev(t), j))

    blk = pl.BlockSpec((HBLK, D), lambda t: (rev(t), 0))
    return pl.pallas_call(
        body, name="hgrn_bwd", grid=(ntb,),
        out_shape=[jax.ShapeDtypeStruct((S, D), BF16)] * 4
        + [jax.ShapeDtypeStruct((1, D), F32), jax.ShapeDtypeStruct((HEADS, 1, 128), F32)],
        in_specs=[zcol(0), zcol(1), zcol(2), zcol(3), blk, blk,
                  pl.BlockSpec((HEADS, nch, 128, 128), lambda t: (0, rev(t), 0, 0)),
                  pl.BlockSpec((1, D), lambda t: (0, 0)), pl.BlockSpec((1, 128), lambda t: (0, 0))],
        out_specs=[blk] * 4 + [pl.BlockSpec((1, D), lambda t: (0, 0)),
                               pl.BlockSpec((HEADS, 1, 128), lambda t: (0, 0, 0))],
        scratch_shapes=[pltpu.VMEM((HEADS, 128, 128), F32)],
        compiler_params=_cp(("arbitrary",)),
    )(z, z, z, z, o, do_a, states, lbv, hnw)


def _attn_bwd(z, cc, ss, ob, lse, do_bg):
    def body(q_ref, k_ref, v_ref, ag_ref, cc_ref, ss_ref, ob_ref, lse_ref, dobg_ref,
             dq_ref, dk_ref, dv_ref, dag_ref,
             tmp, qs, ks, vs, dos, dqs, dks, dvs, dkp, dvp, do_t, ls0_t, ls1_t, dl0_t, dl1_t, ls0, ls1, dl0, dl1):
        g = pl.program_id(1)
        first_half = _half_mask()
        prev_ok, cur_ok = _attn_masks()
        lane = lax.broadcasted_iota(jnp.int32, (1, LANES), 1)
        heads = (lane < 64, lane >= 64)
        nblk = _group_blocks(g)
        cc_v, ss_v = cc_ref[...], ss_ref[...]

        @pl.when(g == 0)
        def _():
            ag, obv, dobg = ag_ref[...], ob_ref[...], dobg_ref[...]
            dag_ref[...] = (dobg * obv * _dsilu(ag)).astype(BF16)
            dob = dobg * _silu(ag)
            do_t[...] = dob
            prod = dob * obv
            dl = jnp.concatenate(
                [jnp.broadcast_to(jnp.sum(prod[:, 0:64], axis=-1, keepdims=True), (S, 64)),
                 jnp.broadcast_to(jnp.sum(prod[:, 64:128], axis=-1, keepdims=True), (S, 64))], axis=1)
            dl_sw = pltpu.roll(dl, 64, 1)
            dl0_t[...] = jnp.where(heads[0], dl, dl_sw)
            dl1_t[...] = jnp.where(heads[0], dl_sw, dl)
            ls = lse_ref[...]
            ls_sw = pltpu.roll(ls, 64, 1)
            ls0_t[...] = jnp.where(heads[0], ls, ls_sw)
            ls1_t[...] = jnp.where(heads[0], ls_sw, ls)
            ks[0:ATT_PAD, :] = jnp.zeros((ATT_PAD, LANES), BF16)
            vs[0:ATT_PAD, :] = jnp.zeros((ATT_PAD, LANES), BF16)

        tmp[...] = _rope(q_ref[...], cc_v, ss_v, first_half) * ATT_SCALE
        _to_residues_dyn(g, qs, tmp)
        tmp[...] = _rope(k_ref[...], cc_v, ss_v, first_half)
        _to_residues_dyn(g, ks, tmp, ATT_PAD, BF16)
        _to_residues_dyn(g, vs, v_ref, ATT_PAD, BF16)
        _to_residues_dyn(g, dos, do_t)
        _to_residues_dyn(g, ls0, ls0_t)
        _to_residues_dyn(g, ls1, ls1_t)
        _to_residues_dyn(g, dl0, dl0_t)
        _to_residues_dyn(g, dl1, dl1_t)
        lss, dls = (ls0, ls1), (dl0, dl1)

        def unit(u, carry):
            start = pl.multiple_of(u * 128, 128)
            cur = pl.ds(start, 128)
            both = pl.ds(start, 256)
            pm = prev_ok & ((u & (nblk - 1)) != 0)
            qu, dou = qs[cur, :], dos[cur, :]
            kcat, vcat = ks[both, :], vs[both, :]
            dq_u = None
            q_l, do_l, ds_l, p_l = [], [], [], []
            for hh in range(2):
                q_h = jnp.where(heads[hh], qu, 0.0).astype(BF16)
                do_h = jnp.where(heads[hh], dou, 0.0).astype(BF16)
                s = _dot_nt(q_h, kcat)
                dp = _dot_nt(do_h, vcat)
                lse_h, dl_h = lss[hh][cur, :], dls[hh][cur, :]
                pp = jnp.where(pm, jnp.exp(s[:, 0:128] - lse_h), 0.0)
                pc = jnp.where(cur_ok, jnp.exp(s[:, 128:256] - lse_h), 0.0)
                ds = jnp.concatenate([pp * (dp[:, 0:128] - dl_h), pc * (dp[:, 128:256] - dl_h)], axis=1).astype(BF16)
                dq = _dot(ds, kcat)
                dq_u = dq if hh == 0 else jnp.where(heads[1], dq, dq_u)
                q_l.append(q_h)
                do_l.append(do_h)
                ds_l.append(ds)
                p_l.append(jnp.concatenate([pp, pc], axis=1).astype(BF16))
            dkcat = _dot_tn(jnp.concatenate(ds_l, axis=0), jnp.concatenate(q_l, axis=0))
            dvcat = _dot_tn(jnp.concatenate(p_l, axis=0), jnp.concatenate(do_l, axis=0))
            dkp[cur, :] = dkcat[0:128]
            dks[cur, :] = dkcat[128:256]
            dvp[cur, :] = dvcat[0:128]
            dvs[cur, :] = dvcat[128:256]
            dqs[cur, :] = dq_u
            return carry

        lax.fori_loop(0, 16, unit, 0, unroll=ATT_UNROLL)
        dks[0:S - 128, :] += dkp[128:S, :]
        dvs[0:S - 128, :] += dvp[128:S, :]
        _from_residues_dyn(g, tmp, dqs)
        dq_ref[0] = (_rope(tmp[...], cc_v, -ss_v, first_half) * ATT_SCALE).astype(BF16)
        _from_residues_dyn(g, tmp, dks)
        dk_ref[0] = _rope(tmp[...], cc_v, -ss_v, first_half).astype(BF16)
        _from_residues_dyn(g, tmp, dvs)
        dv_ref[0] = tmp[...].astype(BF16)

    grp = pl.BlockSpec((1, S, LANES), lambda p, g: (g, 0, p))
    buf = pltpu.VMEM((S, LANES), F32)
    padded_b = pltpu.VMEM((ATT_PAD + S, LANES), BF16)
    return pl.pallas_call(
        body, name="attn_bwd", grid=(4, 3),
        out_shape=[jax.ShapeDtypeStruct((3, S, 512), BF16)] * 3 + [jax.ShapeDtypeStruct((S, 512), BF16)],
        in_specs=_attn_in_specs(3), out_specs=[grp, grp, grp, pl.BlockSpec((S, LANES), lambda p, g: (0, p))],
        scratch_shapes=[buf, buf, padded_b, padded_b] + [buf] * 15,
        compiler_params=_cp(("parallel", "arbitrary")),
    )(z, z, z, z, cc, ss, ob, lse, do_bg)


def _in_proj_bwd(dz, h, w_in):
    half = S // 2
    slab = (D, SHARD_COLS)

    def body(dz_hbm, h_hbm, w_hbm, dh_hbm, g_chip, r1_hbm, r2_hbm,
             h_buf, dz_buf, stage_d, r1_buf, stage_i, acc,
             dz_sem, w_sem, h_sem, r1_sem, out_sem, send_d, recv_d, send_i, recv_i):
        x, y, c = _mesh_pos()
        sibling = (x, y, 1 - c)
        chips = [(1 - x, y), (x, 1 - y), (1 - x, 1 - y), (x, y)]

        def cols(d):
            return pl.ds(pl.multiple_of(d * SHARD_COLS, LANES), SHARD_COLS)

        blocks = []
        for q in chips:
            blocks += [4 * q[0] + 2 * q[1] + (1 - c), 4 * q[0] + 2 * q[1] + c]

        def dz_tile(t):
            return pltpu.make_async_copy(dz_hbm.at[pl.ds((t % 2) * half, half), cols(blocks[t // 2])],
                                         dz_buf.at[t % 2], dz_sem.at[t % 2])

        def to_sibling(i):
            return pltpu.make_async_remote_copy(
                src_ref=stage_d.at[i % 2], dst_ref=r1_hbm.at[i], send_sem=send_d.at[i], recv_sem=recv_d.at[i],
                device_id=sibling, device_id_type=MESH)

        def to_owner(i):
            return pltpu.make_async_remote_copy(
                src_ref=stage_i.at[i], dst_ref=r2_hbm.at[i], send_sem=send_i.at[i], recv_sem=recv_i.at[i],
                device_id=(*chips[i], c), device_id_type=MESH)

        h_copy = pltpu.make_async_copy(h_hbm, h_buf, h_sem)
        h_copy.start()
        dz_tile(0).start()
        h_copy.wait()
        for b in range(8):
            i = b // 2
            if b % 2 == 1:
                to_sibling(i).wait_recv()
                r1_copy = pltpu.make_async_copy(r1_hbm.at[i], r1_buf, r1_sem)
                r1_copy.start()
            g = None
            for r in range(2):
                t = 2 * b + r
                if t + 1 < 16:
                    dz_tile(t + 1).start()
                dz_tile(t).wait()
                part = _dot_tn(h_buf[r * half:(r + 1) * half, :], dz_buf[t % 2])
                g = part if g is None else g + part
            if b % 2 == 0:
                if i >= 2:
                    to_sibling(i - 2).wait_send()
                stage_d[i % 2] = g.astype(BF16)
                to_sibling(i).start()
            else:
                r1_copy.wait()
                g = g + r1_buf[...].astype(F32)
                if i < 3:
                    stage_i[i] = g.astype(BF16)
                    to_owner(i).start()
                else:
                    g_chip[...] = g
        to_sibling(2).wait_send()
        to_sibling(3).wait_send()

        def dz2(t):
            return pltpu.make_async_copy(
                dz_hbm.at[pl.ds((t // 8) * half, half), pl.ds((t % 8) * SHARD_COLS, SHARD_COLS)],
                dz_buf.at[t % 2], dz_sem.at[t % 2])

        def w2(t):
            return pltpu.make_async_copy(w_hbm.at[:, pl.ds((t % 8) * SHARD_COLS, SHARD_COLS)],
                                         stage_d.at[t % 2], w_sem.at[t % 2])

        def dh_out(r):
            return pltpu.make_async_copy(acc, dh_hbm.at[pl.ds(r * half, half), :], out_sem)

        dz2(0).start()
        w2(0).start()
        for t in range(16):
            if t + 1 < 16:
                dz2(t + 1).start()
                w2(t + 1).start()
            dz2(t).wait()
            w2(t).wait()
            part = _dot_nt(dz_buf[t % 2], stage_d[t % 2])
            if t % 8 == 0:
                if t > 0:
                    dh_out(0).wait()
                acc[...] = part
            else:
                acc[...] += part
            if t % 8 == 7:
                dh_out(t // 8).start()
        dh_out(1).wait()
        for i in range(3):
            to_owner(i).wait_send()
        for i in range(3):
            to_owner(i).wait_recv()

    any_spec = pl.BlockSpec(memory_space=pl.ANY)
    return pl.pallas_call(
        body, name="in_proj_bwd",
        out_shape=[jax.ShapeDtypeStruct((S, D), F32), jax.ShapeDtypeStruct(slab, F32),
                   jax.ShapeDtypeStruct((4,) + slab, BF16), jax.ShapeDtypeStruct((3,) + slab, BF16)],
        in_specs=[any_spec] * 3,
        out_specs=[any_spec, pl.BlockSpec(memory_space=pltpu.VMEM), any_spec, any_spec],
        scratch_shapes=[pltpu.VMEM((S, D), BF16), pltpu.VMEM((2, half, SHARD_COLS), BF16),
                        pltpu.VMEM((2,) + slab, BF16), pltpu.VMEM(slab, BF16), pltpu.VMEM((3,) + slab, BF16),
                        pltpu.VMEM((half, D), F32),
                        pltpu.SemaphoreType.DMA((2,)), pltpu.SemaphoreType.DMA((2,)), pltpu.SemaphoreType.DMA,
                        pltpu.SemaphoreType.DMA, pltpu.SemaphoreType.DMA,
                        pltpu.SemaphoreType.DMA((4,)), pltpu.SemaphoreType.DMA((4,)),
                        pltpu.SemaphoreType.DMA((3,)), pltpu.SemaphoreType.DMA((3,))],
        compiler_params=_cp(),
    )(dz, h, w_in)


def _grad_x(x, norm_w, dh, dx2):
    tr = 256

    def body(x_ref, w_ref, dh_ref, dx2_ref, gx_ref, gnw_ref):
        @pl.when(pl.program_id(0) == 0)
        def _():
            gnw_ref[...] = jnp.zeros_like(gnw_ref)

        xv, dhv = x_ref[...], dh_ref[...]
        r = lax.rsqrt(jnp.mean(xv * xv, axis=-1, keepdims=True) + EPS)
        n = xv * r
        gnw_ref[...] += jnp.sum(dhv * n, axis=0, keepdims=True)
        dn = dhv * w_ref[...]
        gx_ref[...] = dx2_ref[...] + r * (dn - n * jnp.mean(dn * n, axis=-1, keepdims=True))

    row = pl.BlockSpec((tr, D), lambda i: (i, 0))
    vec = pl.BlockSpec((1, D), lambda i: (0, 0))
    return pl.pallas_call(
        body, name="grad_x", grid=(S // tr,),
        out_shape=[jax.ShapeDtypeStruct((S, D), F32), jax.ShapeDtypeStruct((1, D), F32)],
        in_specs=[row, vec, row, row], out_specs=[row, vec],
        compiler_params=_cp(("arbitrary",)),
    )(x, norm_w, dh, dx2)


def _rope_tables(positions):
    inv_freq = 10000.0 ** (-jnp.arange(0, 64, 2, dtype=F32) / 64)
    ang = positions.astype(F32)[:, None] * inv_freq[None, :]
    cos, sin = jnp.cos(ang), jnp.sin(ang)
    return jnp.tile(cos, (1, 4)), jnp.tile(jnp.concatenate([-sin, sin], axis=1), (1, 2))


def _local_step(x, positions, norm_w, lb_logits, hnw, fnw, target, w_in_shard, w_a, w_b, w_out):
    cc, ss = _rope_tables(positions)
    lbv = jax.nn.sigmoid(lb_logits[0:1] - lb_logits[1:2])
    h = _rmsnorm_in(x, norm_w)
    z, w_in = _in_proj_gather(h, w_in_shard)
    o, o_a, states = _hgrn_fwd(z, lbv, hnw)
    ob, lse, o_bg = _attn_fwd(z, cc, ss)
    dx2, dx2b, dgp, do_a, do_bg, merged, dy_a, dy_b, tail_small = _tail(x, o_a, o_bg, z, target, w_a, w_b, w_out, fnw)
    g_out, gb_out = _tn_matmul(merged, dx2b, "grad_w_out")
    g_a, gb_a = _tn_matmul(o_a, dy_a, "grad_w_a")
    g_b, gb_b = _tn_matmul(o_bg, dy_b, "grad_w_b")
    dhq, dhf, dhi, dhg, glb, ghn = _hgrn_bwd(z, o, do_a, states, lbv, hnw)
    dq, dk, dv, dag = _attn_bwd(z, cc, ss, ob, lse, do_bg)
    dz = jnp.concatenate([dhq, dhf, dhi, dhg, dq[0], dq[1], dq[2], dk[0], dk[1], dk[2], dv[0], dv[1], dv[2], dag, dgp],
                         axis=1)
    dh, g_chip_in, _, r2_in = _in_proj_bwd(dz, h, w_in)
    grad_x, gnw = _grad_x(x, norm_w, dh, dx2)
    ghn_row = jnp.pad(jnp.sum(ghn, axis=0), ((0, 0), (0, D - 128)))
    small = jnp.concatenate([gnw, glb, ghn_row, tail_small[0:2], jnp.zeros((3, D), F32)], axis=0)
    return grad_x, (g_chip_in, r2_in), (g_a, g_b, g_out), (gb_a, gb_b, gb_out), small


def kernel(x, positions, norm_w, w_in, lb_logits, hgrn_norm_w, w_branch_a, w_branch_b, w_out, final_norm_w, loss_target, m_norm_w, m_w_in, m_lb_logits, m_hgrn_norm_w, m_w_branch_a, m_w_branch_b, m_w_out, m_final_norm_w, v_norm_w, v_w_in, v_lb_logits, v_hgrn_norm_w, v_w_branch_a, v_w_branch_b, v_w_out, v_final_norm_w):
    ix, iy, ic = _mesh_pos()
    core = jnp.reshape(ic, (1,)).astype(jnp.int32)
    pos = jnp.stack([4 * ix + 2 * iy + ic, 2 * ix + iy]).astype(jnp.int32)

    shards = [w_in[0], w_branch_a[0], w_branch_b[0], w_out[0]]
    moments_m = [m_w_in[0], m_w_branch_a[0], m_w_branch_b[0], m_w_out[0]]
    moments_v = [v_w_in[0], v_w_branch_a[0], v_w_branch_b[0], v_w_out[0]]
    names = ("w_in", "w_a", "w_b", "w_out")
    ids = (1, 2, 3)
    shards_b = [_cast_bf16(w, f"cast_{nm}") for w, nm in zip(shards, names)]
    full = _allgather_weights(ids, shards_b[1:])

    fnw2 = final_norm_w.reshape(1, D)
    grad_x, (g_chip_in, r2_in), grads, gb, small = _local_step(
        x[0], positions[0], norm_w, lb_logits, hgrn_norm_w, fnw2, loss_target[0], shards_b[0], *full)

    r1 = _exchange_sibling(ids, gb)
    pb = [_chip_partials(a, grads[i], r1[i], core) for i, a in enumerate(ids)]
    *r2, gathered = _exchange_chips(ids, pb, small)
    big = [_reduce_own_and_update(shards[0], moments_m[0], moments_v[0], g_chip_in, r2_in)]
    big += [_reduce_and_update(a, shards[a], moments_m[a], moments_v[a], grads[i], r1[i], r2[i], pos)
            for i, a in enumerate(ids)]
    sm = _small_update(gathered, norm_w, lb_logits, hgrn_norm_w, fnw2,
                       (m_norm_w, m_lb_logits, m_hgrn_norm_w, m_final_norm_w.reshape(1, D),
                        v_norm_w, v_lb_logits, v_hgrn_norm_w, v_final_norm_w.reshape(1, D)))
    loss = sm[0][0, 0]
    outs = [loss, grad_x[None]]
    for kind in range(4):
        s_nw, s_lb, s_hn, s_fn = sm[1 + 4 * kind:5 + 4 * kind]
        outs += [s_nw, big[0][kind][None], s_lb, s_hn, big[1][kind][None], big[2][kind][None],
                 big[3][kind][None], s_fn.reshape(D)]
    return tuple(outs)
```

```python
import functools

import jax
import jax.numpy as jnp
from jax import lax
from jax.experimental import pallas as pl
from jax.experimental.pallas import tpu as pltpu

F32 = jnp.float32
BF16 = jnp.bfloat16
MESH = pl.DeviceIdType.MESH

S = 2048
D = 1024
NDEV = 8
HEADS = 8
CHUNK = 64
SUB = 16
HBLK = 128
ATT_PAD = 128
ATT_UNROLL = 4
EXP_CLAMP = 80.0
EPS = 1e-6
IN_COLS = 11264
SHARD_COLS = IN_COLS // NDEV
ATT_DILS = (1, 4, 16)
ATT_SCALE = 64 ** -0.5
LANES = 128

ADAM_LR, ADAM_B1, ADAM_B2, ADAM_EPS, ADAM_WD, ADAM_STEP = 0.001, 0.9, 0.999, 1e-08, 0.01, 10

VMEM_LIMIT = 56 * 1024 * 1024


def _cp(sem=None, **kw):
    return pltpu.CompilerParams(dimension_semantics=sem, vmem_limit_bytes=VMEM_LIMIT, **kw)


def _dot(a, b):
    return jnp.dot(a, b, preferred_element_type=F32)


def _dot_nt(a, b):
    return lax.dot_general(a, b, (((1,), (1,)), ((), ())), preferred_element_type=F32)


def _dot_tn(a, b):
    return lax.dot_general(a, b, (((0,), (0,)), ((), ())), preferred_element_type=F32)


def _split2(x):
    hi = x.astype(BF16)
    lo = (x - hi.astype(F32)).astype(BF16)
    return hi, lo


def _split3(x):
    hi = x.astype(BF16)
    r = x - hi.astype(F32)
    mid = r.astype(BF16)
    lo = (r - mid.astype(F32)).astype(BF16)
    return hi, mid, lo


def _dot_ones(ones_bf16, x):
    hi, mid, lo = _split3(x)
    return _dot(ones_bf16, hi) + _dot(ones_bf16, mid) + _dot(ones_bf16, lo)


def _dot3(dotfn, a, b):
    ah, al = _split2(a)
    bh, bl = _split2(b)
    return dotfn(ah, bh) + dotfn(ah, bl) + dotfn(al, bh)


def _silu(x):
    return x * jax.nn.sigmoid(x)


def _dsilu(x):
    s = jax.nn.sigmoid(x)
    return s * (1.0 + x * (1.0 - s))


def _mesh_pos():
    return lax.axis_index("x"), lax.axis_index("y"), lax.axis_index("c")


def _shard_of(ref, a, d):
    if a == 0:
        return ref.at[:, pl.ds(pl.multiple_of(d * SHARD_COLS, LANES), SHARD_COLS)]
    if a == 2:
        return ref.at[:, pl.ds(pl.multiple_of(d * LANES, LANES), LANES)]
    return ref.at[pl.ds(pl.multiple_of(d * 128, 128), 128), :]


FULL_SHAPES = ((D, IN_COLS), (D, D), (512, D), (D, D))
SHARD_SHAPES = ((D, SHARD_COLS), (128, D), (512, 128), (128, D))


def _allgather_weights(ids, shards):
    n = len(shards)

    def body(*refs):
        ins, outs = refs[:n], refs[n:2 * n]
        send_sems, recv_sems, local_sems = refs[2 * n:]
        x, y, c = _mesh_pos()
        me, sibling = (x, y, c), (x, y, 1 - c)
        chips = [(1 - x, y), (x, 1 - y), (1 - x, 1 - y)]

        def blk(a, p):
            return _shard_of(outs[a], ids[a], 4 * p[0] + 2 * p[1] + p[2])

        def copy(a, k, block, to, src=None):
            return pltpu.make_async_remote_copy(
                src_ref=blk(a, block) if src is None else src, dst_ref=blk(a, block),
                send_sem=send_sems.at[a * 7 + k], recv_sem=recv_sems.at[a * 7 + k],
                device_id=to, device_id_type=MESH)

        mine = [pltpu.make_async_copy(ins[a], blk(a, me), local_sems.at[a]) for a in range(n)]
        for cp in mine:
            cp.start()
        first = []
        for a in range(n):
            first += [copy(a, 1 + j, me, (*chip, c), src=ins[a]) for j, chip in enumerate(chips)]
        for a in range(n):
            first.append(copy(a, 0, me, sibling, src=ins[a]))
        for cp in first:
            cp.start()
        passed = []
        for j, chip in enumerate(chips):
            for a in range(n):
                copy(a, 1 + j, (*chip, c), me).wait_recv()
                fwd = copy(a, 4 + j, (*chip, c), sibling)
                fwd.start()
                passed.append(fwd)
        for a in range(n):
            copy(a, 0, sibling, me).wait_recv()
        for j, chip in enumerate(chips):
            for a in range(n):
                copy(a, 4 + j, (*chip, 1 - c), me).wait_recv()
        for cp in first + passed:
            cp.wait_send()
        for cp in mine:
            cp.wait()

    any_spec = pl.BlockSpec(memory_space=pl.ANY)
    return pl.pallas_call(
        body, name="allgather_weights",
        out_shape=[jax.ShapeDtypeStruct(FULL_SHAPES[a], BF16) for a in ids],
        in_specs=[any_spec] * n, out_specs=[any_spec] * n,
        scratch_shapes=[pltpu.SemaphoreType.DMA((7 * n,)), pltpu.SemaphoreType.DMA((7 * n,)),
                        pltpu.SemaphoreType.DMA((n,))],
    )(*shards)


def _in_proj_gather(h, w_shard):
    half = S // 2
    slab = (D, SHARD_COLS)

    def body(h_hbm, w_hbm, z_hbm, wfull_hbm, h_buf, land, zstage,
             h_sem, own_sem, z_sem, wout_sem, send_sems, recv_sems):
        x, y, c = _mesh_pos()
        sibling = (x, y, 1 - c)
        north = c == 1

        def chips_of(first_x):
            near = (jnp.where(first_x, 1 - x, x), jnp.where(first_x, y, 1 - y))
            far = (jnp.where(first_x, x, 1 - x), jnp.where(first_x, 1 - y, y))
            return [near, far, (1 - x, 1 - y)]

        mine, theirs = chips_of(north), chips_of(jnp.logical_not(north))

        def dev(chip, core):
            return 4 * chip[0] + 2 * chip[1] + core

        block_of = ([dev((x, y), c), dev((x, y), 1 - c)] + [dev(q, c) for q in mine]
                    + [dev(q, 1 - c) for q in theirs])

        def cols(d):
            if isinstance(d, int):
                return pl.ds(d * SHARD_COLS, SHARD_COLS)
            return pl.ds(pl.multiple_of(d * SHARD_COLS, LANES), SHARD_COLS)

        def send(k, src, dst_slot, to):
            return pltpu.make_async_remote_copy(
                src_ref=src, dst_ref=land.at[dst_slot], send_sem=send_sems.at[k], recv_sem=recv_sems.at[k],
                device_id=to, device_id_type=MESH)

        def to_sibling():
            return send(0, w_hbm, 1, sibling)

        def to_chip(j):
            if j == 2:
                return send(3, land.at[2], 4, (*mine[1], c))
            return send(1 + j, w_hbm, 2 + j, (*mine[j], c))

        def pass_on(j):
            return send(4 + j, land.at[2 + j], 5 + j, sibling)

        own = pltpu.make_async_copy(w_hbm, land.at[0], own_sem)
        h_copy = pltpu.make_async_copy(h_hbm, h_buf, h_sem)
        own.start()
        h_copy.start()
        to_sibling().start()
        to_chip(0).start()
        h_copy.wait()
        own.wait()

        def multiply(slot, n_done):
            d = block_of[slot]
            out = pltpu.make_async_copy(land.at[slot], wfull_hbm.at[:, cols(d)], wout_sem.at[slot])
            out.start()
            for r in range(2):
                rows = pl.ds(r * half, half)
                zc = pltpu.make_async_copy(zstage.at[r], z_hbm.at[rows, cols(d)], z_sem.at[r])
                if n_done > 0:
                    zc.wait()
                zstage[r] = _dot(h_buf[r * half:(r + 1) * half, :], land[slot])
                zc.start()
            return out

        outs = [multiply(0, 0)]
        to_sibling().wait_recv()
        outs.append(multiply(1, 1))
        done = 2
        for j in range(3):
            to_chip(j).wait_recv()
            pass_on(j).start()
            to_chip(j).wait_send()
            if j < 2:
                to_chip(j + 1).start()
            outs.append(multiply(2 + j, done))
            pass_on(j).wait_recv()
            outs.append(multiply(5 + j, done + 1))
            done += 2
        for r in range(2):
            pltpu.make_async_copy(zstage.at[r], z_hbm.at[pl.ds(r * half, half), cols(0)], z_sem.at[r]).wait()
        for out in outs:
            out.wait()
        to_sibling().wait_send()
        for j in range(3):
            pass_on(j).wait_send()

    any_spec = pl.BlockSpec(memory_space=pl.ANY)
    return pl.pallas_call(
        body, name="in_proj_gather",
        out_shape=[jax.ShapeDtypeStruct((S, IN_COLS), F32), jax.ShapeDtypeStruct((D, IN_COLS), BF16)],
        in_specs=[any_spec] * 2, out_specs=[any_spec] * 2,
        scratch_shapes=[pltpu.VMEM((S, D), BF16), pltpu.VMEM((8,) + slab, BF16), pltpu.VMEM((2, half, SHARD_COLS), F32),
                        pltpu.SemaphoreType.DMA, pltpu.SemaphoreType.DMA, pltpu.SemaphoreType.DMA((2,)),
                        pltpu.SemaphoreType.DMA((8,)), pltpu.SemaphoreType.DMA((7,)), pltpu.SemaphoreType.DMA((7,))],
        compiler_params=_cp(),
    )(h, w_shard)


def _exchange_sibling(ids, gb):
    n = len(gb)

    def body(*refs):
        ins, outs = refs[:n], refs[n:2 * n]
        send_sems, recv_sems = refs[2 * n:]
        x, y, c = _mesh_pos()
        sibling = (x, y, 1 - c)
        copies = []
        for i, a in enumerate(ids):
            for q in range(4):
                copies.append(pltpu.make_async_remote_copy(
                    src_ref=_shard_of(ins[i], a, 2 * q + (1 - c)), dst_ref=outs[i].at[q],
                    send_sem=send_sems.at[i * 4 + q], recv_sem=recv_sems.at[i * 4 + q],
                    device_id=sibling, device_id_type=MESH))
        for cp in copies:
            cp.start()
        for cp in copies:
            cp.wait()

    any_spec = pl.BlockSpec(memory_space=pl.ANY)
    return pl.pallas_call(
        body, name="grads_to_sibling",
        out_shape=[jax.ShapeDtypeStruct((4,) + SHARD_SHAPES[a], BF16) for a in ids],
        in_specs=[any_spec] * n, out_specs=[any_spec] * n,
        scratch_shapes=[pltpu.SemaphoreType.DMA((4 * n,)), pltpu.SemaphoreType.DMA((4 * n,))],
    )(*gb)


def _exchange_chips(ids, pb, small):
    n = len(pb)

    def body(*refs):
        ins, small_ref = refs[:n], refs[n]
        outs, small_out = refs[n + 1:2 * n + 1], refs[2 * n + 1]
        send_sems, recv_sems, ssend, srecv, local_sem = refs[2 * n + 2:]
        x, y, c = _mesh_pos()
        chips = [(1 - x, y), (x, 1 - y), (1 - x, 1 - y)]
        me = 4 * x + 2 * y + c
        copies = []
        for a in range(n):
            for k, chip in enumerate(chips):
                copies.append(pltpu.make_async_remote_copy(
                    src_ref=ins[a].at[2 * chip[0] + chip[1]], dst_ref=outs[a].at[k],
                    send_sem=send_sems.at[a * 3 + k], recv_sem=recv_sems.at[a * 3 + k],
                    device_id=(*chip, c), device_id_type=MESH))
        for r in range(1, NDEV):
            peer = (1 - x if r & 4 else x, 1 - y if r & 2 else y, 1 - c if r & 1 else c)
            copies.append(pltpu.make_async_remote_copy(
                src_ref=small_ref, dst_ref=small_out.at[me],
                send_sem=ssend.at[r - 1], recv_sem=srecv.at[r - 1],
                device_id=peer, device_id_type=MESH))
        own = pltpu.make_async_copy(small_ref, small_out.at[me], local_sem)
        own.start()
        for cp in copies:
            cp.start()
        for cp in copies:
            cp.wait()
        own.wait()

    any_spec = pl.BlockSpec(memory_space=pl.ANY)
    return pl.pallas_call(
        body, name="grads_between_chips",
        out_shape=[jax.ShapeDtypeStruct((3,) + SHARD_SHAPES[a], BF16) for a in ids]
        + [jax.ShapeDtypeStruct((NDEV,) + small.shape, F32)],
        in_specs=[any_spec] * (n + 1), out_specs=[any_spec] * (n + 1),
        scratch_shapes=[pltpu.SemaphoreType.DMA((3 * n,)), pltpu.SemaphoreType.DMA((3 * n,)),
                        pltpu.SemaphoreType.DMA((NDEV - 1,)), pltpu.SemaphoreType.DMA((NDEV - 1,)),
                        pltpu.SemaphoreType.DMA],
    )(*pb, small)


def _shard_tiles(a):
    rows, cols = SHARD_SHAPES[a]
    tr = min(rows, 256)
    return (tr, cols), rows // tr


def _full_index(a, d, i):
    (tr, _), nt = _shard_tiles(a)
    if a in (0, 2):
        return (i, d)
    return (d * nt + i, 0)


def _cast_bf16(x, name):
    rows, cols = x.shape
    tr = min(rows, 256)

    def body(x_ref, o_ref):
        o_ref[...] = x_ref[...].astype(BF16)

    return pl.pallas_call(
        body, name=name, out_shape=jax.ShapeDtypeStruct(x.shape, BF16), grid=(rows // tr,),
        in_specs=[pl.BlockSpec((tr, cols), lambda i: (i, 0))],
        out_specs=pl.BlockSpec((tr, cols), lambda i: (i, 0)),
        compiler_params=_cp(("parallel",)),
    )(x)


def _chip_partials(a, g_full, r1, core):
    tile, nt = _shard_tiles(a)

    def body(c_ref, g_ref, r_ref, o_ref):
        o_ref[0] = (g_ref[...] + r_ref[0].astype(F32)).astype(BF16)

    grid_spec = pltpu.PrefetchScalarGridSpec(
        num_scalar_prefetch=1, grid=(4, nt),
        in_specs=[pl.BlockSpec(tile, lambda q, i, c: _full_index(a, 2 * q + c[0], i)),
                  pl.BlockSpec((1,) + tile, lambda q, i, c: (q, i, 0))],
        out_specs=pl.BlockSpec((1,) + tile, lambda q, i, c: (q, i, 0)))
    return pl.pallas_call(
        body, name=f"chip_partials_{a}", grid_spec=grid_spec,
        out_shape=jax.ShapeDtypeStruct((4,) + SHARD_SHAPES[a], BF16),
        compiler_params=_cp(("parallel", "parallel")),
    )(core, g_full, r1)


def _adam(w, g, m, v):
    m = ADAM_B1 * m + (1.0 - ADAM_B1) * g
    v = ADAM_B2 * v + (1.0 - ADAM_B2) * (g * g)
    m_hat = m / (1.0 - ADAM_B1 ** ADAM_STEP)
    v_hat = v / (1.0 - ADAM_B2 ** ADAM_STEP)
    delta = -ADAM_LR * (m_hat / (jnp.sqrt(v_hat) + ADAM_EPS) + ADAM_WD * w)
    return delta, m, v


def _reduce_and_update(a, w, m, v, g_full, r1, r2, pos):
    tile, nt = _shard_tiles(a)

    def body(p_ref, w_ref, m_ref, v_ref, g_ref, r1_ref, r2_ref, go_ref, do_ref, mo_ref, vo_ref):
        g = g_ref[...] + r1_ref[0].astype(F32)
        g = g + r2_ref[0].astype(F32)
        g = g + r2_ref[1].astype(F32)
        g = g + r2_ref[2].astype(F32)
        delta, m_new, v_new = _adam(w_ref[...], g, m_ref[...], v_ref[...])
        go_ref[...] = g
        do_ref[...] = delta
        mo_ref[...] = m_new
        vo_ref[...] = v_new

    own = pl.BlockSpec(tile, lambda i, p: (i, 0))
    grid_spec = pltpu.PrefetchScalarGridSpec(
        num_scalar_prefetch=1, grid=(nt,),
        in_specs=[own, own, own,
                  pl.BlockSpec(tile, lambda i, p: _full_index(a, p[0], i)),
                  pl.BlockSpec((1,) + tile, lambda i, p: (p[1], i, 0)),
                  pl.BlockSpec((3,) + tile, lambda i, p: (0, i, 0))],
        out_specs=[own] * 4)
    shp = jax.ShapeDtypeStruct(w.shape, F32)
    return pl.pallas_call(
        body, name=f"reduce_update_{a}", grid_spec=grid_spec, out_shape=[shp] * 4,
        compiler_params=_cp(("parallel",)),
    )(pos, w, m, v, g_full, r1, r2)


def _reduce_own_and_update(w, m, v, g_chip, r2):
    tile, nt = _shard_tiles(0)

    def body(w_ref, m_ref, v_ref, g_ref, r2_ref, go_ref, do_ref, mo_ref, vo_ref):
        g = g_ref[...] + r2_ref[0].astype(F32)
        g = g + r2_ref[1].astype(F32)
        delta, m_new, v_new = _adam(w_ref[...], g, m_ref[...], v_ref[...])
        go_ref[...] = g
        do_ref[...] = delta
        mo_ref[...] = m_new
        vo_ref[...] = v_new

    own = pl.BlockSpec(tile, lambda i: (i, 0))
    shp = jax.ShapeDtypeStruct(w.shape, F32)
    return pl.pallas_call(
        body, name="reduce_update_0", grid=(nt,), out_shape=[shp] * 4,
        in_specs=[own, own, own, own, pl.BlockSpec((2,) + tile, lambda i: (0, i, 0))], out_specs=[own] * 4,
        compiler_params=_cp(("parallel",)),
    )(w, m, v, g_chip, r2)


def _small_update(gathered, norm_w, lb_logits, hnw, fnw, moments):
    m_nw, m_lb, m_hn, m_fn, v_nw, v_lb, v_hn, v_fn = moments

    def body(g_ref, nw, lb, hn, fn, mnw, mlb, mhn, mfn, vnw, vlb, vhn, vfn,
             loss_o, g_nw, g_lb, g_hn, g_fn, d_nw, d_lb, d_hn, d_fn,
             mo_nw, mo_lb, mo_hn, mo_fn, vo_nw, vo_lb, vo_hn, vo_fn):
        tot = g_ref[0]
        for d in range(1, NDEV):
            tot = tot + g_ref[d]
        loss_o[...] = tot[4:5, 0:LANES]
        logits = lb[...]
        lbv = jax.nn.sigmoid(logits[0:1] - logits[1:2])
        chain = tot[1:2] * lbv * (1.0 - lbv)
        grads = (tot[0:1], jnp.concatenate([chain, -chain], axis=0), tot[2:3, 0:LANES], tot[3:4])
        outs = ((nw, mnw, vnw, g_nw, d_nw, mo_nw, vo_nw), (lb, mlb, vlb, g_lb, d_lb, mo_lb, vo_lb),
                (hn, mhn, vhn, g_hn, d_hn, mo_hn, vo_hn), (fn, mfn, vfn, g_fn, d_fn, mo_fn, vo_fn))
        for g, (w_r, m_r, v_r, g_o, d_o, m_o, v_o) in zip(grads, outs):
            delta, m_new, v_new = _adam(w_r[...], g, m_r[...], v_r[...])
            g_o[...] = g
            d_o[...] = delta
            m_o[...] = m_new
            v_o[...] = v_new

    shapes = [norm_w.shape, lb_logits.shape, hnw.shape, fnw.shape]
    out_shape = [jax.ShapeDtypeStruct((1, LANES), F32)] + [jax.ShapeDtypeStruct(s, F32) for s in shapes] * 4
    return pl.pallas_call(body, name="small_update", out_shape=out_shape, compiler_params=_cp())(
        gathered, norm_w, lb_logits, hnw, fnw, m_nw, m_lb, m_hn, m_fn, v_nw, v_lb, v_hn, v_fn)


def _rmsnorm_in(x, norm_w):
    tr = 512

    def body(x_ref, w_ref, h_ref):
        xv = x_ref[...]
        r = lax.rsqrt(jnp.mean(xv * xv, axis=-1, keepdims=True) + EPS)
        h_ref[...] = (xv * r * w_ref[...]).astype(BF16)

    return pl.pallas_call(
        body, name="rmsnorm_in", out_shape=jax.ShapeDtypeStruct((S, D), BF16), grid=(S // tr,),
        in_specs=[pl.BlockSpec((tr, D), lambda i: (i, 0)), pl.BlockSpec((1, D), lambda i: (0, 0))],
        out_specs=pl.BlockSpec((tr, D), lambda i: (i, 0)),
        compiler_params=_cp(("parallel",)),
    )(x, norm_w)


def _in_proj(h, w_in):
    tn = 1024

    def body(h_ref, w_ref, z_ref):
        z_ref[...] = _dot(h_ref[...], w_ref[...])

    return pl.pallas_call(
        body, name="in_proj", out_shape=jax.ShapeDtypeStruct((S, IN_COLS), F32), grid=(IN_COLS // tn,),
        in_specs=[pl.BlockSpec((S, D), lambda j: (0, 0)), pl.BlockSpec((D, tn), lambda j: (0, j))],
        out_specs=pl.BlockSpec((S, tn), lambda j: (0, j)),
        compiler_params=_cp(("parallel",)),
    )(h, w_in)


def _block_tri(n, block, upper=False):
    r = lax.broadcasted_iota(jnp.int32, (n, n), 0)
    c = lax.broadcasted_iota(jnp.int32, (n, n), 1)
    keep = (c >= r) if upper else (c <= r)
    return jnp.where(keep & ((r // block) == (c // block)), 1.0, 0.0).astype(BF16)


def _tril_mask(n):
    r = lax.broadcasted_iota(jnp.int32, (n, n), 0)
    c = lax.broadcasted_iota(jnp.int32, (n, n), 1)
    return c <= r


def _chunk_scores(q, k, b, bex, r0, mask):
    rows = slice(r0, r0 + CHUNK)
    parts, qs_l, ek_l, eq_l = [], [], [], []
    for i in range(CHUNK // SUB):
        ri = slice(r0 + SUB * i, r0 + SUB * (i + 1))
        base = bex[r0 + SUB * i:r0 + SUB * i + 1]
        eq = jnp.exp(b[ri] - base)
        ek = jnp.exp(jnp.minimum(base - b[rows], EXP_CLAMP))
        qs = q[ri] * eq
        parts.append(_dot_nt(qs.astype(BF16), (k[rows] * ek).astype(BF16)))
        qs_l.append(qs)
        ek_l.append(ek)
        eq_l.append(eq)
    return jnp.where(mask, jnp.concatenate(parts, axis=0), 0.0), qs_l, ek_l, eq_l


def _hgrn_cols(hq, hf, hi, lb):
    sg = jax.nn.sigmoid(hf)
    f = lb + (1.0 - lb) * sg
    g = jnp.log(f)
    b = _dot_ones(_block_tri(HBLK, CHUNK), g)
    return _silu(hq), 1.0 - f, g, hi, sg, f, b


def _hgrn_fwd(z, lbv, hnw):
    ntb, nch = S // HBLK, HBLK // CHUNK

    def body(hq_ref, hf_ref, hi_ref, hg_ref, lb_ref, hnw_ref, o_ref, oa_ref, st_ref, state):
        @pl.when(pl.program_id(0) == 0)
        def _():
            state[...] = jnp.zeros_like(state)

        q_a, k_a, g_a, v_a, _, _, b_a = _hgrn_cols(hq_ref[...], hf_ref[...], hi_ref[...], lb_ref[...])
        bex_a = b_a - g_a
        eb_a = jnp.exp(b_a)
        mask = _tril_mask(CHUNK)
        hg = hg_ref[...]
        w = hnw_ref[...]
        for h in range(HEADS):
            cols = slice(128 * h, 128 * h + 128)
            q, k, v, b, bex, eb = q_a[:, cols], k_a[:, cols], v_a[:, cols], b_a[:, cols], bex_a[:, cols], eb_a[:, cols]
            st = state[h]
            outs = []
            for c in range(nch):
                r0 = c * CHUNK
                rows = slice(r0, r0 + CHUNK)
                a, _, _, _ = _chunk_scores(q, k, b, bex, r0, mask)
                vb = v[rows].astype(BF16)
                b_last = b[r0 + CHUNK - 1:r0 + CHUNK]
                qe = (q[rows] * eb[rows]).astype(BF16)
                outs.append(_dot(a.astype(BF16), vb) + _dot_nt(qe, st.astype(BF16)))
                st_ref[h, c] = st
                ke = (k[rows] * jnp.exp(b_last - b[rows])).astype(BF16)
                st = st * jnp.exp(b_last) + _dot_tn(vb, ke)
            state[h] = st
            o = jnp.concatenate(outs, axis=0)
            o_ref[:, cols] = o
            r = lax.rsqrt(jnp.mean(o * o, axis=-1, keepdims=True) + EPS)
            oa_ref[:, cols] = (o * r * w * _silu(hg[:, cols])).astype(BF16)

    def zcol(j):
        return pl.BlockSpec((HBLK, D), lambda t: (t, j))

    out_blk = pl.BlockSpec((HBLK, D), lambda t: (t, 0))
    return pl.pallas_call(
        body, name="hgrn_fwd", grid=(ntb,),
        out_shape=[jax.ShapeDtypeStruct((S, D), F32), jax.ShapeDtypeStruct((S, D), BF16),
                   jax.ShapeDtypeStruct((HEADS, S // CHUNK, 128, 128), F32)],
        in_specs=[zcol(0), zcol(1), zcol(2), zcol(3),
                  pl.BlockSpec((1, D), lambda t: (0, 0)), pl.BlockSpec((1, 128), lambda t: (0, 0))],
        out_specs=[out_blk, out_blk, pl.BlockSpec((HEADS, nch, 128, 128), lambda t: (0, t, 0, 0))],
        scratch_shapes=[pltpu.VMEM((HEADS, 128, 128), F32)],
        compiler_params=_cp(("arbitrary",)),
    )(z, z, z, z, lbv, hnw)


def _half_mask():
    lane = lax.broadcasted_iota(jnp.int32, (1, LANES), 1)
    return (lane % 64) < 32


def _rope(t, cc, ss, first_half):
    partner = jnp.where(first_half, pltpu.roll(t, 96, 1), pltpu.roll(t, 32, 1))
    return t * cc + partner * ss


def _attn_masks():
    i = lax.broadcasted_iota(jnp.int32, (128, 128), 0)
    j = lax.broadcasted_iota(jnp.int32, (128, 128), 1)
    return j >= i, j <= i


def _to_residues_dyn(g, dst, src, row0=0, dtype=None):
    for gi, dil in enumerate((1, 4, 16)):
        m = S // dil

        @pl.when(g == gi)
        def _(dil=dil, m=m):
            for r in range(dil):
                v = src[...] if dil == 1 else src[pl.ds(r, m, stride=dil), :]
                if dtype is not None:
                    v = v.astype(dtype)
                dst[row0 + r * m:row0 + (r + 1) * m, 0:LANES] = v


def _from_residues_dyn(g, dst, src, row0=0):
    for gi, dil in enumerate((1, 4, 16)):
        m = S // dil

        @pl.when(g == gi)
        def _(dil=dil, m=m):
            for r in range(dil):
                v = src[row0 + r * m:row0 + (r + 1) * m, :]
                if dil == 1:
                    dst[...] = v
                else:
                    dst[pl.ds(r, m, stride=dil), :] = v


def _group_blocks(g):
    return jnp.where(g == 0, 16, jnp.where(g == 1, 4, 1))


def _attn_in_specs(extra):
    def zcol(off):
        return pl.BlockSpec((S, LANES), lambda p, g: (0, off + 4 * g + p))

    per_pair = pl.BlockSpec((S, LANES), lambda p, g: (0, p))
    const = pl.BlockSpec((S, LANES), lambda p, g: (0, 0))
    return [zcol(32), zcol(44), zcol(56), pl.BlockSpec((S, LANES), lambda p, g: (0, 68 + p)), const, const] + [per_pair] * extra


def _attn_fwd(z, cc, ss):
    def body(q_ref, k_ref, v_ref, ag_ref, cc_ref, ss_ref, ob_ref, lse_ref, obg_ref,
             tmp, qs, ks, vx, og, mg, lg, o_t, m_t, l_t, o_acc, m_acc, l_acc):
        g = pl.program_id(1)
        first_half = _half_mask()
        prev_ok, cur_ok = _attn_masks()
        lane = lax.broadcasted_iota(jnp.int32, (1, LANES), 1)
        heads = (lane < 64, lane >= 64)
        nblk = _group_blocks(g)

        @pl.when(g == 0)
        def _():
            ks[0:ATT_PAD, :] = jnp.zeros((ATT_PAD, LANES), BF16)
            vx[0:ATT_PAD, 0:LANES] = jnp.zeros((ATT_PAD, LANES), BF16)
            vx[:, LANES:2 * LANES] = jnp.ones((ATT_PAD + S, LANES), BF16)

        tmp[...] = _rope(q_ref[...], cc_ref[...], ss_ref[...], first_half) * ATT_SCALE
        _to_residues_dyn(g, qs, tmp)
        tmp[...] = _rope(k_ref[...], cc_ref[...], ss_ref[...], first_half)
        _to_residues_dyn(g, ks, tmp, ATT_PAD, BF16)
        _to_residues_dyn(g, vx, v_ref, ATT_PAD, BF16)

        def unit(u, carry):
            start = pl.multiple_of(u * 128, 128)
            cur = pl.ds(start, 128)
            pm = prev_ok & ((u & (nblk - 1)) != 0)
            qu = qs[cur, :]
            kcat = ks[pl.ds(start, 256), :]
            vext = vx[pl.ds(start, 256), :]
            o_u = m_u = l_u = None
            for hh in range(2):
                s = _dot_nt(jnp.where(heads[hh], qu, 0.0).astype(BF16), kcat)
                sp = jnp.where(pm, s[:, 0:128], -jnp.inf)
                sc = jnp.where(cur_ok, s[:, 128:256], -jnp.inf)
                m = jnp.max(jnp.maximum(sp, sc), axis=-1, keepdims=True)
                p = jnp.concatenate([jnp.exp(sp - m), jnp.exp(sc - m)], axis=1).astype(BF16)
                ol = _dot(p, vext)
                mb = jnp.broadcast_to(m, (128, LANES))
                if hh == 0:
                    o_u, l_u, m_u = ol[:, 0:128], ol[:, 128:256], mb
                else:
                    o_u = jnp.where(heads[1], ol[:, 0:128], o_u)
                    l_u = jnp.where(heads[1], ol[:, 128:256], l_u)
                    m_u = jnp.where(heads[1], mb, m_u)
            og[cur, :] = o_u
            mg[cur, :] = m_u
            lg[cur, :] = l_u
            return carry

        lax.fori_loop(0, 16, unit, 0, unroll=ATT_UNROLL)
        _from_residues_dyn(g, o_t, og)
        _from_residues_dyn(g, m_t, mg)
        _from_residues_dyn(g, l_t, lg)

        @pl.when(g == 0)
        def _():
            o_acc[...] = o_t[...]
            m_acc[...] = m_t[...]
            l_acc[...] = l_t[...]

        @pl.when(g > 0)
        def _():
            m_new = jnp.maximum(m_acc[...], m_t[...])
            wa, wb = jnp.exp(m_acc[...] - m_new), jnp.exp(m_t[...] - m_new)
            o_acc[...] = o_acc[...] * wa + o_t[...] * wb
            l_acc[...] = l_acc[...] * wa + l_t[...] * wb
            m_acc[...] = m_new

        @pl.when(g == 2)
        def _():
            ob = o_acc[...] / l_acc[...]
            ob_ref[...] = ob
            lse_ref[...] = m_acc[...] + jnp.log(l_acc[...])
            obg_ref[...] = (ob * _silu(ag_ref[...])).astype(BF16)

    blk = pl.BlockSpec((S, LANES), lambda p, g: (0, p))
    buf = pltpu.VMEM((S, LANES), F32)
    return pl.pallas_call(
        body, name="attn_fwd", grid=(4, 3),
        out_shape=[jax.ShapeDtypeStruct((S, 512), F32), jax.ShapeDtypeStruct((S, 512), F32),
                   jax.ShapeDtypeStruct((S, 512), BF16)],
        in_specs=_attn_in_specs(0), out_specs=[blk, blk, blk],
        scratch_shapes=[buf, buf, pltpu.VMEM((ATT_PAD + S, LANES), BF16), pltpu.VMEM((ATT_PAD + S, 2 * LANES), BF16)] + [buf] * 9,
        compiler_params=_cp(("parallel", "arbitrary")),
    )(z, z, z, z, cc, ss)


def _tail(x, o_a, o_bg, z, target, w_a, w_b, w_out, fnw):
    tm = 256

    def body(x_ref, oa_ref, ob_ref, gpa_ref, gpb_ref, t_ref, wa_ref, wb_ref, wo_ref, fnw_ref,
             dx2_ref, dx2b_ref, dgp_ref, doa_ref, dob_ref, mg_ref, dya_ref, dyb_ref, small_ref):
        @pl.when(pl.program_id(0) == 0)
        def _():
            small_ref[...] = jnp.zeros_like(small_ref)

        wa, wb, wo = wa_ref[...], wb_ref[...], wo_ref[...]
        y_a = _dot(oa_ref[...], wa)
        y_b = _dot(ob_ref[...], wb)
        ga = jax.nn.sigmoid(gpa_ref[...])
        gb = jax.nn.sigmoid(gpb_ref[...])
        merged = (ga * y_a + gb * y_b).astype(BF16)
        x2 = x_ref[...] + _dot(merged, wo)
        r2 = lax.rsqrt(jnp.mean(x2 * x2, axis=-1, keepdims=True) + EPS)
        n2 = x2 * r2
        fw = fnw_ref[...]
        err = n2 * fw - t_ref[...]
        loss = 0.5 * jnp.sum(jnp.sum(err * err, axis=-1, keepdims=True), axis=0, keepdims=True) / D
        dy = err * (1.0 / D)
        g_fnw = jnp.sum(dy * n2, axis=0, keepdims=True)
        dn = dy * fw
        dx2 = r2 * (dn - n2 * jnp.mean(dn * n2, axis=-1, keepdims=True))
        dx2b = dx2.astype(BF16)
        dmerged = _dot_nt(dx2b, wo)
        dy_a = (dmerged * ga).astype(BF16)
        dy_b = (dmerged * gb).astype(BF16)
        dx2_ref[...] = dx2
        dx2b_ref[...] = dx2b
        dgp_ref[:, 0:D] = (dmerged * y_a * ga * (1.0 - ga)).astype(BF16)
        dgp_ref[:, D:2 * D] = (dmerged * y_b * gb * (1.0 - gb)).astype(BF16)
        doa_ref[...] = _dot_nt(dy_a, wa)
        dob_ref[...] = _dot_nt(dy_b, wb)
        mg_ref[...] = merged
        dya_ref[...] = dy_a
        dyb_ref[...] = dy_b
        small_ref[0:1, :] += g_fnw
        small_ref[1:2, :] += jnp.broadcast_to(loss, (1, D))

    def rows(cols, off=0):
        return pl.BlockSpec((tm, cols), lambda i: (i, off))

    def whole(shape):
        return pl.BlockSpec(shape, lambda i: (0, 0))

    return pl.pallas_call(
        body, name="tail", grid=(S // tm,),
        out_shape=[jax.ShapeDtypeStruct((S, D), F32), jax.ShapeDtypeStruct((S, D), BF16),
                   jax.ShapeDtypeStruct((S, 2 * D), BF16), jax.ShapeDtypeStruct((S, D), F32),
                   jax.ShapeDtypeStruct((S, 512), F32), jax.ShapeDtypeStruct((S, D), BF16),
                   jax.ShapeDtypeStruct((S, D), BF16), jax.ShapeDtypeStruct((S, D), BF16),
                   jax.ShapeDtypeStruct((8, D), F32)],
        in_specs=[rows(D), rows(D), rows(512), rows(D, 9), rows(D, 10), rows(D),
                  whole((D, D)), whole((512, D)), whole((D, D)), whole((1, D))],
        out_specs=[rows(D), rows(D), rows(2 * D), rows(D), rows(512), rows(D), rows(D), rows(D), whole((8, D))],
        compiler_params=_cp(("arbitrary",)),
    )(x, o_a, o_bg, z, z, target, w_a, w_b, w_out, fnw)


def _tn_matmul(a, b, name):
    m, n = a.shape[1], b.shape[1]
    tn = 512

    def body(a_ref, b_ref, o_ref, ob_ref):
        acc = _dot_tn(a_ref[...], b_ref[...])
        o_ref[...] = acc
        ob_ref[...] = acc.astype(BF16)

    out_blk = pl.BlockSpec((m, tn), lambda j: (0, j))
    return pl.pallas_call(
        body, name=name, grid=(n // tn,),
        out_shape=[jax.ShapeDtypeStruct((m, n), F32), jax.ShapeDtypeStruct((m, n), BF16)],
        in_specs=[pl.BlockSpec((S, m), lambda j: (0, 0)), pl.BlockSpec((S, tn), lambda j: (0, j))],
        out_specs=[out_blk, out_blk],
        compiler_params=_cp(("parallel",)),
    )(a, b)


def _hgrn_bwd(z, o, do_a, states, lbv, hnw):
    ntb, nch = S // HBLK, HBLK // CHUNK

    def body(hq_ref, hf_ref, hi_ref, hg_ref, o_ref, doa_ref, st_ref, lb_ref, hnw_ref,
             dhq_ref, dhf_ref, dhi_ref, dhg_ref, glb_ref, ghn_ref, dstate):
        @pl.when(pl.program_id(0) == 0)
        def _():
            dstate[...] = jnp.zeros_like(dstate)
            glb_ref[...] = jnp.zeros_like(glb_ref)
            ghn_ref[...] = jnp.zeros_like(ghn_ref)

        lb_a = lb_ref[...]
        hq_a, hg_a = hq_ref[...], hg_ref[...]
        q_a, k_a, g_a, v_a, sg_a, f_a, b_a = _hgrn_cols(hq_a, hf_ref[...], hi_ref[...], lb_a)
        bex_a = b_a - g_a
        eb_a = jnp.exp(b_a)
        w = hnw_ref[...]
        mask = _tril_mask(CHUNK)
        upper = _block_tri(CHUNK, CHUNK, upper=True)
        for h in range(HEADS):
            cols = slice(128 * h, 128 * h + 128)
            q, k, v, b, bex, eb = q_a[:, cols], k_a[:, cols], v_a[:, cols], b_a[:, cols], bex_a[:, cols], eb_a[:, cols]
            hq, hg, sg, f, lb = hq_a[:, cols], hg_a[:, cols], sg_a[:, cols], f_a[:, cols], lb_a[:, cols]
            ov, doa = o_ref[:, cols], doa_ref[:, cols]
            r = lax.rsqrt(jnp.mean(ov * ov, axis=-1, keepdims=True) + EPS)
            n = ov * r
            sil = _silu(hg)
            dhg_ref[:, cols] = (doa * n * w * _dsilu(hg)).astype(BF16)
            ghn_ref[h] += jnp.sum(doa * sil * n, axis=0, keepdims=True)
            dn = doa * sil * w
            do = r * (dn - n * jnp.mean(dn * n, axis=-1, keepdims=True))

            dst = dstate[h]
            dq_l, dk_l, dv_l, dg_l = [None] * nch, [None] * nch, [None] * nch, [None] * nch
            for c in reversed(range(nch)):
                r0 = c * CHUNK
                rows = slice(r0, r0 + CHUNK)
                st = st_ref[h, c]
                bc, kc, qc = b[rows], k[rows], q[rows]
                vb, dob = v[rows].astype(BF16), do[rows].astype(BF16)
                b_last = bc[CHUNK - 1:CHUNK]
                e_last = jnp.exp(b_last)
                ekl = jnp.exp(b_last - bc)
                dstb = dst.astype(BF16)
                a, qs_l, ek_l, eq_l = _chunk_scores(q, k, b, bex, r0, mask)
                da = jnp.where(mask, _dot_nt(dob, vb), 0.0)
                dv_l[c] = _dot_tn(a.astype(BF16), dob) + _dot_nt((kc * ekl).astype(BF16), dstb)
                dq_inter = _dot(dob, st.astype(BF16)) * eb[rows]
                dk_state = _dot(vb, dstb) * ekl
                dq_parts, dk_intra = [], jnp.zeros((CHUNK, 128), F32)
                for i in range(CHUNK // SUB):
                    da_i = da[SUB * i:SUB * (i + 1)]
                    dq_parts.append(_dot3(_dot, da_i, kc * ek_l[i]) * eq_l[i])
                    dk_intra = dk_intra + _dot3(_dot_tn, da_i, qs_l[i]) * ek_l[i]
                dq = jnp.concatenate(dq_parts, axis=0) + dq_inter
                dk = dk_intra + dk_state
                last = (e_last * jnp.sum(st * dst, axis=0, keepdims=True)
                        + jnp.sum(kc * dk_state, axis=0, keepdims=True))
                dg_l[c] = _dot_ones(upper, qc * dq - kc * dk) + last
                dq_l[c], dk_l[c] = dq, dk
                dst = dst * e_last + _dot_tn(dob, (qc * eb[rows]).astype(BF16))
            dstate[h] = dst
            dq, dk = jnp.concatenate(dq_l, axis=0), jnp.concatenate(dk_l, axis=0)
            dg, dv = jnp.concatenate(dg_l, axis=0), jnp.concatenate(dv_l, axis=0)
            dhq_ref[:, cols] = (dq * _dsilu(hq)).astype(BF16)
            dhi_ref[:, cols] = dv.astype(BF16)
            df = dg / f - dk
            dhf_ref[:, cols] = (df * (1.0 - lb) * sg * (1.0 - sg)).astype(BF16)
            glb_ref[:, cols] += jnp.sum(df * (1.0 - sg), axis=0, keepdims=True)

    def rev(t):
        return ntb - 1 - t

    def zcol(j):
        return pl.BlockSpec((HBLK, D), lambda t: (rev(t), j))

    blk = pl.BlockSpec((HBLK, D), lambda t: (rev(t), 0))
    return pl.pallas_call(
        body, name="hgrn_bwd", grid=(ntb,),
        out_shape=[jax.ShapeDtypeStruct((S, D), BF16)] * 4
        + [jax.ShapeDtypeStruct((1, D), F32), jax.ShapeDtypeStruct((HEADS, 1, 128), F32)],
        in_specs=[zcol(0), zcol(1), zcol(2), zcol(3), blk, blk,
                  pl.BlockSpec((HEADS, nch, 128, 128), lambda t: (0, rev(t), 0, 0)),
                  pl.BlockSpec((1, D), lambda t: (0, 0)), pl.BlockSpec((1, 128), lambda t: (0, 0))],
        out_specs=[blk] * 4 + [pl.BlockSpec((1, D), lambda t: (0, 0)),
                               pl.BlockSpec((HEADS, 1, 128), lambda t: (0, 0, 0))],
        scratch_shapes=[pltpu.VMEM((HEADS, 128, 128), F32)],
        compiler_params=_cp(("arbitrary",)),
    )(z, z, z, z, o, do_a, states, lbv, hnw)


def _attn_bwd(z, cc, ss, ob, lse, do_bg):
    def body(q_ref, k_ref, v_ref, ag_ref, cc_ref, ss_ref, ob_ref, lse_ref, dobg_ref,
             dq_ref, dk_ref, dv_ref, dag_ref,
             tmp, qs, ks, vs, dos, dqs, dks, dvs, dkp, dvp, do_t, ls0_t, ls1_t, dl0_t, dl1_t, ls0, ls1, dl0, dl1):
        g = pl.program_id(1)
        first_half = _half_mask()
        prev_ok, cur_ok = _attn_masks()
        lane = lax.broadcasted_iota(jnp.int32, (1, LANES), 1)
        heads = (lane < 64, lane >= 64)
        nblk = _group_blocks(g)
        cc_v, ss_v = cc_ref[...], ss_ref[...]

        @pl.when(g == 0)
        def _():
            ag, obv, dobg = ag_ref[...], ob_ref[...], dobg_ref[...]
            dag_ref[...] = (dobg * obv * _dsilu(ag)).astype(BF16)
            dob = dobg * _silu(ag)
            do_t[...] = dob
            prod = dob * obv
            dl = jnp.concatenate(
                [jnp.broadcast_to(jnp.sum(prod[:, 0:64], axis=-1, keepdims=True), (S, 64)),
                 jnp.broadcast_to(jnp.sum(prod[:, 64:128], axis=-1, keepdims=True), (S, 64))], axis=1)
            dl_sw = pltpu.roll(dl, 64, 1)
            dl0_t[...] = jnp.where(heads[0], dl, dl_sw)
            dl1_t[...] = jnp.where(heads[0], dl_sw, dl)
            ls = lse_ref[...]
            ls_sw = pltpu.roll(ls, 64, 1)
            ls0_t[...] = jnp.where(heads[0], ls, ls_sw)
            ls1_t[...] = jnp.where(heads[0], ls_sw, ls)
            ks[0:ATT_PAD, :] = jnp.zeros((ATT_PAD, LANES), BF16)
            vs[0:ATT_PAD, :] = jnp.zeros((ATT_PAD, LANES), BF16)

        tmp[...] = _rope(q_ref[...], cc_v, ss_v, first_half) * ATT_SCALE
        _to_residues_dyn(g, qs, tmp)
        tmp[...] = _rope(k_ref[...], cc_v, ss_v, first_half)
        _to_residues_dyn(g, ks, tmp, ATT_PAD, BF16)
        _to_residues_dyn(g, vs, v_ref, ATT_PAD, BF16)
        _to_residues_dyn(g, dos, do_t)
        _to_residues_dyn(g, ls0, ls0_t)
        _to_residues_dyn(g, ls1, ls1_t)
        _to_residues_dyn(g, dl0, dl0_t)
        _to_residues_dyn(g, dl1, dl1_t)
        lss, dls = (ls0, ls1), (dl0, dl1)

        def unit(u, carry):
            start = pl.multiple_of(u * 128, 128)
            cur = pl.ds(start, 128)
            both = pl.ds(start, 256)
            pm = prev_ok & ((u & (nblk - 1)) != 0)
            qu, dou = qs[cur, :], dos[cur, :]
            kcat, vcat = ks[both, :], vs[both, :]
            dq_u = None
            q_l, do_l, ds_l, p_l = [], [], [], []
            for hh in range(2):
                q_h = jnp.where(heads[hh], qu, 0.0).astype(BF16)
                do_h = jnp.where(heads[hh], dou, 0.0).astype(BF16)
                s = _dot_nt(q_h, kcat)
                dp = _dot_nt(do_h, vcat)
                lse_h, dl_h = lss[hh][cur, :], dls[hh][cur, :]
                pp = jnp.where(pm, jnp.exp(s[:, 0:128] - lse_h), 0.0)
                pc = jnp.where(cur_ok, jnp.exp(s[:, 128:256] - lse_h), 0.0)
                ds = jnp.concatenate([pp * (dp[:, 0:128] - dl_h), pc * (dp[:, 128:256] - dl_h)], axis=1).astype(BF16)
                dq = _dot(ds, kcat)
                dq_u = dq if hh == 0 else jnp.where(heads[1], dq, dq_u)
                q_l.append(q_h)
                do_l.append(do_h)
                ds_l.append(ds)
                p_l.append(jnp.concatenate([pp, pc], axis=1).astype(BF16))
            dkcat = _dot_tn(jnp.concatenate(ds_l, axis=0), jnp.concatenate(q_l, axis=0))
            dvcat = _dot_tn(jnp.concatenate(p_l, axis=0), jnp.concatenate(do_l, axis=0))
            dkp[cur, :] = dkcat[0:128]
            dks[cur, :] = dkcat[128:256]
            dvp[cur, :] = dvcat[0:128]
            dvs[cur, :] = dvcat[128:256]
            dqs[cur, :] = dq_u
            return carry

        lax.fori_loop(0, 16, unit, 0, unroll=ATT_UNROLL)
        dks[0:S - 128, :] += dkp[128:S, :]
        dvs[0:S - 128, :] += dvp[128:S, :]
        _from_residues_dyn(g, tmp, dqs)
        dq_ref[0] = (_rope(tmp[...], cc_v, -ss_v, first_half) * ATT_SCALE).astype(BF16)
        _from_residues_dyn(g, tmp, dks)
        dk_ref[0] = _rope(tmp[...], cc_v, -ss_v, first_half).astype(BF16)
        _from_residues_dyn(g, tmp, dvs)
        dv_ref[0] = tmp[...].astype(BF16)

    grp = pl.BlockSpec((1, S, LANES), lambda p, g: (g, 0, p))
    buf = pltpu.VMEM((S, LANES), F32)
    padded_b = pltpu.VMEM((ATT_PAD + S, LANES), BF16)
    return pl.pallas_call(
        body, name="attn_bwd", grid=(4, 3),
        out_shape=[jax.ShapeDtypeStruct((3, S, 512), BF16)] * 3 + [jax.ShapeDtypeStruct((S, 512), BF16)],
        in_specs=_attn_in_specs(3), out_specs=[grp, grp, grp, pl.BlockSpec((S, LANES), lambda p, g: (0, p))],
        scratch_shapes=[buf, buf, padded_b, padded_b] + [buf] * 15,
        compiler_params=_cp(("parallel", "arbitrary")),
    )(z, z, z, z, cc, ss, ob, lse, do_bg)


def _in_proj_bwd(dz, h, w_in):
    half = S // 2
    slab = (D, SHARD_COLS)

    def body(dz_hbm, h_hbm, w_hbm, dh_hbm, g_chip, r1_hbm, relay_hbm, r2_hbm,
             h_buf, dz_buf, stage_d, r1_buf, stage_i, acc,
             dz_sem, w_sem, h_sem, r1_sem, out_sem, send_d, recv_d, send_i, recv_i):
        x, y, c = _mesh_pos()
        sibling = (x, y, 1 - c)
        north = c == 1
        near = (jnp.where(north, 1 - x, x), jnp.where(north, y, 1 - y))
        far = (jnp.where(north, x, 1 - x), jnp.where(north, 1 - y, y))
        chips = [(1 - x, 1 - y), near, far, (x, y)]

        def cols(d):
            return pl.ds(pl.multiple_of(d * SHARD_COLS, LANES), SHARD_COLS)

        blocks = []
        for q_sib, q in zip([chips[0], far, near, chips[3]], chips):
            blocks += [4 * q_sib[0] + 2 * q_sib[1] + (1 - c), 4 * q[0] + 2 * q[1] + c]

        def dz_tile(t):
            return pltpu.make_async_copy(dz_hbm.at[pl.ds((t % 2) * half, half), cols(blocks[t // 2])],
                                         dz_buf.at[t % 2], dz_sem.at[t % 2])

        def to_sibling(i):
            return pltpu.make_async_remote_copy(
                src_ref=stage_d.at[i % 2], dst_ref=r1_hbm.at[i], send_sem=send_d.at[i], recv_sem=recv_d.at[i],
                device_id=sibling, device_id_type=MESH)

        def to_owner(i):
            dst = relay_hbm if i == 0 else r2_hbm.at[i - 1]
            return pltpu.make_async_remote_copy(
                src_ref=stage_i.at[i], dst_ref=dst, send_sem=send_i.at[i], recv_sem=recv_i.at[i],
                device_id=(*(far if i == 2 else near), c), device_id_type=MESH)

        h_copy = pltpu.make_async_copy(h_hbm, h_buf, h_sem)
        h_copy.start()
        dz_tile(0).start()
        h_copy.wait()
        for b in range(8):
            i = b // 2
            if b % 2 == 1:
                to_sibling(i).wait_recv()
                r1_copy = pltpu.make_async_copy(r1_hbm.at[i], r1_buf, r1_sem)
                r1_copy.start()
            g = None
            for r in range(2):
                t = 2 * b + r
                if t + 1 < 16:
                    dz_tile(t + 1).start()
                dz_tile(t).wait()
                part = _dot_tn(h_buf[r * half:(r + 1) * half, :], dz_buf[t % 2])
                g = part if g is None else g + part
            if b % 2 == 0:
                if i >= 2:
                    to_sibling(i - 2).wait_send()
                stage_d[i % 2] = g.astype(BF16)
                to_sibling(i).start()
            else:
                r1_copy.wait()
                g = g + r1_buf[...].astype(F32)
                if i == 2:
                    to_owner(0).wait_recv()
                    relay_copy = pltpu.make_async_copy(relay_hbm, r1_buf, r1_sem)
                    relay_copy.start()
                    relay_copy.wait()
                    g = g + r1_buf[...].astype(F32)
                if i < 3:
                    stage_i[i] = g.astype(BF16)
                    to_owner(i).start()
                else:
                    g_chip[...] = g
        to_sibling(2).wait_send()
        to_sibling(3).wait_send()

        def dz2(t):
            return pltpu.make_async_copy(
                dz_hbm.at[pl.ds((t // 8) * half, half), pl.ds((t % 8) * SHARD_COLS, SHARD_COLS)],
                dz_buf.at[t % 2], dz_sem.at[t % 2])

        def w2(t):
            return pltpu.make_async_copy(w_hbm.at[:, pl.ds((t % 8) * SHARD_COLS, SHARD_COLS)],
                                         stage_d.at[t % 2], w_sem.at[t % 2])

        def dh_out(r):
            return pltpu.make_async_copy(acc, dh_hbm.at[pl.ds(r * half, half), :], out_sem)

        dz2(0).start()
        w2(0).start()
        for t in range(16):
            if t + 1 < 16:
                dz2(t + 1).start()
                w2(t + 1).start()
            dz2(t).wait()
            w2(t).wait()
            part = _dot_nt(dz_buf[t % 2], stage_d[t % 2])
            if t % 8 == 0:
                if t > 0:
                    dh_out(0).wait()
                acc[...] = part
            else:
                acc[...] += part
            if t % 8 == 7:
                dh_out(t // 8).start()
        dh_out(1).wait()
        for i in range(3):
            to_owner(i).wait_send()
        for i in (1, 2):
            to_owner(i).wait_recv()

    any_spec = pl.BlockSpec(memory_space=pl.ANY)
    return pl.pallas_call(
        body, name="in_proj_bwd",
        out_shape=[jax.ShapeDtypeStruct((S, D), F32), jax.ShapeDtypeStruct(slab, F32),
                   jax.ShapeDtypeStruct((4,) + slab, BF16), jax.ShapeDtypeStruct(slab, BF16),
                   jax.ShapeDtypeStruct((2,) + slab, BF16)],
        in_specs=[any_spec] * 3,
        out_specs=[any_spec, pl.BlockSpec(memory_space=pltpu.VMEM), any_spec, any_spec, any_spec],
        scratch_shapes=[pltpu.VMEM((S, D), BF16), pltpu.VMEM((2, half, SHARD_COLS), BF16),
                        pltpu.VMEM((2,) + slab, BF16), pltpu.VMEM(slab, BF16), pltpu.VMEM((3,) + slab, BF16),
                        pltpu.VMEM((half, D), F32),
                        pltpu.SemaphoreType.DMA((2,)), pltpu.SemaphoreType.DMA((2,)), pltpu.SemaphoreType.DMA,
                        pltpu.SemaphoreType.DMA, pltpu.SemaphoreType.DMA,
                        pltpu.SemaphoreType.DMA((4,)), pltpu.SemaphoreType.DMA((4,)),
                        pltpu.SemaphoreType.DMA((3,)), pltpu.SemaphoreType.DMA((3,))],
        compiler_params=_cp(),
    )(dz, h, w_in)


def _grad_x(x, norm_w, dh, dx2):
    tr = 256

    def body(x_ref, w_ref, dh_ref, dx2_ref, gx_ref, gnw_ref):
        @pl.when(pl.program_id(0) == 0)
        def _():
            gnw_ref[...] = jnp.zeros_like(gnw_ref)

        xv, dhv = x_ref[...], dh_ref[...]
        r = lax.rsqrt(jnp.mean(xv * xv, axis=-1, keepdims=True) + EPS)
        n = xv * r
        gnw_ref[...] += jnp.sum(dhv * n, axis=0, keepdims=True)
        dn = dhv * w_ref[...]
        gx_ref[...] = dx2_ref[...] + r * (dn - n * jnp.mean(dn * n, axis=-1, keepdims=True))

    row = pl.BlockSpec((tr, D), lambda i: (i, 0))
    vec = pl.BlockSpec((1, D), lambda i: (0, 0))
    return pl.pallas_call(
        body, name="grad_x", grid=(S // tr,),
        out_shape=[jax.ShapeDtypeStruct((S, D), F32), jax.ShapeDtypeStruct((1, D), F32)],
        in_specs=[row, vec, row, row], out_specs=[row, vec],
        compiler_params=_cp(("arbitrary",)),
    )(x, norm_w, dh, dx2)


def _rope_tables(positions):
    inv_freq = 10000.0 ** (-jnp.arange(0, 64, 2, dtype=F32) / 64)
    ang = positions.astype(F32)[:, None] * inv_freq[None, :]
    cos, sin = jnp.cos(ang), jnp.sin(ang)
    return jnp.tile(cos, (1, 4)), jnp.tile(jnp.concatenate([-sin, sin], axis=1), (1, 2))


def _local_step(x, positions, norm_w, lb_logits, hnw, fnw, target, w_in_shard, w_a, w_b, w_out):
    cc, ss = _rope_tables(positions)
    lbv = jax.nn.sigmoid(lb_logits[0:1] - lb_logits[1:2])
    h = _rmsnorm_in(x, norm_w)
    z, w_in = _in_proj_gather(h, w_in_shard)
    o, o_a, states = _hgrn_fwd(z, lbv, hnw)
    ob, lse, o_bg = _attn_fwd(z, cc, ss)
    dx2, dx2b, dgp, do_a, do_bg, merged, dy_a, dy_b, tail_small = _tail(x, o_a, o_bg, z, target, w_a, w_b, w_out, fnw)
    g_out, gb_out = _tn_matmul(merged, dx2b, "grad_w_out")
    g_a, gb_a = _tn_matmul(o_a, dy_a, "grad_w_a")
    g_b, gb_b = _tn_matmul(o_bg, dy_b, "grad_w_b")
    dhq, dhf, dhi, dhg, glb, ghn = _hgrn_bwd(z, o, do_a, states, lbv, hnw)
    dq, dk, dv, dag = _attn_bwd(z, cc, ss, ob, lse, do_bg)
    dz = jnp.concatenate([dhq, dhf, dhi, dhg, dq[0], dq[1], dq[2], dk[0], dk[1], dk[2], dv[0], dv[1], dv[2], dag, dgp],
                         axis=1)
    dh, g_chip_in, _, _, r2_in = _in_proj_bwd(dz, h, w_in)
    grad_x, gnw = _grad_x(x, norm_w, dh, dx2)
    ghn_row = jnp.pad(jnp.sum(ghn, axis=0), ((0, 0), (0, D - 128)))
    small = jnp.concatenate([gnw, glb, ghn_row, tail_small[0:2], jnp.zeros((3, D), F32)], axis=0)
    return grad_x, (g_chip_in, r2_in), (g_a, g_b, g_out), (gb_a, gb_b, gb_out), small


def kernel(x, positions, norm_w, w_in, lb_logits, hgrn_norm_w, w_branch_a, w_branch_b, w_out, final_norm_w, loss_target, m_norm_w, m_w_in, m_lb_logits, m_hgrn_norm_w, m_w_branch_a, m_w_branch_b, m_w_out, m_final_norm_w, v_norm_w, v_w_in, v_lb_logits, v_hgrn_norm_w, v_w_branch_a, v_w_branch_b, v_w_out, v_final_norm_w):
    ix, iy, ic = _mesh_pos()
    core = jnp.reshape(ic, (1,)).astype(jnp.int32)
    pos = jnp.stack([4 * ix + 2 * iy + ic, 2 * ix + iy]).astype(jnp.int32)

    shards = [w_in[0], w_branch_a[0], w_branch_b[0], w_out[0]]
    moments_m = [m_w_in[0], m_w_branch_a[0], m_w_branch_b[0], m_w_out[0]]
    moments_v = [v_w_in[0], v_w_branch_a[0], v_w_branch_b[0], v_w_out[0]]
    names = ("w_in", "w_a", "w_b", "w_out")
    ids = (1, 2, 3)
    shards_b = [_cast_bf16(w, f"cast_{nm}") for w, nm in zip(shards, names)]
    full = _allgather_weights(ids, shards_b[1:])

    fnw2 = final_norm_w.reshape(1, D)
    grad_x, (g_chip_in, r2_in), grads, gb, small = _local_step(
        x[0], positions[0], norm_w, lb_logits, hgrn_norm_w, fnw2, loss_target[0], shards_b[0], *full)

    r1 = _exchange_sibling(ids, gb)
    pb = [_chip_partials(a, grads[i], r1[i], core) for i, a in enumerate(ids)]
    *r2, gathered = _exchange_chips(ids, pb, small)
    big = [_reduce_own_and_update(shards[0], moments_m[0], moments_v[0], g_chip_in, r2_in)]
    big += [_reduce_and_update(a, shards[a], moments_m[a], moments_v[a], grads[i], r1[i], r2[i], pos)
            for i, a in enumerate(ids)]
    sm = _small_update(gathered, norm_w, lb_logits, hgrn_norm_w, fnw2,
                       (m_norm_w, m_lb_logits, m_hgrn_norm_w, m_final_norm_w.reshape(1, D),
                        v_norm_w, v_lb_logits, v_hgrn_norm_w, v_final_norm_w.reshape(1, D)))
    loss = sm[0][0, 0]
    outs = [loss, grad_x[None]]
    for kind in range(4):
        s_nw, s_lb, s_hn, s_fn = sm[1 + 4 * kind:5 + 4 * kind]
        outs += [s_nw, big[0][kind][None], s_lb, s_hn, big[1][kind][None], big[2][kind][None],
                 big[3][kind][None], s_fn.reshape(D)]
    return tuple(outs)
```

```python
import functools

import jax
import jax.numpy as jnp
from jax import lax
from jax.experimental import pallas as pl
from jax.experimental.pallas import tpu as pltpu

F32 = jnp.float32
BF16 = jnp.bfloat16
MESH = pl.DeviceIdType.MESH

S = 2048
D = 1024
NDEV = 8
HEADS = 8
CHUNK = 64
SUB = 16
HBLK = 128
ATT_PAD = 128
ATT_UNROLL = 4
EXP_CLAMP = 80.0
EPS = 1e-6
IN_COLS = 11264
SHARD_COLS = IN_COLS // NDEV
ATT_DILS = (1, 4, 16)
ATT_SCALE = 64 ** -0.5
LANES = 128

ADAM_LR, ADAM_B1, ADAM_B2, ADAM_EPS, ADAM_WD, ADAM_STEP = 0.001, 0.9, 0.999, 1e-08, 0.01, 10

VMEM_LIMIT = 56 * 1024 * 1024


def _cp(sem=None, **kw):
    return pltpu.CompilerParams(dimension_semantics=sem, vmem_limit_bytes=VMEM_LIMIT, **kw)


def _dot(a, b):
    return jnp.dot(a, b, preferred_element_type=F32)


def _dot_nt(a, b):
    return lax.dot_general(a, b, (((1,), (1,)), ((), ())), preferred_element_type=F32)


def _dot_tn(a, b):
    return lax.dot_general(a, b, (((0,), (0,)), ((), ())), preferred_element_type=F32)


def _split2(x):
    hi = x.astype(BF16)
    lo = (x - hi.astype(F32)).astype(BF16)
    return hi, lo


def _split3(x):
    hi = x.astype(BF16)
    r = x - hi.astype(F32)
    mid = r.astype(BF16)
    lo = (r - mid.astype(F32)).astype(BF16)
    return hi, mid, lo


def _dot_ones(ones_bf16, x):
    hi, mid, lo = _split3(x)
    return _dot(ones_bf16, hi) + _dot(ones_bf16, mid) + _dot(ones_bf16, lo)


def _dot3(dotfn, a, b):
    ah, al = _split2(a)
    bh, bl = _split2(b)
    return dotfn(ah, bh) + dotfn(ah, bl) + dotfn(al, bh)


def _silu(x):
    return x * jax.nn.sigmoid(x)


def _dsilu(x):
    s = jax.nn.sigmoid(x)
    return s * (1.0 + x * (1.0 - s))


def _mesh_pos():
    return lax.axis_index("x"), lax.axis_index("y"), lax.axis_index("c")


def _shard_of(ref, a, d):
    if a == 0:
        return ref.at[:, pl.ds(pl.multiple_of(d * SHARD_COLS, LANES), SHARD_COLS)]
    if a == 2:
        return ref.at[:, pl.ds(pl.multiple_of(d * LANES, LANES), LANES)]
    return ref.at[pl.ds(pl.multiple_of(d * 128, 128), 128), :]


FULL_SHAPES = ((D, IN_COLS), (D, D), (512, D), (D, D))
SHARD_SHAPES = ((D, SHARD_COLS), (128, D), (512, 128), (128, D))


def _allgather_steps(ids, ins, outs, send_sems, recv_sems, local_sems):
    n = len(ids)
    x, y, c = _mesh_pos()
    me, sibling = (x, y, c), (x, y, 1 - c)
    chips = [(1 - x, y), (x, 1 - y), (1 - x, 1 - y)]

    def blk(a, p):
        return _shard_of(outs[a], ids[a], 4 * p[0] + 2 * p[1] + p[2])

    def copy(a, k, block, to, src=None):
        return pltpu.make_async_remote_copy(
            src_ref=blk(a, block) if src is None else src, dst_ref=blk(a, block),
            send_sem=send_sems.at[a * 7 + k], recv_sem=recv_sems.at[a * 7 + k],
            device_id=to, device_id_type=MESH)

    mine = [pltpu.make_async_copy(ins[a], blk(a, me), local_sems.at[a]) for a in range(n)]
    first = []
    for a in range(n):
        first += [copy(a, 1 + j, me, (*chip, c), src=ins[a]) for j, chip in enumerate(chips)]
    for a in range(n):
        first.append(copy(a, 0, me, sibling, src=ins[a]))
    passed = [copy(a, 4 + j, (*chip, c), sibling) for j, chip in enumerate(chips) for a in range(n)]

    def start():
        for cp in mine + first:
            cp.start()

    def middle():
        for j, chip in enumerate(chips):
            for a in range(n):
                copy(a, 1 + j, (*chip, c), me).wait_recv()
                passed[j * n + a].start()

    def end():
        for a in range(n):
            copy(a, 0, sibling, me).wait_recv()
        for j, chip in enumerate(chips):
            for a in range(n):
                copy(a, 4 + j, (*chip, 1 - c), me).wait_recv()
        for cp in first + passed:
            cp.wait_send()
        for cp in mine:
            cp.wait()

    return start, middle, end


def _in_proj_gather(h, w_shard):
    half = S // 2
    slab = (D, SHARD_COLS)

    def body(h_hbm, w_hbm, z_hbm, wfull_hbm, h_buf, land, zstage,
             h_sem, own_sem, z_sem, wout_sem, send_sems, recv_sems):
        x, y, c = _mesh_pos()
        sibling = (x, y, 1 - c)
        north = c == 1

        def chips_of(first_x):
            near = (jnp.where(first_x, 1 - x, x), jnp.where(first_x, y, 1 - y))
            far = (jnp.where(first_x, x, 1 - x), jnp.where(first_x, 1 - y, y))
            return [near, far, (1 - x, 1 - y)]

        mine, theirs = chips_of(north), chips_of(jnp.logical_not(north))

        def dev(chip, core):
            return 4 * chip[0] + 2 * chip[1] + core

        block_of = ([dev((x, y), c), dev((x, y), 1 - c)] + [dev(q, c) for q in mine]
                    + [dev(q, 1 - c) for q in theirs])

        def cols(d):
            if isinstance(d, int):
                return pl.ds(d * SHARD_COLS, SHARD_COLS)
            return pl.ds(pl.multiple_of(d * SHARD_COLS, LANES), SHARD_COLS)

        def send(k, src, dst_slot, to):
            return pltpu.make_async_remote_copy(
                src_ref=src, dst_ref=land.at[dst_slot], send_sem=send_sems.at[k], recv_sem=recv_sems.at[k],
                device_id=to, device_id_type=MESH)

        def to_sibling():
            return send(0, w_hbm, 1, sibling)

        def to_chip(j):
            if j == 2:
                return send(3, land.at[2], 4, (*mine[1], c))
            return send(1 + j, w_hbm, 2 + j, (*mine[j], c))

        def pass_on(j):
            return send(4 + j, land.at[2 + j], 5 + j, sibling)

        own = pltpu.make_async_copy(w_hbm, land.at[0], own_sem)
        h_copy = pltpu.make_async_copy(h_hbm, h_buf, h_sem)
        own.start()
        h_copy.start()
        to_sibling().start()
        to_chip(0).start()
        h_copy.wait()
        own.wait()

        def multiply(slot, n_done):
            d = block_of[slot]
            out = pltpu.make_async_copy(land.at[slot], wfull_hbm.at[:, cols(d)], wout_sem.at[slot])
            out.start()
            for r in range(2):
                rows = pl.ds(r * half, half)
                zc = pltpu.make_async_copy(zstage.at[r], z_hbm.at[rows, cols(d)], z_sem.at[r])
                if n_done > 0:
                    zc.wait()
                zstage[r] = _dot(h_buf[r * half:(r + 1) * half, :], land[slot])
                zc.start()
            return out

        outs = [multiply(0, 0)]
        to_sibling().wait_recv()
        outs.append(multiply(1, 1))
        done = 2
        for j in range(3):
            to_chip(j).wait_recv()
            pass_on(j).start()
            to_chip(j).wait_send()
            if j < 2:
                to_chip(j + 1).start()
            outs.append(multiply(2 + j, done))
            pass_on(j).wait_recv()
            outs.append(multiply(5 + j, done + 1))
            done += 2
        for r in range(2):
            pltpu.make_async_copy(zstage.at[r], z_hbm.at[pl.ds(r * half, half), cols(0)], z_sem.at[r]).wait()
        for out in outs:
            out.wait()
        to_sibling().wait_send()
        for j in range(3):
            pass_on(j).wait_send()

    any_spec = pl.BlockSpec(memory_space=pl.ANY)
    return pl.pallas_call(
        body, name="in_proj_gather",
        out_shape=[jax.ShapeDtypeStruct((S, IN_COLS), F32), jax.ShapeDtypeStruct((D, IN_COLS), BF16)],
        in_specs=[any_spec] * 2, out_specs=[any_spec] * 2,
        scratch_shapes=[pltpu.VMEM((S, D), BF16), pltpu.VMEM((8,) + slab, BF16), pltpu.VMEM((2, half, SHARD_COLS), F32),
                        pltpu.SemaphoreType.DMA, pltpu.SemaphoreType.DMA, pltpu.SemaphoreType.DMA((2,)),
                        pltpu.SemaphoreType.DMA((8,)), pltpu.SemaphoreType.DMA((7,)), pltpu.SemaphoreType.DMA((7,))],
        compiler_params=_cp(),
    )(h, w_shard)


def _exchange_sibling(ids, gb):
    n = len(gb)

    def body(*refs):
        ins, outs = refs[:n], refs[n:2 * n]
        send_sems, recv_sems = refs[2 * n:]
        x, y, c = _mesh_pos()
        sibling = (x, y, 1 - c)
        copies = []
        for i, a in enumerate(ids):
            for q in range(4):
                copies.append(pltpu.make_async_remote_copy(
                    src_ref=_shard_of(ins[i], a, 2 * q + (1 - c)), dst_ref=outs[i].at[q],
                    send_sem=send_sems.at[i * 4 + q], recv_sem=recv_sems.at[i * 4 + q],
                    device_id=sibling, device_id_type=MESH))
        for cp in copies:
            cp.start()
        for cp in copies:
            cp.wait()

    any_spec = pl.BlockSpec(memory_space=pl.ANY)
    return pl.pallas_call(
        body, name="grads_to_sibling",
        out_shape=[jax.ShapeDtypeStruct((4,) + SHARD_SHAPES[a], BF16) for a in ids],
        in_specs=[any_spec] * n, out_specs=[any_spec] * n,
        scratch_shapes=[pltpu.SemaphoreType.DMA((4 * n,)), pltpu.SemaphoreType.DMA((4 * n,))],
    )(*gb)


def _exchange_chips_steps(ins, outs, send_sems, recv_sems):
    x, y, c = _mesh_pos()
    chips = [(1 - x, y), (x, 1 - y), (1 - x, 1 - y)]
    copies = []
    for a in range(len(ins)):
        for k, chip in enumerate(chips):
            copies.append(pltpu.make_async_remote_copy(
                src_ref=ins[a].at[2 * chip[0] + chip[1]], dst_ref=outs[a].at[k],
                send_sem=send_sems.at[a * 3 + k], recv_sem=recv_sems.at[a * 3 + k],
                device_id=(*chip, c), device_id_type=MESH))

    def start():
        for cp in copies:
            cp.start()

    def end():
        for cp in copies:
            cp.wait()

    return start, end


def _gather_small(small):
    def body(small_ref, small_out, ssend, srecv, local_sem):
        x, y, c = _mesh_pos()
        me = 4 * x + 2 * y + c
        copies = []
        for r in range(1, NDEV):
            peer = (1 - x if r & 4 else x, 1 - y if r & 2 else y, 1 - c if r & 1 else c)
            copies.append(pltpu.make_async_remote_copy(
                src_ref=small_ref, dst_ref=small_out.at[me],
                send_sem=ssend.at[r - 1], recv_sem=srecv.at[r - 1],
                device_id=peer, device_id_type=MESH))
        own = pltpu.make_async_copy(small_ref, small_out.at[me], local_sem)
        own.start()
        for cp in copies:
            cp.start()
        for cp in copies:
            cp.wait()
        own.wait()

    any_spec = pl.BlockSpec(memory_space=pl.ANY)
    return pl.pallas_call(
        body, name="gather_small",
        out_shape=jax.ShapeDtypeStruct((NDEV,) + small.shape, F32),
        in_specs=[any_spec], out_specs=any_spec,
        scratch_shapes=[pltpu.SemaphoreType.DMA((NDEV - 1,)), pltpu.SemaphoreType.DMA((NDEV - 1,)),
                        pltpu.SemaphoreType.DMA],
    )(small)


def _shard_tiles(a):
    rows, cols = SHARD_SHAPES[a]
    tr = min(rows, 256)
    return (tr, cols), rows // tr


def _full_index(a, d, i):
    (tr, _), nt = _shard_tiles(a)
    if a in (0, 2):
        return (i, d)
    return (d * nt + i, 0)


def _cast_bf16(x, name):
    rows, cols = x.shape
    tr = min(rows, 256)

    def body(x_ref, o_ref):
        o_ref[...] = x_ref[...].astype(BF16)

    return pl.pallas_call(
        body, name=name, out_shape=jax.ShapeDtypeStruct(x.shape, BF16), grid=(rows // tr,),
        in_specs=[pl.BlockSpec((tr, cols), lambda i: (i, 0))],
        out_specs=pl.BlockSpec((tr, cols), lambda i: (i, 0)),
        compiler_params=_cp(("parallel",)),
    )(x)


def _chip_partials(a, g_full, r1, core):
    tile, nt = _shard_tiles(a)

    def body(c_ref, g_ref, r_ref, o_ref):
        o_ref[0] = (g_ref[...] + r_ref[0].astype(F32)).astype(BF16)

    grid_spec = pltpu.PrefetchScalarGridSpec(
        num_scalar_prefetch=1, grid=(4, nt),
        in_specs=[pl.BlockSpec(tile, lambda q, i, c: _full_index(a, 2 * q + c[0], i)),
                  pl.BlockSpec((1,) + tile, lambda q, i, c: (q, i, 0))],
        out_specs=pl.BlockSpec((1,) + tile, lambda q, i, c: (q, i, 0)))
    return pl.pallas_call(
        body, name=f"chip_partials_{a}", grid_spec=grid_spec,
        out_shape=jax.ShapeDtypeStruct((4,) + SHARD_SHAPES[a], BF16),
        compiler_params=_cp(("parallel", "parallel")),
    )(core, g_full, r1)


def _adam(w, g, m, v):
    m = ADAM_B1 * m + (1.0 - ADAM_B1) * g
    v = ADAM_B2 * v + (1.0 - ADAM_B2) * (g * g)
    m_hat = m / (1.0 - ADAM_B1 ** ADAM_STEP)
    v_hat = v / (1.0 - ADAM_B2 ** ADAM_STEP)
    delta = -ADAM_LR * (m_hat / (jnp.sqrt(v_hat) + ADAM_EPS) + ADAM_WD * w)
    return delta, m, v


def _reduce_and_update(a, w, m, v, g_full, r1, r2, pos):
    tile, nt = _shard_tiles(a)

    def body(p_ref, w_ref, m_ref, v_ref, g_ref, r1_ref, r2_ref, go_ref, do_ref, mo_ref, vo_ref):
        g = g_ref[...] + r1_ref[0].astype(F32)
        g = g + r2_ref[0].astype(F32)
        g = g + r2_ref[1].astype(F32)
        g = g + r2_ref[2].astype(F32)
        delta, m_new, v_new = _adam(w_ref[...], g, m_ref[...], v_ref[...])
        go_ref[...] = g
        do_ref[...] = delta
        mo_ref[...] = m_new
        vo_ref[...] = v_new

    own = pl.BlockSpec(tile, lambda i, p: (i, 0))
    grid_spec = pltpu.PrefetchScalarGridSpec(
        num_scalar_prefetch=1, grid=(nt,),
        in_specs=[own, own, own,
                  pl.BlockSpec(tile, lambda i, p: _full_index(a, p[0], i)),
                  pl.BlockSpec((1,) + tile, lambda i, p: (p[1], i, 0)),
                  pl.BlockSpec((3,) + tile, lambda i, p: (0, i, 0))],
        out_specs=[own] * 4)
    shp = jax.ShapeDtypeStruct(w.shape, F32)
    return pl.pallas_call(
        body, name=f"reduce_update_{a}", grid_spec=grid_spec, out_shape=[shp] * 4,
        compiler_params=_cp(("parallel",)),
    )(pos, w, m, v, g_full, r1, r2)


def _reduce_own_and_update(w, m, v, g_chip, r2):
    tile, nt = _shard_tiles(0)

    def body(w_ref, m_ref, v_ref, g_ref, r2_ref, go_ref, do_ref, mo_ref, vo_ref):
        g = g_ref[...] + r2_ref[0].astype(F32)
        g = g + r2_ref[1].astype(F32)
        delta, m_new, v_new = _adam(w_ref[...], g, m_ref[...], v_ref[...])
        go_ref[...] = g
        do_ref[...] = delta
        mo_ref[...] = m_new
        vo_ref[...] = v_new

    own = pl.BlockSpec(tile, lambda i: (i, 0))
    shp = jax.ShapeDtypeStruct(w.shape, F32)
    return pl.pallas_call(
        body, name="reduce_update_0", grid=(nt,), out_shape=[shp] * 4,
        in_specs=[own, own, own, own, pl.BlockSpec((2,) + tile, lambda i: (0, i, 0))], out_specs=[own] * 4,
        compiler_params=_cp(("parallel",)),
    )(w, m, v, g_chip, r2)


def _small_update(gathered, norm_w, lb_logits, hnw, fnw, moments):
    m_nw, m_lb, m_hn, m_fn, v_nw, v_lb, v_hn, v_fn = moments

    def body(g_ref, nw, lb, hn, fn, mnw, mlb, mhn, mfn, vnw, vlb, vhn, vfn,
             loss_o, g_nw, g_lb, g_hn, g_fn, d_nw, d_lb, d_hn, d_fn,
             mo_nw, mo_lb, mo_hn, mo_fn, vo_nw, vo_lb, vo_hn, vo_fn):
        tot = g_ref[0]
        for d in range(1, NDEV):
            tot = tot + g_ref[d]
        loss_o[...] = tot[4:5, 0:LANES]
        logits = lb[...]
        lbv = jax.nn.sigmoid(logits[0:1] - logits[1:2])
        chain = tot[1:2] * lbv * (1.0 - lbv)
        grads = (tot[0:1], jnp.concatenate([chain, -chain], axis=0), tot[2:3, 0:LANES], tot[3:4])
        outs = ((nw, mnw, vnw, g_nw, d_nw, mo_nw, vo_nw), (lb, mlb, vlb, g_lb, d_lb, mo_lb, vo_lb),
                (hn, mhn, vhn, g_hn, d_hn, mo_hn, vo_hn), (fn, mfn, vfn, g_fn, d_fn, mo_fn, vo_fn))
        for g, (w_r, m_r, v_r, g_o, d_o, m_o, v_o) in zip(grads, outs):
            delta, m_new, v_new = _adam(w_r[...], g, m_r[...], v_r[...])
            g_o[...] = g
            d_o[...] = delta
            m_o[...] = m_new
            v_o[...] = v_new

    shapes = [norm_w.shape, lb_logits.shape, hnw.shape, fnw.shape]
    out_shape = [jax.ShapeDtypeStruct((1, LANES), F32)] + [jax.ShapeDtypeStruct(s, F32) for s in shapes] * 4
    return pl.pallas_call(body, name="small_update", out_shape=out_shape, compiler_params=_cp())(
        gathered, norm_w, lb_logits, hnw, fnw, m_nw, m_lb, m_hn, m_fn, v_nw, v_lb, v_hn, v_fn)


def _rmsnorm_in(x, norm_w):
    tr = 512

    def body(x_ref, w_ref, h_ref):
        xv = x_ref[...]
        r = lax.rsqrt(jnp.mean(xv * xv, axis=-1, keepdims=True) + EPS)
        h_ref[...] = (xv * r * w_ref[...]).astype(BF16)

    return pl.pallas_call(
        body, name="rmsnorm_in", out_shape=jax.ShapeDtypeStruct((S, D), BF16), grid=(S // tr,),
        in_specs=[pl.BlockSpec((tr, D), lambda i: (i, 0)), pl.BlockSpec((1, D), lambda i: (0, 0))],
        out_specs=pl.BlockSpec((tr, D), lambda i: (i, 0)),
        compiler_params=_cp(("parallel",)),
    )(x, norm_w)


def _in_proj(h, w_in):
    tn = 1024

    def body(h_ref, w_ref, z_ref):
        z_ref[...] = _dot(h_ref[...], w_ref[...])

    return pl.pallas_call(
        body, name="in_proj", out_shape=jax.ShapeDtypeStruct((S, IN_COLS), F32), grid=(IN_COLS // tn,),
        in_specs=[pl.BlockSpec((S, D), lambda j: (0, 0)), pl.BlockSpec((D, tn), lambda j: (0, j))],
        out_specs=pl.BlockSpec((S, tn), lambda j: (0, j)),
        compiler_params=_cp(("parallel",)),
    )(h, w_in)


def _block_tri(n, block, upper=False):
    r = lax.broadcasted_iota(jnp.int32, (n, n), 0)
    c = lax.broadcasted_iota(jnp.int32, (n, n), 1)
    keep = (c >= r) if upper else (c <= r)
    return jnp.where(keep & ((r // block) == (c // block)), 1.0, 0.0).astype(BF16)


def _tril_mask(n):
    r = lax.broadcasted_iota(jnp.int32, (n, n), 0)
    c = lax.broadcasted_iota(jnp.int32, (n, n), 1)
    return c <= r


def _chunk_scores(q, k, b, bex, r0, mask):
    rows = slice(r0, r0 + CHUNK)
    parts, qs_l, ek_l, eq_l = [], [], [], []
    for i in range(CHUNK // SUB):
        ri = slice(r0 + SUB * i, r0 + SUB * (i + 1))
        base = bex[r0 + SUB * i:r0 + SUB * i + 1]
        eq = jnp.exp(b[ri] - base)
        ek = jnp.exp(jnp.minimum(base - b[rows], EXP_CLAMP))
        qs = q[ri] * eq
        parts.append(_dot_nt(qs.astype(BF16), (k[rows] * ek).astype(BF16)))
        qs_l.append(qs)
        ek_l.append(ek)
        eq_l.append(eq)
    return jnp.where(mask, jnp.concatenate(parts, axis=0), 0.0), qs_l, ek_l, eq_l


def _hgrn_cols(hq, hf, hi, lb):
    sg = jax.nn.sigmoid(hf)
    f = lb + (1.0 - lb) * sg
    g = jnp.log(f)
    b = _dot_ones(_block_tri(HBLK, CHUNK), g)
    return _silu(hq), 1.0 - f, g, hi, sg, f, b


GATHER_IDS = (1, 2, 3)


def _hgrn_fwd(z, lbv, hnw, shards):
    ntb, nch = S // HBLK, HBLK // CHUNK
    n = len(GATHER_IDS)

    def body(hq_ref, hf_ref, hi_ref, hg_ref, lb_ref, hnw_ref, s0, s1, s2, o_ref, oa_ref, st_ref, f0, f1, f2,
             state, send_sems, recv_sems, local_sems):
        start, middle, end = _allgather_steps(GATHER_IDS, (s0, s1, s2), (f0, f1, f2), send_sems, recv_sems, local_sems)

        @pl.when(pl.program_id(0) == 0)
        def _():
            state[...] = jnp.zeros_like(state)
            start()

        pl.when(pl.program_id(0) == ntb // 2)(middle)

        q_a, k_a, g_a, v_a, _, _, b_a = _hgrn_cols(hq_ref[...], hf_ref[...], hi_ref[...], lb_ref[...])
        bex_a = b_a - g_a
        eb_a = jnp.exp(b_a)
        mask = _tril_mask(CHUNK)
        hg = hg_ref[...]
        w = hnw_ref[...]
        for h in range(HEADS):
            cols = slice(128 * h, 128 * h + 128)
            q, k, v, b, bex, eb = q_a[:, cols], k_a[:, cols], v_a[:, cols], b_a[:, cols], bex_a[:, cols], eb_a[:, cols]
            st = state[h]
            outs = []
            for c in range(nch):
                r0 = c * CHUNK
                rows = slice(r0, r0 + CHUNK)
                a, _, _, _ = _chunk_scores(q, k, b, bex, r0, mask)
                vb = v[rows].astype(BF16)
                b_last = b[r0 + CHUNK - 1:r0 + CHUNK]
                qe = (q[rows] * eb[rows]).astype(BF16)
                outs.append(_dot(a.astype(BF16), vb) + _dot_nt(qe, st.astype(BF16)))
                st_ref[h, c] = st
                ke = (k[rows] * jnp.exp(b_last - b[rows])).astype(BF16)
                st = st * jnp.exp(b_last) + _dot_tn(vb, ke)
            state[h] = st
            o = jnp.concatenate(outs, axis=0)
            o_ref[:, cols] = o
            r = lax.rsqrt(jnp.mean(o * o, axis=-1, keepdims=True) + EPS)
            oa_ref[:, cols] = (o * r * w * _silu(hg[:, cols])).astype(BF16)

        pl.when(pl.program_id(0) == ntb - 1)(end)

    def zcol(j):
        return pl.BlockSpec((HBLK, D), lambda t: (t, j))

    out_blk = pl.BlockSpec((HBLK, D), lambda t: (t, 0))
    any_spec = pl.BlockSpec(memory_space=pl.ANY)
    return pl.pallas_call(
        body, name="hgrn_fwd", grid=(ntb,),
        out_shape=[jax.ShapeDtypeStruct((S, D), F32), jax.ShapeDtypeStruct((S, D), BF16),
                   jax.ShapeDtypeStruct((HEADS, S // CHUNK, 128, 128), F32)]
        + [jax.ShapeDtypeStruct(FULL_SHAPES[a], BF16) for a in GATHER_IDS],
        in_specs=[zcol(0), zcol(1), zcol(2), zcol(3),
                  pl.BlockSpec((1, D), lambda t: (0, 0)), pl.BlockSpec((1, 128), lambda t: (0, 0))] + [any_spec] * n,
        out_specs=[out_blk, out_blk, pl.BlockSpec((HEADS, nch, 128, 128), lambda t: (0, t, 0, 0))] + [any_spec] * n,
        scratch_shapes=[pltpu.VMEM((HEADS, 128, 128), F32), pltpu.SemaphoreType.DMA((7 * n,)),
                        pltpu.SemaphoreType.DMA((7 * n,)), pltpu.SemaphoreType.DMA((n,))],
        compiler_params=_cp(("arbitrary",)),
    )(z, z, z, z, lbv, hnw, *shards)


def _half_mask():
    lane = lax.broadcasted_iota(jnp.int32, (1, LANES), 1)
    return (lane % 64) < 32


def _rope(t, cc, ss, first_half):
    partner = jnp.where(first_half, pltpu.roll(t, 96, 1), pltpu.roll(t, 32, 1))
    return t * cc + partner * ss


def _attn_masks():
    i = lax.broadcasted_iota(jnp.int32, (128, 128), 0)
    j = lax.broadcasted_iota(jnp.int32, (128, 128), 1)
    return j >= i, j <= i


def _to_residues_dyn(g, dst, src, row0=0, dtype=None):
    for gi, dil in enumerate((1, 4, 16)):
        m = S // dil

        @pl.when(g == gi)
        def _(dil=dil, m=m):
            for r in range(dil):
                v = src[...] if dil == 1 else src[pl.ds(r, m, stride=dil), :]
                if dtype is not None:
                    v = v.astype(dtype)
                dst[row0 + r * m:row0 + (r + 1) * m, 0:LANES] = v


def _from_residues_dyn(g, dst, src, row0=0):
    for gi, dil in enumerate((1, 4, 16)):
        m = S // dil

        @pl.when(g == gi)
        def _(dil=dil, m=m):
            for r in range(dil):
                v = src[row0 + r * m:row0 + (r + 1) * m, :]
                if dil == 1:
                    dst[...] = v
                else:
                    dst[pl.ds(r, m, stride=dil), :] = v


def _group_blocks(g):
    return jnp.where(g == 0, 16, jnp.where(g == 1, 4, 1))


def _attn_in_specs(extra):
    def zcol(off):
        return pl.BlockSpec((S, LANES), lambda p, g: (0, off + 4 * g + p))

    per_pair = pl.BlockSpec((S, LANES), lambda p, g: (0, p))
    const = pl.BlockSpec((S, LANES), lambda p, g: (0, 0))
    return [zcol(32), zcol(44), zcol(56), pl.BlockSpec((S, LANES), lambda p, g: (0, 68 + p)), const, const] + [per_pair] * extra


def _attn_fwd(z, cc, ss):
    def body(q_ref, k_ref, v_ref, ag_ref, cc_ref, ss_ref, ob_ref, lse_ref, obg_ref,
             tmp, qs, ks, vx, og, mg, lg, o_t, m_t, l_t, o_acc, m_acc, l_acc):
        g = pl.program_id(1)
        first_half = _half_mask()
        prev_ok, cur_ok = _attn_masks()
        lane = lax.broadcasted_iota(jnp.int32, (1, LANES), 1)
        heads = (lane < 64, lane >= 64)
        nblk = _group_blocks(g)

        @pl.when(g == 0)
        def _():
            ks[0:ATT_PAD, :] = jnp.zeros((ATT_PAD, LANES), BF16)
            vx[0:ATT_PAD, 0:LANES] = jnp.zeros((ATT_PAD, LANES), BF16)
            vx[:, LANES:2 * LANES] = jnp.ones((ATT_PAD + S, LANES), BF16)

        tmp[...] = _rope(q_ref[...], cc_ref[...], ss_ref[...], first_half) * ATT_SCALE
        _to_residues_dyn(g, qs, tmp)
        tmp[...] = _rope(k_ref[...], cc_ref[...], ss_ref[...], first_half)
        _to_residues_dyn(g, ks, tmp, ATT_PAD, BF16)
        _to_residues_dyn(g, vx, v_ref, ATT_PAD, BF16)

        def unit(u, carry):
            start = pl.multiple_of(u * 128, 128)
            cur = pl.ds(start, 128)
            pm = prev_ok & ((u & (nblk - 1)) != 0)
            qu = qs[cur, :]
            kcat = ks[pl.ds(start, 256), :]
            vext = vx[pl.ds(start, 256), :]
            o_u = m_u = l_u = None
            for hh in range(2):
                s = _dot_nt(jnp.where(heads[hh], qu, 0.0).astype(BF16), kcat)
                sp = jnp.where(pm, s[:, 0:128], -jnp.inf)
                sc = jnp.where(cur_ok, s[:, 128:256], -jnp.inf)
                m = jnp.max(jnp.maximum(sp, sc), axis=-1, keepdims=True)
                p = jnp.concatenate([jnp.exp(sp - m), jnp.exp(sc - m)], axis=1).astype(BF16)
                ol = _dot(p, vext)
                mb = jnp.broadcast_to(m, (128, LANES))
                if hh == 0:
                    o_u, l_u, m_u = ol[:, 0:128], ol[:, 128:256], mb
                else:
                    o_u = jnp.where(heads[1], ol[:, 0:128], o_u)
                    l_u = jnp.where(heads[1], ol[:, 128:256], l_u)
                    m_u = jnp.where(heads[1], mb, m_u)
            og[cur, :] = o_u
            mg[cur, :] = m_u
            lg[cur, :] = l_u
            return carry

        lax.fori_loop(0, 16, unit, 0, unroll=ATT_UNROLL)
        _from_residues_dyn(g, o_t, og)
        _from_residues_dyn(g, m_t, mg)
        _from_residues_dyn(g, l_t, lg)

        @pl.when(g == 0)
        def _():
            o_acc[...] = o_t[...]
            m_acc[...] = m_t[...]
            l_acc[...] = l_t[...]

        @pl.when(g > 0)
        def _():
            m_new = jnp.maximum(m_acc[...], m_t[...])
            wa, wb = jnp.exp(m_acc[...] - m_new), jnp.exp(m_t[...] - m_new)
            o_acc[...] = o_acc[...] * wa + o_t[...] * wb
            l_acc[...] = l_acc[...] * wa + l_t[...] * wb
            m_acc[...] = m_new

        @pl.when(g == 2)
        def _():
            ob = o_acc[...] / l_acc[...]
            ob_ref[...] = ob
            lse_ref[...] = m_acc[...] + jnp.log(l_acc[...])
            obg_ref[...] = (ob * _silu(ag_ref[...])).astype(BF16)

    blk = pl.BlockSpec((S, LANES), lambda p, g: (0, p))
    buf = pltpu.VMEM((S, LANES), F32)
    return pl.pallas_call(
        body, name="attn_fwd", grid=(4, 3),
        out_shape=[jax.ShapeDtypeStruct((S, 512), F32), jax.ShapeDtypeStruct((S, 512), F32),
                   jax.ShapeDtypeStruct((S, 512), BF16)],
        in_specs=_attn_in_specs(0), out_specs=[blk, blk, blk],
        scratch_shapes=[buf, buf, pltpu.VMEM((ATT_PAD + S, LANES), BF16), pltpu.VMEM((ATT_PAD + S, 2 * LANES), BF16)] + [buf] * 9,
        compiler_params=_cp(("parallel", "arbitrary")),
    )(z, z, z, z, cc, ss)


def _tail(x, o_a, o_bg, z, target, w_a, w_b, w_out, fnw):
    tm = 256

    def body(x_ref, oa_ref, ob_ref, gpa_ref, gpb_ref, t_ref, wa_ref, wb_ref, wo_ref, fnw_ref,
             dx2_ref, dx2b_ref, dgp_ref, doa_ref, dob_ref, mg_ref, dya_ref, dyb_ref, small_ref):
        @pl.when(pl.program_id(0) == 0)
        def _():
            small_ref[...] = jnp.zeros_like(small_ref)

        wa, wb, wo = wa_ref[...], wb_ref[...], wo_ref[...]
        y_a = _dot(oa_ref[...], wa)
        y_b = _dot(ob_ref[...], wb)
        ga = jax.nn.sigmoid(gpa_ref[...])
        gb = jax.nn.sigmoid(gpb_ref[...])
        merged = (ga * y_a + gb * y_b).astype(BF16)
        x2 = x_ref[...] + _dot(merged, wo)
        r2 = lax.rsqrt(jnp.mean(x2 * x2, axis=-1, keepdims=True) + EPS)
        n2 = x2 * r2
        fw = fnw_ref[...]
        err = n2 * fw - t_ref[...]
        loss = 0.5 * jnp.sum(jnp.sum(err * err, axis=-1, keepdims=True), axis=0, keepdims=True) / D
        dy = err * (1.0 / D)
        g_fnw = jnp.sum(dy * n2, axis=0, keepdims=True)
        dn = dy * fw
        dx2 = r2 * (dn - n2 * jnp.mean(dn * n2, axis=-1, keepdims=True))
        dx2b = dx2.astype(BF16)
        dmerged = _dot_nt(dx2b, wo)
        dy_a = (dmerged * ga).astype(BF16)
        dy_b = (dmerged * gb).astype(BF16)
        dx2_ref[...] = dx2
        dx2b_ref[...] = dx2b
        dgp_ref[:, 0:D] = (dmerged * y_a * ga * (1.0 - ga)).astype(BF16)
        dgp_ref[:, D:2 * D] = (dmerged * y_b * gb * (1.0 - gb)).astype(BF16)
        doa_ref[...] = _dot_nt(dy_a, wa)
        dob_ref[...] = _dot_nt(dy_b, wb)
        mg_ref[...] = merged
        dya_ref[...] = dy_a
        dyb_ref[...] = dy_b
        small_ref[0:1, :] += g_fnw
        small_ref[1:2, :] += jnp.broadcast_to(loss, (1, D))

    def rows(cols, off=0):
        return pl.BlockSpec((tm, cols), lambda i: (i, off))

    def whole(shape):
        return pl.BlockSpec(shape, lambda i: (0, 0))

    return pl.pallas_call(
        body, name="tail", grid=(S // tm,),
        out_shape=[jax.ShapeDtypeStruct((S, D), F32), jax.ShapeDtypeStruct((S, D), BF16),
                   jax.ShapeDtypeStruct((S, 2 * D), BF16), jax.ShapeDtypeStruct((S, D), F32),
                   jax.ShapeDtypeStruct((S, 512), F32), jax.ShapeDtypeStruct((S, D), BF16),
                   jax.ShapeDtypeStruct((S, D), BF16), jax.ShapeDtypeStruct((S, D), BF16),
                   jax.ShapeDtypeStruct((8, D), F32)],
        in_specs=[rows(D), rows(D), rows(512), rows(D, 9), rows(D, 10), rows(D),
                  whole((D, D)), whole((512, D)), whole((D, D)), whole((1, D))],
        out_specs=[rows(D), rows(D), rows(2 * D), rows(D), rows(512), rows(D), rows(D), rows(D), whole((8, D))],
        compiler_params=_cp(("arbitrary",)),
    )(x, o_a, o_bg, z, z, target, w_a, w_b, w_out, fnw)


def _tn_matmul(a, b, name):
    m, n = a.shape[1], b.shape[1]
    tn = 512

    def body(a_ref, b_ref, o_ref, ob_ref):
        acc = _dot_tn(a_ref[...], b_ref[...])
        o_ref[...] = acc
        ob_ref[...] = acc.astype(BF16)

    out_blk = pl.BlockSpec((m, tn), lambda j: (0, j))
    return pl.pallas_call(
        body, name=name, grid=(n // tn,),
        out_shape=[jax.ShapeDtypeStruct((m, n), F32), jax.ShapeDtypeStruct((m, n), BF16)],
        in_specs=[pl.BlockSpec((S, m), lambda j: (0, 0)), pl.BlockSpec((S, tn), lambda j: (0, j))],
        out_specs=[out_blk, out_blk],
        compiler_params=_cp(("parallel",)),
    )(a, b)


def _hgrn_bwd(z, o, do_a, states, lbv, hnw, partials):
    ntb, nch = S // HBLK, HBLK // CHUNK
    n = len(GATHER_IDS)

    def body(hq_ref, hf_ref, hi_ref, hg_ref, o_ref, doa_ref, st_ref, lb_ref, hnw_ref, p0, p1, p2,
             dhq_ref, dhf_ref, dhi_ref, dhg_ref, glb_ref, ghn_ref, e0, e1, e2, dstate, send_sems, recv_sems):
        start, end = _exchange_chips_steps((p0, p1, p2), (e0, e1, e2), send_sems, recv_sems)

        @pl.when(pl.program_id(0) == 0)
        def _():
            dstate[...] = jnp.zeros_like(dstate)
            glb_ref[...] = jnp.zeros_like(glb_ref)
            ghn_ref[...] = jnp.zeros_like(ghn_ref)
            start()

        lb_a = lb_ref[...]
        hq_a, hg_a = hq_ref[...], hg_ref[...]
        q_a, k_a, g_a, v_a, sg_a, f_a, b_a = _hgrn_cols(hq_a, hf_ref[...], hi_ref[...], lb_a)
        bex_a = b_a - g_a
        eb_a = jnp.exp(b_a)
        w = hnw_ref[...]
        mask = _tril_mask(CHUNK)
        upper = _block_tri(CHUNK, CHUNK, upper=True)
        for h in range(HEADS):
            cols = slice(128 * h, 128 * h + 128)
            q, k, v, b, bex, eb = q_a[:, cols], k_a[:, cols], v_a[:, cols], b_a[:, cols], bex_a[:, cols], eb_a[:, cols]
            hq, hg, sg, f, lb = hq_a[:, cols], hg_a[:, cols], sg_a[:, cols], f_a[:, cols], lb_a[:, cols]
            ov, doa = o_ref[:, cols], doa_ref[:, cols]
            r = lax.rsqrt(jnp.mean(ov * ov, axis=-1, keepdims=True) + EPS)
            n = ov * r
            sil = _silu(hg)
            dhg_ref[:, cols] = (doa * n * w * _dsilu(hg)).astype(BF16)
            ghn_ref[h] += jnp.sum(doa * sil * n, axis=0, keepdims=True)
            dn = doa * sil * w
            do = r * (dn - n * jnp.mean(dn * n, axis=-1, keepdims=True))

            dst = dstate[h]
            dq_l, dk_l, dv_l, dg_l = [None] * nch, [None] * nch, [None] * nch, [None] * nch
            for c in reversed(range(nch)):
                r0 = c * CHUNK
                rows = slice(r0, r0 + CHUNK)
                st = st_ref[h, c]
                bc, kc, qc = b[rows], k[rows], q[rows]
                vb, dob = v[rows].astype(BF16), do[rows].astype(BF16)
                b_last = bc[CHUNK - 1:CHUNK]
                e_last = jnp.exp(b_last)
                ekl = jnp.exp(b_last - bc)
                dstb = dst.astype(BF16)
                a, qs_l, ek_l, eq_l = _chunk_scores(q, k, b, bex, r0, mask)
                da = jnp.where(mask, _dot_nt(dob, vb), 0.0)
                dv_l[c] = _dot_tn(a.astype(BF16), dob) + _dot_nt((kc * ekl).astype(BF16), dstb)
                dq_inter = _dot(dob, st.astype(BF16)) * eb[rows]
                dk_state = _dot(vb, dstb) * ekl
                dq_parts, dk_intra = [], jnp.zeros((CHUNK, 128), F32)
                for i in range(CHUNK // SUB):
                    da_i = da[SUB * i:SUB * (i + 1)]
                    dq_parts.append(_dot3(_dot, da_i, kc * ek_l[i]) * eq_l[i])
                    dk_intra = dk_intra + _dot3(_dot_tn, da_i, qs_l[i]) * ek_l[i]
                dq = jnp.concatenate(dq_parts, axis=0) + dq_inter
                dk = dk_intra + dk_state
                last = (e_last * jnp.sum(st * dst, axis=0, keepdims=True)
                        + jnp.sum(kc * dk_state, axis=0, keepdims=True))
                dg_l[c] = _dot_ones(upper, qc * dq - kc * dk) + last
                dq_l[c], dk_l[c] = dq, dk
                dst = dst * e_last + _dot_tn(dob, (qc * eb[rows]).astype(BF16))
            dstate[h] = dst
            dq, dk = jnp.concatenate(dq_l, axis=0), jnp.concatenate(dk_l, axis=0)
            dg, dv = jnp.concatenate(dg_l, axis=0), jnp.concatenate(dv_l, axis=0)
            dhq_ref[:, cols] = (dq * _dsilu(hq)).astype(BF16)
            dhi_ref[:, cols] = dv.astype(BF16)
            df = dg / f - dk
            dhf_ref[:, cols] = (df * (1.0 - lb) * sg * (1.0 - sg)).astype(BF16)
            glb_ref[:, cols] += jnp.sum(df * (1.0 - sg), axis=0, keepdims=True)

        pl.when(pl.program_id(0) == ntb - 1)(end)

    def rev(t):
        return ntb - 1 - t

    def zcol(j):
        return pl.BlockSpec((HBLK, D), lambda t: (rev(t), j))

    blk = pl.BlockSpec((HBLK, D), lambda t: (rev(t), 0))
    any_spec = pl.BlockSpec(memory_space=pl.ANY)
    return pl.pallas_call(
        body, name="hgrn_bwd", grid=(ntb,),
        out_shape=[jax.ShapeDtypeStruct((S, D), BF16)] * 4
        + [jax.ShapeDtypeStruct((1, D), F32), jax.ShapeDtypeStruct((HEADS, 1, 128), F32)]
        + [jax.ShapeDtypeStruct((3,) + SHARD_SHAPES[a], BF16) for a in GATHER_IDS],
        in_specs=[zcol(0), zcol(1), zcol(2), zcol(3), blk, blk,
                  pl.BlockSpec((HEADS, nch, 128, 128), lambda t: (0, rev(t), 0, 0)),
                  pl.BlockSpec((1, D), lambda t: (0, 0)), pl.BlockSpec((1, 128), lambda t: (0, 0))] + [any_spec] * n,
        out_specs=[blk] * 4 + [pl.BlockSpec((1, D), lambda t: (0, 0)),
                               pl.BlockSpec((HEADS, 1, 128), lambda t: (0, 0, 0))] + [any_spec] * n,
        scratch_shapes=[pltpu.VMEM((HEADS, 128, 128), F32), pltpu.SemaphoreType.DMA((3 * n,)),
                        pltpu.SemaphoreType.DMA((3 * n,))],
        compiler_params=_cp(("arbitrary",)),
    )(z, z, z, z, o, do_a, states, lbv, hnw, *partials)


def _attn_bwd(z, cc, ss, ob, lse, do_bg):
    def body(q_ref, k_ref, v_ref, ag_ref, cc_ref, ss_ref, ob_ref, lse_ref, dobg_ref,
             dq_ref, dk_ref, dv_ref, dag_ref,
             tmp, qs, ks, vs, dos, dqs, dks, dvs, dkp, dvp, do_t, ls0_t, ls1_t, dl0_t, dl1_t, ls0, ls1, dl0, dl1):
        g = pl.program_id(1)
        first_half = _half_mask()
        prev_ok, cur_ok = _attn_masks()
        lane = lax.broadcasted_iota(jnp.int32, (1, LANES), 1)
        heads = (lane < 64, lane >= 64)
        nblk = _group_blocks(g)
        cc_v, ss_v = cc_ref[...], ss_ref[...]

        @pl.when(g == 0)
        def _():
            ag, obv, dobg = ag_ref[...], ob_ref[...], dobg_ref[...]
            dag_ref[...] = (dobg * obv * _dsilu(ag)).astype(BF16)
            dob = dobg * _silu(ag)
            do_t[...] = dob
            prod = dob * obv
            dl = jnp.concatenate(
                [jnp.broadcast_to(jnp.sum(prod[:, 0:64], axis=-1, keepdims=True), (S, 64)),
                 jnp.broadcast_to(jnp.sum(prod[:, 64:128], axis=-1, keepdims=True), (S, 64))], axis=1)
            dl_sw = pltpu.roll(dl, 64, 1)
            dl0_t[...] = jnp.where(heads[0], dl, dl_sw)
            dl1_t[...] = jnp.where(heads[0], dl_sw, dl)
            ls = lse_ref[...]
            ls_sw = pltpu.roll(ls, 64, 1)
            ls0_t[...] = jnp.where(heads[0], ls, ls_sw)
            ls1_t[...] = jnp.where(heads[0], ls_sw, ls)
            ks[0:ATT_PAD, :] = jnp.zeros((ATT_PAD, LANES), BF16)
            vs[0:ATT_PAD, :] = jnp.zeros((ATT_PAD, LANES), BF16)

        tmp[...] = _rope(q_ref[...], cc_v, ss_v, first_half) * ATT_SCALE
        _to_residues_dyn(g, qs, tmp)
        tmp[...] = _rope(k_ref[...], cc_v, ss_v, first_half)
        _to_residues_dyn(g, ks, tmp, ATT_PAD, BF16)
        _to_residues_dyn(g, vs, v_ref, ATT_PAD, BF16)
        _to_residues_dyn(g, dos, do_t)
        _to_residues_dyn(g, ls0, ls0_t)
        _to_residues_dyn(g, ls1, ls1_t)
        _to_residues_dyn(g, dl0, dl0_t)
        _to_residues_dyn(g, dl1, dl1_t)
        lss, dls = (ls0, ls1), (dl0, dl1)

        def unit(u, carry):
            start = pl.multiple_of(u * 128, 128)
            cur = pl.ds(start, 128)
            both = pl.ds(start, 256)
            pm = prev_ok & ((u & (nblk - 1)) != 0)
            qu, dou = qs[cur, :], dos[cur, :]
            kcat, vcat = ks[both, :], vs[both, :]
            dq_u = None
            q_l, do_l, ds_l, p_l = [], [], [], []
            for hh in range(2):
                q_h = jnp.where(heads[hh], qu, 0.0).astype(BF16)
                do_h = jnp.where(heads[hh], dou, 0.0).astype(BF16)
                s = _dot_nt(q_h, kcat)
                dp = _dot_nt(do_h, vcat)
                lse_h, dl_h = lss[hh][cur, :], dls[hh][cur, :]
                pp = jnp.where(pm, jnp.exp(s[:, 0:128] - lse_h), 0.0)
                pc = jnp.where(cur_ok, jnp.exp(s[:, 128:256] - lse_h), 0.0)
                ds = jnp.concatenate([pp * (dp[:, 0:128] - dl_h), pc * (dp[:, 128:256] - dl_h)], axis=1).astype(BF16)
                dq = _dot(ds, kcat)
                dq_u = dq if hh == 0 else jnp.where(heads[1], dq, dq_u)
                q_l.append(q_h)
                do_l.append(do_h)
                ds_l.append(ds)
                p_l.append(jnp.concatenate([pp, pc], axis=1).astype(BF16))
            dkcat = _dot_tn(jnp.concatenate(ds_l, axis=0), jnp.concatenate(q_l, axis=0))
            dvcat = _dot_tn(jnp.concatenate(p_l, axis=0), jnp.concatenate(do_l, axis=0))
            dkp[cur, :] = dkcat[0:128]
            dks[cur, :] = dkcat[128:256]
            dvp[cur, :] = dvcat[0:128]
            dvs[cur, :] = dvcat[128:256]
            dqs[cur, :] = dq_u
            return carry

        lax.fori_loop(0, 16, unit, 0, unroll=ATT_UNROLL)
        dks[0:S - 128, :] += dkp[128:S, :]
        dvs[0:S - 128, :] += dvp[128:S, :]
        _from_residues_dyn(g, tmp, dqs)
        dq_ref[0] = (_rope(tmp[...], cc_v, -ss_v, first_half) * ATT_SCALE).astype(BF16)
        _from_residues_dyn(g, tmp, dks)
        dk_ref[0] = _rope(tmp[...], cc_v, -ss_v, first_half).astype(BF16)
        _from_residues_dyn(g, tmp, dvs)
        dv_ref[0] = tmp[...].astype(BF16)

    grp = pl.BlockSpec((1, S, LANES), lambda p, g: (g, 0, p))
    buf = pltpu.VMEM((S, LANES), F32)
    padded_b = pltpu.VMEM((ATT_PAD + S, LANES), BF16)
    return pl.pallas_call(
        body, name="attn_bwd", grid=(4, 3),
        out_shape=[jax.ShapeDtypeStruct((3, S, 512), BF16)] * 3 + [jax.ShapeDtypeStruct((S, 512), BF16)],
        in_specs=_attn_in_specs(3), out_specs=[grp, grp, grp, pl.BlockSpec((S, LANES), lambda p, g: (0, p))],
        scratch_shapes=[buf, buf, padded_b, padded_b] + [buf] * 15,
        compiler_params=_cp(("parallel", "arbitrary")),
    )(z, z, z, z, cc, ss, ob, lse, do_bg)


def _in_proj_bwd(dz, h, w_in):
    half = S // 2
    slab = (D, SHARD_COLS)

    def body(dz_hbm, h_hbm, w_hbm, dh_hbm, g_chip, r1_hbm, relay_hbm, r2_hbm,
             h_buf, dz_buf, stage_d, r1_buf, stage_i, acc,
             dz_sem, w_sem, h_sem, r1_sem, out_sem, send_d, recv_d, send_i, recv_i):
        x, y, c = _mesh_pos()
        sibling = (x, y, 1 - c)
        north = c == 1
        near = (jnp.where(north, 1 - x, x), jnp.where(north, y, 1 - y))
        far = (jnp.where(north, x, 1 - x), jnp.where(north, 1 - y, y))
        chips = [(1 - x, 1 - y), near, far, (x, y)]

        def cols(d):
            return pl.ds(pl.multiple_of(d * SHARD_COLS, LANES), SHARD_COLS)

        blocks = []
        for q_sib, q in zip([chips[0], far, near, chips[3]], chips):
            blocks += [4 * q_sib[0] + 2 * q_sib[1] + (1 - c), 4 * q[0] + 2 * q[1] + c]

        def dz_tile(t):
            return pltpu.make_async_copy(dz_hbm.at[pl.ds((t % 2) * half, half), cols(blocks[t // 2])],
                                         dz_buf.at[t % 2], dz_sem.at[t % 2])

        def to_sibling(i):
            return pltpu.make_async_remote_copy(
                src_ref=stage_d.at[i % 2], dst_ref=r1_hbm.at[i], send_sem=send_d.at[i], recv_sem=recv_d.at[i],
                device_id=sibling, device_id_type=MESH)

        def to_owner(i):
            dst = relay_hbm if i == 0 else r2_hbm.at[i - 1]
            return pltpu.make_async_remote_copy(
                src_ref=stage_i.at[i], dst_ref=dst, send_sem=send_i.at[i], recv_sem=recv_i.at[i],
                device_id=(*(far if i == 2 else near), c), device_id_type=MESH)

        h_copy = pltpu.make_async_copy(h_hbm, h_buf, h_sem)
        h_copy.start()
        dz_tile(0).start()
        h_copy.wait()
        for b in range(8):
            i = b // 2
            g = None
            for r in range(2):
                t = 2 * b + r
                if t + 1 < 16:
                    dz_tile(t + 1).start()
                dz_tile(t).wait()
                part = _dot_tn(h_buf[r * half:(r + 1) * half, :], dz_buf[t % 2])
                g = part if g is None else g + part
                if b % 2 == 1 and r == 0:
                    to_sibling(i).wait_recv()
                    r1_copy = pltpu.make_async_copy(r1_hbm.at[i], r1_buf, r1_sem)
                    r1_copy.start()
            if b % 2 == 0:
                if i >= 2:
                    to_sibling(i - 2).wait_send()
                stage_d[i % 2] = g.astype(BF16)
                to_sibling(i).start()
            else:
                r1_copy.wait()
                g = g + r1_buf[...].astype(F32)
                if i == 2:
                    to_owner(0).wait_recv()
                    relay_copy = pltpu.make_async_copy(relay_hbm, r1_buf, r1_sem)
                    relay_copy.start()
                    relay_copy.wait()
                    g = g + r1_buf[...].astype(F32)
                if i < 3:
                    stage_i[i] = g.astype(BF16)
                    to_owner(i).start()
                else:
                    g_chip[...] = g
        to_sibling(2).wait_send()
        to_sibling(3).wait_send()

        def dz2(t):
            return pltpu.make_async_copy(
                dz_hbm.at[pl.ds((t // 8) * half, half), pl.ds((t % 8) * SHARD_COLS, SHARD_COLS)],
                dz_buf.at[t % 2], dz_sem.at[t % 2])

        def w2(t):
            return pltpu.make_async_copy(w_hbm.at[:, pl.ds((t % 8) * SHARD_COLS, SHARD_COLS)],
                                         stage_d.at[t % 2], w_sem.at[t % 2])

        def dh_out(r):
            return pltpu.make_async_copy(acc, dh_hbm.at[pl.ds(r * half, half), :], out_sem)

        dz2(0).start()
        w2(0).start()
        for t in range(16):
            if t + 1 < 16:
                dz2(t + 1).start()
                w2(t + 1).start()
            dz2(t).wait()
            w2(t).wait()
            part = _dot_nt(dz_buf[t % 2], stage_d[t % 2])
            if t % 8 == 0:
                if t > 0:
                    dh_out(0).wait()
                acc[...] = part
            else:
                acc[...] += part
            if t % 8 == 7:
                dh_out(t // 8).start()
        dh_out(1).wait()
        for i in range(3):
            to_owner(i).wait_send()
        for i in (1, 2):
            to_owner(i).wait_recv()

    any_spec = pl.BlockSpec(memory_space=pl.ANY)
    return pl.pallas_call(
        body, name="in_proj_bwd",
        out_shape=[jax.ShapeDtypeStruct((S, D), F32), jax.ShapeDtypeStruct(slab, F32),
                   jax.ShapeDtypeStruct((4,) + slab, BF16), jax.ShapeDtypeStruct(slab, BF16),
                   jax.ShapeDtypeStruct((2,) + slab, BF16)],
        in_specs=[any_spec] * 3,
        out_specs=[any_spec, pl.BlockSpec(memory_space=pltpu.VMEM), any_spec, any_spec, any_spec],
        scratch_shapes=[pltpu.VMEM((S, D), BF16), pltpu.VMEM((2, half, SHARD_COLS), BF16),
                        pltpu.VMEM((2,) + slab, BF16), pltpu.VMEM(slab, BF16), pltpu.VMEM((3,) + slab, BF16),
                        pltpu.VMEM((half, D), F32),
                        pltpu.SemaphoreType.DMA((2,)), pltpu.SemaphoreType.DMA((2,)), pltpu.SemaphoreType.DMA,
                        pltpu.SemaphoreType.DMA, pltpu.SemaphoreType.DMA,
                        pltpu.SemaphoreType.DMA((4,)), pltpu.SemaphoreType.DMA((4,)),
                        pltpu.SemaphoreType.DMA((3,)), pltpu.SemaphoreType.DMA((3,))],
        compiler_params=_cp(),
    )(dz, h, w_in)


def _grad_x(x, norm_w, dh, dx2):
    tr = 256

    def body(x_ref, w_ref, dh_ref, dx2_ref, gx_ref, gnw_ref):
        @pl.when(pl.program_id(0) == 0)
        def _():
            gnw_ref[...] = jnp.zeros_like(gnw_ref)

        xv, dhv = x_ref[...], dh_ref[...]
        r = lax.rsqrt(jnp.mean(xv * xv, axis=-1, keepdims=True) + EPS)
        n = xv * r
        gnw_ref[...] += jnp.sum(dhv * n, axis=0, keepdims=True)
        dn = dhv * w_ref[...]
        gx_ref[...] = dx2_ref[...] + r * (dn - n * jnp.mean(dn * n, axis=-1, keepdims=True))

    row = pl.BlockSpec((tr, D), lambda i: (i, 0))
    vec = pl.BlockSpec((1, D), lambda i: (0, 0))
    return pl.pallas_call(
        body, name="grad_x", grid=(S // tr,),
        out_shape=[jax.ShapeDtypeStruct((S, D), F32), jax.ShapeDtypeStruct((1, D), F32)],
        in_specs=[row, vec, row, row], out_specs=[row, vec],
        compiler_params=_cp(("arbitrary",)),
    )(x, norm_w, dh, dx2)


def _rope_tables(positions):
    inv_freq = 10000.0 ** (-jnp.arange(0, 64, 2, dtype=F32) / 64)
    ang = positions.astype(F32)[:, None] * inv_freq[None, :]
    cos, sin = jnp.cos(ang), jnp.sin(ang)
    return jnp.tile(cos, (1, 4)), jnp.tile(jnp.concatenate([-sin, sin], axis=1), (1, 2))


def _local_step(x, positions, norm_w, lb_logits, hnw, fnw, target, w_in_shard, small_shards, core):
    cc, ss = _rope_tables(positions)
    lbv = jax.nn.sigmoid(lb_logits[0:1] - lb_logits[1:2])
    h = _rmsnorm_in(x, norm_w)
    z, w_in = _in_proj_gather(h, w_in_shard)
    o, o_a, states, w_a, w_b, w_out = _hgrn_fwd(z, lbv, hnw, small_shards)
    ob, lse, o_bg = _attn_fwd(z, cc, ss)
    dx2, dx2b, dgp, do_a, do_bg, merged, dy_a, dy_b, tail_small = _tail(x, o_a, o_bg, z, target, w_a, w_b, w_out, fnw)
    g_out, gb_out = _tn_matmul(merged, dx2b, "grad_w_out")
    g_a, gb_a = _tn_matmul(o_a, dy_a, "grad_w_a")
    g_b, gb_b = _tn_matmul(o_bg, dy_b, "grad_w_b")
    grads, gb = (g_a, g_b, g_out), (gb_a, gb_b, gb_out)
    r1 = _exchange_sibling(GATHER_IDS, gb)
    pb = [_chip_partials(a, grads[i], r1[i], core) for i, a in enumerate(GATHER_IDS)]
    dhq, dhf, dhi, dhg, glb, ghn, *r2 = _hgrn_bwd(z, o, do_a, states, lbv, hnw, pb)
    dq, dk, dv, dag = _attn_bwd(z, cc, ss, ob, lse, do_bg)
    dz = jnp.concatenate([dhq, dhf, dhi, dhg, dq[0], dq[1], dq[2], dk[0], dk[1], dk[2], dv[0], dv[1], dv[2], dag, dgp],
                         axis=1)
    dh, g_chip_in, _, _, r2_in = _in_proj_bwd(dz, h, w_in)
    grad_x, gnw = _grad_x(x, norm_w, dh, dx2)
    ghn_row = jnp.pad(jnp.sum(ghn, axis=0), ((0, 0), (0, D - 128)))
    small = jnp.concatenate([gnw, glb, ghn_row, tail_small[0:2], jnp.zeros((3, D), F32)], axis=0)
    return grad_x, (g_chip_in, r2_in), grads, r1, r2, small


def kernel(x, positions, norm_w, w_in, lb_logits, hgrn_norm_w, w_branch_a, w_branch_b, w_out, final_norm_w, loss_target, m_norm_w, m_w_in, m_lb_logits, m_hgrn_norm_w, m_w_branch_a, m_w_branch_b, m_w_out, m_final_norm_w, v_norm_w, v_w_in, v_lb_logits, v_hgrn_norm_w, v_w_branch_a, v_w_branch_b, v_w_out, v_final_norm_w):
    ix, iy, ic = _mesh_pos()
    core = jnp.reshape(ic, (1,)).astype(jnp.int32)
    pos = jnp.stack([4 * ix + 2 * iy + ic, 2 * ix + iy]).astype(jnp.int32)

    shards = [w_in[0], w_branch_a[0], w_branch_b[0], w_out[0]]
    moments_m = [m_w_in[0], m_w_branch_a[0], m_w_branch_b[0], m_w_out[0]]
    moments_v = [v_w_in[0], v_w_branch_a[0], v_w_branch_b[0], v_w_out[0]]
    names = ("w_in", "w_a", "w_b", "w_out")
    ids = GATHER_IDS
    shards_b = [_cast_bf16(w, f"cast_{nm}") for w, nm in zip(shards, names)]

    fnw2 = final_norm_w.reshape(1, D)
    grad_x, (g_chip_in, r2_in), grads, r1, r2, small = _local_step(
        x[0], positions[0], norm_w, lb_logits, hgrn_norm_w, fnw2, loss_target[0], shards_b[0], shards_b[1:], core)

    gathered = _gather_small(small)
    big =[_reduce_own_and_update(shards[0], moments_m[0], moments_v[0], g_chip_in, r2_in)]
    big += [_reduce_and_update(a, shards[a], moments_m[a], moments_v[a], grads[i], r1[i], r2[i], pos)
            for i, a in enumerate(ids)]
    sm = _small_update(gathered, norm_w, lb_logits, hgrn_norm_w, fnw2,
                       (m_norm_w, m_lb_logits, m_hgrn_norm_w, m_final_norm_w.reshape(1, D),
                        v_norm_w, v_lb_logits, v_hgrn_norm_w, v_final_norm_w.reshape(1, D)))
    loss = sm[0][0, 0]
    outs = [loss, grad_x[None]]
    for kind in range(4):
        s_nw, s_lb, s_hn, s_fn = sm[1 + 4 * kind:5 + 4 * kind]
        outs += [s_nw, big[0][kind][None], s_lb, s_hn, big[1][kind][None], big[2][kind][None],
                 big[3][kind][None], s_fn.reshape(D)]
    return tuple(outs)
```

```python
import functools

import jax
import jax.numpy as jnp
from jax import lax
from jax.experimental import pallas as pl
from jax.experimental.pallas import tpu as pltpu

F32 = jnp.float32
BF16 = jnp.bfloat16
MESH = pl.DeviceIdType.MESH

S = 2048
D = 1024
NDEV = 8
HEADS = 8
CHUNK = 64
SUB = 16
HBLK = 128
ATT_PAD = 128
ATT_UNROLL = 4
EXP_CLAMP = 80.0
EPS = 1e-6
IN_COLS = 11264
SHARD_COLS = IN_COLS // NDEV
ATT_DILS = (1, 4, 16)
ATT_SCALE = 64 ** -0.5
LANES = 128

ADAM_LR, ADAM_B1, ADAM_B2, ADAM_EPS, ADAM_WD, ADAM_STEP = 0.001, 0.9, 0.999, 1e-08, 0.01, 10

VMEM_LIMIT = 56 * 1024 * 1024


def _cp(sem=None, **kw):
    return pltpu.CompilerParams(dimension_semantics=sem, vmem_limit_bytes=VMEM_LIMIT, **kw)


def _dot(a, b):
    return jnp.dot(a, b, preferred_element_type=F32)


def _dot_nt(a, b):
    return lax.dot_general(a, b, (((1,), (1,)), ((), ())), preferred_element_type=F32)


def _dot_tn(a, b):
    return lax.dot_general(a, b, (((0,), (0,)), ((), ())), preferred_element_type=F32)


def _split2(x):
    hi = x.astype(BF16)
    lo = (x - hi.astype(F32)).astype(BF16)
    return hi, lo


def _split3(x):
    hi = x.astype(BF16)
    r = x - hi.astype(F32)
    mid = r.astype(BF16)
    lo = (r - mid.astype(F32)).astype(BF16)
    return hi, mid, lo


def _dot_ones(ones_bf16, x):
    hi, mid, lo = _split3(x)
    return _dot(ones_bf16, hi) + _dot(ones_bf16, mid) + _dot(ones_bf16, lo)


def _dot3(dotfn, a, b):
    ah, al = _split2(a)
    bh, bl = _split2(b)
    return dotfn(ah, bh) + dotfn(ah, bl) + dotfn(al, bh)


def _silu(x):
    return x * jax.nn.sigmoid(x)


def _dsilu(x):
    s = jax.nn.sigmoid(x)
    return s * (1.0 + x * (1.0 - s))


def _mesh_pos():
    return lax.axis_index("x"), lax.axis_index("y"), lax.axis_index("c")


def _shard_of(ref, a, d):
    if a == 0:
        return ref.at[:, pl.ds(pl.multiple_of(d * SHARD_COLS, LANES), SHARD_COLS)]
    if a == 2:
        return ref.at[:, pl.ds(pl.multiple_of(d * LANES, LANES), LANES)]
    return ref.at[pl.ds(pl.multiple_of(d * 128, 128), 128), :]


FULL_SHAPES = ((D, IN_COLS), (D, D), (512, D), (D, D))
SHARD_SHAPES = ((D, SHARD_COLS), (128, D), (512, 128), (128, D))


def _allgather_steps(ids, ins, outs, send_sems, recv_sems, local_sems):
    n = len(ids)
    x, y, c = _mesh_pos()
    me, sibling = (x, y, c), (x, y, 1 - c)
    chips = [(1 - x, y), (x, 1 - y), (1 - x, 1 - y)]

    def blk(a, p):
        return _shard_of(outs[a], ids[a], 4 * p[0] + 2 * p[1] + p[2])

    def copy(a, k, block, to, src=None):
        return pltpu.make_async_remote_copy(
            src_ref=blk(a, block) if src is None else src, dst_ref=blk(a, block),
            send_sem=send_sems.at[a * 7 + k], recv_sem=recv_sems.at[a * 7 + k],
            device_id=to, device_id_type=MESH)

    mine = [pltpu.make_async_copy(ins[a], blk(a, me), local_sems.at[a]) for a in range(n)]
    first = []
    for a in range(n):
        first += [copy(a, 1 + j, me, (*chip, c), src=ins[a]) for j, chip in enumerate(chips)]
    for a in range(n):
        first.append(copy(a, 0, me, sibling, src=ins[a]))
    passed = [copy(a, 4 + j, (*chip, c), sibling) for j, chip in enumerate(chips) for a in range(n)]

    def start():
        for cp in mine + first:
            cp.start()

    def middle():
        for j, chip in enumerate(chips):
            for a in range(n):
                copy(a, 1 + j, (*chip, c), me).wait_recv()
                passed[j * n + a].start()

    def end():
        for a in range(n):
            copy(a, 0, sibling, me).wait_recv()
        for j, chip in enumerate(chips):
            for a in range(n):
                copy(a, 4 + j, (*chip, 1 - c), me).wait_recv()
        for cp in first + passed:
            cp.wait_send()
        for cp in mine:
            cp.wait()

    return start, middle, end


def _in_proj_gather(h, w_shard):
    half = S // 2
    slab = (D, SHARD_COLS)

    def body(h_hbm, w_hbm, z_hbm, wfull_hbm, h_buf, land, zstage,
             h_sem, own_sem, z_sem, wout_sem, send_sems, recv_sems):
        x, y, c = _mesh_pos()
        sibling = (x, y, 1 - c)
        north = c == 1

        def chips_of(first_x):
            near = (jnp.where(first_x, 1 - x, x), jnp.where(first_x, y, 1 - y))
            far = (jnp.where(first_x, x, 1 - x), jnp.where(first_x, 1 - y, y))
            return [near, far, (1 - x, 1 - y)]

        mine, theirs = chips_of(north), chips_of(jnp.logical_not(north))

        def dev(chip, core):
            return 4 * chip[0] + 2 * chip[1] + core

        block_of = ([dev((x, y), c), dev((x, y), 1 - c)] + [dev(q, c) for q in mine]
                    + [dev(q, 1 - c) for q in theirs])

        def cols(d):
            if isinstance(d, int):
                return pl.ds(d * SHARD_COLS, SHARD_COLS)
            return pl.ds(pl.multiple_of(d * SHARD_COLS, LANES), SHARD_COLS)

        def send(k, src, dst_slot, to):
            return pltpu.make_async_remote_copy(
                src_ref=src, dst_ref=land.at[dst_slot], send_sem=send_sems.at[k], recv_sem=recv_sems.at[k],
                device_id=to, device_id_type=MESH)

        def to_sibling():
            return send(0, w_hbm, 1, sibling)

        def to_chip(j):
            if j == 2:
                return send(3, land.at[2], 4, (*mine[1], c))
            return send(1 + j, w_hbm, 2 + j, (*mine[j], c))

        def pass_on(j):
            return send(4 + j, land.at[2 + j], 5 + j, sibling)

        own = pltpu.make_async_copy(w_hbm, land.at[0], own_sem)
        h_copy = pltpu.make_async_copy(h_hbm, h_buf, h_sem)
        own.start()
        h_copy.start()
        to_sibling().start()
        to_chip(0).start()
        h_copy.wait()
        own.wait()

        def multiply(slot, n_done):
            d = block_of[slot]
            out = pltpu.make_async_copy(land.at[slot], wfull_hbm.at[:, cols(d)], wout_sem.at[slot])
            out.start()
            for r in range(2):
                rows = pl.ds(r * half, half)
                zc = pltpu.make_async_copy(zstage.at[r], z_hbm.at[rows, cols(d)], z_sem.at[r])
                if n_done > 0:
                    zc.wait()
                zstage[r] = _dot(h_buf[r * half:(r + 1) * half, :], land[slot])
                zc.start()
            return out

        outs = [multiply(0, 0)]
        to_sibling().wait_recv()
        outs.append(multiply(1, 1))
        done = 2
        for j in range(3):
            to_chip(j).wait_recv()
            pass_on(j).start()
            to_chip(j).wait_send()
            if j < 2:
                to_chip(j + 1).start()
            outs.append(multiply(2 + j, done))
            pass_on(j).wait_recv()
            outs.append(multiply(5 + j, done + 1))
            done += 2
        for r in range(2):
            pltpu.make_async_copy(zstage.at[r], z_hbm.at[pl.ds(r * half, half), cols(0)], z_sem.at[r]).wait()
        for out in outs:
            out.wait()
        to_sibling().wait_send()
        for j in range(3):
            pass_on(j).wait_send()

    any_spec = pl.BlockSpec(memory_space=pl.ANY)
    return pl.pallas_call(
        body, name="in_proj_gather",
        out_shape=[jax.ShapeDtypeStruct((S, IN_COLS), F32), jax.ShapeDtypeStruct((D, IN_COLS), BF16)],
        in_specs=[any_spec] * 2, out_specs=[any_spec] * 2,
        scratch_shapes=[pltpu.VMEM((S, D), BF16), pltpu.VMEM((8,) + slab, BF16), pltpu.VMEM((2, half, SHARD_COLS), F32),
                        pltpu.SemaphoreType.DMA, pltpu.SemaphoreType.DMA, pltpu.SemaphoreType.DMA((2,)),
                        pltpu.SemaphoreType.DMA((8,)), pltpu.SemaphoreType.DMA((7,)), pltpu.SemaphoreType.DMA((7,))],
        compiler_params=_cp(),
    )(h, w_shard)


def _exchange_sibling(ids, gb):
    n = len(gb)

    def body(*refs):
        ins, outs = refs[:n], refs[n:2 * n]
        send_sems, recv_sems = refs[2 * n:]
        x, y, c = _mesh_pos()
        sibling = (x, y, 1 - c)
        copies = []
        for i, a in enumerate(ids):
            for q in range(4):
                copies.append(pltpu.make_async_remote_copy(
                    src_ref=_shard_of(ins[i], a, 2 * q + (1 - c)), dst_ref=outs[i].at[q],
                    send_sem=send_sems.at[i * 4 + q], recv_sem=recv_sems.at[i * 4 + q],
                    device_id=sibling, device_id_type=MESH))
        for cp in copies:
            cp.start()
        for cp in copies:
            cp.wait()

    any_spec = pl.BlockSpec(memory_space=pl.ANY)
    return pl.pallas_call(
        body, name="grads_to_sibling",
        out_shape=[jax.ShapeDtypeStruct((4,) + SHARD_SHAPES[a], BF16) for a in ids],
        in_specs=[any_spec] * n, out_specs=[any_spec] * n,
        scratch_shapes=[pltpu.SemaphoreType.DMA((4 * n,)), pltpu.SemaphoreType.DMA((4 * n,))],
    )(*gb)


def _exchange_chips_steps(ins, outs, send_sems, recv_sems):
    x, y, c = _mesh_pos()
    chips = [(1 - x, y), (x, 1 - y), (1 - x, 1 - y)]
    copies = []
    for a in range(len(ins)):
        for k, chip in enumerate(chips):
            copies.append(pltpu.make_async_remote_copy(
                src_ref=ins[a].at[2 * chip[0] + chip[1]], dst_ref=outs[a].at[k],
                send_sem=send_sems.at[a * 3 + k], recv_sem=recv_sems.at[a * 3 + k],
                device_id=(*chip, c), device_id_type=MESH))

    def start():
        for cp in copies:
            cp.start()

    def end():
        for cp in copies:
            cp.wait()

    return start, end


def _gather_small(small):
    def body(small_ref, small_out, ssend, srecv, local_sem):
        x, y, c = _mesh_pos()
        me = 4 * x + 2 * y + c
        copies = []
        for r in range(1, NDEV):
            peer = (1 - x if r & 4 else x, 1 - y if r & 2 else y, 1 - c if r & 1 else c)
            copies.append(pltpu.make_async_remote_copy(
                src_ref=small_ref, dst_ref=small_out.at[me],
                send_sem=ssend.at[r - 1], recv_sem=srecv.at[r - 1],
                device_id=peer, device_id_type=MESH))
        own = pltpu.make_async_copy(small_ref, small_out.at[me], local_sem)
        own.start()
        for cp in copies:
            cp.start()
        for cp in copies:
            cp.wait()
        own.wait()

    any_spec = pl.BlockSpec(memory_space=pl.ANY)
    return pl.pallas_call(
        body, name="gather_small",
        out_shape=jax.ShapeDtypeStruct((NDEV,) + small.shape, F32),
        in_specs=[any_spec], out_specs=any_spec,
        scratch_shapes=[pltpu.SemaphoreType.DMA((NDEV - 1,)), pltpu.SemaphoreType.DMA((NDEV - 1,)),
                        pltpu.SemaphoreType.DMA],
    )(small)


def _shard_tiles(a):
    rows, cols = SHARD_SHAPES[a]
    tr = min(rows, 256)
    return (tr, cols), rows // tr


def _full_index(a, d, i):
    (tr, _), nt = _shard_tiles(a)
    if a in (0, 2):
        return (i, d)
    return (d * nt + i, 0)


def _cast_bf16(x, name):
    rows, cols = x.shape
    tr = min(rows, 256)

    def body(x_ref, o_ref):
        o_ref[...] = x_ref[...].astype(BF16)

    return pl.pallas_call(
        body, name=name, out_shape=jax.ShapeDtypeStruct(x.shape, BF16), grid=(rows // tr,),
        in_specs=[pl.BlockSpec((tr, cols), lambda i: (i, 0))],
        out_specs=pl.BlockSpec((tr, cols), lambda i: (i, 0)),
        compiler_params=_cp(("parallel",)),
    )(x)


def _chip_partials(a, g_full, r1, core):
    tile, nt = _shard_tiles(a)

    def body(c_ref, g_ref, r_ref, o_ref):
        o_ref[0] = (g_ref[...] + r_ref[0].astype(F32)).astype(BF16)

    grid_spec = pltpu.PrefetchScalarGridSpec(
        num_scalar_prefetch=1, grid=(4, nt),
        in_specs=[pl.BlockSpec(tile, lambda q, i, c: _full_index(a, 2 * q + c[0], i)),
                  pl.BlockSpec((1,) + tile, lambda q, i, c: (q, i, 0))],
        out_specs=pl.BlockSpec((1,) + tile, lambda q, i, c: (q, i, 0)))
    return pl.pallas_call(
        body, name=f"chip_partials_{a}", grid_spec=grid_spec,
        out_shape=jax.ShapeDtypeStruct((4,) + SHARD_SHAPES[a], BF16),
        compiler_params=_cp(("parallel", "parallel")),
    )(core, g_full, r1)


def _adam(w, g, m, v):
    m = ADAM_B1 * m + (1.0 - ADAM_B1) * g
    v = ADAM_B2 * v + (1.0 - ADAM_B2) * (g * g)
    m_hat = m / (1.0 - ADAM_B1 ** ADAM_STEP)
    v_hat = v / (1.0 - ADAM_B2 ** ADAM_STEP)
    delta = -ADAM_LR * (m_hat / (jnp.sqrt(v_hat) + ADAM_EPS) + ADAM_WD * w)
    return delta, m, v


def _reduce_and_update(a, w, m, v, g_full, r1, r2, pos):
    tile, nt = _shard_tiles(a)

    def body(p_ref, w_ref, m_ref, v_ref, g_ref, r1_ref, r2_ref, go_ref, do_ref, mo_ref, vo_ref):
        g = g_ref[...] + r1_ref[0].astype(F32)
        g = g + r2_ref[0].astype(F32)
        g = g + r2_ref[1].astype(F32)
        g = g + r2_ref[2].astype(F32)
        delta, m_new, v_new = _adam(w_ref[...], g, m_ref[...], v_ref[...])
        go_ref[...] = g
        do_ref[...] = delta
        mo_ref[...] = m_new
        vo_ref[...] = v_new

    own = pl.BlockSpec(tile, lambda i, p: (i, 0))
    grid_spec = pltpu.PrefetchScalarGridSpec(
        num_scalar_prefetch=1, grid=(nt,),
        in_specs=[own, own, own,
                  pl.BlockSpec(tile, lambda i, p: _full_index(a, p[0], i)),
                  pl.BlockSpec((1,) + tile, lambda i, p: (p[1], i, 0)),
                  pl.BlockSpec((3,) + tile, lambda i, p: (0, i, 0))],
        out_specs=[own] * 4)
    shp = jax.ShapeDtypeStruct(w.shape, F32)
    return pl.pallas_call(
        body, name=f"reduce_update_{a}", grid_spec=grid_spec, out_shape=[shp] * 4,
        compiler_params=_cp(("parallel",)),
    )(pos, w, m, v, g_full, r1, r2)


def _reduce_own_and_update(w, m, v, g_chip, r2):
    tile, nt = _shard_tiles(0)

    def body(w_ref, m_ref, v_ref, g_ref, r2_ref, go_ref, do_ref, mo_ref, vo_ref):
        g = g_ref[...] + r2_ref[0].astype(F32)
        g = g + r2_ref[1].astype(F32)
        delta, m_new, v_new = _adam(w_ref[...], g, m_ref[...], v_ref[...])
        go_ref[...] = g
        do_ref[...] = delta
        mo_ref[...] = m_new
        vo_ref[...] = v_new

    own = pl.BlockSpec(tile, lambda i: (i, 0))
    shp = jax.ShapeDtypeStruct(w.shape, F32)
    return pl.pallas_call(
        body, name="reduce_update_0", grid=(nt,), out_shape=[shp] * 4,
        in_specs=[own, own, own, own, pl.BlockSpec((2,) + tile, lambda i: (0, i, 0))], out_specs=[own] * 4,
        compiler_params=_cp(("parallel",)),
    )(w, m, v, g_chip, r2)


def _small_update(gathered, norm_w, lb_logits, hnw, fnw, moments):
    m_nw, m_lb, m_hn, m_fn, v_nw, v_lb, v_hn, v_fn = moments

    def body(g_ref, nw, lb, hn, fn, mnw, mlb, mhn, mfn, vnw, vlb, vhn, vfn,
             loss_o, g_nw, g_lb, g_hn, g_fn, d_nw, d_lb, d_hn, d_fn,
             mo_nw, mo_lb, mo_hn, mo_fn, vo_nw, vo_lb, vo_hn, vo_fn):
        tot = g_ref[0]
        for d in range(1, NDEV):
            tot = tot + g_ref[d]
        loss_o[...] = tot[4:5, 0:LANES]
        logits = lb[...]
        lbv = jax.nn.sigmoid(logits[0:1] - logits[1:2])
        chain = tot[1:2] * lbv * (1.0 - lbv)
        grads = (tot[0:1], jnp.concatenate([chain, -chain], axis=0), tot[2:3, 0:LANES], tot[3:4])
        outs = ((nw, mnw, vnw, g_nw, d_nw, mo_nw, vo_nw), (lb, mlb, vlb, g_lb, d_lb, mo_lb, vo_lb),
                (hn, mhn, vhn, g_hn, d_hn, mo_hn, vo_hn), (fn, mfn, vfn, g_fn, d_fn, mo_fn, vo_fn))
        for g, (w_r, m_r, v_r, g_o, d_o, m_o, v_o) in zip(grads, outs):
            delta, m_new, v_new = _adam(w_r[...], g, m_r[...], v_r[...])
            g_o[...] = g
            d_o[...] = delta
            m_o[...] = m_new
            v_o[...] = v_new

    shapes = [norm_w.shape, lb_logits.shape, hnw.shape, fnw.shape]
    out_shape = [jax.ShapeDtypeStruct((1, LANES), F32)] + [jax.ShapeDtypeStruct(s, F32) for s in shapes] * 4
    return pl.pallas_call(body, name="small_update", out_shape=out_shape, compiler_params=_cp())(
        gathered, norm_w, lb_logits, hnw, fnw, m_nw, m_lb, m_hn, m_fn, v_nw, v_lb, v_hn, v_fn)


def _rmsnorm_in(x, norm_w):
    tr = 512

    def body(x_ref, w_ref, h_ref):
        xv = x_ref[...]
        r = lax.rsqrt(jnp.mean(xv * xv, axis=-1, keepdims=True) + EPS)
        h_ref[...] = (xv * r * w_ref[...]).astype(BF16)

    return pl.pallas_call(
        body, name="rmsnorm_in", out_shape=jax.ShapeDtypeStruct((S, D), BF16), grid=(S // tr,),
        in_specs=[pl.BlockSpec((tr, D), lambda i: (i, 0)), pl.BlockSpec((1, D), lambda i: (0, 0))],
        out_specs=pl.BlockSpec((tr, D), lambda i: (i, 0)),
        compiler_params=_cp(("parallel",)),
    )(x, norm_w)


def _in_proj(h, w_in):
    tn = 1024

    def body(h_ref, w_ref, z_ref):
        z_ref[...] = _dot(h_ref[...], w_ref[...])

    return pl.pallas_call(
        body, name="in_proj", out_shape=jax.ShapeDtypeStruct((S, IN_COLS), F32), grid=(IN_COLS // tn,),
        in_specs=[pl.BlockSpec((S, D), lambda j: (0, 0)), pl.BlockSpec((D, tn), lambda j: (0, j))],
        out_specs=pl.BlockSpec((S, tn), lambda j: (0, j)),
        compiler_params=_cp(("parallel",)),
    )(h, w_in)


def _block_tri(n, block, upper=False):
    r = lax.broadcasted_iota(jnp.int32, (n, n), 0)
    c = lax.broadcasted_iota(jnp.int32, (n, n), 1)
    keep = (c >= r) if upper else (c <= r)
    return jnp.where(keep & ((r // block) == (c // block)), 1.0, 0.0).astype(BF16)


def _tril_mask(n):
    r = lax.broadcasted_iota(jnp.int32, (n, n), 0)
    c = lax.broadcasted_iota(jnp.int32, (n, n), 1)
    return c <= r


def _chunk_scores(q, k, b, bex, r0, mask):
    rows = slice(r0, r0 + CHUNK)
    parts, qs_l, ek_l, eq_l = [], [], [], []
    for i in range(CHUNK // SUB):
        ri = slice(r0 + SUB * i, r0 + SUB * (i + 1))
        base = bex[r0 + SUB * i:r0 + SUB * i + 1]
        eq = jnp.exp(b[ri] - base)
        ek = jnp.exp(jnp.minimum(base - b[rows], EXP_CLAMP))
        qs = q[ri] * eq
        parts.append(_dot_nt(qs.astype(BF16), (k[rows] * ek).astype(BF16)))
        qs_l.append(qs)
        ek_l.append(ek)
        eq_l.append(eq)
    return jnp.where(mask, jnp.concatenate(parts, axis=0), 0.0), qs_l, ek_l, eq_l


def _hgrn_cols(hq, hf, hi, lb):
    sg = jax.nn.sigmoid(hf)
    f = lb + (1.0 - lb) * sg
    g = jnp.log(f)
    b = _dot_ones(_block_tri(HBLK, CHUNK), g)
    return _silu(hq), 1.0 - f, g, hi, sg, f, b


GATHER_IDS = (1, 2, 3)


def _hgrn_fwd(z, lbv, hnw, shards):
    ntb, nch = S // HBLK, HBLK // CHUNK
    n = len(GATHER_IDS)

    def body(hq_ref, hf_ref, hi_ref, hg_ref, lb_ref, hnw_ref, s0, s1, s2, o_ref, oa_ref, st_ref, f0, f1, f2,
             state, send_sems, recv_sems, local_sems):
        start, middle, end = _allgather_steps(GATHER_IDS, (s0, s1, s2), (f0, f1, f2), send_sems, recv_sems, local_sems)

        @pl.when(pl.program_id(0) == 0)
        def _():
            state[...] = jnp.zeros_like(state)
            start()

        pl.when(pl.program_id(0) == ntb // 2)(middle)

        q_a, k_a, g_a, v_a, _, _, b_a = _hgrn_cols(hq_ref[...], hf_ref[...], hi_ref[...], lb_ref[...])
        bex_a = b_a - g_a
        eb_a = jnp.exp(b_a)
        mask = _tril_mask(CHUNK)
        hg = hg_ref[...]
        w = hnw_ref[...]
        for h in range(HEADS):
            cols = slice(128 * h, 128 * h + 128)
            q, k, v, b, bex, eb = q_a[:, cols], k_a[:, cols], v_a[:, cols], b_a[:, cols], bex_a[:, cols], eb_a[:, cols]
            st = state[h]
            outs = []
            for c in range(nch):
                r0 = c * CHUNK
                rows = slice(r0, r0 + CHUNK)
                a, _, _, _ = _chunk_scores(q, k, b, bex, r0, mask)
                vb = v[rows].astype(BF16)
                b_last = b[r0 + CHUNK - 1:r0 + CHUNK]
                qe = (q[rows] * eb[rows]).astype(BF16)
                outs.append(_dot(a.astype(BF16), vb) + _dot_nt(qe, st.astype(BF16)))
                st_ref[h, c] = st
                ke = (k[rows] * jnp.exp(b_last - b[rows])).astype(BF16)
                st = st * jnp.exp(b_last) + _dot_tn(vb, ke)
            state[h] = st
            o = jnp.concatenate(outs, axis=0)
            o_ref[:, cols] = o
            r = lax.rsqrt(jnp.mean(o * o, axis=-1, keepdims=True) + EPS)
            oa_ref[:, cols] = (o * r * w * _silu(hg[:, cols])).astype(BF16)

        pl.when(pl.program_id(0) == ntb - 1)(end)

    def zcol(j):
        return pl.BlockSpec((HBLK, D), lambda t: (t, j))

    out_blk = pl.BlockSpec((HBLK, D), lambda t: (t, 0))
    any_spec = pl.BlockSpec(memory_space=pl.ANY)
    return pl.pallas_call(
        body, name="hgrn_fwd", grid=(ntb,),
        out_shape=[jax.ShapeDtypeStruct((S, D), F32), jax.ShapeDtypeStruct((S, D), BF16),
                   jax.ShapeDtypeStruct((HEADS, S // CHUNK, 128, 128), F32)]
        + [jax.ShapeDtypeStruct(FULL_SHAPES[a], BF16) for a in GATHER_IDS],
        in_specs=[zcol(0), zcol(1), zcol(2), zcol(3),
                  pl.BlockSpec((1, D), lambda t: (0, 0)), pl.BlockSpec((1, 128), lambda t: (0, 0))] + [any_spec] * n,
        out_specs=[out_blk, out_blk, pl.BlockSpec((HEADS, nch, 128, 128), lambda t: (0, t, 0, 0))] + [any_spec] * n,
        scratch_shapes=[pltpu.VMEM((HEADS, 128, 128), F32), pltpu.SemaphoreType.DMA((7 * n,)),
                        pltpu.SemaphoreType.DMA((7 * n,)), pltpu.SemaphoreType.DMA((n,))],
        compiler_params=_cp(("arbitrary",)),
    )(z, z, z, z, lbv, hnw, *shards)


def _half_mask():
    lane = lax.broadcasted_iota(jnp.int32, (1, LANES), 1)
    return (lane % 64) < 32


def _rope(t, cc, ss, first_half):
    partner = jnp.where(first_half, pltpu.roll(t, 96, 1), pltpu.roll(t, 32, 1))
    return t * cc + partner * ss


def _attn_masks():
    i = lax.broadcasted_iota(jnp.int32, (128, 128), 0)
    j = lax.broadcasted_iota(jnp.int32, (128, 128), 1)
    return j >= i, j <= i


def _to_residues_dyn(g, dst, src, row0=0, dtype=None):
    for gi, dil in enumerate((1, 4, 16)):
        m = S // dil

        @pl.when(g == gi)
        def _(dil=dil, m=m):
            for r in range(dil):
                v = src[...] if dil == 1 else src[pl.ds(r, m, stride=dil), :]
                if dtype is not None:
                    v = v.astype(dtype)
                dst[row0 + r * m:row0 + (r + 1) * m, 0:LANES] = v


def _from_residues_dyn(g, dst, src, row0=0):
    for gi, dil in enumerate((1, 4, 16)):
        m = S // dil

        @pl.when(g == gi)
        def _(dil=dil, m=m):
            for r in range(dil):
                v = src[row0 + r * m:row0 + (r + 1) * m, :]
                if dil == 1:
                    dst[...] = v
                else:
                    dst[pl.ds(r, m, stride=dil), :] = v


def _group_blocks(g):
    return jnp.where(g == 0, 16, jnp.where(g == 1, 4, 1))


def _attn_in_specs(extra):
    def zcol(off):
        return pl.BlockSpec((S, LANES), lambda p, g: (0, off + 4 * g + p))

    per_pair = pl.BlockSpec((S, LANES), lambda p, g: (0, p))
    const = pl.BlockSpec((S, LANES), lambda p, g: (0, 0))
    return [zcol(32), zcol(44), zcol(56), pl.BlockSpec((S, LANES), lambda p, g: (0, 68 + p)), const, const] + [per_pair] * extra


def _attn_fwd(z, cc, ss):
    def body(q_ref, k_ref, v_ref, ag_ref, cc_ref, ss_ref, ob_ref, lse_ref, obg_ref,
             tmp, qs, ks, vx, og, mg, lg, o_t, m_t, l_t, o_acc, m_acc, l_acc):
        g = pl.program_id(1)
        first_half = _half_mask()
        prev_ok, cur_ok = _attn_masks()
        lane = lax.broadcasted_iota(jnp.int32, (1, LANES), 1)
        heads = (lane < 64, lane >= 64)
        nblk = _group_blocks(g)

        @pl.when(g == 0)
        def _():
            ks[0:ATT_PAD, :] = jnp.zeros((ATT_PAD, LANES), BF16)
            vx[0:ATT_PAD, 0:LANES] = jnp.zeros((ATT_PAD, LANES), BF16)
            vx[:, LANES:2 * LANES] = jnp.ones((ATT_PAD + S, LANES), BF16)

        tmp[...] = _rope(q_ref[...], cc_ref[...], ss_ref[...], first_half) * ATT_SCALE
        _to_residues_dyn(g, qs, tmp)
        tmp[...] = _rope(k_ref[...], cc_ref[...], ss_ref[...], first_half)
        _to_residues_dyn(g, ks, tmp, ATT_PAD, BF16)
        _to_residues_dyn(g, vx, v_ref, ATT_PAD, BF16)

        def unit(u, carry):
            start = pl.multiple_of(u * 128, 128)
            cur = pl.ds(start, 128)
            pm = prev_ok & ((u & (nblk - 1)) != 0)
            qu = qs[cur, :]
            kcat = ks[pl.ds(start, 256), :]
            vext = vx[pl.ds(start, 256), :]
            o_u = m_u = l_u = None
            for hh in range(2):
                s = _dot_nt(jnp.where(heads[hh], qu, 0.0).astype(BF16), kcat)
                sp = jnp.where(pm, s[:, 0:128], -jnp.inf)
                sc = jnp.where(cur_ok, s[:, 128:256], -jnp.inf)
                m = jnp.max(jnp.maximum(sp, sc), axis=-1, keepdims=True)
                p = jnp.concatenate([jnp.exp(sp - m), jnp.exp(sc - m)], axis=1).astype(BF16)
                ol = _dot(p, vext)
                mb = jnp.broadcast_to(m, (128, LANES))
                if hh == 0:
                    o_u, l_u, m_u = ol[:, 0:128], ol[:, 128:256], mb
                else:
                    o_u = jnp.where(heads[1], ol[:, 0:128], o_u)
                    l_u = jnp.where(heads[1], ol[:, 128:256], l_u)
                    m_u = jnp.where(heads[1], mb, m_u)
            og[cur, :] = o_u
            mg[cur, :] = m_u
            lg[cur, :] = l_u
            return carry

        lax.fori_loop(0, 16, unit, 0, unroll=ATT_UNROLL)
        _from_residues_dyn(g, o_t, og)
        _from_residues_dyn(g, m_t, mg)
        _from_residues_dyn(g, l_t, lg)

        @pl.when(g == 0)
        def _():
            o_acc[...] = o_t[...]
            m_acc[...] = m_t[...]
            l_acc[...] = l_t[...]

        @pl.when(g > 0)
        def _():
            m_new = jnp.maximum(m_acc[...], m_t[...])
            wa, wb = jnp.exp(m_acc[...] - m_new), jnp.exp(m_t[...] - m_new)
            o_acc[...] = o_acc[...] * wa + o_t[...] * wb
            l_acc[...] = l_acc[...] * wa + l_t[...] * wb
            m_acc[...] = m_new

        @pl.when(g == 2)
        def _():
            ob = o_acc[...] / l_acc[...]
            ob_ref[...] = ob
            lse_ref[...] = m_acc[...] + jnp.log(l_acc[...])
            obg_ref[...] = (ob * _silu(ag_ref[...])).astype(BF16)

    blk = pl.BlockSpec((S, LANES), lambda p, g: (0, p))
    buf = pltpu.VMEM((S, LANES), F32)
    return pl.pallas_call(
        body, name="attn_fwd", grid=(4, 3),
        out_shape=[jax.ShapeDtypeStruct((S, 512), F32), jax.ShapeDtypeStruct((S, 512), F32),
                   jax.ShapeDtypeStruct((S, 512), BF16)],
        in_specs=_attn_in_specs(0), out_specs=[blk, blk, blk],
        scratch_shapes=[buf, buf, pltpu.VMEM((ATT_PAD + S, LANES), BF16), pltpu.VMEM((ATT_PAD + S, 2 * LANES), BF16)] + [buf] * 9,
        compiler_params=_cp(("parallel", "arbitrary")),
    )(z, z, z, z, cc, ss)


def _tail(x, o_a, o_bg, z, target, w_a, w_b, w_out, fnw):
    tm = 256

    def body(x_ref, oa_ref, ob_ref, gpa_ref, gpb_ref, t_ref, wa_ref, wb_ref, wo_ref, fnw_ref,
             dx2_ref, dx2b_ref, dgp_ref, doa_ref, dob_ref, mg_ref, dya_ref, dyb_ref, small_ref):
        @pl.when(pl.program_id(0) == 0)
        def _():
            small_ref[...] = jnp.zeros_like(small_ref)

        wa, wb, wo = wa_ref[...], wb_ref[...], wo_ref[...]
        y_a = _dot(oa_ref[...], wa)
        y_b = _dot(ob_ref[...], wb)
        ga = jax.nn.sigmoid(gpa_ref[...])
        gb = jax.nn.sigmoid(gpb_ref[...])
        merged = (ga * y_a + gb * y_b).astype(BF16)
        x2 = x_ref[...] + _dot(merged, wo)
        r2 = lax.rsqrt(jnp.mean(x2 * x2, axis=-1, keepdims=True) + EPS)
        n2 = x2 * r2
        fw = fnw_ref[...]
        err = n2 * fw - t_ref[...]
        loss = 0.5 * jnp.sum(jnp.sum(err * err, axis=-1, keepdims=True), axis=0, keepdims=True) / D
        dy = err * (1.0 / D)
        g_fnw = jnp.sum(dy * n2, axis=0, keepdims=True)
        dn = dy * fw
        dx2 = r2 * (dn - n2 * jnp.mean(dn * n2, axis=-1, keepdims=True))
        dx2b = dx2.astype(BF16)
        dmerged = _dot_nt(dx2b, wo)
        dy_a = (dmerged * ga).astype(BF16)
        dy_b = (dmerged * gb).astype(BF16)
        dx2_ref[...] = dx2
        dx2b_ref[...] = dx2b
        dgp_ref[:, 0:D] = (dmerged * y_a * ga * (1.0 - ga)).astype(BF16)
        dgp_ref[:, D:2 * D] = (dmerged * y_b * gb * (1.0 - gb)).astype(BF16)
        doa_ref[...] = _dot_nt(dy_a, wa)
        dob_ref[...] = _dot_nt(dy_b, wb)
        mg_ref[...] = merged
        dya_ref[...] = dy_a
        dyb_ref[...] = dy_b
        small_ref[0:1, :] += g_fnw
        small_ref[1:2, :] += jnp.broadcast_to(loss, (1, D))

    def rows(cols, off=0):
        return pl.BlockSpec((tm, cols), lambda i: (i, off))

    def whole(shape):
        return pl.BlockSpec(shape, lambda i: (0, 0))

    return pl.pallas_call(
        body, name="tail", grid=(S // tm,),
        out_shape=[jax.ShapeDtypeStruct((S, D), F32), jax.ShapeDtypeStruct((S, D), BF16),
                   jax.ShapeDtypeStruct((S, 2 * D), BF16), jax.ShapeDtypeStruct((S, D), F32),
                   jax.ShapeDtypeStruct((S, 512), F32), jax.ShapeDtypeStruct((S, D), BF16),
                   jax.ShapeDtypeStruct((S, D), BF16), jax.ShapeDtypeStruct((S, D), BF16),
                   jax.ShapeDtypeStruct((8, D), F32)],
        in_specs=[rows(D), rows(D), rows(512), rows(D, 9), rows(D, 10), rows(D),
                  whole((D, D)), whole((512, D)), whole((D, D)), whole((1, D))],
        out_specs=[rows(D), rows(D), rows(2 * D), rows(D), rows(512), rows(D), rows(D), rows(D), whole((8, D))],
        compiler_params=_cp(("arbitrary",)),
    )(x, o_a, o_bg, z, z, target, w_a, w_b, w_out, fnw)


def _tn_matmul(a, b, name):
    m, n = a.shape[1], b.shape[1]
    tn = 512

    def body(a_ref, b_ref, o_ref, ob_ref):
        acc = _dot_tn(a_ref[...], b_ref[...])
        o_ref[...] = acc
        ob_ref[...] = acc.astype(BF16)

    out_blk = pl.BlockSpec((m, tn), lambda j: (0, j))
    return pl.pallas_call(
        body, name=name, grid=(n // tn,),
        out_shape=[jax.ShapeDtypeStruct((m, n), F32), jax.ShapeDtypeStruct((m, n), BF16)],
        in_specs=[pl.BlockSpec((S, m), lambda j: (0, 0)), pl.BlockSpec((S, tn), lambda j: (0, j))],
        out_specs=[out_blk, out_blk],
        compiler_params=_cp(("parallel",)),
    )(a, b)


def _hgrn_bwd(z, o, do_a, states, lbv, hnw, partials):
    ntb, nch = S // HBLK, HBLK // CHUNK
    n = len(GATHER_IDS)

    def body(hq_ref, hf_ref, hi_ref, hg_ref, o_ref, doa_ref, st_ref, lb_ref, hnw_ref, p0, p1, p2,
             dhq_ref, dhf_ref, dhi_ref, dhg_ref, glb_ref, ghn_ref, e0, e1, e2, dstate, send_sems, recv_sems):
        start, end = _exchange_chips_steps((p0, p1, p2), (e0, e1, e2), send_sems, recv_sems)

        @pl.when(pl.program_id(0) == 0)
        def _():
            dstate[...] = jnp.zeros_like(dstate)
            glb_ref[...] = jnp.zeros_like(glb_ref)
            ghn_ref[...] = jnp.zeros_like(ghn_ref)
            start()

        lb_a = lb_ref[...]
        hq_a, hg_a = hq_ref[...], hg_ref[...]
        q_a, k_a, g_a, v_a, sg_a, f_a, b_a = _hgrn_cols(hq_a, hf_ref[...], hi_ref[...], lb_a)
        bex_a = b_a - g_a
        eb_a = jnp.exp(b_a)
        w = hnw_ref[...]
        mask = _tril_mask(CHUNK)
        upper = _block_tri(CHUNK, CHUNK, upper=True)
        for h in range(HEADS):
            cols = slice(128 * h, 128 * h + 128)
            q, k, v, b, bex, eb = q_a[:, cols], k_a[:, cols], v_a[:, cols], b_a[:, cols], bex_a[:, cols], eb_a[:, cols]
            hq, hg, sg, f, lb = hq_a[:, cols], hg_a[:, cols], sg_a[:, cols], f_a[:, cols], lb_a[:, cols]
            ov, doa = o_ref[:, cols], doa_ref[:, cols]
            r = lax.rsqrt(jnp.mean(ov * ov, axis=-1, keepdims=True) + EPS)
            n = ov * r
            sil = _silu(hg)
            dhg_ref[:, cols] = (doa * n * w * _dsilu(hg)).astype(BF16)
            ghn_ref[h] += jnp.sum(doa * sil * n, axis=0, keepdims=True)
            dn = doa * sil * w
            do = r * (dn - n * jnp.mean(dn * n, axis=-1, keepdims=True))

            dst = dstate[h]
            dq_l, dk_l, dv_l, dg_l = [None] * nch, [None] * nch, [None] * nch, [None] * nch
            for c in reversed(range(nch)):
                r0 = c * CHUNK
                rows = slice(r0, r0 + CHUNK)
                st = st_ref[h, c]
                bc, kc, qc = b[rows], k[rows], q[rows]
                vb, dob = v[rows].astype(BF16), do[rows].astype(BF16)
                b_last = bc[CHUNK - 1:CHUNK]
                e_last = jnp.exp(b_last)
                ekl = jnp.exp(b_last - bc)
                dstb = dst.astype(BF16)
                a, qs_l, ek_l, eq_l = _chunk_scores(q, k, b, bex, r0, mask)
                da = jnp.where(mask, _dot_nt(dob, vb), 0.0)
                dv_l[c] = _dot_tn(a.astype(BF16), dob) + _dot_nt((kc * ekl).astype(BF16), dstb)
                dq_inter = _dot(dob, st.astype(BF16)) * eb[rows]
                dk_state = _dot(vb, dstb) * ekl
                dq_parts, dk_intra = [], jnp.zeros((CHUNK, 128), F32)
                for i in range(CHUNK // SUB):
                    da_i = da[SUB * i:SUB * (i + 1)]
                    dq_parts.append(_dot3(_dot, da_i, kc * ek_l[i]) * eq_l[i])
                    dk_intra = dk_intra + _dot3(_dot_tn, da_i, qs_l[i]) * ek_l[i]
                dq = jnp.concatenate(dq_parts, axis=0) + dq_inter
                dk = dk_intra + dk_state
                last = (e_last * jnp.sum(st * dst, axis=0, keepdims=True)
                        + jnp.sum(kc * dk_state, axis=0, keepdims=True))
                dg_l[c] = _dot_ones(upper, qc * dq - kc * dk) + last
                dq_l[c], dk_l[c] = dq, dk
                dst = dst * e_last + _dot_tn(dob, (qc * eb[rows]).astype(BF16))
            dstate[h] = dst
            dq, dk = jnp.concatenate(dq_l, axis=0), jnp.concatenate(dk_l, axis=0)
            dg, dv = jnp.concatenate(dg_l, axis=0), jnp.concatenate(dv_l, axis=0)
            dhq_ref[:, cols] = (dq * _dsilu(hq)).astype(BF16)
            dhi_ref[:, cols] = dv.astype(BF16)
            df = dg / f - dk
            dhf_ref[:, cols] = (df * (1.0 - lb) * sg * (1.0 - sg)).astype(BF16)
            glb_ref[:, cols] += jnp.sum(df * (1.0 - sg), axis=0, keepdims=True)

        pl.when(pl.program_id(0) == ntb - 1)(end)

    def rev(t):
        return ntb - 1 - t

    def zcol(j):
        return pl.BlockSpec((HBLK, D), lambda t: (rev(t), j))

    blk = pl.BlockSpec((HBLK, D), lambda t: (rev(t), 0))
    any_spec = pl.BlockSpec(memory_space=pl.ANY)
    return pl.pallas_call(
        body, name="hgrn_bwd", grid=(ntb,),
        out_shape=[jax.ShapeDtypeStruct((S, D), BF16)] * 4
        + [jax.ShapeDtypeStruct((1, D), F32), jax.ShapeDtypeStruct((HEADS, 1, 128), F32)]
        + [jax.ShapeDtypeStruct((3,) + SHARD_SHAPES[a], BF16) for a in GATHER_IDS],
        in_specs=[zcol(0), zcol(1), zcol(2), zcol(3), blk, blk,
                  pl.BlockSpec((HEADS, nch, 128, 128), lambda t: (0, rev(t), 0, 0)),
                  pl.BlockSpec((1, D), lambda t: (0, 0)), pl.BlockSpec((1, 128), lambda t: (0, 0))] + [any_spec] * n,
        out_specs=[blk] * 4 + [pl.BlockSpec((1, D), lambda t: (0, 0)),
                               pl.BlockSpec((HEADS, 1, 128), lambda t: (0, 0, 0))] + [any_spec] * n,
        scratch_shapes=[pltpu.VMEM((HEADS, 128, 128), F32), pltpu.SemaphoreType.DMA((3 * n,)),
                        pltpu.SemaphoreType.DMA((3 * n,))],
        compiler_params=_cp(("arbitrary",)),
    )(z, z, z, z, o, do_a, states, lbv, hnw, *partials)


def _attn_bwd(z, cc, ss, ob, lse, do_bg):
    def body(q_ref, k_ref, v_ref, ag_ref, cc_ref, ss_ref, ob_ref, lse_ref, dobg_ref,
             dq_ref, dk_ref, dv_ref, dag_ref,
             tmp, qs, ks, vs, dos, dqs, dks, dvs, dkp, dvp, do_t, ls0_t, ls1_t, dl0_t, dl1_t, ls0, ls1, dl0, dl1):
        g = pl.program_id(1)
        first_half = _half_mask()
        prev_ok, cur_ok = _attn_masks()
        lane = lax.broadcasted_iota(jnp.int32, (1, LANES), 1)
        heads = (lane < 64, lane >= 64)
        nblk = _group_blocks(g)
        cc_v, ss_v = cc_ref[...], ss_ref[...]

        @pl.when(g == 0)
        def _():
            ag, obv, dobg = ag_ref[...], ob_ref[...], dobg_ref[...]
            dag_ref[...] = (dobg * obv * _dsilu(ag)).astype(BF16)
            dob = dobg * _silu(ag)
            do_t[...] = dob
            prod = dob * obv
            dl = jnp.concatenate(
                [jnp.broadcast_to(jnp.sum(prod[:, 0:64], axis=-1, keepdims=True), (S, 64)),
                 jnp.broadcast_to(jnp.sum(prod[:, 64:128], axis=-1, keepdims=True), (S, 64))], axis=1)
            dl_sw = pltpu.roll(dl, 64, 1)
            dl0_t[...] = jnp.where(heads[0], dl, dl_sw)
            dl1_t[...] = jnp.where(heads[0], dl_sw, dl)
            ls = lse_ref[...]
            ls_sw = pltpu.roll(ls, 64, 1)
            ls0_t[...] = jnp.where(heads[0], ls, ls_sw)
            ls1_t[...] = jnp.where(heads[0], ls_sw, ls)
            ks[0:ATT_PAD, :] = jnp.zeros((ATT_PAD, LANES), BF16)
            vs[0:ATT_PAD, :] = jnp.zeros((ATT_PAD, LANES), BF16)

        tmp[...] = _rope(q_ref[...], cc_v, ss_v, first_half) * ATT_SCALE
        _to_residues_dyn(g, qs, tmp)
        tmp[...] = _rope(k_ref[...], cc_v, ss_v, first_half)
        _to_residues_dyn(g, ks, tmp, ATT_PAD, BF16)
        _to_residues_dyn(g, vs, v_ref, ATT_PAD, BF16)
        _to_residues_dyn(g, dos, do_t)
        _to_residues_dyn(g, ls0, ls0_t)
        _to_residues_dyn(g, ls1, ls1_t)
        _to_residues_dyn(g, dl0, dl0_t)
        _to_residues_dyn(g, dl1, dl1_t)
        lss, dls = (ls0, ls1), (dl0, dl1)

        def unit(u, carry):
            start = pl.multiple_of(u * 128, 128)
            cur = pl.ds(start, 128)
            both = pl.ds(start, 256)
            pm = prev_ok & ((u & (nblk - 1)) != 0)
            qu, dou = qs[cur, :], dos[cur, :]
            kcat, vcat = ks[both, :], vs[both, :]
            dq_u = None
            q_l, do_l, ds_l, p_l = [], [], [], []
            for hh in range(2):
                q_h = jnp.where(heads[hh], qu, 0.0).astype(BF16)
                do_h = jnp.where(heads[hh], dou, 0.0).astype(BF16)
                s = _dot_nt(q_h, kcat)
                dp = _dot_nt(do_h, vcat)
                lse_h, dl_h = lss[hh][cur, :], dls[hh][cur, :]
                pp = jnp.where(pm, jnp.exp(s[:, 0:128] - lse_h), 0.0)
                pc = jnp.where(cur_ok, jnp.exp(s[:, 128:256] - lse_h), 0.0)
                ds = jnp.concatenate([pp * (dp[:, 0:128] - dl_h), pc * (dp[:, 128:256] - dl_h)], axis=1).astype(BF16)
                dq = _dot(ds, kcat)
                dq_u = dq if hh == 0 else jnp.where(heads[1], dq, dq_u)
                q_l.append(q_h)
                do_l.append(do_h)
                ds_l.append(ds)
                p_l.append(jnp.concatenate([pp, pc], axis=1).astype(BF16))
            dkcat = _dot_tn(jnp.concatenate(ds_l, axis=0), jnp.concatenate(q_l, axis=0))
            dvcat = _dot_tn(jnp.concatenate(p_l, axis=0), jnp.concatenate(do_l, axis=0))
            dkp[cur, :] = dkcat[0:128]
            dks[cur, :] = dkcat[128:256]
            dvp[cur, :] = dvcat[0:128]
            dvs[cur, :] = dvcat[128:256]
            dqs[cur, :] = dq_u
            return carry

        lax.fori_loop(0, 16, unit, 0, unroll=ATT_UNROLL)
        dks[0:S - 128, :] += dkp[128:S, :]
        dvs[0:S - 128, :] += dvp[128:S, :]
        _from_residues_dyn(g, tmp, dqs)
        dq_ref[0] = (_rope(tmp[...], cc_v, -ss_v, first_half) * ATT_SCALE).astype(BF16)
        _from_residues_dyn(g, tmp, dks)
        dk_ref[0] = _rope(tmp[...], cc_v, -ss_v, first_half).astype(BF16)
        _from_residues_dyn(g, tmp, dvs)
        dv_ref[0] = tmp[...].astype(BF16)

    grp = pl.BlockSpec((1, S, LANES), lambda p, g: (g, 0, p))
    buf = pltpu.VMEM((S, LANES), F32)
    padded_b = pltpu.VMEM((ATT_PAD + S, LANES), BF16)
    return pl.pallas_call(
        body, name="attn_bwd", grid=(4, 3),
        out_shape=[jax.ShapeDtypeStruct((3, S, 512), BF16)] * 3 + [jax.ShapeDtypeStruct((S, 512), BF16)],
        in_specs=_attn_in_specs(3), out_specs=[grp, grp, grp, pl.BlockSpec((S, LANES), lambda p, g: (0, p))],
        scratch_shapes=[buf, buf, padded_b, padded_b] + [buf] * 15,
        compiler_params=_cp(("parallel", "arbitrary")),
    )(z, z, z, z, cc, ss, ob, lse, do_bg)


def _in_proj_bwd(dz, h, w_in):
    half = S // 2
    slab = (D, SHARD_COLS)

    def body(dz_hbm, h_hbm, w_hbm, dh_hbm, g_chip, r1_hbm, relay_hbm, r2_hbm,
             h_buf, dz_buf, stage_d, r1_buf, stage_i, acc,
             dz_sem, w_sem, h_sem, r1_sem, out_sem, send_d, recv_d, send_i, recv_i):
        x, y, c = _mesh_pos()
        sibling = (x, y, 1 - c)
        north = c == 1
        near = (jnp.where(north, 1 - x, x), jnp.where(north, y, 1 - y))
        far = (jnp.where(north, x, 1 - x), jnp.where(north, 1 - y, y))
        chips = [(1 - x, 1 - y), near, far, (x, y)]

        def cols(d):
            return pl.ds(pl.multiple_of(d * SHARD_COLS, LANES), SHARD_COLS)

        blocks = []
        for q_sib, q in zip([chips[0], far, near, chips[3]], chips):
            blocks += [4 * q_sib[0] + 2 * q_sib[1] + (1 - c), 4 * q[0] + 2 * q[1] + c]

        def dz_tile(t):
            return pltpu.make_async_copy(dz_hbm.at[pl.ds((t % 2) * half, half), cols(blocks[t // 2])],
                                         dz_buf.at[t % 2], dz_sem.at[t % 2])

        def to_sibling(i):
            return pltpu.make_async_remote_copy(
                src_ref=stage_d.at[i % 2], dst_ref=r1_hbm.at[i], send_sem=send_d.at[i], recv_sem=recv_d.at[i],
                device_id=sibling, device_id_type=MESH)

        def to_owner(i):
            dst = relay_hbm if i == 0 else r2_hbm.at[i - 1]
            return pltpu.make_async_remote_copy(
                src_ref=stage_i.at[i], dst_ref=dst, send_sem=send_i.at[i], recv_sem=recv_i.at[i],
                device_id=(*(far if i == 2 else near), c), device_id_type=MESH)

        h_copy = pltpu.make_async_copy(h_hbm, h_buf, h_sem)
        h_copy.start()
        dz_tile(0).start()
        h_copy.wait()
        for b in range(8):
            i = b // 2
            g = None
            for r in range(2):
                t = 2 * b + r
                if t + 1 < 16:
                    dz_tile(t + 1).start()
                dz_tile(t).wait()
                part = _dot_tn(h_buf[r * half:(r + 1) * half, :], dz_buf[t % 2])
                g = part if g is None else g + part
                if b % 2 == 1 and r == 0:
                    to_sibling(i).wait_recv()
                    r1_copy = pltpu.make_async_copy(r1_hbm.at[i], r1_buf, r1_sem)
                    r1_copy.start()
            if b % 2 == 0:
                if i >= 2:
                    to_sibling(i - 2).wait_send()
                stage_d[i % 2] = g.astype(BF16)
                to_sibling(i).start()
            else:
                r1_copy.wait()
                g = g + r1_buf[...].astype(F32)
                if i == 2:
                    to_owner(0).wait_recv()
                    relay_copy = pltpu.make_async_copy(relay_hbm, r1_buf, r1_sem)
                    relay_copy.start()
                    relay_copy.wait()
                    g = g + r1_buf[...].astype(F32)
                if i < 3:
                    stage_i[i] = g.astype(BF16)
                    to_owner(i).start()
                else:
                    g_chip[...] = g
        to_sibling(2).wait_send()
        to_sibling(3).wait_send()

        def dz2(t):
            return pltpu.make_async_copy(
                dz_hbm.at[pl.ds((t % 2) * half, half), pl.ds((t // 2) * SHARD_COLS, SHARD_COLS)],
                dz_buf.at[t % 2], dz_sem.at[t % 2])

        def w2(b):
            return pltpu.make_async_copy(w_hbm.at[:, pl.ds(b * SHARD_COLS, SHARD_COLS)],
                                         stage_d.at[b % 2], w_sem.at[b % 2])

        dz2(0).start()
        w2(0).start()
        for t in range(16):
            b, r = t // 2, t % 2
            if t + 1 < 16:
                dz2(t + 1).start()
            if r == 0:
                if b + 1 < 8:
                    w2(b + 1).start()
                w2(b).wait()
            dz2(t).wait()
            part = _dot_nt(dz_buf[t % 2], stage_d[b % 2])
            if b == 0:
                acc[r] = part
            else:
                acc[r] += part
        dh_out = [pltpu.make_async_copy(acc.at[r], dh_hbm.at[pl.ds(r * half, half), :], out_sem.at[r])
                  for r in range(2)]
        for cp in dh_out:
            cp.start()
        for cp in dh_out:
            cp.wait()
        for i in range(3):
            to_owner(i).wait_send()
        for i in (1, 2):
            to_owner(i).wait_recv()

    any_spec = pl.BlockSpec(memory_space=pl.ANY)
    return pl.pallas_call(
        body, name="in_proj_bwd",
        out_shape=[jax.ShapeDtypeStruct((S, D), F32), jax.ShapeDtypeStruct(slab, F32),
                   jax.ShapeDtypeStruct((4,) + slab, BF16), jax.ShapeDtypeStruct(slab, BF16),
                   jax.ShapeDtypeStruct((2,) + slab, BF16)],
        in_specs=[any_spec] * 3,
        out_specs=[any_spec, pl.BlockSpec(memory_space=pltpu.VMEM), any_spec, any_spec, any_spec],
        scratch_shapes=[pltpu.VMEM((S, D), BF16), pltpu.VMEM((2, half, SHARD_COLS), BF16),
                        pltpu.VMEM((2,) + slab, BF16), pltpu.VMEM(slab, BF16), pltpu.VMEM((3,) + slab, BF16),
                        pltpu.VMEM((2, half, D), F32),
                        pltpu.SemaphoreType.DMA((2,)), pltpu.SemaphoreType.DMA((2,)), pltpu.SemaphoreType.DMA,
                        pltpu.SemaphoreType.DMA, pltpu.SemaphoreType.DMA((2,)),
                        pltpu.SemaphoreType.DMA((4,)), pltpu.SemaphoreType.DMA((4,)),
                        pltpu.SemaphoreType.DMA((3,)), pltpu.SemaphoreType.DMA((3,))],
        compiler_params=_cp(),
    )(dz, h, w_in)


def _grad_x(x, norm_w, dh, dx2):
    tr = 256

    def body(x_ref, w_ref, dh_ref, dx2_ref, gx_ref, gnw_ref):
        @pl.when(pl.program_id(0) == 0)
        def _():
            gnw_ref[...] = jnp.zeros_like(gnw_ref)

        xv, dhv = x_ref[...], dh_ref[...]
        r = lax.rsqrt(jnp.mean(xv * xv, axis=-1, keepdims=True) + EPS)
        n = xv * r
        gnw_ref[...] += jnp.sum(dhv * n, axis=0, keepdims=True)
        dn = dhv * w_ref[...]
        gx_ref[...] = dx2_ref[...] + r * (dn - n * jnp.mean(dn * n, axis=-1, keepdims=True))

    row = pl.BlockSpec((tr, D), lambda i: (i, 0))
    vec = pl.BlockSpec((1, D), lambda i: (0, 0))
    return pl.pallas_call(
        body, name="grad_x", grid=(S // tr,),
        out_shape=[jax.ShapeDtypeStruct((S, D), F32), jax.ShapeDtypeStruct((1, D), F32)],
        in_specs=[row, vec, row, row], out_specs=[row, vec],
        compiler_params=_cp(("arbitrary",)),
    )(x, norm_w, dh, dx2)


def _rope_tables(positions):
    inv_freq = 10000.0 ** (-jnp.arange(0, 64, 2, dtype=F32) / 64)
    ang = positions.astype(F32)[:, None] * inv_freq[None, :]
    cos, sin = jnp.cos(ang), jnp.sin(ang)
    return jnp.tile(cos, (1, 4)), jnp.tile(jnp.concatenate([-sin, sin], axis=1), (1, 2))


def _local_step(x, positions, norm_w, lb_logits, hnw, fnw, target, w_in_shard, small_shards, core):
    cc, ss = _rope_tables(positions)
    lbv = jax.nn.sigmoid(lb_logits[0:1] - lb_logits[1:2])
    h = _rmsnorm_in(x, norm_w)
    z, w_in = _in_proj_gather(h, w_in_shard)
    o, o_a, states, w_a, w_b, w_out = _hgrn_fwd(z, lbv, hnw, small_shards)
    ob, lse, o_bg = _attn_fwd(z, cc, ss)
    dx2, dx2b, dgp, do_a, do_bg, merged, dy_a, dy_b, tail_small = _tail(x, o_a, o_bg, z, target, w_a, w_b, w_out, fnw)
    g_out, gb_out = _tn_matmul(merged, dx2b, "grad_w_out")
    g_a, gb_a = _tn_matmul(o_a, dy_a, "grad_w_a")
    g_b, gb_b = _tn_matmul(o_bg, dy_b, "grad_w_b")
    grads, gb = (g_a, g_b, g_out), (gb_a, gb_b, gb_out)
    r1 = _exchange_sibling(GATHER_IDS, gb)
    pb = [_chip_partials(a, grads[i], r1[i], core) for i, a in enumerate(GATHER_IDS)]
    dhq, dhf, dhi, dhg, glb, ghn, *r2 = _hgrn_bwd(z, o, do_a, states, lbv, hnw, pb)
    dq, dk, dv, dag = _attn_bwd(z, cc, ss, ob, lse, do_bg)
    dz = jnp.concatenate([dhq, dhf, dhi, dhg, dq[0], dq[1], dq[2], dk[0], dk[1], dk[2], dv[0], dv[1], dv[2], dag, dgp],
                         axis=1)
    dh, g_chip_in, _, _, r2_in = _in_proj_bwd(dz, h, w_in)
    grad_x, gnw = _grad_x(x, norm_w, dh, dx2)
    ghn_row = jnp.pad(jnp.sum(ghn, axis=0), ((0, 0), (0, D - 128)))
    small = jnp.concatenate([gnw, glb, ghn_row, tail_small[0:2], jnp.zeros((3, D), F32)], axis=0)
    return grad_x, (g_chip_in, r2_in), grads, r1, r2, small


def kernel(x, positions, norm_w, w_in, lb_logits, hgrn_norm_w, w_branch_a, w_branch_b, w_out, final_norm_w, loss_target, m_norm_w, m_w_in, m_lb_logits, m_hgrn_norm_w, m_w_branch_a, m_w_branch_b, m_w_out, m_final_norm_w, v_norm_w, v_w_in, v_lb_logits, v_hgrn_norm_w, v_w_branch_a, v_w_branch_b, v_w_out, v_final_norm_w):
    ix, iy, ic = _mesh_pos()
    core = jnp.reshape(ic, (1,)).astype(jnp.int32)
    pos = jnp.stack([4 * ix + 2 * iy + ic, 2 * ix + iy]).astype(jnp.int32)

    shards = [w_in[0], w_branch_a[0], w_branch_b[0], w_out[0]]
    moments_m = [m_w_in[0], m_w_branch_a[0], m_w_branch_b[0], m_w_out[0]]
    moments_v = [v_w_in[0], v_w_branch_a[0], v_w_branch_b[0], v_w_out[0]]
    names = ("w_in", "w_a", "w_b", "w_out")
    ids = GATHER_IDS
    shards_b = [_cast_bf16(w, f"cast_{nm}") for w, nm in zip(shards, names)]

    fnw2 = final_norm_w.reshape(1, D)
    grad_x, (g_chip_in, r2_in), grads, r1, r2, small = _local_step(
        x[0], positions[0], norm_w, lb_logits, hgrn_norm_w, fnw2, loss_target[0], shards_b[0], shards_b[1:], core)

    gathered = _gather_small(small)
    big =[_reduce_own_and_update(shards[0], moments_m[0], moments_v[0], g_chip_in, r2_in)]
    big += [_reduce_and_update(a, shards[a], moments_m[a], moments_v[a], grads[i], r1[i], r2[i], pos)
            for i, a in enumerate(ids)]
    sm = _small_update(gathered, norm_w, lb_logits, hgrn_norm_w, fnw2,
                       (m_norm_w, m_lb_logits, m_hgrn_norm_w, m_final_norm_w.reshape(1, D),
                        v_norm_w, v_lb_logits, v_hgrn_norm_w, v_final_norm_w.reshape(1, D)))
    loss = sm[0][0, 0]
    outs = [loss, grad_x[None]]
    for kind in range(4):
        s_nw, s_lb, s_hn, s_fn = sm[1 + 4 * kind:5 + 4 * kind]
        outs += [s_nw, big[0][kind][None], s_lb, s_hn, big[1][kind][None], big[2][kind][None],
                 big[3][kind][None], s_fn.reshape(D)]
    return tuple(outs)
```

```python
import functools

import jax
import jax.numpy as jnp
from jax import lax
from jax.experimental import pallas as pl
from jax.experimental.pallas import tpu as pltpu

F32 = jnp.float32
BF16 = jnp.bfloat16
MESH = pl.DeviceIdType.MESH

S = 2048
D = 1024
NDEV = 8
HEADS = 8
CHUNK = 64
SUB = 16
HBLK = 128
ATT_PAD = 128
ATT_UNROLL = 4
COPY_PARTS = 4
EXP_CLAMP = 80.0
EPS = 1e-6
IN_COLS = 11264
SHARD_COLS = IN_COLS // NDEV
ATT_DILS = (1, 4, 16)
ATT_SCALE = 64 ** -0.5
LANES = 128

ADAM_LR, ADAM_B1, ADAM_B2, ADAM_EPS, ADAM_WD, ADAM_STEP = 0.001, 0.9, 0.999, 1e-08, 0.01, 10

VMEM_LIMIT = 56 * 1024 * 1024


def _cp(sem=None, **kw):
    return pltpu.CompilerParams(dimension_semantics=sem, vmem_limit_bytes=VMEM_LIMIT, **kw)


def _dot(a, b):
    return jnp.dot(a, b, preferred_element_type=F32)


def _dot_nt(a, b):
    return lax.dot_general(a, b, (((1,), (1,)), ((), ())), preferred_element_type=F32)


def _dot_tn(a, b):
    return lax.dot_general(a, b, (((0,), (0,)), ((), ())), preferred_element_type=F32)


def _split2(x):
    hi = x.astype(BF16)
    lo = (x - hi.astype(F32)).astype(BF16)
    return hi, lo


def _split3(x):
    hi = x.astype(BF16)
    r = x - hi.astype(F32)
    mid = r.astype(BF16)
    lo = (r - mid.astype(F32)).astype(BF16)
    return hi, mid, lo


def _dot_ones(ones_bf16, x):
    hi, mid, lo = _split3(x)
    return _dot(ones_bf16, hi) + _dot(ones_bf16, mid) + _dot(ones_bf16, lo)


def _dot3(dotfn, a, b):
    ah, al = _split2(a)
    bh, bl = _split2(b)
    return dotfn(ah, bh) + dotfn(ah, bl) + dotfn(al, bh)


def _silu(x):
    return x * jax.nn.sigmoid(x)


def _dsilu(x):
    s = jax.nn.sigmoid(x)
    return s * (1.0 + x * (1.0 - s))


def _mesh_pos():
    return lax.axis_index("x"), lax.axis_index("y"), lax.axis_index("c")


class _SplitCopy:
    def __init__(self, src, dst, sem):
        self.src, self.dst, self.sem = src, dst, sem

    def start(self):
        rows = self.src.shape[0] // COPY_PARTS
        for p in range(COPY_PARTS):
            chunk = pl.ds(p * rows, rows)
            pltpu.make_async_copy(self.src.at[chunk], self.dst.at[chunk], self.sem).start()

    def wait(self):
        pltpu.make_async_copy(self.src, self.dst, self.sem).wait()


def _shard_of(ref, a, d):
    if a == 0:
        return ref.at[:, pl.ds(pl.multiple_of(d * SHARD_COLS, LANES), SHARD_COLS)]
    if a == 2:
        return ref.at[:, pl.ds(pl.multiple_of(d * LANES, LANES), LANES)]
    return ref.at[pl.ds(pl.multiple_of(d * 128, 128), 128), :]


FULL_SHAPES = ((D, IN_COLS), (D, D), (512, D), (D, D))
SHARD_SHAPES = ((D, SHARD_COLS), (128, D), (512, 128), (128, D))


def _allgather_steps(ids, ins, outs, send_sems, recv_sems, local_sems):
    n = len(ids)
    x, y, c = _mesh_pos()
    me, sibling = (x, y, c), (x, y, 1 - c)
    chips = [(1 - x, y), (x, 1 - y), (1 - x, 1 - y)]

    def blk(a, p):
        return _shard_of(outs[a], ids[a], 4 * p[0] + 2 * p[1] + p[2])

    def copy(a, k, block, to, src=None):
        return pltpu.make_async_remote_copy(
            src_ref=blk(a, block) if src is None else src, dst_ref=blk(a, block),
            send_sem=send_sems.at[a * 7 + k], recv_sem=recv_sems.at[a * 7 + k],
            device_id=to, device_id_type=MESH)

    mine = [pltpu.make_async_copy(ins[a], blk(a, me), local_sems.at[a]) for a in range(n)]
    first = []
    for a in range(n):
        first += [copy(a, 1 + j, me, (*chip, c), src=ins[a]) for j, chip in enumerate(chips)]
    for a in range(n):
        first.append(copy(a, 0, me, sibling, src=ins[a]))
    passed = [copy(a, 4 + j, (*chip, c), sibling) for j, chip in enumerate(chips) for a in range(n)]

    def start():
        for cp in mine + first:
            cp.start()

    def middle():
        for j, chip in enumerate(chips):
            for a in range(n):
                copy(a, 1 + j, (*chip, c), me).wait_recv()
                passed[j * n + a].start()

    def end():
        for a in range(n):
            copy(a, 0, sibling, me).wait_recv()
        for j, chip in enumerate(chips):
            for a in range(n):
                copy(a, 4 + j, (*chip, 1 - c), me).wait_recv()
        for cp in first + passed:
            cp.wait_send()
        for cp in mine:
            cp.wait()

    return start, middle, end


def _in_proj_gather(h, w_shard):
    half = S // 2
    slab = (D, SHARD_COLS)

    def body(h_hbm, w_hbm, z_hbm, wfull_hbm, h_buf, land, zstage,
             h_sem, own_sem, z_sem, wout_sem, send_sems, recv_sems):
        x, y, c = _mesh_pos()
        sibling = (x, y, 1 - c)
        north = c == 1

        def chips_of(first_x):
            near = (jnp.where(first_x, 1 - x, x), jnp.where(first_x, y, 1 - y))
            far = (jnp.where(first_x, x, 1 - x), jnp.where(first_x, 1 - y, y))
            return [near, far, (1 - x, 1 - y)]

        mine, theirs = chips_of(north), chips_of(jnp.logical_not(north))

        def dev(chip, core):
            return 4 * chip[0] + 2 * chip[1] + core

        block_of = ([dev((x, y), c), dev((x, y), 1 - c)] + [dev(q, c) for q in mine]
                    + [dev(q, 1 - c) for q in theirs])

        def cols(d):
            if isinstance(d, int):
                return pl.ds(d * SHARD_COLS, SHARD_COLS)
            return pl.ds(pl.multiple_of(d * SHARD_COLS, LANES), SHARD_COLS)

        def send(k, src, dst_slot, to):
            return pltpu.make_async_remote_copy(
                src_ref=src, dst_ref=land.at[dst_slot], send_sem=send_sems.at[k], recv_sem=recv_sems.at[k],
                device_id=to, device_id_type=MESH)

        def to_sibling():
            return send(0, w_hbm, 1, sibling)

        def to_chip(j):
            if j == 2:
                return send(3, land.at[2], 4, (*mine[1], c))
            return send(1 + j, w_hbm, 2 + j, (*mine[j], c))

        def pass_on(j):
            return send(4 + j, land.at[2 + j], 5 + j, sibling)

        own = _SplitCopy(w_hbm, land.at[0], own_sem)
        h_copy = _SplitCopy(h_hbm, h_buf, h_sem)
        own.start()
        h_copy.start()
        to_sibling().start()
        to_chip(0).start()
        h_copy.wait()
        own.wait()

        def multiply(slot, n_done):
            d = block_of[slot]
            out = _SplitCopy(land.at[slot], wfull_hbm.at[:, cols(d)], wout_sem.at[slot])
            out.start()
            for r in range(2):
                rows = pl.ds(r * half, half)
                zc = _SplitCopy(zstage.at[r], z_hbm.at[rows, cols(d)], z_sem.at[r])
                if n_done > 0:
                    zc.wait()
                zstage[r] = _dot(h_buf[r * half:(r + 1) * half, :], land[slot])
                zc.start()
            return out

        outs = [multiply(0, 0)]
        to_sibling().wait_recv()
        outs.append(multiply(1, 1))
        done = 2
        for j in range(3):
            to_chip(j).wait_recv()
            pass_on(j).start()
            to_chip(j).wait_send()
            if j < 2:
                to_chip(j + 1).start()
            outs.append(multiply(2 + j, done))
            pass_on(j).wait_recv()
            outs.append(multiply(5 + j, done + 1))
            done += 2
        for r in range(2):
            _SplitCopy(zstage.at[r], z_hbm.at[pl.ds(r * half, half), cols(0)], z_sem.at[r]).wait()
        for out in outs:
            out.wait()
        to_sibling().wait_send()
        for j in range(3):
            pass_on(j).wait_send()

    any_spec = pl.BlockSpec(memory_space=pl.ANY)
    return pl.pallas_call(
        body, name="in_proj_gather",
        out_shape=[jax.ShapeDtypeStruct((S, IN_COLS), F32), jax.ShapeDtypeStruct((D, IN_COLS), BF16)],
        in_specs=[any_spec] * 2, out_specs=[any_spec] * 2,
        scratch_shapes=[pltpu.VMEM((S, D), BF16), pltpu.VMEM((8,) + slab, BF16), pltpu.VMEM((2, half, SHARD_COLS), F32),
                        pltpu.SemaphoreType.DMA, pltpu.SemaphoreType.DMA, pltpu.SemaphoreType.DMA((2,)),
                        pltpu.SemaphoreType.DMA((8,)), pltpu.SemaphoreType.DMA((7,)), pltpu.SemaphoreType.DMA((7,))],
        compiler_params=_cp(),
    )(h, w_shard)


def _exchange_sibling(ids, gb):
    n = len(gb)

    def body(*refs):
        ins, outs = refs[:n], refs[n:2 * n]
        send_sems, recv_sems = refs[2 * n:]
        x, y, c = _mesh_pos()
        sibling = (x, y, 1 - c)
        copies = []
        for i, a in enumerate(ids):
            for q in range(4):
                copies.append(pltpu.make_async_remote_copy(
                    src_ref=_shard_of(ins[i], a, 2 * q + (1 - c)), dst_ref=outs[i].at[q],
                    send_sem=send_sems.at[i * 4 + q], recv_sem=recv_sems.at[i * 4 + q],
                    device_id=sibling, device_id_type=MESH))
        for cp in copies:
            cp.start()
        for cp in copies:
            cp.wait()

    any_spec = pl.BlockSpec(memory_space=pl.ANY)
    return pl.pallas_call(
        body, name="grads_to_sibling",
        out_shape=[jax.ShapeDtypeStruct((4,) + SHARD_SHAPES[a], BF16) for a in ids],
        in_specs=[any_spec] * n, out_specs=[any_spec] * n,
        scratch_shapes=[pltpu.SemaphoreType.DMA((4 * n,)), pltpu.SemaphoreType.DMA((4 * n,))],
    )(*gb)


def _exchange_chips_steps(ins, outs, send_sems, recv_sems):
    x, y, c = _mesh_pos()
    chips = [(1 - x, y), (x, 1 - y), (1 - x, 1 - y)]
    copies = []
    for a in range(len(ins)):
        for k, chip in enumerate(chips):
            copies.append(pltpu.make_async_remote_copy(
                src_ref=ins[a].at[2 * chip[0] + chip[1]], dst_ref=outs[a].at[k],
                send_sem=send_sems.at[a * 3 + k], recv_sem=recv_sems.at[a * 3 + k],
                device_id=(*chip, c), device_id_type=MESH))

    def start():
        for cp in copies:
            cp.start()

    def end():
        for cp in copies:
            cp.wait()

    return start, end


def _gather_small(small):
    def body(small_ref, small_out, ssend, srecv, local_sem):
        x, y, c = _mesh_pos()
        me = 4 * x + 2 * y + c
        copies = []
        for r in range(1, NDEV):
            peer = (1 - x if r & 4 else x, 1 - y if r & 2 else y, 1 - c if r & 1 else c)
            copies.append(pltpu.make_async_remote_copy(
                src_ref=small_ref, dst_ref=small_out.at[me],
                send_sem=ssend.at[r - 1], recv_sem=srecv.at[r - 1],
                device_id=peer, device_id_type=MESH))
        own = pltpu.make_async_copy(small_ref, small_out.at[me], local_sem)
        own.start()
        for cp in copies:
            cp.start()
        for cp in copies:
            cp.wait()
        own.wait()

    any_spec = pl.BlockSpec(memory_space=pl.ANY)
    return pl.pallas_call(
        body, name="gather_small",
        out_shape=jax.ShapeDtypeStruct((NDEV,) + small.shape, F32),
        in_specs=[any_spec], out_specs=any_spec,
        scratch_shapes=[pltpu.SemaphoreType.DMA((NDEV - 1,)), pltpu.SemaphoreType.DMA((NDEV - 1,)),
                        pltpu.SemaphoreType.DMA],
    )(small)


def _shard_tiles(a):
    rows, cols = SHARD_SHAPES[a]
    tr = min(rows, 256)
    return (tr, cols), rows // tr


def _full_index(a, d, i):
    (tr, _), nt = _shard_tiles(a)
    if a in (0, 2):
        return (i, d)
    return (d * nt + i, 0)


def _cast_bf16(x, name):
    rows, cols = x.shape
    tr = min(rows, 256)

    def body(x_ref, o_ref):
        o_ref[...] = x_ref[...].astype(BF16)

    return pl.pallas_call(
        body, name=name, out_shape=jax.ShapeDtypeStruct(x.shape, BF16), grid=(rows // tr,),
        in_specs=[pl.BlockSpec((tr, cols), lambda i: (i, 0))],
        out_specs=pl.BlockSpec((tr, cols), lambda i: (i, 0)),
        compiler_params=_cp(("parallel",)),
    )(x)


def _chip_partials(a, g_full, r1, core):
    tile, nt = _shard_tiles(a)

    def body(c_ref, g_ref, r_ref, o_ref):
        o_ref[0] = (g_ref[...] + r_ref[0].astype(F32)).astype(BF16)

    grid_spec = pltpu.PrefetchScalarGridSpec(
        num_scalar_prefetch=1, grid=(4, nt),
        in_specs=[pl.BlockSpec(tile, lambda q, i, c: _full_index(a, 2 * q + c[0], i)),
                  pl.BlockSpec((1,) + tile, lambda q, i, c: (q, i, 0))],
        out_specs=pl.BlockSpec((1,) + tile, lambda q, i, c: (q, i, 0)))
    return pl.pallas_call(
        body, name=f"chip_partials_{a}", grid_spec=grid_spec,
        out_shape=jax.ShapeDtypeStruct((4,) + SHARD_SHAPES[a], BF16),
        compiler_params=_cp(("parallel", "parallel")),
    )(core, g_full, r1)


def _adam(w, g, m, v):
    m = ADAM_B1 * m + (1.0 - ADAM_B1) * g
    v = ADAM_B2 * v + (1.0 - ADAM_B2) * (g * g)
    m_hat = m / (1.0 - ADAM_B1 ** ADAM_STEP)
    v_hat = v / (1.0 - ADAM_B2 ** ADAM_STEP)
    delta = -ADAM_LR * (m_hat / (jnp.sqrt(v_hat) + ADAM_EPS) + ADAM_WD * w)
    return delta, m, v


def _reduce_and_update(a, w, m, v, g_full, r1, r2, pos):
    tile, nt = _shard_tiles(a)

    def body(p_ref, w_ref, m_ref, v_ref, g_ref, r1_ref, r2_ref, go_ref, do_ref, mo_ref, vo_ref):
        g = g_ref[...] + r1_ref[0].astype(F32)
        g = g + r2_ref[0].astype(F32)
        g = g + r2_ref[1].astype(F32)
        g = g + r2_ref[2].astype(F32)
        delta, m_new, v_new = _adam(w_ref[...], g, m_ref[...], v_ref[...])
        go_ref[...] = g
        do_ref[...] = delta
        mo_ref[...] = m_new
        vo_ref[...] = v_new

    own = pl.BlockSpec(tile, lambda i, p: (i, 0))
    grid_spec = pltpu.PrefetchScalarGridSpec(
        num_scalar_prefetch=1, grid=(nt,),
        in_specs=[own, own, own,
                  pl.BlockSpec(tile, lambda i, p: _full_index(a, p[0], i)),
                  pl.BlockSpec((1,) + tile, lambda i, p: (p[1], i, 0)),
                  pl.BlockSpec((3,) + tile, lambda i, p: (0, i, 0))],
        out_specs=[own] * 4)
    shp = jax.ShapeDtypeStruct(w.shape, F32)
    return pl.pallas_call(
        body, name=f"reduce_update_{a}", grid_spec=grid_spec, out_shape=[shp] * 4,
        compiler_params=_cp(("parallel",)),
    )(pos, w, m, v, g_full, r1, r2)


def _reduce_own_and_update(w, m, v, g_chip, r2):
    tile, nt = _shard_tiles(0)

    def body(w_ref, m_ref, v_ref, g_ref, r2_ref, go_ref, do_ref, mo_ref, vo_ref):
        g = g_ref[...] + r2_ref[0].astype(F32)
        g = g + r2_ref[1].astype(F32)
        delta, m_new, v_new = _adam(w_ref[...], g, m_ref[...], v_ref[...])
        go_ref[...] = g
        do_ref[...] = delta
        mo_ref[...] = m_new
        vo_ref[...] = v_new

    own = pl.BlockSpec(tile, lambda i: (i, 0))
    shp = jax.ShapeDtypeStruct(w.shape, F32)
    return pl.pallas_call(
        body, name="reduce_update_0", grid=(nt,), out_shape=[shp] * 4,
        in_specs=[own, own, own, own, pl.BlockSpec((2,) + tile, lambda i: (0, i, 0))], out_specs=[own] * 4,
        compiler_params=_cp(("parallel",)),
    )(w, m, v, g_chip, r2)


def _small_update(gathered, norm_w, lb_logits, hnw, fnw, moments):
    m_nw, m_lb, m_hn, m_fn, v_nw, v_lb, v_hn, v_fn = moments

    def body(g_ref, nw, lb, hn, fn, mnw, mlb, mhn, mfn, vnw, vlb, vhn, vfn,
             loss_o, g_nw, g_lb, g_hn, g_fn, d_nw, d_lb, d_hn, d_fn,
             mo_nw, mo_lb, mo_hn, mo_fn, vo_nw, vo_lb, vo_hn, vo_fn):
        tot = g_ref[0]
        for d in range(1, NDEV):
            tot = tot + g_ref[d]
        loss_o[...] = tot[4:5, 0:LANES]
        logits = lb[...]
        lbv = jax.nn.sigmoid(logits[0:1] - logits[1:2])
        chain = tot[1:2] * lbv * (1.0 - lbv)
        grads = (tot[0:1], jnp.concatenate([chain, -chain], axis=0), tot[2:3, 0:LANES], tot[3:4])
        outs = ((nw, mnw, vnw, g_nw, d_nw, mo_nw, vo_nw), (lb, mlb, vlb, g_lb, d_lb, mo_lb, vo_lb),
                (hn, mhn, vhn, g_hn, d_hn, mo_hn, vo_hn), (fn, mfn, vfn, g_fn, d_fn, mo_fn, vo_fn))
        for g, (w_r, m_r, v_r, g_o, d_o, m_o, v_o) in zip(grads, outs):
            delta, m_new, v_new = _adam(w_r[...], g, m_r[...], v_r[...])
            g_o[...] = g
            d_o[...] = delta
            m_o[...] = m_new
            v_o[...] = v_new

    shapes = [norm_w.shape, lb_logits.shape, hnw.shape, fnw.shape]
    out_shape = [jax.ShapeDtypeStruct((1, LANES), F32)] + [jax.ShapeDtypeStruct(s, F32) for s in shapes] * 4
    return pl.pallas_call(body, name="small_update", out_shape=out_shape, compiler_params=_cp())(
        gathered, norm_w, lb_logits, hnw, fnw, m_nw, m_lb, m_hn, m_fn, v_nw, v_lb, v_hn, v_fn)


def _rmsnorm_in(x, norm_w):
    tr = 512

    def body(x_ref, w_ref, h_ref):
        xv = x_ref[...]
        r = lax.rsqrt(jnp.mean(xv * xv, axis=-1, keepdims=True) + EPS)
        h_ref[...] = (xv * r * w_ref[...]).astype(BF16)

    return pl.pallas_call(
        body, name="rmsnorm_in", out_shape=jax.ShapeDtypeStruct((S, D), BF16), grid=(S // tr,),
        in_specs=[pl.BlockSpec((tr, D), lambda i: (i, 0)), pl.BlockSpec((1, D), lambda i: (0, 0))],
        out_specs=pl.BlockSpec((tr, D), lambda i: (i, 0)),
        compiler_params=_cp(("parallel",)),
    )(x, norm_w)


def _in_proj(h, w_in):
    tn = 1024

    def body(h_ref, w_ref, z_ref):
        z_ref[...] = _dot(h_ref[...], w_ref[...])

    return pl.pallas_call(
        body, name="in_proj", out_shape=jax.ShapeDtypeStruct((S, IN_COLS), F32), grid=(IN_COLS // tn,),
        in_specs=[pl.BlockSpec((S, D), lambda j: (0, 0)), pl.BlockSpec((D, tn), lambda j: (0, j))],
        out_specs=pl.BlockSpec((S, tn), lambda j: (0, j)),
        compiler_params=_cp(("parallel",)),
    )(h, w_in)


def _block_tri(n, block, upper=False):
    r = lax.broadcasted_iota(jnp.int32, (n, n), 0)
    c = lax.broadcasted_iota(jnp.int32, (n, n), 1)
    keep = (c >= r) if upper else (c <= r)
    return jnp.where(keep & ((r // block) == (c // block)), 1.0, 0.0).astype(BF16)


def _tril_mask(n):
    r = lax.broadcasted_iota(jnp.int32, (n, n), 0)
    c = lax.broadcasted_iota(jnp.int32, (n, n), 1)
    return c <= r


def _chunk_scores(q, k, b, bex, r0, mask):
    rows = slice(r0, r0 + CHUNK)
    parts, qs_l, ek_l, eq_l = [], [], [], []
    for i in range(CHUNK // SUB):
        ri = slice(r0 + SUB * i, r0 + SUB * (i + 1))
        base = bex[r0 + SUB * i:r0 + SUB * i + 1]
        eq = jnp.exp(b[ri] - base)
        ek = jnp.exp(jnp.minimum(base - b[rows], EXP_CLAMP))
        qs = q[ri] * eq
        parts.append(_dot_nt(qs.astype(BF16), (k[rows] * ek).astype(BF16)))
        qs_l.append(qs)
        ek_l.append(ek)
        eq_l.append(eq)
    return jnp.where(mask, jnp.concatenate(parts, axis=0), 0.0), qs_l, ek_l, eq_l


def _hgrn_cols(hq, hf, hi, lb):
    sg = jax.nn.sigmoid(hf)
    f = lb + (1.0 - lb) * sg
    g = jnp.log(f)
    b = _dot_ones(_block_tri(HBLK, CHUNK), g)
    return _silu(hq), 1.0 - f, g, hi, sg, f, b


GATHER_IDS = (1, 2, 3)


def _hgrn_fwd(z, lbv, hnw, shards):
    ntb, nch = S // HBLK, HBLK // CHUNK
    n = len(GATHER_IDS)

    def body(hq_ref, hf_ref, hi_ref, hg_ref, lb_ref, hnw_ref, s0, s1, s2, o_ref, oa_ref, st_ref, f0, f1, f2,
             state, send_sems, recv_sems, local_sems):
        start, middle, end = _allgather_steps(GATHER_IDS, (s0, s1, s2), (f0, f1, f2), send_sems, recv_sems, local_sems)

        @pl.when(pl.program_id(0) == 0)
        def _():
            state[...] = jnp.zeros_like(state)
            start()

        pl.when(pl.program_id(0) == ntb // 2)(middle)

        q_a, k_a, g_a, v_a, _, _, b_a = _hgrn_cols(hq_ref[...], hf_ref[...], hi_ref[...], lb_ref[...])
        bex_a = b_a - g_a
        eb_a = jnp.exp(b_a)
        mask = _tril_mask(CHUNK)
        hg = hg_ref[...]
        w = hnw_ref[...]
        for h in range(HEADS):
            cols = slice(128 * h, 128 * h + 128)
            q, k, v, b, bex, eb = q_a[:, cols], k_a[:, cols], v_a[:, cols], b_a[:, cols], bex_a[:, cols], eb_a[:, cols]
            st = state[h]
            outs = []
            for c in range(nch):
                r0 = c * CHUNK
                rows = slice(r0, r0 + CHUNK)
                a, _, _, _ = _chunk_scores(q, k, b, bex, r0, mask)
                vb = v[rows].astype(BF16)
                b_last = b[r0 + CHUNK - 1:r0 + CHUNK]
                qe = (q[rows] * eb[rows]).astype(BF16)
                outs.append(_dot(a.astype(BF16), vb) + _dot_nt(qe, st.astype(BF16)))
                st_ref[h, c] = st
                ke = (k[rows] * jnp.exp(b_last - b[rows])).astype(BF16)
                st = st * jnp.exp(b_last) + _dot_tn(vb, ke)
            state[h] = st
            o = jnp.concatenate(outs, axis=0)
            o_ref[:, cols] = o
            r = lax.rsqrt(jnp.mean(o * o, axis=-1, keepdims=True) + EPS)
            oa_ref[:, cols] = (o * r * w * _silu(hg[:, cols])).astype(BF16)

        pl.when(pl.program_id(0) == ntb - 1)(end)

    def zcol(j):
        return pl.BlockSpec((HBLK, D), lambda t: (t, j))

    out_blk = pl.BlockSpec((HBLK, D), lambda t: (t, 0))
    any_spec = pl.BlockSpec(memory_space=pl.ANY)
    return pl.pallas_call(
        body, name="hgrn_fwd", grid=(ntb,),
        out_shape=[jax.ShapeDtypeStruct((S, D), F32), jax.ShapeDtypeStruct((S, D), BF16),
                   jax.ShapeDtypeStruct((HEADS, S // CHUNK, 128, 128), F32)]
        + [jax.ShapeDtypeStruct(FULL_SHAPES[a], BF16) for a in GATHER_IDS],
        in_specs=[zcol(0), zcol(1), zcol(2), zcol(3),
                  pl.BlockSpec((1, D), lambda t: (0, 0)), pl.BlockSpec((1, 128), lambda t: (0, 0))] + [any_spec] * n,
        out_specs=[out_blk, out_blk, pl.BlockSpec((HEADS, nch, 128, 128), lambda t: (0, t, 0, 0))] + [any_spec] * n,
        scratch_shapes=[pltpu.VMEM((HEADS, 128, 128), F32), pltpu.SemaphoreType.DMA((7 * n,)),
                        pltpu.SemaphoreType.DMA((7 * n,)), pltpu.SemaphoreType.DMA((n,))],
        compiler_params=_cp(("arbitrary",)),
    )(z, z, z, z, lbv, hnw, *shards)


def _half_mask():
    lane = lax.broadcasted_iota(jnp.int32, (1, LANES), 1)
    return (lane % 64) < 32


def _rope(t, cc, ss, first_half):
    partner = jnp.where(first_half, pltpu.roll(t, 96, 1), pltpu.roll(t, 32, 1))
    return t * cc + partner * ss


def _attn_masks():
    i = lax.broadcasted_iota(jnp.int32, (128, 128), 0)
    j = lax.broadcasted_iota(jnp.int32, (128, 128), 1)
    return j >= i, j <= i


def _to_residues_dyn(g, dst, src, row0=0, dtype=None):
    for gi, dil in enumerate((1, 4, 16)):
        m = S // dil

        @pl.when(g == gi)
        def _(dil=dil, m=m):
            for r in range(dil):
                v = src[...] if dil == 1 else src[pl.ds(r, m, stride=dil), :]
                if dtype is not None:
                    v = v.astype(dtype)
                dst[row0 + r * m:row0 + (r + 1) * m, 0:LANES] = v


def _from_residues_dyn(g, dst, src, row0=0):
    for gi, dil in enumerate((1, 4, 16)):
        m = S // dil

        @pl.when(g == gi)
        def _(dil=dil, m=m):
            for r in range(dil):
                v = src[row0 + r * m:row0 + (r + 1) * m, :]
                if dil == 1:
                    dst[...] = v
                else:
                    dst[pl.ds(r, m, stride=dil), :] = v


def _group_blocks(g):
    return jnp.where(g == 0, 16, jnp.where(g == 1, 4, 1))


def _attn_in_specs(extra):
    def zcol(off):
        return pl.BlockSpec((S, LANES), lambda p, g: (0, off + 4 * g + p))

    per_pair = pl.BlockSpec((S, LANES), lambda p, g: (0, p))
    const = pl.BlockSpec((S, LANES), lambda p, g: (0, 0))
    return [zcol(32), zcol(44), zcol(56), pl.BlockSpec((S, LANES), lambda p, g: (0, 68 + p)), const, const] + [per_pair] * extra


def _attn_fwd(z, cc, ss):
    def body(q_ref, k_ref, v_ref, ag_ref, cc_ref, ss_ref, ob_ref, lse_ref, obg_ref,
             tmp, qs, ks, vx, og, mg, lg, o_t, m_t, l_t, o_acc, m_acc, l_acc):
        g = pl.program_id(1)
        first_half = _half_mask()
        prev_ok, cur_ok = _attn_masks()
        lane = lax.broadcasted_iota(jnp.int32, (1, LANES), 1)
        heads = (lane < 64, lane >= 64)
        nblk = _group_blocks(g)

        @pl.when(g == 0)
        def _():
            ks[0:ATT_PAD, :] = jnp.zeros((ATT_PAD, LANES), BF16)
            vx[0:ATT_PAD, 0:LANES] = jnp.zeros((ATT_PAD, LANES), BF16)
            vx[:, LANES:2 * LANES] = jnp.ones((ATT_PAD + S, LANES), BF16)

        tmp[...] = _rope(q_ref[...], cc_ref[...], ss_ref[...], first_half) * ATT_SCALE
        _to_residues_dyn(g, qs, tmp)
        tmp[...] = _rope(k_ref[...], cc_ref[...], ss_ref[...], first_half)
        _to_residues_dyn(g, ks, tmp, ATT_PAD, BF16)
        _to_residues_dyn(g, vx, v_ref, ATT_PAD, BF16)

        def unit(u, carry):
            start = pl.multiple_of(u * 128, 128)
            cur = pl.ds(start, 128)
            pm = prev_ok & ((u & (nblk - 1)) != 0)
            qu = qs[cur, :]
            kcat = ks[pl.ds(start, 256), :]
            vext = vx[pl.ds(start, 256), :]
            o_u = m_u = l_u = None
            for hh in range(2):
                s = _dot_nt(jnp.where(heads[hh], qu, 0.0).astype(BF16), kcat)
                sp = jnp.where(pm, s[:, 0:128], -jnp.inf)
                sc = jnp.where(cur_ok, s[:, 128:256], -jnp.inf)
                m = jnp.max(jnp.maximum(sp, sc), axis=-1, keepdims=True)
                p = jnp.concatenate([jnp.exp(sp - m), jnp.exp(sc - m)], axis=1).astype(BF16)
                ol = _dot(p, vext)
                mb = jnp.broadcast_to(m, (128, LANES))
                if hh == 0:
                    o_u, l_u, m_u = ol[:, 0:128], ol[:, 128:256], mb
                else:
                    o_u = jnp.where(heads[1], ol[:, 0:128], o_u)
                    l_u = jnp.where(heads[1], ol[:, 128:256], l_u)
                    m_u = jnp.where(heads[1], mb, m_u)
            og[cur, :] = o_u
            mg[cur, :] = m_u
            lg[cur, :] = l_u
            return carry

        lax.fori_loop(0, 16, unit, 0, unroll=ATT_UNROLL)
        _from_residues_dyn(g, o_t, og)
        _from_residues_dyn(g, m_t, mg)
        _from_residues_dyn(g, l_t, lg)

        @pl.when(g == 0)
        def _():
            o_acc[...] = o_t[...]
            m_acc[...] = m_t[...]
            l_acc[...] = l_t[...]

        @pl.when(g > 0)
        def _():
            m_new = jnp.maximum(m_acc[...], m_t[...])
            wa, wb = jnp.exp(m_acc[...] - m_new), jnp.exp(m_t[...] - m_new)
            o_acc[...] = o_acc[...] * wa + o_t[...] * wb
            l_acc[...] = l_acc[...] * wa + l_t[...] * wb
            m_acc[...] = m_new

        @pl.when(g == 2)
        def _():
            ob = o_acc[...] / l_acc[...]
            ob_ref[...] = ob
            lse_ref[...] = m_acc[...] + jnp.log(l_acc[...])
            obg_ref[...] = (ob * _silu(ag_ref[...])).astype(BF16)

    blk = pl.BlockSpec((S, LANES), lambda p, g: (0, p))
    buf = pltpu.VMEM((S, LANES), F32)
    return pl.pallas_call(
        body, name="attn_fwd", grid=(4, 3),
        out_shape=[jax.ShapeDtypeStruct((S, 512), F32), jax.ShapeDtypeStruct((S, 512), F32),
                   jax.ShapeDtypeStruct((S, 512), BF16)],
        in_specs=_attn_in_specs(0), out_specs=[blk, blk, blk],
        scratch_shapes=[buf, buf, pltpu.VMEM((ATT_PAD + S, LANES), BF16), pltpu.VMEM((ATT_PAD + S, 2 * LANES), BF16)] + [buf] * 9,
        compiler_params=_cp(("parallel", "arbitrary")),
    )(z, z, z, z, cc, ss)


def _tail(x, o_a, o_bg, z, target, w_a, w_b, w_out, fnw):
    tm = 256

    def body(x_ref, oa_ref, ob_ref, gpa_ref, gpb_ref, t_ref, wa_ref, wb_ref, wo_ref, fnw_ref,
             dx2_ref, dx2b_ref, dgp_ref, doa_ref, dob_ref, mg_ref, dya_ref, dyb_ref, small_ref):
        @pl.when(pl.program_id(0) == 0)
        def _():
            small_ref[...] = jnp.zeros_like(small_ref)

        wa, wb, wo = wa_ref[...], wb_ref[...], wo_ref[...]
        y_a = _dot(oa_ref[...], wa)
        y_b = _dot(ob_ref[...], wb)
        ga = jax.nn.sigmoid(gpa_ref[...])
        gb = jax.nn.sigmoid(gpb_ref[...])
        merged = (ga * y_a + gb * y_b).astype(BF16)
        x2 = x_ref[...] + _dot(merged, wo)
        r2 = lax.rsqrt(jnp.mean(x2 * x2, axis=-1, keepdims=True) + EPS)
        n2 = x2 * r2
        fw = fnw_ref[...]
        err = n2 * fw - t_ref[...]
        loss = 0.5 * jnp.sum(jnp.sum(err * err, axis=-1, keepdims=True), axis=0, keepdims=True) / D
        dy = err * (1.0 / D)
        g_fnw = jnp.sum(dy * n2, axis=0, keepdims=True)
        dn = dy * fw
        dx2 = r2 * (dn - n2 * jnp.mean(dn * n2, axis=-1, keepdims=True))
        dx2b = dx2.astype(BF16)
        dmerged = _dot_nt(dx2b, wo)
        dy_a = (dmerged * ga).astype(BF16)
        dy_b = (dmerged * gb).astype(BF16)
        dx2_ref[...] = dx2
        dx2b_ref[...] = dx2b
        dgp_ref[:, 0:D] = (dmerged * y_a * ga * (1.0 - ga)).astype(BF16)
        dgp_ref[:, D:2 * D] = (dmerged * y_b * gb * (1.0 - gb)).astype(BF16)
        doa_ref[...] = _dot_nt(dy_a, wa)
        dob_ref[...] = _dot_nt(dy_b, wb)
        mg_ref[...] = merged
        dya_ref[...] = dy_a
        dyb_ref[...] = dy_b
        small_ref[0:1, :] += g_fnw
        small_ref[1:2, :] += jnp.broadcast_to(loss, (1, D))

    def rows(cols, off=0):
        return pl.BlockSpec((tm, cols), lambda i: (i, off))

    def whole(shape):
        return pl.BlockSpec(shape, lambda i: (0, 0))

    return pl.pallas_call(
        body, name="tail", grid=(S // tm,),
        out_shape=[jax.ShapeDtypeStruct((S, D), F32), jax.ShapeDtypeStruct((S, D), BF16),
                   jax.ShapeDtypeStruct((S, 2 * D), BF16), jax.ShapeDtypeStruct((S, D), F32),
                   jax.ShapeDtypeStruct((S, 512), F32), jax.ShapeDtypeStruct((S, D), BF16),
                   jax.ShapeDtypeStruct((S, D), BF16), jax.ShapeDtypeStruct((S, D), BF16),
                   jax.ShapeDtypeStruct((8, D), F32)],
        in_specs=[rows(D), rows(D), rows(512), rows(D, 9), rows(D, 10), rows(D),
                  whole((D, D)), whole((512, D)), whole((D, D)), whole((1, D))],
        out_specs=[rows(D), rows(D), rows(2 * D), rows(D), rows(512), rows(D), rows(D), rows(D), whole((8, D))],
        compiler_params=_cp(("arbitrary",)),
    )(x, o_a, o_bg, z, z, target, w_a, w_b, w_out, fnw)


def _tn_matmul(a, b, name):
    m, n = a.shape[1], b.shape[1]
    tn = 512

    def body(a_ref, b_ref, o_ref, ob_ref):
        acc = _dot_tn(a_ref[...], b_ref[...])
        o_ref[...] = acc
        ob_ref[...] = acc.astype(BF16)

    out_blk = pl.BlockSpec((m, tn), lambda j: (0, j))
    return pl.pallas_call(
        body, name=name, grid=(n // tn,),
        out_shape=[jax.ShapeDtypeStruct((m, n), F32), jax.ShapeDtypeStruct((m, n), BF16)],
        in_specs=[pl.BlockSpec((S, m), lambda j: (0, 0)), pl.BlockSpec((S, tn), lambda j: (0, j))],
        out_specs=[out_blk, out_blk],
        compiler_params=_cp(("parallel",)),
    )(a, b)


def _hgrn_bwd(z, o, do_a, states, lbv, hnw, partials):
    ntb, nch = S // HBLK, HBLK // CHUNK
    n = len(GATHER_IDS)

    def body(hq_ref, hf_ref, hi_ref, hg_ref, o_ref, doa_ref, st_ref, lb_ref, hnw_ref, p0, p1, p2,
             dhq_ref, dhf_ref, dhi_ref, dhg_ref, glb_ref, ghn_ref, e0, e1, e2, dstate, send_sems, recv_sems):
        start, end = _exchange_chips_steps((p0, p1, p2), (e0, e1, e2), send_sems, recv_sems)

        @pl.when(pl.program_id(0) == 0)
        def _():
            dstate[...] = jnp.zeros_like(dstate)
            glb_ref[...] = jnp.zeros_like(glb_ref)
            ghn_ref[...] = jnp.zeros_like(ghn_ref)
            start()

        lb_a = lb_ref[...]
        hq_a, hg_a = hq_ref[...], hg_ref[...]
        q_a, k_a, g_a, v_a, sg_a, f_a, b_a = _hgrn_cols(hq_a, hf_ref[...], hi_ref[...], lb_a)
        bex_a = b_a - g_a
        eb_a = jnp.exp(b_a)
        w = hnw_ref[...]
        mask = _tril_mask(CHUNK)
        upper = _block_tri(CHUNK, CHUNK, upper=True)
        for h in range(HEADS):
            cols = slice(128 * h, 128 * h + 128)
            q, k, v, b, bex, eb = q_a[:, cols], k_a[:, cols], v_a[:, cols], b_a[:, cols], bex_a[:, cols], eb_a[:, cols]
            hq, hg, sg, f, lb = hq_a[:, cols], hg_a[:, cols], sg_a[:, cols], f_a[:, cols], lb_a[:, cols]
            ov, doa = o_ref[:, cols], doa_ref[:, cols]
            r = lax.rsqrt(jnp.mean(ov * ov, axis=-1, keepdims=True) + EPS)
            n = ov * r
            sil = _silu(hg)
            dhg_ref[:, cols] = (doa * n * w * _dsilu(hg)).astype(BF16)
            ghn_ref[h] += jnp.sum(doa * sil * n, axis=0, keepdims=True)
            dn = doa * sil * w
            do = r * (dn - n * jnp.mean(dn * n, axis=-1, keepdims=True))

            dst = dstate[h]
            dq_l, dk_l, dv_l, dg_l = [None] * nch, [None] * nch, [None] * nch, [None] * nch
            for c in reversed(range(nch)):
                r0 = c * CHUNK
                rows = slice(r0, r0 + CHUNK)
                st = st_ref[h, c]
                bc, kc, qc = b[rows], k[rows], q[rows]
                vb, dob = v[rows].astype(BF16), do[rows].astype(BF16)
                b_last = bc[CHUNK - 1:CHUNK]
                e_last = jnp.exp(b_last)
                ekl = jnp.exp(b_last - bc)
                dstb = dst.astype(BF16)
                a, qs_l, ek_l, eq_l = _chunk_scores(q, k, b, bex, r0, mask)
                da = jnp.where(mask, _dot_nt(dob, vb), 0.0)
                dv_l[c] = _dot_tn(a.astype(BF16), dob) + _dot_nt((kc * ekl).astype(BF16), dstb)
                dq_inter = _dot(dob, st.astype(BF16)) * eb[rows]
                dk_state = _dot(vb, dstb) * ekl
                dq_parts, dk_intra = [], jnp.zeros((CHUNK, 128), F32)
                for i in range(CHUNK // SUB):
                    da_i = da[SUB * i:SUB * (i + 1)]
                    dq_parts.append(_dot3(_dot, da_i, kc * ek_l[i]) * eq_l[i])
                    dk_intra = dk_intra + _dot3(_dot_tn, da_i, qs_l[i]) * ek_l[i]
                dq = jnp.concatenate(dq_parts, axis=0) + dq_inter
                dk = dk_intra + dk_state
                last = (e_last * jnp.sum(st * dst, axis=0, keepdims=True)
                        + jnp.sum(kc * dk_state, axis=0, keepdims=True))
                dg_l[c] = _dot_ones(upper, qc * dq - kc * dk) + last
                dq_l[c], dk_l[c] = dq, dk
                dst = dst * e_last + _dot_tn(dob, (qc * eb[rows]).astype(BF16))
            dstate[h] = dst
            dq, dk = jnp.concatenate(dq_l, axis=0), jnp.concatenate(dk_l, axis=0)
            dg, dv = jnp.concatenate(dg_l, axis=0), jnp.concatenate(dv_l, axis=0)
            dhq_ref[:, cols] = (dq * _dsilu(hq)).astype(BF16)
            dhi_ref[:, cols] = dv.astype(BF16)
            df = dg / f - dk
            dhf_ref[:, cols] = (df * (1.0 - lb) * sg * (1.0 - sg)).astype(BF16)
            glb_ref[:, cols] += jnp.sum(df * (1.0 - sg), axis=0, keepdims=True)

        pl.when(pl.program_id(0) == ntb - 1)(end)

    def rev(t):
        return ntb - 1 - t

    def zcol(j):
        return pl.BlockSpec((HBLK, D), lambda t: (rev(t), j))

    blk = pl.BlockSpec((HBLK, D), lambda t: (rev(t), 0))
    any_spec = pl.BlockSpec(memory_space=pl.ANY)
    return pl.pallas_call(
        body, name="hgrn_bwd", grid=(ntb,),
        out_shape=[jax.ShapeDtypeStruct((S, D), BF16)] * 4
        + [jax.ShapeDtypeStruct((1, D), F32), jax.ShapeDtypeStruct((HEADS, 1, 128), F32)]
        + [jax.ShapeDtypeStruct((3,) + SHARD_SHAPES[a], BF16) for a in GATHER_IDS],
        in_specs=[zcol(0), zcol(1), zcol(2), zcol(3), blk, blk,
                  pl.BlockSpec((HEADS, nch, 128, 128), lambda t: (0, rev(t), 0, 0)),
                  pl.BlockSpec((1, D), lambda t: (0, 0)), pl.BlockSpec((1, 128), lambda t: (0, 0))] + [any_spec] * n,
        out_specs=[blk] * 4 + [pl.BlockSpec((1, D), lambda t: (0, 0)),
                               pl.BlockSpec((HEADS, 1, 128), lambda t: (0, 0, 0))] + [any_spec] * n,
        scratch_shapes=[pltpu.VMEM((HEADS, 128, 128), F32), pltpu.SemaphoreType.DMA((3 * n,)),
                        pltpu.SemaphoreType.DMA((3 * n,))],
        compiler_params=_cp(("arbitrary",)),
    )(z, z, z, z, o, do_a, states, lbv, hnw, *partials)


def _attn_bwd(z, cc, ss, ob, lse, do_bg):
    def body(q_ref, k_ref, v_ref, ag_ref, cc_ref, ss_ref, ob_ref, lse_ref, dobg_ref,
             dq_ref, dk_ref, dv_ref, dag_ref,
             tmp, qs, ks, vs, dos, dqs, dks, dvs, dkp, dvp, do_t, ls0_t, ls1_t, dl0_t, dl1_t, ls0, ls1, dl0, dl1):
        g = pl.program_id(1)
        first_half = _half_mask()
        prev_ok, cur_ok = _attn_masks()
        lane = lax.broadcasted_iota(jnp.int32, (1, LANES), 1)
        heads = (lane < 64, lane >= 64)
        nblk = _group_blocks(g)
        cc_v, ss_v = cc_ref[...], ss_ref[...]

        @pl.when(g == 0)
        def _():
            ag, obv, dobg = ag_ref[...], ob_ref[...], dobg_ref[...]
            dag_ref[...] = (dobg * obv * _dsilu(ag)).astype(BF16)
            dob = dobg * _silu(ag)
            do_t[...] = dob
            prod = dob * obv
            dl = jnp.concatenate(
                [jnp.broadcast_to(jnp.sum(prod[:, 0:64], axis=-1, keepdims=True), (S, 64)),
                 jnp.broadcast_to(jnp.sum(prod[:, 64:128], axis=-1, keepdims=True), (S, 64))], axis=1)
            dl_sw = pltpu.roll(dl, 64, 1)
            dl0_t[...] = jnp.where(heads[0], dl, dl_sw)
            dl1_t[...] = jnp.where(heads[0], dl_sw, dl)
            ls = lse_ref[...]
            ls_sw = pltpu.roll(ls, 64, 1)
            ls0_t[...] = jnp.where(heads[0], ls, ls_sw)
            ls1_t[...] = jnp.where(heads[0], ls_sw, ls)
            ks[0:ATT_PAD, :] = jnp.zeros((ATT_PAD, LANES), BF16)
            vs[0:ATT_PAD, :] = jnp.zeros((ATT_PAD, LANES), BF16)

        tmp[...] = _rope(q_ref[...], cc_v, ss_v, first_half) * ATT_SCALE
        _to_residues_dyn(g, qs, tmp)
        tmp[...] = _rope(k_ref[...], cc_v, ss_v, first_half)
        _to_residues_dyn(g, ks, tmp, ATT_PAD, BF16)
        _to_residues_dyn(g, vs, v_ref, ATT_PAD, BF16)
        _to_residues_dyn(g, dos, do_t)
        _to_residues_dyn(g, ls0, ls0_t)
        _to_residues_dyn(g, ls1, ls1_t)
        _to_residues_dyn(g, dl0, dl0_t)
        _to_residues_dyn(g, dl1, dl1_t)
        lss, dls = (ls0, ls1), (dl0, dl1)

        def unit(u, carry):
            start = pl.multiple_of(u * 128, 128)
            cur = pl.ds(start, 128)
            both = pl.ds(start, 256)
            pm = prev_ok & ((u & (nblk - 1)) != 0)
            qu, dou = qs[cur, :], dos[cur, :]
            kcat, vcat = ks[both, :], vs[both, :]
            dq_u = None
            q_l, do_l, ds_l, p_l = [], [], [], []
            for hh in range(2):
                q_h = jnp.where(heads[hh], qu, 0.0).astype(BF16)
                do_h = jnp.where(heads[hh], dou, 0.0).astype(BF16)
                s = _dot_nt(q_h, kcat)
                dp = _dot_nt(do_h, vcat)
                lse_h, dl_h = lss[hh][cur, :], dls[hh][cur, :]
                pp = jnp.where(pm, jnp.exp(s[:, 0:128] - lse_h), 0.0)
                pc = jnp.where(cur_ok, jnp.exp(s[:, 128:256] - lse_h), 0.0)
                ds = jnp.concatenate([pp * (dp[:, 0:128] - dl_h), pc * (dp[:, 128:256] - dl_h)], axis=1).astype(BF16)
                dq = _dot(ds, kcat)
                dq_u = dq if hh == 0 else jnp.where(heads[1], dq, dq_u)
                q_l.append(q_h)
                do_l.append(do_h)
                ds_l.append(ds)
                p_l.append(jnp.concatenate([pp, pc], axis=1).astype(BF16))
            dkcat = _dot_tn(jnp.concatenate(ds_l, axis=0), jnp.concatenate(q_l, axis=0))
            dvcat = _dot_tn(jnp.concatenate(p_l, axis=0), jnp.concatenate(do_l, axis=0))
            dkp[cur, :] = dkcat[0:128]
            dks[cur, :] = dkcat[128:256]
            dvp[cur, :] = dvcat[0:128]
            dvs[cur, :] = dvcat[128:256]
            dqs[cur, :] = dq_u
            return carry

        lax.fori_loop(0, 16, unit, 0, unroll=ATT_UNROLL)
        dks[0:S - 128, :] += dkp[128:S, :]
        dvs[0:S - 128, :] += dvp[128:S, :]
        _from_residues_dyn(g, tmp, dqs)
        dq_ref[0] = (_rope(tmp[...], cc_v, -ss_v, first_half) * ATT_SCALE).astype(BF16)
        _from_residues_dyn(g, tmp, dks)
        dk_ref[0] = _rope(tmp[...], cc_v, -ss_v, first_half).astype(BF16)
        _from_residues_dyn(g, tmp, dvs)
        dv_ref[0] = tmp[...].astype(BF16)

    grp = pl.BlockSpec((1, S, LANES), lambda p, g: (g, 0, p))
    buf = pltpu.VMEM((S, LANES), F32)
    padded_b = pltpu.VMEM((ATT_PAD + S, LANES), BF16)
    return pl.pallas_call(
        body, name="attn_bwd", grid=(4, 3),
        out_shape=[jax.ShapeDtypeStruct((3, S, 512), BF16)] * 3 + [jax.ShapeDtypeStruct((S, 512), BF16)],
        in_specs=_attn_in_specs(3), out_specs=[grp, grp, grp, pl.BlockSpec((S, LANES), lambda p, g: (0, p))],
        scratch_shapes=[buf, buf, padded_b, padded_b] + [buf] * 15,
        compiler_params=_cp(("parallel", "arbitrary")),
    )(z, z, z, z, cc, ss, ob, lse, do_bg)


def _in_proj_bwd(dz, h, w_in):
    half = S // 2
    slab = (D, SHARD_COLS)

    def body(dz_hbm, h_hbm, w_hbm, dh_hbm, g_chip, r1_hbm, relay_hbm, r2_hbm,
             h_buf, dz_buf, stage_d, r1_buf, stage_i, acc,
             dz_sem, w_sem, h_sem, r1_sem, out_sem, send_d, recv_d, send_i, recv_i):
        x, y, c = _mesh_pos()
        sibling = (x, y, 1 - c)
        north = c == 1
        near = (jnp.where(north, 1 - x, x), jnp.where(north, y, 1 - y))
        far = (jnp.where(north, x, 1 - x), jnp.where(north, 1 - y, y))
        chips = [(1 - x, 1 - y), near, far, (x, y)]

        def cols(d):
            return pl.ds(pl.multiple_of(d * SHARD_COLS, LANES), SHARD_COLS)

        blocks = []
        for q_sib, q in zip([chips[0], far, near, chips[3]], chips):
            blocks += [4 * q_sib[0] + 2 * q_sib[1] + (1 - c), 4 * q[0] + 2 * q[1] + c]

        def dz_tile(t):
            return _SplitCopy(dz_hbm.at[pl.ds((t % 2) * half, half), cols(blocks[t // 2])],
                                         dz_buf.at[t % 2], dz_sem.at[t % 2])

        def to_sibling(i):
            return pltpu.make_async_remote_copy(
                src_ref=stage_d.at[i % 2], dst_ref=r1_hbm.at[i], send_sem=send_d.at[i], recv_sem=recv_d.at[i],
                device_id=sibling, device_id_type=MESH)

        def to_owner(i):
            dst = relay_hbm if i == 0 else r2_hbm.at[i - 1]
            return pltpu.make_async_remote_copy(
                src_ref=stage_i.at[i], dst_ref=dst, send_sem=send_i.at[i], recv_sem=recv_i.at[i],
                device_id=(*(far if i == 2 else near), c), device_id_type=MESH)

        h_copy = _SplitCopy(h_hbm, h_buf, h_sem)
        h_copy.start()
        dz_tile(0).start()
        h_copy.wait()
        for b in range(8):
            i = b // 2
            g = None
            for r in range(2):
                t = 2 * b + r
                if t + 1 < 16:
                    dz_tile(t + 1).start()
                dz_tile(t).wait()
                part = _dot_tn(h_buf[r * half:(r + 1) * half, :], dz_buf[t % 2])
                g = part if g is None else g + part
                if b % 2 == 1 and r == 0:
                    to_sibling(i).wait_recv()
                    r1_copy = _SplitCopy(r1_hbm.at[i], r1_buf, r1_sem)
                    r1_copy.start()
            if b % 2 == 0:
                if i >= 2:
                    to_sibling(i - 2).wait_send()
                stage_d[i % 2] = g.astype(BF16)
                to_sibling(i).start()
            else:
                r1_copy.wait()
                g = g + r1_buf[...].astype(F32)
                if i == 2:
                    to_owner(0).wait_recv()
                    relay_copy = _SplitCopy(relay_hbm, r1_buf, r1_sem)
                    relay_copy.start()
                    relay_copy.wait()
                    g = g + r1_buf[...].astype(F32)
                if i < 3:
                    stage_i[i] = g.astype(BF16)
                    to_owner(i).start()
                else:
                    g_chip[...] = g
        to_sibling(2).wait_send()
        to_sibling(3).wait_send()

        def dz2(t):
            return _SplitCopy(
                dz_hbm.at[pl.ds((t % 2) * half, half), pl.ds((t // 2) * SHARD_COLS, SHARD_COLS)],
                dz_buf.at[t % 2], dz_sem.at[t % 2])

        def w2(b):
            return _SplitCopy(w_hbm.at[:, pl.ds(b * SHARD_COLS, SHARD_COLS)],
                                         stage_d.at[b % 2], w_sem.at[b % 2])

        dz2(0).start()
        w2(0).start()
        for t in range(16):
            b, r = t // 2, t % 2
            if t + 1 < 16:
                dz2(t + 1).start()
            if r == 0:
                if b + 1 < 8:
                    w2(b + 1).start()
                w2(b).wait()
            dz2(t).wait()
            part = _dot_nt(dz_buf[t % 2], stage_d[b % 2])
            if b == 0:
                acc[r] = part
            else:
                acc[r] += part
        dh_out = [_SplitCopy(acc.at[r], dh_hbm.at[pl.ds(r * half, half), :], out_sem.at[r])
                  for r in range(2)]
        for cp in dh_out:
            cp.start()
        for cp in dh_out:
            cp.wait()
        for i in range(3):
            to_owner(i).wait_send()
        for i in (1, 2):
            to_owner(i).wait_recv()

    any_spec = pl.BlockSpec(memory_space=pl.ANY)
    return pl.pallas_call(
        body, name="in_proj_bwd",
        out_shape=[jax.ShapeDtypeStruct((S, D), F32), jax.ShapeDtypeStruct(slab, F32),
                   jax.ShapeDtypeStruct((4,) + slab, BF16), jax.ShapeDtypeStruct(slab, BF16),
                   jax.ShapeDtypeStruct((2,) + slab, BF16)],
        in_specs=[any_spec] * 3,
        out_specs=[any_spec, pl.BlockSpec(memory_space=pltpu.VMEM), any_spec, any_spec, any_spec],
        scratch_shapes=[pltpu.VMEM((S, D), BF16), pltpu.VMEM((2, half, SHARD_COLS), BF16),
                        pltpu.VMEM((2,) + slab, BF16), pltpu.VMEM(slab, BF16), pltpu.VMEM((3,) + slab, BF16),
                        pltpu.VMEM((2, half, D), F32),
                        pltpu.SemaphoreType.DMA((2,)), pltpu.SemaphoreType.DMA((2,)), pltpu.SemaphoreType.DMA,
                        pltpu.SemaphoreType.DMA, pltpu.SemaphoreType.DMA((2,)),
                        pltpu.SemaphoreType.DMA((4,)), pltpu.SemaphoreType.DMA((4,)),
                        pltpu.SemaphoreType.DMA((3,)), pltpu.SemaphoreType.DMA((3,))],
        compiler_params=_cp(),
    )(dz, h, w_in)


def _grad_x(x, norm_w, dh, dx2):
    tr = 256

    def body(x_ref, w_ref, dh_ref, dx2_ref, gx_ref, gnw_ref):
        @pl.when(pl.program_id(0) == 0)
        def _():
            gnw_ref[...] = jnp.zeros_like(gnw_ref)

        xv, dhv = x_ref[...], dh_ref[...]
        r = lax.rsqrt(jnp.mean(xv * xv, axis=-1, keepdims=True) + EPS)
        n = xv * r
        gnw_ref[...] += jnp.sum(dhv * n, axis=0, keepdims=True)
        dn = dhv * w_ref[...]
        gx_ref[...] = dx2_ref[...] + r * (dn - n * jnp.mean(dn * n, axis=-1, keepdims=True))

    row = pl.BlockSpec((tr, D), lambda i: (i, 0))
    vec = pl.BlockSpec((1, D), lambda i: (0, 0))
    return pl.pallas_call(
        body, name="grad_x", grid=(S // tr,),
        out_shape=[jax.ShapeDtypeStruct((S, D), F32), jax.ShapeDtypeStruct((1, D), F32)],
        in_specs=[row, vec, row, row], out_specs=[row, vec],
        compiler_params=_cp(("arbitrary",)),
    )(x, norm_w, dh, dx2)


def _rope_tables(positions):
    inv_freq = 10000.0 ** (-jnp.arange(0, 64, 2, dtype=F32) / 64)
    ang = positions.astype(F32)[:, None] * inv_freq[None, :]
    cos, sin = jnp.cos(ang), jnp.sin(ang)
    return jnp.tile(cos, (1, 4)), jnp.tile(jnp.concatenate([-sin, sin], axis=1), (1, 2))


def _local_step(x, positions, norm_w, lb_logits, hnw, fnw, target, w_in_shard, small_shards, core):
    cc, ss = _rope_tables(positions)
    lbv = jax.nn.sigmoid(lb_logits[0:1] - lb_logits[1:2])
    h = _rmsnorm_in(x, norm_w)
    z, w_in = _in_proj_gather(h, w_in_shard)
    o, o_a, states, w_a, w_b, w_out = _hgrn_fwd(z, lbv, hnw, small_shards)
    ob, lse, o_bg = _attn_fwd(z, cc, ss)
    dx2, dx2b, dgp, do_a, do_bg, merged, dy_a, dy_b, tail_small = _tail(x, o_a, o_bg, z, target, w_a, w_b, w_out, fnw)
    g_out, gb_out = _tn_matmul(merged, dx2b, "grad_w_out")
    g_a, gb_a = _tn_matmul(o_a, dy_a, "grad_w_a")
    g_b, gb_b = _tn_matmul(o_bg, dy_b, "grad_w_b")
    grads, gb = (g_a, g_b, g_out), (gb_a, gb_b, gb_out)
    r1 = _exchange_sibling(GATHER_IDS, gb)
    pb = [_chip_partials(a, grads[i], r1[i], core) for i, a in enumerate(GATHER_IDS)]
    dhq, dhf, dhi, dhg, glb, ghn, *r2 = _hgrn_bwd(z, o, do_a, states, lbv, hnw, pb)
    dq, dk, dv, dag = _attn_bwd(z, cc, ss, ob, lse, do_bg)
    dz = jnp.concatenate([dhq, dhf, dhi, dhg, dq[0], dq[1], dq[2], dk[0], dk[1], dk[2], dv[0], dv[1], dv[2], dag, dgp],
                         axis=1)
    dh, g_chip_in, _, _, r2_in = _in_proj_bwd(dz, h, w_in)
    grad_x, gnw = _grad_x(x, norm_w, dh, dx2)
    ghn_row = jnp.pad(jnp.sum(ghn, axis=0), ((0, 0), (0, D - 128)))
    small = jnp.concatenate([gnw, glb, ghn_row, tail_small[0:2], jnp.zeros((3, D), F32)], axis=0)
    return grad_x, (g_chip_in, r2_in), grads, r1, r2, small


def kernel(x, positions, norm_w, w_in, lb_logits, hgrn_norm_w, w_branch_a, w_branch_b, w_out, final_norm_w, loss_target, m_norm_w, m_w_in, m_lb_logits, m_hgrn_norm_w, m_w_branch_a, m_w_branch_b, m_w_out, m_final_norm_w, v_norm_w, v_w_in, v_lb_logits, v_hgrn_norm_w, v_w_branch_a, v_w_branch_b, v_w_out, v_final_norm_w):
    ix, iy, ic = _mesh_pos()
    core = jnp.reshape(ic, (1,)).astype(jnp.int32)
    pos = jnp.stack([4 * ix + 2 * iy + ic, 2 * ix + iy]).astype(jnp.int32)

    shards = [w_in[0], w_branch_a[0], w_branch_b[0], w_out[0]]
    moments_m = [m_w_in[0], m_w_branch_a[0], m_w_branch_b[0], m_w_out[0]]
    moments_v = [v_w_in[0], v_w_branch_a[0], v_w_branch_b[0], v_w_out[0]]
    names = ("w_in", "w_a", "w_b", "w_out")
    ids = GATHER_IDS
    shards_b = [_cast_bf16(w, f"cast_{nm}") for w, nm in zip(shards, names)]

    fnw2 = final_norm_w.reshape(1, D)
    grad_x, (g_chip_in, r2_in), grads, r1, r2, small = _local_step(
        x[0], positions[0], norm_w, lb_logits, hgrn_norm_w, fnw2, loss_target[0], shards_b[0], shards_b[1:], core)

    gathered = _gather_small(small)
    big =[_reduce_own_and_update(shards[0], moments_m[0], moments_v[0], g_chip_in, r2_in)]
    big += [_reduce_and_update(a, shards[a], moments_m[a], moments_v[a], grads[i], r1[i], r2[i], pos)
            for i, a in enumerate(ids)]
    sm = _small_update(gathered, norm_w, lb_logits, hgrn_norm_w, fnw2,
                       (m_norm_w, m_lb_logits, m_hgrn_norm_w, m_final_norm_w.reshape(1, D),
                        v_norm_w, v_lb_logits, v_hgrn_norm_w, v_final_norm_w.reshape(1, D)))
    loss = sm[0][0, 0]
    outs = [loss, grad_x[None]]
    for kind in range(4):
        s_nw, s_lb, s_hn, s_fn = sm[1 + 4 * kind:5 + 4 * kind]
        outs += [s_nw, big[0][kind][None], s_lb, s_hn, big[1][kind][None], big[2][kind][None],
                 big[3][kind][None], s_fn.reshape(D)]
    return tuple(outs)
```

```python
import functools

import jax
import jax.numpy as jnp
from jax import lax
from jax.experimental import pallas as pl
from jax.experimental.pallas import tpu as pltpu

F32 = jnp.float32
BF16 = jnp.bfloat16
MESH = pl.DeviceIdType.MESH

S = 2048
D = 1024
NDEV = 8
HEADS = 8
CHUNK = 64
SUB = 16
HBLK = 128
ATT_PAD = 128
ATT_UNROLL = 4
COPY_PARTS = 4
EXP_CLAMP = 80.0
EPS = 1e-6
IN_COLS = 11264
SHARD_COLS = IN_COLS // NDEV
ATT_DILS = (1, 4, 16)
ATT_SCALE = 64 ** -0.5
LANES = 128

ADAM_LR, ADAM_B1, ADAM_B2, ADAM_EPS, ADAM_WD, ADAM_STEP = 0.001, 0.9, 0.999, 1e-08, 0.01, 10

VMEM_LIMIT = 56 * 1024 * 1024


def _cp(sem=None, **kw):
    return pltpu.CompilerParams(dimension_semantics=sem, vmem_limit_bytes=VMEM_LIMIT, **kw)


def _dot(a, b):
    return jnp.dot(a, b, preferred_element_type=F32)


def _dot_nt(a, b):
    return lax.dot_general(a, b, (((1,), (1,)), ((), ())), preferred_element_type=F32)


def _dot_tn(a, b):
    return lax.dot_general(a, b, (((0,), (0,)), ((), ())), preferred_element_type=F32)


def _split2(x):
    hi = x.astype(BF16)
    lo = (x - hi.astype(F32)).astype(BF16)
    return hi, lo


def _split3(x):
    hi = x.astype(BF16)
    r = x - hi.astype(F32)
    mid = r.astype(BF16)
    lo = (r - mid.astype(F32)).astype(BF16)
    return hi, mid, lo


def _dot_ones(ones_bf16, x):
    hi, mid, lo = _split3(x)
    return _dot(ones_bf16, hi) + _dot(ones_bf16, mid) + _dot(ones_bf16, lo)


def _dot3(dotfn, a, b):
    ah, al = _split2(a)
    bh, bl = _split2(b)
    return dotfn(ah, bh) + dotfn(ah, bl) + dotfn(al, bh)


def _silu(x):
    return x * jax.nn.sigmoid(x)


def _dsilu(x):
    s = jax.nn.sigmoid(x)
    return s * (1.0 + x * (1.0 - s))


def _mesh_pos():
    return lax.axis_index("x"), lax.axis_index("y"), lax.axis_index("c")


class _SplitCopy:
    def __init__(self, src, dst, sem):
        self.src, self.dst, self.sem = src, dst, sem

    def start(self):
        rows = self.src.shape[0] // COPY_PARTS
        for p in range(COPY_PARTS):
            chunk = pl.ds(p * rows, rows)
            pltpu.make_async_copy(self.src.at[chunk], self.dst.at[chunk], self.sem).start()

    def wait(self):
        pltpu.make_async_copy(self.src, self.dst, self.sem).wait()


def _shard_of(ref, a, d):
    if a == 0:
        return ref.at[:, pl.ds(pl.multiple_of(d * SHARD_COLS, LANES), SHARD_COLS)]
    if a == 2:
        return ref.at[:, pl.ds(pl.multiple_of(d * LANES, LANES), LANES)]
    return ref.at[pl.ds(pl.multiple_of(d * 128, 128), 128), :]


FULL_SHAPES = ((D, IN_COLS), (D, D), (512, D), (D, D))
SHARD_SHAPES = ((D, SHARD_COLS), (128, D), (512, 128), (128, D))


def _allgather_steps(ids, ins, outs, send_sems, recv_sems, local_sems):
    n = len(ids)
    x, y, c = _mesh_pos()
    me, sibling = (x, y, c), (x, y, 1 - c)
    chips = [(1 - x, y), (x, 1 - y), (1 - x, 1 - y)]

    def blk(a, p):
        return _shard_of(outs[a], ids[a], 4 * p[0] + 2 * p[1] + p[2])

    def copy(a, k, block, to, src=None):
        return pltpu.make_async_remote_copy(
            src_ref=blk(a, block) if src is None else src, dst_ref=blk(a, block),
            send_sem=send_sems.at[a * 7 + k], recv_sem=recv_sems.at[a * 7 + k],
            device_id=to, device_id_type=MESH)

    mine = [pltpu.make_async_copy(ins[a], blk(a, me), local_sems.at[a]) for a in range(n)]
    first = []
    for a in range(n):
        first += [copy(a, 1 + j, me, (*chip, c), src=ins[a]) for j, chip in enumerate(chips)]
    for a in range(n):
        first.append(copy(a, 0, me, sibling, src=ins[a]))
    passed = [copy(a, 4 + j, (*chip, c), sibling) for j, chip in enumerate(chips) for a in range(n)]

    def start():
        for cp in mine + first:
            cp.start()

    def middle():
        for j, chip in enumerate(chips):
            for a in range(n):
                copy(a, 1 + j, (*chip, c), me).wait_recv()
                passed[j * n + a].start()

    def end():
        for a in range(n):
            copy(a, 0, sibling, me).wait_recv()
        for j, chip in enumerate(chips):
            for a in range(n):
                copy(a, 4 + j, (*chip, 1 - c), me).wait_recv()
        for cp in first + passed:
            cp.wait_send()
        for cp in mine:
            cp.wait()

    return start, middle, end


def _in_proj_gather(h, w_shard):
    half = S // 2
    slab = (D, SHARD_COLS)

    def body(h_hbm, w_hbm, z_hbm, wfull_hbm, h_buf, land, zstage,
             h_sem, own_sem, z_sem, wout_sem, send_sems, recv_sems):
        x, y, c = _mesh_pos()
        sibling = (x, y, 1 - c)
        north = c == 1

        def chips_of(first_x):
            near = (jnp.where(first_x, 1 - x, x), jnp.where(first_x, y, 1 - y))
            far = (jnp.where(first_x, x, 1 - x), jnp.where(first_x, 1 - y, y))
            return [near, far, (1 - x, 1 - y)]

        mine, theirs = chips_of(north), chips_of(jnp.logical_not(north))

        def dev(chip, core):
            return 4 * chip[0] + 2 * chip[1] + core

        block_of = ([dev((x, y), c), dev((x, y), 1 - c)] + [dev(q, c) for q in mine]
                    + [dev(q, 1 - c) for q in theirs])

        def cols(d):
            if isinstance(d, int):
                return pl.ds(d * SHARD_COLS, SHARD_COLS)
            return pl.ds(pl.multiple_of(d * SHARD_COLS, LANES), SHARD_COLS)

        def send(k, src, dst_slot, to):
            return pltpu.make_async_remote_copy(
                src_ref=src, dst_ref=land.at[dst_slot], send_sem=send_sems.at[k], recv_sem=recv_sems.at[k],
                device_id=to, device_id_type=MESH)

        def to_sibling():
            return send(0, w_hbm, 1, sibling)

        def to_chip(j):
            if j == 2:
                return send(3, land.at[2], 4, (*mine[1], c))
            return send(1 + j, w_hbm, 2 + j, (*mine[j], c))

        def pass_on(j):
            return send(4 + j, land.at[2 + j], 5 + j, sibling)

        own = _SplitCopy(w_hbm, land.at[0], own_sem)
        h_copy = _SplitCopy(h_hbm, h_buf, h_sem)
        own.start()
        h_copy.start()
        to_sibling().start()
        to_chip(0).start()
        h_copy.wait()
        own.wait()

        def multiply(slot, n_done):
            d = block_of[slot]
            out = _SplitCopy(land.at[slot], wfull_hbm.at[:, cols(d)], wout_sem.at[slot])
            out.start()
            for r in range(2):
                rows = pl.ds(r * half, half)
                zc = _SplitCopy(zstage.at[r], z_hbm.at[rows, cols(d)], z_sem.at[r])
                if n_done > 0:
                    zc.wait()
                zstage[r] = _dot(h_buf[r * half:(r + 1) * half, :], land[slot])
                zc.start()
            return out

        outs = [multiply(0, 0)]
        to_sibling().wait_recv()
        outs.append(multiply(1, 1))
        done = 2
        for j in range(3):
            to_chip(j).wait_recv()
            pass_on(j).start()
            to_chip(j).wait_send()
            if j < 2:
                to_chip(j + 1).start()
            outs.append(multiply(2 + j, done))
            pass_on(j).wait_recv()
            outs.append(multiply(5 + j, done + 1))
            done += 2
        for r in range(2):
            _SplitCopy(zstage.at[r], z_hbm.at[pl.ds(r * half, half), cols(0)], z_sem.at[r]).wait()
        for out in outs:
            out.wait()
        to_sibling().wait_send()
        for j in range(3):
            pass_on(j).wait_send()

    any_spec = pl.BlockSpec(memory_space=pl.ANY)
    return pl.pallas_call(
        body, name="in_proj_gather",
        out_shape=[jax.ShapeDtypeStruct((S, IN_COLS), F32), jax.ShapeDtypeStruct((D, IN_COLS), BF16)],
        in_specs=[any_spec] * 2, out_specs=[any_spec] * 2,
        scratch_shapes=[pltpu.VMEM((S, D), BF16), pltpu.VMEM((8,) + slab, BF16), pltpu.VMEM((2, half, SHARD_COLS), F32),
                        pltpu.SemaphoreType.DMA, pltpu.SemaphoreType.DMA, pltpu.SemaphoreType.DMA((2,)),
                        pltpu.SemaphoreType.DMA((8,)), pltpu.SemaphoreType.DMA((7,)), pltpu.SemaphoreType.DMA((7,))],
        compiler_params=_cp(),
    )(h, w_shard)


def _exchange_sibling(ids, gb):
    n = len(gb)

    def body(*refs):
        ins, outs = refs[:n], refs[n:2 * n]
        send_sems, recv_sems = refs[2 * n:]
        x, y, c = _mesh_pos()
        sibling = (x, y, 1 - c)
        copies = []
        for i, a in enumerate(ids):
            for q in range(4):
                copies.append(pltpu.make_async_remote_copy(
                    src_ref=_shard_of(ins[i], a, 2 * q + (1 - c)), dst_ref=outs[i].at[q],
                    send_sem=send_sems.at[i * 4 + q], recv_sem=recv_sems.at[i * 4 + q],
                    device_id=sibling, device_id_type=MESH))
        for cp in copies:
            cp.start()
        for cp in copies:
            cp.wait()

    any_spec = pl.BlockSpec(memory_space=pl.ANY)
    return pl.pallas_call(
        body, name="grads_to_sibling",
        out_shape=[jax.ShapeDtypeStruct((4,) + SHARD_SHAPES[a], BF16) for a in ids],
        in_specs=[any_spec] * n, out_specs=[any_spec] * n,
        scratch_shapes=[pltpu.SemaphoreType.DMA((4 * n,)), pltpu.SemaphoreType.DMA((4 * n,))],
    )(*gb)


def _exchange_chips_steps(ins, outs, send_sems, recv_sems):
    x, y, c = _mesh_pos()
    chips = [(1 - x, y), (x, 1 - y), (1 - x, 1 - y)]
    copies = []
    for a in range(len(ins)):
        for k, chip in enumerate(chips):
            copies.append(pltpu.make_async_remote_copy(
                src_ref=ins[a].at[2 * chip[0] + chip[1]], dst_ref=outs[a].at[k],
                send_sem=send_sems.at[a * 3 + k], recv_sem=recv_sems.at[a * 3 + k],
                device_id=(*chip, c), device_id_type=MESH))

    def start():
        for cp in copies:
            cp.start()

    def end():
        for cp in copies:
            cp.wait()

    return start, end


def _gather_small(small):
    def body(small_ref, small_out, ssend, srecv, local_sem):
        x, y, c = _mesh_pos()
        me = 4 * x + 2 * y + c
        copies = []
        for r in range(1, NDEV):
            peer = (1 - x if r & 4 else x, 1 - y if r & 2 else y, 1 - c if r & 1 else c)
            copies.append(pltpu.make_async_remote_copy(
                src_ref=small_ref, dst_ref=small_out.at[me],
                send_sem=ssend.at[r - 1], recv_sem=srecv.at[r - 1],
                device_id=peer, device_id_type=MESH))
        own = pltpu.make_async_copy(small_ref, small_out.at[me], local_sem)
        own.start()
        for cp in copies:
            cp.start()
        for cp in copies:
            cp.wait()
        own.wait()

    any_spec = pl.BlockSpec(memory_space=pl.ANY)
    return pl.pallas_call(
        body, name="gather_small",
        out_shape=jax.ShapeDtypeStruct((NDEV,) + small.shape, F32),
        in_specs=[any_spec], out_specs=any_spec,
        scratch_shapes=[pltpu.SemaphoreType.DMA((NDEV - 1,)), pltpu.SemaphoreType.DMA((NDEV - 1,)),
                        pltpu.SemaphoreType.DMA],
    )(small)


def _shard_tiles(a):
    rows, cols = SHARD_SHAPES[a]
    tr = min(rows, 256)
    return (tr, cols), rows // tr


def _full_index(a, d, i):
    (tr, _), nt = _shard_tiles(a)
    if a in (0, 2):
        return (i, d)
    return (d * nt + i, 0)


def _cast_bf16(x, name):
    rows, cols = x.shape
    tr = min(rows, 256)

    def body(x_ref, o_ref):
        o_ref[...] = x_ref[...].astype(BF16)

    return pl.pallas_call(
        body, name=name, out_shape=jax.ShapeDtypeStruct(x.shape, BF16), grid=(rows // tr,),
        in_specs=[pl.BlockSpec((tr, cols), lambda i: (i, 0))],
        out_specs=pl.BlockSpec((tr, cols), lambda i: (i, 0)),
        compiler_params=_cp(("parallel",)),
    )(x)


def _chip_partials(a, g_full, r1, core):
    tile, nt = _shard_tiles(a)

    def body(c_ref, g_ref, r_ref, o_ref):
        o_ref[0] = (g_ref[...] + r_ref[0].astype(F32)).astype(BF16)

    grid_spec = pltpu.PrefetchScalarGridSpec(
        num_scalar_prefetch=1, grid=(4, nt),
        in_specs=[pl.BlockSpec(tile, lambda q, i, c: _full_index(a, 2 * q + c[0], i)),
                  pl.BlockSpec((1,) + tile, lambda q, i, c: (q, i, 0))],
        out_specs=pl.BlockSpec((1,) + tile, lambda q, i, c: (q, i, 0)))
    return pl.pallas_call(
        body, name=f"chip_partials_{a}", grid_spec=grid_spec,
        out_shape=jax.ShapeDtypeStruct((4,) + SHARD_SHAPES[a], BF16),
        compiler_params=_cp(("parallel", "parallel")),
    )(core, g_full, r1)


def _adam(w, g, m, v):
    m = ADAM_B1 * m + (1.0 - ADAM_B1) * g
    v = ADAM_B2 * v + (1.0 - ADAM_B2) * (g * g)
    m_hat = m / (1.0 - ADAM_B1 ** ADAM_STEP)
    v_hat = v / (1.0 - ADAM_B2 ** ADAM_STEP)
    delta = -ADAM_LR * (m_hat / (jnp.sqrt(v_hat) + ADAM_EPS) + ADAM_WD * w)
    return delta, m, v


def _reduce_and_update(a, w, m, v, g_full, r1, r2, pos):
    tile, nt = _shard_tiles(a)

    def body(p_ref, w_ref, m_ref, v_ref, g_ref, r1_ref, r2_ref, go_ref, do_ref, mo_ref, vo_ref):
        g = g_ref[...] + r1_ref[0].astype(F32)
        g = g + r2_ref[0].astype(F32)
        g = g + r2_ref[1].astype(F32)
        g = g + r2_ref[2].astype(F32)
        delta, m_new, v_new = _adam(w_ref[...], g, m_ref[...], v_ref[...])
        go_ref[...] = g
        do_ref[...] = delta
        mo_ref[...] = m_new
        vo_ref[...] = v_new

    own = pl.BlockSpec(tile, lambda i, p: (i, 0))
    grid_spec = pltpu.PrefetchScalarGridSpec(
        num_scalar_prefetch=1, grid=(nt,),
        in_specs=[own, own, own,
                  pl.BlockSpec(tile, lambda i, p: _full_index(a, p[0], i)),
                  pl.BlockSpec((1,) + tile, lambda i, p: (p[1], i, 0)),
                  pl.BlockSpec((3,) + tile, lambda i, p: (0, i, 0))],
        out_specs=[own] * 4)
    shp = jax.ShapeDtypeStruct(w.shape, F32)
    return pl.pallas_call(
        body, name=f"reduce_update_{a}", grid_spec=grid_spec, out_shape=[shp] * 4,
        compiler_params=_cp(("parallel",)),
    )(pos, w, m, v, g_full, r1, r2)


def _reduce_own_and_update(w, m, v, g_chip, r2):
    tile, nt = _shard_tiles(0)

    def body(w_ref, m_ref, v_ref, g_ref, r2_ref, go_ref, do_ref, mo_ref, vo_ref):
        g = g_ref[...] + r2_ref[0].astype(F32)
        g = g + r2_ref[1].astype(F32)
        delta, m_new, v_new = _adam(w_ref[...], g, m_ref[...], v_ref[...])
        go_ref[...] = g
        do_ref[...] = delta
        mo_ref[...] = m_new
        vo_ref[...] = v_new

    own = pl.BlockSpec(tile, lambda i: (i, 0))
    shp = jax.ShapeDtypeStruct(w.shape, F32)
    return pl.pallas_call(
        body, name="reduce_update_0", grid=(nt,), out_shape=[shp] * 4,
        in_specs=[own, own, own, own, pl.BlockSpec((2,) + tile, lambda i: (0, i, 0))], out_specs=[own] * 4,
        compiler_params=_cp(("parallel",)),
    )(w, m, v, g_chip, r2)


def _small_update(gathered, norm_w, lb_logits, hnw, fnw, moments):
    m_nw, m_lb, m_hn, m_fn, v_nw, v_lb, v_hn, v_fn = moments

    def body(g_ref, nw, lb, hn, fn, mnw, mlb, mhn, mfn, vnw, vlb, vhn, vfn,
             loss_o, g_nw, g_lb, g_hn, g_fn, d_nw, d_lb, d_hn, d_fn,
             mo_nw, mo_lb, mo_hn, mo_fn, vo_nw, vo_lb, vo_hn, vo_fn):
        tot = g_ref[0]
        for d in range(1, NDEV):
            tot = tot + g_ref[d]
        loss_o[...] = tot[4:5, 0:LANES]
        logits = lb[...]
        lbv = jax.nn.sigmoid(logits[0:1] - logits[1:2])
        chain = tot[1:2] * lbv * (1.0 - lbv)
        grads = (tot[0:1], jnp.concatenate([chain, -chain], axis=0), tot[2:3, 0:LANES], tot[3:4])
        outs = ((nw, mnw, vnw, g_nw, d_nw, mo_nw, vo_nw), (lb, mlb, vlb, g_lb, d_lb, mo_lb, vo_lb),
                (hn, mhn, vhn, g_hn, d_hn, mo_hn, vo_hn), (fn, mfn, vfn, g_fn, d_fn, mo_fn, vo_fn))
        for g, (w_r, m_r, v_r, g_o, d_o, m_o, v_o) in zip(grads, outs):
            delta, m_new, v_new = _adam(w_r[...], g, m_r[...], v_r[...])
            g_o[...] = g
            d_o[...] = delta
            m_o[...] = m_new
            v_o[...] = v_new

    shapes = [norm_w.shape, lb_logits.shape, hnw.shape, fnw.shape]
    out_shape = [jax.ShapeDtypeStruct((1, LANES), F32)] + [jax.ShapeDtypeStruct(s, F32) for s in shapes] * 4
    return pl.pallas_call(body, name="small_update", out_shape=out_shape, compiler_params=_cp())(
        gathered, norm_w, lb_logits, hnw, fnw, m_nw, m_lb, m_hn, m_fn, v_nw, v_lb, v_hn, v_fn)


def _rmsnorm_in(x, norm_w):
    tr = 512

    def body(x_ref, w_ref, h_ref):
        xv = x_ref[...]
        r = lax.rsqrt(jnp.mean(xv * xv, axis=-1, keepdims=True) + EPS)
        h_ref[...] = (xv * r * w_ref[...]).astype(BF16)

    return pl.pallas_call(
        body, name="rmsnorm_in", out_shape=jax.ShapeDtypeStruct((S, D), BF16), grid=(S // tr,),
        in_specs=[pl.BlockSpec((tr, D), lambda i: (i, 0)), pl.BlockSpec((1, D), lambda i: (0, 0))],
        out_specs=pl.BlockSpec((tr, D), lambda i: (i, 0)),
        compiler_params=_cp(("parallel",)),
    )(x, norm_w)


def _in_proj(h, w_in):
    tn = 1024

    def body(h_ref, w_ref, z_ref):
        z_ref[...] = _dot(h_ref[...], w_ref[...])

    return pl.pallas_call(
        body, name="in_proj", out_shape=jax.ShapeDtypeStruct((S, IN_COLS), F32), grid=(IN_COLS // tn,),
        in_specs=[pl.BlockSpec((S, D), lambda j: (0, 0)), pl.BlockSpec((D, tn), lambda j: (0, j))],
        out_specs=pl.BlockSpec((S, tn), lambda j: (0, j)),
        compiler_params=_cp(("parallel",)),
    )(h, w_in)


def _block_tri(n, block, upper=False):
    r = lax.broadcasted_iota(jnp.int32, (n, n), 0)
    c = lax.broadcasted_iota(jnp.int32, (n, n), 1)
    keep = (c >= r) if upper else (c <= r)
    return jnp.where(keep & ((r // block) == (c // block)), 1.0, 0.0).astype(BF16)


def _tril_mask(n):
    r = lax.broadcasted_iota(jnp.int32, (n, n), 0)
    c = lax.broadcasted_iota(jnp.int32, (n, n), 1)
    return c <= r


def _chunk_scores(q, k, b, bex, r0, mask):
    rows = slice(r0, r0 + CHUNK)
    parts, qs_l, ek_l, eq_l = [], [], [], []
    for i in range(CHUNK // SUB):
        ri = slice(r0 + SUB * i, r0 + SUB * (i + 1))
        base = bex[r0 + SUB * i:r0 + SUB * i + 1]
        eq = jnp.exp(b[ri] - base)
        ek = jnp.exp(jnp.minimum(base - b[rows], EXP_CLAMP))
        qs = q[ri] * eq
        parts.append(_dot_nt(qs.astype(BF16), (k[rows] * ek).astype(BF16)))
        qs_l.append(qs)
        ek_l.append(ek)
        eq_l.append(eq)
    return jnp.where(mask, jnp.concatenate(parts, axis=0), 0.0), qs_l, ek_l, eq_l


def _hgrn_cols(hq, hf, hi, lb):
    sg = jax.nn.sigmoid(hf)
    f = lb + (1.0 - lb) * sg
    g = jnp.log(f)
    b = _dot_ones(_block_tri(HBLK, CHUNK), g)
    return _silu(hq), 1.0 - f, g, hi, sg, f, b


GATHER_IDS = (1, 2, 3)


def _hgrn_fwd(z, lbv, hnw, shards):
    ntb, nch = S // HBLK, HBLK // CHUNK
    n = len(GATHER_IDS)

    def body(hq_ref, hf_ref, hi_ref, hg_ref, lb_ref, hnw_ref, s0, s1, s2, o_ref, oa_ref, st_ref, f0, f1, f2,
             state, send_sems, recv_sems, local_sems):
        start, middle, end = _allgather_steps(GATHER_IDS, (s0, s1, s2), (f0, f1, f2), send_sems, recv_sems, local_sems)

        @pl.when(pl.program_id(0) == 0)
        def _():
            state[...] = jnp.zeros_like(state)
            start()

        pl.when(pl.program_id(0) == ntb // 2)(middle)

        q_a, k_a, g_a, v_a, _, _, b_a = _hgrn_cols(hq_ref[...], hf_ref[...], hi_ref[...], lb_ref[...])
        bex_a = b_a - g_a
        eb_a = jnp.exp(b_a)
        mask = _tril_mask(CHUNK)
        hg = hg_ref[...]
        w = hnw_ref[...]
        for h in range(HEADS):
            cols = slice(128 * h, 128 * h + 128)
            q, k, v, b, bex, eb = q_a[:, cols], k_a[:, cols], v_a[:, cols], b_a[:, cols], bex_a[:, cols], eb_a[:, cols]
            st = state[h]
            outs = []
            for c in range(nch):
                r0 = c * CHUNK
                rows = slice(r0, r0 + CHUNK)
                a, _, _, _ = _chunk_scores(q, k, b, bex, r0, mask)
                vb = v[rows].astype(BF16)
                b_last = b[r0 + CHUNK - 1:r0 + CHUNK]
                qe = (q[rows] * eb[rows]).astype(BF16)
                outs.append(_dot(a.astype(BF16), vb) + _dot_nt(qe, st.astype(BF16)))
                st_ref[h, c] = st
                ke = (k[rows] * jnp.exp(b_last - b[rows])).astype(BF16)
                st = st * jnp.exp(b_last) + _dot_tn(vb, ke)
            state[h] = st
            o = jnp.concatenate(outs, axis=0)
            o_ref[:, cols] = o
            r = lax.rsqrt(jnp.mean(o * o, axis=-1, keepdims=True) + EPS)
            oa_ref[:, cols] = (o * r * w * _silu(hg[:, cols])).astype(BF16)

        pl.when(pl.program_id(0) == ntb - 1)(end)

    def zcol(j):
        return pl.BlockSpec((HBLK, D), lambda t: (t, j))

    out_blk = pl.BlockSpec((HBLK, D), lambda t: (t, 0))
    any_spec = pl.BlockSpec(memory_space=pl.ANY)
    return pl.pallas_call(
        body, name="hgrn_fwd", grid=(ntb,),
        out_shape=[jax.ShapeDtypeStruct((S, D), F32), jax.ShapeDtypeStruct((S, D), BF16),
                   jax.ShapeDtypeStruct((HEADS, S // CHUNK, 128, 128), F32)]
        + [jax.ShapeDtypeStruct(FULL_SHAPES[a], BF16) for a in GATHER_IDS],
        in_specs=[zcol(0), zcol(1), zcol(2), zcol(3),
                  pl.BlockSpec((1, D), lambda t: (0, 0)), pl.BlockSpec((1, 128), lambda t: (0, 0))] + [any_spec] * n,
        out_specs=[out_blk, out_blk, pl.BlockSpec((HEADS, nch, 128, 128), lambda t: (0, t, 0, 0))] + [any_spec] * n,
        scratch_shapes=[pltpu.VMEM((HEADS, 128, 128), F32), pltpu.SemaphoreType.DMA((7 * n,)),
                        pltpu.SemaphoreType.DMA((7 * n,)), pltpu.SemaphoreType.DMA((n,))],
        compiler_params=_cp(("arbitrary",)),
    )(z, z, z, z, lbv, hnw, *shards)


def _half_mask():
    lane = lax.broadcasted_iota(jnp.int32, (1, LANES), 1)
    return (lane % 64) < 32


def _rope(t, cc, ss, first_half):
    partner = jnp.where(first_half, pltpu.roll(t, 96, 1), pltpu.roll(t, 32, 1))
    return t * cc + partner * ss


def _attn_masks():
    i = lax.broadcasted_iota(jnp.int32, (128, 128), 0)
    j = lax.broadcasted_iota(jnp.int32, (128, 128), 1)
    return j >= i, j <= i


def _to_residues_dyn(g, dst, src, row0=0, dtype=None):
    for gi, dil in enumerate((1, 4, 16)):
        m = S // dil

        @pl.when(g == gi)
        def _(dil=dil, m=m):
            for r in range(dil):
                v = src[...] if dil == 1 else src[pl.ds(r, m, stride=dil), :]
                if dtype is not None:
                    v = v.astype(dtype)
                dst[row0 + r * m:row0 + (r + 1) * m, 0:LANES] = v


def _from_residues_dyn(g, dst, src, row0=0):
    for gi, dil in enumerate((1, 4, 16)):
        m = S // dil

        @pl.when(g == gi)
        def _(dil=dil, m=m):
            for r in range(dil):
                v = src[row0 + r * m:row0 + (r + 1) * m, :]
                if dil == 1:
                    dst[...] = v
                else:
                    dst[pl.ds(r, m, stride=dil), :] = v


def _group_blocks(g):
    return jnp.where(g == 0, 16, jnp.where(g == 1, 4, 1))


def _attn_in_specs(extra):
    def zcol(off):
        return pl.BlockSpec((S, LANES), lambda p, g: (0, off + 4 * g + p))

    per_pair = pl.BlockSpec((S, LANES), lambda p, g: (0, p))
    const = pl.BlockSpec((S, LANES), lambda p, g: (0, 0))
    return [zcol(32), zcol(44), zcol(56), pl.BlockSpec((S, LANES), lambda p, g: (0, 68 + p)), const, const] + [per_pair] * extra


def _attn_fwd(z, cc, ss):
    def body(q_ref, k_ref, v_ref, ag_ref, cc_ref, ss_ref, ob_ref, lse_ref, obg_ref,
             tmp, qs, ks, vx, og, mg, lg, o_t, m_t, l_t, o_acc, m_acc, l_acc):
        g = pl.program_id(1)
        first_half = _half_mask()
        prev_ok, cur_ok = _attn_masks()
        lane = lax.broadcasted_iota(jnp.int32, (1, LANES), 1)
        heads = (lane < 64, lane >= 64)
        nblk = _group_blocks(g)

        @pl.when(g == 0)
        def _():
            ks[0:ATT_PAD, :] = jnp.zeros((ATT_PAD, LANES), BF16)
            vx[0:ATT_PAD, 0:LANES] = jnp.zeros((ATT_PAD, LANES), BF16)
            vx[:, LANES:2 * LANES] = jnp.ones((ATT_PAD + S, LANES), BF16)

        tmp[...] = _rope(q_ref[...], cc_ref[...], ss_ref[...], first_half) * ATT_SCALE
        _to_residues_dyn(g, qs, tmp)
        tmp[...] = _rope(k_ref[...], cc_ref[...], ss_ref[...], first_half)
        _to_residues_dyn(g, ks, tmp, ATT_PAD, BF16)
        _to_residues_dyn(g, vx, v_ref, ATT_PAD, BF16)

        def unit(u, carry):
            start = pl.multiple_of(u * 128, 128)
            cur = pl.ds(start, 128)
            pm = prev_ok & ((u & (nblk - 1)) != 0)
            qu = qs[cur, :]
            kcat = ks[pl.ds(start, 256), :]
            vext = vx[pl.ds(start, 256), :]
            o_u = m_u = l_u = None
            for hh in range(2):
                s = _dot_nt(jnp.where(heads[hh], qu, 0.0).astype(BF16), kcat)
                sp = jnp.where(pm, s[:, 0:128], -jnp.inf)
                sc = jnp.where(cur_ok, s[:, 128:256], -jnp.inf)
                m = jnp.max(jnp.maximum(sp, sc), axis=-1, keepdims=True)
                p = jnp.concatenate([jnp.exp(sp - m), jnp.exp(sc - m)], axis=1).astype(BF16)
                ol = _dot(p, vext)
                mb = jnp.broadcast_to(m, (128, LANES))
                if hh == 0:
                    o_u, l_u, m_u = ol[:, 0:128], ol[:, 128:256], mb
                else:
                    o_u = jnp.where(heads[1], ol[:, 0:128], o_u)
                    l_u = jnp.where(heads[1], ol[:, 128:256], l_u)
                    m_u = jnp.where(heads[1], mb, m_u)
            og[cur, :] = o_u
            mg[cur, :] = m_u
            lg[cur, :] = l_u
            return carry

        lax.fori_loop(0, 16, unit, 0, unroll=ATT_UNROLL)
        _from_residues_dyn(g, o_t, og)
        _from_residues_dyn(g, m_t, mg)
        _from_residues_dyn(g, l_t, lg)

        @pl.when(g == 0)
        def _():
            o_acc[...] = o_t[...]
            m_acc[...] = m_t[...]
            l_acc[...] = l_t[...]

        @pl.when(g > 0)
        def _():
            m_new = jnp.maximum(m_acc[...], m_t[...])
            wa, wb = jnp.exp(m_acc[...] - m_new), jnp.exp(m_t[...] - m_new)
            o_acc[...] = o_acc[...] * wa + o_t[...] * wb
            l_acc[...] = l_acc[...] * wa + l_t[...] * wb
            m_acc[...] = m_new

        @pl.when(g == 2)
        def _():
            ob = o_acc[...] / l_acc[...]
            ob_ref[...] = ob
            lse_ref[...] = m_acc[...] + jnp.log(l_acc[...])
            obg_ref[...] = (ob * _silu(ag_ref[...])).astype(BF16)

    blk = pl.BlockSpec((S, LANES), lambda p, g: (0, p))
    buf = pltpu.VMEM((S, LANES), F32)
    return pl.pallas_call(
        body, name="attn_fwd", grid=(4, 3),
        out_shape=[jax.ShapeDtypeStruct((S, 512), F32), jax.ShapeDtypeStruct((S, 512), F32),
                   jax.ShapeDtypeStruct((S, 512), BF16)],
        in_specs=_attn_in_specs(0), out_specs=[blk, blk, blk],
        scratch_shapes=[buf, buf, pltpu.VMEM((ATT_PAD + S, LANES), BF16), pltpu.VMEM((ATT_PAD + S, 2 * LANES), BF16)] + [buf] * 9,
        compiler_params=_cp(("parallel", "arbitrary")),
    )(z, z, z, z, cc, ss)


def _tail(x, o_a, o_bg, z, target, w_a, w_b, w_out, fnw):
    tm = 256

    def body(x_ref, oa_ref, ob_ref, gpa_ref, gpb_ref, t_ref, wa_ref, wb_ref, wo_ref, fnw_ref,
             dx2_ref, dx2b_ref, dgp_ref, doa_ref, dob_ref, mg_ref, dya_ref, dyb_ref, small_ref):
        @pl.when(pl.program_id(0) == 0)
        def _():
            small_ref[...] = jnp.zeros_like(small_ref)

        wa, wb, wo = wa_ref[...], wb_ref[...], wo_ref[...]
        y_a = _dot(oa_ref[...], wa)
        y_b = _dot(ob_ref[...], wb)
        ga = jax.nn.sigmoid(gpa_ref[...])
        gb = jax.nn.sigmoid(gpb_ref[...])
        merged = (ga * y_a + gb * y_b).astype(BF16)
        x2 = x_ref[...] + _dot(merged, wo)
        r2 = lax.rsqrt(jnp.mean(x2 * x2, axis=-1, keepdims=True) + EPS)
        n2 = x2 * r2
        fw = fnw_ref[...]
        err = n2 * fw - t_ref[...]
        loss = 0.5 * jnp.sum(jnp.sum(err * err, axis=-1, keepdims=True), axis=0, keepdims=True) / D
        dy = err * (1.0 / D)
        g_fnw = jnp.sum(dy * n2, axis=0, keepdims=True)
        dn = dy * fw
        dx2 = r2 * (dn - n2 * jnp.mean(dn * n2, axis=-1, keepdims=True))
        dx2b = dx2.astype(BF16)
        dmerged = _dot_nt(dx2b, wo)
        dy_a = (dmerged * ga).astype(BF16)
        dy_b = (dmerged * gb).astype(BF16)
        dx2_ref[...] = dx2
        dx2b_ref[...] = dx2b
        dgp_ref[:, 0:D] = (dmerged * y_a * ga * (1.0 - ga)).astype(BF16)
        dgp_ref[:, D:2 * D] = (dmerged * y_b * gb * (1.0 - gb)).astype(BF16)
        doa_ref[...] = _dot_nt(dy_a, wa)
        dob_ref[...] = _dot_nt(dy_b, wb)
        mg_ref[...] = merged
        dya_ref[...] = dy_a
        dyb_ref[...] = dy_b
        small_ref[0:1, :] += g_fnw
        small_ref[1:2, :] += jnp.broadcast_to(loss, (1, D))

    def rows(cols, off=0):
        return pl.BlockSpec((tm, cols), lambda i: (i, off))

    def whole(shape):
        return pl.BlockSpec(shape, lambda i: (0, 0))

    return pl.pallas_call(
        body, name="tail", grid=(S // tm,),
        out_shape=[jax.ShapeDtypeStruct((S, D), F32), jax.ShapeDtypeStruct((S, D), BF16),
                   jax.ShapeDtypeStruct((S, 2 * D), BF16), jax.ShapeDtypeStruct((S, D), F32),
                   jax.ShapeDtypeStruct((S, 512), F32), jax.ShapeDtypeStruct((S, D), BF16),
                   jax.ShapeDtypeStruct((S, D), BF16), jax.ShapeDtypeStruct((S, D), BF16),
                   jax.ShapeDtypeStruct((8, D), F32)],
        in_specs=[rows(D), rows(D), rows(512), rows(D, 9), rows(D, 10), rows(D),
                  whole((D, D)), whole((512, D)), whole((D, D)), whole((1, D))],
        out_specs=[rows(D), rows(D), rows(2 * D), rows(D), rows(512), rows(D), rows(D), rows(D), whole((8, D))],
        compiler_params=_cp(("arbitrary",)),
    )(x, o_a, o_bg, z, z, target, w_a, w_b, w_out, fnw)


def _tn_matmul(a, b, name):
    m, n = a.shape[1], b.shape[1]
    tn = 512

    def body(a_ref, b_ref, o_ref, ob_ref):
        acc = _dot_tn(a_ref[...], b_ref[...])
        o_ref[...] = acc
        ob_ref[...] = acc.astype(BF16)

    out_blk = pl.BlockSpec((m, tn), lambda j: (0, j))
    return pl.pallas_call(
        body, name=name, grid=(n // tn,),
        out_shape=[jax.ShapeDtypeStruct((m, n), F32), jax.ShapeDtypeStruct((m, n), BF16)],
        in_specs=[pl.BlockSpec((S, m), lambda j: (0, 0)), pl.BlockSpec((S, tn), lambda j: (0, j))],
        out_specs=[out_blk, out_blk],
        compiler_params=_cp(("parallel",)),
    )(a, b)


def _hgrn_bwd(z, o, do_a, states, lbv, hnw, partials):
    ntb, nch = S // HBLK, HBLK // CHUNK
    n = len(GATHER_IDS)

    def body(hq_ref, hf_ref, hi_ref, hg_ref, o_ref, doa_ref, st_ref, lb_ref, hnw_ref, p0, p1, p2,
             dhq_ref, dhf_ref, dhi_ref, dhg_ref, glb_ref, ghn_ref, e0, e1, e2, dstate, send_sems, recv_sems):
        start, end = _exchange_chips_steps((p0, p1, p2), (e0, e1, e2), send_sems, recv_sems)

        @pl.when(pl.program_id(0) == 0)
        def _():
            dstate[...] = jnp.zeros_like(dstate)
            glb_ref[...] = jnp.zeros_like(glb_ref)
            ghn_ref[...] = jnp.zeros_like(ghn_ref)
            start()

        lb_a = lb_ref[...]
        hq_a, hg_a = hq_ref[...], hg_ref[...]
        q_a, k_a, g_a, v_a, sg_a, f_a, b_a = _hgrn_cols(hq_a, hf_ref[...], hi_ref[...], lb_a)
        bex_a = b_a - g_a
        eb_a = jnp.exp(b_a)
        w = hnw_ref[...]
        mask = _tril_mask(CHUNK)
        upper = _block_tri(CHUNK, CHUNK, upper=True)
        for h in range(HEADS):
            cols = slice(128 * h, 128 * h + 128)
            q, k, v, b, bex, eb = q_a[:, cols], k_a[:, cols], v_a[:, cols], b_a[:, cols], bex_a[:, cols], eb_a[:, cols]
            hq, hg, sg, f, lb = hq_a[:, cols], hg_a[:, cols], sg_a[:, cols], f_a[:, cols], lb_a[:, cols]
            ov, doa = o_ref[:, cols], doa_ref[:, cols]
            r = lax.rsqrt(jnp.mean(ov * ov, axis=-1, keepdims=True) + EPS)
            n = ov * r
            sil = _silu(hg)
            dhg_ref[:, cols] = (doa * n * w * _dsilu(hg)).astype(BF16)
            ghn_ref[h] += jnp.sum(doa * sil * n, axis=0, keepdims=True)
            dn = doa * sil * w
            do = r * (dn - n * jnp.mean(dn * n, axis=-1, keepdims=True))

            dst = dstate[h]
            dq_l, dk_l, dv_l, dg_l = [None] * nch, [None] * nch, [None] * nch, [None] * nch
            for c in reversed(range(nch)):
                r0 = c * CHUNK
                rows = slice(r0, r0 + CHUNK)
                st = st_ref[h, c]
                bc, kc, qc = b[rows], k[rows], q[rows]
                vb, dob = v[rows].astype(BF16), do[rows].astype(BF16)
                b_last = bc[CHUNK - 1:CHUNK]
                e_last = jnp.exp(b_last)
                ekl = jnp.exp(b_last - bc)
                dstb = dst.astype(BF16)
                a, qs_l, ek_l, eq_l = _chunk_scores(q, k, b, bex, r0, mask)
                da = jnp.where(mask, _dot_nt(dob, vb), 0.0)
                dv_l[c] = _dot_tn(a.astype(BF16), dob) + _dot_nt((kc * ekl).astype(BF16), dstb)
                dq_inter = _dot(dob, st.astype(BF16)) * eb[rows]
                dk_state = _dot(vb, dstb) * ekl
                dq_parts, dk_intra = [], jnp.zeros((CHUNK, 128), F32)
                for i in range(CHUNK // SUB):
                    da_i = da[SUB * i:SUB * (i + 1)]
                    dq_parts.append(_dot3(_dot, da_i, kc * ek_l[i]) * eq_l[i])
                    dk_intra = dk_intra + _dot3(_dot_tn, da_i, qs_l[i]) * ek_l[i]
                dq = jnp.concatenate(dq_parts, axis=0) + dq_inter
                dk = dk_intra + dk_state
                last = (e_last * jnp.sum(st * dst, axis=0, keepdims=True)
                        + jnp.sum(kc * dk_state, axis=0, keepdims=True))
                dg_l[c] = _dot_ones(upper, qc * dq - kc * dk) + last
                dq_l[c], dk_l[c] = dq, dk
                dst = dst * e_last + _dot_tn(dob, (qc * eb[rows]).astype(BF16))
            dstate[h] = dst
            dq, dk = jnp.concatenate(dq_l, axis=0), jnp.concatenate(dk_l, axis=0)
            dg, dv = jnp.concatenate(dg_l, axis=0), jnp.concatenate(dv_l, axis=0)
            dhq_ref[:, cols] = (dq * _dsilu(hq)).astype(BF16)
            dhi_ref[:, cols] = dv.astype(BF16)
            df = dg / f - dk
            dhf_ref[:, cols] = (df * (1.0 - lb) * sg * (1.0 - sg)).astype(BF16)
            glb_ref[:, cols] += jnp.sum(df * (1.0 - sg), axis=0, keepdims=True)

        pl.when(pl.program_id(0) == ntb - 1)(end)

    def rev(t):
        return ntb - 1 - t

    def zcol(j):
        return pl.BlockSpec((HBLK, D), lambda t: (rev(t), j))

    blk = pl.BlockSpec((HBLK, D), lambda t: (rev(t), 0))
    any_spec = pl.BlockSpec(memory_space=pl.ANY)
    return pl.pallas_call(
        body, name="hgrn_bwd", grid=(ntb,),
        out_shape=[jax.ShapeDtypeStruct((S, D), BF16)] * 4
        + [jax.ShapeDtypeStruct((1, D), F32), jax.ShapeDtypeStruct((HEADS, 1, 128), F32)]
        + [jax.ShapeDtypeStruct((3,) + SHARD_SHAPES[a], BF16) for a in GATHER_IDS],
        in_specs=[zcol(0), zcol(1), zcol(2), zcol(3), blk, blk,
                  pl.BlockSpec((HEADS, nch, 128, 128), lambda t: (0, rev(t), 0, 0)),
                  pl.BlockSpec((1, D), lambda t: (0, 0)), pl.BlockSpec((1, 128), lambda t: (0, 0))] + [any_spec] * n,
        out_specs=[blk] * 4 + [pl.BlockSpec((1, D), lambda t: (0, 0)),
                               pl.BlockSpec((HEADS, 1, 128), lambda t: (0, 0, 0))] + [any_spec] * n,
        scratch_shapes=[pltpu.VMEM((HEADS, 128, 128), F32), pltpu.SemaphoreType.DMA((3 * n,)),
                        pltpu.SemaphoreType.DMA((3 * n,))],
        compiler_params=_cp(("arbitrary",)),
    )(z, z, z, z, o, do_a, states, lbv, hnw, *partials)


def _attn_bwd(z, cc, ss, ob, lse, do_bg):
    def body(q_ref, k_ref, v_ref, ag_ref, cc_ref, ss_ref, ob_ref, lse_ref, dobg_ref,
             dq_ref, dk_ref, dv_ref, dag_ref,
             tmp, qs, ks, vs, dos, dqs, dks, dvs, dkp, dvp, do_t, ls0_t, ls1_t, dl0_t, dl1_t, ls0, ls1, dl0, dl1):
        g = pl.program_id(1)
        first_half = _half_mask()
        prev_ok, cur_ok = _attn_masks()
        lane = lax.broadcasted_iota(jnp.int32, (1, LANES), 1)
        heads = (lane < 64, lane >= 64)
        nblk = _group_blocks(g)
        cc_v, ss_v = cc_ref[...], ss_ref[...]

        @pl.when(g == 0)
        def _():
            ag, obv, dobg = ag_ref[...], ob_ref[...], dobg_ref[...]
            dag_ref[...] = (dobg * obv * _dsilu(ag)).astype(BF16)
            dob = dobg * _silu(ag)
            do_t[...] = dob
            prod = dob * obv
            dl = jnp.concatenate(
                [jnp.broadcast_to(jnp.sum(prod[:, 0:64], axis=-1, keepdims=True), (S, 64)),
                 jnp.broadcast_to(jnp.sum(prod[:, 64:128], axis=-1, keepdims=True), (S, 64))], axis=1)
            dl_sw = pltpu.roll(dl, 64, 1)
            dl0_t[...] = jnp.where(heads[0], dl, dl_sw)
            dl1_t[...] = jnp.where(heads[0], dl_sw, dl)
            ls = lse_ref[...]
            ls_sw = pltpu.roll(ls, 64, 1)
            ls0_t[...] = jnp.where(heads[0], ls, ls_sw)
            ls1_t[...] = jnp.where(heads[0], ls_sw, ls)
            ks[0:ATT_PAD, :] = jnp.zeros((ATT_PAD, LANES), BF16)
            vs[0:ATT_PAD, :] = jnp.zeros((ATT_PAD, LANES), BF16)

        tmp[...] = _rope(q_ref[...], cc_v, ss_v, first_half) * ATT_SCALE
        _to_residues_dyn(g, qs, tmp)
        tmp[...] = _rope(k_ref[...], cc_v, ss_v, first_half)
        _to_residues_dyn(g, ks, tmp, ATT_PAD, BF16)
        _to_residues_dyn(g, vs, v_ref, ATT_PAD, BF16)
        _to_residues_dyn(g, dos, do_t)
        _to_residues_dyn(g, ls0, ls0_t)
        _to_residues_dyn(g, ls1, ls1_t)
        _to_residues_dyn(g, dl0, dl0_t)
        _to_residues_dyn(g, dl1, dl1_t)
        lss, dls = (ls0, ls1), (dl0, dl1)

        def unit(u, carry):
            start = pl.multiple_of(u * 128, 128)
            cur = pl.ds(start, 128)
            both = pl.ds(start, 256)
            pm = prev_ok & ((u & (nblk - 1)) != 0)
            qu, dou = qs[cur, :], dos[cur, :]
            kcat, vcat = ks[both, :], vs[both, :]
            dq_u = None
            q_l, do_l, ds_l, p_l = [], [], [], []
            for hh in range(2):
                q_h = jnp.where(heads[hh], qu, 0.0).astype(BF16)
                do_h = jnp.where(heads[hh], dou, 0.0).astype(BF16)
                s = _dot_nt(q_h, kcat)
                dp = _dot_nt(do_h, vcat)
                lse_h, dl_h = lss[hh][cur, :], dls[hh][cur, :]
                pp = jnp.where(pm, jnp.exp(s[:, 0:128] - lse_h), 0.0)
                pc = jnp.where(cur_ok, jnp.exp(s[:, 128:256] - lse_h), 0.0)
                ds = jnp.concatenate([pp * (dp[:, 0:128] - dl_h), pc * (dp[:, 128:256] - dl_h)], axis=1).astype(BF16)
                dq = _dot(ds, kcat)
                dq_u = dq if hh == 0 else jnp.where(heads[1], dq, dq_u)
                q_l.append(q_h)
                do_l.append(do_h)
                ds_l.append(ds)
                p_l.append(jnp.concatenate([pp, pc], axis=1).astype(BF16))
            dkcat = _dot_tn(jnp.concatenate(ds_l, axis=0), jnp.concatenate(q_l, axis=0))
            dvcat = _dot_tn(jnp.concatenate(p_l, axis=0), jnp.concatenate(do_l, axis=0))
            dkp[cur, :] = dkcat[0:128]
            dks[cur, :] = dkcat[128:256]
            dvp[cur, :] = dvcat[0:128]
            dvs[cur, :] = dvcat[128:256]
            dqs[cur, :] = dq_u
            return carry

        lax.fori_loop(0, 16, unit, 0, unroll=ATT_UNROLL)
        dks[0:S - 128, :] += dkp[128:S, :]
        dvs[0:S - 128, :] += dvp[128:S, :]
        _from_residues_dyn(g, tmp, dqs)
        dq_ref[0] = (_rope(tmp[...], cc_v, -ss_v, first_half) * ATT_SCALE).astype(BF16)
        _from_residues_dyn(g, tmp, dks)
        dk_ref[0] = _rope(tmp[...], cc_v, -ss_v, first_half).astype(BF16)
        _from_residues_dyn(g, tmp, dvs)
        dv_ref[0] = tmp[...].astype(BF16)

    grp = pl.BlockSpec((1, S, LANES), lambda p, g: (g, 0, p))
    buf = pltpu.VMEM((S, LANES), F32)
    padded_b = pltpu.VMEM((ATT_PAD + S, LANES), BF16)
    return pl.pallas_call(
        body, name="attn_bwd", grid=(4, 3),
        out_shape=[jax.ShapeDtypeStruct((3, S, 512), BF16)] * 3 + [jax.ShapeDtypeStruct((S, 512), BF16)],
        in_specs=_attn_in_specs(3), out_specs=[grp, grp, grp, pl.BlockSpec((S, LANES), lambda p, g: (0, p))],
        scratch_shapes=[buf, buf, padded_b, padded_b] + [buf] * 15,
        compiler_params=_cp(("parallel", "arbitrary")),
    )(z, z, z, z, cc, ss, ob, lse, do_bg)


def _in_proj_bwd(dz, h, w_in):
    half = S // 2
    slab = (D, SHARD_COLS)

    def body(dz_hbm, h_hbm, w_hbm, dh_hbm, g_chip, r1_hbm, relay_hbm, r2_hbm, stage_hbm,
             h_buf, dz_buf, stage_d, r1_buf, stage_i, acc,
             dz_sem, w_sem, h_sem, r1_sem, out_sem, stage_sem, send_d, recv_d, send_i, recv_i):
        x, y, c = _mesh_pos()
        sibling = (x, y, 1 - c)
        north = c == 1
        near = (jnp.where(north, 1 - x, x), jnp.where(north, y, 1 - y))
        far = (jnp.where(north, x, 1 - x), jnp.where(north, 1 - y, y))
        chips = [(1 - x, 1 - y), near, far, (x, y)]

        def cols(d):
            return pl.ds(pl.multiple_of(d * SHARD_COLS, LANES), SHARD_COLS)

        blocks = []
        for q_sib, q in zip([chips[0], far, near, chips[3]], chips):
            blocks += [4 * q_sib[0] + 2 * q_sib[1] + (1 - c), 4 * q[0] + 2 * q[1] + c]

        def dz_tile(t):
            return _SplitCopy(dz_hbm.at[pl.ds((t % 2) * half, half), cols(blocks[t // 2])],
                                         dz_buf.at[t % 2], dz_sem.at[t % 2])

        def to_sibling(i):
            return pltpu.make_async_remote_copy(
                src_ref=stage_d.at[i % 2], dst_ref=r1_hbm.at[i], send_sem=send_d.at[i], recv_sem=recv_d.at[i],
                device_id=sibling, device_id_type=MESH)

        def to_owner(i):
            dst = relay_hbm if i == 0 else r2_hbm.at[i - 1]
            return pltpu.make_async_remote_copy(
                src_ref=stage_hbm.at[i], dst_ref=dst, send_sem=send_i.at[i], recv_sem=recv_i.at[i],
                device_id=(*(far if i == 2 else near), c), device_id_type=MESH)

        h_copy = _SplitCopy(h_hbm, h_buf, h_sem)
        h_copy.start()
        dz_tile(0).start()
        h_copy.wait()
        for b in range(8):
            i = b // 2
            g = None
            for r in range(2):
                t = 2 * b + r
                if t + 1 < 16:
                    dz_tile(t + 1).start()
                dz_tile(t).wait()
                part = _dot_tn(h_buf[r * half:(r + 1) * half, :], dz_buf[t % 2])
                g = part if g is None else g + part
                if b % 2 == 1 and r == 0:
                    to_sibling(i).wait_recv()
                    r1_copy = _SplitCopy(r1_hbm.at[i], r1_buf, r1_sem)
                    r1_copy.start()
            if b % 2 == 0:
                if i >= 2:
                    to_sibling(i - 2).wait_send()
                stage_d[i % 2] = g.astype(BF16)
                to_sibling(i).start()
            else:
                r1_copy.wait()
                g = g + r1_buf[...].astype(F32)
                if i == 2:
                    to_owner(0).wait_recv()
                    relay_copy = _SplitCopy(relay_hbm, r1_buf, r1_sem)
                    relay_copy.start()
                    relay_copy.wait()
                    g = g + r1_buf[...].astype(F32)
                if i < 3:
                    stage_i[...] = g.astype(BF16)
                    to_hbm = _SplitCopy(stage_i, stage_hbm.at[i], stage_sem)
                    to_hbm.start()
                    to_hbm.wait()
                    to_owner(i).start()
                else:
                    g_chip[...] = g
        to_sibling(2).wait_send()
        to_sibling(3).wait_send()

        def dz2(t):
            return _SplitCopy(
                dz_hbm.at[pl.ds((t % 2) * half, half), pl.ds((t // 2) * SHARD_COLS, SHARD_COLS)],
                dz_buf.at[t % 2], dz_sem.at[t % 2])

        def w2(b):
            return _SplitCopy(w_hbm.at[:, pl.ds(b * SHARD_COLS, SHARD_COLS)],
                                         stage_d.at[b % 2], w_sem.at[b % 2])

        dz2(0).start()
        w2(0).start()
        n2 = 16
        for t in range(n2):
            b, r = t // 2, t % 2
            if t + 1 < n2:
                dz2(t + 1).start()
            if r == 0:
                if b + 1 < n2 // 2:
                    w2(b + 1).start()
                w2(b).wait()
            dz2(t).wait()
            part = _dot_nt(dz_buf[t % 2], stage_d[b % 2])
            if b == 0:
                acc[r] = part
            else:
                acc[r] += part
        dh_out = [_SplitCopy(acc.at[r], dh_hbm.at[pl.ds(r * half, half), :], out_sem.at[r])
                  for r in range(2)]
        for cp in dh_out:
            cp.start()
        for cp in dh_out:
            cp.wait()
        for i in range(3):
            to_owner(i).wait_send()
        for i in (1, 2):
            to_owner(i).wait_recv()

    any_spec = pl.BlockSpec(memory_space=pl.ANY)
    return pl.pallas_call(
        body, name="in_proj_bwd",
        out_shape=[jax.ShapeDtypeStruct((S, D), F32), jax.ShapeDtypeStruct(slab, F32),
                   jax.ShapeDtypeStruct((4,) + slab, BF16), jax.ShapeDtypeStruct(slab, BF16),
                   jax.ShapeDtypeStruct((2,) + slab, BF16), jax.ShapeDtypeStruct((3,) + slab, BF16)],
        in_specs=[any_spec] * 3,
        out_specs=[any_spec, pl.BlockSpec(memory_space=pltpu.VMEM), any_spec, any_spec, any_spec, any_spec],
        scratch_shapes=[pltpu.VMEM((S, D), BF16), pltpu.VMEM((2, half, SHARD_COLS), BF16),
                        pltpu.VMEM((2,) + slab, BF16), pltpu.VMEM(slab, BF16), pltpu.VMEM(slab, BF16),
                        pltpu.VMEM((2, half, D), F32),
                        pltpu.SemaphoreType.DMA((2,)), pltpu.SemaphoreType.DMA((2,)), pltpu.SemaphoreType.DMA,
                        pltpu.SemaphoreType.DMA, pltpu.SemaphoreType.DMA((2,)), pltpu.SemaphoreType.DMA,
                        pltpu.SemaphoreType.DMA((4,)), pltpu.SemaphoreType.DMA((4,)),
                        pltpu.SemaphoreType.DMA((3,)), pltpu.SemaphoreType.DMA((3,))],
        compiler_params=_cp(),
    )(dz, h, w_in)


def _grad_x(x, norm_w, dh, dx2):
    tr = 256

    def body(x_ref, w_ref, dh_ref, dx2_ref, gx_ref, gnw_ref):
        @pl.when(pl.program_id(0) == 0)
        def _():
            gnw_ref[...] = jnp.zeros_like(gnw_ref)

        xv, dhv = x_ref[...], dh_ref[...]
        r = lax.rsqrt(jnp.mean(xv * xv, axis=-1, keepdims=True) + EPS)
        n = xv * r
        gnw_ref[...] += jnp.sum(dhv * n, axis=0, keepdims=True)
        dn = dhv * w_ref[...]
        gx_ref[...] = dx2_ref[...] + r * (dn - n * jnp.mean(dn * n, axis=-1, keepdims=True))

    row = pl.BlockSpec((tr, D), lambda i: (i, 0))
    vec = pl.BlockSpec((1, D), lambda i: (0, 0))
    return pl.pallas_call(
        body, name="grad_x", grid=(S // tr,),
        out_shape=[jax.ShapeDtypeStruct((S, D), F32), jax.ShapeDtypeStruct((1, D), F32)],
        in_specs=[row, vec, row, row], out_specs=[row, vec],
        compiler_params=_cp(("arbitrary",)),
    )(x, norm_w, dh, dx2)


def _rope_tables(positions):
    inv_freq = 10000.0 ** (-jnp.arange(0, 64, 2, dtype=F32) / 64)
    ang = positions.astype(F32)[:, None] * inv_freq[None, :]
    cos, sin = jnp.cos(ang), jnp.sin(ang)
    return jnp.tile(cos, (1, 4)), jnp.tile(jnp.concatenate([-sin, sin], axis=1), (1, 2))


def _local_step(x, positions, norm_w, lb_logits, hnw, fnw, target, w_in_shard, small_shards, core):
    cc, ss = _rope_tables(positions)
    lbv = jax.nn.sigmoid(lb_logits[0:1] - lb_logits[1:2])
    h = _rmsnorm_in(x, norm_w)
    z, w_in = _in_proj_gather(h, w_in_shard)
    o, o_a, states, w_a, w_b, w_out = _hgrn_fwd(z, lbv, hnw, small_shards)
    ob, lse, o_bg = _attn_fwd(z, cc, ss)
    dx2, dx2b, dgp, do_a, do_bg, merged, dy_a, dy_b, tail_small = _tail(x, o_a, o_bg, z, target, w_a, w_b, w_out, fnw)
    g_out, gb_out = _tn_matmul(merged, dx2b, "grad_w_out")
    g_a, gb_a = _tn_matmul(o_a, dy_a, "grad_w_a")
    g_b, gb_b = _tn_matmul(o_bg, dy_b, "grad_w_b")
    grads, gb = (g_a, g_b, g_out), (gb_a, gb_b, gb_out)
    r1 = _exchange_sibling(GATHER_IDS, gb)
    pb = [_chip_partials(a, grads[i], r1[i], core) for i, a in enumerate(GATHER_IDS)]
    dhq, dhf, dhi, dhg, glb, ghn, *r2 = _hgrn_bwd(z, o, do_a, states, lbv, hnw, pb)
    dq, dk, dv, dag = _attn_bwd(z, cc, ss, ob, lse, do_bg)
    dz = jnp.concatenate([dhq, dhf, dhi, dhg, dq[0], dq[1], dq[2], dk[0], dk[1], dk[2], dv[0], dv[1], dv[2], dag, dgp],
                         axis=1)
    dh, g_chip_in, _, _, r2_in, _ = _in_proj_bwd(dz, h, w_in)
    grad_x, gnw = _grad_x(x, norm_w, dh, dx2)
    ghn_row = jnp.pad(jnp.sum(ghn, axis=0), ((0, 0), (0, D - 128)))
    small = jnp.concatenate([gnw, glb, ghn_row, tail_small[0:2], jnp.zeros((3, D), F32)], axis=0)
    return grad_x, (g_chip_in, r2_in), grads, r1, r2, small


def kernel(x, positions, norm_w, w_in, lb_logits, hgrn_norm_w, w_branch_a, w_branch_b, w_out, final_norm_w, loss_target, m_norm_w, m_w_in, m_lb_logits, m_hgrn_norm_w, m_w_branch_a, m_w_branch_b, m_w_out, m_final_norm_w, v_norm_w, v_w_in, v_lb_logits, v_hgrn_norm_w, v_w_branch_a, v_w_branch_b, v_w_out, v_final_norm_w):
    ix, iy, ic = _mesh_pos()
    core = jnp.reshape(ic, (1,)).astype(jnp.int32)
    pos = jnp.stack([4 * ix + 2 * iy + ic, 2 * ix + iy]).astype(jnp.int32)

    shards = [w_in[0], w_branch_a[0], w_branch_b[0], w_out[0]]
    moments_m = [m_w_in[0], m_w_branch_a[0], m_w_branch_b[0], m_w_out[0]]
    moments_v = [v_w_in[0], v_w_branch_a[0], v_w_branch_b[0], v_w_out[0]]
    names = ("w_in", "w_a", "w_b", "w_out")
    ids = GATHER_IDS
    shards_b = [_cast_bf16(w, f"cast_{nm}") for w, nm in zip(shards, names)]

    fnw2 = final_norm_w.reshape(1, D)
    grad_x, (g_chip_in, r2_in), grads, r1, r2, small = _local_step(
        x[0], positions[0], norm_w, lb_logits, hgrn_norm_w, fnw2, loss_target[0], shards_b[0], shards_b[1:], core)

    gathered = _gather_small(small)
    big =[_reduce_own_and_update(shards[0], moments_m[0], moments_v[0], g_chip_in, r2_in)]
    big += [_reduce_and_update(a, shards[a], moments_m[a], moments_v[a], grads[i], r1[i], r2[i], pos)
            for i, a in enumerate(ids)]
    sm = _small_update(gathered, norm_w, lb_logits, hgrn_norm_w, fnw2,
                       (m_norm_w, m_lb_logits, m_hgrn_norm_w, m_final_norm_w.reshape(1, D),
                        v_norm_w, v_lb_logits, v_hgrn_norm_w, v_final_norm_w.reshape(1, D)))
    loss = sm[0][0, 0]
    outs = [loss, grad_x[None]]
    for kind in range(4):
        s_nw, s_lb, s_hn, s_fn = sm[1 + 4 * kind:5 + 4 * kind]
        outs += [s_nw, big[0][kind][None], s_lb, s_hn, big[1][kind][None], big[2][kind][None],
                 big[3][kind][None], s_fn.reshape(D)]
    return tuple(outs)
```

```python
import functools

import jax
import jax.numpy as jnp
from jax import lax
from jax.experimental import pallas as pl
from jax.experimental.pallas import tpu as pltpu

F32 = jnp.float32
BF16 = jnp.bfloat16
MESH = pl.DeviceIdType.MESH

S = 2048
D = 1024
NDEV = 8
HEADS = 8
CHUNK = 64
SUB = 16
HBLK = 256
ATT_PAD = 128
ATT_UNROLL = 4
COPY_PARTS = 4
EXP_CLAMP = 80.0
EPS = 1e-6
IN_COLS = 11264
SHARD_COLS = IN_COLS // NDEV
ATT_DILS = (1, 4, 16)
ATT_SCALE = 64 ** -0.5
LANES = 128

ADAM_LR, ADAM_B1, ADAM_B2, ADAM_EPS, ADAM_WD, ADAM_STEP = 0.001, 0.9, 0.999, 1e-08, 0.01, 10

VMEM_LIMIT = 56 * 1024 * 1024


def _cp(sem=None, **kw):
    return pltpu.CompilerParams(dimension_semantics=sem, vmem_limit_bytes=VMEM_LIMIT, **kw)


def _dot(a, b):
    return jnp.dot(a, b, preferred_element_type=F32)


def _dot_nt(a, b):
    return lax.dot_general(a, b, (((1,), (1,)), ((), ())), preferred_element_type=F32)


def _dot_tn(a, b):
    return lax.dot_general(a, b, (((0,), (0,)), ((), ())), preferred_element_type=F32)


def _split2(x):
    hi = x.astype(BF16)
    lo = (x - hi.astype(F32)).astype(BF16)
    return hi, lo


def _split3(x):
    hi = x.astype(BF16)
    r = x - hi.astype(F32)
    mid = r.astype(BF16)
    lo = (r - mid.astype(F32)).astype(BF16)
    return hi, mid, lo


def _dot_ones(ones_bf16, x):
    hi, mid, lo = _split3(x)
    return _dot(ones_bf16, hi) + _dot(ones_bf16, mid) + _dot(ones_bf16, lo)


def _silu(x):
    return x * jax.nn.sigmoid(x)


def _dsilu(x):
    s = jax.nn.sigmoid(x)
    return s * (1.0 + x * (1.0 - s))


def _mesh_pos():
    return lax.axis_index("x"), lax.axis_index("y"), lax.axis_index("c")


class _SplitCopy:
    def __init__(self, src, dst, sem):
        self.src, self.dst, self.sem = src, dst, sem

    def start(self):
        rows = self.src.shape[0] // COPY_PARTS
        for p in range(COPY_PARTS):
            chunk = pl.ds(p * rows, rows)
            pltpu.make_async_copy(self.src.at[chunk], self.dst.at[chunk], self.sem).start()

    def wait(self):
        pltpu.make_async_copy(self.src, self.dst, self.sem).wait()


def _shard_of(ref, a, d):
    if a == 0:
        return ref.at[:, pl.ds(pl.multiple_of(d * SHARD_COLS, LANES), SHARD_COLS)]
    if a == 2:
        return ref.at[:, pl.ds(pl.multiple_of(d * LANES, LANES), LANES)]
    return ref.at[pl.ds(pl.multiple_of(d * 128, 128), 128), :]


FULL_SHAPES = ((D, IN_COLS), (D, D), (512, D), (D, D))
SHARD_SHAPES = ((D, SHARD_COLS), (128, D), (512, 128), (128, D))


def _allgather_steps(ids, ins, outs, send_sems, recv_sems, local_sems):
    n = len(ids)
    x, y, c = _mesh_pos()
    me, sibling = (x, y, c), (x, y, 1 - c)
    chips = [(1 - x, y), (x, 1 - y), (1 - x, 1 - y)]

    def blk(a, p):
        return _shard_of(outs[a], ids[a], 4 * p[0] + 2 * p[1] + p[2])

    def copy(a, k, block, to, src=None):
        return pltpu.make_async_remote_copy(
            src_ref=blk(a, block) if src is None else src, dst_ref=blk(a, block),
            send_sem=send_sems.at[a * 7 + k], recv_sem=recv_sems.at[a * 7 + k],
            device_id=to, device_id_type=MESH)

    mine = [pltpu.make_async_copy(ins[a], blk(a, me), local_sems.at[a]) for a in range(n)]
    first = []
    for a in range(n):
        first += [copy(a, 1 + j, me, (*chip, c), src=ins[a]) for j, chip in enumerate(chips)]
    for a in range(n):
        first.append(copy(a, 0, me, sibling, src=ins[a]))
    passed = [copy(a, 4 + j, (*chip, c), sibling) for j, chip in enumerate(chips) for a in range(n)]

    def start():
        for cp in mine + first:
            cp.start()

    def middle():
        for j, chip in enumerate(chips):
            for a in range(n):
                copy(a, 1 + j, (*chip, c), me).wait_recv()
                passed[j * n + a].start()

    def end():
        for a in range(n):
            copy(a, 0, sibling, me).wait_recv()
        for j, chip in enumerate(chips):
            for a in range(n):
                copy(a, 4 + j, (*chip, 1 - c), me).wait_recv()
        for cp in first + passed:
            cp.wait_send()
        for cp in mine:
            cp.wait()

    return start, middle, end


def _in_proj_gather(h, w_shard):
    half = S // 2
    slab = (D, SHARD_COLS)

    def body(h_hbm, w_hbm, z_hbm, wfull_hbm, h_buf, land, zstage,
             h_sem, own_sem, z_sem, wout_sem, send_sems, recv_sems):
        x, y, c = _mesh_pos()
        sibling = (x, y, 1 - c)
        north = c == 1

        def chips_of(first_x):
            near = (jnp.where(first_x, 1 - x, x), jnp.where(first_x, y, 1 - y))
            far = (jnp.where(first_x, x, 1 - x), jnp.where(first_x, 1 - y, y))
            return [near, far, (1 - x, 1 - y)]

        mine, theirs = chips_of(north), chips_of(jnp.logical_not(north))

        def dev(chip, core):
            return 4 * chip[0] + 2 * chip[1] + core

        block_of = ([dev((x, y), c), dev((x, y), 1 - c)] + [dev(q, c) for q in mine]
                    + [dev(q, 1 - c) for q in theirs])

        def cols(d):
            if isinstance(d, int):
                return pl.ds(d * SHARD_COLS, SHARD_COLS)
            return pl.ds(pl.multiple_of(d * SHARD_COLS, LANES), SHARD_COLS)

        def send(k, src, dst_slot, to):
            return pltpu.make_async_remote_copy(
                src_ref=src, dst_ref=land.at[dst_slot], send_sem=send_sems.at[k], recv_sem=recv_sems.at[k],
                device_id=to, device_id_type=MESH)

        def to_sibling():
            return send(0, w_hbm, 1, sibling)

        def to_chip(j):
            if j == 2:
                return send(3, land.at[2], 4, (*mine[1], c))
            return send(1 + j, w_hbm, 2 + j, (*mine[j], c))

        def pass_on(j):
            return send(4 + j, land.at[2 + j], 5 + j, sibling)

        own = _SplitCopy(w_hbm, land.at[0], own_sem)
        h_copy = _SplitCopy(h_hbm, h_buf, h_sem)
        own.start()
        h_copy.start()
        to_sibling().start()
        to_chip(0).start()
        h_copy.wait()
        own.wait()

        def multiply(slot, n_done):
            d = block_of[slot]
            out = _SplitCopy(land.at[slot], wfull_hbm.at[:, cols(d)], wout_sem.at[slot])
            out.start()
            for r in range(2):
                rows = pl.ds(r * half, half)
                zc = _SplitCopy(zstage.at[r], z_hbm.at[rows, cols(d)], z_sem.at[r])
                if n_done > 0:
                    zc.wait()
                zstage[r] = _dot(h_buf[r * half:(r + 1) * half, :], land[slot])
                zc.start()
            return out

        outs = [multiply(0, 0)]
        to_sibling().wait_recv()
        outs.append(multiply(1, 1))
        done = 2
        for j in range(3):
            to_chip(j).wait_recv()
            pass_on(j).start()
            to_chip(j).wait_send()
            if j < 2:
                to_chip(j + 1).start()
            outs.append(multiply(2 + j, done))
            pass_on(j).wait_recv()
            outs.append(multiply(5 + j, done + 1))
            done += 2
        for r in range(2):
            _SplitCopy(zstage.at[r], z_hbm.at[pl.ds(r * half, half), cols(0)], z_sem.at[r]).wait()
        for out in outs:
            out.wait()
        to_sibling().wait_send()
        for j in range(3):
            pass_on(j).wait_send()

    any_spec = pl.BlockSpec(memory_space=pl.ANY)
    return pl.pallas_call(
        body, name="in_proj_gather",
        out_shape=[jax.ShapeDtypeStruct((S, IN_COLS), F32), jax.ShapeDtypeStruct((D, IN_COLS), BF16)],
        in_specs=[any_spec] * 2, out_specs=[any_spec] * 2,
        scratch_shapes=[pltpu.VMEM((S, D), BF16), pltpu.VMEM((8,) + slab, BF16), pltpu.VMEM((2, half, SHARD_COLS), F32),
                        pltpu.SemaphoreType.DMA, pltpu.SemaphoreType.DMA, pltpu.SemaphoreType.DMA((2,)),
                        pltpu.SemaphoreType.DMA((8,)), pltpu.SemaphoreType.DMA((7,)), pltpu.SemaphoreType.DMA((7,))],
        compiler_params=_cp(),
    )(h, w_shard)


def _exchange_sibling(ids, gb):
    n = len(gb)

    def body(*refs):
        ins, outs = refs[:n], refs[n:2 * n]
        send_sems, recv_sems = refs[2 * n:]
        x, y, c = _mesh_pos()
        sibling = (x, y, 1 - c)
        copies = []
        for i, a in enumerate(ids):
            for q in range(4):
                copies.append(pltpu.make_async_remote_copy(
                    src_ref=_shard_of(ins[i], a, 2 * q + (1 - c)), dst_ref=outs[i].at[q],
                    send_sem=send_sems.at[i * 4 + q], recv_sem=recv_sems.at[i * 4 + q],
                    device_id=sibling, device_id_type=MESH))
        for cp in copies:
            cp.start()
        for cp in copies:
            cp.wait()

    any_spec = pl.BlockSpec(memory_space=pl.ANY)
    return pl.pallas_call(
        body, name="grads_to_sibling",
        out_shape=[jax.ShapeDtypeStruct((4,) + SHARD_SHAPES[a], BF16) for a in ids],
        in_specs=[any_spec] * n, out_specs=[any_spec] * n,
        scratch_shapes=[pltpu.SemaphoreType.DMA((4 * n,)), pltpu.SemaphoreType.DMA((4 * n,))],
    )(*gb)


def _exchange_chips_steps(ins, outs, send_sems, recv_sems):
    x, y, c = _mesh_pos()
    chips = [(1 - x, y), (x, 1 - y), (1 - x, 1 - y)]
    copies = []
    for a in range(len(ins)):
        for k, chip in enumerate(chips):
            copies.append(pltpu.make_async_remote_copy(
                src_ref=ins[a].at[2 * chip[0] + chip[1]], dst_ref=outs[a].at[k],
                send_sem=send_sems.at[a * 3 + k], recv_sem=recv_sems.at[a * 3 + k],
                device_id=(*chip, c), device_id_type=MESH))

    def start():
        for cp in copies:
            cp.start()

    def end():
        for cp in copies:
            cp.wait()

    return start, end


def _gather_small(small):
    def body(small_ref, small_out, ssend, srecv, local_sem):
        x, y, c = _mesh_pos()
        me = 4 * x + 2 * y + c
        copies = []
        for r in range(1, NDEV):
            peer = (1 - x if r & 4 else x, 1 - y if r & 2 else y, 1 - c if r & 1 else c)
            copies.append(pltpu.make_async_remote_copy(
                src_ref=small_ref, dst_ref=small_out.at[me],
                send_sem=ssend.at[r - 1], recv_sem=srecv.at[r - 1],
                device_id=peer, device_id_type=MESH))
        own = pltpu.make_async_copy(small_ref, small_out.at[me], local_sem)
        own.start()
        for cp in copies:
            cp.start()
        for cp in copies:
            cp.wait()
        own.wait()

    any_spec = pl.BlockSpec(memory_space=pl.ANY)
    return pl.pallas_call(
        body, name="gather_small",
        out_shape=jax.ShapeDtypeStruct((NDEV,) + small.shape, F32),
        in_specs=[any_spec], out_specs=any_spec,
        scratch_shapes=[pltpu.SemaphoreType.DMA((NDEV - 1,)), pltpu.SemaphoreType.DMA((NDEV - 1,)),
                        pltpu.SemaphoreType.DMA],
    )(small)


def _shard_tiles(a):
    rows, cols = SHARD_SHAPES[a]
    tr = min(rows, 256)
    return (tr, cols), rows // tr


def _full_index(a, d, i):
    (tr, _), nt = _shard_tiles(a)
    if a in (0, 2):
        return (i, d)
    return (d * nt + i, 0)


def _cast_bf16(x, name):
    rows, cols = x.shape
    tr = min(rows, 256)

    def body(x_ref, o_ref):
        o_ref[...] = x_ref[...].astype(BF16)

    return pl.pallas_call(
        body, name=name, out_shape=jax.ShapeDtypeStruct(x.shape, BF16), grid=(rows // tr,),
        in_specs=[pl.BlockSpec((tr, cols), lambda i: (i, 0))],
        out_specs=pl.BlockSpec((tr, cols), lambda i: (i, 0)),
        compiler_params=_cp(("parallel",)),
    )(x)


def _chip_partials(a, g_full, r1, core):
    tile, nt = _shard_tiles(a)

    def body(c_ref, g_ref, r_ref, o_ref):
        o_ref[0] = (g_ref[...] + r_ref[0].astype(F32)).astype(BF16)

    grid_spec = pltpu.PrefetchScalarGridSpec(
        num_scalar_prefetch=1, grid=(4, nt),
        in_specs=[pl.BlockSpec(tile, lambda q, i, c: _full_index(a, 2 * q + c[0], i)),
                  pl.BlockSpec((1,) + tile, lambda q, i, c: (q, i, 0))],
        out_specs=pl.BlockSpec((1,) + tile, lambda q, i, c: (q, i, 0)))
    return pl.pallas_call(
        body, name=f"chip_partials_{a}", grid_spec=grid_spec,
        out_shape=jax.ShapeDtypeStruct((4,) + SHARD_SHAPES[a], BF16),
        compiler_params=_cp(("parallel", "parallel")),
    )(core, g_full, r1)


def _adam(w, g, m, v):
    m = ADAM_B1 * m + (1.0 - ADAM_B1) * g
    v = ADAM_B2 * v + (1.0 - ADAM_B2) * (g * g)
    m_hat = m / (1.0 - ADAM_B1 ** ADAM_STEP)
    v_hat = v / (1.0 - ADAM_B2 ** ADAM_STEP)
    delta = -ADAM_LR * (m_hat / (jnp.sqrt(v_hat) + ADAM_EPS) + ADAM_WD * w)
    return delta, m, v


def _reduce_and_update(a, w, m, v, g_full, r1, r2, pos):
    tile, nt = _shard_tiles(a)

    def body(p_ref, w_ref, m_ref, v_ref, g_ref, r1_ref, r2_ref, go_ref, do_ref, mo_ref, vo_ref):
        g = g_ref[...] + r1_ref[0].astype(F32)
        g = g + r2_ref[0].astype(F32)
        g = g + r2_ref[1].astype(F32)
        g = g + r2_ref[2].astype(F32)
        delta, m_new, v_new = _adam(w_ref[...], g, m_ref[...], v_ref[...])
        go_ref[...] = g
        do_ref[...] = delta
        mo_ref[...] = m_new
        vo_ref[...] = v_new

    own = pl.BlockSpec(tile, lambda i, p: (i, 0))
    grid_spec = pltpu.PrefetchScalarGridSpec(
        num_scalar_prefetch=1, grid=(nt,),
        in_specs=[own, own, own,
                  pl.BlockSpec(tile, lambda i, p: _full_index(a, p[0], i)),
                  pl.BlockSpec((1,) + tile, lambda i, p: (p[1], i, 0)),
                  pl.BlockSpec((3,) + tile, lambda i, p: (0, i, 0))],
        out_specs=[own] * 4)
    shp = jax.ShapeDtypeStruct(w.shape, F32)
    return pl.pallas_call(
        body, name=f"reduce_update_{a}", grid_spec=grid_spec, out_shape=[shp] * 4,
        compiler_params=_cp(("parallel",)),
    )(pos, w, m, v, g_full, r1, r2)


def _reduce_own_and_update(w, m, v, g_chip, r2):
    tile, nt = _shard_tiles(0)

    def body(w_ref, m_ref, v_ref, g_ref, r2_ref, go_ref, do_ref, mo_ref, vo_ref):
        g = g_ref[...] + r2_ref[0].astype(F32)
        g = g + r2_ref[1].astype(F32)
        delta, m_new, v_new = _adam(w_ref[...], g, m_ref[...], v_ref[...])
        go_ref[...] = g
        do_ref[...] = delta
        mo_ref[...] = m_new
        vo_ref[...] = v_new

    own = pl.BlockSpec(tile, lambda i: (i, 0))
    shp = jax.ShapeDtypeStruct(w.shape, F32)
    return pl.pallas_call(
        body, name="reduce_update_0", grid=(nt,), out_shape=[shp] * 4,
        in_specs=[own, own, own, own, pl.BlockSpec((2,) + tile, lambda i: (0, i, 0))], out_specs=[own] * 4,
        compiler_params=_cp(("parallel",)),
    )(w, m, v, g_chip, r2)


def _small_update(gathered, norm_w, lb_logits, hnw, fnw, moments):
    m_nw, m_lb, m_hn, m_fn, v_nw, v_lb, v_hn, v_fn = moments

    def body(g_ref, nw, lb, hn, fn, mnw, mlb, mhn, mfn, vnw, vlb, vhn, vfn,
             loss_o, g_nw, g_lb, g_hn, g_fn, d_nw, d_lb, d_hn, d_fn,
             mo_nw, mo_lb, mo_hn, mo_fn, vo_nw, vo_lb, vo_hn, vo_fn):
        tot = g_ref[0]
        for d in range(1, NDEV):
            tot = tot + g_ref[d]
        loss_o[...] = tot[4:5, 0:LANES]
        logits = lb[...]
        lbv = jax.nn.sigmoid(logits[0:1] - logits[1:2])
        chain = tot[1:2] * lbv * (1.0 - lbv)
        grads = (tot[0:1], jnp.concatenate([chain, -chain], axis=0), tot[2:3, 0:LANES], tot[3:4])
        outs = ((nw, mnw, vnw, g_nw, d_nw, mo_nw, vo_nw), (lb, mlb, vlb, g_lb, d_lb, mo_lb, vo_lb),
                (hn, mhn, vhn, g_hn, d_hn, mo_hn, vo_hn), (fn, mfn, vfn, g_fn, d_fn, mo_fn, vo_fn))
        for g, (w_r, m_r, v_r, g_o, d_o, m_o, v_o) in zip(grads, outs):
            delta, m_new, v_new = _adam(w_r[...], g, m_r[...], v_r[...])
            g_o[...] = g
            d_o[...] = delta
            m_o[...] = m_new
            v_o[...] = v_new

    shapes = [norm_w.shape, lb_logits.shape, hnw.shape, fnw.shape]
    out_shape = [jax.ShapeDtypeStruct((1, LANES), F32)] + [jax.ShapeDtypeStruct(s, F32) for s in shapes] * 4
    return pl.pallas_call(body, name="small_update", out_shape=out_shape, compiler_params=_cp())(
        gathered, norm_w, lb_logits, hnw, fnw, m_nw, m_lb, m_hn, m_fn, v_nw, v_lb, v_hn, v_fn)


def _rmsnorm_in(x, norm_w):
    tr = 512

    def body(x_ref, w_ref, h_ref):
        xv = x_ref[...]
        r = lax.rsqrt(jnp.mean(xv * xv, axis=-1, keepdims=True) + EPS)
        h_ref[...] = (xv * r * w_ref[...]).astype(BF16)

    return pl.pallas_call(
        body, name="rmsnorm_in", out_shape=jax.ShapeDtypeStruct((S, D), BF16), grid=(S // tr,),
        in_specs=[pl.BlockSpec((tr, D), lambda i: (i, 0)), pl.BlockSpec((1, D), lambda i: (0, 0))],
        out_specs=pl.BlockSpec((tr, D), lambda i: (i, 0)),
        compiler_params=_cp(("parallel",)),
    )(x, norm_w)


def _in_proj(h, w_in):
    tn = 1024

    def body(h_ref, w_ref, z_ref):
        z_ref[...] = _dot(h_ref[...], w_ref[...])

    return pl.pallas_call(
        body, name="in_proj", out_shape=jax.ShapeDtypeStruct((S, IN_COLS), F32), grid=(IN_COLS // tn,),
        in_specs=[pl.BlockSpec((S, D), lambda j: (0, 0)), pl.BlockSpec((D, tn), lambda j: (0, j))],
        out_specs=pl.BlockSpec((S, tn), lambda j: (0, j)),
        compiler_params=_cp(("parallel",)),
    )(h, w_in)


def _block_tri(n, block, upper=False):
    r = lax.broadcasted_iota(jnp.int32, (n, n), 0)
    c = lax.broadcasted_iota(jnp.int32, (n, n), 1)
    keep = (c >= r) if upper else (c <= r)
    return jnp.where(keep & ((r // block) == (c // block)), 1.0, 0.0).astype(BF16)


def _tril_mask(n):
    r = lax.broadcasted_iota(jnp.int32, (n, n), 0)
    c = lax.broadcasted_iota(jnp.int32, (n, n), 1)
    return c <= r


def _chunk_scores(q, k, b, bex, r0, mask):
    parts, qs_l, ks_l, ek_l, eq_l = [], [], [], [], []
    for i in range(CHUNK // SUB):
        ri = slice(r0 + SUB * i, r0 + SUB * (i + 1))
        seen = slice(r0, r0 + SUB * (i + 1))
        base = bex[r0 + SUB * i:r0 + SUB * i + 1]
        eq = jnp.exp(b[ri] - base)
        ek = jnp.exp(jnp.minimum(base - b[seen], EXP_CLAMP))
        ks = k[seen] * ek
        if i + 1 < CHUNK // SUB:
            rest = jnp.zeros((CHUNK - SUB * (i + 1), 128), F32)
            ek, ks = jnp.concatenate([ek, rest], axis=0), jnp.concatenate([ks, rest], axis=0)
        qs = q[ri] * eq
        parts.append(_dot_nt(qs.astype(BF16), ks.astype(BF16)))
        qs_l.append(qs)
        ks_l.append(ks)
        ek_l.append(ek)
        eq_l.append(eq)
    return jnp.where(mask, jnp.concatenate(parts, axis=0), 0.0), qs_l, ks_l, ek_l, eq_l


def _hgrn_cols(hq, hf, hi, lb):
    sg = jax.nn.sigmoid(hf)
    f = lb + (1.0 - lb) * sg
    g = jnp.log(f)
    b = _dot_ones(_block_tri(HBLK, CHUNK), g)
    return _silu(hq), 1.0 - f, g, hi, sg, f, b


GATHER_IDS = (1, 2, 3)


def _hgrn_fwd(z, lbv, hnw, shards):
    ntb, nch = S // HBLK, HBLK // CHUNK
    n = len(GATHER_IDS)

    def body(hq_ref, hf_ref, hi_ref, hg_ref, lb_ref, hnw_ref, s0, s1, s2, o_ref, oa_ref, st_ref, f0, f1, f2,
             state, send_sems, recv_sems, local_sems):
        start, middle, end = _allgather_steps(GATHER_IDS, (s0, s1, s2), (f0, f1, f2), send_sems, recv_sems, local_sems)

        @pl.when(pl.program_id(0) == 0)
        def _():
            state[...] = jnp.zeros_like(state)
            start()

        pl.when(pl.program_id(0) == ntb // 2)(middle)

        q_a, k_a, g_a, v_a, _, _, b_a = _hgrn_cols(hq_ref[...], hf_ref[...], hi_ref[...], lb_ref[...])
        bex_a = b_a - g_a
        eb_a = jnp.exp(b_a)
        mask = _tril_mask(CHUNK)
        hg = hg_ref[...]
        w = hnw_ref[...]
        for h in range(HEADS):
            cols = slice(128 * h, 128 * h + 128)
            q, k, v, b, bex, eb = q_a[:, cols], k_a[:, cols], v_a[:, cols], b_a[:, cols], bex_a[:, cols], eb_a[:, cols]
            st = state[h]
            outs = []
            for c in range(nch):
                r0 = c * CHUNK
                rows = slice(r0, r0 + CHUNK)
                a = _chunk_scores(q, k, b, bex, r0, mask)[0]
                vb = v[rows].astype(BF16)
                b_last = b[r0 + CHUNK - 1:r0 + CHUNK]
                qe = (q[rows] * eb[rows]).astype(BF16)
                outs.append(_dot(a.astype(BF16), vb) + _dot_nt(qe, st.astype(BF16)))
                st_ref[h, c] = st
                ke = (k[rows] * jnp.exp(b_last - b[rows])).astype(BF16)
                st = st * jnp.exp(b_last) + _dot_tn(vb, ke)
            state[h] = st
            o = jnp.concatenate(outs, axis=0)
            o_ref[:, cols] = o
            r = lax.rsqrt(jnp.mean(o * o, axis=-1, keepdims=True) + EPS)
            oa_ref[:, cols] = (o * r * w * _silu(hg[:, cols])).astype(BF16)

        pl.when(pl.program_id(0) == ntb - 1)(end)

    def zcol(j):
        return pl.BlockSpec((HBLK, D), lambda t: (t, j))

    out_blk = pl.BlockSpec((HBLK, D), lambda t: (t, 0))
    any_spec = pl.BlockSpec(memory_space=pl.ANY)
    return pl.pallas_call(
        body, name="hgrn_fwd", grid=(ntb,),
        out_shape=[jax.ShapeDtypeStruct((S, D), F32), jax.ShapeDtypeStruct((S, D), BF16),
                   jax.ShapeDtypeStruct((HEADS, S // CHUNK, 128, 128), F32)]
        + [jax.ShapeDtypeStruct(FULL_SHAPES[a], BF16) for a in GATHER_IDS],
        in_specs=[zcol(0), zcol(1), zcol(2), zcol(3),
                  pl.BlockSpec((1, D), lambda t: (0, 0)), pl.BlockSpec((1, 128), lambda t: (0, 0))] + [any_spec] * n,
        out_specs=[out_blk, out_blk, pl.BlockSpec((HEADS, nch, 128, 128), lambda t: (0, t, 0, 0))] + [any_spec] * n,
        scratch_shapes=[pltpu.VMEM((HEADS, 128, 128), F32), pltpu.SemaphoreType.DMA((7 * n,)),
                        pltpu.SemaphoreType.DMA((7 * n,)), pltpu.SemaphoreType.DMA((n,))],
        compiler_params=_cp(("arbitrary",)),
    )(z, z, z, z, lbv, hnw, *shards)


def _half_mask():
    lane = lax.broadcasted_iota(jnp.int32, (1, LANES), 1)
    return (lane % 64) < 32


def _rope(t, cc, ss, first_half):
    partner = jnp.where(first_half, pltpu.roll(t, 96, 1), pltpu.roll(t, 32, 1))
    return t * cc + partner * ss


def _attn_masks():
    i = lax.broadcasted_iota(jnp.int32, (128, 128), 0)
    j = lax.broadcasted_iota(jnp.int32, (128, 128), 1)
    return j >= i, j <= i


def _to_residues_dyn(g, dst, src, row0=0, dtype=None):
    for gi, dil in enumerate((1, 4, 16)):
        m = S // dil

        @pl.when(g == gi)
        def _(dil=dil, m=m):
            for r in range(dil):
                v = src[...] if dil == 1 else src[pl.ds(r, m, stride=dil), :]
                if dtype is not None:
                    v = v.astype(dtype)
                dst[row0 + r * m:row0 + (r + 1) * m, 0:LANES] = v


def _from_residues_dyn(g, dst, src, row0=0):
    for gi, dil in enumerate((1, 4, 16)):
        m = S // dil

        @pl.when(g == gi)
        def _(dil=dil, m=m):
            for r in range(dil):
                v = src[row0 + r * m:row0 + (r + 1) * m, :]
                if dil == 1:
                    dst[...] = v
                else:
                    dst[pl.ds(r, m, stride=dil), :] = v


def _group_blocks(g):
    return jnp.where(g == 0, 16, jnp.where(g == 1, 4, 1))


def _attn_in_specs(extra):
    def zcol(off):
        return pl.BlockSpec((S, LANES), lambda p, g: (0, off + 4 * g + p))

    per_pair = pl.BlockSpec((S, LANES), lambda p, g: (0, p))
    const = pl.BlockSpec((S, LANES), lambda p, g: (0, 0))
    return [zcol(32), zcol(44), zcol(56), pl.BlockSpec((S, LANES), lambda p, g: (0, 68 + p)), const, const] + [per_pair] * extra


def _attn_fwd(z, cc, ss):
    def body(q_ref, k_ref, v_ref, ag_ref, cc_ref, ss_ref, ob_ref, lse_ref, obg_ref,
             tmp, qs, ks, vx, og, mg, lg, o_t, m_t, l_t, o_acc, m_acc, l_acc):
        g = pl.program_id(1)
        first_half = _half_mask()
        prev_ok, cur_ok = _attn_masks()
        lane = lax.broadcasted_iota(jnp.int32, (1, LANES), 1)
        heads = (lane < 64, lane >= 64)
        nblk = _group_blocks(g)

        @pl.when(g == 0)
        def _():
            ks[0:ATT_PAD, :] = jnp.zeros((ATT_PAD, LANES), BF16)
            vx[0:ATT_PAD, 0:LANES] = jnp.zeros((ATT_PAD, LANES), BF16)
            vx[:, LANES:2 * LANES] = jnp.ones((ATT_PAD + S, LANES), BF16)

        tmp[...] = _rope(q_ref[...], cc_ref[...], ss_ref[...], first_half) * ATT_SCALE
        _to_residues_dyn(g, qs, tmp)
        tmp[...] = _rope(k_ref[...], cc_ref[...], ss_ref[...], first_half)
        _to_residues_dyn(g, ks, tmp, ATT_PAD, BF16)
        _to_residues_dyn(g, vx, v_ref, ATT_PAD, BF16)

        def unit(u, carry):
            start = pl.multiple_of(u * 128, 128)
            cur = pl.ds(start, 128)
            pm = prev_ok & ((u & (nblk - 1)) != 0)
            qu = qs[cur, :]
            kcat = ks[pl.ds(start, 256), :]
            vext = vx[pl.ds(start, 256), :]
            o_u = m_u = l_u = None
            for hh in range(2):
                s = _dot_nt(jnp.where(heads[hh], qu, 0.0).astype(BF16), kcat)
                sp = jnp.where(pm, s[:, 0:128], -jnp.inf)
                sc = jnp.where(cur_ok, s[:, 128:256], -jnp.inf)
                m = jnp.max(jnp.maximum(sp, sc), axis=-1, keepdims=True)
                p = jnp.concatenate([jnp.exp(sp - m), jnp.exp(sc - m)], axis=1).astype(BF16)
                ol = _dot(p, vext)
                mb = jnp.broadcast_to(m, (128, LANES))
                if hh == 0:
                    o_u, l_u, m_u = ol[:, 0:128], ol[:, 128:256], mb
                else:
                    o_u = jnp.where(heads[1], ol[:, 0:128], o_u)
                    l_u = jnp.where(heads[1], ol[:, 128:256], l_u)
                    m_u = jnp.where(heads[1], mb, m_u)
            og[cur, :] = o_u
            mg[cur, :] = m_u
            lg[cur, :] = l_u
            return carry

        lax.fori_loop(0, 16, unit, 0, unroll=ATT_UNROLL)
        _from_residues_dyn(g, o_t, og)
        _from_residues_dyn(g, m_t, mg)
        _from_residues_dyn(g, l_t, lg)

        @pl.when(g == 0)
        def _():
            o_acc[...] = o_t[...]
            m_acc[...] = m_t[...]
            l_acc[...] = l_t[...]

        @pl.when(g > 0)
        def _():
            m_new = jnp.maximum(m_acc[...], m_t[...])
            wa, wb = jnp.exp(m_acc[...] - m_new), jnp.exp(m_t[...] - m_new)
            o_acc[...] = o_acc[...] * wa + o_t[...] * wb
            l_acc[...] = l_acc[...] * wa + l_t[...] * wb
            m_acc[...] = m_new

        @pl.when(g == 2)
        def _():
            ob = o_acc[...] / l_acc[...]
            ob_ref[...] = ob
            lse_ref[...] = m_acc[...] + jnp.log(l_acc[...])
            obg_ref[...] = (ob * _silu(ag_ref[...])).astype(BF16)

    blk = pl.BlockSpec((S, LANES), lambda p, g: (0, p))
    buf = pltpu.VMEM((S, LANES), F32)
    return pl.pallas_call(
        body, name="attn_fwd", grid=(4, 3),
        out_shape=[jax.ShapeDtypeStruct((S, 512), F32), jax.ShapeDtypeStruct((S, 512), F32),
                   jax.ShapeDtypeStruct((S, 512), BF16)],
        in_specs=_attn_in_specs(0), out_specs=[blk, blk, blk],
        scratch_shapes=[buf, buf, pltpu.VMEM((ATT_PAD + S, LANES), BF16), pltpu.VMEM((ATT_PAD + S, 2 * LANES), BF16)] + [buf] * 9,
        compiler_params=_cp(("parallel", "arbitrary")),
    )(z, z, z, z, cc, ss)


def _tail(x, o_a, o_bg, z, target, w_a, w_b, w_out, fnw):
    tm = 256

    def body(x_ref, oa_ref, ob_ref, gpa_ref, gpb_ref, t_ref, wa_ref, wb_ref, wo_ref, fnw_ref,
             dx2_ref, dx2b_ref, dgp_ref, doa_ref, dob_ref, mg_ref, dya_ref, dyb_ref, small_ref):
        @pl.when(pl.program_id(0) == 0)
        def _():
            small_ref[...] = jnp.zeros_like(small_ref)

        wa, wb, wo = wa_ref[...], wb_ref[...], wo_ref[...]
        y_a = _dot(oa_ref[...], wa)
        y_b = _dot(ob_ref[...], wb)
        ga = jax.nn.sigmoid(gpa_ref[...])
        gb = jax.nn.sigmoid(gpb_ref[...])
        merged = (ga * y_a + gb * y_b).astype(BF16)
        x2 = x_ref[...] + _dot(merged, wo)
        r2 = lax.rsqrt(jnp.mean(x2 * x2, axis=-1, keepdims=True) + EPS)
        n2 = x2 * r2
        fw = fnw_ref[...]
        err = n2 * fw - t_ref[...]
        loss = 0.5 * jnp.sum(jnp.sum(err * err, axis=-1, keepdims=True), axis=0, keepdims=True) / D
        dy = err * (1.0 / D)
        g_fnw = jnp.sum(dy * n2, axis=0, keepdims=True)
        dn = dy * fw
        dx2 = r2 * (dn - n2 * jnp.mean(dn * n2, axis=-1, keepdims=True))
        dx2b = dx2.astype(BF16)
        dmerged = _dot_nt(dx2b, wo)
        dy_a = (dmerged * ga).astype(BF16)
        dy_b = (dmerged * gb).astype(BF16)
        dx2_ref[...] = dx2
        dx2b_ref[...] = dx2b
        dgp_ref[:, 0:D] = (dmerged * y_a * ga * (1.0 - ga)).astype(BF16)
        dgp_ref[:, D:2 * D] = (dmerged * y_b * gb * (1.0 - gb)).astype(BF16)
        doa_ref[...] = _dot_nt(dy_a, wa)
        dob_ref[...] = _dot_nt(dy_b, wb)
        mg_ref[...] = merged
        dya_ref[...] = dy_a
        dyb_ref[...] = dy_b
        small_ref[0:1, :] += g_fnw
        small_ref[1:2, :] += jnp.broadcast_to(loss, (1, D))

    def rows(cols, off=0):
        return pl.BlockSpec((tm, cols), lambda i: (i, off))

    def whole(shape):
        return pl.BlockSpec(shape, lambda i: (0, 0))

    return pl.pallas_call(
        body, name="tail", grid=(S // tm,),
        out_shape=[jax.ShapeDtypeStruct((S, D), F32), jax.ShapeDtypeStruct((S, D), BF16),
                   jax.ShapeDtypeStruct((S, 2 * D), BF16), jax.ShapeDtypeStruct((S, D), F32),
                   jax.ShapeDtypeStruct((S, 512), F32), jax.ShapeDtypeStruct((S, D), BF16),
                   jax.ShapeDtypeStruct((S, D), BF16), jax.ShapeDtypeStruct((S, D), BF16),
                   jax.ShapeDtypeStruct((8, D), F32)],
        in_specs=[rows(D), rows(D), rows(512), rows(D, 9), rows(D, 10), rows(D),
                  whole((D, D)), whole((512, D)), whole((D, D)), whole((1, D))],
        out_specs=[rows(D), rows(D), rows(2 * D), rows(D), rows(512), rows(D), rows(D), rows(D), whole((8, D))],
        compiler_params=_cp(("arbitrary",)),
    )(x, o_a, o_bg, z, z, target, w_a, w_b, w_out, fnw)


def _tn_matmul(a, b, name):
    m, n = a.shape[1], b.shape[1]
    tn = 512

    def body(a_ref, b_ref, o_ref, ob_ref):
        acc = _dot_tn(a_ref[...], b_ref[...])
        o_ref[...] = acc
        ob_ref[...] = acc.astype(BF16)

    out_blk = pl.BlockSpec((m, tn), lambda j: (0, j))
    return pl.pallas_call(
        body, name=name, grid=(n // tn,),
        out_shape=[jax.ShapeDtypeStruct((m, n), F32), jax.ShapeDtypeStruct((m, n), BF16)],
        in_specs=[pl.BlockSpec((S, m), lambda j: (0, 0)), pl.BlockSpec((S, tn), lambda j: (0, j))],
        out_specs=[out_blk, out_blk],
        compiler_params=_cp(("parallel",)),
    )(a, b)


def _hgrn_bwd(z, o, do_a, states, lbv, hnw, partials):
    ntb, nch = S // HBLK, HBLK // CHUNK
    n = len(GATHER_IDS)

    def body(hq_ref, hf_ref, hi_ref, hg_ref, o_ref, doa_ref, st_ref, lb_ref, hnw_ref, p0, p1, p2,
             dhq_ref, dhf_ref, dhi_ref, dhg_ref, glb_ref, ghn_ref, e0, e1, e2, dstate, send_sems, recv_sems):
        start, end = _exchange_chips_steps((p0, p1, p2), (e0, e1, e2), send_sems, recv_sems)

        @pl.when(pl.program_id(0) == 0)
        def _():
            dstate[...] = jnp.zeros_like(dstate)
            glb_ref[...] = jnp.zeros_like(glb_ref)
            ghn_ref[...] = jnp.zeros_like(ghn_ref)
            start()

        lb_a = lb_ref[...]
        hq_a, hg_a = hq_ref[...], hg_ref[...]
        q_a, k_a, g_a, v_a, sg_a, f_a, b_a = _hgrn_cols(hq_a, hf_ref[...], hi_ref[...], lb_a)
        bex_a = b_a - g_a
        eb_a = jnp.exp(b_a)
        w = hnw_ref[...]
        mask = _tril_mask(CHUNK)
        upper = _block_tri(CHUNK, CHUNK, upper=True)
        for h in range(HEADS):
            cols = slice(128 * h, 128 * h + 128)
            q, k, v, b, bex, eb = q_a[:, cols], k_a[:, cols], v_a[:, cols], b_a[:, cols], bex_a[:, cols], eb_a[:, cols]
            hq, hg, sg, f, lb = hq_a[:, cols], hg_a[:, cols], sg_a[:, cols], f_a[:, cols], lb_a[:, cols]
            ov, doa = o_ref[:, cols], doa_ref[:, cols]
            r = lax.rsqrt(jnp.mean(ov * ov, axis=-1, keepdims=True) + EPS)
            n = ov * r
            sil = _silu(hg)
            dhg_ref[:, cols] = (doa * n * w * _dsilu(hg)).astype(BF16)
            ghn_ref[h] += jnp.sum(doa * sil * n, axis=0, keepdims=True)
            dn = doa * sil * w
            do = r * (dn - n * jnp.mean(dn * n, axis=-1, keepdims=True))

            dst = dstate[h]
            dq_l, dk_l, dv_l, dg_l = [None] * nch, [None] * nch, [None] * nch, [None] * nch
            for c in reversed(range(nch)):
                r0 = c * CHUNK
                rows = slice(r0, r0 + CHUNK)
                st = st_ref[h, c]
                bc, kc, qc = b[rows], k[rows], q[rows]
                vb, dob = v[rows].astype(BF16), do[rows].astype(BF16)
                b_last = bc[CHUNK - 1:CHUNK]
                e_last = jnp.exp(b_last)
                ekl = jnp.exp(b_last - bc)
                dstb = dst.astype(BF16)
                a, qs_l, ks_l, ek_l, eq_l = _chunk_scores(q, k, b, bex, r0, mask)
                da = jnp.where(mask, _dot_nt(dob, vb), 0.0)
                dv_l[c] = _dot_tn(a.astype(BF16), dob) + _dot_nt((kc * ekl).astype(BF16), dstb)
                dq_inter = _dot(dob, st.astype(BF16)) * eb[rows]
                dk_state = _dot(vb, dstb) * ekl
                dq_parts, dk_intra = [], jnp.zeros((CHUNK, 128), F32)
                dab = da.astype(BF16)
                for i in range(CHUNK // SUB):
                    da_i = dab[SUB * i:SUB * (i + 1)]
                    ks_hi, ks_lo = _split2(ks_l[i])
                    qs_hi, qs_lo = _split2(qs_l[i])
                    dq_parts.append((_dot(da_i, ks_hi) + _dot(da_i, ks_lo)) * eq_l[i])
                    dk_intra = dk_intra + (_dot_tn(da_i, qs_hi) + _dot_tn(da_i, qs_lo)) * ek_l[i]
                dq = jnp.concatenate(dq_parts, axis=0) + dq_inter
                dk = dk_intra + dk_state
                last = (e_last * jnp.sum(st * dst, axis=0, keepdims=True)
                        + jnp.sum(kc * dk_state, axis=0, keepdims=True))
                dg_l[c] = _dot_ones(upper, qc * dq - kc * dk) + last
                dq_l[c], dk_l[c] = dq, dk
                dst = dst * e_last + _dot_tn(dob, (qc * eb[rows]).astype(BF16))
            dstate[h] = dst
            dq, dk = jnp.concatenate(dq_l, axis=0), jnp.concatenate(dk_l, axis=0)
            dg, dv = jnp.concatenate(dg_l, axis=0), jnp.concatenate(dv_l, axis=0)
            dhq_ref[:, cols] = (dq * _dsilu(hq)).astype(BF16)
            dhi_ref[:, cols] = dv.astype(BF16)
            df = dg / f - dk
            dhf_ref[:, cols] = (df * (1.0 - lb) * sg * (1.0 - sg)).astype(BF16)
            glb_ref[:, cols] += jnp.sum(df * (1.0 - sg), axis=0, keepdims=True)

        pl.when(pl.program_id(0) == ntb - 1)(end)

    def rev(t):
        return ntb - 1 - t

    def zcol(j):
        return pl.BlockSpec((HBLK, D), lambda t: (rev(t), j))

    blk = pl.BlockSpec((HBLK, D), lambda t: (rev(t), 0))
    any_spec = pl.BlockSpec(memory_space=pl.ANY)
    return pl.pallas_call(
        body, name="hgrn_bwd", grid=(ntb,),
        out_shape=[jax.ShapeDtypeStruct((S, D), BF16)] * 4
        + [jax.ShapeDtypeStruct((1, D), F32), jax.ShapeDtypeStruct((HEADS, 1, 128), F32)]
        + [jax.ShapeDtypeStruct((3,) + SHARD_SHAPES[a], BF16) for a in GATHER_IDS],
        in_specs=[zcol(0), zcol(1), zcol(2), zcol(3), blk, blk,
                  pl.BlockSpec((HEADS, nch, 128, 128), lambda t: (0, rev(t), 0, 0)),
                  pl.BlockSpec((1, D), lambda t: (0, 0)), pl.BlockSpec((1, 128), lambda t: (0, 0))] + [any_spec] * n,
        out_specs=[blk] * 4 + [pl.BlockSpec((1, D), lambda t: (0, 0)),
                               pl.BlockSpec((HEADS, 1, 128), lambda t: (0, 0, 0))] + [any_spec] * n,
        scratch_shapes=[pltpu.VMEM((HEADS, 128, 128), F32), pltpu.SemaphoreType.DMA((3 * n,)),
                        pltpu.SemaphoreType.DMA((3 * n,))],
        compiler_params=_cp(("arbitrary",)),
    )(z, z, z, z, o, do_a, states, lbv, hnw, *partials)


def _attn_bwd(z, cc, ss, ob, lse, do_bg):
    def body(q_ref, k_ref, v_ref, ag_ref, cc_ref, ss_ref, ob_ref, lse_ref, dobg_ref,
             dq_ref, dk_ref, dv_ref, dag_ref,
             tmp, qs, ks, vs, dos, dqs, dks, dvs, dkp, dvp, do_t, ls0_t, ls1_t, dl0_t, dl1_t, ls0, ls1, dl0, dl1):
        g = pl.program_id(1)
        first_half = _half_mask()
        prev_ok, cur_ok = _attn_masks()
        lane = lax.broadcasted_iota(jnp.int32, (1, LANES), 1)
        heads = (lane < 64, lane >= 64)
        nblk = _group_blocks(g)
        cc_v, ss_v = cc_ref[...], ss_ref[...]

        @pl.when(g == 0)
        def _():
            ag, obv, dobg = ag_ref[...], ob_ref[...], dobg_ref[...]
            dag_ref[...] = (dobg * obv * _dsilu(ag)).astype(BF16)
            dob = dobg * _silu(ag)
            do_t[...] = dob
            prod = dob * obv
            dl = jnp.concatenate(
                [jnp.broadcast_to(jnp.sum(prod[:, 0:64], axis=-1, keepdims=True), (S, 64)),
                 jnp.broadcast_to(jnp.sum(prod[:, 64:128], axis=-1, keepdims=True), (S, 64))], axis=1)
            dl_sw = pltpu.roll(dl, 64, 1)
            dl0_t[...] = jnp.where(heads[0], dl, dl_sw)
            dl1_t[...] = jnp.where(heads[0], dl_sw, dl)
            ls = lse_ref[...]
            ls_sw = pltpu.roll(ls, 64, 1)
            ls0_t[...] = jnp.where(heads[0], ls, ls_sw)
            ls1_t[...] = jnp.where(heads[0], ls_sw, ls)
            ks[0:ATT_PAD, :] = jnp.zeros((ATT_PAD, LANES), BF16)
            vs[0:ATT_PAD, :] = jnp.zeros((ATT_PAD, LANES), BF16)

        tmp[...] = _rope(q_ref[...], cc_v, ss_v, first_half) * ATT_SCALE
        _to_residues_dyn(g, qs, tmp)
        tmp[...] = _rope(k_ref[...], cc_v, ss_v, first_half)
        _to_residues_dyn(g, ks, tmp, ATT_PAD, BF16)
        _to_residues_dyn(g, vs, v_ref, ATT_PAD, BF16)
        _to_residues_dyn(g, dos, do_t)
        _to_residues_dyn(g, ls0, ls0_t)
        _to_residues_dyn(g, ls1, ls1_t)
        _to_residues_dyn(g, dl0, dl0_t)
        _to_residues_dyn(g, dl1, dl1_t)
        lss, dls = (ls0, ls1), (dl0, dl1)

        def unit(u, carry):
            start = pl.multiple_of(u * 128, 128)
            cur = pl.ds(start, 128)
            both = pl.ds(start, 256)
            pm = prev_ok & ((u & (nblk - 1)) != 0)
            qu, dou = qs[cur, :], dos[cur, :]
            kcat, vcat = ks[both, :], vs[both, :]
            dq_u = None
            q_l, do_l, ds_l, p_l = [], [], [], []
            for hh in range(2):
                q_h = jnp.where(heads[hh], qu, 0.0).astype(BF16)
                do_h = jnp.where(heads[hh], dou, 0.0).astype(BF16)
                s = _dot_nt(q_h, kcat)
                dp = _dot_nt(do_h, vcat)
                lse_h, dl_h = lss[hh][cur, :], dls[hh][cur, :]
                pp = jnp.where(pm, jnp.exp(s[:, 0:128] - lse_h), 0.0)
                pc = jnp.where(cur_ok, jnp.exp(s[:, 128:256] - lse_h), 0.0)
                ds = jnp.concatenate([pp * (dp[:, 0:128] - dl_h), pc * (dp[:, 128:256] - dl_h)], axis=1).astype(BF16)
                dq = _dot(ds, kcat)
                dq_u = dq if hh == 0 else jnp.where(heads[1], dq, dq_u)
                q_l.append(q_h)
                do_l.append(do_h)
                ds_l.append(ds)
                p_l.append(jnp.concatenate([pp, pc], axis=1).astype(BF16))
            dkcat = _dot_tn(jnp.concatenate(ds_l, axis=0), jnp.concatenate(q_l, axis=0))
            dvcat = _dot_tn(jnp.concatenate(p_l, axis=0), jnp.concatenate(do_l, axis=0))
            dkp[cur, :] = dkcat[0:128]
            dks[cur, :] = dkcat[128:256]
            dvp[cur, :] = dvcat[0:128]
            dvs[cur, :] = dvcat[128:256]
            dqs[cur, :] = dq_u
            return carry

        lax.fori_loop(0, 16, unit, 0, unroll=ATT_UNROLL)
        dks[0:S - 128, :] += dkp[128:S, :]
        dvs[0:S - 128, :] += dvp[128:S, :]
        _from_residues_dyn(g, tmp, dqs)
        dq_ref[0] = (_rope(tmp[...], cc_v, -ss_v, first_half) * ATT_SCALE).astype(BF16)
        _from_residues_dyn(g, tmp, dks)
        dk_ref[0] = _rope(tmp[...], cc_v, -ss_v, first_half).astype(BF16)
        _from_residues_dyn(g, tmp, dvs)
        dv_ref[0] = tmp[...].astype(BF16)

    grp = pl.BlockSpec((1, S, LANES), lambda p, g: (g, 0, p))
    buf = pltpu.VMEM((S, LANES), F32)
    padded_b = pltpu.VMEM((ATT_PAD + S, LANES), BF16)
    return pl.pallas_call(
        body, name="attn_bwd", grid=(4, 3),
        out_shape=[jax.ShapeDtypeStruct((3, S, 512), BF16)] * 3 + [jax.ShapeDtypeStruct((S, 512), BF16)],
        in_specs=_attn_in_specs(3), out_specs=[grp, grp, grp, pl.BlockSpec((S, LANES), lambda p, g: (0, p))],
        scratch_shapes=[buf, buf, padded_b, padded_b] + [buf] * 15,
        compiler_params=_cp(("parallel", "arbitrary")),
    )(z, z, z, z, cc, ss, ob, lse, do_bg)


def _in_proj_bwd(dz, h, w_in):
    half = S // 2
    slab = (D, SHARD_COLS)

    def body(dz_hbm, h_hbm, w_hbm, dh_hbm, g_chip, r1_hbm, relay_hbm, r2_hbm,
             h_buf, dz_buf, stage_d, r1_buf, stage_i, acc,
             dz_sem, w_sem, h_sem, r1_sem, out_sem, send_d, recv_d, send_i, recv_i):
        x, y, c = _mesh_pos()
        sibling = (x, y, 1 - c)
        north = c == 1
        near = (jnp.where(north, 1 - x, x), jnp.where(north, y, 1 - y))
        far = (jnp.where(north, x, 1 - x), jnp.where(north, 1 - y, y))
        chips = [(1 - x, 1 - y), near, far, (x, y)]

        def cols(d):
            return pl.ds(pl.multiple_of(d * SHARD_COLS, LANES), SHARD_COLS)

        blocks = []
        for q_sib, q in zip([chips[0], far, near, chips[3]], chips):
            blocks += [4 * q_sib[0] + 2 * q_sib[1] + (1 - c), 4 * q[0] + 2 * q[1] + c]

        def dz_tile(t):
            return _SplitCopy(dz_hbm.at[pl.ds((t % 2) * half, half), cols(blocks[t // 2])],
                                         dz_buf.at[t % 2], dz_sem.at[t % 2])

        def to_sibling(i):
            return pltpu.make_async_remote_copy(
                src_ref=stage_d.at[i % 2], dst_ref=r1_hbm.at[i], send_sem=send_d.at[i], recv_sem=recv_d.at[i],
                device_id=sibling, device_id_type=MESH)

        def to_owner(i):
            dst = relay_hbm if i == 0 else r2_hbm.at[i - 1]
            return pltpu.make_async_remote_copy(
                src_ref=stage_i.at[i], dst_ref=dst, send_sem=send_i.at[i], recv_sem=recv_i.at[i],
                device_id=(*(far if i == 2 else near), c), device_id_type=MESH)

        h_copy = _SplitCopy(h_hbm, h_buf, h_sem)
        h_copy.start()
        dz_tile(0).start()
        h_copy.wait()
        for b in range(8):
            i = b // 2
            g = None
            for r in range(2):
                t = 2 * b + r
                if t + 1 < 16:
                    dz_tile(t + 1).start()
                dz_tile(t).wait()
                part = _dot_tn(h_buf[r * half:(r + 1) * half, :], dz_buf[t % 2])
                g = part if g is None else g + part
                if b % 2 == 1 and r == 0:
                    to_sibling(i).wait_recv()
                    r1_copy = _SplitCopy(r1_hbm.at[i], r1_buf, r1_sem)
                    r1_copy.start()
            if b % 2 == 0:
                if i >= 2:
                    to_sibling(i - 2).wait_send()
                stage_d[i % 2] = g.astype(BF16)
                to_sibling(i).start()
            else:
                r1_copy.wait()
                g = g + r1_buf[...].astype(F32)
                if i == 2:
                    to_owner(0).wait_recv()
                    relay_copy = _SplitCopy(relay_hbm, r1_buf, r1_sem)
                    relay_copy.start()
                    relay_copy.wait()
                    g = g + r1_buf[...].astype(F32)
                if i < 3:
                    stage_i[i] = g.astype(BF16)
                    to_owner(i).start()
                else:
                    g_chip[...] = g
        to_sibling(2).wait_send()
        to_sibling(3).wait_send()

        def dz2(t):
            return _SplitCopy(
                dz_hbm.at[pl.ds((t % 2) * half, half), pl.ds((t // 2) * SHARD_COLS, SHARD_COLS)],
                dz_buf.at[t % 2], dz_sem.at[t % 2])

        def w2(b):
            return _SplitCopy(w_hbm.at[:, pl.ds(b * SHARD_COLS, SHARD_COLS)],
                                         stage_d.at[b % 2], w_sem.at[b % 2])

        dz2(0).start()
        w2(0).start()
        for t in range(16):
            b, r = t // 2, t % 2
            if t + 1 < 16:
                dz2(t + 1).start()
            if r == 0:
                if b + 1 < 8:
                    w2(b + 1).start()
                w2(b).wait()
            dz2(t).wait()
            part = _dot_nt(dz_buf[t % 2], stage_d[b % 2])
            if b == 0:
                acc[r] = part
            else:
                acc[r] += part
        dh_out = [_SplitCopy(acc.at[r], dh_hbm.at[pl.ds(r * half, half), :], out_sem.at[r])
                  for r in range(2)]
        for cp in dh_out:
            cp.start()
        for cp in dh_out:
            cp.wait()
        for i in range(3):
            to_owner(i).wait_send()
        for i in (1, 2):
            to_owner(i).wait_recv()

    any_spec = pl.BlockSpec(memory_space=pl.ANY)
    return pl.pallas_call(
        body, name="in_proj_bwd",
        out_shape=[jax.ShapeDtypeStruct((S, D), F32), jax.ShapeDtypeStruct(slab, F32),
                   jax.ShapeDtypeStruct((4,) + slab, BF16), jax.ShapeDtypeStruct(slab, BF16),
                   jax.ShapeDtypeStruct((2,) + slab, BF16)],
        in_specs=[any_spec] * 3,
        out_specs=[any_spec, pl.BlockSpec(memory_space=pltpu.VMEM), any_spec, any_spec, any_spec],
        scratch_shapes=[pltpu.VMEM((S, D), BF16), pltpu.VMEM((2, half, SHARD_COLS), BF16),
                        pltpu.VMEM((2,) + slab, BF16), pltpu.VMEM(slab, BF16), pltpu.VMEM((3,) + slab, BF16),
                        pltpu.VMEM((2, half, D), F32),
                        pltpu.SemaphoreType.DMA((2,)), pltpu.SemaphoreType.DMA((2,)), pltpu.SemaphoreType.DMA,
                        pltpu.SemaphoreType.DMA, pltpu.SemaphoreType.DMA((2,)),
                        pltpu.SemaphoreType.DMA((4,)), pltpu.SemaphoreType.DMA((4,)),
                        pltpu.SemaphoreType.DMA((3,)), pltpu.SemaphoreType.DMA((3,))],
        compiler_params=_cp(),
    )(dz, h, w_in)


def _grad_x(x, norm_w, dh, dx2):
    tr = 256

    def body(x_ref, w_ref, dh_ref, dx2_ref, gx_ref, gnw_ref):
        @pl.when(pl.program_id(0) == 0)
        def _():
            gnw_ref[...] = jnp.zeros_like(gnw_ref)

        xv, dhv = x_ref[...], dh_ref[...]
        r = lax.rsqrt(jnp.mean(xv * xv, axis=-1, keepdims=True) + EPS)
        n = xv * r
        gnw_ref[...] += jnp.sum(dhv * n, axis=0, keepdims=True)
        dn = dhv * w_ref[...]
        gx_ref[...] = dx2_ref[...] + r * (dn - n * jnp.mean(dn * n, axis=-1, keepdims=True))

    row = pl.BlockSpec((tr, D), lambda i: (i, 0))
    vec = pl.BlockSpec((1, D), lambda i: (0, 0))
    return pl.pallas_call(
        body, name="grad_x", grid=(S // tr,),
        out_shape=[jax.ShapeDtypeStruct((S, D), F32), jax.ShapeDtypeStruct((1, D), F32)],
        in_specs=[row, vec, row, row], out_specs=[row, vec],
        compiler_params=_cp(("arbitrary",)),
    )(x, norm_w, dh, dx2)


def _rope_tables(positions):
    inv_freq = 10000.0 ** (-jnp.arange(0, 64, 2, dtype=F32) / 64)
    ang = positions.astype(F32)[:, None] * inv_freq[None, :]
    cos, sin = jnp.cos(ang), jnp.sin(ang)
    return jnp.tile(cos, (1, 4)), jnp.tile(jnp.concatenate([-sin, sin], axis=1), (1, 2))


def _local_step(x, positions, norm_w, lb_logits, hnw, fnw, target, w_in_shard, small_shards, core):
    cc, ss = _rope_tables(positions)
    lbv = jax.nn.sigmoid(lb_logits[0:1] - lb_logits[1:2])
    h = _rmsnorm_in(x, norm_w)
    z, w_in = _in_proj_gather(h, w_in_shard)
    o, o_a, states, w_a, w_b, w_out = _hgrn_fwd(z, lbv, hnw, small_shards)
    ob, lse, o_bg = _attn_fwd(z, cc, ss)
    dx2, dx2b, dgp, do_a, do_bg, merged, dy_a, dy_b, tail_small = _tail(x, o_a, o_bg, z, target, w_a, w_b, w_out, fnw)
    g_out, gb_out = _tn_matmul(merged, dx2b, "grad_w_out")
    g_a, gb_a = _tn_matmul(o_a, dy_a, "grad_w_a")
    g_b, gb_b = _tn_matmul(o_bg, dy_b, "grad_w_b")
    grads, gb = (g_a, g_b, g_out), (gb_a, gb_b, gb_out)
    r1 = _exchange_sibling(GATHER_IDS, gb)
    pb = [_chip_partials(a, grads[i], r1[i], core) for i, a in enumerate(GATHER_IDS)]
    dhq, dhf, dhi, dhg, glb, ghn, *r2 = _hgrn_bwd(z, o, do_a, states, lbv, hnw, pb)
    dq, dk, dv, dag = _attn_bwd(z, cc, ss, ob, lse, do_bg)
    dz = jnp.concatenate([dhq, dhf, dhi, dhg, dq[0], dq[1], dq[2], dk[0], dk[1], dk[2], dv[0], dv[1], dv[2], dag, dgp],
                         axis=1)
    dh, g_chip_in, _, _, r2_in = _in_proj_bwd(dz, h, w_in)
    grad_x, gnw = _grad_x(x, norm_w, dh, dx2)
    ghn_row = jnp.pad(jnp.sum(ghn, axis=0), ((0, 0), (0, D - 128)))
    small = jnp.concatenate([gnw, glb, ghn_row, tail_small[0:2], jnp.zeros((3, D), F32)], axis=0)
    return grad_x, (g_chip_in, r2_in), grads, r1, r2, small


def kernel(x, positions, norm_w, w_in, lb_logits, hgrn_norm_w, w_branch_a, w_branch_b, w_out, final_norm_w, loss_target, m_norm_w, m_w_in, m_lb_logits, m_hgrn_norm_w, m_w_branch_a, m_w_branch_b, m_w_out, m_final_norm_w, v_norm_w, v_w_in, v_lb_logits, v_hgrn_norm_w, v_w_branch_a, v_w_branch_b, v_w_out, v_final_norm_w):
    ix, iy, ic = _mesh_pos()
    core = jnp.reshape(ic, (1,)).astype(jnp.int32)
    pos = jnp.stack([4 * ix + 2 * iy + ic, 2 * ix + iy]).astype(jnp.int32)

    shards = [w_in[0], w_branch_a[0], w_branch_b[0], w_out[0]]
    moments_m = [m_w_in[0], m_w_branch_a[0], m_w_branch_b[0], m_w_out[0]]
    moments_v = [v_w_in[0], v_w_branch_a[0], v_w_branch_b[0], v_w_out[0]]
    names = ("w_in", "w_a", "w_b", "w_out")
    ids = GATHER_IDS
    shards_b = [_cast_bf16(w, f"cast_{nm}") for w, nm in zip(shards, names)]

    fnw2 = final_norm_w.reshape(1, D)
    grad_x, (g_chip_in, r2_in), grads, r1, r2, small = _local_step(
        x[0], positions[0], norm_w, lb_logits, hgrn_norm_w, fnw2, loss_target[0], shards_b[0], shards_b[1:], core)

    gathered = _gather_small(small)
    big =[_reduce_own_and_update(shards[0], moments_m[0], moments_v[0], g_chip_in, r2_in)]
    big += [_reduce_and_update(a, shards[a], moments_m[a], moments_v[a], grads[i], r1[i], r2[i], pos)
            for i, a in enumerate(ids)]
    sm = _small_update(gathered, norm_w, lb_logits, hgrn_norm_w, fnw2,
                       (m_norm_w, m_lb_logits, m_hgrn_norm_w, m_final_norm_w.reshape(1, D),
                        v_norm_w, v_lb_logits, v_hgrn_norm_w, v_final_norm_w.reshape(1, D)))
    loss = sm[0][0, 0]
    outs = [loss, grad_x[None]]
    for kind in range(4):
        s_nw, s_lb, s_hn, s_fn = sm[1 + 4 * kind:5 + 4 * kind]
        outs += [s_nw, big[0][kind][None], s_lb, s_hn, big[1][kind][None], big[2][kind][None],
                 big[3][kind][None], s_fn.reshape(D)]
    return tuple(outs)
```

```python
import functools

import jax
import jax.numpy as jnp
from jax import lax
from jax.experimental import pallas as pl
from jax.experimental.pallas import tpu as pltpu

F32 = jnp.float32
BF16 = jnp.bfloat16
MESH = pl.DeviceIdType.MESH

S = 2048
D = 1024
NDEV = 8
HEADS = 8
CHUNK = 64
SUB = 16
HBLK = 256
ATT_PAD = 128
ATT_UNROLL = 16
COPY_PARTS = 4
EXP_CLAMP = 80.0
EPS = 1e-6
IN_COLS = 11264
SHARD_COLS = IN_COLS // NDEV
ATT_DILS = (1, 4, 16)
ATT_SCALE = 64 ** -0.5
LANES = 128

ADAM_LR, ADAM_B1, ADAM_B2, ADAM_EPS, ADAM_WD, ADAM_STEP = 0.001, 0.9, 0.999, 1e-08, 0.01, 10

VMEM_LIMIT = 56 * 1024 * 1024


def _cp(sem=None, **kw):
    return pltpu.CompilerParams(dimension_semantics=sem, vmem_limit_bytes=VMEM_LIMIT, **kw)


def _dot(a, b):
    return jnp.dot(a, b, preferred_element_type=F32)


def _dot_nt(a, b):
    return lax.dot_general(a, b, (((1,), (1,)), ((), ())), preferred_element_type=F32)


def _dot_tn(a, b):
    return lax.dot_general(a, b, (((0,), (0,)), ((), ())), preferred_element_type=F32)


def _split2(x):
    hi = x.astype(BF16)
    lo = (x - hi.astype(F32)).astype(BF16)
    return hi, lo


def _split3(x):
    hi = x.astype(BF16)
    r = x - hi.astype(F32)
    mid = r.astype(BF16)
    lo = (r - mid.astype(F32)).astype(BF16)
    return hi, mid, lo


def _dot_ones(ones_bf16, x):
    hi, mid, lo = _split3(x)
    return _dot(ones_bf16, hi) + _dot(ones_bf16, mid) + _dot(ones_bf16, lo)


def _silu(x):
    return x * jax.nn.sigmoid(x)


def _dsilu(x):
    s = jax.nn.sigmoid(x)
    return s * (1.0 + x * (1.0 - s))


def _mesh_pos():
    return lax.axis_index("x"), lax.axis_index("y"), lax.axis_index("c")


class _SplitCopy:
    def __init__(self, src, dst, sem):
        self.src, self.dst, self.sem = src, dst, sem

    def start(self):
        rows = self.src.shape[0] // COPY_PARTS
        for p in range(COPY_PARTS):
            chunk = pl.ds(p * rows, rows)
            pltpu.make_async_copy(self.src.at[chunk], self.dst.at[chunk], self.sem).start()

    def wait(self):
        pltpu.make_async_copy(self.src, self.dst, self.sem).wait()


def _shard_of(ref, a, d):
    if a == 0:
        return ref.at[:, pl.ds(pl.multiple_of(d * SHARD_COLS, LANES), SHARD_COLS)]
    if a == 2:
        return ref.at[:, pl.ds(pl.multiple_of(d * LANES, LANES), LANES)]
    return ref.at[pl.ds(pl.multiple_of(d * 128, 128), 128), :]


FULL_SHAPES = ((D, IN_COLS), (D, D), (512, D), (D, D))
SHARD_SHAPES = ((D, SHARD_COLS), (128, D), (512, 128), (128, D))


def _allgather_steps(ids, ins, outs, send_sems, recv_sems, local_sems):
    n = len(ids)
    x, y, c = _mesh_pos()
    me, sibling = (x, y, c), (x, y, 1 - c)
    chips = [(1 - x, y), (x, 1 - y), (1 - x, 1 - y)]

    def blk(a, p):
        return _shard_of(outs[a], ids[a], 4 * p[0] + 2 * p[1] + p[2])

    def copy(a, k, block, to, src=None):
        return pltpu.make_async_remote_copy(
            src_ref=blk(a, block) if src is None else src, dst_ref=blk(a, block),
            send_sem=send_sems.at[a * 7 + k], recv_sem=recv_sems.at[a * 7 + k],
            device_id=to, device_id_type=MESH)

    mine = [pltpu.make_async_copy(ins[a], blk(a, me), local_sems.at[a]) for a in range(n)]
    first = []
    for a in range(n):
        first += [copy(a, 1 + j, me, (*chip, c), src=ins[a]) for j, chip in enumerate(chips)]
    for a in range(n):
        first.append(copy(a, 0, me, sibling, src=ins[a]))
    passed = [copy(a, 4 + j, (*chip, c), sibling) for j, chip in enumerate(chips) for a in range(n)]

    def start():
        for cp in mine + first:
            cp.start()

    def middle():
        for j, chip in enumerate(chips):
            for a in range(n):
                copy(a, 1 + j, (*chip, c), me).wait_recv()
                passed[j * n + a].start()

    def end():
        for a in range(n):
            copy(a, 0, sibling, me).wait_recv()
        for j, chip in enumerate(chips):
            for a in range(n):
                copy(a, 4 + j, (*chip, 1 - c), me).wait_recv()
        for cp in first + passed:
            cp.wait_send()
        for cp in mine:
            cp.wait()

    return start, middle, end


def _in_proj_gather(h, w_shard):
    half = S // 2
    slab = (D, SHARD_COLS)

    def body(h_hbm, w_hbm, z_hbm, wfull_hbm, h_buf, land, zstage,
             h_sem, own_sem, z_sem, wout_sem, send_sems, recv_sems):
        x, y, c = _mesh_pos()
        sibling = (x, y, 1 - c)
        north = c == 1

        def chips_of(first_x):
            near = (jnp.where(first_x, 1 - x, x), jnp.where(first_x, y, 1 - y))
            far = (jnp.where(first_x, x, 1 - x), jnp.where(first_x, 1 - y, y))
            return [near, far, (1 - x, 1 - y)]

        mine, theirs = chips_of(north), chips_of(jnp.logical_not(north))

        def dev(chip, core):
            return 4 * chip[0] + 2 * chip[1] + core

        block_of = ([dev((x, y), c), dev((x, y), 1 - c)] + [dev(q, c) for q in mine]
                    + [dev(q, 1 - c) for q in theirs])

        def cols(d):
            if isinstance(d, int):
                return pl.ds(d * SHARD_COLS, SHARD_COLS)
            return pl.ds(pl.multiple_of(d * SHARD_COLS, LANES), SHARD_COLS)

        def send(k, src, dst_slot, to):
            return pltpu.make_async_remote_copy(
                src_ref=src, dst_ref=land.at[dst_slot], send_sem=send_sems.at[k], recv_sem=recv_sems.at[k],
                device_id=to, device_id_type=MESH)

        def to_sibling():
            return send(0, w_hbm, 1, sibling)

        def to_chip(j):
            if j == 2:
                return send(3, land.at[2], 4, (*mine[1], c))
            return send(1 + j, w_hbm, 2 + j, (*mine[j], c))

        def pass_on(j):
            return send(4 + j, land.at[2 + j], 5 + j, sibling)

        own = _SplitCopy(w_hbm, land.at[0], own_sem)
        h_copy = _SplitCopy(h_hbm, h_buf, h_sem)
        own.start()
        h_copy.start()
        to_sibling().start()
        to_chip(0).start()
        h_copy.wait()
        own.wait()

        def multiply(slot, n_done):
            d = block_of[slot]
            out = _SplitCopy(land.at[slot], wfull_hbm.at[:, cols(d)], wout_sem.at[slot])
            out.start()
            for r in range(2):
                rows = pl.ds(r * half, half)
                zc = _SplitCopy(zstage.at[r], z_hbm.at[rows, cols(d)], z_sem.at[r])
                if n_done > 0:
                    zc.wait()
                zstage[r] = _dot(h_buf[r * half:(r + 1) * half, :], land[slot])
                zc.start()
            return out

        outs = [multiply(0, 0)]
        to_sibling().wait_recv()
        outs.append(multiply(1, 1))
        done = 2
        for j in range(3):
            to_chip(j).wait_recv()
            pass_on(j).start()
            to_chip(j).wait_send()
            if j < 2:
                to_chip(j + 1).start()
            outs.append(multiply(2 + j, done))
            pass_on(j).wait_recv()
            outs.append(multiply(5 + j, done + 1))
            done += 2
        for r in range(2):
            _SplitCopy(zstage.at[r], z_hbm.at[pl.ds(r * half, half), cols(0)], z_sem.at[r]).wait()
        for out in outs:
            out.wait()
        to_sibling().wait_send()
        for j in range(3):
            pass_on(j).wait_send()

    any_spec = pl.BlockSpec(memory_space=pl.ANY)
    return pl.pallas_call(
        body, name="in_proj_gather",
        out_shape=[jax.ShapeDtypeStruct((S, IN_COLS), F32), jax.ShapeDtypeStruct((D, IN_COLS), BF16)],
        in_specs=[any_spec] * 2, out_specs=[any_spec] * 2,
        scratch_shapes=[pltpu.VMEM((S, D), BF16), pltpu.VMEM((8,) + slab, BF16), pltpu.VMEM((2, half, SHARD_COLS), F32),
                        pltpu.SemaphoreType.DMA, pltpu.SemaphoreType.DMA, pltpu.SemaphoreType.DMA((2,)),
                        pltpu.SemaphoreType.DMA((8,)), pltpu.SemaphoreType.DMA((7,)), pltpu.SemaphoreType.DMA((7,))],
        compiler_params=_cp(),
    )(h, w_shard)


def _exchange_sibling(ids, gb):
    n = len(gb)

    def body(*refs):
        ins, outs = refs[:n], refs[n:2 * n]
        send_sems, recv_sems = refs[2 * n:]
        x, y, c = _mesh_pos()
        sibling = (x, y, 1 - c)
        copies = []
        for i, a in enumerate(ids):
            for q in range(4):
                copies.append(pltpu.make_async_remote_copy(
                    src_ref=_shard_of(ins[i], a, 2 * q + (1 - c)), dst_ref=outs[i].at[q],
                    send_sem=send_sems.at[i * 4 + q], recv_sem=recv_sems.at[i * 4 + q],
                    device_id=sibling, device_id_type=MESH))
        for cp in copies:
            cp.start()
        for cp in copies:
            cp.wait()

    any_spec = pl.BlockSpec(memory_space=pl.ANY)
    return pl.pallas_call(
        body, name="grads_to_sibling",
        out_shape=[jax.ShapeDtypeStruct((4,) + SHARD_SHAPES[a], BF16) for a in ids],
        in_specs=[any_spec] * n, out_specs=[any_spec] * n,
        scratch_shapes=[pltpu.SemaphoreType.DMA((4 * n,)), pltpu.SemaphoreType.DMA((4 * n,))],
    )(*gb)


def _exchange_chips_steps(ins, outs, send_sems, recv_sems):
    x, y, c = _mesh_pos()
    chips = [(1 - x, y), (x, 1 - y), (1 - x, 1 - y)]
    copies = []
    for a in range(len(ins)):
        for k, chip in enumerate(chips):
            copies.append(pltpu.make_async_remote_copy(
                src_ref=ins[a].at[2 * chip[0] + chip[1]], dst_ref=outs[a].at[k],
                send_sem=send_sems.at[a * 3 + k], recv_sem=recv_sems.at[a * 3 + k],
                device_id=(*chip, c), device_id_type=MESH))

    def start():
        for cp in copies:
            cp.start()

    def end():
        for cp in copies:
            cp.wait()

    return start, end


def _gather_small(small):
    def body(small_ref, small_out, ssend, srecv, local_sem):
        x, y, c = _mesh_pos()
        me = 4 * x + 2 * y + c
        copies = []
        for r in range(1, NDEV):
            peer = (1 - x if r & 4 else x, 1 - y if r & 2 else y, 1 - c if r & 1 else c)
            copies.append(pltpu.make_async_remote_copy(
                src_ref=small_ref, dst_ref=small_out.at[me],
                send_sem=ssend.at[r - 1], recv_sem=srecv.at[r - 1],
                device_id=peer, device_id_type=MESH))
        own = pltpu.make_async_copy(small_ref, small_out.at[me], local_sem)
        own.start()
        for cp in copies:
            cp.start()
        for cp in copies:
            cp.wait()
        own.wait()

    any_spec = pl.BlockSpec(memory_space=pl.ANY)
    return pl.pallas_call(
        body, name="gather_small",
        out_shape=jax.ShapeDtypeStruct((NDEV,) + small.shape, F32),
        in_specs=[any_spec], out_specs=any_spec,
        scratch_shapes=[pltpu.SemaphoreType.DMA((NDEV - 1,)), pltpu.SemaphoreType.DMA((NDEV - 1,)),
                        pltpu.SemaphoreType.DMA],
    )(small)


def _shard_tiles(a):
    rows, cols = SHARD_SHAPES[a]
    tr = min(rows, 256)
    return (tr, cols), rows // tr


def _full_index(a, d, i):
    (tr, _), nt = _shard_tiles(a)
    if a in (0, 2):
        return (i, d)
    return (d * nt + i, 0)


def _cast_bf16(x, name):
    rows, cols = x.shape
    tr = min(rows, 256)

    def body(x_ref, o_ref):
        o_ref[...] = x_ref[...].astype(BF16)

    return pl.pallas_call(
        body, name=name, out_shape=jax.ShapeDtypeStruct(x.shape, BF16), grid=(rows // tr,),
        in_specs=[pl.BlockSpec((tr, cols), lambda i: (i, 0))],
        out_specs=pl.BlockSpec((tr, cols), lambda i: (i, 0)),
        compiler_params=_cp(("parallel",)),
    )(x)


def _chip_partials(a, g_full, r1, core):
    tile, nt = _shard_tiles(a)

    def body(c_ref, g_ref, r_ref, o_ref):
        o_ref[0] = (g_ref[...] + r_ref[0].astype(F32)).astype(BF16)

    grid_spec = pltpu.PrefetchScalarGridSpec(
        num_scalar_prefetch=1, grid=(4, nt),
        in_specs=[pl.BlockSpec(tile, lambda q, i, c: _full_index(a, 2 * q + c[0], i)),
                  pl.BlockSpec((1,) + tile, lambda q, i, c: (q, i, 0))],
        out_specs=pl.BlockSpec((1,) + tile, lambda q, i, c: (q, i, 0)))
    return pl.pallas_call(
        body, name=f"chip_partials_{a}", grid_spec=grid_spec,
        out_shape=jax.ShapeDtypeStruct((4,) + SHARD_SHAPES[a], BF16),
        compiler_params=_cp(("parallel", "parallel")),
    )(core, g_full, r1)


def _adam(w, g, m, v):
    m = ADAM_B1 * m + (1.0 - ADAM_B1) * g
    v = ADAM_B2 * v + (1.0 - ADAM_B2) * (g * g)
    m_hat = m / (1.0 - ADAM_B1 ** ADAM_STEP)
    v_hat = v / (1.0 - ADAM_B2 ** ADAM_STEP)
    delta = -ADAM_LR * (m_hat / (jnp.sqrt(v_hat) + ADAM_EPS) + ADAM_WD * w)
    return delta, m, v


def _reduce_and_update(a, w, m, v, g_full, r1, r2, pos):
    tile, nt = _shard_tiles(a)

    def body(p_ref, w_ref, m_ref, v_ref, g_ref, r1_ref, r2_ref, go_ref, do_ref, mo_ref, vo_ref):
        g = g_ref[...] + r1_ref[0].astype(F32)
        g = g + r2_ref[0].astype(F32)
        g = g + r2_ref[1].astype(F32)
        g = g + r2_ref[2].astype(F32)
        delta, m_new, v_new = _adam(w_ref[...], g, m_ref[...], v_ref[...])
        go_ref[...] = g
        do_ref[...] = delta
        mo_ref[...] = m_new
        vo_ref[...] = v_new

    own = pl.BlockSpec(tile, lambda i, p: (i, 0))
    grid_spec = pltpu.PrefetchScalarGridSpec(
        num_scalar_prefetch=1, grid=(nt,),
        in_specs=[own, own, own,
                  pl.BlockSpec(tile, lambda i, p: _full_index(a, p[0], i)),
                  pl.BlockSpec((1,) + tile, lambda i, p: (p[1], i, 0)),
                  pl.BlockSpec((3,) + tile, lambda i, p: (0, i, 0))],
        out_specs=[own] * 4)
    shp = jax.ShapeDtypeStruct(w.shape, F32)
    return pl.pallas_call(
        body, name=f"reduce_update_{a}", grid_spec=grid_spec, out_shape=[shp] * 4,
        compiler_params=_cp(("parallel",)),
    )(pos, w, m, v, g_full, r1, r2)


def _reduce_own_and_update(w, m, v, g_chip, r2):
    tile, nt = _shard_tiles(0)

    def body(w_ref, m_ref, v_ref, g_ref, r2_ref, go_ref, do_ref, mo_ref, vo_ref):
        g = g_ref[...] + r2_ref[0].astype(F32)
        g = g + r2_ref[1].astype(F32)
        delta, m_new, v_new = _adam(w_ref[...], g, m_ref[...], v_ref[...])
        go_ref[...] = g
        do_ref[...] = delta
        mo_ref[...] = m_new
        vo_ref[...] = v_new

    own = pl.BlockSpec(tile, lambda i: (i, 0))
    shp = jax.ShapeDtypeStruct(w.shape, F32)
    return pl.pallas_call(
        body, name="reduce_update_0", grid=(nt,), out_shape=[shp] * 4,
        in_specs=[own, own, own, own, pl.BlockSpec((2,) + tile, lambda i: (0, i, 0))], out_specs=[own] * 4,
        compiler_params=_cp(("parallel",)),
    )(w, m, v, g_chip, r2)


def _small_update(gathered, norm_w, lb_logits, hnw, fnw, moments):
    m_nw, m_lb, m_hn, m_fn, v_nw, v_lb, v_hn, v_fn = moments

    def body(g_ref, nw, lb, hn, fn, mnw, mlb, mhn, mfn, vnw, vlb, vhn, vfn,
             loss_o, g_nw, g_lb, g_hn, g_fn, d_nw, d_lb, d_hn, d_fn,
             mo_nw, mo_lb, mo_hn, mo_fn, vo_nw, vo_lb, vo_hn, vo_fn):
        tot = g_ref[0]
        for d in range(1, NDEV):
            tot = tot + g_ref[d]
        loss_o[...] = tot[4:5, 0:LANES]
        logits = lb[...]
        lbv = jax.nn.sigmoid(logits[0:1] - logits[1:2])
        chain = tot[1:2] * lbv * (1.0 - lbv)
        grads = (tot[0:1], jnp.concatenate([chain, -chain], axis=0), tot[2:3, 0:LANES], tot[3:4])
        outs = ((nw, mnw, vnw, g_nw, d_nw, mo_nw, vo_nw), (lb, mlb, vlb, g_lb, d_lb, mo_lb, vo_lb),
                (hn, mhn, vhn, g_hn, d_hn, mo_hn, vo_hn), (fn, mfn, vfn, g_fn, d_fn, mo_fn, vo_fn))
        for g, (w_r, m_r, v_r, g_o, d_o, m_o, v_o) in zip(grads, outs):
            delta, m_new, v_new = _adam(w_r[...], g, m_r[...], v_r[...])
            g_o[...] = g
            d_o[...] = delta
            m_o[...] = m_new
            v_o[...] = v_new

    shapes = [norm_w.shape, lb_logits.shape, hnw.shape, fnw.shape]
    out_shape = [jax.ShapeDtypeStruct((1, LANES), F32)] + [jax.ShapeDtypeStruct(s, F32) for s in shapes] * 4
    return pl.pallas_call(body, name="small_update", out_shape=out_shape, compiler_params=_cp())(
        gathered, norm_w, lb_logits, hnw, fnw, m_nw, m_lb, m_hn, m_fn, v_nw, v_lb, v_hn, v_fn)


def _rmsnorm_in(x, norm_w):
    tr = 512

    def body(x_ref, w_ref, h_ref):
        xv = x_ref[...]
        r = lax.rsqrt(jnp.mean(xv * xv, axis=-1, keepdims=True) + EPS)
        h_ref[...] = (xv * r * w_ref[...]).astype(BF16)

    return pl.pallas_call(
        body, name="rmsnorm_in", out_shape=jax.ShapeDtypeStruct((S, D), BF16), grid=(S // tr,),
        in_specs=[pl.BlockSpec((tr, D), lambda i: (i, 0)), pl.BlockSpec((1, D), lambda i: (0, 0))],
        out_specs=pl.BlockSpec((tr, D), lambda i: (i, 0)),
        compiler_params=_cp(("parallel",)),
    )(x, norm_w)


def _in_proj(h, w_in):
    tn = 1024

    def body(h_ref, w_ref, z_ref):
        z_ref[...] = _dot(h_ref[...], w_ref[...])

    return pl.pallas_call(
        body, name="in_proj", out_shape=jax.ShapeDtypeStruct((S, IN_COLS), F32), grid=(IN_COLS // tn,),
        in_specs=[pl.BlockSpec((S, D), lambda j: (0, 0)), pl.BlockSpec((D, tn), lambda j: (0, j))],
        out_specs=pl.BlockSpec((S, tn), lambda j: (0, j)),
        compiler_params=_cp(("parallel",)),
    )(h, w_in)


def _block_tri(n, block, upper=False):
    r = lax.broadcasted_iota(jnp.int32, (n, n), 0)
    c = lax.broadcasted_iota(jnp.int32, (n, n), 1)
    keep = (c >= r) if upper else (c <= r)
    return jnp.where(keep & ((r // block) == (c // block)), 1.0, 0.0).astype(BF16)


def _tril_mask(n):
    r = lax.broadcasted_iota(jnp.int32, (n, n), 0)
    c = lax.broadcasted_iota(jnp.int32, (n, n), 1)
    return c <= r


def _chunk_scores(q, k, b, bex, r0, mask):
    parts, qs_l, ks_l, ek_l, eq_l = [], [], [], [], []
    for i in range(CHUNK // SUB):
        ri = slice(r0 + SUB * i, r0 + SUB * (i + 1))
        seen = slice(r0, r0 + SUB * (i + 1))
        base = bex[r0 + SUB * i:r0 + SUB * i + 1]
        eq = jnp.exp(b[ri] - base)
        ek = jnp.exp(jnp.minimum(base - b[seen], EXP_CLAMP))
        ks = k[seen] * ek
        if i + 1 < CHUNK // SUB:
            rest = jnp.zeros((CHUNK - SUB * (i + 1), 128), F32)
            ek, ks = jnp.concatenate([ek, rest], axis=0), jnp.concatenate([ks, rest], axis=0)
        qs = q[ri] * eq
        parts.append(_dot_nt(qs.astype(BF16), ks.astype(BF16)))
        qs_l.append(qs)
        ks_l.append(ks)
        ek_l.append(ek)
        eq_l.append(eq)
    return jnp.where(mask, jnp.concatenate(parts, axis=0), 0.0), qs_l, ks_l, ek_l, eq_l


def _hgrn_cols(hq, hf, hi, lb):
    sg = jax.nn.sigmoid(hf)
    f = lb + (1.0 - lb) * sg
    g = jnp.log(f)
    b = _dot_ones(_block_tri(HBLK, CHUNK), g)
    return _silu(hq), 1.0 - f, g, hi, sg, f, b


GATHER_IDS = (1, 2, 3)


def _hgrn_fwd(z, lbv, hnw, shards):
    ntb, nch = S // HBLK, HBLK // CHUNK
    n = len(GATHER_IDS)

    def body(hq_ref, hf_ref, hi_ref, hg_ref, lb_ref, hnw_ref, s0, s1, s2, o_ref, oa_ref, st_ref, f0, f1, f2,
             state, send_sems, recv_sems, local_sems):
        start, middle, end = _allgather_steps(GATHER_IDS, (s0, s1, s2), (f0, f1, f2), send_sems, recv_sems, local_sems)

        @pl.when(pl.program_id(0) == 0)
        def _():
            state[...] = jnp.zeros_like(state)
            start()

        pl.when(pl.program_id(0) == ntb // 2)(middle)

        q_a, k_a, g_a, v_a, _, _, b_a = _hgrn_cols(hq_ref[...], hf_ref[...], hi_ref[...], lb_ref[...])
        bex_a = b_a - g_a
        eb_a = jnp.exp(b_a)
        mask = _tril_mask(CHUNK)
        hg = hg_ref[...]
        w = hnw_ref[...]
        for h in range(HEADS):
            cols = slice(128 * h, 128 * h + 128)
            q, k, v, b, bex, eb = q_a[:, cols], k_a[:, cols], v_a[:, cols], b_a[:, cols], bex_a[:, cols], eb_a[:, cols]
            st = state[h]
            outs = []
            for c in range(nch):
                r0 = c * CHUNK
                rows = slice(r0, r0 + CHUNK)
                a = _chunk_scores(q, k, b, bex, r0, mask)[0]
                vb = v[rows].astype(BF16)
                b_last = b[r0 + CHUNK - 1:r0 + CHUNK]
                qe = (q[rows] * eb[rows]).astype(BF16)
                outs.append(_dot(a.astype(BF16), vb) + _dot_nt(qe, st.astype(BF16)))
                st_ref[h, c] = st
                ke = (k[rows] * jnp.exp(b_last - b[rows])).astype(BF16)
                st = st * jnp.exp(b_last) + _dot_tn(vb, ke)
            state[h] = st
            o = jnp.concatenate(outs, axis=0)
            o_ref[:, cols] = o
            r = lax.rsqrt(jnp.mean(o * o, axis=-1, keepdims=True) + EPS)
            oa_ref[:, cols] = (o * r * w * _silu(hg[:, cols])).astype(BF16)

        pl.when(pl.program_id(0) == ntb - 1)(end)

    def zcol(j):
        return pl.BlockSpec((HBLK, D), lambda t: (t, j))

    out_blk = pl.BlockSpec((HBLK, D), lambda t: (t, 0))
    any_spec = pl.BlockSpec(memory_space=pl.ANY)
    return pl.pallas_call(
        body, name="hgrn_fwd", grid=(ntb,),
        out_shape=[jax.ShapeDtypeStruct((S, D), F32), jax.ShapeDtypeStruct((S, D), BF16),
                   jax.ShapeDtypeStruct((HEADS, S // CHUNK, 128, 128), F32)]
        + [jax.ShapeDtypeStruct(FULL_SHAPES[a], BF16) for a in GATHER_IDS],
        in_specs=[zcol(0), zcol(1), zcol(2), zcol(3),
                  pl.BlockSpec((1, D), lambda t: (0, 0)), pl.BlockSpec((1, 128), lambda t: (0, 0))] + [any_spec] * n,
        out_specs=[out_blk, out_blk, pl.BlockSpec((HEADS, nch, 128, 128), lambda t: (0, t, 0, 0))] + [any_spec] * n,
        scratch_shapes=[pltpu.VMEM((HEADS, 128, 128), F32), pltpu.SemaphoreType.DMA((7 * n,)),
                        pltpu.SemaphoreType.DMA((7 * n,)), pltpu.SemaphoreType.DMA((n,))],
        compiler_params=_cp(("arbitrary",)),
    )(z, z, z, z, lbv, hnw, *shards)


def _half_mask():
    lane = lax.broadcasted_iota(jnp.int32, (1, LANES), 1)
    return (lane % 64) < 32


def _rope(t, cc, ss, first_half):
    partner = jnp.where(first_half, pltpu.roll(t, 96, 1), pltpu.roll(t, 32, 1))
    return t * cc + partner * ss


def _attn_masks():
    i = lax.broadcasted_iota(jnp.int32, (128, 128), 0)
    j = lax.broadcasted_iota(jnp.int32, (128, 128), 1)
    return j >= i, j <= i


def _to_residues_dyn(g, dst, src, row0=0, dtype=None):
    for gi, dil in enumerate((1, 4, 16)):
        m = S // dil

        @pl.when(g == gi)
        def _(dil=dil, m=m):
            for r in range(dil):
                v = src[...] if dil == 1 else src[pl.ds(r, m, stride=dil), :]
                if dtype is not None:
                    v = v.astype(dtype)
                dst[row0 + r * m:row0 + (r + 1) * m, 0:LANES] = v


def _from_residues_dyn(g, dst, src, row0=0):
    for gi, dil in enumerate((1, 4, 16)):
        m = S // dil

        @pl.when(g == gi)
        def _(dil=dil, m=m):
            for r in range(dil):
                v = src[row0 + r * m:row0 + (r + 1) * m, :]
                if dil == 1:
                    dst[...] = v
                else:
                    dst[pl.ds(r, m, stride=dil), :] = v


def _group_blocks(g):
    return jnp.where(g == 0, 16, jnp.where(g == 1, 4, 1))


def _attn_in_specs(extra):
    def zcol(off):
        return pl.BlockSpec((S, LANES), lambda p, g: (0, off + 4 * g + p))

    per_pair = pl.BlockSpec((S, LANES), lambda p, g: (0, p))
    const = pl.BlockSpec((S, LANES), lambda p, g: (0, 0))
    return [zcol(32), zcol(44), zcol(56), pl.BlockSpec((S, LANES), lambda p, g: (0, 68 + p)), const, const] + [per_pair] * extra


def _attn_fwd(z, cc, ss):
    def body(q_ref, k_ref, v_ref, ag_ref, cc_ref, ss_ref, ob_ref, lse_ref, obg_ref,
             tmp, qs, ks, vx, og, mg, lg, o_t, m_t, l_t, o_acc, m_acc, l_acc):
        g = pl.program_id(1)
        first_half = _half_mask()
        prev_ok, cur_ok = _attn_masks()
        lane = lax.broadcasted_iota(jnp.int32, (1, LANES), 1)
        heads = (lane < 64, lane >= 64)
        nblk = _group_blocks(g)

        @pl.when(g == 0)
        def _():
            ks[0:ATT_PAD, :] = jnp.zeros((ATT_PAD, LANES), BF16)
            vx[0:ATT_PAD, 0:LANES] = jnp.zeros((ATT_PAD, LANES), BF16)
            vx[:, LANES:2 * LANES] = jnp.ones((ATT_PAD + S, LANES), BF16)

        tmp[...] = _rope(q_ref[...], cc_ref[...], ss_ref[...], first_half) * ATT_SCALE
        _to_residues_dyn(g, qs, tmp)
        tmp[...] = _rope(k_ref[...], cc_ref[...], ss_ref[...], first_half)
        _to_residues_dyn(g, ks, tmp, ATT_PAD, BF16)
        _to_residues_dyn(g, vx, v_ref, ATT_PAD, BF16)

        def unit(u, carry):
            start = pl.multiple_of(u * 128, 128)
            cur = pl.ds(start, 128)
            pm = prev_ok & ((u & (nblk - 1)) != 0)
            qu = qs[cur, :]
            kcat = ks[pl.ds(start, 256), :]
            vext = vx[pl.ds(start, 256), :]
            o_u = m_u = l_u = None
            for hh in range(2):
                s = _dot_nt(jnp.where(heads[hh], qu, 0.0).astype(BF16), kcat)
                sp = jnp.where(pm, s[:, 0:128], -jnp.inf)
                sc = jnp.where(cur_ok, s[:, 128:256], -jnp.inf)
                m = jnp.max(jnp.maximum(sp, sc), axis=-1, keepdims=True)
                p = jnp.concatenate([jnp.exp(sp - m), jnp.exp(sc - m)], axis=1).astype(BF16)
                ol = _dot(p, vext)
                mb = jnp.broadcast_to(m, (128, LANES))
                if hh == 0:
                    o_u, l_u, m_u = ol[:, 0:128], ol[:, 128:256], mb
                else:
                    o_u = jnp.where(heads[1], ol[:, 0:128], o_u)
                    l_u = jnp.where(heads[1], ol[:, 128:256], l_u)
                    m_u = jnp.where(heads[1], mb, m_u)
            og[cur, :] = o_u
            mg[cur, :] = m_u
            lg[cur, :] = l_u
            return carry

        lax.fori_loop(0, 16, unit, 0, unroll=16)
        _from_residues_dyn(g, o_t, og)
        _from_residues_dyn(g, m_t, mg)
        _from_residues_dyn(g, l_t, lg)

        @pl.when(g == 0)
        def _():
            o_acc[...] = o_t[...]
            m_acc[...] = m_t[...]
            l_acc[...] = l_t[...]

        @pl.when(g > 0)
        def _():
            m_new = jnp.maximum(m_acc[...], m_t[...])
            wa, wb = jnp.exp(m_acc[...] - m_new), jnp.exp(m_t[...] - m_new)
            o_acc[...] = o_acc[...] * wa + o_t[...] * wb
            l_acc[...] = l_acc[...] * wa + l_t[...] * wb
            m_acc[...] = m_new

        @pl.when(g == 2)
        def _():
            ob = o_acc[...] / l_acc[...]
            ob_ref[...] = ob
            lse_ref[...] = m_acc[...] + jnp.log(l_acc[...])
            obg_ref[...] = (ob * _silu(ag_ref[...])).astype(BF16)

    blk = pl.BlockSpec((S, LANES), lambda p, g: (0, p))
    buf = pltpu.VMEM((S, LANES), F32)
    return pl.pallas_call(
        body, name="attn_fwd", grid=(4, 3),
        out_shape=[jax.ShapeDtypeStruct((S, 512), F32), jax.ShapeDtypeStruct((S, 512), F32),
                   jax.ShapeDtypeStruct((S, 512), BF16)],
        in_specs=_attn_in_specs(0), out_specs=[blk, blk, blk],
        scratch_shapes=[buf, buf, pltpu.VMEM((ATT_PAD + S, LANES), BF16), pltpu.VMEM((ATT_PAD + S, 2 * LANES), BF16)] + [buf] * 9,
        compiler_params=_cp(("parallel", "arbitrary")),
    )(z, z, z, z, cc, ss)


def _tail(x, o_a, o_bg, z, target, w_a, w_b, w_out, fnw):
    tm = 256

    def body(x_ref, oa_ref, ob_ref, gpa_ref, gpb_ref, t_ref, wa_ref, wb_ref, wo_ref, fnw_ref,
             dx2_ref, dx2b_ref, dgp_ref, doa_ref, dob_ref, mg_ref, dya_ref, dyb_ref, small_ref):
        @pl.when(pl.program_id(0) == 0)
        def _():
            small_ref[...] = jnp.zeros_like(small_ref)

        wa, wb, wo = wa_ref[...], wb_ref[...], wo_ref[...]
        y_a = _dot(oa_ref[...], wa)
        y_b = _dot(ob_ref[...], wb)
        ga = jax.nn.sigmoid(gpa_ref[...])
        gb = jax.nn.sigmoid(gpb_ref[...])
        merged = (ga * y_a + gb * y_b).astype(BF16)
        x2 = x_ref[...] + _dot(merged, wo)
        r2 = lax.rsqrt(jnp.mean(x2 * x2, axis=-1, keepdims=True) + EPS)
        n2 = x2 * r2
        fw = fnw_ref[...]
        err = n2 * fw - t_ref[...]
        loss = 0.5 * jnp.sum(jnp.sum(err * err, axis=-1, keepdims=True), axis=0, keepdims=True) / D
        dy = err * (1.0 / D)
        g_fnw = jnp.sum(dy * n2, axis=0, keepdims=True)
        dn = dy * fw
        dx2 = r2 * (dn - n2 * jnp.mean(dn * n2, axis=-1, keepdims=True))
        dx2b = dx2.astype(BF16)
        dmerged = _dot_nt(dx2b, wo)
        dy_a = (dmerged * ga).astype(BF16)
        dy_b = (dmerged * gb).astype(BF16)
        dx2_ref[...] = dx2
        dx2b_ref[...] = dx2b
        dgp_ref[:, 0:D] = (dmerged * y_a * ga * (1.0 - ga)).astype(BF16)
        dgp_ref[:, D:2 * D] = (dmerged * y_b * gb * (1.0 - gb)).astype(BF16)
        doa_ref[...] = _dot_nt(dy_a, wa)
        dob_ref[...] = _dot_nt(dy_b, wb)
        mg_ref[...] = merged
        dya_ref[...] = dy_a
        dyb_ref[...] = dy_b
        small_ref[0:1, :] += g_fnw
        small_ref[1:2, :] += jnp.broadcast_to(loss, (1, D))

    def rows(cols, off=0):
        return pl.BlockSpec((tm, cols), lambda i: (i, off))

    def whole(shape):
        return pl.BlockSpec(shape, lambda i: (0, 0))

    return pl.pallas_call(
        body, name="tail", grid=(S // tm,),
        out_shape=[jax.ShapeDtypeStruct((S, D), F32), jax.ShapeDtypeStruct((S, D), BF16),
                   jax.ShapeDtypeStruct((S, 2 * D), BF16), jax.ShapeDtypeStruct((S, D), F32),
                   jax.ShapeDtypeStruct((S, 512), F32), jax.ShapeDtypeStruct((S, D), BF16),
                   jax.ShapeDtypeStruct((S, D), BF16), jax.ShapeDtypeStruct((S, D), BF16),
                   jax.ShapeDtypeStruct((8, D), F32)],
        in_specs=[rows(D), rows(D), rows(512), rows(D, 9), rows(D, 10), rows(D),
                  whole((D, D)), whole((512, D)), whole((D, D)), whole((1, D))],
        out_specs=[rows(D), rows(D), rows(2 * D), rows(D), rows(512), rows(D), rows(D), rows(D), whole((8, D))],
        compiler_params=_cp(("arbitrary",)),
    )(x, o_a, o_bg, z, z, target, w_a, w_b, w_out, fnw)


def _tn_matmul(a, b, name):
    m, n = a.shape[1], b.shape[1]
    tn = 512

    def body(a_ref, b_ref, o_ref, ob_ref):
        acc = _dot_tn(a_ref[...], b_ref[...])
        o_ref[...] = acc
        ob_ref[...] = acc.astype(BF16)

    out_blk = pl.BlockSpec((m, tn), lambda j: (0, j))
    return pl.pallas_call(
        body, name=name, grid=(n // tn,),
        out_shape=[jax.ShapeDtypeStruct((m, n), F32), jax.ShapeDtypeStruct((m, n), BF16)],
        in_specs=[pl.BlockSpec((S, m), lambda j: (0, 0)), pl.BlockSpec((S, tn), lambda j: (0, j))],
        out_specs=[out_blk, out_blk],
        compiler_params=_cp(("parallel",)),
    )(a, b)


def _hgrn_bwd(z, o, do_a, states, lbv, hnw, partials):
    ntb, nch = S // HBLK, HBLK // CHUNK
    n = len(GATHER_IDS)

    def body(hq_ref, hf_ref, hi_ref, hg_ref, o_ref, doa_ref, st_ref, lb_ref, hnw_ref, p0, p1, p2,
             dhq_ref, dhf_ref, dhi_ref, dhg_ref, glb_ref, ghn_ref, e0, e1, e2, dstate, send_sems, recv_sems):
        start, end = _exchange_chips_steps((p0, p1, p2), (e0, e1, e2), send_sems, recv_sems)

        @pl.when(pl.program_id(0) == 0)
        def _():
            dstate[...] = jnp.zeros_like(dstate)
            glb_ref[...] = jnp.zeros_like(glb_ref)
            ghn_ref[...] = jnp.zeros_like(ghn_ref)
            start()

        lb_a = lb_ref[...]
        hq_a, hg_a = hq_ref[...], hg_ref[...]
        q_a, k_a, g_a, v_a, sg_a, f_a, b_a = _hgrn_cols(hq_a, hf_ref[...], hi_ref[...], lb_a)
        bex_a = b_a - g_a
        eb_a = jnp.exp(b_a)
        w = hnw_ref[...]
        mask = _tril_mask(CHUNK)
        upper = _block_tri(CHUNK, CHUNK, upper=True)
        for h in range(HEADS):
            cols = slice(128 * h, 128 * h + 128)
            q, k, v, b, bex, eb = q_a[:, cols], k_a[:, cols], v_a[:, cols], b_a[:, cols], bex_a[:, cols], eb_a[:, cols]
            hq, hg, sg, f, lb = hq_a[:, cols], hg_a[:, cols], sg_a[:, cols], f_a[:, cols], lb_a[:, cols]
            ov, doa = o_ref[:, cols], doa_ref[:, cols]
            r = lax.rsqrt(jnp.mean(ov * ov, axis=-1, keepdims=True) + EPS)
            n = ov * r
            sil = _silu(hg)
            dhg_ref[:, cols] = (doa * n * w * _dsilu(hg)).astype(BF16)
            ghn_ref[h] += jnp.sum(doa * sil * n, axis=0, keepdims=True)
            dn = doa * sil * w
            do = r * (dn - n * jnp.mean(dn * n, axis=-1, keepdims=True))

            dst = dstate[h]
            dq_l, dk_l, dv_l, dg_l = [None] * nch, [None] * nch, [None] * nch, [None] * nch
            for c in reversed(range(nch)):
                r0 = c * CHUNK
                rows = slice(r0, r0 + CHUNK)
                st = st_ref[h, c]
                bc, kc, qc = b[rows], k[rows], q[rows]
                vb, dob = v[rows].astype(BF16), do[rows].astype(BF16)
                b_last = bc[CHUNK - 1:CHUNK]
                e_last = jnp.exp(b_last)
                ekl = jnp.exp(b_last - bc)
                dstb = dst.astype(BF16)
                a, qs_l, ks_l, ek_l, eq_l = _chunk_scores(q, k, b, bex, r0, mask)
                da = jnp.where(mask, _dot_nt(dob, vb), 0.0)
                dv_l[c] = _dot_tn(a.astype(BF16), dob) + _dot_nt((kc * ekl).astype(BF16), dstb)
                dq_inter = _dot(dob, st.astype(BF16)) * eb[rows]
                dk_state = _dot(vb, dstb) * ekl
                dq_parts, dk_intra = [], jnp.zeros((CHUNK, 128), F32)
                dab = da.astype(BF16)
                for i in range(CHUNK // SUB):
                    da_i = dab[SUB * i:SUB * (i + 1)]
                    ks_hi, ks_lo = _split2(ks_l[i])
                    qs_hi, qs_lo = _split2(qs_l[i])
                    dq_parts.append((_dot(da_i, ks_hi) + _dot(da_i, ks_lo)) * eq_l[i])
                    dk_intra = dk_intra + (_dot_tn(da_i, qs_hi) + _dot_tn(da_i, qs_lo)) * ek_l[i]
                dq = jnp.concatenate(dq_parts, axis=0) + dq_inter
                dk = dk_intra + dk_state
                last = (e_last * jnp.sum(st * dst, axis=0, keepdims=True)
                        + jnp.sum(kc * dk_state, axis=0, keepdims=True))
                dg_l[c] = _dot_ones(upper, qc * dq - kc * dk) + last
                dq_l[c], dk_l[c] = dq, dk
                dst = dst * e_last + _dot_tn(dob, (qc * eb[rows]).astype(BF16))
            dstate[h] = dst
            dq, dk = jnp.concatenate(dq_l, axis=0), jnp.concatenate(dk_l, axis=0)
            dg, dv = jnp.concatenate(dg_l, axis=0), jnp.concatenate(dv_l, axis=0)
            dhq_ref[:, cols] = (dq * _dsilu(hq)).astype(BF16)
            dhi_ref[:, cols] = dv.astype(BF16)
            df = dg / f - dk
            dhf_ref[:, cols] = (df * (1.0 - lb) * sg * (1.0 - sg)).astype(BF16)
            glb_ref[:, cols] += jnp.sum(df * (1.0 - sg), axis=0, keepdims=True)

        pl.when(pl.program_id(0) == ntb - 1)(end)

    def rev(t):
        return ntb - 1 - t

    def zcol(j):
        return pl.BlockSpec((HBLK, D), lambda t: (rev(t), j))

    blk = pl.BlockSpec((HBLK, D), lambda t: (rev(t), 0))
    any_spec = pl.BlockSpec(memory_space=pl.ANY)
    return pl.pallas_call(
        body, name="hgrn_bwd", grid=(ntb,),
        out_shape=[jax.ShapeDtypeStruct((S, D), BF16)] * 4
        + [jax.ShapeDtypeStruct((1, D), F32), jax.ShapeDtypeStruct((HEADS, 1, 128), F32)]
        + [jax.ShapeDtypeStruct((3,) + SHARD_SHAPES[a], BF16) for a in GATHER_IDS],
        in_specs=[zcol(0), zcol(1), zcol(2), zcol(3), blk, blk,
                  pl.BlockSpec((HEADS, nch, 128, 128), lambda t: (0, rev(t), 0, 0)),
                  pl.BlockSpec((1, D), lambda t: (0, 0)), pl.BlockSpec((1, 128), lambda t: (0, 0))] + [any_spec] * n,
        out_specs=[blk] * 4 + [pl.BlockSpec((1, D), lambda t: (0, 0)),
                               pl.BlockSpec((HEADS, 1, 128), lambda t: (0, 0, 0))] + [any_spec] * n,
        scratch_shapes=[pltpu.VMEM((HEADS, 128, 128), F32), pltpu.SemaphoreType.DMA((3 * n,)),
                        pltpu.SemaphoreType.DMA((3 * n,))],
        compiler_params=_cp(("arbitrary",)),
    )(z, z, z, z, o, do_a, states, lbv, hnw, *partials)


def _attn_bwd(z, cc, ss, ob, lse, do_bg):
    def body(q_ref, k_ref, v_ref, ag_ref, cc_ref, ss_ref, ob_ref, lse_ref, dobg_ref,
             dq_ref, dk_ref, dv_ref, dag_ref,
             tmp, qs, ks, vs, dos, dqs, dks, dvs, dkp, dvp, do_t, ls0_t, ls1_t, dl0_t, dl1_t, ls0, ls1, dl0, dl1):
        g = pl.program_id(1)
        first_half = _half_mask()
        prev_ok, cur_ok = _attn_masks()
        lane = lax.broadcasted_iota(jnp.int32, (1, LANES), 1)
        heads = (lane < 64, lane >= 64)
        nblk = _group_blocks(g)
        cc_v, ss_v = cc_ref[...], ss_ref[...]

        @pl.when(g == 0)
        def _():
            ag, obv, dobg = ag_ref[...], ob_ref[...], dobg_ref[...]
            dag_ref[...] = (dobg * obv * _dsilu(ag)).astype(BF16)
            dob = dobg * _silu(ag)
            do_t[...] = dob
            prod = dob * obv
            dl = jnp.concatenate(
                [jnp.broadcast_to(jnp.sum(prod[:, 0:64], axis=-1, keepdims=True), (S, 64)),
                 jnp.broadcast_to(jnp.sum(prod[:, 64:128], axis=-1, keepdims=True), (S, 64))], axis=1)
            dl_sw = pltpu.roll(dl, 64, 1)
            dl0_t[...] = jnp.where(heads[0], dl, dl_sw)
            dl1_t[...] = jnp.where(heads[0], dl_sw, dl)
            ls = lse_ref[...]
            ls_sw = pltpu.roll(ls, 64, 1)
            ls0_t[...] = jnp.where(heads[0], ls, ls_sw)
            ls1_t[...] = jnp.where(heads[0], ls_sw, ls)
            ks[0:ATT_PAD, :] = jnp.zeros((ATT_PAD, LANES), BF16)
            vs[0:ATT_PAD, :] = jnp.zeros((ATT_PAD, LANES), BF16)

        tmp[...] = _rope(q_ref[...], cc_v, ss_v, first_half) * ATT_SCALE
        _to_residues_dyn(g, qs, tmp)
        tmp[...] = _rope(k_ref[...], cc_v, ss_v, first_half)
        _to_residues_dyn(g, ks, tmp, ATT_PAD, BF16)
        _to_residues_dyn(g, vs, v_ref, ATT_PAD, BF16)
        _to_residues_dyn(g, dos, do_t)
        _to_residues_dyn(g, ls0, ls0_t)
        _to_residues_dyn(g, ls1, ls1_t)
        _to_residues_dyn(g, dl0, dl0_t)
        _to_residues_dyn(g, dl1, dl1_t)
        lss, dls = (ls0, ls1), (dl0, dl1)

        def unit(u, carry):
            start = pl.multiple_of(u * 128, 128)
            cur = pl.ds(start, 128)
            both = pl.ds(start, 256)
            pm = prev_ok & ((u & (nblk - 1)) != 0)
            qu, dou = qs[cur, :], dos[cur, :]
            kcat, vcat = ks[both, :], vs[both, :]
            dq_u = None
            q_l, do_l, ds_l, p_l = [], [], [], []
            for hh in range(2):
                q_h = jnp.where(heads[hh], qu, 0.0).astype(BF16)
                do_h = jnp.where(heads[hh], dou, 0.0).astype(BF16)
                s = _dot_nt(q_h, kcat)
                dp = _dot_nt(do_h, vcat)
                lse_h, dl_h = lss[hh][cur, :], dls[hh][cur, :]
                pp = jnp.where(pm, jnp.exp(s[:, 0:128] - lse_h), 0.0)
                pc = jnp.where(cur_ok, jnp.exp(s[:, 128:256] - lse_h), 0.0)
                ds = jnp.concatenate([pp * (dp[:, 0:128] - dl_h), pc * (dp[:, 128:256] - dl_h)], axis=1).astype(BF16)
                dq = _dot(ds, kcat)
                dq_u = dq if hh == 0 else jnp.where(heads[1], dq, dq_u)
                q_l.append(q_h)
                do_l.append(do_h)
                ds_l.append(ds)
                p_l.append(jnp.concatenate([pp, pc], axis=1).astype(BF16))
            dkcat = _dot_tn(jnp.concatenate(ds_l, axis=0), jnp.concatenate(q_l, axis=0))
            dvcat = _dot_tn(jnp.concatenate(p_l, axis=0), jnp.concatenate(do_l, axis=0))
            dkp[cur, :] = dkcat[0:128]
            dks[cur, :] = dkcat[128:256]
            dvp[cur, :] = dvcat[0:128]
            dvs[cur, :] = dvcat[128:256]
            dqs[cur, :] = dq_u
            return carry

        lax.fori_loop(0, 16, unit, 0, unroll=ATT_UNROLL)
        dks[0:S - 128, :] += dkp[128:S, :]
        dvs[0:S - 128, :] += dvp[128:S, :]
        _from_residues_dyn(g, tmp, dqs)
        dq_ref[0] = (_rope(tmp[...], cc_v, -ss_v, first_half) * ATT_SCALE).astype(BF16)
        _from_residues_dyn(g, tmp, dks)
        dk_ref[0] = _rope(tmp[...], cc_v, -ss_v, first_half).astype(BF16)
        _from_residues_dyn(g, tmp, dvs)
        dv_ref[0] = tmp[...].astype(BF16)

    grp = pl.BlockSpec((1, S, LANES), lambda p, g: (g, 0, p))
    buf = pltpu.VMEM((S, LANES), F32)
    padded_b = pltpu.VMEM((ATT_PAD + S, LANES), BF16)
    return pl.pallas_call(
        body, name="attn_bwd", grid=(4, 3),
        out_shape=[jax.ShapeDtypeStruct((3, S, 512), BF16)] * 3 + [jax.ShapeDtypeStruct((S, 512), BF16)],
        in_specs=_attn_in_specs(3), out_specs=[grp, grp, grp, pl.BlockSpec((S, LANES), lambda p, g: (0, p))],
        scratch_shapes=[buf, buf, padded_b, padded_b] + [buf] * 15,
        compiler_params=_cp(("parallel", "arbitrary")),
    )(z, z, z, z, cc, ss, ob, lse, do_bg)


def _in_proj_bwd(dz, h, w_in):
    half = S // 2
    slab = (D, SHARD_COLS)

    def body(dz_hbm, h_hbm, w_hbm, dh_hbm, g_chip, r1_hbm, relay_hbm, r2_hbm,
             h_buf, dz_buf, stage_d, r1_buf, stage_i, acc,
             dz_sem, w_sem, h_sem, r1_sem, out_sem, send_d, recv_d, send_i, recv_i):
        x, y, c = _mesh_pos()
        sibling = (x, y, 1 - c)
        north = c == 1
        near = (jnp.where(north, 1 - x, x), jnp.where(north, y, 1 - y))
        far = (jnp.where(north, x, 1 - x), jnp.where(north, 1 - y, y))
        chips = [(1 - x, 1 - y), near, far, (x, y)]

        def cols(d):
            return pl.ds(pl.multiple_of(d * SHARD_COLS, LANES), SHARD_COLS)

        blocks = []
        for q_sib, q in zip([chips[0], far, near, chips[3]], chips):
            blocks += [4 * q_sib[0] + 2 * q_sib[1] + (1 - c), 4 * q[0] + 2 * q[1] + c]

        def dz_tile(t):
            return _SplitCopy(dz_hbm.at[pl.ds((t % 2) * half, half), cols(blocks[t // 2])],
                                         dz_buf.at[t % 2], dz_sem.at[t % 2])

        def to_sibling(i):
            return pltpu.make_async_remote_copy(
                src_ref=stage_d.at[i % 2], dst_ref=r1_hbm.at[i], send_sem=send_d.at[i], recv_sem=recv_d.at[i],
                device_id=sibling, device_id_type=MESH)

        def to_owner(i):
            dst = relay_hbm if i == 0 else r2_hbm.at[i - 1]
            return pltpu.make_async_remote_copy(
                src_ref=stage_i.at[i], dst_ref=dst, send_sem=send_i.at[i], recv_sem=recv_i.at[i],
                device_id=(*(far if i == 2 else near), c), device_id_type=MESH)

        h_copy = _SplitCopy(h_hbm, h_buf, h_sem)
        h_copy.start()
        dz_tile(0).start()
        h_copy.wait()
        for b in range(8):
            i = b // 2
            g = None
            for r in range(2):
                t = 2 * b + r
                if t + 1 < 16:
                    dz_tile(t + 1).start()
                dz_tile(t).wait()
                part = _dot_tn(h_buf[r * half:(r + 1) * half, :], dz_buf[t % 2])
                g = part if g is None else g + part
                if b % 2 == 1 and r == 0:
                    to_sibling(i).wait_recv()
                    r1_copy = _SplitCopy(r1_hbm.at[i], r1_buf, r1_sem)
                    r1_copy.start()
            if b % 2 == 0:
                if i >= 2:
                    to_sibling(i - 2).wait_send()
                stage_d[i % 2] = g.astype(BF16)
                to_sibling(i).start()
            else:
                r1_copy.wait()
                g = g + r1_buf[...].astype(F32)
                if i == 2:
                    to_owner(0).wait_recv()
                    relay_copy = _SplitCopy(relay_hbm, r1_buf, r1_sem)
                    relay_copy.start()
                    relay_copy.wait()
                    g = g + r1_buf[...].astype(F32)
                if i < 3:
                    stage_i[i] = g.astype(BF16)
                    to_owner(i).start()
                else:
                    g_chip[...] = g
        to_sibling(2).wait_send()
        to_sibling(3).wait_send()

        def dz2(t):
            return _SplitCopy(
                dz_hbm.at[pl.ds((t % 2) * half, half), pl.ds((t // 2) * SHARD_COLS, SHARD_COLS)],
                dz_buf.at[t % 2], dz_sem.at[t % 2])

        def w2(b):
            return _SplitCopy(w_hbm.at[:, pl.ds(b * SHARD_COLS, SHARD_COLS)],
                                         stage_d.at[b % 2], w_sem.at[b % 2])

        dz2(0).start()
        w2(0).start()
        for t in range(16):
            b, r = t // 2, t % 2
            if t + 1 < 16:
                dz2(t + 1).start()
            if r == 0:
                if b + 1 < 8:
                    w2(b + 1).start()
                w2(b).wait()
            dz2(t).wait()
            part = _dot_nt(dz_buf[t % 2], stage_d[b % 2])
            if b == 0:
                acc[r] = part
            else:
                acc[r] += part
        dh_out = [_SplitCopy(acc.at[r], dh_hbm.at[pl.ds(r * half, half), :], out_sem.at[r])
                  for r in range(2)]
        for cp in dh_out:
            cp.start()
        for cp in dh_out:
            cp.wait()
        for i in range(3):
            to_owner(i).wait_send()
        for i in (1, 2):
            to_owner(i).wait_recv()

    any_spec = pl.BlockSpec(memory_space=pl.ANY)
    return pl.pallas_call(
        body, name="in_proj_bwd",
        out_shape=[jax.ShapeDtypeStruct((S, D), F32), jax.ShapeDtypeStruct(slab, F32),
                   jax.ShapeDtypeStruct((4,) + slab, BF16), jax.ShapeDtypeStruct(slab, BF16),
                   jax.ShapeDtypeStruct((2,) + slab, BF16)],
        in_specs=[any_spec] * 3,
        out_specs=[any_spec, pl.BlockSpec(memory_space=pltpu.VMEM), any_spec, any_spec, any_spec],
        scratch_shapes=[pltpu.VMEM((S, D), BF16), pltpu.VMEM((2, half, SHARD_COLS), BF16),
                        pltpu.VMEM((2,) + slab, BF16), pltpu.VMEM(slab, BF16), pltpu.VMEM((3,) + slab, BF16),
                        pltpu.VMEM((2, half, D), F32),
                        pltpu.SemaphoreType.DMA((2,)), pltpu.SemaphoreType.DMA((2,)), pltpu.SemaphoreType.DMA,
                        pltpu.SemaphoreType.DMA, pltpu.SemaphoreType.DMA((2,)),
                        pltpu.SemaphoreType.DMA((4,)), pltpu.SemaphoreType.DMA((4,)),
                        pltpu.SemaphoreType.DMA((3,)), pltpu.SemaphoreType.DMA((3,))],
        compiler_params=_cp(),
    )(dz, h, w_in)


def _grad_x(x, norm_w, dh, dx2):
    tr = 256

    def body(x_ref, w_ref, dh_ref, dx2_ref, gx_ref, gnw_ref):
        @pl.when(pl.program_id(0) == 0)
        def _():
            gnw_ref[...] = jnp.zeros_like(gnw_ref)

        xv, dhv = x_ref[...], dh_ref[...]
        r = lax.rsqrt(jnp.mean(xv * xv, axis=-1, keepdims=True) + EPS)
        n = xv * r
        gnw_ref[...] += jnp.sum(dhv * n, axis=0, keepdims=True)
        dn = dhv * w_ref[...]
        gx_ref[...] = dx2_ref[...] + r * (dn - n * jnp.mean(dn * n, axis=-1, keepdims=True))

    row = pl.BlockSpec((tr, D), lambda i: (i, 0))
    vec = pl.BlockSpec((1, D), lambda i: (0, 0))
    return pl.pallas_call(
        body, name="grad_x", grid=(S // tr,),
        out_shape=[jax.ShapeDtypeStruct((S, D), F32), jax.ShapeDtypeStruct((1, D), F32)],
        in_specs=[row, vec, row, row], out_specs=[row, vec],
        compiler_params=_cp(("arbitrary",)),
    )(x, norm_w, dh, dx2)


def _rope_tables(positions):
    inv_freq = 10000.0 ** (-jnp.arange(0, 64, 2, dtype=F32) / 64)
    ang = positions.astype(F32)[:, None] * inv_freq[None, :]
    cos, sin = jnp.cos(ang), jnp.sin(ang)
    return jnp.tile(cos, (1, 4)), jnp.tile(jnp.concatenate([-sin, sin], axis=1), (1, 2))


def _local_step(x, positions, norm_w, lb_logits, hnw, fnw, target, w_in_shard, small_shards, core):
    cc, ss = _rope_tables(positions)
    lbv = jax.nn.sigmoid(lb_logits[0:1] - lb_logits[1:2])
    h = _rmsnorm_in(x, norm_w)
    z, w_in = _in_proj_gather(h, w_in_shard)
    o, o_a, states, w_a, w_b, w_out = _hgrn_fwd(z, lbv, hnw, small_shards)
    ob, lse, o_bg = _attn_fwd(z, cc, ss)
    dx2, dx2b, dgp, do_a, do_bg, merged, dy_a, dy_b, tail_small = _tail(x, o_a, o_bg, z, target, w_a, w_b, w_out, fnw)
    g_out, gb_out = _tn_matmul(merged, dx2b, "grad_w_out")
    g_a, gb_a = _tn_matmul(o_a, dy_a, "grad_w_a")
    g_b, gb_b = _tn_matmul(o_bg, dy_b, "grad_w_b")
    grads, gb = (g_a, g_b, g_out), (gb_a, gb_b, gb_out)
    r1 = _exchange_sibling(GATHER_IDS, gb)
    pb = [_chip_partials(a, grads[i], r1[i], core) for i, a in enumerate(GATHER_IDS)]
    dhq, dhf, dhi, dhg, glb, ghn, *r2 = _hgrn_bwd(z, o, do_a, states, lbv, hnw, pb)
    dq, dk, dv, dag = _attn_bwd(z, cc, ss, ob, lse, do_bg)
    dz = jnp.concatenate([dhq, dhf, dhi, dhg, dq[0], dq[1], dq[2], dk[0], dk[1], dk[2], dv[0], dv[1], dv[2], dag, dgp],
                         axis=1)
    dh, g_chip_in, _, _, r2_in = _in_proj_bwd(dz, h, w_in)
    grad_x, gnw = _grad_x(x, norm_w, dh, dx2)
    ghn_row = jnp.pad(jnp.sum(ghn, axis=0), ((0, 0), (0, D - 128)))
    small = jnp.concatenate([gnw, glb, ghn_row, tail_small[0:2], jnp.zeros((3, D), F32)], axis=0)
    return grad_x, (g_chip_in, r2_in), grads, r1, r2, small


def kernel(x, positions, norm_w, w_in, lb_logits, hgrn_norm_w, w_branch_a, w_branch_b, w_out, final_norm_w, loss_target, m_norm_w, m_w_in, m_lb_logits, m_hgrn_norm_w, m_w_branch_a, m_w_branch_b, m_w_out, m_final_norm_w, v_norm_w, v_w_in, v_lb_logits, v_hgrn_norm_w, v_w_branch_a, v_w_branch_b, v_w_out, v_final_norm_w):
    ix, iy, ic = _mesh_pos()
    core = jnp.reshape(ic, (1,)).astype(jnp.int32)
    pos = jnp.stack([4 * ix + 2 * iy + ic, 2 * ix + iy]).astype(jnp.int32)

    shards = [w_in[0], w_branch_a[0], w_branch_b[0], w_out[0]]
    moments_m = [m_w_in[0], m_w_branch_a[0], m_w_branch_b[0], m_w_out[0]]
    moments_v = [v_w_in[0], v_w_branch_a[0], v_w_branch_b[0], v_w_out[0]]
    names = ("w_in", "w_a", "w_b", "w_out")
    ids = GATHER_IDS
    shards_b = [_cast_bf16(w, f"cast_{nm}") for w, nm in zip(shards, names)]

    fnw2 = final_norm_w.reshape(1, D)
    grad_x, (g_chip_in, r2_in), grads, r1, r2, small = _local_step(
        x[0], positions[0], norm_w, lb_logits, hgrn_norm_w, fnw2, loss_target[0], shards_b[0], shards_b[1:], core)

    gathered = _gather_small(small)
    big =[_reduce_own_and_update(shards[0], moments_m[0], moments_v[0], g_chip_in, r2_in)]
    big += [_reduce_and_update(a, shards[a], moments_m[a], moments_v[a], grads[i], r1[i], r2[i], pos)
            for i, a in enumerate(ids)]
    sm = _small_update(gathered, norm_w, lb_logits, hgrn_norm_w, fnw2,
                       (m_norm_w, m_lb_logits, m_hgrn_norm_w, m_final_norm_w.reshape(1, D),
                        v_norm_w, v_lb_logits, v_hgrn_norm_w, v_final_norm_w.reshape(1, D)))
    loss = sm[0][0, 0]
    outs = [loss, grad_x[None]]
    for kind in range(4):
        s_nw, s_lb, s_hn, s_fn = sm[1 + 4 * kind:5 + 4 * kind]
        outs += [s_nw, big[0][kind][None], s_lb, s_hn, big[1][kind][None], big[2][kind][None],
                 big[3][kind][None], s_fn.reshape(D)]
    return tuple(outs)
```

```python
import functools

import jax
import jax.numpy as jnp
from jax import lax
from jax.experimental import pallas as pl
from jax.experimental.pallas import tpu as pltpu

F32 = jnp.float32
BF16 = jnp.bfloat16
MESH = pl.DeviceIdType.MESH

S = 2048
D = 1024
NDEV = 8
HEADS = 8
CHUNK = 64
SUB = 16
HBLK = 256
ATT_PAD = 128
ATT_UNROLL = 16
COPY_PARTS = 4
EXP_CLAMP = 80.0
EPS = 1e-6
IN_COLS = 11264
SHARD_COLS = IN_COLS // NDEV
DZ_ATT = 4096
DZ_GATES = 9216
ATT_DILS = (1, 4, 16)
ATT_SCALE = 64 ** -0.5
LANES = 128

ADAM_LR, ADAM_B1, ADAM_B2, ADAM_EPS, ADAM_WD, ADAM_STEP = 0.001, 0.9, 0.999, 1e-08, 0.01, 10

VMEM_LIMIT = 56 * 1024 * 1024


def _cp(sem=None, **kw):
    return pltpu.CompilerParams(dimension_semantics=sem, vmem_limit_bytes=VMEM_LIMIT, **kw)


def _dot(a, b):
    return jnp.dot(a, b, preferred_element_type=F32)


def _dot_nt(a, b):
    return lax.dot_general(a, b, (((1,), (1,)), ((), ())), preferred_element_type=F32)


def _dot_tn(a, b):
    return lax.dot_general(a, b, (((0,), (0,)), ((), ())), preferred_element_type=F32)


def _split2(x):
    hi = x.astype(BF16)
    lo = (x - hi.astype(F32)).astype(BF16)
    return hi, lo


def _split3(x):
    hi = x.astype(BF16)
    r = x - hi.astype(F32)
    mid = r.astype(BF16)
    lo = (r - mid.astype(F32)).astype(BF16)
    return hi, mid, lo


def _dot_ones(ones_bf16, x):
    hi, mid, lo = _split3(x)
    return _dot(ones_bf16, hi) + _dot(ones_bf16, mid) + _dot(ones_bf16, lo)


def _silu(x):
    return x * jax.nn.sigmoid(x)


def _dsilu(x):
    s = jax.nn.sigmoid(x)
    return s * (1.0 + x * (1.0 - s))


def _mesh_pos():
    return lax.axis_index("x"), lax.axis_index("y"), lax.axis_index("c")


class _SplitCopy:
    def __init__(self, src, dst, sem):
        self.src, self.dst, self.sem = src, dst, sem

    def start(self):
        rows = self.src.shape[0] // COPY_PARTS
        for p in range(COPY_PARTS):
            chunk = pl.ds(p * rows, rows)
            pltpu.make_async_copy(self.src.at[chunk], self.dst.at[chunk], self.sem).start()

    def wait(self):
        pltpu.make_async_copy(self.src, self.dst, self.sem).wait()


def _shard_of(ref, a, d):
    if a == 0:
        return ref.at[:, pl.ds(pl.multiple_of(d * SHARD_COLS, LANES), SHARD_COLS)]
    if a == 2:
        return ref.at[:, pl.ds(pl.multiple_of(d * LANES, LANES), LANES)]
    return ref.at[pl.ds(pl.multiple_of(d * 128, 128), 128), :]


FULL_SHAPES = ((D, IN_COLS), (D, D), (512, D), (D, D))
SHARD_SHAPES = ((D, SHARD_COLS), (128, D), (512, 128), (128, D))


def _allgather_steps(ids, ins, outs, send_sems, recv_sems, local_sems):
    n = len(ids)
    x, y, c = _mesh_pos()
    me, sibling = (x, y, c), (x, y, 1 - c)
    chips = [(1 - x, y), (x, 1 - y), (1 - x, 1 - y)]

    def blk(a, p):
        return _shard_of(outs[a], ids[a], 4 * p[0] + 2 * p[1] + p[2])

    def copy(a, k, block, to, src=None):
        return pltpu.make_async_remote_copy(
            src_ref=blk(a, block) if src is None else src, dst_ref=blk(a, block),
            send_sem=send_sems.at[a * 7 + k], recv_sem=recv_sems.at[a * 7 + k],
            device_id=to, device_id_type=MESH)

    mine = [pltpu.make_async_copy(ins[a], blk(a, me), local_sems.at[a]) for a in range(n)]
    first = []
    for a in range(n):
        first += [copy(a, 1 + j, me, (*chip, c), src=ins[a]) for j, chip in enumerate(chips)]
    for a in range(n):
        first.append(copy(a, 0, me, sibling, src=ins[a]))
    passed = [copy(a, 4 + j, (*chip, c), sibling) for j, chip in enumerate(chips) for a in range(n)]

    def start():
        for cp in mine + first:
            cp.start()

    def middle():
        for j, chip in enumerate(chips):
            for a in range(n):
                copy(a, 1 + j, (*chip, c), me).wait_recv()
                passed[j * n + a].start()

    def end():
        for a in range(n):
            copy(a, 0, sibling, me).wait_recv()
        for j, chip in enumerate(chips):
            for a in range(n):
                copy(a, 4 + j, (*chip, 1 - c), me).wait_recv()
        for cp in first + passed:
            cp.wait_send()
        for cp in mine:
            cp.wait()

    return start, middle, end


def _in_proj_gather(h, w_shard):
    half = S // 2
    slab = (D, SHARD_COLS)

    def body(h_hbm, w_hbm, z_hbm, wfull_hbm, h_buf, land, zstage,
             h_sem, own_sem, z_sem, wout_sem, send_sems, recv_sems):
        x, y, c = _mesh_pos()
        sibling = (x, y, 1 - c)
        north = c == 1

        def chips_of(first_x):
            near = (jnp.where(first_x, 1 - x, x), jnp.where(first_x, y, 1 - y))
            far = (jnp.where(first_x, x, 1 - x), jnp.where(first_x, 1 - y, y))
            return [near, far, (1 - x, 1 - y)]

        mine, theirs = chips_of(north), chips_of(jnp.logical_not(north))

        def dev(chip, core):
            return 4 * chip[0] + 2 * chip[1] + core

        block_of = ([dev((x, y), c), dev((x, y), 1 - c)] + [dev(q, c) for q in mine]
                    + [dev(q, 1 - c) for q in theirs])

        def cols(d):
            if isinstance(d, int):
                return pl.ds(d * SHARD_COLS, SHARD_COLS)
            return pl.ds(pl.multiple_of(d * SHARD_COLS, LANES), SHARD_COLS)

        def send(k, src, dst_slot, to):
            return pltpu.make_async_remote_copy(
                src_ref=src, dst_ref=land.at[dst_slot], send_sem=send_sems.at[k], recv_sem=recv_sems.at[k],
                device_id=to, device_id_type=MESH)

        def to_sibling():
            return send(0, w_hbm, 1, sibling)

        def to_chip(j):
            if j == 2:
                return send(3, land.at[2], 4, (*mine[1], c))
            return send(1 + j, w_hbm, 2 + j, (*mine[j], c))

        def pass_on(j):
            return send(4 + j, land.at[2 + j], 5 + j, sibling)

        own = _SplitCopy(w_hbm, land.at[0], own_sem)
        h_copy = _SplitCopy(h_hbm, h_buf, h_sem)
        own.start()
        h_copy.start()
        to_sibling().start()
        to_chip(0).start()
        h_copy.wait()
        own.wait()

        def multiply(slot, n_done):
            d = block_of[slot]
            out = _SplitCopy(land.at[slot], wfull_hbm.at[:, cols(d)], wout_sem.at[slot])
            out.start()
            for r in range(2):
                rows = pl.ds(r * half, half)
                zc = _SplitCopy(zstage.at[r], z_hbm.at[rows, cols(d)], z_sem.at[r])
                if n_done > 0:
                    zc.wait()
                zstage[r] = _dot(h_buf[r * half:(r + 1) * half, :], land[slot])
                zc.start()
            return out

        outs = [multiply(0, 0)]
        to_sibling().wait_recv()
        outs.append(multiply(1, 1))
        done = 2
        for j in range(3):
            to_chip(j).wait_recv()
            pass_on(j).start()
            to_chip(j).wait_send()
            if j < 2:
                to_chip(j + 1).start()
            outs.append(multiply(2 + j, done))
            pass_on(j).wait_recv()
            outs.append(multiply(5 + j, done + 1))
            done += 2
        for r in range(2):
            _SplitCopy(zstage.at[r], z_hbm.at[pl.ds(r * half, half), cols(0)], z_sem.at[r]).wait()
        for out in outs:
            out.wait()
        to_sibling().wait_send()
        for j in range(3):
            pass_on(j).wait_send()

    any_spec = pl.BlockSpec(memory_space=pl.ANY)
    return pl.pallas_call(
        body, name="in_proj_gather",
        out_shape=[jax.ShapeDtypeStruct((S, IN_COLS), F32), jax.ShapeDtypeStruct((D, IN_COLS), BF16)],
        in_specs=[any_spec] * 2, out_specs=[any_spec] * 2,
        scratch_shapes=[pltpu.VMEM((S, D), BF16), pltpu.VMEM((8,) + slab, BF16), pltpu.VMEM((2, half, SHARD_COLS), F32),
                        pltpu.SemaphoreType.DMA, pltpu.SemaphoreType.DMA, pltpu.SemaphoreType.DMA((2,)),
                        pltpu.SemaphoreType.DMA((8,)), pltpu.SemaphoreType.DMA((7,)), pltpu.SemaphoreType.DMA((7,))],
        compiler_params=_cp(),
    )(h, w_shard)


def _exchange_sibling(ids, gb):
    n = len(gb)

    def body(*refs):
        ins, outs = refs[:n], refs[n:2 * n]
        send_sems, recv_sems = refs[2 * n:]
        x, y, c = _mesh_pos()
        sibling = (x, y, 1 - c)
        copies = []
        for i, a in enumerate(ids):
            for q in range(4):
                copies.append(pltpu.make_async_remote_copy(
                    src_ref=_shard_of(ins[i], a, 2 * q + (1 - c)), dst_ref=outs[i].at[q],
                    send_sem=send_sems.at[i * 4 + q], recv_sem=recv_sems.at[i * 4 + q],
                    device_id=sibling, device_id_type=MESH))
        for cp in copies:
            cp.start()
        for cp in copies:
            cp.wait()

    any_spec = pl.BlockSpec(memory_space=pl.ANY)
    return pl.pallas_call(
        body, name="grads_to_sibling",
        out_shape=[jax.ShapeDtypeStruct((4,) + SHARD_SHAPES[a], BF16) for a in ids],
        in_specs=[any_spec] * n, out_specs=[any_spec] * n,
        scratch_shapes=[pltpu.SemaphoreType.DMA((4 * n,)), pltpu.SemaphoreType.DMA((4 * n,))],
    )(*gb)


def _exchange_chips_steps(ins, outs, send_sems, recv_sems):
    x, y, c = _mesh_pos()
    chips = [(1 - x, y), (x, 1 - y), (1 - x, 1 - y)]
    copies = []
    for a in range(len(ins)):
        for k, chip in enumerate(chips):
            copies.append(pltpu.make_async_remote_copy(
                src_ref=ins[a].at[2 * chip[0] + chip[1]], dst_ref=outs[a].at[k],
                send_sem=send_sems.at[a * 3 + k], recv_sem=recv_sems.at[a * 3 + k],
                device_id=(*chip, c), device_id_type=MESH))

    def start():
        for cp in copies:
            cp.start()

    def end():
        for cp in copies:
            cp.wait()

    return start, end


def _gather_small(small):
    def body(small_ref, small_out, ssend, srecv, local_sem):
        x, y, c = _mesh_pos()
        me = 4 * x + 2 * y + c
        copies = []
        for r in range(1, NDEV):
            peer = (1 - x if r & 4 else x, 1 - y if r & 2 else y, 1 - c if r & 1 else c)
            copies.append(pltpu.make_async_remote_copy(
                src_ref=small_ref, dst_ref=small_out.at[me],
                send_sem=ssend.at[r - 1], recv_sem=srecv.at[r - 1],
                device_id=peer, device_id_type=MESH))
        own = pltpu.make_async_copy(small_ref, small_out.at[me], local_sem)
        own.start()
        for cp in copies:
            cp.start()
        for cp in copies:
            cp.wait()
        own.wait()

    any_spec = pl.BlockSpec(memory_space=pl.ANY)
    return pl.pallas_call(
        body, name="gather_small",
        out_shape=jax.ShapeDtypeStruct((NDEV,) + small.shape, F32),
        in_specs=[any_spec], out_specs=any_spec,
        scratch_shapes=[pltpu.SemaphoreType.DMA((NDEV - 1,)), pltpu.SemaphoreType.DMA((NDEV - 1,)),
                        pltpu.SemaphoreType.DMA],
    )(small)


def _shard_tiles(a):
    rows, cols = SHARD_SHAPES[a]
    tr = min(rows, 256)
    return (tr, cols), rows // tr


def _full_index(a, d, i):
    (tr, _), nt = _shard_tiles(a)
    if a in (0, 2):
        return (i, d)
    return (d * nt + i, 0)


def _cast_bf16(x, name):
    rows, cols = x.shape
    tr = min(rows, 256)

    def body(x_ref, o_ref):
        o_ref[...] = x_ref[...].astype(BF16)

    return pl.pallas_call(
        body, name=name, out_shape=jax.ShapeDtypeStruct(x.shape, BF16), grid=(rows // tr,),
        in_specs=[pl.BlockSpec((tr, cols), lambda i: (i, 0))],
        out_specs=pl.BlockSpec((tr, cols), lambda i: (i, 0)),
        compiler_params=_cp(("parallel",)),
    )(x)


def _chip_partials(a, g_full, r1, core):
    tile, nt = _shard_tiles(a)

    def body(c_ref, g_ref, r_ref, o_ref):
        o_ref[0] = (g_ref[...] + r_ref[0].astype(F32)).astype(BF16)

    grid_spec = pltpu.PrefetchScalarGridSpec(
        num_scalar_prefetch=1, grid=(4, nt),
        in_specs=[pl.BlockSpec(tile, lambda q, i, c: _full_index(a, 2 * q + c[0], i)),
                  pl.BlockSpec((1,) + tile, lambda q, i, c: (q, i, 0))],
        out_specs=pl.BlockSpec((1,) + tile, lambda q, i, c: (q, i, 0)))
    return pl.pallas_call(
        body, name=f"chip_partials_{a}", grid_spec=grid_spec,
        out_shape=jax.ShapeDtypeStruct((4,) + SHARD_SHAPES[a], BF16),
        compiler_params=_cp(("parallel", "parallel")),
    )(core, g_full, r1)


def _adam(w, g, m, v):
    m = ADAM_B1 * m + (1.0 - ADAM_B1) * g
    v = ADAM_B2 * v + (1.0 - ADAM_B2) * (g * g)
    m_hat = m / (1.0 - ADAM_B1 ** ADAM_STEP)
    v_hat = v / (1.0 - ADAM_B2 ** ADAM_STEP)
    delta = -ADAM_LR * (m_hat / (jnp.sqrt(v_hat) + ADAM_EPS) + ADAM_WD * w)
    return delta, m, v


def _reduce_and_update(a, w, m, v, g_full, r1, r2, pos):
    tile, nt = _shard_tiles(a)

    def body(p_ref, w_ref, m_ref, v_ref, g_ref, r1_ref, r2_ref, go_ref, do_ref, mo_ref, vo_ref):
        g = g_ref[...] + r1_ref[0].astype(F32)
        g = g + r2_ref[0].astype(F32)
        g = g + r2_ref[1].astype(F32)
        g = g + r2_ref[2].astype(F32)
        delta, m_new, v_new = _adam(w_ref[...], g, m_ref[...], v_ref[...])
        go_ref[...] = g
        do_ref[...] = delta
        mo_ref[...] = m_new
        vo_ref[...] = v_new

    own = pl.BlockSpec(tile, lambda i, p: (i, 0))
    grid_spec = pltpu.PrefetchScalarGridSpec(
        num_scalar_prefetch=1, grid=(nt,),
        in_specs=[own, own, own,
                  pl.BlockSpec(tile, lambda i, p: _full_index(a, p[0], i)),
                  pl.BlockSpec((1,) + tile, lambda i, p: (p[1], i, 0)),
                  pl.BlockSpec((3,) + tile, lambda i, p: (0, i, 0))],
        out_specs=[own] * 4)
    shp = jax.ShapeDtypeStruct(w.shape, F32)
    return pl.pallas_call(
        body, name=f"reduce_update_{a}", grid_spec=grid_spec, out_shape=[shp] * 4,
        compiler_params=_cp(("parallel",)),
    )(pos, w, m, v, g_full, r1, r2)


def _reduce_own_and_update(w, m, v, g_chip, r2):
    tile, nt = _shard_tiles(0)

    def body(w_ref, m_ref, v_ref, g_ref, r2_ref, go_ref, do_ref, mo_ref, vo_ref):
        g = g_ref[...] + r2_ref[0].astype(F32)
        g = g + r2_ref[1].astype(F32)
        delta, m_new, v_new = _adam(w_ref[...], g, m_ref[...], v_ref[...])
        go_ref[...] = g
        do_ref[...] = delta
        mo_ref[...] = m_new
        vo_ref[...] = v_new

    own = pl.BlockSpec(tile, lambda i: (i, 0))
    shp = jax.ShapeDtypeStruct(w.shape, F32)
    return pl.pallas_call(
        body, name="reduce_update_0", grid=(nt,), out_shape=[shp] * 4,
        in_specs=[own, own, own, own, pl.BlockSpec((2,) + tile, lambda i: (0, i, 0))], out_specs=[own] * 4,
        compiler_params=_cp(("parallel",)),
    )(w, m, v, g_chip, r2)


def _small_update(gathered, norm_w, lb_logits, hnw, fnw, moments):
    m_nw, m_lb, m_hn, m_fn, v_nw, v_lb, v_hn, v_fn = moments

    def body(g_ref, nw, lb, hn, fn, mnw, mlb, mhn, mfn, vnw, vlb, vhn, vfn,
             loss_o, g_nw, g_lb, g_hn, g_fn, d_nw, d_lb, d_hn, d_fn,
             mo_nw, mo_lb, mo_hn, mo_fn, vo_nw, vo_lb, vo_hn, vo_fn):
        tot = g_ref[0]
        for d in range(1, NDEV):
            tot = tot + g_ref[d]
        loss_o[...] = tot[4:5, 0:LANES]
        logits = lb[...]
        lbv = jax.nn.sigmoid(logits[0:1] - logits[1:2])
        chain = tot[1:2] * lbv * (1.0 - lbv)
        grads = (tot[0:1], jnp.concatenate([chain, -chain], axis=0), tot[2:3, 0:LANES], tot[3:4])
        outs = ((nw, mnw, vnw, g_nw, d_nw, mo_nw, vo_nw), (lb, mlb, vlb, g_lb, d_lb, mo_lb, vo_lb),
                (hn, mhn, vhn, g_hn, d_hn, mo_hn, vo_hn), (fn, mfn, vfn, g_fn, d_fn, mo_fn, vo_fn))
        for g, (w_r, m_r, v_r, g_o, d_o, m_o, v_o) in zip(grads, outs):
            delta, m_new, v_new = _adam(w_r[...], g, m_r[...], v_r[...])
            g_o[...] = g
            d_o[...] = delta
            m_o[...] = m_new
            v_o[...] = v_new

    shapes = [norm_w.shape, lb_logits.shape, hnw.shape, fnw.shape]
    out_shape = [jax.ShapeDtypeStruct((1, LANES), F32)] + [jax.ShapeDtypeStruct(s, F32) for s in shapes] * 4
    return pl.pallas_call(body, name="small_update", out_shape=out_shape, compiler_params=_cp())(
        gathered, norm_w, lb_logits, hnw, fnw, m_nw, m_lb, m_hn, m_fn, v_nw, v_lb, v_hn, v_fn)


def _rmsnorm_in(x, norm_w):
    tr = 512

    def body(x_ref, w_ref, h_ref):
        xv = x_ref[...]
        r = lax.rsqrt(jnp.mean(xv * xv, axis=-1, keepdims=True) + EPS)
        h_ref[...] = (xv * r * w_ref[...]).astype(BF16)

    return pl.pallas_call(
        body, name="rmsnorm_in", out_shape=jax.ShapeDtypeStruct((S, D), BF16), grid=(S // tr,),
        in_specs=[pl.BlockSpec((tr, D), lambda i: (i, 0)), pl.BlockSpec((1, D), lambda i: (0, 0))],
        out_specs=pl.BlockSpec((tr, D), lambda i: (i, 0)),
        compiler_params=_cp(("parallel",)),
    )(x, norm_w)


def _in_proj(h, w_in):
    tn = 1024

    def body(h_ref, w_ref, z_ref):
        z_ref[...] = _dot(h_ref[...], w_ref[...])

    return pl.pallas_call(
        body, name="in_proj", out_shape=jax.ShapeDtypeStruct((S, IN_COLS), F32), grid=(IN_COLS // tn,),
        in_specs=[pl.BlockSpec((S, D), lambda j: (0, 0)), pl.BlockSpec((D, tn), lambda j: (0, j))],
        out_specs=pl.BlockSpec((S, tn), lambda j: (0, j)),
        compiler_params=_cp(("parallel",)),
    )(h, w_in)


def _block_tri(n, block, upper=False):
    r = lax.broadcasted_iota(jnp.int32, (n, n), 0)
    c = lax.broadcasted_iota(jnp.int32, (n, n), 1)
    keep = (c >= r) if upper else (c <= r)
    return jnp.where(keep & ((r // block) == (c // block)), 1.0, 0.0).astype(BF16)


def _tril_mask(n):
    r = lax.broadcasted_iota(jnp.int32, (n, n), 0)
    c = lax.broadcasted_iota(jnp.int32, (n, n), 1)
    return c <= r


def _chunk_scores(q, k, b, bex, r0, mask):
    parts, qs_l, ks_l, ek_l, eq_l = [], [], [], [], []
    for i in range(CHUNK // SUB):
        ri = slice(r0 + SUB * i, r0 + SUB * (i + 1))
        seen = slice(r0, r0 + SUB * (i + 1))
        base = bex[r0 + SUB * i:r0 + SUB * i + 1]
        eq = jnp.exp(b[ri] - base)
        ek = jnp.exp(jnp.minimum(base - b[seen], EXP_CLAMP))
        ks = k[seen] * ek
        if i + 1 < CHUNK // SUB:
            rest = jnp.zeros((CHUNK - SUB * (i + 1), 128), F32)
            ek, ks = jnp.concatenate([ek, rest], axis=0), jnp.concatenate([ks, rest], axis=0)
        qs = q[ri] * eq
        parts.append(_dot_nt(qs.astype(BF16), ks.astype(BF16)))
        qs_l.append(qs)
        ks_l.append(ks)
        ek_l.append(ek)
        eq_l.append(eq)
    return jnp.where(mask, jnp.concatenate(parts, axis=0), 0.0), qs_l, ks_l, ek_l, eq_l


def _hgrn_cols(hq, hf, hi, lb):
    sg = jax.nn.sigmoid(hf)
    f = lb + (1.0 - lb) * sg
    g = jnp.log(f)
    b = _dot_ones(_block_tri(HBLK, CHUNK), g)
    return _silu(hq), 1.0 - f, g, hi, sg, f, b


GATHER_IDS = (1, 2, 3)


def _hgrn_fwd(z, lbv, hnw, shards):
    ntb, nch = S // HBLK, HBLK // CHUNK
    n = len(GATHER_IDS)

    def body(hq_ref, hf_ref, hi_ref, hg_ref, lb_ref, hnw_ref, s0, s1, s2, o_ref, oa_ref, st_ref, f0, f1, f2,
             state, send_sems, recv_sems, local_sems):
        start, middle, end = _allgather_steps(GATHER_IDS, (s0, s1, s2), (f0, f1, f2), send_sems, recv_sems, local_sems)

        @pl.when(pl.program_id(0) == 0)
        def _():
            state[...] = jnp.zeros_like(state)
            start()

        pl.when(pl.program_id(0) == ntb // 2)(middle)

        q_a, k_a, g_a, v_a, _, _, b_a = _hgrn_cols(hq_ref[...], hf_ref[...], hi_ref[...], lb_ref[...])
        bex_a = b_a - g_a
        eb_a = jnp.exp(b_a)
        mask = _tril_mask(CHUNK)
        hg = hg_ref[...]
        w = hnw_ref[...]
        for h in range(HEADS):
            cols = slice(128 * h, 128 * h + 128)
            q, k, v, b, bex, eb = q_a[:, cols], k_a[:, cols], v_a[:, cols], b_a[:, cols], bex_a[:, cols], eb_a[:, cols]
            st = state[h]
            outs = []
            for c in range(nch):
                r0 = c * CHUNK
                rows = slice(r0, r0 + CHUNK)
                a = _chunk_scores(q, k, b, bex, r0, mask)[0]
                vb = v[rows].astype(BF16)
                b_last = b[r0 + CHUNK - 1:r0 + CHUNK]
                qe = (q[rows] * eb[rows]).astype(BF16)
                outs.append(_dot(a.astype(BF16), vb) + _dot_nt(qe, st.astype(BF16)))
                st_ref[h, c] = st
                ke = (k[rows] * jnp.exp(b_last - b[rows])).astype(BF16)
                st = st * jnp.exp(b_last) + _dot_tn(vb, ke)
            state[h] = st
            o = jnp.concatenate(outs, axis=0)
            o_ref[:, cols] = o
            r = lax.rsqrt(jnp.mean(o * o, axis=-1, keepdims=True) + EPS)
            oa_ref[:, cols] = (o * r * w * _silu(hg[:, cols])).astype(BF16)

        pl.when(pl.program_id(0) == ntb - 1)(end)

    def zcol(j):
        return pl.BlockSpec((HBLK, D), lambda t: (t, j))

    out_blk = pl.BlockSpec((HBLK, D), lambda t: (t, 0))
    any_spec = pl.BlockSpec(memory_space=pl.ANY)
    return pl.pallas_call(
        body, name="hgrn_fwd", grid=(ntb,),
        out_shape=[jax.ShapeDtypeStruct((S, D), F32), jax.ShapeDtypeStruct((S, D), BF16),
                   jax.ShapeDtypeStruct((HEADS, S // CHUNK, 128, 128), F32)]
        + [jax.ShapeDtypeStruct(FULL_SHAPES[a], BF16) for a in GATHER_IDS],
        in_specs=[zcol(0), zcol(1), zcol(2), zcol(3),
                  pl.BlockSpec((1, D), lambda t: (0, 0)), pl.BlockSpec((1, 128), lambda t: (0, 0))] + [any_spec] * n,
        out_specs=[out_blk, out_blk, pl.BlockSpec((HEADS, nch, 128, 128), lambda t: (0, t, 0, 0))] + [any_spec] * n,
        scratch_shapes=[pltpu.VMEM((HEADS, 128, 128), F32), pltpu.SemaphoreType.DMA((7 * n,)),
                        pltpu.SemaphoreType.DMA((7 * n,)), pltpu.SemaphoreType.DMA((n,))],
        compiler_params=_cp(("arbitrary",)),
    )(z, z, z, z, lbv, hnw, *shards)


def _half_mask():
    lane = lax.broadcasted_iota(jnp.int32, (1, LANES), 1)
    return (lane % 64) < 32


def _rope(t, cc, ss, first_half):
    partner = jnp.where(first_half, pltpu.roll(t, 96, 1), pltpu.roll(t, 32, 1))
    return t * cc + partner * ss


def _attn_masks():
    i = lax.broadcasted_iota(jnp.int32, (128, 128), 0)
    j = lax.broadcasted_iota(jnp.int32, (128, 128), 1)
    return j >= i, j <= i


def _to_residues_dyn(g, dst, src, row0=0, dtype=None):
    for gi, dil in enumerate((1, 4, 16)):
        m = S // dil

        @pl.when(g == gi)
        def _(dil=dil, m=m):
            for r in range(dil):
                v = src[...] if dil == 1 else src[pl.ds(r, m, stride=dil), :]
                if dtype is not None:
                    v = v.astype(dtype)
                dst[row0 + r * m:row0 + (r + 1) * m, 0:LANES] = v


def _from_residues_dyn(g, dst, src, row0=0):
    for gi, dil in enumerate((1, 4, 16)):
        m = S // dil

        @pl.when(g == gi)
        def _(dil=dil, m=m):
            for r in range(dil):
                v = src[row0 + r * m:row0 + (r + 1) * m, :]
                if dil == 1:
                    dst[...] = v
                else:
                    dst[pl.ds(r, m, stride=dil), :] = v


def _group_blocks(g):
    return jnp.where(g == 0, 16, jnp.where(g == 1, 4, 1))


def _attn_in_specs(extra):
    def zcol(off):
        return pl.BlockSpec((S, LANES), lambda p, g: (0, off + 4 * g + p))

    per_pair = pl.BlockSpec((S, LANES), lambda p, g: (0, p))
    const = pl.BlockSpec((S, LANES), lambda p, g: (0, 0))
    return [zcol(32), zcol(44), zcol(56), pl.BlockSpec((S, LANES), lambda p, g: (0, 68 + p)), const, const] + [per_pair] * extra


def _attn_fwd(z, cc, ss):
    def body(q_ref, k_ref, v_ref, ag_ref, cc_ref, ss_ref, ob_ref, lse_ref, obg_ref,
             tmp, qs, ks, vx, og, mg, lg, o_t, m_t, l_t, o_acc, m_acc, l_acc):
        g = pl.program_id(1)
        first_half = _half_mask()
        prev_ok, cur_ok = _attn_masks()
        lane = lax.broadcasted_iota(jnp.int32, (1, LANES), 1)
        heads = (lane < 64, lane >= 64)
        nblk = _group_blocks(g)

        @pl.when(g == 0)
        def _():
            ks[0:ATT_PAD, :] = jnp.zeros((ATT_PAD, LANES), BF16)
            vx[0:ATT_PAD, 0:LANES] = jnp.zeros((ATT_PAD, LANES), BF16)
            vx[:, LANES:2 * LANES] = jnp.ones((ATT_PAD + S, LANES), BF16)

        tmp[...] = _rope(q_ref[...], cc_ref[...], ss_ref[...], first_half) * ATT_SCALE
        _to_residues_dyn(g, qs, tmp)
        tmp[...] = _rope(k_ref[...], cc_ref[...], ss_ref[...], first_half)
        _to_residues_dyn(g, ks, tmp, ATT_PAD, BF16)
        _to_residues_dyn(g, vx, v_ref, ATT_PAD, BF16)

        def unit(u, carry):
            start = pl.multiple_of(u * 128, 128)
            cur = pl.ds(start, 128)
            pm = prev_ok & ((u & (nblk - 1)) != 0)
            qu = qs[cur, :]
            kcat = ks[pl.ds(start, 256), :]
            vext = vx[pl.ds(start, 256), :]
            o_u = m_u = l_u = None
            for hh in range(2):
                s = _dot_nt(jnp.where(heads[hh], qu, 0.0).astype(BF16), kcat)
                sp = jnp.where(pm, s[:, 0:128], -jnp.inf)
                sc = jnp.where(cur_ok, s[:, 128:256], -jnp.inf)
                m = jnp.max(jnp.maximum(sp, sc), axis=-1, keepdims=True)
                p = jnp.concatenate([jnp.exp(sp - m), jnp.exp(sc - m)], axis=1).astype(BF16)
                ol = _dot(p, vext)
                mb = jnp.broadcast_to(m, (128, LANES))
                if hh == 0:
                    o_u, l_u, m_u = ol[:, 0:128], ol[:, 128:256], mb
                else:
                    o_u = jnp.where(heads[1], ol[:, 0:128], o_u)
                    l_u = jnp.where(heads[1], ol[:, 128:256], l_u)
                    m_u = jnp.where(heads[1], mb, m_u)
            og[cur, :] = o_u
            mg[cur, :] = m_u
            lg[cur, :] = l_u
            return carry

        lax.fori_loop(0, 16, unit, 0, unroll=16)
        _from_residues_dyn(g, o_t, og)
        _from_residues_dyn(g, m_t, mg)
        _from_residues_dyn(g, l_t, lg)

        @pl.when(g == 0)
        def _():
            o_acc[...] = o_t[...]
            m_acc[...] = m_t[...]
            l_acc[...] = l_t[...]

        @pl.when(g > 0)
        def _():
            m_new = jnp.maximum(m_acc[...], m_t[...])
            wa, wb = jnp.exp(m_acc[...] - m_new), jnp.exp(m_t[...] - m_new)
            o_acc[...] = o_acc[...] * wa + o_t[...] * wb
            l_acc[...] = l_acc[...] * wa + l_t[...] * wb
            m_acc[...] = m_new

        @pl.when(g == 2)
        def _():
            ob = o_acc[...] / l_acc[...]
            ob_ref[...] = ob
            lse_ref[...] = m_acc[...] + jnp.log(l_acc[...])
            obg_ref[...] = (ob * _silu(ag_ref[...])).astype(BF16)

    blk = pl.BlockSpec((S, LANES), lambda p, g: (0, p))
    buf = pltpu.VMEM((S, LANES), F32)
    return pl.pallas_call(
        body, name="attn_fwd", grid=(4, 3),
        out_shape=[jax.ShapeDtypeStruct((S, 512), F32), jax.ShapeDtypeStruct((S, 512), F32),
                   jax.ShapeDtypeStruct((S, 512), BF16)],
        in_specs=_attn_in_specs(0), out_specs=[blk, blk, blk],
        scratch_shapes=[buf, buf, pltpu.VMEM((ATT_PAD + S, LANES), BF16), pltpu.VMEM((ATT_PAD + S, 2 * LANES), BF16)] + [buf] * 9,
        compiler_params=_cp(("parallel", "arbitrary")),
    )(z, z, z, z, cc, ss)


def _tail(x, o_a, o_bg, z, target, w_a, w_b, w_out, fnw):
    tm = 256

    def body(x_ref, oa_ref, ob_ref, gpa_ref, gpb_ref, t_ref, wa_ref, wb_ref, wo_ref, fnw_ref,
             dx2_ref, dx2b_ref, dz_hbm, doa_ref, dob_ref, mg_ref, dya_ref, dyb_ref, small_ref, dgp, dgp_sem):
        step = pl.program_id(0)
        slot = step % 2

        def dgp_copy(at_step, at_slot):
            return pltpu.make_async_copy(
                dgp.at[at_slot], dz_hbm.at[pl.ds(pl.multiple_of(at_step * tm, tm), tm), pl.ds(DZ_GATES, 2 * D)],
                dgp_sem.at[at_slot])

        @pl.when(step == 0)
        def _():
            small_ref[...] = jnp.zeros_like(small_ref)

        @pl.when(step >= 2)
        def _():
            dgp_copy(step - 2, slot).wait()

        wa, wb, wo = wa_ref[...], wb_ref[...], wo_ref[...]
        y_a = _dot(oa_ref[...], wa)
        y_b = _dot(ob_ref[...], wb)
        ga = jax.nn.sigmoid(gpa_ref[...])
        gb = jax.nn.sigmoid(gpb_ref[...])
        merged = (ga * y_a + gb * y_b).astype(BF16)
        x2 = x_ref[...] + _dot(merged, wo)
        r2 = lax.rsqrt(jnp.mean(x2 * x2, axis=-1, keepdims=True) + EPS)
        n2 = x2 * r2
        fw = fnw_ref[...]
        err = n2 * fw - t_ref[...]
        loss = 0.5 * jnp.sum(jnp.sum(err * err, axis=-1, keepdims=True), axis=0, keepdims=True) / D
        dy = err * (1.0 / D)
        g_fnw = jnp.sum(dy * n2, axis=0, keepdims=True)
        dn = dy * fw
        dx2 = r2 * (dn - n2 * jnp.mean(dn * n2, axis=-1, keepdims=True))
        dx2b = dx2.astype(BF16)
        dmerged = _dot_nt(dx2b, wo)
        dy_a = (dmerged * ga).astype(BF16)
        dy_b = (dmerged * gb).astype(BF16)
        dx2_ref[...] = dx2
        dx2b_ref[...] = dx2b
        dgp[slot, :, 0:D] = (dmerged * y_a * ga * (1.0 - ga)).astype(BF16)
        dgp[slot, :, D:2 * D] = (dmerged * y_b * gb * (1.0 - gb)).astype(BF16)
        dgp_copy(step, slot).start()
        doa_ref[...] = _dot_nt(dy_a, wa)
        dob_ref[...] = _dot_nt(dy_b, wb)
        mg_ref[...] = merged
        dya_ref[...] = dy_a
        dyb_ref[...] = dy_b
        small_ref[0:1, :] += g_fnw
        small_ref[1:2, :] += jnp.broadcast_to(loss, (1, D))

        @pl.when(step == S // tm - 1)
        def _():
            dgp_copy(step - 1, 1 - slot).wait()
            dgp_copy(step, slot).wait()

    def rows(cols, off=0):
        return pl.BlockSpec((tm, cols), lambda i: (i, off))

    def whole(shape):
        return pl.BlockSpec(shape, lambda i: (0, 0))

    return pl.pallas_call(
        body, name="tail", grid=(S // tm,),
        out_shape=[jax.ShapeDtypeStruct((S, D), F32), jax.ShapeDtypeStruct((S, D), BF16),
                   jax.ShapeDtypeStruct((S, IN_COLS), BF16), jax.ShapeDtypeStruct((S, D), F32),
                   jax.ShapeDtypeStruct((S, 512), F32), jax.ShapeDtypeStruct((S, D), BF16),
                   jax.ShapeDtypeStruct((S, D), BF16), jax.ShapeDtypeStruct((S, D), BF16),
                   jax.ShapeDtypeStruct((8, D), F32)],
        in_specs=[rows(D), rows(D), rows(512), rows(D, 9), rows(D, 10), rows(D),
                  whole((D, D)), whole((512, D)), whole((D, D)), whole((1, D))],
        out_specs=[rows(D), rows(D), pl.BlockSpec(memory_space=pl.ANY), rows(D), rows(512), rows(D), rows(D),
                   rows(D), whole((8, D))],
        scratch_shapes=[pltpu.VMEM((2, tm, 2 * D), BF16), pltpu.SemaphoreType.DMA((2,))],
        compiler_params=_cp(("arbitrary",)),
    )(x, o_a, o_bg, z, z, target, w_a, w_b, w_out, fnw)


def _tn_matmul(a, b, name):
    m, n = a.shape[1], b.shape[1]
    tn = 512

    def body(a_ref, b_ref, o_ref, ob_ref):
        acc = _dot_tn(a_ref[...], b_ref[...])
        o_ref[...] = acc
        ob_ref[...] = acc.astype(BF16)

    out_blk = pl.BlockSpec((m, tn), lambda j: (0, j))
    return pl.pallas_call(
        body, name=name, grid=(n // tn,),
        out_shape=[jax.ShapeDtypeStruct((m, n), F32), jax.ShapeDtypeStruct((m, n), BF16)],
        in_specs=[pl.BlockSpec((S, m), lambda j: (0, 0)), pl.BlockSpec((S, tn), lambda j: (0, j))],
        out_specs=[out_blk, out_blk],
        compiler_params=_cp(("parallel",)),
    )(a, b)


def _hgrn_bwd(z, o, do_a, states, lbv, hnw, partials, dz):
    ntb, nch = S // HBLK, HBLK // CHUNK
    n = len(GATHER_IDS)

    def body(hq_ref, hf_ref, hi_ref, hg_ref, o_ref, doa_ref, st_ref, lb_ref, hnw_ref, p0, p1, p2, dz_in,
             dz_ref, glb_ref, ghn_ref, e0, e1, e2, dstate, send_sems, recv_sems):
        dhq_ref, dhf_ref, dhi_ref, dhg_ref = (dz_ref.at[:, pl.ds(j * D, D)] for j in range(4))
        start, end = _exchange_chips_steps((p0, p1, p2), (e0, e1, e2), send_sems, recv_sems)

        @pl.when(pl.program_id(0) == 0)
        def _():
            dstate[...] = jnp.zeros_like(dstate)
            glb_ref[...] = jnp.zeros_like(glb_ref)
            ghn_ref[...] = jnp.zeros_like(ghn_ref)
            start()

        lb_a = lb_ref[...]
        hq_a, hg_a = hq_ref[...], hg_ref[...]
        q_a, k_a, g_a, v_a, sg_a, f_a, b_a = _hgrn_cols(hq_a, hf_ref[...], hi_ref[...], lb_a)
        bex_a = b_a - g_a
        eb_a = jnp.exp(b_a)
        w = hnw_ref[...]
        mask = _tril_mask(CHUNK)
        upper = _block_tri(CHUNK, CHUNK, upper=True)
        for h in range(HEADS):
            cols = slice(128 * h, 128 * h + 128)
            q, k, v, b, bex, eb = q_a[:, cols], k_a[:, cols], v_a[:, cols], b_a[:, cols], bex_a[:, cols], eb_a[:, cols]
            hq, hg, sg, f, lb = hq_a[:, cols], hg_a[:, cols], sg_a[:, cols], f_a[:, cols], lb_a[:, cols]
            ov, doa = o_ref[:, cols], doa_ref[:, cols]
            r = lax.rsqrt(jnp.mean(ov * ov, axis=-1, keepdims=True) + EPS)
            n = ov * r
            sil = _silu(hg)
            dhg_ref[:, cols] = (doa * n * w * _dsilu(hg)).astype(BF16)
            ghn_ref[h] += jnp.sum(doa * sil * n, axis=0, keepdims=True)
            dn = doa * sil * w
            do = r * (dn - n * jnp.mean(dn * n, axis=-1, keepdims=True))

            dst = dstate[h]
            dq_l, dk_l, dv_l, dg_l = [None] * nch, [None] * nch, [None] * nch, [None] * nch
            for c in reversed(range(nch)):
                r0 = c * CHUNK
                rows = slice(r0, r0 + CHUNK)
                st = st_ref[h, c]
                bc, kc, qc = b[rows], k[rows], q[rows]
                vb, dob = v[rows].astype(BF16), do[rows].astype(BF16)
                b_last = bc[CHUNK - 1:CHUNK]
                e_last = jnp.exp(b_last)
                ekl = jnp.exp(b_last - bc)
                dstb = dst.astype(BF16)
                a, qs_l, ks_l, ek_l, eq_l = _chunk_scores(q, k, b, bex, r0, mask)
                da = jnp.where(mask, _dot_nt(dob, vb), 0.0)
                dv_l[c] = _dot_tn(a.astype(BF16), dob) + _dot_nt((kc * ekl).astype(BF16), dstb)
                dq_inter = _dot(dob, st.astype(BF16)) * eb[rows]
                dk_state = _dot(vb, dstb) * ekl
                dq_parts, dk_intra = [], jnp.zeros((CHUNK, 128), F32)
                dab = da.astype(BF16)
                for i in range(CHUNK // SUB):
                    da_i = dab[SUB * i:SUB * (i + 1)]
                    ks_hi, ks_lo = _split2(ks_l[i])
                    qs_hi, qs_lo = _split2(qs_l[i])
                    dq_parts.append((_dot(da_i, ks_hi) + _dot(da_i, ks_lo)) * eq_l[i])
                    dk_intra = dk_intra + (_dot_tn(da_i, qs_hi) + _dot_tn(da_i, qs_lo)) * ek_l[i]
                dq = jnp.concatenate(dq_parts, axis=0) + dq_inter
                dk = dk_intra + dk_state
                last = (e_last * jnp.sum(st * dst, axis=0, keepdims=True)
                        + jnp.sum(kc * dk_state, axis=0, keepdims=True))
                dg_l[c] = _dot_ones(upper, qc * dq - kc * dk) + last
                dq_l[c], dk_l[c] = dq, dk
                dst = dst * e_last + _dot_tn(dob, (qc * eb[rows]).astype(BF16))
            dstate[h] = dst
            dq, dk = jnp.concatenate(dq_l, axis=0), jnp.concatenate(dk_l, axis=0)
            dg, dv = jnp.concatenate(dg_l, axis=0), jnp.concatenate(dv_l, axis=0)
            dhq_ref[:, cols] = (dq * _dsilu(hq)).astype(BF16)
            dhi_ref[:, cols] = dv.astype(BF16)
            df = dg / f - dk
            dhf_ref[:, cols] = (df * (1.0 - lb) * sg * (1.0 - sg)).astype(BF16)
            glb_ref[:, cols] += jnp.sum(df * (1.0 - sg), axis=0, keepdims=True)

        pl.when(pl.program_id(0) == ntb - 1)(end)

    def rev(t):
        return ntb - 1 - t

    def zcol(j):
        return pl.BlockSpec((HBLK, D), lambda t: (rev(t), j))

    blk = pl.BlockSpec((HBLK, D), lambda t: (rev(t), 0))
    any_spec = pl.BlockSpec(memory_space=pl.ANY)
    return pl.pallas_call(
        body, name="hgrn_bwd", grid=(ntb,),
        out_shape=[jax.ShapeDtypeStruct((S, IN_COLS), BF16)]
        + [jax.ShapeDtypeStruct((1, D), F32), jax.ShapeDtypeStruct((HEADS, 1, 128), F32)]
        + [jax.ShapeDtypeStruct((3,) + SHARD_SHAPES[a], BF16) for a in GATHER_IDS],
        in_specs=[zcol(0), zcol(1), zcol(2), zcol(3), blk, blk,
                  pl.BlockSpec((HEADS, nch, 128, 128), lambda t: (0, rev(t), 0, 0)),
                  pl.BlockSpec((1, D), lambda t: (0, 0)), pl.BlockSpec((1, 128), lambda t: (0, 0))]
        + [any_spec] * (n + 1),
        out_specs=[pl.BlockSpec((HBLK, DZ_ATT), lambda t: (rev(t), 0)), pl.BlockSpec((1, D), lambda t: (0, 0)),
                   pl.BlockSpec((HEADS, 1, 128), lambda t: (0, 0, 0))] + [any_spec] * n,
        scratch_shapes=[pltpu.VMEM((HEADS, 128, 128), F32), pltpu.SemaphoreType.DMA((3 * n,)),
                        pltpu.SemaphoreType.DMA((3 * n,))],
        input_output_aliases={9 + n: 0},
        compiler_params=_cp(("arbitrary",)),
    )(z, z, z, z, o, do_a, states, lbv, hnw, *partials, dz)


def _attn_bwd(z, cc, ss, ob, lse, do_bg, dz):
    def body(q_ref, k_ref, v_ref, ag_ref, cc_ref, ss_ref, ob_ref, lse_ref, dobg_ref, dz_in, dz_hbm,
             tmp, qs, ks, vs, dos, dqs, dks, dvs, dkp, dvp, do_t, ls0_t, ls1_t, dl0_t, dl1_t, ls0, ls1, dl0, dl1,
             stage, stage_sem):
        pair, g = pl.program_id(0), pl.program_id(1)

        def out_copy(j):
            tile = DZ_ATT // LANES + (36 + pair if j == 3 else 12 * j + 4 * g + pair)
            return pltpu.make_async_copy(
                stage.at[j], dz_hbm.at[:, pl.ds(pl.multiple_of(tile * LANES, LANES), LANES)], stage_sem.at[j])

        @pl.when(pair * 3 + g > 0)
        def _():
            for j in range(3):
                out_copy(j).wait()

        pl.when(g == 1)(lambda: out_copy(3).wait())
        first_half = _half_mask()
        prev_ok, cur_ok = _attn_masks()
        lane = lax.broadcasted_iota(jnp.int32, (1, LANES), 1)
        heads = (lane < 64, lane >= 64)
        nblk = _group_blocks(g)
        cc_v, ss_v = cc_ref[...], ss_ref[...]

        @pl.when(g == 0)
        def _():
            ag, obv, dobg = ag_ref[...], ob_ref[...], dobg_ref[...]
            stage[3] = (dobg * obv * _dsilu(ag)).astype(BF16)
            out_copy(3).start()
            dob = dobg * _silu(ag)
            do_t[...] = dob
            prod = dob * obv
            dl = jnp.concatenate(
                [jnp.broadcast_to(jnp.sum(prod[:, 0:64], axis=-1, keepdims=True), (S, 64)),
                 jnp.broadcast_to(jnp.sum(prod[:, 64:128], axis=-1, keepdims=True), (S, 64))], axis=1)
            dl_sw = pltpu.roll(dl, 64, 1)
            dl0_t[...] = jnp.where(heads[0], dl, dl_sw)
            dl1_t[...] = jnp.where(heads[0], dl_sw, dl)
            ls = lse_ref[...]
            ls_sw = pltpu.roll(ls, 64, 1)
            ls0_t[...] = jnp.where(heads[0], ls, ls_sw)
            ls1_t[...] = jnp.where(heads[0], ls_sw, ls)
            ks[0:ATT_PAD, :] = jnp.zeros((ATT_PAD, LANES), BF16)
            vs[0:ATT_PAD, :] = jnp.zeros((ATT_PAD, LANES), BF16)

        tmp[...] = _rope(q_ref[...], cc_v, ss_v, first_half) * ATT_SCALE
        _to_residues_dyn(g, qs, tmp)
        tmp[...] = _rope(k_ref[...], cc_v, ss_v, first_half)
        _to_residues_dyn(g, ks, tmp, ATT_PAD, BF16)
        _to_residues_dyn(g, vs, v_ref, ATT_PAD, BF16)
        _to_residues_dyn(g, dos, do_t)
        _to_residues_dyn(g, ls0, ls0_t)
        _to_residues_dyn(g, ls1, ls1_t)
        _to_residues_dyn(g, dl0, dl0_t)
        _to_residues_dyn(g, dl1, dl1_t)
        lss, dls = (ls0, ls1), (dl0, dl1)

        def unit(u, carry):
            start = pl.multiple_of(u * 128, 128)
            cur = pl.ds(start, 128)
            both = pl.ds(start, 256)
            pm = prev_ok & ((u & (nblk - 1)) != 0)
            qu, dou = qs[cur, :], dos[cur, :]
            kcat, vcat = ks[both, :], vs[both, :]
            dq_u = None
            q_l, do_l, ds_l, p_l = [], [], [], []
            for hh in range(2):
                q_h = jnp.where(heads[hh], qu, 0.0).astype(BF16)
                do_h = jnp.where(heads[hh], dou, 0.0).astype(BF16)
                s = _dot_nt(q_h, kcat)
                dp = _dot_nt(do_h, vcat)
                lse_h, dl_h = lss[hh][cur, :], dls[hh][cur, :]
                pp = jnp.where(pm, jnp.exp(s[:, 0:128] - lse_h), 0.0)
                pc = jnp.where(cur_ok, jnp.exp(s[:, 128:256] - lse_h), 0.0)
                ds = jnp.concatenate([pp * (dp[:, 0:128] - dl_h), pc * (dp[:, 128:256] - dl_h)], axis=1).astype(BF16)
                dq = _dot(ds, kcat)
                dq_u = dq if hh == 0 else jnp.where(heads[1], dq, dq_u)
                q_l.append(q_h)
                do_l.append(do_h)
                ds_l.append(ds)
                p_l.append(jnp.concatenate([pp, pc], axis=1).astype(BF16))
            dkcat = _dot_tn(jnp.concatenate(ds_l, axis=0), jnp.concatenate(q_l, axis=0))
            dvcat = _dot_tn(jnp.concatenate(p_l, axis=0), jnp.concatenate(do_l, axis=0))
            dkp[cur, :] = dkcat[0:128]
            dks[cur, :] = dkcat[128:256]
            dvp[cur, :] = dvcat[0:128]
            dvs[cur, :] = dvcat[128:256]
            dqs[cur, :] = dq_u
            return carry

        lax.fori_loop(0, 16, unit, 0, unroll=ATT_UNROLL)
        dks[0:S - 128, :] += dkp[128:S, :]
        dvs[0:S - 128, :] += dvp[128:S, :]
        _from_residues_dyn(g, tmp, dqs)
        stage[0] = (_rope(tmp[...], cc_v, -ss_v, first_half) * ATT_SCALE).astype(BF16)
        out_copy(0).start()
        _from_residues_dyn(g, tmp, dks)
        stage[1] = _rope(tmp[...], cc_v, -ss_v, first_half).astype(BF16)
        out_copy(1).start()
        _from_residues_dyn(g, tmp, dvs)
        stage[2] = tmp[...].astype(BF16)
        out_copy(2).start()

        @pl.when(pair * 3 + g == 11)
        def _():
            for j in range(3):
                out_copy(j).wait()

    any_spec = pl.BlockSpec(memory_space=pl.ANY)
    buf = pltpu.VMEM((S, LANES), F32)
    padded_b = pltpu.VMEM((ATT_PAD + S, LANES), BF16)
    return pl.pallas_call(
        body, name="attn_bwd", grid=(4, 3),
        out_shape=jax.ShapeDtypeStruct((S, IN_COLS), BF16),
        in_specs=_attn_in_specs(3) + [any_spec], out_specs=any_spec,
        scratch_shapes=[buf, buf, padded_b, padded_b] + [buf] * 15
        + [pltpu.VMEM((4, S, LANES), BF16), pltpu.SemaphoreType.DMA((4,))],
        input_output_aliases={9: 0},
        compiler_params=_cp(("arbitrary", "arbitrary")),
    )(z, z, z, z, cc, ss, ob, lse, do_bg, dz)


def _in_proj_bwd(dz, h, w_in):
    half = S // 2
    slab = (D, SHARD_COLS)

    def body(dz_hbm, h_hbm, w_hbm, dh_hbm, g_chip, r1_hbm, relay_hbm, r2_hbm,
             h_buf, dz_buf, stage_d, r1_buf, stage_i, acc,
             dz_sem, w_sem, h_sem, r1_sem, out_sem, send_d, recv_d, send_i, recv_i):
        x, y, c = _mesh_pos()
        sibling = (x, y, 1 - c)
        north = c == 1
        near = (jnp.where(north, 1 - x, x), jnp.where(north, y, 1 - y))
        far = (jnp.where(north, x, 1 - x), jnp.where(north, 1 - y, y))
        chips = [(1 - x, 1 - y), near, far, (x, y)]

        def cols(d):
            return pl.ds(pl.multiple_of(d * SHARD_COLS, LANES), SHARD_COLS)

        blocks = []
        for q_sib, q in zip([chips[0], far, near, chips[3]], chips):
            blocks += [4 * q_sib[0] + 2 * q_sib[1] + (1 - c), 4 * q[0] + 2 * q[1] + c]

        def dz_tile(t):
            return _SplitCopy(dz_hbm.at[pl.ds((t % 2) * half, half), cols(blocks[t // 2])],
                                         dz_buf.at[t % 2], dz_sem.at[t % 2])

        def to_sibling(i):
            return pltpu.make_async_remote_copy(
                src_ref=stage_d.at[i % 2], dst_ref=r1_hbm.at[i], send_sem=send_d.at[i], recv_sem=recv_d.at[i],
                device_id=sibling, device_id_type=MESH)

        def to_owner(i):
            dst = relay_hbm if i == 0 else r2_hbm.at[i - 1]
            return pltpu.make_async_remote_copy(
                src_ref=stage_i.at[i], dst_ref=dst, send_sem=send_i.at[i], recv_sem=recv_i.at[i],
                device_id=(*(far if i == 2 else near), c), device_id_type=MESH)

        h_copy = _SplitCopy(h_hbm, h_buf, h_sem)
        h_copy.start()
        dz_tile(0).start()
        h_copy.wait()
        for b in range(8):
            i = b // 2
            g = None
            for r in range(2):
                t = 2 * b + r
                if t + 1 < 16:
                    dz_tile(t + 1).start()
                dz_tile(t).wait()
                part = _dot_tn(h_buf[r * half:(r + 1) * half, :], dz_buf[t % 2])
                g = part if g is None else g + part
                if b % 2 == 1 and r == 0:
                    to_sibling(i).wait_recv()
                    r1_copy = _SplitCopy(r1_hbm.at[i], r1_buf, r1_sem)
                    r1_copy.start()
            if b % 2 == 0:
                if i >= 2:
                    to_sibling(i - 2).wait_send()
                stage_d[i % 2] = g.astype(BF16)
                to_sibling(i).start()
            else:
                r1_copy.wait()
                g = g + r1_buf[...].astype(F32)
                if i == 2:
                    to_owner(0).wait_recv()
                    relay_copy = _SplitCopy(relay_hbm, r1_buf, r1_sem)
                    relay_copy.start()
                    relay_copy.wait()
                    g = g + r1_buf[...].astype(F32)
                if i < 3:
                    stage_i[i] = g.astype(BF16)
                    to_owner(i).start()
                else:
                    g_chip[...] = g
        to_sibling(2).wait_send()
        to_sibling(3).wait_send()

        def dz2(t):
            return _SplitCopy(
                dz_hbm.at[pl.ds((t % 2) * half, half), pl.ds((t // 2) * SHARD_COLS, SHARD_COLS)],
                dz_buf.at[t % 2], dz_sem.at[t % 2])

        def w2(b):
            return _SplitCopy(w_hbm.at[:, pl.ds(b * SHARD_COLS, SHARD_COLS)],
                                         stage_d.at[b % 2], w_sem.at[b % 2])

        dz2(0).start()
        w2(0).start()
        for t in range(16):
            b, r = t // 2, t % 2
            if t + 1 < 16:
                dz2(t + 1).start()
            if r == 0:
                if b + 1 < 8:
                    w2(b + 1).start()
                w2(b).wait()
            dz2(t).wait()
            part = _dot_nt(dz_buf[t % 2], stage_d[b % 2])
            if b == 0:
                acc[r] = part
            else:
                acc[r] += part
        dh_out = [_SplitCopy(acc.at[r], dh_hbm.at[pl.ds(r * half, half), :], out_sem.at[r])
                  for r in range(2)]
        for cp in dh_out:
            cp.start()
        for cp in dh_out:
            cp.wait()
        for i in range(3):
            to_owner(i).wait_send()
        for i in (1, 2):
            to_owner(i).wait_recv()

    any_spec = pl.BlockSpec(memory_space=pl.ANY)
    return pl.pallas_call(
        body, name="in_proj_bwd",
        out_shape=[jax.ShapeDtypeStruct((S, D), F32), jax.ShapeDtypeStruct(slab, F32),
                   jax.ShapeDtypeStruct((4,) + slab, BF16), jax.ShapeDtypeStruct(slab, BF16),
                   jax.ShapeDtypeStruct((2,) + slab, BF16)],
        in_specs=[any_spec] * 3,
        out_specs=[any_spec, pl.BlockSpec(memory_space=pltpu.VMEM), any_spec, any_spec, any_spec],
        scratch_shapes=[pltpu.VMEM((S, D), BF16), pltpu.VMEM((2, half, SHARD_COLS), BF16),
                        pltpu.VMEM((2,) + slab, BF16), pltpu.VMEM(slab, BF16), pltpu.VMEM((3,) + slab, BF16),
                        pltpu.VMEM((2, half, D), F32),
                        pltpu.SemaphoreType.DMA((2,)), pltpu.SemaphoreType.DMA((2,)), pltpu.SemaphoreType.DMA,
                        pltpu.SemaphoreType.DMA, pltpu.SemaphoreType.DMA((2,)),
                        pltpu.SemaphoreType.DMA((4,)), pltpu.SemaphoreType.DMA((4,)),
                        pltpu.SemaphoreType.DMA((3,)), pltpu.SemaphoreType.DMA((3,))],
        compiler_params=_cp(),
    )(dz, h, w_in)


def _grad_x(x, norm_w, dh, dx2):
    tr = 256

    def body(x_ref, w_ref, dh_ref, dx2_ref, gx_ref, gnw_ref):
        @pl.when(pl.program_id(0) == 0)
        def _():
            gnw_ref[...] = jnp.zeros_like(gnw_ref)

        xv, dhv = x_ref[...], dh_ref[...]
        r = lax.rsqrt(jnp.mean(xv * xv, axis=-1, keepdims=True) + EPS)
        n = xv * r
        gnw_ref[...] += jnp.sum(dhv * n, axis=0, keepdims=True)
        dn = dhv * w_ref[...]
        gx_ref[...] = dx2_ref[...] + r * (dn - n * jnp.mean(dn * n, axis=-1, keepdims=True))

    row = pl.BlockSpec((tr, D), lambda i: (i, 0))
    vec = pl.BlockSpec((1, D), lambda i: (0, 0))
    return pl.pallas_call(
        body, name="grad_x", grid=(S // tr,),
        out_shape=[jax.ShapeDtypeStruct((S, D), F32), jax.ShapeDtypeStruct((1, D), F32)],
        in_specs=[row, vec, row, row], out_specs=[row, vec],
        compiler_params=_cp(("arbitrary",)),
    )(x, norm_w, dh, dx2)


def _rope_tables(positions):
    inv_freq = 10000.0 ** (-jnp.arange(0, 64, 2, dtype=F32) / 64)
    ang = positions.astype(F32)[:, None] * inv_freq[None, :]
    cos, sin = jnp.cos(ang), jnp.sin(ang)
    return jnp.tile(cos, (1, 4)), jnp.tile(jnp.concatenate([-sin, sin], axis=1), (1, 2))


def _local_step(x, positions, norm_w, lb_logits, hnw, fnw, target, w_in_shard, small_shards, core):
    cc, ss = _rope_tables(positions)
    lbv = jax.nn.sigmoid(lb_logits[0:1] - lb_logits[1:2])
    h = _rmsnorm_in(x, norm_w)
    z, w_in = _in_proj_gather(h, w_in_shard)
    o, o_a, states, w_a, w_b, w_out = _hgrn_fwd(z, lbv, hnw, small_shards)
    ob, lse, o_bg = _attn_fwd(z, cc, ss)
    dx2, dx2b, dz, do_a, do_bg, merged, dy_a, dy_b, tail_small = _tail(x, o_a, o_bg, z, target, w_a, w_b, w_out, fnw)
    g_out, gb_out = _tn_matmul(merged, dx2b, "grad_w_out")
    g_a, gb_a = _tn_matmul(o_a, dy_a, "grad_w_a")
    g_b, gb_b = _tn_matmul(o_bg, dy_b, "grad_w_b")
    grads, gb = (g_a, g_b, g_out), (gb_a, gb_b, gb_out)
    r1 = _exchange_sibling(GATHER_IDS, gb)
    pb = [_chip_partials(a, grads[i], r1[i], core) for i, a in enumerate(GATHER_IDS)]
    dz, glb, ghn, *r2 = _hgrn_bwd(z, o, do_a, states, lbv, hnw, pb, dz)
    dz = _attn_bwd(z, cc, ss, ob, lse, do_bg, dz)
    dh, g_chip_in, _, _, r2_in = _in_proj_bwd(dz, h, w_in)
    grad_x, gnw = _grad_x(x, norm_w, dh, dx2)
    ghn_row = jnp.pad(jnp.sum(ghn, axis=0), ((0, 0), (0, D - 128)))
    small = jnp.concatenate([gnw, glb, ghn_row, tail_small[0:2], jnp.zeros((3, D), F32)], axis=0)
    return grad_x, (g_chip_in, r2_in), grads, r1, r2, small


def kernel(x, positions, norm_w, w_in, lb_logits, hgrn_norm_w, w_branch_a, w_branch_b, w_out, final_norm_w, loss_target, m_norm_w, m_w_in, m_lb_logits, m_hgrn_norm_w, m_w_branch_a, m_w_branch_b, m_w_out, m_final_norm_w, v_norm_w, v_w_in, v_lb_logits, v_hgrn_norm_w, v_w_branch_a, v_w_branch_b, v_w_out, v_final_norm_w):
    ix, iy, ic = _mesh_pos()
    core = jnp.reshape(ic, (1,)).astype(jnp.int32)
    pos = jnp.stack([4 * ix + 2 * iy + ic, 2 * ix + iy]).astype(jnp.int32)

    shards = [w_in[0], w_branch_a[0], w_branch_b[0], w_out[0]]
    moments_m = [m_w_in[0], m_w_branch_a[0], m_w_branch_b[0], m_w_out[0]]
    moments_v = [v_w_in[0], v_w_branch_a[0], v_w_branch_b[0], v_w_out[0]]
    names = ("w_in", "w_a", "w_b", "w_out")
    ids = GATHER_IDS
    shards_b = [_cast_bf16(w, f"cast_{nm}") for w, nm in zip(shards, names)]

    fnw2 = final_norm_w.reshape(1, D)
    grad_x, (g_chip_in, r2_in), grads, r1, r2, small = _local_step(
        x[0], positions[0], norm_w, lb_logits, hgrn_norm_w, fnw2, loss_target[0], shards_b[0], shards_b[1:], core)

    gathered = _gather_small(small)
    big =[_reduce_own_and_update(shards[0], moments_m[0], moments_v[0], g_chip_in, r2_in)]
    big += [_reduce_and_update(a, shards[a], moments_m[a], moments_v[a], grads[i], r1[i], r2[i], pos)
            for i, a in enumerate(ids)]
    sm = _small_update(gathered, norm_w, lb_logits, hgrn_norm_w, fnw2,
                       (m_norm_w, m_lb_logits, m_hgrn_norm_w, m_final_norm_w.reshape(1, D),
                        v_norm_w, v_lb_logits, v_hgrn_norm_w, v_final_norm_w.reshape(1, D)))
    loss = sm[0][0, 0]
    outs = [loss, grad_x[None]]
    for kind in range(4):
        s_nw, s_lb, s_hn, s_fn = sm[1 + 4 * kind:5 + 4 * kind]
        outs += [s_nw, big[0][kind][None], s_lb, s_hn, big[1][kind][None], big[2][kind][None],
                 big[3][kind][None], s_fn.reshape(D)]
    return tuple(outs)
```

```python
import functools

import jax
import jax.numpy as jnp
from jax import lax
from jax.experimental import pallas as pl
from jax.experimental.pallas import tpu as pltpu

F32 = jnp.float32
BF16 = jnp.bfloat16
MESH = pl.DeviceIdType.MESH

S = 2048
D = 1024
NDEV = 8
HEADS = 8
CHUNK = 64
SUB = 16
HBLK = 256
ATT_PAD = 128
ATT_UNROLL = 16
COPY_PARTS = 4
EXP_CLAMP = 80.0
EPS = 1e-6
IN_COLS = 11264
SHARD_COLS = IN_COLS // NDEV
DZ_ATT = 4096
DZ_GATES = 9216
ATT_DILS = (1, 4, 16)
ATT_SCALE = 64 ** -0.5
LANES = 128

ADAM_LR, ADAM_B1, ADAM_B2, ADAM_EPS, ADAM_WD, ADAM_STEP = 0.001, 0.9, 0.999, 1e-08, 0.01, 10

VMEM_LIMIT = 56 * 1024 * 1024


def _cp(sem=None, **kw):
    return pltpu.CompilerParams(dimension_semantics=sem, vmem_limit_bytes=VMEM_LIMIT, **kw)


def _dot(a, b):
    return jnp.dot(a, b, preferred_element_type=F32)


def _dot_nt(a, b):
    return lax.dot_general(a, b, (((1,), (1,)), ((), ())), preferred_element_type=F32)


def _dot_tn(a, b):
    return lax.dot_general(a, b, (((0,), (0,)), ((), ())), preferred_element_type=F32)


def _split2(x):
    hi = x.astype(BF16)
    lo = (x - hi.astype(F32)).astype(BF16)
    return hi, lo


def _split3(x):
    hi = x.astype(BF16)
    r = x - hi.astype(F32)
    mid = r.astype(BF16)
    lo = (r - mid.astype(F32)).astype(BF16)
    return hi, mid, lo


def _dot_ones(ones_bf16, x):
    hi, mid, lo = _split3(x)
    return _dot(ones_bf16, hi) + _dot(ones_bf16, mid) + _dot(ones_bf16, lo)


def _silu(x):
    return x * jax.nn.sigmoid(x)


def _dsilu(x):
    s = jax.nn.sigmoid(x)
    return s * (1.0 + x * (1.0 - s))


def _mesh_pos():
    return lax.axis_index("x"), lax.axis_index("y"), lax.axis_index("c")


class _SplitCopy:
    def __init__(self, src, dst, sem):
        self.src, self.dst, self.sem = src, dst, sem

    def start(self):
        rows = self.src.shape[0] // COPY_PARTS
        for p in range(COPY_PARTS):
            chunk = pl.ds(p * rows, rows)
            pltpu.make_async_copy(self.src.at[chunk], self.dst.at[chunk], self.sem).start()

    def wait(self):
        pltpu.make_async_copy(self.src, self.dst, self.sem).wait()


def _shard_of(ref, a, d):
    if a == 0:
        return ref.at[:, pl.ds(pl.multiple_of(d * SHARD_COLS, LANES), SHARD_COLS)]
    if a == 2:
        return ref.at[:, pl.ds(pl.multiple_of(d * LANES, LANES), LANES)]
    return ref.at[pl.ds(pl.multiple_of(d * 128, 128), 128), :]


FULL_SHAPES = ((D, IN_COLS), (D, D), (512, D), (D, D))
SHARD_SHAPES = ((D, SHARD_COLS), (128, D), (512, 128), (128, D))


def _allgather_steps(ids, ins, outs, send_sems, recv_sems, local_sems):
    n = len(ids)
    x, y, c = _mesh_pos()
    me, sibling = (x, y, c), (x, y, 1 - c)
    chips = [(1 - x, y), (x, 1 - y), (1 - x, 1 - y)]

    def blk(a, p):
        return _shard_of(outs[a], ids[a], 4 * p[0] + 2 * p[1] + p[2])

    def copy(a, k, block, to, src=None):
        return pltpu.make_async_remote_copy(
            src_ref=blk(a, block) if src is None else src, dst_ref=blk(a, block),
            send_sem=send_sems.at[a * 7 + k], recv_sem=recv_sems.at[a * 7 + k],
            device_id=to, device_id_type=MESH)

    mine = [pltpu.make_async_copy(ins[a], blk(a, me), local_sems.at[a]) for a in range(n)]
    first = []
    for a in range(n):
        first += [copy(a, 1 + j, me, (*chip, c), src=ins[a]) for j, chip in enumerate(chips)]
    for a in range(n):
        first.append(copy(a, 0, me, sibling, src=ins[a]))
    passed = [copy(a, 4 + j, (*chip, c), sibling) for j, chip in enumerate(chips) for a in range(n)]

    def start():
        for cp in mine + first:
            cp.start()

    def middle():
        for j, chip in enumerate(chips):
            for a in range(n):
                copy(a, 1 + j, (*chip, c), me).wait_recv()
                passed[j * n + a].start()

    def end():
        for a in range(n):
            copy(a, 0, sibling, me).wait_recv()
        for j, chip in enumerate(chips):
            for a in range(n):
                copy(a, 4 + j, (*chip, 1 - c), me).wait_recv()
        for cp in first + passed:
            cp.wait_send()
        for cp in mine:
            cp.wait()

    return start, middle, end


def _in_proj_gather(h, w_shard):
    half = S // 2
    slab = (D, SHARD_COLS)

    def body(h_hbm, w_hbm, z_hbm, wfull_hbm, h_buf, land, zstage,
             h_sem, own_sem, z_sem, wout_sem, send_sems, recv_sems):
        x, y, c = _mesh_pos()
        sibling = (x, y, 1 - c)
        north = c == 1

        def chips_of(first_x):
            near = (jnp.where(first_x, 1 - x, x), jnp.where(first_x, y, 1 - y))
            far = (jnp.where(first_x, x, 1 - x), jnp.where(first_x, 1 - y, y))
            return [near, far, (1 - x, 1 - y)]

        mine, theirs = chips_of(north), chips_of(jnp.logical_not(north))

        def dev(chip, core):
            return 4 * chip[0] + 2 * chip[1] + core

        block_of = ([dev((x, y), c), dev((x, y), 1 - c)] + [dev(q, c) for q in mine]
                    + [dev(q, 1 - c) for q in theirs])

        def cols(d):
            if isinstance(d, int):
                return pl.ds(d * SHARD_COLS, SHARD_COLS)
            return pl.ds(pl.multiple_of(d * SHARD_COLS, LANES), SHARD_COLS)

        def send(k, src, dst_slot, to):
            return pltpu.make_async_remote_copy(
                src_ref=src, dst_ref=land.at[dst_slot], send_sem=send_sems.at[k], recv_sem=recv_sems.at[k],
                device_id=to, device_id_type=MESH)

        def to_sibling():
            return send(0, w_hbm, 1, sibling)

        def to_chip(j):
            if j == 2:
                return send(3, land.at[2], 4, (*mine[1], c))
            return send(1 + j, w_hbm, 2 + j, (*mine[j], c))

        def pass_on(j):
            return send(4 + j, land.at[2 + j], 5 + j, sibling)

        own = _SplitCopy(w_hbm, land.at[0], own_sem)
        h_copy = _SplitCopy(h_hbm, h_buf, h_sem)
        own.start()
        h_copy.start()
        to_sibling().start()
        to_chip(0).start()
        h_copy.wait()
        own.wait()

        def multiply(slot, n_done):
            d = block_of[slot]
            out = _SplitCopy(land.at[slot], wfull_hbm.at[:, cols(d)], wout_sem.at[slot])
            out.start()
            for r in range(2):
                rows = pl.ds(r * half, half)
                zc = _SplitCopy(zstage.at[r], z_hbm.at[rows, cols(d)], z_sem.at[r])
                if n_done > 0:
                    zc.wait()
                zstage[r] = _dot(h_buf[r * half:(r + 1) * half, :], land[slot])
                zc.start()
            return out

        outs = [multiply(0, 0)]
        to_sibling().wait_recv()
        outs.append(multiply(1, 1))
        done = 2
        for j in range(3):
            to_chip(j).wait_recv()
            pass_on(j).start()
            to_chip(j).wait_send()
            if j < 2:
                to_chip(j + 1).start()
            outs.append(multiply(2 + j, done))
            pass_on(j).wait_recv()
            outs.append(multiply(5 + j, done + 1))
            done += 2
        for r in range(2):
            _SplitCopy(zstage.at[r], z_hbm.at[pl.ds(r * half, half), cols(0)], z_sem.at[r]).wait()
        for out in outs:
            out.wait()
        to_sibling().wait_send()
        for j in range(3):
            pass_on(j).wait_send()

    any_spec = pl.BlockSpec(memory_space=pl.ANY)
    return pl.pallas_call(
        body, name="in_proj_gather",
        out_shape=[jax.ShapeDtypeStruct((S, IN_COLS), F32), jax.ShapeDtypeStruct((D, IN_COLS), BF16)],
        in_specs=[any_spec] * 2, out_specs=[any_spec] * 2,
        scratch_shapes=[pltpu.VMEM((S, D), BF16), pltpu.VMEM((8,) + slab, BF16), pltpu.VMEM((2, half, SHARD_COLS), F32),
                        pltpu.SemaphoreType.DMA, pltpu.SemaphoreType.DMA, pltpu.SemaphoreType.DMA((2,)),
                        pltpu.SemaphoreType.DMA((8,)), pltpu.SemaphoreType.DMA((7,)), pltpu.SemaphoreType.DMA((7,))],
        compiler_params=_cp(),
    )(h, w_shard)


def _exchange_sibling(ids, gb):
    n = len(gb)

    def body(*refs):
        ins, outs = refs[:n], refs[n:2 * n]
        send_sems, recv_sems = refs[2 * n:]
        x, y, c = _mesh_pos()
        sibling = (x, y, 1 - c)
        copies = []
        for i, a in enumerate(ids):
            for q in range(4):
                copies.append(pltpu.make_async_remote_copy(
                    src_ref=_shard_of(ins[i], a, 2 * q + (1 - c)), dst_ref=outs[i].at[q],
                    send_sem=send_sems.at[i * 4 + q], recv_sem=recv_sems.at[i * 4 + q],
                    device_id=sibling, device_id_type=MESH))
        for cp in copies:
            cp.start()
        for cp in copies:
            cp.wait()

    any_spec = pl.BlockSpec(memory_space=pl.ANY)
    return pl.pallas_call(
        body, name="grads_to_sibling",
        out_shape=[jax.ShapeDtypeStruct((4,) + SHARD_SHAPES[a], BF16) for a in ids],
        in_specs=[any_spec] * n, out_specs=[any_spec] * n,
        scratch_shapes=[pltpu.SemaphoreType.DMA((4 * n,)), pltpu.SemaphoreType.DMA((4 * n,))],
    )(*gb)


def _exchange_chips_steps(ins, outs, send_sems, recv_sems):
    x, y, c = _mesh_pos()
    chips = [(1 - x, y), (x, 1 - y), (1 - x, 1 - y)]
    copies = []
    for a in range(len(ins)):
        for k, chip in enumerate(chips):
            copies.append(pltpu.make_async_remote_copy(
                src_ref=ins[a].at[2 * chip[0] + chip[1]], dst_ref=outs[a].at[k],
                send_sem=send_sems.at[a * 3 + k], recv_sem=recv_sems.at[a * 3 + k],
                device_id=(*chip, c), device_id_type=MESH))

    def start():
        for cp in copies:
            cp.start()

    def end():
        for cp in copies:
            cp.wait()

    return start, end


def _gather_small_steps(small_ref, small_out, ssend, srecv, local_sem):
    x, y, c = _mesh_pos()
    me = 4 * x + 2 * y + c
    copies = []
    for r in range(1, NDEV):
        peer = (1 - x if r & 4 else x, 1 - y if r & 2 else y, 1 - c if r & 1 else c)
        copies.append(pltpu.make_async_remote_copy(
            src_ref=small_ref, dst_ref=small_out.at[me],
            send_sem=ssend.at[r - 1], recv_sem=srecv.at[r - 1],
            device_id=peer, device_id_type=MESH))
    own = pltpu.make_async_copy(small_ref, small_out.at[me], local_sem)

    def start():
        own.start()
        for cp in copies:
            cp.start()

    def end():
        for cp in copies:
            cp.wait()
        own.wait()

    return start, end


def _gather_small(small):
    def body(small_ref, small_out, ssend, srecv, local_sem):
        start, end = _gather_small_steps(small_ref, small_out, ssend, srecv, local_sem)
        start()
        end()

    any_spec = pl.BlockSpec(memory_space=pl.ANY)
    return pl.pallas_call(
        body, name="gather_small",
        out_shape=jax.ShapeDtypeStruct((NDEV,) + small.shape, F32),
        in_specs=[any_spec], out_specs=any_spec,
        scratch_shapes=[pltpu.SemaphoreType.DMA((NDEV - 1,)), pltpu.SemaphoreType.DMA((NDEV - 1,)),
                        pltpu.SemaphoreType.DMA],
    )(small)


def _shard_tiles(a):
    rows, cols = SHARD_SHAPES[a]
    tr = min(rows, 256)
    return (tr, cols), rows // tr


def _full_index(a, d, i):
    (tr, _), nt = _shard_tiles(a)
    if a in (0, 2):
        return (i, d)
    return (d * nt + i, 0)


def _cast_bf16(x, name):
    rows, cols = x.shape
    tr = min(rows, 256)

    def body(x_ref, o_ref):
        o_ref[...] = x_ref[...].astype(BF16)

    return pl.pallas_call(
        body, name=name, out_shape=jax.ShapeDtypeStruct(x.shape, BF16), grid=(rows // tr,),
        in_specs=[pl.BlockSpec((tr, cols), lambda i: (i, 0))],
        out_specs=pl.BlockSpec((tr, cols), lambda i: (i, 0)),
        compiler_params=_cp(("parallel",)),
    )(x)


def _chip_partials(a, g_full, r1, core):
    tile, nt = _shard_tiles(a)

    def body(c_ref, g_ref, r_ref, o_ref):
        o_ref[0] = (g_ref[...] + r_ref[0].astype(F32)).astype(BF16)

    grid_spec = pltpu.PrefetchScalarGridSpec(
        num_scalar_prefetch=1, grid=(4, nt),
        in_specs=[pl.BlockSpec(tile, lambda q, i, c: _full_index(a, 2 * q + c[0], i)),
                  pl.BlockSpec((1,) + tile, lambda q, i, c: (q, i, 0))],
        out_specs=pl.BlockSpec((1,) + tile, lambda q, i, c: (q, i, 0)))
    return pl.pallas_call(
        body, name=f"chip_partials_{a}", grid_spec=grid_spec,
        out_shape=jax.ShapeDtypeStruct((4,) + SHARD_SHAPES[a], BF16),
        compiler_params=_cp(("parallel", "parallel")),
    )(core, g_full, r1)


def _adam(w, g, m, v):
    m = ADAM_B1 * m + (1.0 - ADAM_B1) * g
    v = ADAM_B2 * v + (1.0 - ADAM_B2) * (g * g)
    m_hat = m / (1.0 - ADAM_B1 ** ADAM_STEP)
    v_hat = v / (1.0 - ADAM_B2 ** ADAM_STEP)
    delta = -ADAM_LR * (m_hat / (jnp.sqrt(v_hat) + ADAM_EPS) + ADAM_WD * w)
    return delta, m, v


def _reduce_and_update(a, w, m, v, g_full, r1, r2, pos):
    tile, nt = _shard_tiles(a)

    def body(p_ref, w_ref, m_ref, v_ref, g_ref, r1_ref, r2_ref, go_ref, do_ref, mo_ref, vo_ref):
        g = g_ref[...] + r1_ref[0].astype(F32)
        g = g + r2_ref[0].astype(F32)
        g = g + r2_ref[1].astype(F32)
        g = g + r2_ref[2].astype(F32)
        delta, m_new, v_new = _adam(w_ref[...], g, m_ref[...], v_ref[...])
        go_ref[...] = g
        do_ref[...] = delta
        mo_ref[...] = m_new
        vo_ref[...] = v_new

    own = pl.BlockSpec(tile, lambda i, p: (i, 0))
    grid_spec = pltpu.PrefetchScalarGridSpec(
        num_scalar_prefetch=1, grid=(nt,),
        in_specs=[own, own, own,
                  pl.BlockSpec(tile, lambda i, p: _full_index(a, p[0], i)),
                  pl.BlockSpec((1,) + tile, lambda i, p: (p[1], i, 0)),
                  pl.BlockSpec((3,) + tile, lambda i, p: (0, i, 0))],
        out_specs=[own] * 4)
    shp = jax.ShapeDtypeStruct(w.shape, F32)
    return pl.pallas_call(
        body, name=f"reduce_update_{a}", grid_spec=grid_spec, out_shape=[shp] * 4,
        compiler_params=_cp(("parallel",)),
    )(pos, w, m, v, g_full, r1, r2)


def _reduce_own_and_update(w, m, v, g_chip, r2):
    tile, nt = _shard_tiles(0)

    def body(w_ref, m_ref, v_ref, g_ref, r2_ref, go_ref, do_ref, mo_ref, vo_ref):
        g = g_ref[...] + r2_ref[0].astype(F32)
        g = g + r2_ref[1].astype(F32)
        delta, m_new, v_new = _adam(w_ref[...], g, m_ref[...], v_ref[...])
        go_ref[...] = g
        do_ref[...] = delta
        mo_ref[...] = m_new
        vo_ref[...] = v_new

    own = pl.BlockSpec(tile, lambda i: (i, 0))
    shp = jax.ShapeDtypeStruct(w.shape, F32)
    return pl.pallas_call(
        body, name="reduce_update_0", grid=(nt,), out_shape=[shp] * 4,
        in_specs=[own, own, own, own, pl.BlockSpec((2,) + tile, lambda i: (0, i, 0))], out_specs=[own] * 4,
        compiler_params=_cp(("parallel",)),
    )(w, m, v, g_chip, r2)


def _small_update(gathered, norm_w, lb_logits, hnw, fnw, moments):
    m_nw, m_lb, m_hn, m_fn, v_nw, v_lb, v_hn, v_fn = moments

    def body(g_ref, nw, lb, hn, fn, mnw, mlb, mhn, mfn, vnw, vlb, vhn, vfn,
             loss_o, g_nw, g_lb, g_hn, g_fn, d_nw, d_lb, d_hn, d_fn,
             mo_nw, mo_lb, mo_hn, mo_fn, vo_nw, vo_lb, vo_hn, vo_fn):
        tot = g_ref[0]
        for d in range(1, NDEV):
            tot = tot + g_ref[d]
        loss_o[...] = tot[4:5, 0:LANES]
        logits = lb[...]
        lbv = jax.nn.sigmoid(logits[0:1] - logits[1:2])
        chain = tot[1:2] * lbv * (1.0 - lbv)
        grads = (tot[0:1], jnp.concatenate([chain, -chain], axis=0), tot[2:3, 0:LANES], tot[3:4])
        outs = ((nw, mnw, vnw, g_nw, d_nw, mo_nw, vo_nw), (lb, mlb, vlb, g_lb, d_lb, mo_lb, vo_lb),
                (hn, mhn, vhn, g_hn, d_hn, mo_hn, vo_hn), (fn, mfn, vfn, g_fn, d_fn, mo_fn, vo_fn))
        for g, (w_r, m_r, v_r, g_o, d_o, m_o, v_o) in zip(grads, outs):
            delta, m_new, v_new = _adam(w_r[...], g, m_r[...], v_r[...])
            g_o[...] = g
            d_o[...] = delta
            m_o[...] = m_new
            v_o[...] = v_new

    shapes = [norm_w.shape, lb_logits.shape, hnw.shape, fnw.shape]
    out_shape = [jax.ShapeDtypeStruct((1, LANES), F32)] + [jax.ShapeDtypeStruct(s, F32) for s in shapes] * 4
    return pl.pallas_call(body, name="small_update", out_shape=out_shape, compiler_params=_cp())(
        gathered, norm_w, lb_logits, hnw, fnw, m_nw, m_lb, m_hn, m_fn, v_nw, v_lb, v_hn, v_fn)


def _rmsnorm_in(x, norm_w):
    tr = 512

    def body(x_ref, w_ref, h_ref):
        xv = x_ref[...]
        r = lax.rsqrt(jnp.mean(xv * xv, axis=-1, keepdims=True) + EPS)
        h_ref[...] = (xv * r * w_ref[...]).astype(BF16)

    return pl.pallas_call(
        body, name="rmsnorm_in", out_shape=jax.ShapeDtypeStruct((S, D), BF16), grid=(S // tr,),
        in_specs=[pl.BlockSpec((tr, D), lambda i: (i, 0)), pl.BlockSpec((1, D), lambda i: (0, 0))],
        out_specs=pl.BlockSpec((tr, D), lambda i: (i, 0)),
        compiler_params=_cp(("parallel",)),
    )(x, norm_w)


def _in_proj(h, w_in):
    tn = 1024

    def body(h_ref, w_ref, z_ref):
        z_ref[...] = _dot(h_ref[...], w_ref[...])

    return pl.pallas_call(
        body, name="in_proj", out_shape=jax.ShapeDtypeStruct((S, IN_COLS), F32), grid=(IN_COLS // tn,),
        in_specs=[pl.BlockSpec((S, D), lambda j: (0, 0)), pl.BlockSpec((D, tn), lambda j: (0, j))],
        out_specs=pl.BlockSpec((S, tn), lambda j: (0, j)),
        compiler_params=_cp(("parallel",)),
    )(h, w_in)


def _block_tri(n, block, upper=False):
    r = lax.broadcasted_iota(jnp.int32, (n, n), 0)
    c = lax.broadcasted_iota(jnp.int32, (n, n), 1)
    keep = (c >= r) if upper else (c <= r)
    return jnp.where(keep & ((r // block) == (c // block)), 1.0, 0.0).astype(BF16)


def _tril_mask(n):
    r = lax.broadcasted_iota(jnp.int32, (n, n), 0)
    c = lax.broadcasted_iota(jnp.int32, (n, n), 1)
    return c <= r


def _chunk_scores(q, k, b, bex, r0, mask):
    parts, qs_l, ks_l, ek_l, eq_l = [], [], [], [], []
    for i in range(CHUNK // SUB):
        ri = slice(r0 + SUB * i, r0 + SUB * (i + 1))
        seen = slice(r0, r0 + SUB * (i + 1))
        base = bex[r0 + SUB * i:r0 + SUB * i + 1]
        eq = jnp.exp(b[ri] - base)
        ek = jnp.exp(jnp.minimum(base - b[seen], EXP_CLAMP))
        ks = k[seen] * ek
        if i + 1 < CHUNK // SUB:
            rest = jnp.zeros((CHUNK - SUB * (i + 1), 128), F32)
            ek, ks = jnp.concatenate([ek, rest], axis=0), jnp.concatenate([ks, rest], axis=0)
        qs = q[ri] * eq
        parts.append(_dot_nt(qs.astype(BF16), ks.astype(BF16)))
        qs_l.append(qs)
        ks_l.append(ks)
        ek_l.append(ek)
        eq_l.append(eq)
    return jnp.where(mask, jnp.concatenate(parts, axis=0), 0.0), qs_l, ks_l, ek_l, eq_l


def _hgrn_cols(hq, hf, hi, lb):
    sg = jax.nn.sigmoid(hf)
    f = lb + (1.0 - lb) * sg
    g = jnp.log(f)
    b = _dot_ones(_block_tri(HBLK, CHUNK), g)
    return _silu(hq), 1.0 - f, g, hi, sg, f, b


GATHER_IDS = (1, 2, 3)


def _hgrn_fwd(z, lbv, hnw, shards):
    ntb, nch = S // HBLK, HBLK // CHUNK
    n = len(GATHER_IDS)

    def body(hq_ref, hf_ref, hi_ref, hg_ref, lb_ref, hnw_ref, s0, s1, s2, o_ref, oa_ref, st_ref, f0, f1, f2,
             state, send_sems, recv_sems, local_sems):
        start, middle, end = _allgather_steps(GATHER_IDS, (s0, s1, s2), (f0, f1, f2), send_sems, recv_sems, local_sems)

        @pl.when(pl.program_id(0) == 0)
        def _():
            state[...] = jnp.zeros_like(state)
            start()

        pl.when(pl.program_id(0) == ntb // 2)(middle)

        q_a, k_a, g_a, v_a, _, _, b_a = _hgrn_cols(hq_ref[...], hf_ref[...], hi_ref[...], lb_ref[...])
        bex_a = b_a - g_a
        eb_a = jnp.exp(b_a)
        mask = _tril_mask(CHUNK)
        hg = hg_ref[...]
        w = hnw_ref[...]
        for h in range(HEADS):
            cols = slice(128 * h, 128 * h + 128)
            q, k, v, b, bex, eb = q_a[:, cols], k_a[:, cols], v_a[:, cols], b_a[:, cols], bex_a[:, cols], eb_a[:, cols]
            st = state[h]
            outs = []
            for c in range(nch):
                r0 = c * CHUNK
                rows = slice(r0, r0 + CHUNK)
                a = _chunk_scores(q, k, b, bex, r0, mask)[0]
                vb = v[rows].astype(BF16)
                b_last = b[r0 + CHUNK - 1:r0 + CHUNK]
                qe = (q[rows] * eb[rows]).astype(BF16)
                outs.append(_dot(a.astype(BF16), vb) + _dot_nt(qe, st.astype(BF16)))
                st_ref[h, c] = st
                ke = (k[rows] * jnp.exp(b_last - b[rows])).astype(BF16)
                st = st * jnp.exp(b_last) + _dot_tn(vb, ke)
            state[h] = st
            o = jnp.concatenate(outs, axis=0)
            o_ref[:, cols] = o
            r = lax.rsqrt(jnp.mean(o * o, axis=-1, keepdims=True) + EPS)
            oa_ref[:, cols] = (o * r * w * _silu(hg[:, cols])).astype(BF16)

        pl.when(pl.program_id(0) == ntb - 1)(end)

    def zcol(j):
        return pl.BlockSpec((HBLK, D), lambda t: (t, j))

    out_blk = pl.BlockSpec((HBLK, D), lambda t: (t, 0))
    any_spec = pl.BlockSpec(memory_space=pl.ANY)
    return pl.pallas_call(
        body, name="hgrn_fwd", grid=(ntb,),
        out_shape=[jax.ShapeDtypeStruct((S, D), F32), jax.ShapeDtypeStruct((S, D), BF16),
                   jax.ShapeDtypeStruct((HEADS, S // CHUNK, 128, 128), F32)]
        + [jax.ShapeDtypeStruct(FULL_SHAPES[a], BF16) for a in GATHER_IDS],
        in_specs=[zcol(0), zcol(1), zcol(2), zcol(3),
                  pl.BlockSpec((1, D), lambda t: (0, 0)), pl.BlockSpec((1, 128), lambda t: (0, 0))] + [any_spec] * n,
        out_specs=[out_blk, out_blk, pl.BlockSpec((HEADS, nch, 128, 128), lambda t: (0, t, 0, 0))] + [any_spec] * n,
        scratch_shapes=[pltpu.VMEM((HEADS, 128, 128), F32), pltpu.SemaphoreType.DMA((7 * n,)),
                        pltpu.SemaphoreType.DMA((7 * n,)), pltpu.SemaphoreType.DMA((n,))],
        compiler_params=_cp(("arbitrary",)),
    )(z, z, z, z, lbv, hnw, *shards)


def _half_mask():
    lane = lax.broadcasted_iota(jnp.int32, (1, LANES), 1)
    return (lane % 64) < 32


def _rope(t, cc, ss, first_half):
    partner = jnp.where(first_half, pltpu.roll(t, 96, 1), pltpu.roll(t, 32, 1))
    return t * cc + partner * ss


def _attn_masks():
    i = lax.broadcasted_iota(jnp.int32, (128, 128), 0)
    j = lax.broadcasted_iota(jnp.int32, (128, 128), 1)
    return j >= i, j <= i


def _to_residues_dyn(g, dst, src, row0=0, dtype=None):
    for gi, dil in enumerate((1, 4, 16)):
        m = S // dil

        @pl.when(g == gi)
        def _(dil=dil, m=m):
            for r in range(dil):
                v = src[...] if dil == 1 else src[pl.ds(r, m, stride=dil), :]
                if dtype is not None:
                    v = v.astype(dtype)
                dst[row0 + r * m:row0 + (r + 1) * m, 0:LANES] = v


def _from_residues_dyn(g, dst, src, row0=0):
    for gi, dil in enumerate((1, 4, 16)):
        m = S // dil

        @pl.when(g == gi)
        def _(dil=dil, m=m):
            for r in range(dil):
                v = src[row0 + r * m:row0 + (r + 1) * m, :]
                if dil == 1:
                    dst[...] = v
                else:
                    dst[pl.ds(r, m, stride=dil), :] = v


def _group_blocks(g):
    return jnp.where(g == 0, 16, jnp.where(g == 1, 4, 1))


def _attn_in_specs(extra):
    def zcol(off):
        return pl.BlockSpec((S, LANES), lambda p, g: (0, off + 4 * g + p))

    per_pair = pl.BlockSpec((S, LANES), lambda p, g: (0, p))
    const = pl.BlockSpec((S, LANES), lambda p, g: (0, 0))
    return [zcol(32), zcol(44), zcol(56), pl.BlockSpec((S, LANES), lambda p, g: (0, 68 + p)), const, const] + [per_pair] * extra


def _attn_fwd(z, cc, ss):
    def body(q_ref, k_ref, v_ref, ag_ref, cc_ref, ss_ref, ob_ref, lse_ref, obg_ref,
             tmp, qs, ks, vx, og, mg, lg, o_t, m_t, l_t, o_acc, m_acc, l_acc):
        g = pl.program_id(1)
        first_half = _half_mask()
        prev_ok, cur_ok = _attn_masks()
        lane = lax.broadcasted_iota(jnp.int32, (1, LANES), 1)
        heads = (lane < 64, lane >= 64)
        nblk = _group_blocks(g)

        @pl.when(g == 0)
        def _():
            ks[0:ATT_PAD, :] = jnp.zeros((ATT_PAD, LANES), BF16)
            vx[0:ATT_PAD, 0:LANES] = jnp.zeros((ATT_PAD, LANES), BF16)
            vx[:, LANES:2 * LANES] = jnp.ones((ATT_PAD + S, LANES), BF16)

        tmp[...] = _rope(q_ref[...], cc_ref[...], ss_ref[...], first_half) * ATT_SCALE
        _to_residues_dyn(g, qs, tmp)
        tmp[...] = _rope(k_ref[...], cc_ref[...], ss_ref[...], first_half)
        _to_residues_dyn(g, ks, tmp, ATT_PAD, BF16)
        _to_residues_dyn(g, vx, v_ref, ATT_PAD, BF16)

        def unit(u, carry):
            start = pl.multiple_of(u * 128, 128)
            cur = pl.ds(start, 128)
            pm = prev_ok & ((u & (nblk - 1)) != 0)
            qu = qs[cur, :]
            kcat = ks[pl.ds(start, 256), :]
            vext = vx[pl.ds(start, 256), :]
            o_u = m_u = l_u = None
            for hh in range(2):
                s = _dot_nt(jnp.where(heads[hh], qu, 0.0).astype(BF16), kcat)
                sp = jnp.where(pm, s[:, 0:128], -jnp.inf)
                sc = jnp.where(cur_ok, s[:, 128:256], -jnp.inf)
                m = jnp.max(jnp.maximum(sp, sc), axis=-1, keepdims=True)
                p = jnp.concatenate([jnp.exp(sp - m), jnp.exp(sc - m)], axis=1).astype(BF16)
                ol = _dot(p, vext)
                mb = jnp.broadcast_to(m, (128, LANES))
                if hh == 0:
                    o_u, l_u, m_u = ol[:, 0:128], ol[:, 128:256], mb
                else:
                    o_u = jnp.where(heads[1], ol[:, 0:128], o_u)
                    l_u = jnp.where(heads[1], ol[:, 128:256], l_u)
                    m_u = jnp.where(heads[1], mb, m_u)
            og[cur, :] = o_u
            mg[cur, :] = m_u
            lg[cur, :] = l_u
            return carry

        lax.fori_loop(0, 16, unit, 0, unroll=16)
        _from_residues_dyn(g, o_t, og)
        _from_residues_dyn(g, m_t, mg)
        _from_residues_dyn(g, l_t, lg)

        @pl.when(g == 0)
        def _():
            o_acc[...] = o_t[...]
            m_acc[...] = m_t[...]
            l_acc[...] = l_t[...]

        @pl.when(g > 0)
        def _():
            m_new = jnp.maximum(m_acc[...], m_t[...])
            wa, wb = jnp.exp(m_acc[...] - m_new), jnp.exp(m_t[...] - m_new)
            o_acc[...] = o_acc[...] * wa + o_t[...] * wb
            l_acc[...] = l_acc[...] * wa + l_t[...] * wb
            m_acc[...] = m_new

        @pl.when(g == 2)
        def _():
            ob = o_acc[...] / l_acc[...]
            ob_ref[...] = ob
            lse_ref[...] = m_acc[...] + jnp.log(l_acc[...])
            obg_ref[...] = (ob * _silu(ag_ref[...])).astype(BF16)

    blk = pl.BlockSpec((S, LANES), lambda p, g: (0, p))
    buf = pltpu.VMEM((S, LANES), F32)
    return pl.pallas_call(
        body, name="attn_fwd", grid=(4, 3),
        out_shape=[jax.ShapeDtypeStruct((S, 512), F32), jax.ShapeDtypeStruct((S, 512), F32),
                   jax.ShapeDtypeStruct((S, 512), BF16)],
        in_specs=_attn_in_specs(0), out_specs=[blk, blk, blk],
        scratch_shapes=[buf, buf, pltpu.VMEM((ATT_PAD + S, LANES), BF16), pltpu.VMEM((ATT_PAD + S, 2 * LANES), BF16)] + [buf] * 9,
        compiler_params=_cp(("parallel", "arbitrary")),
    )(z, z, z, z, cc, ss)


def _tail(x, o_a, o_bg, z, target, w_a, w_b, w_out, fnw):
    tm = 256

    def body(x_ref, oa_ref, ob_ref, gpa_ref, gpb_ref, t_ref, wa_ref, wb_ref, wo_ref, fnw_ref,
             dx2_ref, dx2b_ref, dz_hbm, doa_ref, dob_ref, mg_ref, dya_ref, dyb_ref, small_ref, dgp, dgp_sem):
        step = pl.program_id(0)
        slot = step % 2

        def dgp_copy(at_step, at_slot):
            return pltpu.make_async_copy(
                dgp.at[at_slot], dz_hbm.at[pl.ds(pl.multiple_of(at_step * tm, tm), tm), pl.ds(DZ_GATES, 2 * D)],
                dgp_sem.at[at_slot])

        @pl.when(step == 0)
        def _():
            small_ref[...] = jnp.zeros_like(small_ref)

        @pl.when(step >= 2)
        def _():
            dgp_copy(step - 2, slot).wait()

        wa, wb, wo = wa_ref[...], wb_ref[...], wo_ref[...]
        y_a = _dot(oa_ref[...], wa)
        y_b = _dot(ob_ref[...], wb)
        ga = jax.nn.sigmoid(gpa_ref[...])
        gb = jax.nn.sigmoid(gpb_ref[...])
        merged = (ga * y_a + gb * y_b).astype(BF16)
        x2 = x_ref[...] + _dot(merged, wo)
        r2 = lax.rsqrt(jnp.mean(x2 * x2, axis=-1, keepdims=True) + EPS)
        n2 = x2 * r2
        fw = fnw_ref[...]
        err = n2 * fw - t_ref[...]
        loss = 0.5 * jnp.sum(jnp.sum(err * err, axis=-1, keepdims=True), axis=0, keepdims=True) / D
        dy = err * (1.0 / D)
        g_fnw = jnp.sum(dy * n2, axis=0, keepdims=True)
        dn = dy * fw
        dx2 = r2 * (dn - n2 * jnp.mean(dn * n2, axis=-1, keepdims=True))
        dx2b = dx2.astype(BF16)
        dmerged = _dot_nt(dx2b, wo)
        dy_a = (dmerged * ga).astype(BF16)
        dy_b = (dmerged * gb).astype(BF16)
        dx2_ref[...] = dx2
        dx2b_ref[...] = dx2b
        dgp[slot, :, 0:D] = (dmerged * y_a * ga * (1.0 - ga)).astype(BF16)
        dgp[slot, :, D:2 * D] = (dmerged * y_b * gb * (1.0 - gb)).astype(BF16)
        dgp_copy(step, slot).start()
        doa_ref[...] = _dot_nt(dy_a, wa)
        dob_ref[...] = _dot_nt(dy_b, wb)
        mg_ref[...] = merged
        dya_ref[...] = dy_a
        dyb_ref[...] = dy_b
        small_ref[0:1, :] += g_fnw
        small_ref[1:2, :] += jnp.broadcast_to(loss, (1, D))

        @pl.when(step == S // tm - 1)
        def _():
            dgp_copy(step - 1, 1 - slot).wait()
            dgp_copy(step, slot).wait()

    def rows(cols, off=0):
        return pl.BlockSpec((tm, cols), lambda i: (i, off))

    def whole(shape):
        return pl.BlockSpec(shape, lambda i: (0, 0))

    return pl.pallas_call(
        body, name="tail", grid=(S // tm,),
        out_shape=[jax.ShapeDtypeStruct((S, D), F32), jax.ShapeDtypeStruct((S, D), BF16),
                   jax.ShapeDtypeStruct((S, IN_COLS), BF16), jax.ShapeDtypeStruct((S, D), F32),
                   jax.ShapeDtypeStruct((S, 512), F32), jax.ShapeDtypeStruct((S, D), BF16),
                   jax.ShapeDtypeStruct((S, D), BF16), jax.ShapeDtypeStruct((S, D), BF16),
                   jax.ShapeDtypeStruct((8, D), F32)],
        in_specs=[rows(D), rows(D), rows(512), rows(D, 9), rows(D, 10), rows(D),
                  whole((D, D)), whole((512, D)), whole((D, D)), whole((1, D))],
        out_specs=[rows(D), rows(D), pl.BlockSpec(memory_space=pl.ANY), rows(D), rows(512), rows(D), rows(D),
                   rows(D), whole((8, D))],
        scratch_shapes=[pltpu.VMEM((2, tm, 2 * D), BF16), pltpu.SemaphoreType.DMA((2,))],
        compiler_params=_cp(("arbitrary",)),
    )(x, o_a, o_bg, z, z, target, w_a, w_b, w_out, fnw)


def _tn_matmul(a, b, name):
    m, n = a.shape[1], b.shape[1]
    tn = 512

    def body(a_ref, b_ref, o_ref, ob_ref):
        acc = _dot_tn(a_ref[...], b_ref[...])
        o_ref[...] = acc
        ob_ref[...] = acc.astype(BF16)

    out_blk = pl.BlockSpec((m, tn), lambda j: (0, j))
    return pl.pallas_call(
        body, name=name, grid=(n // tn,),
        out_shape=[jax.ShapeDtypeStruct((m, n), F32), jax.ShapeDtypeStruct((m, n), BF16)],
        in_specs=[pl.BlockSpec((S, m), lambda j: (0, 0)), pl.BlockSpec((S, tn), lambda j: (0, j))],
        out_specs=[out_blk, out_blk],
        compiler_params=_cp(("parallel",)),
    )(a, b)


def _hgrn_bwd(z, o, do_a, states, lbv, hnw, partials, dz):
    ntb, nch = S // HBLK, HBLK // CHUNK
    n = len(GATHER_IDS)

    def body(hq_ref, hf_ref, hi_ref, hg_ref, o_ref, doa_ref, st_ref, lb_ref, hnw_ref, p0, p1, p2, dz_in,
             dz_ref, glb_ref, ghn_ref, e0, e1, e2, dstate, send_sems, recv_sems):
        dhq_ref, dhf_ref, dhi_ref, dhg_ref = (dz_ref.at[:, pl.ds(j * D, D)] for j in range(4))
        start, end = _exchange_chips_steps((p0, p1, p2), (e0, e1, e2), send_sems, recv_sems)

        @pl.when(pl.program_id(0) == 0)
        def _():
            dstate[...] = jnp.zeros_like(dstate)
            glb_ref[...] = jnp.zeros_like(glb_ref)
            ghn_ref[...] = jnp.zeros_like(ghn_ref)
            start()

        lb_a = lb_ref[...]
        hq_a, hg_a = hq_ref[...], hg_ref[...]
        q_a, k_a, g_a, v_a, sg_a, f_a, b_a = _hgrn_cols(hq_a, hf_ref[...], hi_ref[...], lb_a)
        bex_a = b_a - g_a
        eb_a = jnp.exp(b_a)
        w = hnw_ref[...]
        mask = _tril_mask(CHUNK)
        upper = _block_tri(CHUNK, CHUNK, upper=True)
        for h in range(HEADS):
            cols = slice(128 * h, 128 * h + 128)
            q, k, v, b, bex, eb = q_a[:, cols], k_a[:, cols], v_a[:, cols], b_a[:, cols], bex_a[:, cols], eb_a[:, cols]
            hq, hg, sg, f, lb = hq_a[:, cols], hg_a[:, cols], sg_a[:, cols], f_a[:, cols], lb_a[:, cols]
            ov, doa = o_ref[:, cols], doa_ref[:, cols]
            r = lax.rsqrt(jnp.mean(ov * ov, axis=-1, keepdims=True) + EPS)
            n = ov * r
            sil = _silu(hg)
            dhg_ref[:, cols] = (doa * n * w * _dsilu(hg)).astype(BF16)
            ghn_ref[h] += jnp.sum(doa * sil * n, axis=0, keepdims=True)
            dn = doa * sil * w
            do = r * (dn - n * jnp.mean(dn * n, axis=-1, keepdims=True))

            dst = dstate[h]
            dq_l, dk_l, dv_l, dg_l = [None] * nch, [None] * nch, [None] * nch, [None] * nch
            for c in reversed(range(nch)):
                r0 = c * CHUNK
                rows = slice(r0, r0 + CHUNK)
                st = st_ref[h, c]
                bc, kc, qc = b[rows], k[rows], q[rows]
                vb, dob = v[rows].astype(BF16), do[rows].astype(BF16)
                b_last = bc[CHUNK - 1:CHUNK]
                e_last = jnp.exp(b_last)
                ekl = jnp.exp(b_last - bc)
                dstb = dst.astype(BF16)
                a, qs_l, ks_l, ek_l, eq_l = _chunk_scores(q, k, b, bex, r0, mask)
                da = jnp.where(mask, _dot_nt(dob, vb), 0.0)
                dv_l[c] = _dot_tn(a.astype(BF16), dob) + _dot_nt((kc * ekl).astype(BF16), dstb)
                dq_inter = _dot(dob, st.astype(BF16)) * eb[rows]
                dk_state = _dot(vb, dstb) * ekl
                dq_parts, dk_intra = [], jnp.zeros((CHUNK, 128), F32)
                dab = da.astype(BF16)
                for i in range(CHUNK // SUB):
                    da_i = dab[SUB * i:SUB * (i + 1)]
                    ks_hi, ks_lo = _split2(ks_l[i])
                    qs_hi, qs_lo = _split2(qs_l[i])
                    dq_parts.append((_dot(da_i, ks_hi) + _dot(da_i, ks_lo)) * eq_l[i])
                    dk_intra = dk_intra + (_dot_tn(da_i, qs_hi) + _dot_tn(da_i, qs_lo)) * ek_l[i]
                dq = jnp.concatenate(dq_parts, axis=0) + dq_inter
                dk = dk_intra + dk_state
                last = (e_last * jnp.sum(st * dst, axis=0, keepdims=True)
                        + jnp.sum(kc * dk_state, axis=0, keepdims=True))
                dg_l[c] = _dot_ones(upper, qc * dq - kc * dk) + last
                dq_l[c], dk_l[c] = dq, dk
                dst = dst * e_last + _dot_tn(dob, (qc * eb[rows]).astype(BF16))
            dstate[h] = dst
            dq, dk = jnp.concatenate(dq_l, axis=0), jnp.concatenate(dk_l, axis=0)
            dg, dv = jnp.concatenate(dg_l, axis=0), jnp.concatenate(dv_l, axis=0)
            dhq_ref[:, cols] = (dq * _dsilu(hq)).astype(BF16)
            dhi_ref[:, cols] = dv.astype(BF16)
            df = dg / f - dk
            dhf_ref[:, cols] = (df * (1.0 - lb) * sg * (1.0 - sg)).astype(BF16)
            glb_ref[:, cols] += jnp.sum(df * (1.0 - sg), axis=0, keepdims=True)

        pl.when(pl.program_id(0) == ntb - 1)(end)

    def rev(t):
        return ntb - 1 - t

    def zcol(j):
        return pl.BlockSpec((HBLK, D), lambda t: (rev(t), j))

    blk = pl.BlockSpec((HBLK, D), lambda t: (rev(t), 0))
    any_spec = pl.BlockSpec(memory_space=pl.ANY)
    return pl.pallas_call(
        body, name="hgrn_bwd", grid=(ntb,),
        out_shape=[jax.ShapeDtypeStruct((S, IN_COLS), BF16)]
        + [jax.ShapeDtypeStruct((1, D), F32), jax.ShapeDtypeStruct((HEADS, 1, 128), F32)]
        + [jax.ShapeDtypeStruct((3,) + SHARD_SHAPES[a], BF16) for a in GATHER_IDS],
        in_specs=[zcol(0), zcol(1), zcol(2), zcol(3), blk, blk,
                  pl.BlockSpec((HEADS, nch, 128, 128), lambda t: (0, rev(t), 0, 0)),
                  pl.BlockSpec((1, D), lambda t: (0, 0)), pl.BlockSpec((1, 128), lambda t: (0, 0))]
        + [any_spec] * (n + 1),
        out_specs=[pl.BlockSpec((HBLK, DZ_ATT), lambda t: (rev(t), 0)), pl.BlockSpec((1, D), lambda t: (0, 0)),
                   pl.BlockSpec((HEADS, 1, 128), lambda t: (0, 0, 0))] + [any_spec] * n,
        scratch_shapes=[pltpu.VMEM((HEADS, 128, 128), F32), pltpu.SemaphoreType.DMA((3 * n,)),
                        pltpu.SemaphoreType.DMA((3 * n,))],
        input_output_aliases={9 + n: 0},
        compiler_params=_cp(("arbitrary",)),
    )(z, z, z, z, o, do_a, states, lbv, hnw, *partials, dz)


def _attn_bwd(z, cc, ss, ob, lse, do_bg, dz):
    def body(q_ref, k_ref, v_ref, ag_ref, cc_ref, ss_ref, ob_ref, lse_ref, dobg_ref, dz_in, dz_hbm,
             tmp, qs, ks, vs, dos, dqs, dks, dvs, dkp, dvp, do_t, ls0_t, ls1_t, dl0_t, dl1_t, ls0, ls1, dl0, dl1,
             stage, stage_sem):
        pair, g = pl.program_id(0), pl.program_id(1)

        def out_copy(j):
            tile = DZ_ATT // LANES + (36 + pair if j == 3 else 12 * j + 4 * g + pair)
            return pltpu.make_async_copy(
                stage.at[j], dz_hbm.at[:, pl.ds(pl.multiple_of(tile * LANES, LANES), LANES)], stage_sem.at[j])

        def restage(j, value):
            pl.when(pair * 3 + g > 0)(lambda: out_copy(j).wait())
            stage[j] = value
            out_copy(j).start()

        pl.when(g == 2)(lambda: out_copy(3).wait())
        first_half = _half_mask()
        prev_ok, cur_ok = _attn_masks()
        lane = lax.broadcasted_iota(jnp.int32, (1, LANES), 1)
        heads = (lane < 64, lane >= 64)
        nblk = _group_blocks(g)
        cc_v, ss_v = cc_ref[...], ss_ref[...]

        @pl.when(g == 0)
        def _():
            ag, obv, dobg = ag_ref[...], ob_ref[...], dobg_ref[...]
            stage[3] = (dobg * obv * _dsilu(ag)).astype(BF16)
            out_copy(3).start()
            dob = dobg * _silu(ag)
            do_t[...] = dob
            prod = dob * obv
            dl = jnp.concatenate(
                [jnp.broadcast_to(jnp.sum(prod[:, 0:64], axis=-1, keepdims=True), (S, 64)),
                 jnp.broadcast_to(jnp.sum(prod[:, 64:128], axis=-1, keepdims=True), (S, 64))], axis=1)
            dl_sw = pltpu.roll(dl, 64, 1)
            dl0_t[...] = jnp.where(heads[0], dl, dl_sw)
            dl1_t[...] = jnp.where(heads[0], dl_sw, dl)
            ls = lse_ref[...]
            ls_sw = pltpu.roll(ls, 64, 1)
            ls0_t[...] = jnp.where(heads[0], ls, ls_sw)
            ls1_t[...] = jnp.where(heads[0], ls_sw, ls)
            ks[0:ATT_PAD, :] = jnp.zeros((ATT_PAD, LANES), BF16)
            vs[0:ATT_PAD, :] = jnp.zeros((ATT_PAD, LANES), BF16)

        tmp[...] = _rope(q_ref[...], cc_v, ss_v, first_half) * ATT_SCALE
        _to_residues_dyn(g, qs, tmp)
        tmp[...] = _rope(k_ref[...], cc_v, ss_v, first_half)
        _to_residues_dyn(g, ks, tmp, ATT_PAD, BF16)
        _to_residues_dyn(g, vs, v_ref, ATT_PAD, BF16)
        _to_residues_dyn(g, dos, do_t)
        _to_residues_dyn(g, ls0, ls0_t)
        _to_residues_dyn(g, ls1, ls1_t)
        _to_residues_dyn(g, dl0, dl0_t)
        _to_residues_dyn(g, dl1, dl1_t)
        lss, dls = (ls0, ls1), (dl0, dl1)

        def unit(u, carry):
            start = pl.multiple_of(u * 128, 128)
            cur = pl.ds(start, 128)
            both = pl.ds(start, 256)
            pm = prev_ok & ((u & (nblk - 1)) != 0)
            qu, dou = qs[cur, :], dos[cur, :]
            kcat, vcat = ks[both, :], vs[both, :]
            dq_u = None
            q_l, do_l, ds_l, p_l = [], [], [], []
            for hh in range(2):
                q_h = jnp.where(heads[hh], qu, 0.0).astype(BF16)
                do_h = jnp.where(heads[hh], dou, 0.0).astype(BF16)
                s = _dot_nt(q_h, kcat)
                dp = _dot_nt(do_h, vcat)
                lse_h, dl_h = lss[hh][cur, :], dls[hh][cur, :]
                pp = jnp.where(pm, jnp.exp(s[:, 0:128] - lse_h), 0.0)
                pc = jnp.where(cur_ok, jnp.exp(s[:, 128:256] - lse_h), 0.0)
                ds = jnp.concatenate([pp * (dp[:, 0:128] - dl_h), pc * (dp[:, 128:256] - dl_h)], axis=1).astype(BF16)
                dq = _dot(ds, kcat)
                dq_u = dq if hh == 0 else jnp.where(heads[1], dq, dq_u)
                q_l.append(q_h)
                do_l.append(do_h)
                ds_l.append(ds)
                p_l.append(jnp.concatenate([pp, pc], axis=1).astype(BF16))
            dkcat = _dot_tn(jnp.concatenate(ds_l, axis=0), jnp.concatenate(q_l, axis=0))
            dvcat = _dot_tn(jnp.concatenate(p_l, axis=0), jnp.concatenate(do_l, axis=0))
            dkp[cur, :] = dkcat[0:128]
            dks[cur, :] = dkcat[128:256]
            dvp[cur, :] = dvcat[0:128]
            dvs[cur, :] = dvcat[128:256]
            dqs[cur, :] = dq_u
            return carry

        lax.fori_loop(0, 16, unit, 0, unroll=ATT_UNROLL)
        dks[0:S - 128, :] += dkp[128:S, :]
        dvs[0:S - 128, :] += dvp[128:S, :]
        _from_residues_dyn(g, tmp, dqs)
        restage(0, (_rope(tmp[...], cc_v, -ss_v, first_half) * ATT_SCALE).astype(BF16))
        _from_residues_dyn(g, tmp, dks)
        restage(1, _rope(tmp[...], cc_v, -ss_v, first_half).astype(BF16))
        _from_residues_dyn(g, tmp, dvs)
        restage(2, tmp[...].astype(BF16))

        @pl.when(pair * 3 + g == 11)
        def _():
            for j in range(3):
                out_copy(j).wait()

    any_spec = pl.BlockSpec(memory_space=pl.ANY)
    buf = pltpu.VMEM((S, LANES), F32)
    padded_b = pltpu.VMEM((ATT_PAD + S, LANES), BF16)
    return pl.pallas_call(
        body, name="attn_bwd", grid=(4, 3),
        out_shape=jax.ShapeDtypeStruct((S, IN_COLS), BF16),
        in_specs=_attn_in_specs(3) + [any_spec], out_specs=any_spec,
        scratch_shapes=[buf, buf, padded_b, padded_b] + [buf] * 15
        + [pltpu.VMEM((4, S, LANES), BF16), pltpu.SemaphoreType.DMA((4,))],
        input_output_aliases={9: 0},
        compiler_params=_cp(("arbitrary", "arbitrary")),
    )(z, z, z, z, cc, ss, ob, lse, do_bg, dz)


def _in_proj_bwd(dz, h, w_in):
    half = S // 2
    slab = (D, SHARD_COLS)

    def body(dz_hbm, h_hbm, w_hbm, dh_hbm, g_chip, r1_hbm, relay_hbm, r2_hbm,
             h_buf, dz_buf, stage_d, r1_buf, stage_i, acc,
             dz_sem, w_sem, h_sem, r1_sem, out_sem, send_d, recv_d, send_i, recv_i):
        x, y, c = _mesh_pos()
        sibling = (x, y, 1 - c)
        north = c == 1
        near = (jnp.where(north, 1 - x, x), jnp.where(north, y, 1 - y))
        far = (jnp.where(north, x, 1 - x), jnp.where(north, 1 - y, y))
        chips = [(1 - x, 1 - y), near, far, (x, y)]

        def cols(d):
            return pl.ds(pl.multiple_of(d * SHARD_COLS, LANES), SHARD_COLS)

        blocks = []
        for q_sib, q in zip([chips[0], far, near, chips[3]], chips):
            blocks += [4 * q_sib[0] + 2 * q_sib[1] + (1 - c), 4 * q[0] + 2 * q[1] + c]

        def dz_tile(t):
            return _SplitCopy(dz_hbm.at[pl.ds((t % 2) * half, half), cols(blocks[t // 2])],
                                         dz_buf.at[t % 2], dz_sem.at[t % 2])

        def to_sibling(i):
            return pltpu.make_async_remote_copy(
                src_ref=stage_d.at[i % 2], dst_ref=r1_hbm.at[i], send_sem=send_d.at[i], recv_sem=recv_d.at[i],
                device_id=sibling, device_id_type=MESH)

        def to_owner(i):
            dst = relay_hbm if i == 0 else r2_hbm.at[i - 1]
            return pltpu.make_async_remote_copy(
                src_ref=stage_i.at[i], dst_ref=dst, send_sem=send_i.at[i], recv_sem=recv_i.at[i],
                device_id=(*(far if i == 2 else near), c), device_id_type=MESH)

        h_copy = _SplitCopy(h_hbm, h_buf, h_sem)
        h_copy.start()
        dz_tile(0).start()
        h_copy.wait()
        for b in range(8):
            i = b // 2
            g = None
            for r in range(2):
                t = 2 * b + r
                if t + 1 < 16:
                    dz_tile(t + 1).start()
                dz_tile(t).wait()
                part = _dot_tn(h_buf[r * half:(r + 1) * half, :], dz_buf[t % 2])
                g = part if g is None else g + part
                if b % 2 == 1 and r == 0:
                    to_sibling(i).wait_recv()
                    r1_copy = _SplitCopy(r1_hbm.at[i], r1_buf, r1_sem)
                    r1_copy.start()
            if b % 2 == 0:
                if i >= 2:
                    to_sibling(i - 2).wait_send()
                stage_d[i % 2] = g.astype(BF16)
                to_sibling(i).start()
            else:
                r1_copy.wait()
                g = g + r1_buf[...].astype(F32)
                if i == 2:
                    to_owner(0).wait_recv()
                    relay_copy = _SplitCopy(relay_hbm, r1_buf, r1_sem)
                    relay_copy.start()
                    relay_copy.wait()
                    g = g + r1_buf[...].astype(F32)
                if i < 3:
                    stage_i[i] = g.astype(BF16)
                    to_owner(i).start()
                else:
                    g_chip[...] = g
        to_sibling(2).wait_send()
        to_sibling(3).wait_send()

        def dz2(t):
            return _SplitCopy(
                dz_hbm.at[pl.ds((t % 2) * half, half), pl.ds((t // 2) * SHARD_COLS, SHARD_COLS)],
                dz_buf.at[t % 2], dz_sem.at[t % 2])

        def w2(b):
            return _SplitCopy(w_hbm.at[:, pl.ds(b * SHARD_COLS, SHARD_COLS)],
                                         stage_d.at[b % 2], w_sem.at[b % 2])

        dz2(0).start()
        w2(0).start()
        for t in range(16):
            b, r = t // 2, t % 2
            if t + 1 < 16:
                dz2(t + 1).start()
            if r == 0:
                if b + 1 < 8:
                    w2(b + 1).start()
                w2(b).wait()
            dz2(t).wait()
            part = _dot_nt(dz_buf[t % 2], stage_d[b % 2])
            if b == 0:
                acc[r] = part
            else:
                acc[r] += part
        dh_out = [_SplitCopy(acc.at[r], dh_hbm.at[pl.ds(r * half, half), :], out_sem.at[r])
                  for r in range(2)]
        for cp in dh_out:
            cp.start()
        for cp in dh_out:
            cp.wait()
        for i in range(3):
            to_owner(i).wait_send()
        for i in (1, 2):
            to_owner(i).wait_recv()

    any_spec = pl.BlockSpec(memory_space=pl.ANY)
    return pl.pallas_call(
        body, name="in_proj_bwd",
        out_shape=[jax.ShapeDtypeStruct((S, D), F32), jax.ShapeDtypeStruct(slab, F32),
                   jax.ShapeDtypeStruct((4,) + slab, BF16), jax.ShapeDtypeStruct(slab, BF16),
                   jax.ShapeDtypeStruct((2,) + slab, BF16)],
        in_specs=[any_spec] * 3,
        out_specs=[any_spec, pl.BlockSpec(memory_space=pltpu.VMEM), any_spec, any_spec, any_spec],
        scratch_shapes=[pltpu.VMEM((S, D), BF16), pltpu.VMEM((2, half, SHARD_COLS), BF16),
                        pltpu.VMEM((2,) + slab, BF16), pltpu.VMEM(slab, BF16), pltpu.VMEM((3,) + slab, BF16),
                        pltpu.VMEM((2, half, D), F32),
                        pltpu.SemaphoreType.DMA((2,)), pltpu.SemaphoreType.DMA((2,)), pltpu.SemaphoreType.DMA,
                        pltpu.SemaphoreType.DMA, pltpu.SemaphoreType.DMA((2,)),
                        pltpu.SemaphoreType.DMA((4,)), pltpu.SemaphoreType.DMA((4,)),
                        pltpu.SemaphoreType.DMA((3,)), pltpu.SemaphoreType.DMA((3,))],
        compiler_params=_cp(),
    )(dz, h, w_in)


def _grad_x(x, norm_w, dh, dx2):
    tr = 256

    def body(x_ref, w_ref, dh_ref, dx2_ref, gx_ref, gnw_ref):
        @pl.when(pl.program_id(0) == 0)
        def _():
            gnw_ref[...] = jnp.zeros_like(gnw_ref)

        xv, dhv = x_ref[...], dh_ref[...]
        r = lax.rsqrt(jnp.mean(xv * xv, axis=-1, keepdims=True) + EPS)
        n = xv * r
        gnw_ref[...] += jnp.sum(dhv * n, axis=0, keepdims=True)
        dn = dhv * w_ref[...]
        gx_ref[...] = dx2_ref[...] + r * (dn - n * jnp.mean(dn * n, axis=-1, keepdims=True))

    row = pl.BlockSpec((tr, D), lambda i: (i, 0))
    vec = pl.BlockSpec((1, D), lambda i: (0, 0))
    return pl.pallas_call(
        body, name="grad_x", grid=(S // tr,),
        out_shape=[jax.ShapeDtypeStruct((S, D), F32), jax.ShapeDtypeStruct((1, D), F32)],
        in_specs=[row, vec, row, row], out_specs=[row, vec],
        compiler_params=_cp(("arbitrary",)),
    )(x, norm_w, dh, dx2)


def _rope_tables(positions):
    inv_freq = 10000.0 ** (-jnp.arange(0, 64, 2, dtype=F32) / 64)
    ang = positions.astype(F32)[:, None] * inv_freq[None, :]
    cos, sin = jnp.cos(ang), jnp.sin(ang)
    return jnp.tile(cos, (1, 4)), jnp.tile(jnp.concatenate([-sin, sin], axis=1), (1, 2))


def _local_step(x, positions, norm_w, lb_logits, hnw, fnw, target, w_in_shard, small_shards, core):
    cc, ss = _rope_tables(positions)
    lbv = jax.nn.sigmoid(lb_logits[0:1] - lb_logits[1:2])
    h = _rmsnorm_in(x, norm_w)
    z, w_in = _in_proj_gather(h, w_in_shard)
    o, o_a, states, w_a, w_b, w_out = _hgrn_fwd(z, lbv, hnw, small_shards)
    ob, lse, o_bg = _attn_fwd(z, cc, ss)
    dx2, dx2b, dz, do_a, do_bg, merged, dy_a, dy_b, tail_small = _tail(x, o_a, o_bg, z, target, w_a, w_b, w_out, fnw)
    g_out, gb_out = _tn_matmul(merged, dx2b, "grad_w_out")
    g_a, gb_a = _tn_matmul(o_a, dy_a, "grad_w_a")
    g_b, gb_b = _tn_matmul(o_bg, dy_b, "grad_w_b")
    grads, gb = (g_a, g_b, g_out), (gb_a, gb_b, gb_out)
    r1 = _exchange_sibling(GATHER_IDS, gb)
    pb = [_chip_partials(a, grads[i], r1[i], core) for i, a in enumerate(GATHER_IDS)]
    dz, glb, ghn, *r2 = _hgrn_bwd(z, o, do_a, states, lbv, hnw, pb, dz)
    dz = _attn_bwd(z, cc, ss, ob, lse, do_bg, dz)
    dh, g_chip_in, _, _, r2_in = _in_proj_bwd(dz, h, w_in)
    grad_x, gnw = _grad_x(x, norm_w, dh, dx2)
    ghn_row = jnp.pad(jnp.sum(ghn, axis=0), ((0, 0), (0, D - 128)))
    small = jnp.concatenate([gnw, glb, ghn_row, tail_small[0:2], jnp.zeros((3, D), F32)], axis=0)
    return grad_x, (g_chip_in, r2_in), grads, r1, r2, small


def kernel(x, positions, norm_w, w_in, lb_logits, hgrn_norm_w, w_branch_a, w_branch_b, w_out, final_norm_w, loss_target, m_norm_w, m_w_in, m_lb_logits, m_hgrn_norm_w, m_w_branch_a, m_w_branch_b, m_w_out, m_final_norm_w, v_norm_w, v_w_in, v_lb_logits, v_hgrn_norm_w, v_w_branch_a, v_w_branch_b, v_w_out, v_final_norm_w):
    ix, iy, ic = _mesh_pos()
    core = jnp.reshape(ic, (1,)).astype(jnp.int32)
    pos = jnp.stack([4 * ix + 2 * iy + ic, 2 * ix + iy]).astype(jnp.int32)

    shards = [w_in[0], w_branch_a[0], w_branch_b[0], w_out[0]]
    moments_m = [m_w_in[0], m_w_branch_a[0], m_w_branch_b[0], m_w_out[0]]
    moments_v = [v_w_in[0], v_w_branch_a[0], v_w_branch_b[0], v_w_out[0]]
    names = ("w_in", "w_a", "w_b", "w_out")
    ids = GATHER_IDS
    shards_b = [_cast_bf16(w, f"cast_{nm}") for w, nm in zip(shards, names)]

    fnw2 = final_norm_w.reshape(1, D)
    grad_x, (g_chip_in, r2_in), grads, r1, r2, small = _local_step(
        x[0], positions[0], norm_w, lb_logits, hgrn_norm_w, fnw2, loss_target[0], shards_b[0], shards_b[1:], core)

    gathered = _gather_small(small)
    big =[_reduce_own_and_update(shards[0], moments_m[0], moments_v[0], g_chip_in, r2_in)]
    big += [_reduce_and_update(a, shards[a], moments_m[a], moments_v[a], grads[i], r1[i], r2[i], pos)
            for i, a in enumerate(ids)]
    sm = _small_update(gathered, norm_w, lb_logits, hgrn_norm_w, fnw2,
                       (m_norm_w, m_lb_logits, m_hgrn_norm_w, m_final_norm_w.reshape(1, D),
                        v_norm_w, v_lb_logits, v_hgrn_norm_w, v_final_norm_w.reshape(1, D)))
    loss = sm[0][0, 0]
    outs = [loss, grad_x[None]]
    for kind in range(4):
        s_nw, s_lb, s_hn, s_fn = sm[1 + 4 * kind:5 + 4 * kind]
        outs += [s_nw, big[0][kind][None], s_lb, s_hn, big[1][kind][None], big[2][kind][None],
                 big[3][kind][None], s_fn.reshape(D)]
    return tuple(outs)
```

```python
import functools

import jax
import jax.numpy as jnp
from jax import lax
from jax.experimental import pallas as pl
from jax.experimental.pallas import tpu as pltpu

F32 = jnp.float32
BF16 = jnp.bfloat16
MESH = pl.DeviceIdType.MESH

S = 2048
D = 1024
NDEV = 8
HEADS = 8
CHUNK = 64
SUB = 16
HBLK = 256
ATT_PAD = 128
ATT_UNROLL = 16
COPY_PARTS = 4
EXP_CLAMP = 80.0
EPS = 1e-6
IN_COLS = 11264
SHARD_COLS = IN_COLS // NDEV
DZ_ATT = 4096
DZ_GATES = 9216
ATT_DILS = (1, 4, 16)
ATT_SCALE = 64 ** -0.5
LANES = 128

ADAM_LR, ADAM_B1, ADAM_B2, ADAM_EPS, ADAM_WD, ADAM_STEP = 0.001, 0.9, 0.999, 1e-08, 0.01, 10

VMEM_LIMIT = 56 * 1024 * 1024


def _cp(sem=None, **kw):
    return pltpu.CompilerParams(dimension_semantics=sem, vmem_limit_bytes=VMEM_LIMIT, **kw)


def _dot(a, b):
    return jnp.dot(a, b, preferred_element_type=F32)


def _dot_nt(a, b):
    return lax.dot_general(a, b, (((1,), (1,)), ((), ())), preferred_element_type=F32)


def _dot_tn(a, b):
    return lax.dot_general(a, b, (((0,), (0,)), ((), ())), preferred_element_type=F32)


def _split2(x):
    hi = x.astype(BF16)
    lo = (x - hi.astype(F32)).astype(BF16)
    return hi, lo


def _split3(x):
    hi = x.astype(BF16)
    r = x - hi.astype(F32)
    mid = r.astype(BF16)
    lo = (r - mid.astype(F32)).astype(BF16)
    return hi, mid, lo


def _dot_ones(ones_bf16, x):
    hi, mid, lo = _split3(x)
    return _dot(ones_bf16, hi) + _dot(ones_bf16, mid) + _dot(ones_bf16, lo)


def _silu(x):
    return x * jax.nn.sigmoid(x)


def _dsilu(x):
    s = jax.nn.sigmoid(x)
    return s * (1.0 + x * (1.0 - s))


def _mesh_pos():
    return lax.axis_index("x"), lax.axis_index("y"), lax.axis_index("c")


class _SplitCopy:
    def __init__(self, src, dst, sem):
        self.src, self.dst, self.sem = src, dst, sem

    def start(self):
        rows = self.src.shape[0] // COPY_PARTS
        for p in range(COPY_PARTS):
            chunk = pl.ds(p * rows, rows)
            pltpu.make_async_copy(self.src.at[chunk], self.dst.at[chunk], self.sem).start()

    def wait(self):
        pltpu.make_async_copy(self.src, self.dst, self.sem).wait()


def _shard_of(ref, a, d):
    if a == 0:
        return ref.at[:, pl.ds(pl.multiple_of(d * SHARD_COLS, LANES), SHARD_COLS)]
    if a == 2:
        return ref.at[:, pl.ds(pl.multiple_of(d * LANES, LANES), LANES)]
    return ref.at[pl.ds(pl.multiple_of(d * 128, 128), 128), :]


FULL_SHAPES = ((D, IN_COLS), (D, D), (512, D), (D, D))
SHARD_SHAPES = ((D, SHARD_COLS), (128, D), (512, 128), (128, D))


def _allgather_steps(ids, ins, outs, send_sems, recv_sems, local_sems):
    n = len(ids)
    x, y, c = _mesh_pos()
    me, sibling = (x, y, c), (x, y, 1 - c)
    chips = [(1 - x, y), (x, 1 - y), (1 - x, 1 - y)]

    def blk(a, p):
        return _shard_of(outs[a], ids[a], 4 * p[0] + 2 * p[1] + p[2])

    def copy(a, k, block, to, src=None):
        return pltpu.make_async_remote_copy(
            src_ref=blk(a, block) if src is None else src, dst_ref=blk(a, block),
            send_sem=send_sems.at[a * 7 + k], recv_sem=recv_sems.at[a * 7 + k],
            device_id=to, device_id_type=MESH)

    mine = [pltpu.make_async_copy(ins[a], blk(a, me), local_sems.at[a]) for a in range(n)]
    first = []
    for a in range(n):
        first += [copy(a, 1 + j, me, (*chip, c), src=ins[a]) for j, chip in enumerate(chips)]
    for a in range(n):
        first.append(copy(a, 0, me, sibling, src=ins[a]))
    passed = [copy(a, 4 + j, (*chip, c), sibling) for j, chip in enumerate(chips) for a in range(n)]

    def start():
        for cp in mine + first:
            cp.start()

    def middle():
        for j, chip in enumerate(chips):
            for a in range(n):
                copy(a, 1 + j, (*chip, c), me).wait_recv()
                passed[j * n + a].start()

    def end():
        for a in range(n):
            copy(a, 0, sibling, me).wait_recv()
        for j, chip in enumerate(chips):
            for a in range(n):
                copy(a, 4 + j, (*chip, 1 - c), me).wait_recv()
        for cp in first + passed:
            cp.wait_send()
        for cp in mine:
            cp.wait()

    return start, middle, end


def _in_proj_gather(x, norm_w, w_shard):
    half = S // 2
    slab = (D, SHARD_COLS)
    xt = 512

    def body(x_hbm, nw_ref, w_hbm, z_hbm, wfull_hbm, h_hbm, h_buf, land, zstage, xbuf,
             h_sem, own_sem, z_sem, wout_sem, x_sem, send_sems, recv_sems):
        x, y, c = _mesh_pos()
        sibling = (x, y, 1 - c)
        north = c == 1

        def chips_of(first_x):
            near = (jnp.where(first_x, 1 - x, x), jnp.where(first_x, y, 1 - y))
            far = (jnp.where(first_x, x, 1 - x), jnp.where(first_x, 1 - y, y))
            return [near, far, (1 - x, 1 - y)]

        mine, theirs = chips_of(north), chips_of(jnp.logical_not(north))

        def dev(chip, core):
            return 4 * chip[0] + 2 * chip[1] + core

        block_of = ([dev((x, y), c), dev((x, y), 1 - c)] + [dev(q, c) for q in mine]
                    + [dev(q, 1 - c) for q in theirs])

        def cols(d):
            if isinstance(d, int):
                return pl.ds(d * SHARD_COLS, SHARD_COLS)
            return pl.ds(pl.multiple_of(d * SHARD_COLS, LANES), SHARD_COLS)

        def send(k, src, dst_slot, to):
            return pltpu.make_async_remote_copy(
                src_ref=src, dst_ref=land.at[dst_slot], send_sem=send_sems.at[k], recv_sem=recv_sems.at[k],
                device_id=to, device_id_type=MESH)

        def to_sibling():
            return send(0, w_hbm, 1, sibling)

        def to_chip(j):
            if j == 2:
                return send(3, land.at[2], 4, (*mine[1], c))
            return send(1 + j, w_hbm, 2 + j, (*mine[j], c))

        def pass_on(j):
            return send(4 + j, land.at[2 + j], 5 + j, sibling)

        def x_tile(i):
            return _SplitCopy(x_hbm.at[pl.ds(i * xt, xt), :], xbuf.at[i % 2], x_sem.at[i % 2])

        own = _SplitCopy(w_hbm, land.at[0], own_sem)
        own.start()
        x_tile(0).start()
        to_sibling().start()
        to_chip(0).start()
        for i in range(S // xt):
            if i + 1 < S // xt:
                x_tile(i + 1).start()
            x_tile(i).wait()
            xv = xbuf[i % 2]
            r = lax.rsqrt(jnp.mean(xv * xv, axis=-1, keepdims=True) + EPS)
            h_buf[i * xt:(i + 1) * xt, :] = (xv * r * nw_ref[...]).astype(BF16)
        h_out = _SplitCopy(h_buf, h_hbm, h_sem)
        h_out.start()
        own.wait()

        def multiply(slot, n_done):
            d = block_of[slot]
            out = _SplitCopy(land.at[slot], wfull_hbm.at[:, cols(d)], wout_sem.at[slot])
            out.start()
            for r in range(2):
                rows = pl.ds(r * half, half)
                zc = _SplitCopy(zstage.at[r], z_hbm.at[rows, cols(d)], z_sem.at[r])
                if n_done > 0:
                    zc.wait()
                zstage[r] = _dot(h_buf[r * half:(r + 1) * half, :], land[slot])
                zc.start()
            return out

        outs = [multiply(0, 0)]
        to_sibling().wait_recv()
        outs.append(multiply(1, 1))
        done = 2
        for j in range(3):
            to_chip(j).wait_recv()
            pass_on(j).start()
            to_chip(j).wait_send()
            if j < 2:
                to_chip(j + 1).start()
            outs.append(multiply(2 + j, done))
            pass_on(j).wait_recv()
            outs.append(multiply(5 + j, done + 1))
            done += 2
        for r in range(2):
            _SplitCopy(zstage.at[r], z_hbm.at[pl.ds(r * half, half), cols(0)], z_sem.at[r]).wait()
        for out in outs:
            out.wait()
        h_out.wait()
        to_sibling().wait_send()
        for j in range(3):
            pass_on(j).wait_send()

    any_spec = pl.BlockSpec(memory_space=pl.ANY)
    return pl.pallas_call(
        body, name="in_proj_gather",
        out_shape=[jax.ShapeDtypeStruct((S, IN_COLS), F32), jax.ShapeDtypeStruct((D, IN_COLS), BF16),
                   jax.ShapeDtypeStruct((S, D), BF16)],
        in_specs=[any_spec, pl.BlockSpec(memory_space=pltpu.VMEM), any_spec], out_specs=[any_spec] * 3,
        scratch_shapes=[pltpu.VMEM((S, D), BF16), pltpu.VMEM((8,) + slab, BF16), pltpu.VMEM((2, half, SHARD_COLS), F32),
                        pltpu.VMEM((2, xt, D), F32),
                        pltpu.SemaphoreType.DMA, pltpu.SemaphoreType.DMA, pltpu.SemaphoreType.DMA((2,)),
                        pltpu.SemaphoreType.DMA((8,)), pltpu.SemaphoreType.DMA((2,)),
                        pltpu.SemaphoreType.DMA((7,)), pltpu.SemaphoreType.DMA((7,))],
        compiler_params=_cp(),
    )(x, norm_w, w_shard)


def _exchange_sibling(ids, gb):
    n = len(gb)

    def body(*refs):
        ins, outs = refs[:n], refs[n:2 * n]
        send_sems, recv_sems = refs[2 * n:]
        x, y, c = _mesh_pos()
        sibling = (x, y, 1 - c)
        copies = []
        for i, a in enumerate(ids):
            for q in range(4):
                copies.append(pltpu.make_async_remote_copy(
                    src_ref=_shard_of(ins[i], a, 2 * q + (1 - c)), dst_ref=outs[i].at[q],
                    send_sem=send_sems.at[i * 4 + q], recv_sem=recv_sems.at[i * 4 + q],
                    device_id=sibling, device_id_type=MESH))
        for cp in copies:
            cp.start()
        for cp in copies:
            cp.wait()

    any_spec = pl.BlockSpec(memory_space=pl.ANY)
    return pl.pallas_call(
        body, name="grads_to_sibling",
        out_shape=[jax.ShapeDtypeStruct((4,) + SHARD_SHAPES[a], BF16) for a in ids],
        in_specs=[any_spec] * n, out_specs=[any_spec] * n,
        scratch_shapes=[pltpu.SemaphoreType.DMA((4 * n,)), pltpu.SemaphoreType.DMA((4 * n,))],
    )(*gb)


def _exchange_chips_steps(ins, outs, send_sems, recv_sems):
    x, y, c = _mesh_pos()
    chips = [(1 - x, y), (x, 1 - y), (1 - x, 1 - y)]
    copies = []
    for a in range(len(ins)):
        for k, chip in enumerate(chips):
            copies.append(pltpu.make_async_remote_copy(
                src_ref=ins[a].at[2 * chip[0] + chip[1]], dst_ref=outs[a].at[k],
                send_sem=send_sems.at[a * 3 + k], recv_sem=recv_sems.at[a * 3 + k],
                device_id=(*chip, c), device_id_type=MESH))

    def start():
        for cp in copies:
            cp.start()

    def end():
        for cp in copies:
            cp.wait()

    return start, end


def _gather_small_steps(small_ref, small_out, ssend, srecv, local_sem):
    x, y, c = _mesh_pos()
    me = 4 * x + 2 * y + c
    copies = []
    for r in range(1, NDEV):
        peer = (1 - x if r & 4 else x, 1 - y if r & 2 else y, 1 - c if r & 1 else c)
        copies.append(pltpu.make_async_remote_copy(
            src_ref=small_ref, dst_ref=small_out.at[me],
            send_sem=ssend.at[r - 1], recv_sem=srecv.at[r - 1],
            device_id=peer, device_id_type=MESH))
    own = pltpu.make_async_copy(small_ref, small_out.at[me], local_sem)

    def start():
        own.start()
        for cp in copies:
            cp.start()

    def end():
        for cp in copies:
            cp.wait()
        own.wait()

    return start, end


def _gather_small(small):
    def body(small_ref, small_out, ssend, srecv, local_sem):
        start, end = _gather_small_steps(small_ref, small_out, ssend, srecv, local_sem)
        start()
        end()

    any_spec = pl.BlockSpec(memory_space=pl.ANY)
    return pl.pallas_call(
        body, name="gather_small",
        out_shape=jax.ShapeDtypeStruct((NDEV,) + small.shape, F32),
        in_specs=[any_spec], out_specs=any_spec,
        scratch_shapes=[pltpu.SemaphoreType.DMA((NDEV - 1,)), pltpu.SemaphoreType.DMA((NDEV - 1,)),
                        pltpu.SemaphoreType.DMA],
    )(small)


def _shard_tiles(a):
    rows, cols = SHARD_SHAPES[a]
    tr = min(rows, 256)
    return (tr, cols), rows // tr


def _full_index(a, d, i):
    (tr, _), nt = _shard_tiles(a)
    if a in (0, 2):
        return (i, d)
    return (d * nt + i, 0)


def _cast_bf16(x, name):
    rows, cols = x.shape
    tr = min(rows, 256)

    def body(x_ref, o_ref):
        o_ref[...] = x_ref[...].astype(BF16)

    return pl.pallas_call(
        body, name=name, out_shape=jax.ShapeDtypeStruct(x.shape, BF16), grid=(rows // tr,),
        in_specs=[pl.BlockSpec((tr, cols), lambda i: (i, 0))],
        out_specs=pl.BlockSpec((tr, cols), lambda i: (i, 0)),
        compiler_params=_cp(("parallel",)),
    )(x)


def _chip_partials(a, g_full, r1, core):
    tile, nt = _shard_tiles(a)

    def body(c_ref, g_ref, r_ref, o_ref):
        o_ref[0] = (g_ref[...] + r_ref[0].astype(F32)).astype(BF16)

    grid_spec = pltpu.PrefetchScalarGridSpec(
        num_scalar_prefetch=1, grid=(4, nt),
        in_specs=[pl.BlockSpec(tile, lambda q, i, c: _full_index(a, 2 * q + c[0], i)),
                  pl.BlockSpec((1,) + tile, lambda q, i, c: (q, i, 0))],
        out_specs=pl.BlockSpec((1,) + tile, lambda q, i, c: (q, i, 0)))
    return pl.pallas_call(
        body, name=f"chip_partials_{a}", grid_spec=grid_spec,
        out_shape=jax.ShapeDtypeStruct((4,) + SHARD_SHAPES[a], BF16),
        compiler_params=_cp(("parallel", "parallel")),
    )(core, g_full, r1)


def _adam(w, g, m, v):
    m = ADAM_B1 * m + (1.0 - ADAM_B1) * g
    v = ADAM_B2 * v + (1.0 - ADAM_B2) * (g * g)
    m_hat = m / (1.0 - ADAM_B1 ** ADAM_STEP)
    v_hat = v / (1.0 - ADAM_B2 ** ADAM_STEP)
    delta = -ADAM_LR * (m_hat / (jnp.sqrt(v_hat) + ADAM_EPS) + ADAM_WD * w)
    return delta, m, v


def _reduce_and_update(a, w, m, v, g_full, r1, r2, pos):
    tile, nt = _shard_tiles(a)

    def body(p_ref, w_ref, m_ref, v_ref, g_ref, r1_ref, r2_ref, go_ref, do_ref, mo_ref, vo_ref):
        g = g_ref[...] + r1_ref[0].astype(F32)
        g = g + r2_ref[0].astype(F32)
        g = g + r2_ref[1].astype(F32)
        g = g + r2_ref[2].astype(F32)
        delta, m_new, v_new = _adam(w_ref[...], g, m_ref[...], v_ref[...])
        go_ref[...] = g
        do_ref[...] = delta
        mo_ref[...] = m_new
        vo_ref[...] = v_new

    own = pl.BlockSpec(tile, lambda i, p: (i, 0))
    grid_spec = pltpu.PrefetchScalarGridSpec(
        num_scalar_prefetch=1, grid=(nt,),
        in_specs=[own, own, own,
                  pl.BlockSpec(tile, lambda i, p: _full_index(a, p[0], i)),
                  pl.BlockSpec((1,) + tile, lambda i, p: (p[1], i, 0)),
                  pl.BlockSpec((3,) + tile, lambda i, p: (0, i, 0))],
        out_specs=[own] * 4)
    shp = jax.ShapeDtypeStruct(w.shape, F32)
    return pl.pallas_call(
        body, name=f"reduce_update_{a}", grid_spec=grid_spec, out_shape=[shp] * 4,
        compiler_params=_cp(("parallel",)),
    )(pos, w, m, v, g_full, r1, r2)


def _reduce_own_and_update(w, m, v, g_chip, r2):
    tile, nt = _shard_tiles(0)

    def body(w_ref, m_ref, v_ref, g_ref, r2_ref, go_ref, do_ref, mo_ref, vo_ref):
        g = g_ref[...] + r2_ref[0].astype(F32)
        g = g + r2_ref[1].astype(F32)
        delta, m_new, v_new = _adam(w_ref[...], g, m_ref[...], v_ref[...])
        go_ref[...] = g
        do_ref[...] = delta
        mo_ref[...] = m_new
        vo_ref[...] = v_new

    own = pl.BlockSpec(tile, lambda i: (i, 0))
    shp = jax.ShapeDtypeStruct(w.shape, F32)
    return pl.pallas_call(
        body, name="reduce_update_0", grid=(nt,), out_shape=[shp] * 4,
        in_specs=[own, own, own, own, pl.BlockSpec((2,) + tile, lambda i: (0, i, 0))], out_specs=[own] * 4,
        compiler_params=_cp(("parallel",)),
    )(w, m, v, g_chip, r2)


def _small_update(gathered, norm_w, lb_logits, hnw, fnw, moments):
    m_nw, m_lb, m_hn, m_fn, v_nw, v_lb, v_hn, v_fn = moments

    def body(g_ref, nw, lb, hn, fn, mnw, mlb, mhn, mfn, vnw, vlb, vhn, vfn,
             loss_o, g_nw, g_lb, g_hn, g_fn, d_nw, d_lb, d_hn, d_fn,
             mo_nw, mo_lb, mo_hn, mo_fn, vo_nw, vo_lb, vo_hn, vo_fn):
        tot = g_ref[0]
        for d in range(1, NDEV):
            tot = tot + g_ref[d]
        loss_o[...] = tot[4:5, 0:LANES]
        logits = lb[...]
        lbv = jax.nn.sigmoid(logits[0:1] - logits[1:2])
        chain = tot[1:2] * lbv * (1.0 - lbv)
        grads = (tot[0:1], jnp.concatenate([chain, -chain], axis=0), tot[2:3, 0:LANES], tot[3:4])
        outs = ((nw, mnw, vnw, g_nw, d_nw, mo_nw, vo_nw), (lb, mlb, vlb, g_lb, d_lb, mo_lb, vo_lb),
                (hn, mhn, vhn, g_hn, d_hn, mo_hn, vo_hn), (fn, mfn, vfn, g_fn, d_fn, mo_fn, vo_fn))
        for g, (w_r, m_r, v_r, g_o, d_o, m_o, v_o) in zip(grads, outs):
            delta, m_new, v_new = _adam(w_r[...], g, m_r[...], v_r[...])
            g_o[...] = g
            d_o[...] = delta
            m_o[...] = m_new
            v_o[...] = v_new

    shapes = [norm_w.shape, lb_logits.shape, hnw.shape, fnw.shape]
    out_shape = [jax.ShapeDtypeStruct((1, LANES), F32)] + [jax.ShapeDtypeStruct(s, F32) for s in shapes] * 4
    return pl.pallas_call(body, name="small_update", out_shape=out_shape, compiler_params=_cp())(
        gathered, norm_w, lb_logits, hnw, fnw, m_nw, m_lb, m_hn, m_fn, v_nw, v_lb, v_hn, v_fn)


def _block_tri(n, block, upper=False):
    r = lax.broadcasted_iota(jnp.int32, (n, n), 0)
    c = lax.broadcasted_iota(jnp.int32, (n, n), 1)
    keep = (c >= r) if upper else (c <= r)
    return jnp.where(keep & ((r // block) == (c // block)), 1.0, 0.0).astype(BF16)


def _tril_mask(n):
    r = lax.broadcasted_iota(jnp.int32, (n, n), 0)
    c = lax.broadcasted_iota(jnp.int32, (n, n), 1)
    return c <= r


def _chunk_scores(q, k, b, bex, r0, mask):
    parts, qs_l, ks_l, ek_l, eq_l = [], [], [], [], []
    for i in range(CHUNK // SUB):
        ri = slice(r0 + SUB * i, r0 + SUB * (i + 1))
        seen = slice(r0, r0 + SUB * (i + 1))
        base = bex[r0 + SUB * i:r0 + SUB * i + 1]
        eq = jnp.exp(b[ri] - base)
        ek = jnp.exp(jnp.minimum(base - b[seen], EXP_CLAMP))
        ks = k[seen] * ek
        if i + 1 < CHUNK // SUB:
            rest = jnp.zeros((CHUNK - SUB * (i + 1), 128), F32)
            ek, ks = jnp.concatenate([ek, rest], axis=0), jnp.concatenate([ks, rest], axis=0)
        qs = q[ri] * eq
        parts.append(_dot_nt(qs.astype(BF16), ks.astype(BF16)))
        qs_l.append(qs)
        ks_l.append(ks)
        ek_l.append(ek)
        eq_l.append(eq)
    return jnp.where(mask, jnp.concatenate(parts, axis=0), 0.0), qs_l, ks_l, ek_l, eq_l


def _hgrn_cols(hq, hf, hi, lb):
    sg = jax.nn.sigmoid(hf)
    f = lb + (1.0 - lb) * sg
    g = jnp.log(f)
    b = _dot_ones(_block_tri(HBLK, CHUNK), g)
    return _silu(hq), 1.0 - f, g, hi, sg, f, b


GATHER_IDS = (1, 2, 3)


def _hgrn_fwd(z, lbv, hnw, shards):
    ntb, nch = S // HBLK, HBLK // CHUNK
    n = len(GATHER_IDS)

    def body(hq_ref, hf_ref, hi_ref, hg_ref, lb_ref, hnw_ref, s0, s1, s2, o_ref, oa_ref, st_ref, f0, f1, f2,
             state, send_sems, recv_sems, local_sems):
        start, middle, end = _allgather_steps(GATHER_IDS, (s0, s1, s2), (f0, f1, f2), send_sems, recv_sems, local_sems)

        @pl.when(pl.program_id(0) == 0)
        def _():
            state[...] = jnp.zeros_like(state)
            start()

        pl.when(pl.program_id(0) == ntb // 2)(middle)

        q_a, k_a, g_a, v_a, _, _, b_a = _hgrn_cols(hq_ref[...], hf_ref[...], hi_ref[...], lb_ref[...])
        bex_a = b_a - g_a
        eb_a = jnp.exp(b_a)
        mask = _tril_mask(CHUNK)
        hg = hg_ref[...]
        w = hnw_ref[...]
        for h in range(HEADS):
            cols = slice(128 * h, 128 * h + 128)
            q, k, v, b, bex, eb = q_a[:, cols], k_a[:, cols], v_a[:, cols], b_a[:, cols], bex_a[:, cols], eb_a[:, cols]
            st = state[h]
            outs = []
            for c in range(nch):
                r0 = c * CHUNK
                rows = slice(r0, r0 + CHUNK)
                a = _chunk_scores(q, k, b, bex, r0, mask)[0]
                vb = v[rows].astype(BF16)
                b_last = b[r0 + CHUNK - 1:r0 + CHUNK]
                qe = (q[rows] * eb[rows]).astype(BF16)
                outs.append(_dot(a.astype(BF16), vb) + _dot_nt(qe, st.astype(BF16)))
                st_ref[h, c] = st
                ke = (k[rows] * jnp.exp(b_last - b[rows])).astype(BF16)
                st = st * jnp.exp(b_last) + _dot_tn(vb, ke)
            state[h] = st
            o = jnp.concatenate(outs, axis=0)
            o_ref[:, cols] = o
            r = lax.rsqrt(jnp.mean(o * o, axis=-1, keepdims=True) + EPS)
            oa_ref[:, cols] = (o * r * w * _silu(hg[:, cols])).astype(BF16)

        pl.when(pl.program_id(0) == ntb - 1)(end)

    def zcol(j):
        return pl.BlockSpec((HBLK, D), lambda t: (t, j))

    out_blk = pl.BlockSpec((HBLK, D), lambda t: (t, 0))
    any_spec = pl.BlockSpec(memory_space=pl.ANY)
    return pl.pallas_call(
        body, name="hgrn_fwd", grid=(ntb,),
        out_shape=[jax.ShapeDtypeStruct((S, D), F32), jax.ShapeDtypeStruct((S, D), BF16),
                   jax.ShapeDtypeStruct((HEADS, S // CHUNK, 128, 128), F32)]
        + [jax.ShapeDtypeStruct(FULL_SHAPES[a], BF16) for a in GATHER_IDS],
        in_specs=[zcol(0), zcol(1), zcol(2), zcol(3),
                  pl.BlockSpec((1, D), lambda t: (0, 0)), pl.BlockSpec((1, 128), lambda t: (0, 0))] + [any_spec] * n,
        out_specs=[out_blk, out_blk, pl.BlockSpec((HEADS, nch, 128, 128), lambda t: (0, t, 0, 0))] + [any_spec] * n,
        scratch_shapes=[pltpu.VMEM((HEADS, 128, 128), F32), pltpu.SemaphoreType.DMA((7 * n,)),
                        pltpu.SemaphoreType.DMA((7 * n,)), pltpu.SemaphoreType.DMA((n,))],
        compiler_params=_cp(("arbitrary",)),
    )(z, z, z, z, lbv, hnw, *shards)


def _half_mask():
    lane = lax.broadcasted_iota(jnp.int32, (1, LANES), 1)
    return (lane % 64) < 32


def _rope(t, cc, ss, first_half):
    partner = jnp.where(first_half, pltpu.roll(t, 96, 1), pltpu.roll(t, 32, 1))
    return t * cc + partner * ss


def _attn_masks():
    i = lax.broadcasted_iota(jnp.int32, (128, 128), 0)
    j = lax.broadcasted_iota(jnp.int32, (128, 128), 1)
    return j >= i, j <= i


def _to_residues_dyn(g, dst, src, row0=0, dtype=None):
    for gi, dil in enumerate((1, 4, 16)):
        m = S // dil

        @pl.when(g == gi)
        def _(dil=dil, m=m):
            for r in range(dil):
                v = src[...] if dil == 1 else src[pl.ds(r, m, stride=dil), :]
                if dtype is not None:
                    v = v.astype(dtype)
                dst[row0 + r * m:row0 + (r + 1) * m, 0:LANES] = v


def _from_residues_dyn(g, dst, src, row0=0):
    for gi, dil in enumerate((1, 4, 16)):
        m = S // dil

        @pl.when(g == gi)
        def _(dil=dil, m=m):
            for r in range(dil):
                v = src[row0 + r * m:row0 + (r + 1) * m, :]
                if dil == 1:
                    dst[...] = v
                else:
                    dst[pl.ds(r, m, stride=dil), :] = v


def _group_blocks(g):
    return jnp.where(g == 0, 16, jnp.where(g == 1, 4, 1))


def _attn_in_specs(extra):
    def zcol(off):
        return pl.BlockSpec((S, LANES), lambda p, g: (0, off + 4 * g + p))

    per_pair = pl.BlockSpec((S, LANES), lambda p, g: (0, p))
    const = pl.BlockSpec((S, LANES), lambda p, g: (0, 0))
    return [zcol(32), zcol(44), zcol(56), pl.BlockSpec((S, LANES), lambda p, g: (0, 68 + p)), const, const] + [per_pair] * extra


def _attn_fwd(z, cc, ss):
    def body(q_ref, k_ref, v_ref, ag_ref, cc_ref, ss_ref, ob_ref, lse_ref, obg_ref,
             tmp, qs, ks, vx, og, mg, lg, o_t, m_t, l_t, o_acc, m_acc, l_acc):
        g = pl.program_id(1)
        first_half = _half_mask()
        prev_ok, cur_ok = _attn_masks()
        lane = lax.broadcasted_iota(jnp.int32, (1, LANES), 1)
        heads = (lane < 64, lane >= 64)
        nblk = _group_blocks(g)

        @pl.when(g == 0)
        def _():
            ks[0:ATT_PAD, :] = jnp.zeros((ATT_PAD, LANES), BF16)
            vx[0:ATT_PAD, 0:LANES] = jnp.zeros((ATT_PAD, LANES), BF16)
            vx[:, LANES:2 * LANES] = jnp.ones((ATT_PAD + S, LANES), BF16)

        tmp[...] = _rope(q_ref[...], cc_ref[...], ss_ref[...], first_half) * ATT_SCALE
        _to_residues_dyn(g, qs, tmp)
        tmp[...] = _rope(k_ref[...], cc_ref[...], ss_ref[...], first_half)
        _to_residues_dyn(g, ks, tmp, ATT_PAD, BF16)
        _to_residues_dyn(g, vx, v_ref, ATT_PAD, BF16)

        def unit(u, carry):
            start = pl.multiple_of(u * 128, 128)
            cur = pl.ds(start, 128)
            pm = prev_ok & ((u & (nblk - 1)) != 0)
            qu = qs[cur, :]
            kcat = ks[pl.ds(start, 256), :]
            vext = vx[pl.ds(start, 256), :]
            o_u = m_u = l_u = None
            for hh in range(2):
                s = _dot_nt(jnp.where(heads[hh], qu, 0.0).astype(BF16), kcat)
                sp = jnp.where(pm, s[:, 0:128], -jnp.inf)
                sc = jnp.where(cur_ok, s[:, 128:256], -jnp.inf)
                m = jnp.max(jnp.maximum(sp, sc), axis=-1, keepdims=True)
                p = jnp.concatenate([jnp.exp(sp - m), jnp.exp(sc - m)], axis=1).astype(BF16)
                ol = _dot(p, vext)
                mb = jnp.broadcast_to(m, (128, LANES))
                if hh == 0:
                    o_u, l_u, m_u = ol[:, 0:128], ol[:, 128:256], mb
                else:
                    o_u = jnp.where(heads[1], ol[:, 0:128], o_u)
                    l_u = jnp.where(heads[1], ol[:, 128:256], l_u)
                    m_u = jnp.where(heads[1], mb, m_u)
            og[cur, :] = o_u
            mg[cur, :] = m_u
            lg[cur, :] = l_u
            return carry

        lax.fori_loop(0, 16, unit, 0, unroll=16)
        _from_residues_dyn(g, o_t, og)
        _from_residues_dyn(g, m_t, mg)
        _from_residues_dyn(g, l_t, lg)

        @pl.when(g == 0)
        def _():
            o_acc[...] = o_t[...]
            m_acc[...] = m_t[...]
            l_acc[...] = l_t[...]

        @pl.when(g > 0)
        def _():
            m_new = jnp.maximum(m_acc[...], m_t[...])
            wa, wb = jnp.exp(m_acc[...] - m_new), jnp.exp(m_t[...] - m_new)
            o_acc[...] = o_acc[...] * wa + o_t[...] * wb
            l_acc[...] = l_acc[...] * wa + l_t[...] * wb
            m_acc[...] = m_new

        @pl.when(g == 2)
        def _():
            ob = o_acc[...] / l_acc[...]
            ob_ref[...] = ob
            lse_ref[...] = m_acc[...] + jnp.log(l_acc[...])
            obg_ref[...] = (ob * _silu(ag_ref[...])).astype(BF16)

    blk = pl.BlockSpec((S, LANES), lambda p, g: (0, p))
    buf = pltpu.VMEM((S, LANES), F32)
    return pl.pallas_call(
        body, name="attn_fwd", grid=(4, 3),
        out_shape=[jax.ShapeDtypeStruct((S, 512), F32), jax.ShapeDtypeStruct((S, 512), F32),
                   jax.ShapeDtypeStruct((S, 512), BF16)],
        in_specs=_attn_in_specs(0), out_specs=[blk, blk, blk],
        scratch_shapes=[buf, buf, pltpu.VMEM((ATT_PAD + S, LANES), BF16), pltpu.VMEM((ATT_PAD + S, 2 * LANES), BF16)] + [buf] * 9,
        compiler_params=_cp(("parallel", "arbitrary")),
    )(z, z, z, z, cc, ss)


def _tail(x, o_a, o_bg, z, target, w_a, w_b, w_out, fnw):
    tm = 256

    def body(x_ref, oa_ref, ob_ref, gpa_ref, gpb_ref, t_ref, wa_ref, wb_ref, wo_ref, fnw_ref,
             dx2_ref, dx2b_ref, dz_hbm, doa_ref, dob_ref, mg_ref, dya_ref, dyb_ref, small_ref, dgp, dgp_sem):
        step = pl.program_id(0)
        slot = step % 2

        def dgp_copy(at_step, at_slot):
            return pltpu.make_async_copy(
                dgp.at[at_slot], dz_hbm.at[pl.ds(pl.multiple_of(at_step * tm, tm), tm), pl.ds(DZ_GATES, 2 * D)],
                dgp_sem.at[at_slot])

        @pl.when(step == 0)
        def _():
            small_ref[...] = jnp.zeros_like(small_ref)

        @pl.when(step >= 2)
        def _():
            dgp_copy(step - 2, slot).wait()

        wa, wb, wo = wa_ref[...], wb_ref[...], wo_ref[...]
        y_a = _dot(oa_ref[...], wa)
        y_b = _dot(ob_ref[...], wb)
        ga = jax.nn.sigmoid(gpa_ref[...])
        gb = jax.nn.sigmoid(gpb_ref[...])
        merged = (ga * y_a + gb * y_b).astype(BF16)
        x2 = x_ref[...] + _dot(merged, wo)
        r2 = lax.rsqrt(jnp.mean(x2 * x2, axis=-1, keepdims=True) + EPS)
        n2 = x2 * r2
        fw = fnw_ref[...]
        err = n2 * fw - t_ref[...]
        loss = 0.5 * jnp.sum(jnp.sum(err * err, axis=-1, keepdims=True), axis=0, keepdims=True) / D
        dy = err * (1.0 / D)
        g_fnw = jnp.sum(dy * n2, axis=0, keepdims=True)
        dn = dy * fw
        dx2 = r2 * (dn - n2 * jnp.mean(dn * n2, axis=-1, keepdims=True))
        dx2b = dx2.astype(BF16)
        dmerged = _dot_nt(dx2b, wo)
        dy_a = (dmerged * ga).astype(BF16)
        dy_b = (dmerged * gb).astype(BF16)
        dx2_ref[...] = dx2
        dx2b_ref[...] = dx2b
        dgp[slot, :, 0:D] = (dmerged * y_a * ga * (1.0 - ga)).astype(BF16)
        dgp[slot, :, D:2 * D] = (dmerged * y_b * gb * (1.0 - gb)).astype(BF16)
        dgp_copy(step, slot).start()
        doa_ref[...] = _dot_nt(dy_a, wa)
        dob_ref[...] = _dot_nt(dy_b, wb)
        mg_ref[...] = merged
        dya_ref[...] = dy_a
        dyb_ref[...] = dy_b
        small_ref[0:1, :] += g_fnw
        small_ref[1:2, :] += jnp.broadcast_to(loss, (1, D))

        @pl.when(step == S // tm - 1)
        def _():
            dgp_copy(step - 1, 1 - slot).wait()
            dgp_copy(step, slot).wait()

    def rows(cols, off=0):
        return pl.BlockSpec((tm, cols), lambda i: (i, off))

    def whole(shape):
        return pl.BlockSpec(shape, lambda i: (0, 0))

    return pl.pallas_call(
        body, name="tail", grid=(S // tm,),
        out_shape=[jax.ShapeDtypeStruct((S, D), F32), jax.ShapeDtypeStruct((S, D), BF16),
                   jax.ShapeDtypeStruct((S, IN_COLS), BF16), jax.ShapeDtypeStruct((S, D), F32),
                   jax.ShapeDtypeStruct((S, 512), F32), jax.ShapeDtypeStruct((S, D), BF16),
                   jax.ShapeDtypeStruct((S, D), BF16), jax.ShapeDtypeStruct((S, D), BF16),
                   jax.ShapeDtypeStruct((8, D), F32)],
        in_specs=[rows(D), rows(D), rows(512), rows(D, 9), rows(D, 10), rows(D),
                  whole((D, D)), whole((512, D)), whole((D, D)), whole((1, D))],
        out_specs=[rows(D), rows(D), pl.BlockSpec(memory_space=pl.ANY), rows(D), rows(512), rows(D), rows(D),
                   rows(D), whole((8, D))],
        scratch_shapes=[pltpu.VMEM((2, tm, 2 * D), BF16), pltpu.SemaphoreType.DMA((2,))],
        compiler_params=_cp(("arbitrary",)),
    )(x, o_a, o_bg, z, z, target, w_a, w_b, w_out, fnw)


def _tn_matmul(a, b, name):
    m, n = a.shape[1], b.shape[1]
    tn = 512

    def body(a_ref, b_ref, o_ref, ob_ref):
        acc = _dot_tn(a_ref[...], b_ref[...])
        o_ref[...] = acc
        ob_ref[...] = acc.astype(BF16)

    out_blk = pl.BlockSpec((m, tn), lambda j: (0, j))
    return pl.pallas_call(
        body, name=name, grid=(n // tn,),
        out_shape=[jax.ShapeDtypeStruct((m, n), F32), jax.ShapeDtypeStruct((m, n), BF16)],
        in_specs=[pl.BlockSpec((S, m), lambda j: (0, 0)), pl.BlockSpec((S, tn), lambda j: (0, j))],
        out_specs=[out_blk, out_blk],
        compiler_params=_cp(("parallel",)),
    )(a, b)


def _hgrn_bwd(z, o, do_a, states, lbv, hnw, partials, dz):
    ntb, nch = S // HBLK, HBLK // CHUNK
    n = len(GATHER_IDS)

    def body(hq_ref, hf_ref, hi_ref, hg_ref, o_ref, doa_ref, st_ref, lb_ref, hnw_ref, p0, p1, p2, dz_in,
             dz_ref, glb_ref, ghn_ref, e0, e1, e2, dstate, send_sems, recv_sems):
        dhq_ref, dhf_ref, dhi_ref, dhg_ref = (dz_ref.at[:, pl.ds(j * D, D)] for j in range(4))
        start, end = _exchange_chips_steps((p0, p1, p2), (e0, e1, e2), send_sems, recv_sems)

        @pl.when(pl.program_id(0) == 0)
        def _():
            dstate[...] = jnp.zeros_like(dstate)
            glb_ref[...] = jnp.zeros_like(glb_ref)
            ghn_ref[...] = jnp.zeros_like(ghn_ref)
            start()

        lb_a = lb_ref[...]
        hq_a, hg_a = hq_ref[...], hg_ref[...]
        q_a, k_a, g_a, v_a, sg_a, f_a, b_a = _hgrn_cols(hq_a, hf_ref[...], hi_ref[...], lb_a)
        bex_a = b_a - g_a
        eb_a = jnp.exp(b_a)
        w = hnw_ref[...]
        mask = _tril_mask(CHUNK)
        upper = _block_tri(CHUNK, CHUNK, upper=True)
        for h in range(HEADS):
            cols = slice(128 * h, 128 * h + 128)
            q, k, v, b, bex, eb = q_a[:, cols], k_a[:, cols], v_a[:, cols], b_a[:, cols], bex_a[:, cols], eb_a[:, cols]
            hq, hg, sg, f, lb = hq_a[:, cols], hg_a[:, cols], sg_a[:, cols], f_a[:, cols], lb_a[:, cols]
            ov, doa = o_ref[:, cols], doa_ref[:, cols]
            r = lax.rsqrt(jnp.mean(ov * ov, axis=-1, keepdims=True) + EPS)
            n = ov * r
            sil = _silu(hg)
            dhg_ref[:, cols] = (doa * n * w * _dsilu(hg)).astype(BF16)
            ghn_ref[h] += jnp.sum(doa * sil * n, axis=0, keepdims=True)
            dn = doa * sil * w
            do = r * (dn - n * jnp.mean(dn * n, axis=-1, keepdims=True))

            dst = dstate[h]
            dq_l, dk_l, dv_l, dg_l = [None] * nch, [None] * nch, [None] * nch, [None] * nch
            for c in reversed(range(nch)):
                r0 = c * CHUNK
                rows = slice(r0, r0 + CHUNK)
                st = st_ref[h, c]
                bc, kc, qc = b[rows], k[rows], q[rows]
                vb, dob = v[rows].astype(BF16), do[rows].astype(BF16)
                b_last = bc[CHUNK - 1:CHUNK]
                e_last = jnp.exp(b_last)
                ekl = jnp.exp(b_last - bc)
                dstb = dst.astype(BF16)
                a, qs_l, ks_l, ek_l, eq_l = _chunk_scores(q, k, b, bex, r0, mask)
                da = jnp.where(mask, _dot_nt(dob, vb), 0.0)
                dv_l[c] = _dot_tn(a.astype(BF16), dob) + _dot_nt((kc * ekl).astype(BF16), dstb)
                dq_inter = _dot(dob, st.astype(BF16)) * eb[rows]
                dk_state = _dot(vb, dstb) * ekl
                dq_parts, dk_intra = [], jnp.zeros((CHUNK, 128), F32)
                dab = da.astype(BF16)
                for i in range(CHUNK // SUB):
                    da_i = dab[SUB * i:SUB * (i + 1)]
                    ks_hi, ks_lo = _split2(ks_l[i])
                    qs_hi, qs_lo = _split2(qs_l[i])
                    dq_parts.append((_dot(da_i, ks_hi) + _dot(da_i, ks_lo)) * eq_l[i])
                    dk_intra = dk_intra + (_dot_tn(da_i, qs_hi) + _dot_tn(da_i, qs_lo)) * ek_l[i]
                dq = jnp.concatenate(dq_parts, axis=0) + dq_inter
                dk = dk_intra + dk_state
                last = (e_last * jnp.sum(st * dst, axis=0, keepdims=True)
                        + jnp.sum(kc * dk_state, axis=0, keepdims=True))
                dg_l[c] = _dot_ones(upper, qc * dq - kc * dk) + last
                dq_l[c], dk_l[c] = dq, dk
                dst = dst * e_last + _dot_tn(dob, (qc * eb[rows]).astype(BF16))
            dstate[h] = dst
            dq, dk = jnp.concatenate(dq_l, axis=0), jnp.concatenate(dk_l, axis=0)
            dg, dv = jnp.concatenate(dg_l, axis=0), jnp.concatenate(dv_l, axis=0)
            dhq_ref[:, cols] = (dq * _dsilu(hq)).astype(BF16)
            dhi_ref[:, cols] = dv.astype(BF16)
            df = dg / f - dk
            dhf_ref[:, cols] = (df * (1.0 - lb) * sg * (1.0 - sg)).astype(BF16)
            glb_ref[:, cols] += jnp.sum(df * (1.0 - sg), axis=0, keepdims=True)

        pl.when(pl.program_id(0) == ntb - 1)(end)

    def rev(t):
        return ntb - 1 - t

    def zcol(j):
        return pl.BlockSpec((HBLK, D), lambda t: (rev(t), j))

    blk = pl.BlockSpec((HBLK, D), lambda t: (rev(t), 0))
    any_spec = pl.BlockSpec(memory_space=pl.ANY)
    return pl.pallas_call(
        body, name="hgrn_bwd", grid=(ntb,),
        out_shape=[jax.ShapeDtypeStruct((S, IN_COLS), BF16)]
        + [jax.ShapeDtypeStruct((1, D), F32), jax.ShapeDtypeStruct((HEADS, 1, 128), F32)]
        + [jax.ShapeDtypeStruct((3,) + SHARD_SHAPES[a], BF16) for a in GATHER_IDS],
        in_specs=[zcol(0), zcol(1), zcol(2), zcol(3), blk, blk,
                  pl.BlockSpec((HEADS, nch, 128, 128), lambda t: (0, rev(t), 0, 0)),
                  pl.BlockSpec((1, D), lambda t: (0, 0)), pl.BlockSpec((1, 128), lambda t: (0, 0))]
        + [any_spec] * (n + 1),
        out_specs=[pl.BlockSpec((HBLK, DZ_ATT), lambda t: (rev(t), 0)), pl.BlockSpec((1, D), lambda t: (0, 0)),
                   pl.BlockSpec((HEADS, 1, 128), lambda t: (0, 0, 0))] + [any_spec] * n,
        scratch_shapes=[pltpu.VMEM((HEADS, 128, 128), F32), pltpu.SemaphoreType.DMA((3 * n,)),
                        pltpu.SemaphoreType.DMA((3 * n,))],
        input_output_aliases={9 + n: 0},
        compiler_params=_cp(("arbitrary",)),
    )(z, z, z, z, o, do_a, states, lbv, hnw, *partials, dz)


def _attn_bwd(z, cc, ss, ob, lse, do_bg, dz):
    def body(q_ref, k_ref, v_ref, ag_ref, cc_ref, ss_ref, ob_ref, lse_ref, dobg_ref, dz_in, dz_hbm,
             tmp, qs, ks, vs, dos, dqs, dks, dvs, dkp, dvp, do_t, ls0_t, ls1_t, dl0_t, dl1_t, ls0, ls1, dl0, dl1,
             stage, stage_sem):
        pair, g = pl.program_id(0), pl.program_id(1)

        def out_copy(j):
            tile = DZ_ATT // LANES + (36 + pair if j == 3 else 12 * j + 4 * g + pair)
            return pltpu.make_async_copy(
                stage.at[j], dz_hbm.at[:, pl.ds(pl.multiple_of(tile * LANES, LANES), LANES)], stage_sem.at[j])

        def restage(j, value):
            pl.when(pair * 3 + g > 0)(lambda: out_copy(j).wait())
            stage[j] = value
            out_copy(j).start()

        pl.when(g == 2)(lambda: out_copy(3).wait())
        first_half = _half_mask()
        prev_ok, cur_ok = _attn_masks()
        lane = lax.broadcasted_iota(jnp.int32, (1, LANES), 1)
        heads = (lane < 64, lane >= 64)
        nblk = _group_blocks(g)
        cc_v, ss_v = cc_ref[...], ss_ref[...]

        @pl.when(g == 0)
        def _():
            ag, obv, dobg = ag_ref[...], ob_ref[...], dobg_ref[...]
            stage[3] = (dobg * obv * _dsilu(ag)).astype(BF16)
            out_copy(3).start()
            dob = dobg * _silu(ag)
            do_t[...] = dob
            prod = dob * obv
            dl = jnp.concatenate(
                [jnp.broadcast_to(jnp.sum(prod[:, 0:64], axis=-1, keepdims=True), (S, 64)),
                 jnp.broadcast_to(jnp.sum(prod[:, 64:128], axis=-1, keepdims=True), (S, 64))], axis=1)
            dl_sw = pltpu.roll(dl, 64, 1)
            dl0_t[...] = jnp.where(heads[0], dl, dl_sw)
            dl1_t[...] = jnp.where(heads[0], dl_sw, dl)
            ls = lse_ref[...]
            ls_sw = pltpu.roll(ls, 64, 1)
            ls0_t[...] = jnp.where(heads[0], ls, ls_sw)
            ls1_t[...] = jnp.where(heads[0], ls_sw, ls)
            ks[0:ATT_PAD, :] = jnp.zeros((ATT_PAD, LANES), BF16)
            vs[0:ATT_PAD, :] = jnp.zeros((ATT_PAD, LANES), BF16)

        tmp[...] = _rope(q_ref[...], cc_v, ss_v, first_half) * ATT_SCALE
        _to_residues_dyn(g, qs, tmp)
        tmp[...] = _rope(k_ref[...], cc_v, ss_v, first_half)
        _to_residues_dyn(g, ks, tmp, ATT_PAD, BF16)
        _to_residues_dyn(g, vs, v_ref, ATT_PAD, BF16)
        _to_residues_dyn(g, dos, do_t)
        _to_residues_dyn(g, ls0, ls0_t)
        _to_residues_dyn(g, ls1, ls1_t)
        _to_residues_dyn(g, dl0, dl0_t)
        _to_residues_dyn(g, dl1, dl1_t)
        lss, dls = (ls0, ls1), (dl0, dl1)

        def unit(u, carry):
            start = pl.multiple_of(u * 128, 128)
            cur = pl.ds(start, 128)
            both = pl.ds(start, 256)
            pm = prev_ok & ((u & (nblk - 1)) != 0)
            qu, dou = qs[cur, :], dos[cur, :]
            kcat, vcat = ks[both, :], vs[both, :]
            dq_u = None
            q_l, do_l, ds_l, p_l = [], [], [], []
            for hh in range(2):
                q_h = jnp.where(heads[hh], qu, 0.0).astype(BF16)
                do_h = jnp.where(heads[hh], dou, 0.0).astype(BF16)
                s = _dot_nt(q_h, kcat)
                dp = _dot_nt(do_h, vcat)
                lse_h, dl_h = lss[hh][cur, :], dls[hh][cur, :]
                pp = jnp.where(pm, jnp.exp(s[:, 0:128] - lse_h), 0.0)
                pc = jnp.where(cur_ok, jnp.exp(s[:, 128:256] - lse_h), 0.0)
                ds = jnp.concatenate([pp * (dp[:, 0:128] - dl_h), pc * (dp[:, 128:256] - dl_h)], axis=1).astype(BF16)
                dq = _dot(ds, kcat)
                dq_u = dq if hh == 0 else jnp.where(heads[1], dq, dq_u)
                q_l.append(q_h)
                do_l.append(do_h)
                ds_l.append(ds)
                p_l.append(jnp.concatenate([pp, pc], axis=1).astype(BF16))
            dkcat = _dot_tn(jnp.concatenate(ds_l, axis=0), jnp.concatenate(q_l, axis=0))
            dvcat = _dot_tn(jnp.concatenate(p_l, axis=0), jnp.concatenate(do_l, axis=0))
            dkp[cur, :] = dkcat[0:128]
            dks[cur, :] = dkcat[128:256]
            dvp[cur, :] = dvcat[0:128]
            dvs[cur, :] = dvcat[128:256]
            dqs[cur, :] = dq_u
            return carry

        lax.fori_loop(0, 16, unit, 0, unroll=ATT_UNROLL)
        dks[0:S - 128, :] += dkp[128:S, :]
        dvs[0:S - 128, :] += dvp[128:S, :]
        _from_residues_dyn(g, tmp, dqs)
        restage(0, (_rope(tmp[...], cc_v, -ss_v, first_half) * ATT_SCALE).astype(BF16))
        _from_residues_dyn(g, tmp, dks)
        restage(1, _rope(tmp[...], cc_v, -ss_v, first_half).astype(BF16))
        _from_residues_dyn(g, tmp, dvs)
        restage(2, tmp[...].astype(BF16))

        @pl.when(pair * 3 + g == 11)
        def _():
            for j in range(3):
                out_copy(j).wait()

    any_spec = pl.BlockSpec(memory_space=pl.ANY)
    buf = pltpu.VMEM((S, LANES), F32)
    padded_b = pltpu.VMEM((ATT_PAD + S, LANES), BF16)
    return pl.pallas_call(
        body, name="attn_bwd", grid=(4, 3),
        out_shape=jax.ShapeDtypeStruct((S, IN_COLS), BF16),
        in_specs=_attn_in_specs(3) + [any_spec], out_specs=any_spec,
        scratch_shapes=[buf, buf, padded_b, padded_b] + [buf] * 15
        + [pltpu.VMEM((4, S, LANES), BF16), pltpu.SemaphoreType.DMA((4,))],
        input_output_aliases={9: 0},
        compiler_params=_cp(("arbitrary", "arbitrary")),
    )(z, z, z, z, cc, ss, ob, lse, do_bg, dz)


def _in_proj_bwd(dz, h, w_in):
    half = S // 2
    slab = (D, SHARD_COLS)

    def body(dz_hbm, h_hbm, w_hbm, dh_hbm, g_chip, r1_hbm, relay_hbm, r2_hbm,
             h_buf, dz_buf, stage_d, r1_buf, stage_i, acc,
             dz_sem, w_sem, h_sem, r1_sem, out_sem, send_d, recv_d, send_i, recv_i):
        x, y, c = _mesh_pos()
        sibling = (x, y, 1 - c)
        north = c == 1
        near = (jnp.where(north, 1 - x, x), jnp.where(north, y, 1 - y))
        far = (jnp.where(north, x, 1 - x), jnp.where(north, 1 - y, y))
        chips = [(1 - x, 1 - y), near, far, (x, y)]

        def cols(d):
            return pl.ds(pl.multiple_of(d * SHARD_COLS, LANES), SHARD_COLS)

        blocks = []
        for q_sib, q in zip([chips[0], far, near, chips[3]], chips):
            blocks += [4 * q_sib[0] + 2 * q_sib[1] + (1 - c), 4 * q[0] + 2 * q[1] + c]

        def dz_tile(t):
            return _SplitCopy(dz_hbm.at[pl.ds((t % 2) * half, half), cols(blocks[t // 2])],
                                         dz_buf.at[t % 2], dz_sem.at[t % 2])

        def to_sibling(i):
            return pltpu.make_async_remote_copy(
                src_ref=stage_d.at[i % 2], dst_ref=r1_hbm.at[i], send_sem=send_d.at[i], recv_sem=recv_d.at[i],
                device_id=sibling, device_id_type=MESH)

        def to_owner(i):
            dst = relay_hbm if i == 0 else r2_hbm.at[i - 1]
            return pltpu.make_async_remote_copy(
                src_ref=stage_i.at[i], dst_ref=dst, send_sem=send_i.at[i], recv_sem=recv_i.at[i],
                device_id=(*(far if i == 2 else near), c), device_id_type=MESH)

        h_copy = _SplitCopy(h_hbm, h_buf, h_sem)
        h_copy.start()
        dz_tile(0).start()
        h_copy.wait()
        for b in range(8):
            i = b // 2
            g = None
            for r in range(2):
                t = 2 * b + r
                if t + 1 < 16:
                    dz_tile(t + 1).start()
                dz_tile(t).wait()
                part = _dot_tn(h_buf[r * half:(r + 1) * half, :], dz_buf[t % 2])
                g = part if g is None else g + part
                if b % 2 == 1 and r == 0:
                    to_sibling(i).wait_recv()
                    r1_copy = _SplitCopy(r1_hbm.at[i], r1_buf, r1_sem)
                    r1_copy.start()
            if b % 2 == 0:
                if i >= 2:
                    to_sibling(i - 2).wait_send()
                stage_d[i % 2] = g.astype(BF16)
                to_sibling(i).start()
            else:
                r1_copy.wait()
                g = g + r1_buf[...].astype(F32)
                if i == 2:
                    to_owner(0).wait_recv()
                    relay_copy = _SplitCopy(relay_hbm, r1_buf, r1_sem)
                    relay_copy.start()
                    relay_copy.wait()
                    g = g + r1_buf[...].astype(F32)
                if i < 3:
                    stage_i[i] = g.astype(BF16)
                    to_owner(i).start()
                else:
                    g_chip[...] = g
        to_sibling(2).wait_send()
        to_sibling(3).wait_send()

        def dz2(t):
            return _SplitCopy(
                dz_hbm.at[pl.ds((t % 2) * half, half), pl.ds((t // 2) * SHARD_COLS, SHARD_COLS)],
                dz_buf.at[t % 2], dz_sem.at[t % 2])

        def w2(b):
            return _SplitCopy(w_hbm.at[:, pl.ds(b * SHARD_COLS, SHARD_COLS)],
                                         stage_d.at[b % 2], w_sem.at[b % 2])

        dz2(0).start()
        w2(0).start()
        for t in range(16):
            b, r = t // 2, t % 2
            if t + 1 < 16:
                dz2(t + 1).start()
            if r == 0:
                if b + 1 < 8:
                    w2(b + 1).start()
                w2(b).wait()
            dz2(t).wait()
            part = _dot_nt(dz_buf[t % 2], stage_d[b % 2])
            if b == 0:
                acc[r] = part
            else:
                acc[r] += part
        dh_out = [_SplitCopy(acc.at[r], dh_hbm.at[pl.ds(r * half, half), :], out_sem.at[r])
                  for r in range(2)]
        for cp in dh_out:
            cp.start()
        for cp in dh_out:
            cp.wait()
        for i in range(3):
            to_owner(i).wait_send()
        for i in (1, 2):
            to_owner(i).wait_recv()

    any_spec = pl.BlockSpec(memory_space=pl.ANY)
    return pl.pallas_call(
        body, name="in_proj_bwd",
        out_shape=[jax.ShapeDtypeStruct((S, D), F32), jax.ShapeDtypeStruct(slab, F32),
                   jax.ShapeDtypeStruct((4,) + slab, BF16), jax.ShapeDtypeStruct(slab, BF16),
                   jax.ShapeDtypeStruct((2,) + slab, BF16)],
        in_specs=[any_spec] * 3,
        out_specs=[any_spec, pl.BlockSpec(memory_space=pltpu.VMEM), any_spec, any_spec, any_spec],
        scratch_shapes=[pltpu.VMEM((S, D), BF16), pltpu.VMEM((2, half, SHARD_COLS), BF16),
                        pltpu.VMEM((2,) + slab, BF16), pltpu.VMEM(slab, BF16), pltpu.VMEM((3,) + slab, BF16),
                        pltpu.VMEM((2, half, D), F32),
                        pltpu.SemaphoreType.DMA((2,)), pltpu.SemaphoreType.DMA((2,)), pltpu.SemaphoreType.DMA,
                        pltpu.SemaphoreType.DMA, pltpu.SemaphoreType.DMA((2,)),
                        pltpu.SemaphoreType.DMA((4,)), pltpu.SemaphoreType.DMA((4,)),
                        pltpu.SemaphoreType.DMA((3,)), pltpu.SemaphoreType.DMA((3,))],
        compiler_params=_cp(),
    )(dz, h, w_in)


def _grad_x(x, norm_w, dh, dx2):
    tr = 256

    def body(x_ref, w_ref, dh_ref, dx2_ref, gx_ref, gnw_ref):
        @pl.when(pl.program_id(0) == 0)
        def _():
            gnw_ref[...] = jnp.zeros_like(gnw_ref)

        xv, dhv = x_ref[...], dh_ref[...]
        r = lax.rsqrt(jnp.mean(xv * xv, axis=-1, keepdims=True) + EPS)
        n = xv * r
        gnw_ref[...] += jnp.sum(dhv * n, axis=0, keepdims=True)
        dn = dhv * w_ref[...]
        gx_ref[...] = dx2_ref[...] + r * (dn - n * jnp.mean(dn * n, axis=-1, keepdims=True))

    row = pl.BlockSpec((tr, D), lambda i: (i, 0))
    vec = pl.BlockSpec((1, D), lambda i: (0, 0))
    return pl.pallas_call(
        body, name="grad_x", grid=(S // tr,),
        out_shape=[jax.ShapeDtypeStruct((S, D), F32), jax.ShapeDtypeStruct((1, D), F32)],
        in_specs=[row, vec, row, row], out_specs=[row, vec],
        compiler_params=_cp(("arbitrary",)),
    )(x, norm_w, dh, dx2)


def _rope_tables(positions):
    inv_freq = 10000.0 ** (-jnp.arange(0, 64, 2, dtype=F32) / 64)
    ang = positions.astype(F32)[:, None] * inv_freq[None, :]
    cos, sin = jnp.cos(ang), jnp.sin(ang)
    return jnp.tile(cos, (1, 4)), jnp.tile(jnp.concatenate([-sin, sin], axis=1), (1, 2))


def _local_step(x, positions, norm_w, lb_logits, hnw, fnw, target, w_in_shard, small_shards, core):
    cc, ss = _rope_tables(positions)
    lbv = jax.nn.sigmoid(lb_logits[0:1] - lb_logits[1:2])
    z, w_in, h = _in_proj_gather(x, norm_w, w_in_shard)
    o, o_a, states, w_a, w_b, w_out = _hgrn_fwd(z, lbv, hnw, small_shards)
    ob, lse, o_bg = _attn_fwd(z, cc, ss)
    dx2, dx2b, dz, do_a, do_bg, merged, dy_a, dy_b, tail_small = _tail(x, o_a, o_bg, z, target, w_a, w_b, w_out, fnw)
    g_out, gb_out = _tn_matmul(merged, dx2b, "grad_w_out")
    g_a, gb_a = _tn_matmul(o_a, dy_a, "grad_w_a")
    g_b, gb_b = _tn_matmul(o_bg, dy_b, "grad_w_b")
    grads, gb = (g_a, g_b, g_out), (gb_a, gb_b, gb_out)
    r1 = _exchange_sibling(GATHER_IDS, gb)
    pb = [_chip_partials(a, grads[i], r1[i], core) for i, a in enumerate(GATHER_IDS)]
    dz, glb, ghn, *r2 = _hgrn_bwd(z, o, do_a, states, lbv, hnw, pb, dz)
    dz = _attn_bwd(z, cc, ss, ob, lse, do_bg, dz)
    dh, g_chip_in, _, _, r2_in = _in_proj_bwd(dz, h, w_in)
    grad_x, gnw = _grad_x(x, norm_w, dh, dx2)
    ghn_row = jnp.pad(jnp.sum(ghn, axis=0), ((0, 0), (0, D - 128)))
    small = jnp.concatenate([gnw, glb, ghn_row, tail_small[0:2], jnp.zeros((3, D), F32)], axis=0)
    return grad_x, (g_chip_in, r2_in), grads, r1, r2, small


def kernel(x, positions, norm_w, w_in, lb_logits, hgrn_norm_w, w_branch_a, w_branch_b, w_out, final_norm_w, loss_target, m_norm_w, m_w_in, m_lb_logits, m_hgrn_norm_w, m_w_branch_a, m_w_branch_b, m_w_out, m_final_norm_w, v_norm_w, v_w_in, v_lb_logits, v_hgrn_norm_w, v_w_branch_a, v_w_branch_b, v_w_out, v_final_norm_w):
    ix, iy, ic = _mesh_pos()
    core = jnp.reshape(ic, (1,)).astype(jnp.int32)
    pos = jnp.stack([4 * ix + 2 * iy + ic, 2 * ix + iy]).astype(jnp.int32)

    shards = [w_in[0], w_branch_a[0], w_branch_b[0], w_out[0]]
    moments_m = [m_w_in[0], m_w_branch_a[0], m_w_branch_b[0], m_w_out[0]]
    moments_v = [v_w_in[0], v_w_branch_a[0], v_w_branch_b[0], v_w_out[0]]
    names = ("w_in", "w_a", "w_b", "w_out")
    ids = GATHER_IDS
    shards_b = [_cast_bf16(w, f"cast_{nm}") for w, nm in zip(shards, names)]

    fnw2 = final_norm_w.reshape(1, D)
    grad_x, (g_chip_in, r2_in), grads, r1, r2, small = _local_step(
        x[0], positions[0], norm_w, lb_logits, hgrn_norm_w, fnw2, loss_target[0], shards_b[0], shards_b[1:], core)

    gathered = _gather_small(small)
    big =[_reduce_own_and_update(shards[0], moments_m[0], moments_v[0], g_chip_in, r2_in)]
    big += [_reduce_and_update(a, shards[a], moments_m[a], moments_v[a], grads[i], r1[i], r2[i], pos)
            for i, a in enumerate(ids)]
    sm = _small_update(gathered, norm_w, lb_logits, hgrn_norm_w, fnw2,
                       (m_norm_w, m_lb_logits, m_hgrn_norm_w, m_final_norm_w.reshape(1, D),
                        v_norm_w, v_lb_logits, v_hgrn_norm_w, v_final_norm_w.reshape(1, D)))
    loss = sm[0][0, 0]
    outs = [loss, grad_x[None]]
    for kind in range(4):
        s_nw, s_lb, s_hn, s_fn = sm[1 + 4 * kind:5 + 4 * kind]
        outs += [s_nw, big[0][kind][None], s_lb, s_hn, big[1][kind][None], big[2][kind][None],
                 big[3][kind][None], s_fn.reshape(D)]
    return tuple(outs)
```

```python
import functools

import jax
import jax.numpy as jnp
from jax import lax
from jax.experimental import pallas as pl
from jax.experimental.pallas import tpu as pltpu

F32 = jnp.float32
BF16 = jnp.bfloat16
MESH = pl.DeviceIdType.MESH

S = 2048
D = 1024
NDEV = 8
HEADS = 8
CHUNK = 64
SUB = 16
HBLK = 256
ATT_PAD = 128
ATT_UNROLL = 16
COPY_PARTS = 4
EXP_CLAMP = 80.0
EPS = 1e-6
IN_COLS = 11264
SHARD_COLS = IN_COLS // NDEV
DZ_ATT = 4096
DZ_GATES = 9216
ATT_DILS = (1, 4, 16)
ATT_SCALE = 64 ** -0.5
LANES = 128

ADAM_LR, ADAM_B1, ADAM_B2, ADAM_EPS, ADAM_WD, ADAM_STEP = 0.001, 0.9, 0.999, 1e-08, 0.01, 10

VMEM_LIMIT = 56 * 1024 * 1024


def _cp(sem=None, **kw):
    return pltpu.CompilerParams(dimension_semantics=sem, vmem_limit_bytes=VMEM_LIMIT, **kw)


def _dot(a, b):
    return jnp.dot(a, b, preferred_element_type=F32)


def _dot_nt(a, b):
    return lax.dot_general(a, b, (((1,), (1,)), ((), ())), preferred_element_type=F32)


def _dot_tn(a, b):
    return lax.dot_general(a, b, (((0,), (0,)), ((), ())), preferred_element_type=F32)


def _split2(x):
    hi = x.astype(BF16)
    lo = (x - hi.astype(F32)).astype(BF16)
    return hi, lo


def _split3(x):
    hi = x.astype(BF16)
    r = x - hi.astype(F32)
    mid = r.astype(BF16)
    lo = (r - mid.astype(F32)).astype(BF16)
    return hi, mid, lo


def _dot_ones(ones_bf16, x):
    hi, mid, lo = _split3(x)
    return _dot(ones_bf16, hi) + _dot(ones_bf16, mid) + _dot(ones_bf16, lo)


def _silu(x):
    return x * jax.nn.sigmoid(x)


def _dsilu(x):
    s = jax.nn.sigmoid(x)
    return s * (1.0 + x * (1.0 - s))


def _mesh_pos():
    return lax.axis_index("x"), lax.axis_index("y"), lax.axis_index("c")


class _SplitCopy:
    def __init__(self, src, dst, sem):
        self.src, self.dst, self.sem = src, dst, sem

    def start(self):
        rows = self.src.shape[0] // COPY_PARTS
        for p in range(COPY_PARTS):
            chunk = pl.ds(p * rows, rows)
            pltpu.make_async_copy(self.src.at[chunk], self.dst.at[chunk], self.sem).start()

    def wait(self):
        pltpu.make_async_copy(self.src, self.dst, self.sem).wait()


def _shard_of(ref, a, d):
    if a == 0:
        return ref.at[:, pl.ds(pl.multiple_of(d * SHARD_COLS, LANES), SHARD_COLS)]
    if a == 2:
        return ref.at[:, pl.ds(pl.multiple_of(d * LANES, LANES), LANES)]
    return ref.at[pl.ds(pl.multiple_of(d * 128, 128), 128), :]


FULL_SHAPES = ((D, IN_COLS), (D, D), (512, D), (D, D))
SHARD_SHAPES = ((D, SHARD_COLS), (128, D), (512, 128), (128, D))


def _allgather_steps(ids, ins, outs, send_sems, recv_sems, local_sems):
    n = len(ids)
    x, y, c = _mesh_pos()
    me, sibling = (x, y, c), (x, y, 1 - c)
    chips = [(1 - x, y), (x, 1 - y), (1 - x, 1 - y)]

    def blk(a, p):
        return _shard_of(outs[a], ids[a], 4 * p[0] + 2 * p[1] + p[2])

    def copy(a, k, block, to, src=None):
        return pltpu.make_async_remote_copy(
            src_ref=blk(a, block) if src is None else src, dst_ref=blk(a, block),
            send_sem=send_sems.at[a * 7 + k], recv_sem=recv_sems.at[a * 7 + k],
            device_id=to, device_id_type=MESH)

    mine = [pltpu.make_async_copy(ins[a], blk(a, me), local_sems.at[a]) for a in range(n)]
    first = []
    for a in range(n):
        first += [copy(a, 1 + j, me, (*chip, c), src=ins[a]) for j, chip in enumerate(chips)]
    for a in range(n):
        first.append(copy(a, 0, me, sibling, src=ins[a]))
    passed = [copy(a, 4 + j, (*chip, c), sibling) for j, chip in enumerate(chips) for a in range(n)]

    def start():
        for cp in mine + first:
            cp.start()

    def middle():
        for j, chip in enumerate(chips):
            for a in range(n):
                copy(a, 1 + j, (*chip, c), me).wait_recv()
                passed[j * n + a].start()

    def end():
        for a in range(n):
            copy(a, 0, sibling, me).wait_recv()
        for j, chip in enumerate(chips):
            for a in range(n):
                copy(a, 4 + j, (*chip, 1 - c), me).wait_recv()
        for cp in first + passed:
            cp.wait_send()
        for cp in mine:
            cp.wait()

    return start, middle, end


def _in_proj_gather(x, norm_w, w_shard):
    half = S // 2
    slab = (D, SHARD_COLS)
    xt = 512

    def body(x_hbm, nw_ref, w_hbm, z_hbm, wfull_hbm, h_hbm, h_buf, land, zstage, xbuf,
             h_sem, own_sem, z_sem, wout_sem, x_sem, send_sems, recv_sems):
        x, y, c = _mesh_pos()
        sibling = (x, y, 1 - c)
        north = c == 1

        def chips_of(first_x):
            near = (jnp.where(first_x, 1 - x, x), jnp.where(first_x, y, 1 - y))
            far = (jnp.where(first_x, x, 1 - x), jnp.where(first_x, 1 - y, y))
            return [near, far, (1 - x, 1 - y)]

        mine, theirs = chips_of(north), chips_of(jnp.logical_not(north))

        def dev(chip, core):
            return 4 * chip[0] + 2 * chip[1] + core

        block_of = ([dev((x, y), c), dev((x, y), 1 - c)] + [dev(q, c) for q in mine]
                    + [dev(q, 1 - c) for q in theirs])

        def cols(d):
            if isinstance(d, int):
                return pl.ds(d * SHARD_COLS, SHARD_COLS)
            return pl.ds(pl.multiple_of(d * SHARD_COLS, LANES), SHARD_COLS)

        def send(k, src, dst_slot, to):
            return pltpu.make_async_remote_copy(
                src_ref=src, dst_ref=land.at[dst_slot], send_sem=send_sems.at[k], recv_sem=recv_sems.at[k],
                device_id=to, device_id_type=MESH)

        def to_sibling():
            return send(0, w_hbm, 1, sibling)

        def to_chip(j):
            if j == 2:
                return send(3, land.at[2], 4, (*mine[1], c))
            return send(1 + j, w_hbm, 2 + j, (*mine[j], c))

        def pass_on(j):
            return send(4 + j, land.at[2 + j], 5 + j, sibling)

        def x_tile(i):
            return _SplitCopy(x_hbm.at[pl.ds(i * xt, xt), :], xbuf.at[i % 2], x_sem.at[i % 2])

        own = _SplitCopy(w_hbm, land.at[0], own_sem)
        own.start()
        x_tile(0).start()
        to_sibling().start()
        to_chip(0).start()
        for i in range(S // xt):
            if i + 1 < S // xt:
                x_tile(i + 1).start()
            x_tile(i).wait()
            xv = xbuf[i % 2]
            r = lax.rsqrt(jnp.mean(xv * xv, axis=-1, keepdims=True) + EPS)
            h_buf[i * xt:(i + 1) * xt, :] = (xv * r * nw_ref[...]).astype(BF16)
        h_out = _SplitCopy(h_buf, h_hbm, h_sem)
        h_out.start()
        own.wait()

        def multiply(slot, n_done):
            d = block_of[slot]
            out = _SplitCopy(land.at[slot], wfull_hbm.at[:, cols(d)], wout_sem.at[slot])
            out.start()
            for r in range(2):
                rows = pl.ds(r * half, half)
                zc = _SplitCopy(zstage.at[r], z_hbm.at[rows, cols(d)], z_sem.at[r])
                if n_done > 0:
                    zc.wait()
                zstage[r] = _dot(h_buf[r * half:(r + 1) * half, :], land[slot])
                zc.start()
            return out

        outs = [multiply(0, 0)]
        to_sibling().wait_recv()
        outs.append(multiply(1, 1))
        done = 2
        for j in range(3):
            to_chip(j).wait_recv()
            pass_on(j).start()
            to_chip(j).wait_send()
            if j < 2:
                to_chip(j + 1).start()
            outs.append(multiply(2 + j, done))
            pass_on(j).wait_recv()
            outs.append(multiply(5 + j, done + 1))
            done += 2
        for r in range(2):
            _SplitCopy(zstage.at[r], z_hbm.at[pl.ds(r * half, half), cols(0)], z_sem.at[r]).wait()
        for out in outs:
            out.wait()
        h_out.wait()
        to_sibling().wait_send()
        for j in range(3):
            pass_on(j).wait_send()

    any_spec = pl.BlockSpec(memory_space=pl.ANY)
    return pl.pallas_call(
        body, name="in_proj_gather",
        out_shape=[jax.ShapeDtypeStruct((S, IN_COLS), F32), jax.ShapeDtypeStruct((D, IN_COLS), BF16),
                   jax.ShapeDtypeStruct((S, D), BF16)],
        in_specs=[any_spec, pl.BlockSpec(memory_space=pltpu.VMEM), any_spec], out_specs=[any_spec] * 3,
        scratch_shapes=[pltpu.VMEM((S, D), BF16), pltpu.VMEM((8,) + slab, BF16), pltpu.VMEM((2, half, SHARD_COLS), F32),
                        pltpu.VMEM((2, xt, D), F32),
                        pltpu.SemaphoreType.DMA, pltpu.SemaphoreType.DMA, pltpu.SemaphoreType.DMA((2,)),
                        pltpu.SemaphoreType.DMA((8,)), pltpu.SemaphoreType.DMA((2,)),
                        pltpu.SemaphoreType.DMA((7,)), pltpu.SemaphoreType.DMA((7,))],
        compiler_params=_cp(),
    )(x, norm_w, w_shard)


def _exchange_sibling(ids, gb):
    n = len(gb)

    def body(*refs):
        ins, outs = refs[:n], refs[n:2 * n]
        send_sems, recv_sems = refs[2 * n:]
        x, y, c = _mesh_pos()
        sibling = (x, y, 1 - c)
        copies = []
        for i, a in enumerate(ids):
            for q in range(4):
                copies.append(pltpu.make_async_remote_copy(
                    src_ref=_shard_of(ins[i], a, 2 * q + (1 - c)), dst_ref=outs[i].at[q],
                    send_sem=send_sems.at[i * 4 + q], recv_sem=recv_sems.at[i * 4 + q],
                    device_id=sibling, device_id_type=MESH))
        for cp in copies:
            cp.start()
        for cp in copies:
            cp.wait()

    any_spec = pl.BlockSpec(memory_space=pl.ANY)
    return pl.pallas_call(
        body, name="grads_to_sibling",
        out_shape=[jax.ShapeDtypeStruct((4,) + SHARD_SHAPES[a], BF16) for a in ids],
        in_specs=[any_spec] * n, out_specs=[any_spec] * n,
        scratch_shapes=[pltpu.SemaphoreType.DMA((4 * n,)), pltpu.SemaphoreType.DMA((4 * n,))],
    )(*gb)


def _exchange_chips_steps(ins, outs, send_sems, recv_sems):
    x, y, c = _mesh_pos()
    chips = [(1 - x, y), (x, 1 - y), (1 - x, 1 - y)]
    copies = []
    for a in range(len(ins)):
        for k, chip in enumerate(chips):
            copies.append(pltpu.make_async_remote_copy(
                src_ref=ins[a].at[2 * chip[0] + chip[1]], dst_ref=outs[a].at[k],
                send_sem=send_sems.at[a * 3 + k], recv_sem=recv_sems.at[a * 3 + k],
                device_id=(*chip, c), device_id_type=MESH))

    def start():
        for cp in copies:
            cp.start()

    def end():
        for cp in copies:
            cp.wait()

    return start, end


def _gather_small_steps(small_ref, small_out, ssend, srecv, local_sem):
    x, y, c = _mesh_pos()
    me = 4 * x + 2 * y + c
    copies = []
    for r in range(1, NDEV):
        peer = (1 - x if r & 4 else x, 1 - y if r & 2 else y, 1 - c if r & 1 else c)
        copies.append(pltpu.make_async_remote_copy(
            src_ref=small_ref, dst_ref=small_out.at[me],
            send_sem=ssend.at[r - 1], recv_sem=srecv.at[r - 1],
            device_id=peer, device_id_type=MESH))
    own = pltpu.make_async_copy(small_ref, small_out.at[me], local_sem)

    def start():
        own.start()
        for cp in copies:
            cp.start()

    def end():
        for cp in copies:
            cp.wait()
        own.wait()

    return start, end


def _gather_small(small):
    def body(small_ref, small_out, ssend, srecv, local_sem):
        start, end = _gather_small_steps(small_ref, small_out, ssend, srecv, local_sem)
        start()
        end()

    any_spec = pl.BlockSpec(memory_space=pl.ANY)
    return pl.pallas_call(
        body, name="gather_small",
        out_shape=jax.ShapeDtypeStruct((NDEV,) + small.shape, F32),
        in_specs=[any_spec], out_specs=any_spec,
        scratch_shapes=[pltpu.SemaphoreType.DMA((NDEV - 1,)), pltpu.SemaphoreType.DMA((NDEV - 1,)),
                        pltpu.SemaphoreType.DMA],
    )(small)


def _shard_tiles(a):
    rows, cols = SHARD_SHAPES[a]
    tr = min(rows, 256)
    return (tr, cols), rows // tr


def _full_index(a, d, i):
    (tr, _), nt = _shard_tiles(a)
    if a in (0, 2):
        return (i, d)
    return (d * nt + i, 0)


def _cast_bf16(x, name):
    rows, cols = x.shape
    tr = min(rows, 256)

    def body(x_ref, o_ref):
        o_ref[...] = x_ref[...].astype(BF16)

    return pl.pallas_call(
        body, name=name, out_shape=jax.ShapeDtypeStruct(x.shape, BF16), grid=(rows // tr,),
        in_specs=[pl.BlockSpec((tr, cols), lambda i: (i, 0))],
        out_specs=pl.BlockSpec((tr, cols), lambda i: (i, 0)),
        compiler_params=_cp(("parallel",)),
    )(x)


def _chip_partials(a, g_full, r1, core):
    tile, nt = _shard_tiles(a)

    def body(c_ref, g_ref, r_ref, o_ref):
        o_ref[0] = (g_ref[...] + r_ref[0].astype(F32)).astype(BF16)

    grid_spec = pltpu.PrefetchScalarGridSpec(
        num_scalar_prefetch=1, grid=(4, nt),
        in_specs=[pl.BlockSpec(tile, lambda q, i, c: _full_index(a, 2 * q + c[0], i)),
                  pl.BlockSpec((1,) + tile, lambda q, i, c: (q, i, 0))],
        out_specs=pl.BlockSpec((1,) + tile, lambda q, i, c: (q, i, 0)))
    return pl.pallas_call(
        body, name=f"chip_partials_{a}", grid_spec=grid_spec,
        out_shape=jax.ShapeDtypeStruct((4,) + SHARD_SHAPES[a], BF16),
        compiler_params=_cp(("parallel", "parallel")),
    )(core, g_full, r1)


def _adam(w, g, m, v):
    m = ADAM_B1 * m + (1.0 - ADAM_B1) * g
    v = ADAM_B2 * v + (1.0 - ADAM_B2) * (g * g)
    m_hat = m / (1.0 - ADAM_B1 ** ADAM_STEP)
    v_hat = v / (1.0 - ADAM_B2 ** ADAM_STEP)
    delta = -ADAM_LR * (m_hat / (jnp.sqrt(v_hat) + ADAM_EPS) + ADAM_WD * w)
    return delta, m, v


def _reduce_and_update(a, w, m, v, g_full, r1, r2, pos):
    tile, nt = _shard_tiles(a)

    def body(p_ref, w_ref, m_ref, v_ref, g_ref, r1_ref, r2_ref, go_ref, do_ref, mo_ref, vo_ref):
        g = g_ref[...] + r1_ref[0].astype(F32)
        g = g + r2_ref[0].astype(F32)
        g = g + r2_ref[1].astype(F32)
        g = g + r2_ref[2].astype(F32)
        delta, m_new, v_new = _adam(w_ref[...], g, m_ref[...], v_ref[...])
        go_ref[...] = g
        do_ref[...] = delta
        mo_ref[...] = m_new
        vo_ref[...] = v_new

    own = pl.BlockSpec(tile, lambda i, p: (i, 0))
    grid_spec = pltpu.PrefetchScalarGridSpec(
        num_scalar_prefetch=1, grid=(nt,),
        in_specs=[own, own, own,
                  pl.BlockSpec(tile, lambda i, p: _full_index(a, p[0], i)),
                  pl.BlockSpec((1,) + tile, lambda i, p: (p[1], i, 0)),
                  pl.BlockSpec((3,) + tile, lambda i, p: (0, i, 0))],
        out_specs=[own] * 4)
    shp = jax.ShapeDtypeStruct(w.shape, F32)
    return pl.pallas_call(
        body, name=f"reduce_update_{a}", grid_spec=grid_spec, out_shape=[shp] * 4,
        compiler_params=_cp(("parallel",)),
    )(pos, w, m, v, g_full, r1, r2)


def _reduce_own_and_update(w, m, v, g_chip, r2):
    tile, nt = _shard_tiles(0)

    def body(w_ref, m_ref, v_ref, g_ref, r2_ref, go_ref, do_ref, mo_ref, vo_ref):
        g = g_ref[...] + r2_ref[0].astype(F32)
        g = g + r2_ref[1].astype(F32)
        delta, m_new, v_new = _adam(w_ref[...], g, m_ref[...], v_ref[...])
        go_ref[...] = g
        do_ref[...] = delta
        mo_ref[...] = m_new
        vo_ref[...] = v_new

    own = pl.BlockSpec(tile, lambda i: (i, 0))
    shp = jax.ShapeDtypeStruct(w.shape, F32)
    return pl.pallas_call(
        body, name="reduce_update_0", grid=(nt,), out_shape=[shp] * 4,
        in_specs=[own, own, own, own, pl.BlockSpec((2,) + tile, lambda i: (0, i, 0))], out_specs=[own] * 4,
        compiler_params=_cp(("parallel",)),
    )(w, m, v, g_chip, r2)


def _small_update(gathered, norm_w, lb_logits, hnw, fnw, moments):
    m_nw, m_lb, m_hn, m_fn, v_nw, v_lb, v_hn, v_fn = moments

    def body(g_ref, nw, lb, hn, fn, mnw, mlb, mhn, mfn, vnw, vlb, vhn, vfn,
             loss_o, g_nw, g_lb, g_hn, g_fn, d_nw, d_lb, d_hn, d_fn,
             mo_nw, mo_lb, mo_hn, mo_fn, vo_nw, vo_lb, vo_hn, vo_fn):
        tot = g_ref[0]
        for d in range(1, NDEV):
            tot = tot + g_ref[d]
        loss_o[...] = tot[4:5, 0:LANES]
        logits = lb[...]
        lbv = jax.nn.sigmoid(logits[0:1] - logits[1:2])
        chain = tot[1:2] * lbv * (1.0 - lbv)
        grads = (tot[0:1], jnp.concatenate([chain, -chain], axis=0), tot[2:3, 0:LANES], tot[3:4])
        outs = ((nw, mnw, vnw, g_nw, d_nw, mo_nw, vo_nw), (lb, mlb, vlb, g_lb, d_lb, mo_lb, vo_lb),
                (hn, mhn, vhn, g_hn, d_hn, mo_hn, vo_hn), (fn, mfn, vfn, g_fn, d_fn, mo_fn, vo_fn))
        for g, (w_r, m_r, v_r, g_o, d_o, m_o, v_o) in zip(grads, outs):
            delta, m_new, v_new = _adam(w_r[...], g, m_r[...], v_r[...])
            g_o[...] = g
            d_o[...] = delta
            m_o[...] = m_new
            v_o[...] = v_new

    shapes = [norm_w.shape, lb_logits.shape, hnw.shape, fnw.shape]
    out_shape = [jax.ShapeDtypeStruct((1, LANES), F32)] + [jax.ShapeDtypeStruct(s, F32) for s in shapes] * 4
    return pl.pallas_call(body, name="small_update", out_shape=out_shape, compiler_params=_cp())(
        gathered, norm_w, lb_logits, hnw, fnw, m_nw, m_lb, m_hn, m_fn, v_nw, v_lb, v_hn, v_fn)


def _block_tri(n, block, upper=False):
    r = lax.broadcasted_iota(jnp.int32, (n, n), 0)
    c = lax.broadcasted_iota(jnp.int32, (n, n), 1)
    keep = (c >= r) if upper else (c <= r)
    return jnp.where(keep & ((r // block) == (c // block)), 1.0, 0.0).astype(BF16)


def _tril_mask(n):
    r = lax.broadcasted_iota(jnp.int32, (n, n), 0)
    c = lax.broadcasted_iota(jnp.int32, (n, n), 1)
    return c <= r


def _chunk_scores(q, k, b, bex, r0, mask):
    parts, qs_l, ks_l, ek_l, eq_l = [], [], [], [], []
    for i in range(CHUNK // SUB):
        ri = slice(r0 + SUB * i, r0 + SUB * (i + 1))
        seen = slice(r0, r0 + SUB * (i + 1))
        base = bex[r0 + SUB * i:r0 + SUB * i + 1]
        eq = jnp.exp(b[ri] - base)
        ek = jnp.exp(jnp.minimum(base - b[seen], EXP_CLAMP))
        ks = k[seen] * ek
        if i + 1 < CHUNK // SUB:
            rest = jnp.zeros((CHUNK - SUB * (i + 1), 128), F32)
            ek, ks = jnp.concatenate([ek, rest], axis=0), jnp.concatenate([ks, rest], axis=0)
        qs = q[ri] * eq
        parts.append(_dot_nt(qs.astype(BF16), ks.astype(BF16)))
        qs_l.append(qs)
        ks_l.append(ks)
        ek_l.append(ek)
        eq_l.append(eq)
    return jnp.where(mask, jnp.concatenate(parts, axis=0), 0.0), qs_l, ks_l, ek_l, eq_l


def _hgrn_cols(hq, hf, hi, lb):
    sg = jax.nn.sigmoid(hf)
    f = lb + (1.0 - lb) * sg
    g = jnp.log(f)
    b = _dot_ones(_block_tri(HBLK, CHUNK), g)
    return _silu(hq), 1.0 - f, g, hi, sg, f, b


GATHER_IDS = (1, 2, 3)


def _hgrn_fwd(z, lbv, hnw, shards):
    ntb, nch = S // HBLK, HBLK // CHUNK
    n = len(GATHER_IDS)

    def body(hq_ref, hf_ref, hi_ref, hg_ref, lb_ref, hnw_ref, s0, s1, s2, o_ref, oa_ref, st_ref, f0, f1, f2,
             state, send_sems, recv_sems, local_sems):
        start, middle, end = _allgather_steps(GATHER_IDS, (s0, s1, s2), (f0, f1, f2), send_sems, recv_sems, local_sems)

        @pl.when(pl.program_id(0) == 0)
        def _():
            state[...] = jnp.zeros_like(state)
            start()

        pl.when(pl.program_id(0) == ntb // 2)(middle)

        q_a, k_a, g_a, v_a, _, _, b_a = _hgrn_cols(hq_ref[...], hf_ref[...], hi_ref[...], lb_ref[...])
        bex_a = b_a - g_a
        eb_a = jnp.exp(b_a)
        mask = _tril_mask(CHUNK)
        hg = hg_ref[...]
        w = hnw_ref[...]
        for h in range(HEADS):
            cols = slice(128 * h, 128 * h + 128)
            q, k, v, b, bex, eb = q_a[:, cols], k_a[:, cols], v_a[:, cols], b_a[:, cols], bex_a[:, cols], eb_a[:, cols]
            st = state[h]
            outs = []
            for c in range(nch):
                r0 = c * CHUNK
                rows = slice(r0, r0 + CHUNK)
                a = _chunk_scores(q, k, b, bex, r0, mask)[0]
                vb = v[rows].astype(BF16)
                b_last = b[r0 + CHUNK - 1:r0 + CHUNK]
                qe = (q[rows] * eb[rows]).astype(BF16)
                outs.append(_dot(a.astype(BF16), vb) + _dot_nt(qe, st.astype(BF16)))
                st_ref[h, c] = st
                ke = (k[rows] * jnp.exp(b_last - b[rows])).astype(BF16)
                st = st * jnp.exp(b_last) + _dot_tn(vb, ke)
            state[h] = st
            o = jnp.concatenate(outs, axis=0)
            o_ref[:, cols] = o
            r = lax.rsqrt(jnp.mean(o * o, axis=-1, keepdims=True) + EPS)
            oa_ref[:, cols] = (o * r * w * _silu(hg[:, cols])).astype(BF16)

        pl.when(pl.program_id(0) == ntb - 1)(end)

    def zcol(j):
        return pl.BlockSpec((HBLK, D), lambda t: (t, j))

    out_blk = pl.BlockSpec((HBLK, D), lambda t: (t, 0))
    any_spec = pl.BlockSpec(memory_space=pl.ANY)
    return pl.pallas_call(
        body, name="hgrn_fwd", grid=(ntb,),
        out_shape=[jax.ShapeDtypeStruct((S, D), F32), jax.ShapeDtypeStruct((S, D), BF16),
                   jax.ShapeDtypeStruct((HEADS, S // CHUNK, 128, 128), F32)]
        + [jax.ShapeDtypeStruct(FULL_SHAPES[a], BF16) for a in GATHER_IDS],
        in_specs=[zcol(0), zcol(1), zcol(2), zcol(3),
                  pl.BlockSpec((1, D), lambda t: (0, 0)), pl.BlockSpec((1, 128), lambda t: (0, 0))] + [any_spec] * n,
        out_specs=[out_blk, out_blk, pl.BlockSpec((HEADS, nch, 128, 128), lambda t: (0, t, 0, 0))] + [any_spec] * n,
        scratch_shapes=[pltpu.VMEM((HEADS, 128, 128), F32), pltpu.SemaphoreType.DMA((7 * n,)),
                        pltpu.SemaphoreType.DMA((7 * n,)), pltpu.SemaphoreType.DMA((n,))],
        compiler_params=_cp(("arbitrary",)),
    )(z, z, z, z, lbv, hnw, *shards)


def _half_mask():
    lane = lax.broadcasted_iota(jnp.int32, (1, LANES), 1)
    return (lane % 64) < 32


def _rope(t, cc, ss, first_half):
    partner = jnp.where(first_half, pltpu.roll(t, 96, 1), pltpu.roll(t, 32, 1))
    return t * cc + partner * ss


def _attn_masks():
    i = lax.broadcasted_iota(jnp.int32, (128, 128), 0)
    j = lax.broadcasted_iota(jnp.int32, (128, 128), 1)
    return j >= i, j <= i


def _to_residues_dyn(g, dst, src, row0=0, dtype=None):
    for gi, dil in enumerate((1, 4, 16)):
        m = S // dil

        @pl.when(g == gi)
        def _(dil=dil, m=m):
            for r in range(dil):
                v = src[...] if dil == 1 else src[pl.ds(r, m, stride=dil), :]
                if dtype is not None:
                    v = v.astype(dtype)
                dst[row0 + r * m:row0 + (r + 1) * m, 0:LANES] = v


def _from_residues_dyn(g, dst, src, row0=0):
    for gi, dil in enumerate((1, 4, 16)):
        m = S // dil

        @pl.when(g == gi)
        def _(dil=dil, m=m):
            for r in range(dil):
                v = src[row0 + r * m:row0 + (r + 1) * m, :]
                if dil == 1:
                    dst[...] = v
                else:
                    dst[pl.ds(r, m, stride=dil), :] = v


def _group_blocks(g):
    return jnp.where(g == 0, 16, jnp.where(g == 1, 4, 1))


def _attn_in_specs(extra):
    def zcol(off):
        return pl.BlockSpec((S, LANES), lambda p, g: (0, off + 4 * g + p))

    per_pair = pl.BlockSpec((S, LANES), lambda p, g: (0, p))
    const = pl.BlockSpec((S, LANES), lambda p, g: (0, 0))
    return [zcol(32), zcol(44), zcol(56), pl.BlockSpec((S, LANES), lambda p, g: (0, 68 + p)), const, const] + [per_pair] * extra


def _attn_fwd(z, cc, ss):
    def body(q_ref, k_ref, v_ref, ag_ref, cc_ref, ss_ref, ob_ref, lse_ref, obg_ref, qsb_ref, ksb_ref, vsb_ref,
             tmp, qs, ks, vx, og, mg, lg, o_t, m_t, l_t, o_acc, m_acc, l_acc):
        g = pl.program_id(1)
        first_half = _half_mask()
        prev_ok, cur_ok = _attn_masks()
        lane = lax.broadcasted_iota(jnp.int32, (1, LANES), 1)
        heads = (lane < 64, lane >= 64)
        nblk = _group_blocks(g)

        @pl.when(g == 0)
        def _():
            ks[0:ATT_PAD, :] = jnp.zeros((ATT_PAD, LANES), BF16)
            vx[0:ATT_PAD, 0:LANES] = jnp.zeros((ATT_PAD, LANES), BF16)
            vx[:, LANES:2 * LANES] = jnp.ones((ATT_PAD + S, LANES), BF16)

        tmp[...] = _rope(q_ref[...], cc_ref[...], ss_ref[...], first_half) * ATT_SCALE
        _to_residues_dyn(g, qs, tmp)
        tmp[...] = _rope(k_ref[...], cc_ref[...], ss_ref[...], first_half)
        _to_residues_dyn(g, ks, tmp, ATT_PAD, BF16)
        _to_residues_dyn(g, vx, v_ref, ATT_PAD, BF16)

        def unit(u, carry):
            start = pl.multiple_of(u * 128, 128)
            cur = pl.ds(start, 128)
            pm = prev_ok & ((u & (nblk - 1)) != 0)
            qu = qs[cur, :]
            kcat = ks[pl.ds(start, 256), :]
            vext = vx[pl.ds(start, 256), :]
            o_u = m_u = l_u = None
            for hh in range(2):
                s = _dot_nt(jnp.where(heads[hh], qu, 0.0).astype(BF16), kcat)
                sp = jnp.where(pm, s[:, 0:128], -jnp.inf)
                sc = jnp.where(cur_ok, s[:, 128:256], -jnp.inf)
                m = jnp.max(jnp.maximum(sp, sc), axis=-1, keepdims=True)
                p = jnp.concatenate([jnp.exp(sp - m), jnp.exp(sc - m)], axis=1).astype(BF16)
                ol = _dot(p, vext)
                mb = jnp.broadcast_to(m, (128, LANES))
                if hh == 0:
                    o_u, l_u, m_u = ol[:, 0:128], ol[:, 128:256], mb
                else:
                    o_u = jnp.where(heads[1], ol[:, 0:128], o_u)
                    l_u = jnp.where(heads[1], ol[:, 128:256], l_u)
                    m_u = jnp.where(heads[1], mb, m_u)
            og[cur, :] = o_u
            mg[cur, :] = m_u
            lg[cur, :] = l_u
            return carry

        lax.fori_loop(0, 16, unit, 0, unroll=16)
        qsb_ref[0] = qs[...].astype(BF16)
        ksb_ref[0] = ks[...]
        vsb_ref[0] = vx[:, 0:LANES]
        _from_residues_dyn(g, o_t, og)
        _from_residues_dyn(g, m_t, mg)
        _from_residues_dyn(g, l_t, lg)

        @pl.when(g == 0)
        def _():
            o_acc[...] = o_t[...]
            m_acc[...] = m_t[...]
            l_acc[...] = l_t[...]

        @pl.when(g > 0)
        def _():
            m_new = jnp.maximum(m_acc[...], m_t[...])
            wa, wb = jnp.exp(m_acc[...] - m_new), jnp.exp(m_t[...] - m_new)
            o_acc[...] = o_acc[...] * wa + o_t[...] * wb
            l_acc[...] = l_acc[...] * wa + l_t[...] * wb
            m_acc[...] = m_new

        @pl.when(g == 2)
        def _():
            ob = o_acc[...] / l_acc[...]
            ob_ref[...] = ob
            lse_ref[...] = m_acc[...] + jnp.log(l_acc[...])
            obg_ref[...] = (ob * _silu(ag_ref[...])).astype(BF16)

    blk = pl.BlockSpec((S, LANES), lambda p, g: (0, p))
    buf = pltpu.VMEM((S, LANES), F32)
    return pl.pallas_call(
        body, name="attn_fwd", grid=(4, 3),
        out_shape=[jax.ShapeDtypeStruct((S, 512), F32), jax.ShapeDtypeStruct((S, 512), F32),
                   jax.ShapeDtypeStruct((S, 512), BF16), jax.ShapeDtypeStruct((3, S, 512), BF16),
                   jax.ShapeDtypeStruct((3, ATT_PAD + S, 512), BF16), jax.ShapeDtypeStruct((3, ATT_PAD + S, 512), BF16)],
        in_specs=_attn_in_specs(0),
        out_specs=[blk, blk, blk, pl.BlockSpec((1, S, LANES), lambda p, g: (g, 0, p)),
                   pl.BlockSpec((1, ATT_PAD + S, LANES), lambda p, g: (g, 0, p)),
                   pl.BlockSpec((1, ATT_PAD + S, LANES), lambda p, g: (g, 0, p))],
        scratch_shapes=[buf, buf, pltpu.VMEM((ATT_PAD + S, LANES), BF16), pltpu.VMEM((ATT_PAD + S, 2 * LANES), BF16)] + [buf] * 9,
        compiler_params=_cp(("parallel", "arbitrary")),
    )(z, z, z, z, cc, ss)


def _tail(x, o_a, o_bg, z, target, w_a, w_b, w_out, fnw):
    tm = 256

    def body(x_ref, oa_ref, ob_ref, gpa_ref, gpb_ref, t_ref, wa_ref, wb_ref, wo_ref, fnw_ref,
             dx2_ref, dx2b_ref, dz_hbm, doa_ref, dob_ref, mg_ref, dya_ref, dyb_ref, small_ref, dgp, dgp_sem):
        step = pl.program_id(0)
        slot = step % 2

        def dgp_copy(at_step, at_slot):
            return pltpu.make_async_copy(
                dgp.at[at_slot], dz_hbm.at[pl.ds(pl.multiple_of(at_step * tm, tm), tm), pl.ds(DZ_GATES, 2 * D)],
                dgp_sem.at[at_slot])

        @pl.when(step == 0)
        def _():
            small_ref[...] = jnp.zeros_like(small_ref)

        @pl.when(step >= 2)
        def _():
            dgp_copy(step - 2, slot).wait()

        wa, wb, wo = wa_ref[...], wb_ref[...], wo_ref[...]
        y_a = _dot(oa_ref[...], wa)
        y_b = _dot(ob_ref[...], wb)
        ga = jax.nn.sigmoid(gpa_ref[...])
        gb = jax.nn.sigmoid(gpb_ref[...])
        merged = (ga * y_a + gb * y_b).astype(BF16)
        x2 = x_ref[...] + _dot(merged, wo)
        r2 = lax.rsqrt(jnp.mean(x2 * x2, axis=-1, keepdims=True) + EPS)
        n2 = x2 * r2
        fw = fnw_ref[...]
        err = n2 * fw - t_ref[...]
        loss = 0.5 * jnp.sum(jnp.sum(err * err, axis=-1, keepdims=True), axis=0, keepdims=True) / D
        dy = err * (1.0 / D)
        g_fnw = jnp.sum(dy * n2, axis=0, keepdims=True)
        dn = dy * fw
        dx2 = r2 * (dn - n2 * jnp.mean(dn * n2, axis=-1, keepdims=True))
        dx2b = dx2.astype(BF16)
        dmerged = _dot_nt(dx2b, wo)
        dy_a = (dmerged * ga).astype(BF16)
        dy_b = (dmerged * gb).astype(BF16)
        dx2_ref[...] = dx2
        dx2b_ref[...] = dx2b
        dgp[slot, :, 0:D] = (dmerged * y_a * ga * (1.0 - ga)).astype(BF16)
        dgp[slot, :, D:2 * D] = (dmerged * y_b * gb * (1.0 - gb)).astype(BF16)
        dgp_copy(step, slot).start()
        doa_ref[...] = _dot_nt(dy_a, wa)
        dob_ref[...] = _dot_nt(dy_b, wb)
        mg_ref[...] = merged
        dya_ref[...] = dy_a
        dyb_ref[...] = dy_b
        small_ref[0:1, :] += g_fnw
        small_ref[1:2, :] += jnp.broadcast_to(loss, (1, D))

        @pl.when(step == S // tm - 1)
        def _():
            dgp_copy(step - 1, 1 - slot).wait()
            dgp_copy(step, slot).wait()

    def rows(cols, off=0):
        return pl.BlockSpec((tm, cols), lambda i: (i, off))

    def whole(shape):
        return pl.BlockSpec(shape, lambda i: (0, 0))

    return pl.pallas_call(
        body, name="tail", grid=(S // tm,),
        out_shape=[jax.ShapeDtypeStruct((S, D), F32), jax.ShapeDtypeStruct((S, D), BF16),
                   jax.ShapeDtypeStruct((S, IN_COLS), BF16), jax.ShapeDtypeStruct((S, D), F32),
                   jax.ShapeDtypeStruct((S, 512), F32), jax.ShapeDtypeStruct((S, D), BF16),
                   jax.ShapeDtypeStruct((S, D), BF16), jax.ShapeDtypeStruct((S, D), BF16),
                   jax.ShapeDtypeStruct((8, D), F32)],
        in_specs=[rows(D), rows(D), rows(512), rows(D, 9), rows(D, 10), rows(D),
                  whole((D, D)), whole((512, D)), whole((D, D)), whole((1, D))],
        out_specs=[rows(D), rows(D), pl.BlockSpec(memory_space=pl.ANY), rows(D), rows(512), rows(D), rows(D),
                   rows(D), whole((8, D))],
        scratch_shapes=[pltpu.VMEM((2, tm, 2 * D), BF16), pltpu.SemaphoreType.DMA((2,))],
        compiler_params=_cp(("arbitrary",)),
    )(x, o_a, o_bg, z, z, target, w_a, w_b, w_out, fnw)


def _tn_matmul(a, b, name):
    m, n = a.shape[1], b.shape[1]
    tn = 512

    def body(a_ref, b_ref, o_ref, ob_ref):
        acc = _dot_tn(a_ref[...], b_ref[...])
        o_ref[...] = acc
        ob_ref[...] = acc.astype(BF16)

    out_blk = pl.BlockSpec((m, tn), lambda j: (0, j))
    return pl.pallas_call(
        body, name=name, grid=(n // tn,),
        out_shape=[jax.ShapeDtypeStruct((m, n), F32), jax.ShapeDtypeStruct((m, n), BF16)],
        in_specs=[pl.BlockSpec((S, m), lambda j: (0, 0)), pl.BlockSpec((S, tn), lambda j: (0, j))],
        out_specs=[out_blk, out_blk],
        compiler_params=_cp(("parallel",)),
    )(a, b)


def _hgrn_bwd(z, o, do_a, states, lbv, hnw, partials, dz):
    ntb, nch = S // HBLK, HBLK // CHUNK
    n = len(GATHER_IDS)

    def body(hq_ref, hf_ref, hi_ref, hg_ref, o_ref, doa_ref, st_ref, lb_ref, hnw_ref, p0, p1, p2, dz_in,
             dz_ref, glb_ref, ghn_ref, e0, e1, e2, dstate, send_sems, recv_sems):
        dhq_ref, dhf_ref, dhi_ref, dhg_ref = (dz_ref.at[:, pl.ds(j * D, D)] for j in range(4))
        start, end = _exchange_chips_steps((p0, p1, p2), (e0, e1, e2), send_sems, recv_sems)

        @pl.when(pl.program_id(0) == 0)
        def _():
            dstate[...] = jnp.zeros_like(dstate)
            glb_ref[...] = jnp.zeros_like(glb_ref)
            ghn_ref[...] = jnp.zeros_like(ghn_ref)
            start()

        lb_a = lb_ref[...]
        hq_a, hg_a = hq_ref[...], hg_ref[...]
        q_a, k_a, g_a, v_a, sg_a, f_a, b_a = _hgrn_cols(hq_a, hf_ref[...], hi_ref[...], lb_a)
        bex_a = b_a - g_a
        eb_a = jnp.exp(b_a)
        w = hnw_ref[...]
        mask = _tril_mask(CHUNK)
        upper = _block_tri(CHUNK, CHUNK, upper=True)
        for h in range(HEADS):
            cols = slice(128 * h, 128 * h + 128)
            q, k, v, b, bex, eb = q_a[:, cols], k_a[:, cols], v_a[:, cols], b_a[:, cols], bex_a[:, cols], eb_a[:, cols]
            hq, hg, sg, f, lb = hq_a[:, cols], hg_a[:, cols], sg_a[:, cols], f_a[:, cols], lb_a[:, cols]
            ov, doa = o_ref[:, cols], doa_ref[:, cols]
            r = lax.rsqrt(jnp.mean(ov * ov, axis=-1, keepdims=True) + EPS)
            n = ov * r
            sil = _silu(hg)
            dhg_ref[:, cols] = (doa * n * w * _dsilu(hg)).astype(BF16)
            ghn_ref[h] += jnp.sum(doa * sil * n, axis=0, keepdims=True)
            dn = doa * sil * w
            do = r * (dn - n * jnp.mean(dn * n, axis=-1, keepdims=True))

            dst = dstate[h]
            dq_l, dk_l, dv_l, dg_l = [None] * nch, [None] * nch, [None] * nch, [None] * nch
            for c in reversed(range(nch)):
                r0 = c * CHUNK
                rows = slice(r0, r0 + CHUNK)
                st = st_ref[h, c]
                bc, kc, qc = b[rows], k[rows], q[rows]
                vb, dob = v[rows].astype(BF16), do[rows].astype(BF16)
                b_last = bc[CHUNK - 1:CHUNK]
                e_last = jnp.exp(b_last)
                ekl = jnp.exp(b_last - bc)
                dstb = dst.astype(BF16)
                a, qs_l, ks_l, ek_l, eq_l = _chunk_scores(q, k, b, bex, r0, mask)
                da = jnp.where(mask, _dot_nt(dob, vb), 0.0)
                dv_l[c] = _dot_tn(a.astype(BF16), dob) + _dot_nt((kc * ekl).astype(BF16), dstb)
                dq_inter = _dot(dob, st.astype(BF16)) * eb[rows]
                dk_state = _dot(vb, dstb) * ekl
                dq_parts, dk_intra = [], jnp.zeros((CHUNK, 128), F32)
                dab = da.astype(BF16)
                for i in range(CHUNK // SUB):
                    da_i = dab[SUB * i:SUB * (i + 1)]
                    ks_hi, ks_lo = _split2(ks_l[i])
                    qs_hi, qs_lo = _split2(qs_l[i])
                    dq_parts.append((_dot(da_i, ks_hi) + _dot(da_i, ks_lo)) * eq_l[i])
                    dk_intra = dk_intra + (_dot_tn(da_i, qs_hi) + _dot_tn(da_i, qs_lo)) * ek_l[i]
                dq = jnp.concatenate(dq_parts, axis=0) + dq_inter
                dk = dk_intra + dk_state
                last = (e_last * jnp.sum(st * dst, axis=0, keepdims=True)
                        + jnp.sum(kc * dk_state, axis=0, keepdims=True))
                dg_l[c] = _dot_ones(upper, qc * dq - kc * dk) + last
                dq_l[c], dk_l[c] = dq, dk
                dst = dst * e_last + _dot_tn(dob, (qc * eb[rows]).astype(BF16))
            dstate[h] = dst
            dq, dk = jnp.concatenate(dq_l, axis=0), jnp.concatenate(dk_l, axis=0)
            dg, dv = jnp.concatenate(dg_l, axis=0), jnp.concatenate(dv_l, axis=0)
            dhq_ref[:, cols] = (dq * _dsilu(hq)).astype(BF16)
            dhi_ref[:, cols] = dv.astype(BF16)
            df = dg / f - dk
            dhf_ref[:, cols] = (df * (1.0 - lb) * sg * (1.0 - sg)).astype(BF16)
            glb_ref[:, cols] += jnp.sum(df * (1.0 - sg), axis=0, keepdims=True)

        pl.when(pl.program_id(0) == ntb - 1)(end)

    def rev(t):
        return ntb - 1 - t

    def zcol(j):
        return pl.BlockSpec((HBLK, D), lambda t: (rev(t), j))

    blk = pl.BlockSpec((HBLK, D), lambda t: (rev(t), 0))
    any_spec = pl.BlockSpec(memory_space=pl.ANY)
    return pl.pallas_call(
        body, name="hgrn_bwd", grid=(ntb,),
        out_shape=[jax.ShapeDtypeStruct((S, IN_COLS), BF16)]
        + [jax.ShapeDtypeStruct((1, D), F32), jax.ShapeDtypeStruct((HEADS, 1, 128), F32)]
        + [jax.ShapeDtypeStruct((3,) + SHARD_SHAPES[a], BF16) for a in GATHER_IDS],
        in_specs=[zcol(0), zcol(1), zcol(2), zcol(3), blk, blk,
                  pl.BlockSpec((HEADS, nch, 128, 128), lambda t: (0, rev(t), 0, 0)),
                  pl.BlockSpec((1, D), lambda t: (0, 0)), pl.BlockSpec((1, 128), lambda t: (0, 0))]
        + [any_spec] * (n + 1),
        out_specs=[pl.BlockSpec((HBLK, DZ_ATT), lambda t: (rev(t), 0)), pl.BlockSpec((1, D), lambda t: (0, 0)),
                   pl.BlockSpec((HEADS, 1, 128), lambda t: (0, 0, 0))] + [any_spec] * n,
        scratch_shapes=[pltpu.VMEM((HEADS, 128, 128), F32), pltpu.SemaphoreType.DMA((3 * n,)),
                        pltpu.SemaphoreType.DMA((3 * n,))],
        input_output_aliases={9 + n: 0},
        compiler_params=_cp(("arbitrary",)),
    )(z, z, z, z, o, do_a, states, lbv, hnw, *partials, dz)


def _attn_bwd(z, qsb, ksb, vsb, cc, ss, ob, lse, do_bg, dz):
    def body(qs, ks, vs, ag_ref, cc_ref, ss_ref, ob_ref, lse_ref, dobg_ref, dz_in, dz_hbm,
             tmp, dos, dqs, dks, dvs, dkp, dvp, do_t, ls0_t, ls1_t, dl0_t, dl1_t, ls0, ls1, dl0, dl1,
             stage, stage_sem):
        pair, g = pl.program_id(0), pl.program_id(1)

        def out_copy(j):
            tile = DZ_ATT // LANES + (36 + pair if j == 3 else 12 * j + 4 * g + pair)
            return pltpu.make_async_copy(
                stage.at[j], dz_hbm.at[:, pl.ds(pl.multiple_of(tile * LANES, LANES), LANES)], stage_sem.at[j])

        def restage(j, value):
            pl.when(pair * 3 + g > 0)(lambda: out_copy(j).wait())
            stage[j] = value
            out_copy(j).start()

        pl.when(g == 2)(lambda: out_copy(3).wait())
        first_half = _half_mask()
        prev_ok, cur_ok = _attn_masks()
        lane = lax.broadcasted_iota(jnp.int32, (1, LANES), 1)
        heads = (lane < 64, lane >= 64)
        nblk = _group_blocks(g)
        cc_v, ss_v = cc_ref[...], ss_ref[...]

        @pl.when(g == 0)
        def _():
            ag, obv, dobg = ag_ref[...], ob_ref[...], dobg_ref[...]
            stage[3] = (dobg * obv * _dsilu(ag)).astype(BF16)
            out_copy(3).start()
            dob = dobg * _silu(ag)
            do_t[...] = dob
            prod = dob * obv
            dl = jnp.concatenate(
                [jnp.broadcast_to(jnp.sum(prod[:, 0:64], axis=-1, keepdims=True), (S, 64)),
                 jnp.broadcast_to(jnp.sum(prod[:, 64:128], axis=-1, keepdims=True), (S, 64))], axis=1)
            dl_sw = pltpu.roll(dl, 64, 1)
            dl0_t[...] = jnp.where(heads[0], dl, dl_sw)
            dl1_t[...] = jnp.where(heads[0], dl_sw, dl)
            ls = lse_ref[...]
            ls_sw = pltpu.roll(ls, 64, 1)
            ls0_t[...] = jnp.where(heads[0], ls, ls_sw)
            ls1_t[...] = jnp.where(heads[0], ls_sw, ls)

        _to_residues_dyn(g, dos, do_t)
        _to_residues_dyn(g, ls0, ls0_t)
        _to_residues_dyn(g, ls1, ls1_t)
        _to_residues_dyn(g, dl0, dl0_t)
        _to_residues_dyn(g, dl1, dl1_t)
        lss, dls = (ls0, ls1), (dl0, dl1)

        def unit(u, carry):
            start = pl.multiple_of(u * 128, 128)
            cur = pl.ds(start, 128)
            both = pl.ds(start, 256)
            pm = prev_ok & ((u & (nblk - 1)) != 0)
            qu, dou = qs[0, cur, :], dos[cur, :]
            kcat, vcat = ks[0, both, :], vs[0, both, :]
            dq_u = None
            q_l, do_l, ds_l, p_l = [], [], [], []
            for hh in range(2):
                q_h = jnp.where(heads[hh], qu, jnp.zeros((), BF16))
                do_h = jnp.where(heads[hh], dou, 0.0).astype(BF16)
                s = _dot_nt(q_h, kcat)
                dp = _dot_nt(do_h, vcat)
                lse_h, dl_h = lss[hh][cur, :], dls[hh][cur, :]
                pp = jnp.where(pm, jnp.exp(s[:, 0:128] - lse_h), 0.0)
                pc = jnp.where(cur_ok, jnp.exp(s[:, 128:256] - lse_h), 0.0)
                ds = jnp.concatenate([pp * (dp[:, 0:128] - dl_h), pc * (dp[:, 128:256] - dl_h)], axis=1).astype(BF16)
                dq = _dot(ds, kcat)
                dq_u = dq if hh == 0 else jnp.where(heads[1], dq, dq_u)
                q_l.append(q_h)
                do_l.append(do_h)
                ds_l.append(ds)
                p_l.append(jnp.concatenate([pp, pc], axis=1).astype(BF16))
            dkcat = _dot_tn(jnp.concatenate(ds_l, axis=0), jnp.concatenate(q_l, axis=0))
            dvcat = _dot_tn(jnp.concatenate(p_l, axis=0), jnp.concatenate(do_l, axis=0))
            dkp[cur, :] = dkcat[0:128]
            dks[cur, :] = dkcat[128:256]
            dvp[cur, :] = dvcat[0:128]
            dvs[cur, :] = dvcat[128:256]
            dqs[cur, :] = dq_u
            return carry

        lax.fori_loop(0, 16, unit, 0, unroll=ATT_UNROLL)
        dks[0:S - 128, :] += dkp[128:S, :]
        dvs[0:S - 128, :] += dvp[128:S, :]
        _from_residues_dyn(g, tmp, dqs)
        restage(0, (_rope(tmp[...], cc_v, -ss_v, first_half) * ATT_SCALE).astype(BF16))
        _from_residues_dyn(g, tmp, dks)
        restage(1, _rope(tmp[...], cc_v, -ss_v, first_half).astype(BF16))
        _from_residues_dyn(g, tmp, dvs)
        restage(2, tmp[...].astype(BF16))

        @pl.when(pair * 3 + g == 11)
        def _():
            for j in range(3):
                out_copy(j).wait()

    any_spec = pl.BlockSpec(memory_space=pl.ANY)
    buf = pltpu.VMEM((S, LANES), F32)
    padded_b = pltpu.VMEM((ATT_PAD + S, LANES), BF16)
    return pl.pallas_call(
        body, name="attn_bwd", grid=(4, 3),
        out_shape=jax.ShapeDtypeStruct((S, IN_COLS), BF16),
        in_specs=[pl.BlockSpec((1, S, LANES), lambda p, g: (g, 0, p)),
                  pl.BlockSpec((1, ATT_PAD + S, LANES), lambda p, g: (g, 0, p)),
                  pl.BlockSpec((1, ATT_PAD + S, LANES), lambda p, g: (g, 0, p))] + _attn_in_specs(3)[3:] + [any_spec],
        out_specs=any_spec,
        scratch_shapes=[buf] * 16 + [pltpu.VMEM((4, S, LANES), BF16), pltpu.SemaphoreType.DMA((4,))],
        input_output_aliases={9: 0},
        compiler_params=_cp(("arbitrary", "arbitrary")),
    )(qsb, ksb, vsb, z, cc, ss, ob, lse, do_bg, dz)


def _in_proj_bwd(dz, h, w_in):
    half = S // 2
    slab = (D, SHARD_COLS)

    def body(dz_hbm, h_hbm, w_hbm, dh_hbm, g_chip, r1_hbm, relay_hbm, r2_hbm,
             h_buf, dz_buf, stage_d, r1_buf, stage_i, acc,
             dz_sem, w_sem, h_sem, r1_sem, out_sem, send_d, recv_d, send_i, recv_i):
        x, y, c = _mesh_pos()
        sibling = (x, y, 1 - c)
        north = c == 1
        near = (jnp.where(north, 1 - x, x), jnp.where(north, y, 1 - y))
        far = (jnp.where(north, x, 1 - x), jnp.where(north, 1 - y, y))
        chips = [(1 - x, 1 - y), near, far, (x, y)]

        def cols(d):
            return pl.ds(pl.multiple_of(d * SHARD_COLS, LANES), SHARD_COLS)

        blocks = []
        for q_sib, q in zip([chips[0], far, near, chips[3]], chips):
            blocks += [4 * q_sib[0] + 2 * q_sib[1] + (1 - c), 4 * q[0] + 2 * q[1] + c]

        def dz_tile(t):
            return _SplitCopy(dz_hbm.at[pl.ds((t % 2) * half, half), cols(blocks[t // 2])],
                                         dz_buf.at[t % 2], dz_sem.at[t % 2])

        def to_sibling(i):
            return pltpu.make_async_remote_copy(
                src_ref=stage_d.at[i % 2], dst_ref=r1_hbm.at[i], send_sem=send_d.at[i], recv_sem=recv_d.at[i],
                device_id=sibling, device_id_type=MESH)

        def to_owner(i):
            dst = relay_hbm if i == 0 else r2_hbm.at[i - 1]
            return pltpu.make_async_remote_copy(
                src_ref=stage_i.at[i], dst_ref=dst, send_sem=send_i.at[i], recv_sem=recv_i.at[i],
                device_id=(*(far if i == 2 else near), c), device_id_type=MESH)

        h_copy = _SplitCopy(h_hbm, h_buf, h_sem)
        h_copy.start()
        dz_tile(0).start()
        h_copy.wait()
        for b in range(8):
            i = b // 2
            g = None
            for r in range(2):
                t = 2 * b + r
                if t + 1 < 16:
                    dz_tile(t + 1).start()
                dz_tile(t).wait()
                part = _dot_tn(h_buf[r * half:(r + 1) * half, :], dz_buf[t % 2])
                g = part if g is None else g + part
                if b % 2 == 1 and r == 0:
                    to_sibling(i).wait_recv()
                    r1_copy = _SplitCopy(r1_hbm.at[i], r1_buf, r1_sem)
                    r1_copy.start()
            if b % 2 == 0:
                if i >= 2:
                    to_sibling(i - 2).wait_send()
                stage_d[i % 2] = g.astype(BF16)
                to_sibling(i).start()
            else:
                r1_copy.wait()
                g = g + r1_buf[...].astype(F32)
                if i == 2:
                    to_owner(0).wait_recv()
                    relay_copy = _SplitCopy(relay_hbm, r1_buf, r1_sem)
                    relay_copy.start()
                    relay_copy.wait()
                    g = g + r1_buf[...].astype(F32)
                if i < 3:
                    stage_i[i] = g.astype(BF16)
                    to_owner(i).start()
                else:
                    g_chip[...] = g
        to_sibling(2).wait_send()
        to_sibling(3).wait_send()

        def dz2(t):
            return _SplitCopy(
                dz_hbm.at[pl.ds((t % 2) * half, half), pl.ds((t // 2) * SHARD_COLS, SHARD_COLS)],
                dz_buf.at[t % 2], dz_sem.at[t % 2])

        def w2(b):
            return _SplitCopy(w_hbm.at[:, pl.ds(b * SHARD_COLS, SHARD_COLS)],
                                         stage_d.at[b % 2], w_sem.at[b % 2])

        dz2(0).start()
        w2(0).start()
        for t in range(16):
            b, r = t // 2, t % 2
            if t + 1 < 16:
                dz2(t + 1).start()
            if r == 0:
                if b + 1 < 8:
                    w2(b + 1).start()
                w2(b).wait()
            dz2(t).wait()
            part = _dot_nt(dz_buf[t % 2], stage_d[b % 2])
            if b == 0:
                acc[r] = part
            else:
                acc[r] += part
        dh_out = [_SplitCopy(acc.at[r], dh_hbm.at[pl.ds(r * half, half), :], out_sem.at[r])
                  for r in range(2)]
        for cp in dh_out:
            cp.start()
        for cp in dh_out:
            cp.wait()
        for i in range(3):
            to_owner(i).wait_send()
        for i in (1, 2):
            to_owner(i).wait_recv()

    any_spec = pl.BlockSpec(memory_space=pl.ANY)
    return pl.pallas_call(
        body, name="in_proj_bwd",
        out_shape=[jax.ShapeDtypeStruct((S, D), F32), jax.ShapeDtypeStruct(slab, F32),
                   jax.ShapeDtypeStruct((4,) + slab, BF16), jax.ShapeDtypeStruct(slab, BF16),
                   jax.ShapeDtypeStruct((2,) + slab, BF16)],
        in_specs=[any_spec] * 3,
        out_specs=[any_spec, pl.BlockSpec(memory_space=pltpu.VMEM), any_spec, any_spec, any_spec],
        scratch_shapes=[pltpu.VMEM((S, D), BF16), pltpu.VMEM((2, half, SHARD_COLS), BF16),
                        pltpu.VMEM((2,) + slab, BF16), pltpu.VMEM(slab, BF16), pltpu.VMEM((3,) + slab, BF16),
                        pltpu.VMEM((2, half, D), F32),
                        pltpu.SemaphoreType.DMA((2,)), pltpu.SemaphoreType.DMA((2,)), pltpu.SemaphoreType.DMA,
                        pltpu.SemaphoreType.DMA, pltpu.SemaphoreType.DMA((2,)),
                        pltpu.SemaphoreType.DMA((4,)), pltpu.SemaphoreType.DMA((4,)),
                        pltpu.SemaphoreType.DMA((3,)), pltpu.SemaphoreType.DMA((3,))],
        compiler_params=_cp(),
    )(dz, h, w_in)


def _grad_x(x, norm_w, dh, dx2):
    tr = 256

    def body(x_ref, w_ref, dh_ref, dx2_ref, gx_ref, gnw_ref):
        @pl.when(pl.program_id(0) == 0)
        def _():
            gnw_ref[...] = jnp.zeros_like(gnw_ref)

        xv, dhv = x_ref[...], dh_ref[...]
        r = lax.rsqrt(jnp.mean(xv * xv, axis=-1, keepdims=True) + EPS)
        n = xv * r
        gnw_ref[...] += jnp.sum(dhv * n, axis=0, keepdims=True)
        dn = dhv * w_ref[...]
        gx_ref[...] = dx2_ref[...] + r * (dn - n * jnp.mean(dn * n, axis=-1, keepdims=True))

    row = pl.BlockSpec((tr, D), lambda i: (i, 0))
    vec = pl.BlockSpec((1, D), lambda i: (0, 0))
    return pl.pallas_call(
        body, name="grad_x", grid=(S // tr,),
        out_shape=[jax.ShapeDtypeStruct((S, D), F32), jax.ShapeDtypeStruct((1, D), F32)],
        in_specs=[row, vec, row, row], out_specs=[row, vec],
        compiler_params=_cp(("arbitrary",)),
    )(x, norm_w, dh, dx2)


def _rope_tables(positions):
    inv_freq = 10000.0 ** (-jnp.arange(0, 64, 2, dtype=F32) / 64)
    ang = positions.astype(F32)[:, None] * inv_freq[None, :]
    cos, sin = jnp.cos(ang), jnp.sin(ang)
    return jnp.tile(cos, (1, 4)), jnp.tile(jnp.concatenate([-sin, sin], axis=1), (1, 2))


def _local_step(x, positions, norm_w, lb_logits, hnw, fnw, target, w_in_shard, small_shards, core):
    cc, ss = _rope_tables(positions)
    lbv = jax.nn.sigmoid(lb_logits[0:1] - lb_logits[1:2])
    z, w_in, h = _in_proj_gather(x, norm_w, w_in_shard)
    o, o_a, states, w_a, w_b, w_out = _hgrn_fwd(z, lbv, hnw, small_shards)
    ob, lse, o_bg, qsb, ksb, vsb = _attn_fwd(z, cc, ss)
    dx2, dx2b, dz, do_a, do_bg, merged, dy_a, dy_b, tail_small = _tail(x, o_a, o_bg, z, target, w_a, w_b, w_out, fnw)
    g_out, gb_out = _tn_matmul(merged, dx2b, "grad_w_out")
    g_a, gb_a = _tn_matmul(o_a, dy_a, "grad_w_a")
    g_b, gb_b = _tn_matmul(o_bg, dy_b, "grad_w_b")
    grads, gb = (g_a, g_b, g_out), (gb_a, gb_b, gb_out)
    r1 = _exchange_sibling(GATHER_IDS, gb)
    pb = [_chip_partials(a, grads[i], r1[i], core) for i, a in enumerate(GATHER_IDS)]
    dz, glb, ghn, *r2 = _hgrn_bwd(z, o, do_a, states, lbv, hnw, pb, dz)
    dz = _attn_bwd(z, qsb, ksb, vsb, cc, ss, ob, lse, do_bg, dz)
    dh, g_chip_in, _, _, r2_in = _in_proj_bwd(dz, h, w_in)
    grad_x, gnw = _grad_x(x, norm_w, dh, dx2)
    ghn_row = jnp.pad(jnp.sum(ghn, axis=0), ((0, 0), (0, D - 128)))
    small = jnp.concatenate([gnw, glb, ghn_row, tail_small[0:2], jnp.zeros((3, D), F32)], axis=0)
    return grad_x, (g_chip_in, r2_in), grads, r1, r2, small


def kernel(x, positions, norm_w, w_in, lb_logits, hgrn_norm_w, w_branch_a, w_branch_b, w_out, final_norm_w, loss_target, m_norm_w, m_w_in, m_lb_logits, m_hgrn_norm_w, m_w_branch_a, m_w_branch_b, m_w_out, m_final_norm_w, v_norm_w, v_w_in, v_lb_logits, v_hgrn_norm_w, v_w_branch_a, v_w_branch_b, v_w_out, v_final_norm_w):
    ix, iy, ic = _mesh_pos()
    core = jnp.reshape(ic, (1,)).astype(jnp.int32)
    pos = jnp.stack([4 * ix + 2 * iy + ic, 2 * ix + iy]).astype(jnp.int32)

    shards = [w_in[0], w_branch_a[0], w_branch_b[0], w_out[0]]
    moments_m = [m_w_in[0], m_w_branch_a[0], m_w_branch_b[0], m_w_out[0]]
    moments_v = [v_w_in[0], v_w_branch_a[0], v_w_branch_b[0], v_w_out[0]]
    names = ("w_in", "w_a", "w_b", "w_out")
    ids = GATHER_IDS
    shards_b = [_cast_bf16(w, f"cast_{nm}") for w, nm in zip(shards, names)]

    fnw2 = final_norm_w.reshape(1, D)
    grad_x, (g_chip_in, r2_in), grads, r1, r2, small = _local_step(
        x[0], positions[0], norm_w, lb_logits, hgrn_norm_w, fnw2, loss_target[0], shards_b[0], shards_b[1:], core)

    gathered = _gather_small(small)
    big =[_reduce_own_and_update(shards[0], moments_m[0], moments_v[0], g_chip_in, r2_in)]
    big += [_reduce_and_update(a, shards[a], moments_m[a], moments_v[a], grads[i], r1[i], r2[i], pos)
            for i, a in enumerate(ids)]
    sm = _small_update(gathered, norm_w, lb_logits, hgrn_norm_w, fnw2,
                       (m_norm_w, m_lb_logits, m_hgrn_norm_w, m_final_norm_w.reshape(1, D),
                        v_norm_w, v_lb_logits, v_hgrn_norm_w, v_final_norm_w.reshape(1, D)))
    loss = sm[0][0, 0]
    outs = [loss, grad_x[None]]
    for kind in range(4):
        s_nw, s_lb, s_hn, s_fn = sm[1 + 4 * kind:5 + 4 * kind]
        outs += [s_nw, big[0][kind][None], s_lb, s_hn, big[1][kind][None], big[2][kind][None],
                 big[3][kind][None], s_fn.reshape(D)]
    return tuple(outs)
```

```python
import functools

import jax
import jax.numpy as jnp
from jax import lax
from jax.experimental import pallas as pl
from jax.experimental.pallas import tpu as pltpu

F32 = jnp.float32
BF16 = jnp.bfloat16
MESH = pl.DeviceIdType.MESH

S = 2048
D = 1024
NDEV = 8
HEADS = 8
CHUNK = 64
SUB = 16
HBLK = 256
ATT_PAD = 128
ATT_UNROLL = 16
COPY_PARTS = 4
EXP_CLAMP = 80.0
EPS = 1e-6
IN_COLS = 11264
SHARD_COLS = IN_COLS // NDEV
DZ_ATT = 4096
DZ_GATES = 9216
ATT_DILS = (1, 4, 16)
ATT_SCALE = 64 ** -0.5
LANES = 128

ADAM_LR, ADAM_B1, ADAM_B2, ADAM_EPS, ADAM_WD, ADAM_STEP = 0.001, 0.9, 0.999, 1e-08, 0.01, 10

VMEM_LIMIT = 56 * 1024 * 1024


def _cp(sem=None, **kw):
    return pltpu.CompilerParams(dimension_semantics=sem, vmem_limit_bytes=VMEM_LIMIT, **kw)


def _dot(a, b):
    return jnp.dot(a, b, preferred_element_type=F32)


def _dot_nt(a, b):
    return lax.dot_general(a, b, (((1,), (1,)), ((), ())), preferred_element_type=F32)


def _dot_tn(a, b):
    return lax.dot_general(a, b, (((0,), (0,)), ((), ())), preferred_element_type=F32)


def _split2(x):
    hi = x.astype(BF16)
    lo = (x - hi.astype(F32)).astype(BF16)
    return hi, lo


def _split3(x):
    hi = x.astype(BF16)
    r = x - hi.astype(F32)
    mid = r.astype(BF16)
    lo = (r - mid.astype(F32)).astype(BF16)
    return hi, mid, lo


def _dot_ones(ones_bf16, x):
    hi, mid, lo = _split3(x)
    return _dot(ones_bf16, hi) + _dot(ones_bf16, mid) + _dot(ones_bf16, lo)


def _silu(x):
    return x * jax.nn.sigmoid(x)


def _dsilu(x):
    s = jax.nn.sigmoid(x)
    return s * (1.0 + x * (1.0 - s))


def _mesh_pos():
    return lax.axis_index("x"), lax.axis_index("y"), lax.axis_index("c")


class _SplitCopy:
    def __init__(self, src, dst, sem):
        self.src, self.dst, self.sem = src, dst, sem

    def start(self):
        rows = self.src.shape[0] // COPY_PARTS
        for p in range(COPY_PARTS):
            chunk = pl.ds(p * rows, rows)
            pltpu.make_async_copy(self.src.at[chunk], self.dst.at[chunk], self.sem).start()

    def wait(self):
        pltpu.make_async_copy(self.src, self.dst, self.sem).wait()


def _shard_of(ref, a, d):
    if a == 0:
        return ref.at[:, pl.ds(pl.multiple_of(d * SHARD_COLS, LANES), SHARD_COLS)]
    if a == 2:
        return ref.at[:, pl.ds(pl.multiple_of(d * LANES, LANES), LANES)]
    return ref.at[pl.ds(pl.multiple_of(d * 128, 128), 128), :]


FULL_SHAPES = ((D, IN_COLS), (D, D), (512, D), (D, D))
SHARD_SHAPES = ((D, SHARD_COLS), (128, D), (512, 128), (128, D))


def _allgather_steps(ids, ins, outs, send_sems, recv_sems, local_sems):
    n = len(ids)
    x, y, c = _mesh_pos()
    me, sibling = (x, y, c), (x, y, 1 - c)
    chips = [(1 - x, y), (x, 1 - y), (1 - x, 1 - y)]

    def blk(a, p):
        return _shard_of(outs[a], ids[a], 4 * p[0] + 2 * p[1] + p[2])

    def copy(a, k, block, to, src=None):
        return pltpu.make_async_remote_copy(
            src_ref=blk(a, block) if src is None else src, dst_ref=blk(a, block),
            send_sem=send_sems.at[a * 7 + k], recv_sem=recv_sems.at[a * 7 + k],
            device_id=to, device_id_type=MESH)

    mine = [pltpu.make_async_copy(ins[a], blk(a, me), local_sems.at[a]) for a in range(n)]
    first = []
    for a in range(n):
        first += [copy(a, 1 + j, me, (*chip, c), src=ins[a]) for j, chip in enumerate(chips)]
    for a in range(n):
        first.append(copy(a, 0, me, sibling, src=ins[a]))
    passed = [copy(a, 4 + j, (*chip, c), sibling) for j, chip in enumerate(chips) for a in range(n)]

    def start():
        for cp in mine + first:
            cp.start()

    def middle():
        for j, chip in enumerate(chips):
            for a in range(n):
                copy(a, 1 + j, (*chip, c), me).wait_recv()
                passed[j * n + a].start()

    def end():
        for a in range(n):
            copy(a, 0, sibling, me).wait_recv()
        for j, chip in enumerate(chips):
            for a in range(n):
                copy(a, 4 + j, (*chip, 1 - c), me).wait_recv()
        for cp in first + passed:
            cp.wait_send()
        for cp in mine:
            cp.wait()

    return start, middle, end


def _in_proj_gather(x, norm_w, w_shard):
    half = S // 2
    slab = (D, SHARD_COLS)
    xt = 512

    def body(x_hbm, nw_ref, w_hbm, z_hbm, wfull_hbm, h_hbm, h_buf, land, zstage, xbuf,
             h_sem, own_sem, z_sem, wout_sem, x_sem, send_sems, recv_sems):
        x, y, c = _mesh_pos()
        sibling = (x, y, 1 - c)
        north = c == 1

        def chips_of(first_x):
            near = (jnp.where(first_x, 1 - x, x), jnp.where(first_x, y, 1 - y))
            far = (jnp.where(first_x, x, 1 - x), jnp.where(first_x, 1 - y, y))
            return [near, far, (1 - x, 1 - y)]

        mine, theirs = chips_of(north), chips_of(jnp.logical_not(north))

        def dev(chip, core):
            return 4 * chip[0] + 2 * chip[1] + core

        block_of = ([dev((x, y), c), dev((x, y), 1 - c)] + [dev(q, c) for q in mine]
                    + [dev(q, 1 - c) for q in theirs])

        def cols(d):
            if isinstance(d, int):
                return pl.ds(d * SHARD_COLS, SHARD_COLS)
            return pl.ds(pl.multiple_of(d * SHARD_COLS, LANES), SHARD_COLS)

        def send(k, src, dst_slot, to):
            return pltpu.make_async_remote_copy(
                src_ref=src, dst_ref=land.at[dst_slot], send_sem=send_sems.at[k], recv_sem=recv_sems.at[k],
                device_id=to, device_id_type=MESH)

        def to_sibling():
            return send(0, w_hbm, 1, sibling)

        def to_chip(j):
            if j == 2:
                return send(3, land.at[2], 4, (*mine[1], c))
            return send(1 + j, w_hbm, 2 + j, (*mine[j], c))

        def pass_on(j):
            return send(4 + j, land.at[2 + j], 5 + j, sibling)

        def x_tile(i):
            return _SplitCopy(x_hbm.at[pl.ds(i * xt, xt), :], xbuf.at[i % 2], x_sem.at[i % 2])

        own = _SplitCopy(w_hbm, land.at[0], own_sem)
        own.start()
        x_tile(0).start()
        to_sibling().start()
        to_chip(0).start()
        for i in range(S // xt):
            if i + 1 < S // xt:
                x_tile(i + 1).start()
            x_tile(i).wait()
            xv = xbuf[i % 2]
            r = lax.rsqrt(jnp.mean(xv * xv, axis=-1, keepdims=True) + EPS)
            h_buf[i * xt:(i + 1) * xt, :] = (xv * r * nw_ref[...]).astype(BF16)
        h_out = _SplitCopy(h_buf, h_hbm, h_sem)
        h_out.start()
        own.wait()

        def multiply(slot, n_done):
            d = block_of[slot]
            out = _SplitCopy(land.at[slot], wfull_hbm.at[:, cols(d)], wout_sem.at[slot])
            out.start()
            for r in range(2):
                rows = pl.ds(r * half, half)
                zc = _SplitCopy(zstage.at[r], z_hbm.at[rows, cols(d)], z_sem.at[r])
                if n_done > 0:
                    zc.wait()
                zstage[r] = _dot(h_buf[r * half:(r + 1) * half, :], land[slot])
                zc.start()
            return out

        outs = [multiply(0, 0)]
        to_sibling().wait_recv()
        outs.append(multiply(1, 1))
        done = 2
        for j in range(3):
            to_chip(j).wait_recv()
            pass_on(j).start()
            to_chip(j).wait_send()
            if j < 2:
                to_chip(j + 1).start()
            outs.append(multiply(2 + j, done))
            pass_on(j).wait_recv()
            outs.append(multiply(5 + j, done + 1))
            done += 2
        for r in range(2):
            _SplitCopy(zstage.at[r], z_hbm.at[pl.ds(r * half, half), cols(0)], z_sem.at[r]).wait()
        for out in outs:
            out.wait()
        h_out.wait()
        to_sibling().wait_send()
        for j in range(3):
            pass_on(j).wait_send()

    any_spec = pl.BlockSpec(memory_space=pl.ANY)
    return pl.pallas_call(
        body, name="in_proj_gather",
        out_shape=[jax.ShapeDtypeStruct((S, IN_COLS), F32), jax.ShapeDtypeStruct((D, IN_COLS), BF16),
                   jax.ShapeDtypeStruct((S, D), BF16)],
        in_specs=[any_spec, pl.BlockSpec(memory_space=pltpu.VMEM), any_spec], out_specs=[any_spec] * 3,
        scratch_shapes=[pltpu.VMEM((S, D), BF16), pltpu.VMEM((8,) + slab, BF16), pltpu.VMEM((2, half, SHARD_COLS), F32),
                        pltpu.VMEM((2, xt, D), F32),
                        pltpu.SemaphoreType.DMA, pltpu.SemaphoreType.DMA, pltpu.SemaphoreType.DMA((2,)),
                        pltpu.SemaphoreType.DMA((8,)), pltpu.SemaphoreType.DMA((2,)),
                        pltpu.SemaphoreType.DMA((7,)), pltpu.SemaphoreType.DMA((7,))],
        compiler_params=_cp(),
    )(x, norm_w, w_shard)


def _exchange_sibling(ids, gb):
    n = len(gb)

    def body(*refs):
        ins, outs = refs[:n], refs[n:2 * n]
        send_sems, recv_sems = refs[2 * n:]
        x, y, c = _mesh_pos()
        sibling = (x, y, 1 - c)
        copies = []
        for i, a in enumerate(ids):
            for q in range(4):
                copies.append(pltpu.make_async_remote_copy(
                    src_ref=_shard_of(ins[i], a, 2 * q + (1 - c)), dst_ref=outs[i].at[q],
                    send_sem=send_sems.at[i * 4 + q], recv_sem=recv_sems.at[i * 4 + q],
                    device_id=sibling, device_id_type=MESH))
        for cp in copies:
            cp.start()
        for cp in copies:
            cp.wait()

    any_spec = pl.BlockSpec(memory_space=pl.ANY)
    return pl.pallas_call(
        body, name="grads_to_sibling",
        out_shape=[jax.ShapeDtypeStruct((4,) + SHARD_SHAPES[a], BF16) for a in ids],
        in_specs=[any_spec] * n, out_specs=[any_spec] * n,
        scratch_shapes=[pltpu.SemaphoreType.DMA((4 * n,)), pltpu.SemaphoreType.DMA((4 * n,))],
    )(*gb)


def _exchange_chips_steps(ins, outs, send_sems, recv_sems):
    x, y, c = _mesh_pos()
    chips = [(1 - x, y), (x, 1 - y), (1 - x, 1 - y)]
    copies = []
    for a in range(len(ins)):
        for k, chip in enumerate(chips):
            copies.append(pltpu.make_async_remote_copy(
                src_ref=ins[a].at[2 * chip[0] + chip[1]], dst_ref=outs[a].at[k],
                send_sem=send_sems.at[a * 3 + k], recv_sem=recv_sems.at[a * 3 + k],
                device_id=(*chip, c), device_id_type=MESH))

    def start():
        for cp in copies:
            cp.start()

    def end():
        for cp in copies:
            cp.wait()

    return start, end


def _gather_small_steps(small_ref, small_out, ssend, srecv, local_sem):
    x, y, c = _mesh_pos()
    me = 4 * x + 2 * y + c
    copies = []
    for r in range(1, NDEV):
        peer = (1 - x if r & 4 else x, 1 - y if r & 2 else y, 1 - c if r & 1 else c)
        copies.append(pltpu.make_async_remote_copy(
            src_ref=small_ref, dst_ref=small_out.at[me],
            send_sem=ssend.at[r - 1], recv_sem=srecv.at[r - 1],
            device_id=peer, device_id_type=MESH))
    own = pltpu.make_async_copy(small_ref, small_out.at[me], local_sem)

    def start():
        own.start()
        for cp in copies:
            cp.start()

    def end():
        for cp in copies:
            cp.wait()
        own.wait()

    return start, end


def _gather_small(small):
    def body(small_ref, small_out, ssend, srecv, local_sem):
        start, end = _gather_small_steps(small_ref, small_out, ssend, srecv, local_sem)
        start()
        end()

    any_spec = pl.BlockSpec(memory_space=pl.ANY)
    return pl.pallas_call(
        body, name="gather_small",
        out_shape=jax.ShapeDtypeStruct((NDEV,) + small.shape, F32),
        in_specs=[any_spec], out_specs=any_spec,
        scratch_shapes=[pltpu.SemaphoreType.DMA((NDEV - 1,)), pltpu.SemaphoreType.DMA((NDEV - 1,)),
                        pltpu.SemaphoreType.DMA],
    )(small)


def _shard_tiles(a):
    rows, cols = SHARD_SHAPES[a]
    tr = min(rows, 256)
    return (tr, cols), rows // tr


def _full_index(a, d, i):
    (tr, _), nt = _shard_tiles(a)
    if a in (0, 2):
        return (i, d)
    return (d * nt + i, 0)


def _cast_bf16(x, name):
    rows, cols = x.shape
    tr = min(rows, 256)

    def body(x_ref, o_ref):
        o_ref[...] = x_ref[...].astype(BF16)

    return pl.pallas_call(
        body, name=name, out_shape=jax.ShapeDtypeStruct(x.shape, BF16), grid=(rows // tr,),
        in_specs=[pl.BlockSpec((tr, cols), lambda i: (i, 0))],
        out_specs=pl.BlockSpec((tr, cols), lambda i: (i, 0)),
        compiler_params=_cp(("parallel",)),
    )(x)


def _chip_partials(a, g_full, r1, core):
    tile, nt = _shard_tiles(a)

    def body(c_ref, g_ref, r_ref, o_ref):
        o_ref[0] = (g_ref[...] + r_ref[0].astype(F32)).astype(BF16)

    grid_spec = pltpu.PrefetchScalarGridSpec(
        num_scalar_prefetch=1, grid=(4, nt),
        in_specs=[pl.BlockSpec(tile, lambda q, i, c: _full_index(a, 2 * q + c[0], i)),
                  pl.BlockSpec((1,) + tile, lambda q, i, c: (q, i, 0))],
        out_specs=pl.BlockSpec((1,) + tile, lambda q, i, c: (q, i, 0)))
    return pl.pallas_call(
        body, name=f"chip_partials_{a}", grid_spec=grid_spec,
        out_shape=jax.ShapeDtypeStruct((4,) + SHARD_SHAPES[a], BF16),
        compiler_params=_cp(("parallel", "parallel")),
    )(core, g_full, r1)


def _adam(w, g, m, v):
    m = ADAM_B1 * m + (1.0 - ADAM_B1) * g
    v = ADAM_B2 * v + (1.0 - ADAM_B2) * (g * g)
    m_hat = m / (1.0 - ADAM_B1 ** ADAM_STEP)
    v_hat = v / (1.0 - ADAM_B2 ** ADAM_STEP)
    delta = -ADAM_LR * (m_hat / (jnp.sqrt(v_hat) + ADAM_EPS) + ADAM_WD * w)
    return delta, m, v


def _reduce_and_update(a, w, m, v, g_full, r1, r2, pos):
    tile, nt = _shard_tiles(a)

    def body(p_ref, w_ref, m_ref, v_ref, g_ref, r1_ref, r2_ref, go_ref, do_ref, mo_ref, vo_ref):
        g = g_ref[...] + r1_ref[0].astype(F32)
        g = g + r2_ref[0].astype(F32)
        g = g + r2_ref[1].astype(F32)
        g = g + r2_ref[2].astype(F32)
        delta, m_new, v_new = _adam(w_ref[...], g, m_ref[...], v_ref[...])
        go_ref[...] = g
        do_ref[...] = delta
        mo_ref[...] = m_new
        vo_ref[...] = v_new

    own = pl.BlockSpec(tile, lambda i, p: (i, 0))
    grid_spec = pltpu.PrefetchScalarGridSpec(
        num_scalar_prefetch=1, grid=(nt,),
        in_specs=[own, own, own,
                  pl.BlockSpec(tile, lambda i, p: _full_index(a, p[0], i)),
                  pl.BlockSpec((1,) + tile, lambda i, p: (p[1], i, 0)),
                  pl.BlockSpec((3,) + tile, lambda i, p: (0, i, 0))],
        out_specs=[own] * 4)
    shp = jax.ShapeDtypeStruct(w.shape, F32)
    return pl.pallas_call(
        body, name=f"reduce_update_{a}", grid_spec=grid_spec, out_shape=[shp] * 4,
        compiler_params=_cp(("parallel",)),
    )(pos, w, m, v, g_full, r1, r2)


def _reduce_own_and_update(w, m, v, g_chip, r2):
    tile, nt = _shard_tiles(0)

    def body(w_ref, m_ref, v_ref, g_ref, r2_ref, go_ref, do_ref, mo_ref, vo_ref):
        g = g_ref[...] + r2_ref[0].astype(F32)
        g = g + r2_ref[1].astype(F32)
        delta, m_new, v_new = _adam(w_ref[...], g, m_ref[...], v_ref[...])
        go_ref[...] = g
        do_ref[...] = delta
        mo_ref[...] = m_new
        vo_ref[...] = v_new

    own = pl.BlockSpec(tile, lambda i: (i, 0))
    shp = jax.ShapeDtypeStruct(w.shape, F32)
    return pl.pallas_call(
        body, name="reduce_update_0", grid=(nt,), out_shape=[shp] * 4,
        in_specs=[own, own, own, own, pl.BlockSpec((2,) + tile, lambda i: (0, i, 0))], out_specs=[own] * 4,
        compiler_params=_cp(("parallel",)),
    )(w, m, v, g_chip, r2)


def _small_update(gathered, norm_w, lb_logits, hnw, fnw, moments):
    m_nw, m_lb, m_hn, m_fn, v_nw, v_lb, v_hn, v_fn = moments

    def body(g_ref, nw, lb, hn, fn, mnw, mlb, mhn, mfn, vnw, vlb, vhn, vfn,
             loss_o, g_nw, g_lb, g_hn, g_fn, d_nw, d_lb, d_hn, d_fn,
             mo_nw, mo_lb, mo_hn, mo_fn, vo_nw, vo_lb, vo_hn, vo_fn):
        tot = g_ref[0]
        for d in range(1, NDEV):
            tot = tot + g_ref[d]
        loss_o[...] = tot[4:5, 0:LANES]
        logits = lb[...]
        lbv = jax.nn.sigmoid(logits[0:1] - logits[1:2])
        chain = tot[1:2] * lbv * (1.0 - lbv)
        grads = (tot[0:1], jnp.concatenate([chain, -chain], axis=0), tot[2:3, 0:LANES], tot[3:4])
        outs = ((nw, mnw, vnw, g_nw, d_nw, mo_nw, vo_nw), (lb, mlb, vlb, g_lb, d_lb, mo_lb, vo_lb),
                (hn, mhn, vhn, g_hn, d_hn, mo_hn, vo_hn), (fn, mfn, vfn, g_fn, d_fn, mo_fn, vo_fn))
        for g, (w_r, m_r, v_r, g_o, d_o, m_o, v_o) in zip(grads, outs):
            delta, m_new, v_new = _adam(w_r[...], g, m_r[...], v_r[...])
            g_o[...] = g
            d_o[...] = delta
            m_o[...] = m_new
            v_o[...] = v_new

    shapes = [norm_w.shape, lb_logits.shape, hnw.shape, fnw.shape]
    out_shape = [jax.ShapeDtypeStruct((1, LANES), F32)] + [jax.ShapeDtypeStruct(s, F32) for s in shapes] * 4
    return pl.pallas_call(body, name="small_update", out_shape=out_shape, compiler_params=_cp())(
        gathered, norm_w, lb_logits, hnw, fnw, m_nw, m_lb, m_hn, m_fn, v_nw, v_lb, v_hn, v_fn)


def _block_tri(n, block, upper=False):
    r = lax.broadcasted_iota(jnp.int32, (n, n), 0)
    c = lax.broadcasted_iota(jnp.int32, (n, n), 1)
    keep = (c >= r) if upper else (c <= r)
    return jnp.where(keep & ((r // block) == (c // block)), 1.0, 0.0).astype(BF16)


def _tril_mask(n):
    r = lax.broadcasted_iota(jnp.int32, (n, n), 0)
    c = lax.broadcasted_iota(jnp.int32, (n, n), 1)
    return c <= r


def _chunk_scores(q, k, b, bex, r0, mask):
    parts, qs_l, ks_l, ek_l, eq_l = [], [], [], [], []
    for i in range(CHUNK // SUB):
        ri = slice(r0 + SUB * i, r0 + SUB * (i + 1))
        seen = slice(r0, r0 + SUB * (i + 1))
        base = bex[r0 + SUB * i:r0 + SUB * i + 1]
        eq = jnp.exp(b[ri] - base)
        ek = jnp.exp(jnp.minimum(base - b[seen], EXP_CLAMP))
        ks = k[seen] * ek
        if i + 1 < CHUNK // SUB:
            rest = jnp.zeros((CHUNK - SUB * (i + 1), 128), F32)
            ek, ks = jnp.concatenate([ek, rest], axis=0), jnp.concatenate([ks, rest], axis=0)
        qs = q[ri] * eq
        parts.append(_dot_nt(qs.astype(BF16), ks.astype(BF16)))
        qs_l.append(qs)
        ks_l.append(ks)
        ek_l.append(ek)
        eq_l.append(eq)
    return jnp.where(mask, jnp.concatenate(parts, axis=0), 0.0), qs_l, ks_l, ek_l, eq_l


def _hgrn_cols(hq, hf, hi, lb):
    sg = jax.nn.sigmoid(hf)
    f = lb + (1.0 - lb) * sg
    g = jnp.log(f)
    b = _dot_ones(_block_tri(HBLK, CHUNK), g)
    return _silu(hq), 1.0 - f, g, hi, sg, f, b


GATHER_IDS = (1, 2, 3)


def _hgrn_fwd(z, lbv, hnw, shards):
    ntb, nch = S // HBLK, HBLK // CHUNK
    n = len(GATHER_IDS)

    def body(hq_ref, hf_ref, hi_ref, hg_ref, lb_ref, hnw_ref, s0, s1, s2, o_ref, oa_ref, st_ref, f0, f1, f2,
             state, send_sems, recv_sems, local_sems):
        start, middle, end = _allgather_steps(GATHER_IDS, (s0, s1, s2), (f0, f1, f2), send_sems, recv_sems, local_sems)

        @pl.when(pl.program_id(0) == 0)
        def _():
            state[...] = jnp.zeros_like(state)
            start()

        pl.when(pl.program_id(0) == ntb // 2)(middle)

        q_a, k_a, g_a, v_a, _, _, b_a = _hgrn_cols(hq_ref[...], hf_ref[...], hi_ref[...], lb_ref[...])
        bex_a = b_a - g_a
        eb_a = jnp.exp(b_a)
        mask = _tril_mask(CHUNK)
        hg = hg_ref[...]
        w = hnw_ref[...]
        for h in range(HEADS):
            cols = slice(128 * h, 128 * h + 128)
            q, k, v, b, bex, eb = q_a[:, cols], k_a[:, cols], v_a[:, cols], b_a[:, cols], bex_a[:, cols], eb_a[:, cols]
            st = state[h]
            outs = []
            for c in range(nch):
                r0 = c * CHUNK
                rows = slice(r0, r0 + CHUNK)
                a = _chunk_scores(q, k, b, bex, r0, mask)[0]
                vb = v[rows].astype(BF16)
                b_last = b[r0 + CHUNK - 1:r0 + CHUNK]
                qe = (q[rows] * eb[rows]).astype(BF16)
                outs.append(_dot(a.astype(BF16), vb) + _dot_nt(qe, st.astype(BF16)))
                st_ref[h, c] = st
                ke = (k[rows] * jnp.exp(b_last - b[rows])).astype(BF16)
                st = st * jnp.exp(b_last) + _dot_tn(vb, ke)
            state[h] = st
            o = jnp.concatenate(outs, axis=0)
            o_ref[:, cols] = o
            r = lax.rsqrt(jnp.mean(o * o, axis=-1, keepdims=True) + EPS)
            oa_ref[:, cols] = (o * r * w * _silu(hg[:, cols])).astype(BF16)

        pl.when(pl.program_id(0) == ntb - 1)(end)

    def zcol(j):
        return pl.BlockSpec((HBLK, D), lambda t: (t, j))

    out_blk = pl.BlockSpec((HBLK, D), lambda t: (t, 0))
    any_spec = pl.BlockSpec(memory_space=pl.ANY)
    return pl.pallas_call(
        body, name="hgrn_fwd", grid=(ntb,),
        out_shape=[jax.ShapeDtypeStruct((S, D), F32), jax.ShapeDtypeStruct((S, D), BF16),
                   jax.ShapeDtypeStruct((HEADS, S // CHUNK, 128, 128), F32)]
        + [jax.ShapeDtypeStruct(FULL_SHAPES[a], BF16) for a in GATHER_IDS],
        in_specs=[zcol(0), zcol(1), zcol(2), zcol(3),
                  pl.BlockSpec((1, D), lambda t: (0, 0)), pl.BlockSpec((1, 128), lambda t: (0, 0))] + [any_spec] * n,
        out_specs=[out_blk, out_blk, pl.BlockSpec((HEADS, nch, 128, 128), lambda t: (0, t, 0, 0))] + [any_spec] * n,
        scratch_shapes=[pltpu.VMEM((HEADS, 128, 128), F32), pltpu.SemaphoreType.DMA((7 * n,)),
                        pltpu.SemaphoreType.DMA((7 * n,)), pltpu.SemaphoreType.DMA((n,))],
        compiler_params=_cp(("arbitrary",)),
    )(z, z, z, z, lbv, hnw, *shards)


def _half_mask():
    lane = lax.broadcasted_iota(jnp.int32, (1, LANES), 1)
    return (lane % 64) < 32


def _rope(t, cc, ss, first_half):
    partner = jnp.where(first_half, pltpu.roll(t, 96, 1), pltpu.roll(t, 32, 1))
    return t * cc + partner * ss


def _attn_masks():
    i = lax.broadcasted_iota(jnp.int32, (128, 128), 0)
    j = lax.broadcasted_iota(jnp.int32, (128, 128), 1)
    return j >= i, j <= i


def _to_residues_dyn(g, dst, src, row0=0, dtype=None):
    for gi, dil in enumerate((1, 4, 16)):
        m = S // dil

        @pl.when(g == gi)
        def _(dil=dil, m=m):
            for r in range(dil):
                v = src[...] if dil == 1 else src[pl.ds(r, m, stride=dil), :]
                if dtype is not None:
                    v = v.astype(dtype)
                dst[row0 + r * m:row0 + (r + 1) * m, 0:LANES] = v


def _from_residues_dyn(g, dst, src, row0=0):
    for gi, dil in enumerate((1, 4, 16)):
        m = S // dil

        @pl.when(g == gi)
        def _(dil=dil, m=m):
            for r in range(dil):
                v = src[row0 + r * m:row0 + (r + 1) * m, :]
                if dil == 1:
                    dst[...] = v
                else:
                    dst[pl.ds(r, m, stride=dil), :] = v


def _group_blocks(g):
    return jnp.where(g == 0, 16, jnp.where(g == 1, 4, 1))


def _attn_in_specs(extra):
    def zcol(off):
        return pl.BlockSpec((S, LANES), lambda p, g: (0, off + 4 * g + p))

    per_pair = pl.BlockSpec((S, LANES), lambda p, g: (0, p))
    const = pl.BlockSpec((S, LANES), lambda p, g: (0, 0))
    return [zcol(32), zcol(44), zcol(56), pl.BlockSpec((S, LANES), lambda p, g: (0, 68 + p)), const, const] + [per_pair] * extra


def _attn_fwd(z, cc, ss):
    def body(q_ref, k_ref, v_ref, ag_ref, cc_ref, ss_ref, ob_ref, lse_ref, obg_ref, qsb_ref, ksb_ref, vsb_ref,
             tmp, qs, ks, vx, og, mg, lg, o_t, m_t, l_t, o_acc, m_acc, l_acc):
        g = pl.program_id(1)
        first_half = _half_mask()
        prev_ok, cur_ok = _attn_masks()
        lane = lax.broadcasted_iota(jnp.int32, (1, LANES), 1)
        heads = (lane < 64, lane >= 64)
        nblk = _group_blocks(g)

        @pl.when(g == 0)
        def _():
            ks[0:ATT_PAD, :] = jnp.zeros((ATT_PAD, LANES), BF16)
            vx[0:ATT_PAD, 0:LANES] = jnp.zeros((ATT_PAD, LANES), BF16)
            vx[:, LANES:2 * LANES] = jnp.ones((ATT_PAD + S, LANES), BF16)

        tmp[...] = _rope(q_ref[...], cc_ref[...], ss_ref[...], first_half) * ATT_SCALE
        _to_residues_dyn(g, qs, tmp)
        tmp[...] = _rope(k_ref[...], cc_ref[...], ss_ref[...], first_half)
        _to_residues_dyn(g, ks, tmp, ATT_PAD, BF16)
        _to_residues_dyn(g, vx, v_ref, ATT_PAD, BF16)

        def unit(u, carry):
            start = pl.multiple_of(u * 128, 128)
            cur = pl.ds(start, 128)
            pm = prev_ok & ((u & (nblk - 1)) != 0)
            qu = qs[cur, :]
            kcat = ks[pl.ds(start, 256), :]
            vext = vx[pl.ds(start, 256), :]
            o_u = m_u = l_u = None
            for hh in range(2):
                s = _dot_nt(jnp.where(heads[hh], qu, 0.0).astype(BF16), kcat)
                sp = jnp.where(pm, s[:, 0:128], -jnp.inf)
                sc = jnp.where(cur_ok, s[:, 128:256], -jnp.inf)
                m = jnp.max(jnp.maximum(sp, sc), axis=-1, keepdims=True)
                p = jnp.concatenate([jnp.exp(sp - m), jnp.exp(sc - m)], axis=1).astype(BF16)
                ol = _dot(p, vext)
                mb = jnp.broadcast_to(m, (128, LANES))
                if hh == 0:
                    o_u, l_u, m_u = ol[:, 0:128], ol[:, 128:256], mb
                else:
                    o_u = jnp.where(heads[1], ol[:, 0:128], o_u)
                    l_u = jnp.where(heads[1], ol[:, 128:256], l_u)
                    m_u = jnp.where(heads[1], mb, m_u)
            og[cur, :] = o_u
            mg[cur, :] = m_u
            lg[cur, :] = l_u
            return carry

        lax.fori_loop(0, 16, unit, 0, unroll=16)
        qsb_ref[0] = qs[...].astype(BF16)
        ksb_ref[0] = ks[...]
        vsb_ref[0] = vx[:, 0:LANES]
        _from_residues_dyn(g, o_t, og)
        _from_residues_dyn(g, m_t, mg)
        _from_residues_dyn(g, l_t, lg)

        @pl.when(g == 0)
        def _():
            o_acc[...] = o_t[...]
            m_acc[...] = m_t[...]
            l_acc[...] = l_t[...]

        @pl.when(g > 0)
        def _():
            m_new = jnp.maximum(m_acc[...], m_t[...])
            wa, wb = jnp.exp(m_acc[...] - m_new), jnp.exp(m_t[...] - m_new)
            o_acc[...] = o_acc[...] * wa + o_t[...] * wb
            l_acc[...] = l_acc[...] * wa + l_t[...] * wb
            m_acc[...] = m_new

        @pl.when(g == 2)
        def _():
            ob = o_acc[...] / l_acc[...]
            ob_ref[...] = ob
            lse_ref[...] = m_acc[...] + jnp.log(l_acc[...])
            obg_ref[...] = (ob * _silu(ag_ref[...])).astype(BF16)

    blk = pl.BlockSpec((S, LANES), lambda p, g: (0, p))
    buf = pltpu.VMEM((S, LANES), F32)
    return pl.pallas_call(
        body, name="attn_fwd", grid=(4, 3),
        out_shape=[jax.ShapeDtypeStruct((S, 512), F32), jax.ShapeDtypeStruct((S, 512), F32),
                   jax.ShapeDtypeStruct((S, 512), BF16), jax.ShapeDtypeStruct((3, S, 512), BF16),
                   jax.ShapeDtypeStruct((3, ATT_PAD + S, 512), BF16), jax.ShapeDtypeStruct((3, ATT_PAD + S, 512), BF16)],
        in_specs=_attn_in_specs(0),
        out_specs=[blk, blk, blk, pl.BlockSpec((1, S, LANES), lambda p, g: (g, 0, p)),
                   pl.BlockSpec((1, ATT_PAD + S, LANES), lambda p, g: (g, 0, p)),
                   pl.BlockSpec((1, ATT_PAD + S, LANES), lambda p, g: (g, 0, p))],
        scratch_shapes=[buf, buf, pltpu.VMEM((ATT_PAD + S, LANES), BF16), pltpu.VMEM((ATT_PAD + S, 2 * LANES), BF16)] + [buf] * 9,
        compiler_params=_cp(("parallel", "arbitrary")),
    )(z, z, z, z, cc, ss)


def _tail(x, o_a, o_bg, z, target, w_a, w_b, w_out, fnw):
    tm = 256

    def body(x_ref, oa_ref, ob_ref, gpa_ref, gpb_ref, t_ref, wa_ref, wb_ref, wo_ref, fnw_ref,
             dx2_ref, dx2b_ref, dz_hbm, doa_ref, dob_ref, mg_ref, dya_ref, dyb_ref, small_ref, dgp, dgp_sem):
        step = pl.program_id(0)
        slot = step % 2

        def dgp_copy(at_step, at_slot):
            return pltpu.make_async_copy(
                dgp.at[at_slot], dz_hbm.at[pl.ds(pl.multiple_of(at_step * tm, tm), tm), pl.ds(DZ_GATES, 2 * D)],
                dgp_sem.at[at_slot])

        @pl.when(step == 0)
        def _():
            small_ref[...] = jnp.zeros_like(small_ref)

        @pl.when(step >= 2)
        def _():
            dgp_copy(step - 2, slot).wait()

        wa, wb, wo = wa_ref[...], wb_ref[...], wo_ref[...]
        y_a = _dot(oa_ref[...], wa)
        y_b = _dot(ob_ref[...], wb)
        ga = jax.nn.sigmoid(gpa_ref[...])
        gb = jax.nn.sigmoid(gpb_ref[...])
        merged = (ga * y_a + gb * y_b).astype(BF16)
        x2 = x_ref[...] + _dot(merged, wo)
        r2 = lax.rsqrt(jnp.mean(x2 * x2, axis=-1, keepdims=True) + EPS)
        n2 = x2 * r2
        fw = fnw_ref[...]
        err = n2 * fw - t_ref[...]
        loss = 0.5 * jnp.sum(jnp.sum(err * err, axis=-1, keepdims=True), axis=0, keepdims=True) / D
        dy = err * (1.0 / D)
        g_fnw = jnp.sum(dy * n2, axis=0, keepdims=True)
        dn = dy * fw
        dx2 = r2 * (dn - n2 * jnp.mean(dn * n2, axis=-1, keepdims=True))
        dx2b = dx2.astype(BF16)
        dmerged = _dot_nt(dx2b, wo)
        dy_a = (dmerged * ga).astype(BF16)
        dy_b = (dmerged * gb).astype(BF16)
        dx2_ref[...] = dx2
        dx2b_ref[...] = dx2b
        dgp[slot, :, 0:D] = (dmerged * y_a * ga * (1.0 - ga)).astype(BF16)
        dgp[slot, :, D:2 * D] = (dmerged * y_b * gb * (1.0 - gb)).astype(BF16)
        dgp_copy(step, slot).start()
        doa_ref[...] = _dot_nt(dy_a, wa)
        dob_ref[...] = _dot_nt(dy_b, wb)
        mg_ref[...] = merged
        dya_ref[...] = dy_a
        dyb_ref[...] = dy_b
        small_ref[0:1, :] += g_fnw
        small_ref[1:2, :] += jnp.broadcast_to(loss, (1, D))

        @pl.when(step == S // tm - 1)
        def _():
            dgp_copy(step - 1, 1 - slot).wait()
            dgp_copy(step, slot).wait()

    def rows(cols, off=0):
        return pl.BlockSpec((tm, cols), lambda i: (i, off))

    def whole(shape):
        return pl.BlockSpec(shape, lambda i: (0, 0))

    return pl.pallas_call(
        body, name="tail", grid=(S // tm,),
        out_shape=[jax.ShapeDtypeStruct((S, D), F32), jax.ShapeDtypeStruct((S, D), BF16),
                   jax.ShapeDtypeStruct((S, IN_COLS), BF16), jax.ShapeDtypeStruct((S, D), F32),
                   jax.ShapeDtypeStruct((S, 512), F32), jax.ShapeDtypeStruct((S, D), BF16),
                   jax.ShapeDtypeStruct((S, D), BF16), jax.ShapeDtypeStruct((S, D), BF16),
                   jax.ShapeDtypeStruct((8, D), F32)],
        in_specs=[rows(D), rows(D), rows(512), rows(D, 9), rows(D, 10), rows(D),
                  whole((D, D)), whole((512, D)), whole((D, D)), whole((1, D))],
        out_specs=[rows(D), rows(D), pl.BlockSpec(memory_space=pl.ANY), rows(D), rows(512), rows(D), rows(D),
                   rows(D), whole((8, D))],
        scratch_shapes=[pltpu.VMEM((2, tm, 2 * D), BF16), pltpu.SemaphoreType.DMA((2,))],
        compiler_params=_cp(("arbitrary",)),
    )(x, o_a, o_bg, z, z, target, w_a, w_b, w_out, fnw)


def _tn_matmul(a, b, name):
    m, n = a.shape[1], b.shape[1]
    tn = 512

    def body(a_ref, b_ref, o_ref, ob_ref):
        acc = _dot_tn(a_ref[...], b_ref[...])
        o_ref[...] = acc
        ob_ref[...] = acc.astype(BF16)

    out_blk = pl.BlockSpec((m, tn), lambda j: (0, j))
    return pl.pallas_call(
        body, name=name, grid=(n // tn,),
        out_shape=[jax.ShapeDtypeStruct((m, n), F32), jax.ShapeDtypeStruct((m, n), BF16)],
        in_specs=[pl.BlockSpec((S, m), lambda j: (0, 0)), pl.BlockSpec((S, tn), lambda j: (0, j))],
        out_specs=[out_blk, out_blk],
        compiler_params=_cp(("parallel",)),
    )(a, b)


def _hgrn_bwd(z, o, do_a, states, lbv, hnw, partials, dz):
    ntb, nch = S // HBLK, HBLK // CHUNK
    n = len(GATHER_IDS)

    def body(hq_ref, hf_ref, hi_ref, hg_ref, o_ref, doa_ref, st_ref, lb_ref, hnw_ref, p0, p1, p2, dz_in,
             dz_ref, glb_ref, ghn_ref, e0, e1, e2, dstate, send_sems, recv_sems):
        dhq_ref, dhf_ref, dhi_ref, dhg_ref = (dz_ref.at[:, pl.ds(j * D, D)] for j in range(4))
        start, end = _exchange_chips_steps((p0, p1, p2), (e0, e1, e2), send_sems, recv_sems)

        @pl.when(pl.program_id(0) == 0)
        def _():
            dstate[...] = jnp.zeros_like(dstate)
            glb_ref[...] = jnp.zeros_like(glb_ref)
            ghn_ref[...] = jnp.zeros_like(ghn_ref)
            start()

        lb_a = lb_ref[...]
        hq_a, hg_a = hq_ref[...], hg_ref[...]
        q_a, k_a, g_a, v_a, sg_a, f_a, b_a = _hgrn_cols(hq_a, hf_ref[...], hi_ref[...], lb_a)
        bex_a = b_a - g_a
        eb_a = jnp.exp(b_a)
        w = hnw_ref[...]
        mask = _tril_mask(CHUNK)
        upper = _block_tri(CHUNK, CHUNK, upper=True)
        for h in range(HEADS):
            cols = slice(128 * h, 128 * h + 128)
            q, k, v, b, bex, eb = q_a[:, cols], k_a[:, cols], v_a[:, cols], b_a[:, cols], bex_a[:, cols], eb_a[:, cols]
            hq, hg, sg, f, lb = hq_a[:, cols], hg_a[:, cols], sg_a[:, cols], f_a[:, cols], lb_a[:, cols]
            ov, doa = o_ref[:, cols], doa_ref[:, cols]
            r = lax.rsqrt(jnp.mean(ov * ov, axis=-1, keepdims=True) + EPS)
            n = ov * r
            sil = _silu(hg)
            dhg_ref[:, cols] = (doa * n * w * _dsilu(hg)).astype(BF16)
            ghn_ref[h] += jnp.sum(doa * sil * n, axis=0, keepdims=True)
            dn = doa * sil * w
            do = r * (dn - n * jnp.mean(dn * n, axis=-1, keepdims=True))

            dst = dstate[h]
            dq_l, dk_l, dv_l, dg_l = [None] * nch, [None] * nch, [None] * nch, [None] * nch
            for c in reversed(range(nch)):
                r0 = c * CHUNK
                rows = slice(r0, r0 + CHUNK)
                st = st_ref[h, c]
                bc, kc, qc = b[rows], k[rows], q[rows]
                vb, dob = v[rows].astype(BF16), do[rows].astype(BF16)
                b_last = bc[CHUNK - 1:CHUNK]
                e_last = jnp.exp(b_last)
                ekl = jnp.exp(b_last - bc)
                dstb = dst.astype(BF16)
                a, qs_l, ks_l, ek_l, eq_l = _chunk_scores(q, k, b, bex, r0, mask)
                da = jnp.where(mask, _dot_nt(dob, vb), 0.0)
                dv_l[c] = _dot_tn(a.astype(BF16), dob) + _dot_nt((kc * ekl).astype(BF16), dstb)
                dq_inter = _dot(dob, st.astype(BF16)) * eb[rows]
                dk_state = _dot(vb, dstb) * ekl
                dq_parts, dk_intra = [], jnp.zeros((CHUNK, 128), F32)
                dab = da.astype(BF16)
                for i in range(CHUNK // SUB):
                    da_i = dab[SUB * i:SUB * (i + 1)]
                    ks_hi, ks_lo = _split2(ks_l[i])
                    qs_hi, qs_lo = _split2(qs_l[i])
                    dq_parts.append((_dot(da_i, ks_hi) + _dot(da_i, ks_lo)) * eq_l[i])
                    dk_intra = dk_intra + (_dot_tn(da_i, qs_hi) + _dot_tn(da_i, qs_lo)) * ek_l[i]
                dq = jnp.concatenate(dq_parts, axis=0) + dq_inter
                dk = dk_intra + dk_state
                last = (e_last * jnp.sum(st * dst, axis=0, keepdims=True)
                        + jnp.sum(kc * dk_state, axis=0, keepdims=True))
                dg_l[c] = _dot_ones(upper, qc * dq - kc * dk) + last
                dq_l[c], dk_l[c] = dq, dk
                dst = dst * e_last + _dot_tn(dob, (qc * eb[rows]).astype(BF16))
            dstate[h] = dst
            dq, dk = jnp.concatenate(dq_l, axis=0), jnp.concatenate(dk_l, axis=0)
            dg, dv = jnp.concatenate(dg_l, axis=0), jnp.concatenate(dv_l, axis=0)
            dhq_ref[:, cols] = (dq * _dsilu(hq)).astype(BF16)
            dhi_ref[:, cols] = dv.astype(BF16)
            df = dg / f - dk
            dhf_ref[:, cols] = (df * (1.0 - lb) * sg * (1.0 - sg)).astype(BF16)
            glb_ref[:, cols] += jnp.sum(df * (1.0 - sg), axis=0, keepdims=True)

        pl.when(pl.program_id(0) == ntb - 1)(end)

    def rev(t):
        return ntb - 1 - t

    def zcol(j):
        return pl.BlockSpec((HBLK, D), lambda t: (rev(t), j))

    blk = pl.BlockSpec((HBLK, D), lambda t: (rev(t), 0))
    any_spec = pl.BlockSpec(memory_space=pl.ANY)
    return pl.pallas_call(
        body, name="hgrn_bwd", grid=(ntb,),
        out_shape=[jax.ShapeDtypeStruct((S, IN_COLS), BF16)]
        + [jax.ShapeDtypeStruct((1, D), F32), jax.ShapeDtypeStruct((HEADS, 1, 128), F32)]
        + [jax.ShapeDtypeStruct((3,) + SHARD_SHAPES[a], BF16) for a in GATHER_IDS],
        in_specs=[zcol(0), zcol(1), zcol(2), zcol(3), blk, blk,
                  pl.BlockSpec((HEADS, nch, 128, 128), lambda t: (0, rev(t), 0, 0)),
                  pl.BlockSpec((1, D), lambda t: (0, 0)), pl.BlockSpec((1, 128), lambda t: (0, 0))]
        + [any_spec] * (n + 1),
        out_specs=[pl.BlockSpec((HBLK, DZ_ATT), lambda t: (rev(t), 0)), pl.BlockSpec((1, D), lambda t: (0, 0)),
                   pl.BlockSpec((HEADS, 1, 128), lambda t: (0, 0, 0))] + [any_spec] * n,
        scratch_shapes=[pltpu.VMEM((HEADS, 128, 128), F32), pltpu.SemaphoreType.DMA((3 * n,)),
                        pltpu.SemaphoreType.DMA((3 * n,))],
        input_output_aliases={9 + n: 0},
        compiler_params=_cp(("arbitrary",)),
    )(z, z, z, z, o, do_a, states, lbv, hnw, *partials, dz)


def _attn_bwd(z, qsb, ksb, vsb, cc, ss, ob, lse, do_bg, dz):
    def body(qs, ks, vs, ag_ref, cc_ref, ss_ref, ob_ref, lse_ref, dobg_ref, dz_in, dz_hbm,
             tmp, dos, dqs, dks, dvs, dkp, dvp, do_t, ls0_t, ls1_t, dl0_t, dl1_t, ls0, ls1, dl0, dl1,
             stage, stage_sem):
        pair, g = pl.program_id(0), pl.program_id(1)

        def out_copy(j):
            tile = DZ_ATT // LANES + (36 + pair if j == 3 else 12 * j + 4 * g + pair)
            return pltpu.make_async_copy(
                stage.at[j], dz_hbm.at[:, pl.ds(pl.multiple_of(tile * LANES, LANES), LANES)], stage_sem.at[j])

        def restage(j, value):
            pl.when(pair * 3 + g > 0)(lambda: out_copy(j).wait())
            stage[j] = value
            out_copy(j).start()

        pl.when(g == 2)(lambda: out_copy(3).wait())
        first_half = _half_mask()
        prev_ok, cur_ok = _attn_masks()
        lane = lax.broadcasted_iota(jnp.int32, (1, LANES), 1)
        heads = (lane < 64, lane >= 64)
        nblk = _group_blocks(g)
        cc_v, ss_v = cc_ref[...], ss_ref[...]

        @pl.when(g == 0)
        def _():
            ag, obv, dobg = ag_ref[...], ob_ref[...], dobg_ref[...]
            stage[3] = (dobg * obv * _dsilu(ag)).astype(BF16)
            out_copy(3).start()
            dob = dobg * _silu(ag)
            do_t[...] = dob
            prod = dob * obv
            dl = jnp.concatenate(
                [jnp.broadcast_to(jnp.sum(prod[:, 0:64], axis=-1, keepdims=True), (S, 64)),
                 jnp.broadcast_to(jnp.sum(prod[:, 64:128], axis=-1, keepdims=True), (S, 64))], axis=1)
            dl0_t[...] = dl

        _to_residues_dyn(g, dos, do_t)
        _to_residues_dyn(g, ls0, lse_ref)
        _to_residues_dyn(g, dl0, dl0_t)

        def unit(u, carry):
            start = pl.multiple_of(u * 128, 128)
            cur = pl.ds(start, 128)
            both = pl.ds(start, 256)
            pm = prev_ok & ((u & (nblk - 1)) != 0)
            qu, dou = qs[0, cur, :], dos[cur, :]
            kcat, vcat = ks[0, both, :], vs[0, both, :]
            dq_u = None
            q_l, do_l, ds_l, p_l = [], [], [], []
            ls_u, dl_u = ls0[cur, :], dl0[cur, :]
            ls_sw, dl_sw = pltpu.roll(ls_u, 64, 1), pltpu.roll(dl_u, 64, 1)
            for hh in range(2):
                q_h = jnp.where(heads[hh], qu, jnp.zeros((), BF16))
                do_h = jnp.where(heads[hh], dou, 0.0).astype(BF16)
                s = _dot_nt(q_h, kcat)
                dp = _dot_nt(do_h, vcat)
                lse_h = jnp.where(heads[hh], ls_u, ls_sw)
                dl_h = jnp.where(heads[hh], dl_u, dl_sw)
                pp = jnp.where(pm, jnp.exp(s[:, 0:128] - lse_h), 0.0)
                pc = jnp.where(cur_ok, jnp.exp(s[:, 128:256] - lse_h), 0.0)
                ds = jnp.concatenate([pp * (dp[:, 0:128] - dl_h), pc * (dp[:, 128:256] - dl_h)], axis=1).astype(BF16)
                dq = _dot(ds, kcat)
                dq_u = dq if hh == 0 else jnp.where(heads[1], dq, dq_u)
                q_l.append(q_h)
                do_l.append(do_h)
                ds_l.append(ds)
                p_l.append(jnp.concatenate([pp, pc], axis=1).astype(BF16))
            dkcat = _dot_tn(jnp.concatenate(ds_l, axis=0), jnp.concatenate(q_l, axis=0))
            dvcat = _dot_tn(jnp.concatenate(p_l, axis=0), jnp.concatenate(do_l, axis=0))
            dkp[cur, :] = dkcat[0:128]
            dks[cur, :] = dkcat[128:256]
            dvp[cur, :] = dvcat[0:128]
            dvs[cur, :] = dvcat[128:256]
            dqs[cur, :] = dq_u
            return carry

        lax.fori_loop(0, 16, unit, 0, unroll=ATT_UNROLL)
        dks[0:S - 128, :] += dkp[128:S, :]
        dvs[0:S - 128, :] += dvp[128:S, :]
        _from_residues_dyn(g, tmp, dqs)
        restage(0, (_rope(tmp[...], cc_v, -ss_v, first_half) * ATT_SCALE).astype(BF16))
        _from_residues_dyn(g, tmp, dks)
        restage(1, _rope(tmp[...], cc_v, -ss_v, first_half).astype(BF16))
        _from_residues_dyn(g, tmp, dvs)
        restage(2, tmp[...].astype(BF16))

        @pl.when(pair * 3 + g == 11)
        def _():
            for j in range(3):
                out_copy(j).wait()

    any_spec = pl.BlockSpec(memory_space=pl.ANY)
    buf = pltpu.VMEM((S, LANES), F32)
    padded_b = pltpu.VMEM((ATT_PAD + S, LANES), BF16)
    return pl.pallas_call(
        body, name="attn_bwd", grid=(4, 3),
        out_shape=jax.ShapeDtypeStruct((S, IN_COLS), BF16),
        in_specs=[pl.BlockSpec((1, S, LANES), lambda p, g: (g, 0, p)),
                  pl.BlockSpec((1, ATT_PAD + S, LANES), lambda p, g: (g, 0, p)),
                  pl.BlockSpec((1, ATT_PAD + S, LANES), lambda p, g: (g, 0, p))] + _attn_in_specs(3)[3:] + [any_spec],
        out_specs=any_spec,
        scratch_shapes=[buf] * 16 + [pltpu.VMEM((4, S, LANES), BF16), pltpu.SemaphoreType.DMA((4,))],
        input_output_aliases={9: 0},
        compiler_params=_cp(("arbitrary", "arbitrary")),
    )(qsb, ksb, vsb, z, cc, ss, ob, lse, do_bg, dz)


def _in_proj_bwd(dz, h, w_in):
    half = S // 2
    slab = (D, SHARD_COLS)

    def body(dz_hbm, h_hbm, w_hbm, dh_hbm, g_chip, r1_hbm, relay_hbm, r2_hbm,
             h_buf, dz_buf, stage_d, r1_buf, stage_i, acc,
             dz_sem, w_sem, h_sem, r1_sem, out_sem, send_d, recv_d, send_i, recv_i):
        x, y, c = _mesh_pos()
        sibling = (x, y, 1 - c)
        north = c == 1
        near = (jnp.where(north, 1 - x, x), jnp.where(north, y, 1 - y))
        far = (jnp.where(north, x, 1 - x), jnp.where(north, 1 - y, y))
        chips = [(1 - x, 1 - y), near, far, (x, y)]

        def cols(d):
            return pl.ds(pl.multiple_of(d * SHARD_COLS, LANES), SHARD_COLS)

        blocks = []
        for q_sib, q in zip([chips[0], far, near, chips[3]], chips):
            blocks += [4 * q_sib[0] + 2 * q_sib[1] + (1 - c), 4 * q[0] + 2 * q[1] + c]

        def dz_tile(t):
            return _SplitCopy(dz_hbm.at[pl.ds((t % 2) * half, half), cols(blocks[t // 2])],
                                         dz_buf.at[t % 2], dz_sem.at[t % 2])

        def to_sibling(i):
            return pltpu.make_async_remote_copy(
                src_ref=stage_d.at[i % 2], dst_ref=r1_hbm.at[i], send_sem=send_d.at[i], recv_sem=recv_d.at[i],
                device_id=sibling, device_id_type=MESH)

        def to_owner(i):
            dst = relay_hbm if i == 0 else r2_hbm.at[i - 1]
            return pltpu.make_async_remote_copy(
                src_ref=stage_i.at[i], dst_ref=dst, send_sem=send_i.at[i], recv_sem=recv_i.at[i],
                device_id=(*(far if i == 2 else near), c), device_id_type=MESH)

        h_copy = _SplitCopy(h_hbm, h_buf, h_sem)
        h_copy.start()
        dz_tile(0).start()
        h_copy.wait()
        for b in range(8):
            i = b // 2
            g = None
            for r in range(2):
                t = 2 * b + r
                if t + 1 < 16:
                    dz_tile(t + 1).start()
                dz_tile(t).wait()
                part = _dot_tn(h_buf[r * half:(r + 1) * half, :], dz_buf[t % 2])
                g = part if g is None else g + part
                if b % 2 == 1 and r == 0:
                    to_sibling(i).wait_recv()
                    r1_copy = _SplitCopy(r1_hbm.at[i], r1_buf, r1_sem)
                    r1_copy.start()
            if b % 2 == 0:
                if i >= 2:
                    to_sibling(i - 2).wait_send()
                stage_d[i % 2] = g.astype(BF16)
                to_sibling(i).start()
            else:
                r1_copy.wait()
                g = g + r1_buf[...].astype(F32)
                if i == 2:
                    to_owner(0).wait_recv()
                    relay_copy = _SplitCopy(relay_hbm, r1_buf, r1_sem)
                    relay_copy.start()
                    relay_copy.wait()
                    g = g + r1_buf[...].astype(F32)
                if i < 3:
                    stage_i[i] = g.astype(BF16)
                    to_owner(i).start()
                else:
                    g_chip[...] = g
        to_sibling(2).wait_send()
        to_sibling(3).wait_send()

        def dz2(t):
            return _SplitCopy(
                dz_hbm.at[pl.ds((t % 2) * half, half), pl.ds((t // 2) * SHARD_COLS, SHARD_COLS)],
                dz_buf.at[t % 2], dz_sem.at[t % 2])

        def w2(b):
            return _SplitCopy(w_hbm.at[:, pl.ds(b * SHARD_COLS, SHARD_COLS)],
                                         stage_d.at[b % 2], w_sem.at[b % 2])

        dz2(0).start()
        w2(0).start()
        for t in range(16):
            b, r = t // 2, t % 2
            if t + 1 < 16:
                dz2(t + 1).start()
            if r == 0:
                if b + 1 < 8:
                    w2(b + 1).start()
                w2(b).wait()
            dz2(t).wait()
            part = _dot_nt(dz_buf[t % 2], stage_d[b % 2])
            if b == 0:
                acc[r] = part
            else:
                acc[r] += part
        dh_out = [_SplitCopy(acc.at[r], dh_hbm.at[pl.ds(r * half, half), :], out_sem.at[r])
                  for r in range(2)]
        for cp in dh_out:
            cp.start()
        for cp in dh_out:
            cp.wait()
        for i in range(3):
            to_owner(i).wait_send()
        for i in (1, 2):
            to_owner(i).wait_recv()

    any_spec = pl.BlockSpec(memory_space=pl.ANY)
    return pl.pallas_call(
        body, name="in_proj_bwd",
        out_shape=[jax.ShapeDtypeStruct((S, D), F32), jax.ShapeDtypeStruct(slab, F32),
                   jax.ShapeDtypeStruct((4,) + slab, BF16), jax.ShapeDtypeStruct(slab, BF16),
                   jax.ShapeDtypeStruct((2,) + slab, BF16)],
        in_specs=[any_spec] * 3,
        out_specs=[any_spec, pl.BlockSpec(memory_space=pltpu.VMEM), any_spec, any_spec, any_spec],
        scratch_shapes=[pltpu.VMEM((S, D), BF16), pltpu.VMEM((2, half, SHARD_COLS), BF16),
                        pltpu.VMEM((2,) + slab, BF16), pltpu.VMEM(slab, BF16), pltpu.VMEM((3,) + slab, BF16),
                        pltpu.VMEM((2, half, D), F32),
                        pltpu.SemaphoreType.DMA((2,)), pltpu.SemaphoreType.DMA((2,)), pltpu.SemaphoreType.DMA,
                        pltpu.SemaphoreType.DMA, pltpu.SemaphoreType.DMA((2,)),
                        pltpu.SemaphoreType.DMA((4,)), pltpu.SemaphoreType.DMA((4,)),
                        pltpu.SemaphoreType.DMA((3,)), pltpu.SemaphoreType.DMA((3,))],
        compiler_params=_cp(),
    )(dz, h, w_in)


def _grad_x(x, norm_w, dh, dx2):
    tr = 256

    def body(x_ref, w_ref, dh_ref, dx2_ref, gx_ref, gnw_ref):
        @pl.when(pl.program_id(0) == 0)
        def _():
            gnw_ref[...] = jnp.zeros_like(gnw_ref)

        xv, dhv = x_ref[...], dh_ref[...]
        r = lax.rsqrt(jnp.mean(xv * xv, axis=-1, keepdims=True) + EPS)
        n = xv * r
        gnw_ref[...] += jnp.sum(dhv * n, axis=0, keepdims=True)
        dn = dhv * w_ref[...]
        gx_ref[...] = dx2_ref[...] + r * (dn - n * jnp.mean(dn * n, axis=-1, keepdims=True))

    row = pl.BlockSpec((tr, D), lambda i: (i, 0))
    vec = pl.BlockSpec((1, D), lambda i: (0, 0))
    return pl.pallas_call(
        body, name="grad_x", grid=(S // tr,),
        out_shape=[jax.ShapeDtypeStruct((S, D), F32), jax.ShapeDtypeStruct((1, D), F32)],
        in_specs=[row, vec, row, row], out_specs=[row, vec],
        compiler_params=_cp(("arbitrary",)),
    )(x, norm_w, dh, dx2)


def _rope_tables(positions):
    inv_freq = 10000.0 ** (-jnp.arange(0, 64, 2, dtype=F32) / 64)
    ang = positions.astype(F32)[:, None] * inv_freq[None, :]
    cos, sin = jnp.cos(ang), jnp.sin(ang)
    return jnp.tile(cos, (1, 4)), jnp.tile(jnp.concatenate([-sin, sin], axis=1), (1, 2))


def _local_step(x, positions, norm_w, lb_logits, hnw, fnw, target, w_in_shard, small_shards, core):
    cc, ss = _rope_tables(positions)
    lbv = jax.nn.sigmoid(lb_logits[0:1] - lb_logits[1:2])
    z, w_in, h = _in_proj_gather(x, norm_w, w_in_shard)
    o, o_a, states, w_a, w_b, w_out = _hgrn_fwd(z, lbv, hnw, small_shards)
    ob, lse, o_bg, qsb, ksb, vsb = _attn_fwd(z, cc, ss)
    dx2, dx2b, dz, do_a, do_bg, merged, dy_a, dy_b, tail_small = _tail(x, o_a, o_bg, z, target, w_a, w_b, w_out, fnw)
    g_out, gb_out = _tn_matmul(merged, dx2b, "grad_w_out")
    g_a, gb_a = _tn_matmul(o_a, dy_a, "grad_w_a")
    g_b, gb_b = _tn_matmul(o_bg, dy_b, "grad_w_b")
    grads, gb = (g_a, g_b, g_out), (gb_a, gb_b, gb_out)
    r1 = _exchange_sibling(GATHER_IDS, gb)
    pb = [_chip_partials(a, grads[i], r1[i], core) for i, a in enumerate(GATHER_IDS)]
    dz, glb, ghn, *r2 = _hgrn_bwd(z, o, do_a, states, lbv, hnw, pb, dz)
    dz = _attn_bwd(z, qsb, ksb, vsb, cc, ss, ob, lse, do_bg, dz)
    dh, g_chip_in, _, _, r2_in = _in_proj_bwd(dz, h, w_in)
    grad_x, gnw = _grad_x(x, norm_w, dh, dx2)
    ghn_row = jnp.pad(jnp.sum(ghn, axis=0), ((0, 0), (0, D - 128)))
    small = jnp.concatenate([gnw, glb, ghn_row, tail_small[0:2], jnp.zeros((3, D), F32)], axis=0)
    return grad_x, (g_chip_in, r2_in), grads, r1, r2, small


def kernel(x, positions, norm_w, w_in, lb_logits, hgrn_norm_w, w_branch_a, w_branch_b, w_out, final_norm_w, loss_target, m_norm_w, m_w_in, m_lb_logits, m_hgrn_norm_w, m_w_branch_a, m_w_branch_b, m_w_out, m_final_norm_w, v_norm_w, v_w_in, v_lb_logits, v_hgrn_norm_w, v_w_branch_a, v_w_branch_b, v_w_out, v_final_norm_w):
    ix, iy, ic = _mesh_pos()
    core = jnp.reshape(ic, (1,)).astype(jnp.int32)
    pos = jnp.stack([4 * ix + 2 * iy + ic, 2 * ix + iy]).astype(jnp.int32)

    shards = [w_in[0], w_branch_a[0], w_branch_b[0], w_out[0]]
    moments_m = [m_w_in[0], m_w_branch_a[0], m_w_branch_b[0], m_w_out[0]]
    moments_v = [v_w_in[0], v_w_branch_a[0], v_w_branch_b[0], v_w_out[0]]
    names = ("w_in", "w_a", "w_b", "w_out")
    ids = GATHER_IDS
    shards_b = [_cast_bf16(w, f"cast_{nm}") for w, nm in zip(shards, names)]

    fnw2 = final_norm_w.reshape(1, D)
    grad_x, (g_chip_in, r2_in), grads, r1, r2, small = _local_step(
        x[0], positions[0], norm_w, lb_logits, hgrn_norm_w, fnw2, loss_target[0], shards_b[0], shards_b[1:], core)

    gathered = _gather_small(small)
    big =[_reduce_own_and_update(shards[0], moments_m[0], moments_v[0], g_chip_in, r2_in)]
    big += [_reduce_and_update(a, shards[a], moments_m[a], moments_v[a], grads[i], r1[i], r2[i], pos)
            for i, a in enumerate(ids)]
    sm = _small_update(gathered, norm_w, lb_logits, hgrn_norm_w, fnw2,
                       (m_norm_w, m_lb_logits, m_hgrn_norm_w, m_final_norm_w.reshape(1, D),
                        v_norm_w, v_lb_logits, v_hgrn_norm_w, v_final_norm_w.reshape(1, D)))
    loss = sm[0][0, 0]
    outs = [loss, grad_x[None]]
    for kind in range(4):
        s_nw, s_lb, s_hn, s_fn = sm[1 + 4 * kind:5 + 4 * kind]
        outs += [s_nw, big[0][kind][None], s_lb, s_hn, big[1][kind][None], big[2][kind][None],
                 big[3][kind][None], s_fn.reshape(D)]
    return tuple(outs)
```

```python
import functools

import jax
import jax.numpy as jnp
from jax import lax
from jax.experimental import pallas as pl
from jax.experimental.pallas import tpu as pltpu

F32 = jnp.float32
BF16 = jnp.bfloat16
MESH = pl.DeviceIdType.MESH

S = 2048
D = 1024
NDEV = 8
HEADS = 8
CHUNK = 64
SUB = 16
HBLK = 256
ATT_PAD = 128
ATT_UNROLL = 16
COPY_PARTS = 4
EXP_CLAMP = 80.0
EPS = 1e-6
IN_COLS = 11264
SHARD_COLS = IN_COLS // NDEV
DZ_ATT = 4096
DZ_GATES = 9216
ATT_DILS = (1, 4, 16)
ATT_SCALE = 64 ** -0.5
LANES = 128

ADAM_LR, ADAM_B1, ADAM_B2, ADAM_EPS, ADAM_WD, ADAM_STEP = 0.001, 0.9, 0.999, 1e-08, 0.01, 10

VMEM_LIMIT = 56 * 1024 * 1024


def _cp(sem=None, **kw):
    return pltpu.CompilerParams(dimension_semantics=sem, vmem_limit_bytes=VMEM_LIMIT, **kw)


def _dot(a, b):
    return jnp.dot(a, b, preferred_element_type=F32)


def _dot_nt(a, b):
    return lax.dot_general(a, b, (((1,), (1,)), ((), ())), preferred_element_type=F32)


def _dot_tn(a, b):
    return lax.dot_general(a, b, (((0,), (0,)), ((), ())), preferred_element_type=F32)


def _split2(x):
    hi = x.astype(BF16)
    lo = (x - hi.astype(F32)).astype(BF16)
    return hi, lo


def _split3(x):
    hi = x.astype(BF16)
    r = x - hi.astype(F32)
    mid = r.astype(BF16)
    lo = (r - mid.astype(F32)).astype(BF16)
    return hi, mid, lo


def _dot_ones(ones_bf16, x):
    hi, mid, lo = _split3(x)
    return _dot(ones_bf16, hi) + _dot(ones_bf16, mid) + _dot(ones_bf16, lo)


def _silu(x):
    return x * jax.nn.sigmoid(x)


def _dsilu(x):
    s = jax.nn.sigmoid(x)
    return s * (1.0 + x * (1.0 - s))


def _mesh_pos():
    return lax.axis_index("x"), lax.axis_index("y"), lax.axis_index("c")


class _SplitCopy:
    def __init__(self, src, dst, sem):
        self.src, self.dst, self.sem = src, dst, sem

    def start(self):
        rows = self.src.shape[0] // COPY_PARTS
        for p in range(COPY_PARTS):
            chunk = pl.ds(p * rows, rows)
            pltpu.make_async_copy(self.src.at[chunk], self.dst.at[chunk], self.sem).start()

    def wait(self):
        pltpu.make_async_copy(self.src, self.dst, self.sem).wait()


def _shard_of(ref, a, d):
    if a == 0:
        return ref.at[:, pl.ds(pl.multiple_of(d * SHARD_COLS, LANES), SHARD_COLS)]
    if a == 2:
        return ref.at[:, pl.ds(pl.multiple_of(d * LANES, LANES), LANES)]
    return ref.at[pl.ds(pl.multiple_of(d * 128, 128), 128), :]


FULL_SHAPES = ((D, IN_COLS), (D, D), (512, D), (D, D))
SHARD_SHAPES = ((D, SHARD_COLS), (128, D), (512, 128), (128, D))


def _allgather_steps(ids, ins, outs, send_sems, recv_sems, local_sems):
    n = len(ids)
    x, y, c = _mesh_pos()
    me, sibling = (x, y, c), (x, y, 1 - c)
    chips = [(1 - x, y), (x, 1 - y), (1 - x, 1 - y)]

    def blk(a, p):
        return _shard_of(outs[a], ids[a], 4 * p[0] + 2 * p[1] + p[2])

    def copy(a, k, block, to, src=None):
        return pltpu.make_async_remote_copy(
            src_ref=blk(a, block) if src is None else src, dst_ref=blk(a, block),
            send_sem=send_sems.at[a * 7 + k], recv_sem=recv_sems.at[a * 7 + k],
            device_id=to, device_id_type=MESH)

    mine = [pltpu.make_async_copy(ins[a], blk(a, me), local_sems.at[a]) for a in range(n)]
    first = []
    for a in range(n):
        first += [copy(a, 1 + j, me, (*chip, c), src=ins[a]) for j, chip in enumerate(chips)]
    for a in range(n):
        first.append(copy(a, 0, me, sibling, src=ins[a]))
    passed = [copy(a, 4 + j, (*chip, c), sibling) for j, chip in enumerate(chips) for a in range(n)]

    def start():
        for cp in mine + first:
            cp.start()

    def middle():
        for j, chip in enumerate(chips):
            for a in range(n):
                copy(a, 1 + j, (*chip, c), me).wait_recv()
                passed[j * n + a].start()

    def end():
        for a in range(n):
            copy(a, 0, sibling, me).wait_recv()
        for j, chip in enumerate(chips):
            for a in range(n):
                copy(a, 4 + j, (*chip, 1 - c), me).wait_recv()
        for cp in first + passed:
            cp.wait_send()
        for cp in mine:
            cp.wait()

    return start, middle, end


def _in_proj_gather(x, norm_w, w_shard):
    half = S // 2
    slab = (D, SHARD_COLS)
    xt = 512

    def body(x_hbm, nw_ref, w_hbm, z_hbm, wfull_hbm, h_hbm, h_buf, land, zstage, xbuf,
             h_sem, own_sem, z_sem, wout_sem, x_sem, send_sems, recv_sems):
        x, y, c = _mesh_pos()
        sibling = (x, y, 1 - c)
        north = c == 1

        def chips_of(first_x):
            near = (jnp.where(first_x, 1 - x, x), jnp.where(first_x, y, 1 - y))
            far = (jnp.where(first_x, x, 1 - x), jnp.where(first_x, 1 - y, y))
            return [near, far, (1 - x, 1 - y)]

        mine, theirs = chips_of(north), chips_of(jnp.logical_not(north))

        def dev(chip, core):
            return 4 * chip[0] + 2 * chip[1] + core

        block_of = ([dev((x, y), c), dev((x, y), 1 - c)] + [dev(q, c) for q in mine]
                    + [dev(q, 1 - c) for q in theirs])

        def cols(d):
            if isinstance(d, int):
                return pl.ds(d * SHARD_COLS, SHARD_COLS)
            return pl.ds(pl.multiple_of(d * SHARD_COLS, LANES), SHARD_COLS)

        def send(k, src, dst_slot, to):
            return pltpu.make_async_remote_copy(
                src_ref=src, dst_ref=land.at[dst_slot], send_sem=send_sems.at[k], recv_sem=recv_sems.at[k],
                device_id=to, device_id_type=MESH)

        def to_sibling():
            return send(0, w_hbm, 1, sibling)

        def to_chip(j):
            if j == 2:
                return send(3, land.at[2], 4, (*mine[1], c))
            return send(1 + j, w_hbm, 2 + j, (*mine[j], c))

        def pass_on(j):
            return send(4 + j, land.at[2 + j], 5 + j, sibling)

        def x_tile(i):
            return _SplitCopy(x_hbm.at[pl.ds(i * xt, xt), :], xbuf.at[i % 2], x_sem.at[i % 2])

        own = _SplitCopy(w_hbm, land.at[0], own_sem)
        own.start()
        x_tile(0).start()
        to_sibling().start()
        to_chip(0).start()
        for i in range(S // xt):
            if i + 1 < S // xt:
                x_tile(i + 1).start()
            x_tile(i).wait()
            xv = xbuf[i % 2]
            r = lax.rsqrt(jnp.mean(xv * xv, axis=-1, keepdims=True) + EPS)
            h_buf[i * xt:(i + 1) * xt, :] = (xv * r * nw_ref[...]).astype(BF16)
        h_out = _SplitCopy(h_buf, h_hbm, h_sem)
        h_out.start()
        own.wait()

        def multiply(slot, n_done):
            d = block_of[slot]
            out = _SplitCopy(land.at[slot], wfull_hbm.at[:, cols(d)], wout_sem.at[slot])
            out.start()
            for r in range(2):
                rows = pl.ds(r * half, half)
                zc = _SplitCopy(zstage.at[r], z_hbm.at[rows, cols(d)], z_sem.at[r])
                if n_done > 0:
                    zc.wait()
                zstage[r] = _dot(h_buf[r * half:(r + 1) * half, :], land[slot])
                zc.start()
            return out

        outs = [multiply(0, 0)]
        to_sibling().wait_recv()
        outs.append(multiply(1, 1))
        done = 2
        for j in range(3):
            to_chip(j).wait_recv()
            pass_on(j).start()
            to_chip(j).wait_send()
            if j < 2:
                to_chip(j + 1).start()
            outs.append(multiply(2 + j, done))
            pass_on(j).wait_recv()
            outs.append(multiply(5 + j, done + 1))
            done += 2
        for r in range(2):
            _SplitCopy(zstage.at[r], z_hbm.at[pl.ds(r * half, half), cols(0)], z_sem.at[r]).wait()
        for out in outs:
            out.wait()
        h_out.wait()
        to_sibling().wait_send()
        for j in range(3):
            pass_on(j).wait_send()

    any_spec = pl.BlockSpec(memory_space=pl.ANY)
    return pl.pallas_call(
        body, name="in_proj_gather",
        out_shape=[jax.ShapeDtypeStruct((S, IN_COLS), F32), jax.ShapeDtypeStruct((D, IN_COLS), BF16),
                   jax.ShapeDtypeStruct((S, D), BF16)],
        in_specs=[any_spec, pl.BlockSpec(memory_space=pltpu.VMEM), any_spec], out_specs=[any_spec] * 3,
        scratch_shapes=[pltpu.VMEM((S, D), BF16), pltpu.VMEM((8,) + slab, BF16), pltpu.VMEM((2, half, SHARD_COLS), F32),
                        pltpu.VMEM((2, xt, D), F32),
                        pltpu.SemaphoreType.DMA, pltpu.SemaphoreType.DMA, pltpu.SemaphoreType.DMA((2,)),
                        pltpu.SemaphoreType.DMA((8,)), pltpu.SemaphoreType.DMA((2,)),
                        pltpu.SemaphoreType.DMA((7,)), pltpu.SemaphoreType.DMA((7,))],
        compiler_params=_cp(),
    )(x, norm_w, w_shard)


def _exchange_sibling(ids, gb):
    n = len(gb)

    def body(*refs):
        ins, outs = refs[:n], refs[n:2 * n]
        send_sems, recv_sems = refs[2 * n:]
        x, y, c = _mesh_pos()
        sibling = (x, y, 1 - c)
        copies = []
        for i, a in enumerate(ids):
            for q in range(4):
                copies.append(pltpu.make_async_remote_copy(
                    src_ref=_shard_of(ins[i], a, 2 * q + (1 - c)), dst_ref=outs[i].at[q],
                    send_sem=send_sems.at[i * 4 + q], recv_sem=recv_sems.at[i * 4 + q],
                    device_id=sibling, device_id_type=MESH))
        for cp in copies:
            cp.start()
        for cp in copies:
            cp.wait()

    any_spec = pl.BlockSpec(memory_space=pl.ANY)
    return pl.pallas_call(
        body, name="grads_to_sibling",
        out_shape=[jax.ShapeDtypeStruct((4,) + SHARD_SHAPES[a], BF16) for a in ids],
        in_specs=[any_spec] * n, out_specs=[any_spec] * n,
        scratch_shapes=[pltpu.SemaphoreType.DMA((4 * n,)), pltpu.SemaphoreType.DMA((4 * n,))],
    )(*gb)


def _exchange_chips_steps(ins, outs, send_sems, recv_sems):
    x, y, c = _mesh_pos()
    chips = [(1 - x, y), (x, 1 - y), (1 - x, 1 - y)]
    copies = []
    for a in range(len(ins)):
        for k, chip in enumerate(chips):
            copies.append(pltpu.make_async_remote_copy(
                src_ref=ins[a].at[2 * chip[0] + chip[1]], dst_ref=outs[a].at[k],
                send_sem=send_sems.at[a * 3 + k], recv_sem=recv_sems.at[a * 3 + k],
                device_id=(*chip, c), device_id_type=MESH))

    def start():
        for cp in copies:
            cp.start()

    def end():
        for cp in copies:
            cp.wait()

    return start, end


def _gather_small_steps(small_ref, small_out, ssend, srecv, local_sem):
    x, y, c = _mesh_pos()
    me = 4 * x + 2 * y + c
    copies = []
    for r in range(1, NDEV):
        peer = (1 - x if r & 4 else x, 1 - y if r & 2 else y, 1 - c if r & 1 else c)
        copies.append(pltpu.make_async_remote_copy(
            src_ref=small_ref, dst_ref=small_out.at[me],
            send_sem=ssend.at[r - 1], recv_sem=srecv.at[r - 1],
            device_id=peer, device_id_type=MESH))
    own = pltpu.make_async_copy(small_ref, small_out.at[me], local_sem)

    def start():
        own.start()
        for cp in copies:
            cp.start()

    def end():
        for cp in copies:
            cp.wait()
        own.wait()

    return start, end


def _gather_small(small):
    def body(small_ref, small_out, ssend, srecv, local_sem):
        start, end = _gather_small_steps(small_ref, small_out, ssend, srecv, local_sem)
        start()
        end()

    any_spec = pl.BlockSpec(memory_space=pl.ANY)
    return pl.pallas_call(
        body, name="gather_small",
        out_shape=jax.ShapeDtypeStruct((NDEV,) + small.shape, F32),
        in_specs=[any_spec], out_specs=any_spec,
        scratch_shapes=[pltpu.SemaphoreType.DMA((NDEV - 1,)), pltpu.SemaphoreType.DMA((NDEV - 1,)),
                        pltpu.SemaphoreType.DMA],
    )(small)


def _shard_tiles(a):
    rows, cols = SHARD_SHAPES[a]
    tr = min(rows, 256)
    return (tr, cols), rows // tr


def _full_index(a, d, i):
    (tr, _), nt = _shard_tiles(a)
    if a in (0, 2):
        return (i, d)
    return (d * nt + i, 0)


def _cast_bf16(x, name):
    rows, cols = x.shape
    tr = min(rows, 256)

    def body(x_ref, o_ref):
        o_ref[...] = x_ref[...].astype(BF16)

    return pl.pallas_call(
        body, name=name, out_shape=jax.ShapeDtypeStruct(x.shape, BF16), grid=(rows // tr,),
        in_specs=[pl.BlockSpec((tr, cols), lambda i: (i, 0))],
        out_specs=pl.BlockSpec((tr, cols), lambda i: (i, 0)),
        compiler_params=_cp(("parallel",)),
    )(x)


def _chip_partials(a, g_full, r1, core):
    tile, nt = _shard_tiles(a)

    def body(c_ref, g_ref, r_ref, o_ref):
        o_ref[0] = (g_ref[...] + r_ref[0].astype(F32)).astype(BF16)

    grid_spec = pltpu.PrefetchScalarGridSpec(
        num_scalar_prefetch=1, grid=(4, nt),
        in_specs=[pl.BlockSpec(tile, lambda q, i, c: _full_index(a, 2 * q + c[0], i)),
                  pl.BlockSpec((1,) + tile, lambda q, i, c: (q, i, 0))],
        out_specs=pl.BlockSpec((1,) + tile, lambda q, i, c: (q, i, 0)))
    return pl.pallas_call(
        body, name=f"chip_partials_{a}", grid_spec=grid_spec,
        out_shape=jax.ShapeDtypeStruct((4,) + SHARD_SHAPES[a], BF16),
        compiler_params=_cp(("parallel", "parallel")),
    )(core, g_full, r1)


def _adam(w, g, m, v):
    m = ADAM_B1 * m + (1.0 - ADAM_B1) * g
    v = ADAM_B2 * v + (1.0 - ADAM_B2) * (g * g)
    m_hat = m / (1.0 - ADAM_B1 ** ADAM_STEP)
    v_hat = v / (1.0 - ADAM_B2 ** ADAM_STEP)
    delta = -ADAM_LR * (m_hat / (jnp.sqrt(v_hat) + ADAM_EPS) + ADAM_WD * w)
    return delta, m, v


def _reduce_and_update(a, w, m, v, g_full, r1, r2, pos):
    tile, nt = _shard_tiles(a)

    def body(p_ref, w_ref, m_ref, v_ref, g_ref, r1_ref, r2_ref, go_ref, do_ref, mo_ref, vo_ref):
        g = g_ref[...] + r1_ref[0].astype(F32)
        g = g + r2_ref[0].astype(F32)
        g = g + r2_ref[1].astype(F32)
        g = g + r2_ref[2].astype(F32)
        delta, m_new, v_new = _adam(w_ref[...], g, m_ref[...], v_ref[...])
        go_ref[...] = g
        do_ref[...] = delta
        mo_ref[...] = m_new
        vo_ref[...] = v_new

    own = pl.BlockSpec(tile, lambda i, p: (i, 0))
    grid_spec = pltpu.PrefetchScalarGridSpec(
        num_scalar_prefetch=1, grid=(nt,),
        in_specs=[own, own, own,
                  pl.BlockSpec(tile, lambda i, p: _full_index(a, p[0], i)),
                  pl.BlockSpec((1,) + tile, lambda i, p: (p[1], i, 0)),
                  pl.BlockSpec((3,) + tile, lambda i, p: (0, i, 0))],
        out_specs=[own] * 4)
    shp = jax.ShapeDtypeStruct(w.shape, F32)
    return pl.pallas_call(
        body, name=f"reduce_update_{a}", grid_spec=grid_spec, out_shape=[shp] * 4,
        compiler_params=_cp(("parallel",)),
    )(pos, w, m, v, g_full, r1, r2)


def _reduce_own_and_update(w, m, v, g_chip, r2):
    tile, nt = _shard_tiles(0)

    def body(w_ref, m_ref, v_ref, g_ref, r2_ref, go_ref, do_ref, mo_ref, vo_ref):
        g = g_ref[...] + r2_ref[0].astype(F32)
        g = g + r2_ref[1].astype(F32)
        delta, m_new, v_new = _adam(w_ref[...], g, m_ref[...], v_ref[...])
        go_ref[...] = g
        do_ref[...] = delta
        mo_ref[...] = m_new
        vo_ref[...] = v_new

    own = pl.BlockSpec(tile, lambda i: (i, 0))
    shp = jax.ShapeDtypeStruct(w.shape, F32)
    return pl.pallas_call(
        body, name="reduce_update_0", grid=(nt,), out_shape=[shp] * 4,
        in_specs=[own, own, own, own, pl.BlockSpec((2,) + tile, lambda i: (0, i, 0))], out_specs=[own] * 4,
        compiler_params=_cp(("parallel",)),
    )(w, m, v, g_chip, r2)


def _small_update(gathered, norm_w, lb_logits, hnw, fnw, moments):
    m_nw, m_lb, m_hn, m_fn, v_nw, v_lb, v_hn, v_fn = moments

    def body(g_ref, nw, lb, hn, fn, mnw, mlb, mhn, mfn, vnw, vlb, vhn, vfn,
             loss_o, g_nw, g_lb, g_hn, g_fn, d_nw, d_lb, d_hn, d_fn,
             mo_nw, mo_lb, mo_hn, mo_fn, vo_nw, vo_lb, vo_hn, vo_fn):
        tot = g_ref[0]
        for d in range(1, NDEV):
            tot = tot + g_ref[d]
        loss_o[...] = tot[4:5, 0:LANES]
        logits = lb[...]
        lbv = jax.nn.sigmoid(logits[0:1] - logits[1:2])
        chain = tot[1:2] * lbv * (1.0 - lbv)
        grads = (tot[0:1], jnp.concatenate([chain, -chain], axis=0), tot[2:3, 0:LANES], tot[3:4])
        outs = ((nw, mnw, vnw, g_nw, d_nw, mo_nw, vo_nw), (lb, mlb, vlb, g_lb, d_lb, mo_lb, vo_lb),
                (hn, mhn, vhn, g_hn, d_hn, mo_hn, vo_hn), (fn, mfn, vfn, g_fn, d_fn, mo_fn, vo_fn))
        for g, (w_r, m_r, v_r, g_o, d_o, m_o, v_o) in zip(grads, outs):
            delta, m_new, v_new = _adam(w_r[...], g, m_r[...], v_r[...])
            g_o[...] = g
            d_o[...] = delta
            m_o[...] = m_new
            v_o[...] = v_new

    shapes = [norm_w.shape, lb_logits.shape, hnw.shape, fnw.shape]
    out_shape = [jax.ShapeDtypeStruct((1, LANES), F32)] + [jax.ShapeDtypeStruct(s, F32) for s in shapes] * 4
    return pl.pallas_call(body, name="small_update", out_shape=out_shape, compiler_params=_cp())(
        gathered, norm_w, lb_logits, hnw, fnw, m_nw, m_lb, m_hn, m_fn, v_nw, v_lb, v_hn, v_fn)


def _block_tri(n, block, upper=False):
    r = lax.broadcasted_iota(jnp.int32, (n, n), 0)
    c = lax.broadcasted_iota(jnp.int32, (n, n), 1)
    keep = (c >= r) if upper else (c <= r)
    return jnp.where(keep & ((r // block) == (c // block)), 1.0, 0.0).astype(BF16)


def _tril_mask(n):
    r = lax.broadcasted_iota(jnp.int32, (n, n), 0)
    c = lax.broadcasted_iota(jnp.int32, (n, n), 1)
    return c <= r


def _chunk_scores(q, k, b, bex, r0, mask):
    parts, qs_l, ks_l, ek_l, eq_l = [], [], [], [], []
    for i in range(CHUNK // SUB):
        ri = slice(r0 + SUB * i, r0 + SUB * (i + 1))
        seen = slice(r0, r0 + SUB * (i + 1))
        base = bex[r0 + SUB * i:r0 + SUB * i + 1]
        eq = jnp.exp(b[ri] - base)
        ek = jnp.exp(jnp.minimum(base - b[seen], EXP_CLAMP))
        ks = k[seen] * ek
        if i + 1 < CHUNK // SUB:
            rest = jnp.zeros((CHUNK - SUB * (i + 1), 128), F32)
            ek, ks = jnp.concatenate([ek, rest], axis=0), jnp.concatenate([ks, rest], axis=0)
        qs = q[ri] * eq
        parts.append(_dot_nt(qs.astype(BF16), ks.astype(BF16)))
        qs_l.append(qs)
        ks_l.append(ks)
        ek_l.append(ek)
        eq_l.append(eq)
    return jnp.where(mask, jnp.concatenate(parts, axis=0), 0.0), qs_l, ks_l, ek_l, eq_l


def _hgrn_cols(hq, hf, hi, lb):
    sg = jax.nn.sigmoid(hf)
    f = lb + (1.0 - lb) * sg
    g = jnp.log(f)
    b = _dot_ones(_block_tri(HBLK, CHUNK), g)
    return _silu(hq), 1.0 - f, g, hi, sg, f, b


GATHER_IDS = (1, 2, 3)


def _hgrn_fwd(z, lbv, hnw):
    ntb, nch = S // HBLK, HBLK // CHUNK

    def body(hq_ref, hf_ref, hi_ref, hg_ref, lb_ref, hnw_ref, o_ref, oa_ref, st_ref, state):
        @pl.when(pl.program_id(0) == 0)
        def _():
            state[...] = jnp.zeros_like(state)

        q_a, k_a, g_a, v_a, _, _, b_a = _hgrn_cols(hq_ref[...], hf_ref[...], hi_ref[...], lb_ref[...])
        bex_a = b_a - g_a
        eb_a = jnp.exp(b_a)
        mask = _tril_mask(CHUNK)
        hg = hg_ref[...]
        w = hnw_ref[...]
        for h in range(HEADS):
            cols = slice(128 * h, 128 * h + 128)
            q, k, v, b, bex, eb = q_a[:, cols], k_a[:, cols], v_a[:, cols], b_a[:, cols], bex_a[:, cols], eb_a[:, cols]
            st = state[h]
            outs = []
            for c in range(nch):
                r0 = c * CHUNK
                rows = slice(r0, r0 + CHUNK)
                a = _chunk_scores(q, k, b, bex, r0, mask)[0]
                vb = v[rows].astype(BF16)
                b_last = b[r0 + CHUNK - 1:r0 + CHUNK]
                qe = (q[rows] * eb[rows]).astype(BF16)
                outs.append(_dot(a.astype(BF16), vb) + _dot_nt(qe, st.astype(BF16)))
                st_ref[h, c] = st
                ke = (k[rows] * jnp.exp(b_last - b[rows])).astype(BF16)
                st = st * jnp.exp(b_last) + _dot_tn(vb, ke)
            state[h] = st
            o = jnp.concatenate(outs, axis=0)
            o_ref[:, cols] = o
            r = lax.rsqrt(jnp.mean(o * o, axis=-1, keepdims=True) + EPS)
            oa_ref[:, cols] = (o * r * w * _silu(hg[:, cols])).astype(BF16)

    def zcol(j):
        return pl.BlockSpec((HBLK, D), lambda t: (t, j))

    out_blk = pl.BlockSpec((HBLK, D), lambda t: (t, 0))
    return pl.pallas_call(
        body, name="hgrn_fwd", grid=(ntb,),
        out_shape=[jax.ShapeDtypeStruct((S, D), F32), jax.ShapeDtypeStruct((S, D), BF16),
                   jax.ShapeDtypeStruct((HEADS, S // CHUNK, 128, 128), F32)],
        in_specs=[zcol(0), zcol(1), zcol(2), zcol(3),
                  pl.BlockSpec((1, D), lambda t: (0, 0)), pl.BlockSpec((1, 128), lambda t: (0, 0))],
        out_specs=[out_blk, out_blk, pl.BlockSpec((HEADS, nch, 128, 128), lambda t: (0, t, 0, 0))],
        scratch_shapes=[pltpu.VMEM((HEADS, 128, 128), F32)],
        compiler_params=_cp(("arbitrary",)),
    )(z, z, z, z, lbv, hnw)


def _half_mask():
    lane = lax.broadcasted_iota(jnp.int32, (1, LANES), 1)
    return (lane % 64) < 32


def _rope(t, cc, ss, first_half):
    partner = jnp.where(first_half, pltpu.roll(t, 96, 1), pltpu.roll(t, 32, 1))
    return t * cc + partner * ss


def _attn_masks():
    i = lax.broadcasted_iota(jnp.int32, (128, 128), 0)
    j = lax.broadcasted_iota(jnp.int32, (128, 128), 1)
    return j >= i, j <= i


def _to_residues_dyn(g, dst, src, row0=0, dtype=None):
    for gi, dil in enumerate((1, 4, 16)):
        m = S // dil

        @pl.when(g == gi)
        def _(dil=dil, m=m):
            for r in range(dil):
                v = src[...] if dil == 1 else src[pl.ds(r, m, stride=dil), :]
                if dtype is not None:
                    v = v.astype(dtype)
                dst[row0 + r * m:row0 + (r + 1) * m, 0:LANES] = v


def _from_residues_dyn(g, dst, src, row0=0):
    for gi, dil in enumerate((1, 4, 16)):
        m = S // dil

        @pl.when(g == gi)
        def _(dil=dil, m=m):
            for r in range(dil):
                v = src[row0 + r * m:row0 + (r + 1) * m, :]
                if dil == 1:
                    dst[...] = v
                else:
                    dst[pl.ds(r, m, stride=dil), :] = v


def _group_blocks(g):
    return jnp.where(g == 0, 16, jnp.where(g == 1, 4, 1))


def _attn_in_specs(extra):
    def zcol(off):
        return pl.BlockSpec((S, LANES), lambda p, g: (0, off + 4 * g + p))

    per_pair = pl.BlockSpec((S, LANES), lambda p, g: (0, p))
    const = pl.BlockSpec((S, LANES), lambda p, g: (0, 0))
    return [zcol(32), zcol(44), zcol(56), pl.BlockSpec((S, LANES), lambda p, g: (0, 68 + p)), const, const] + [per_pair] * extra


def _attn_fwd(z, cc, ss, shards):
    def body(q_ref, k_ref, v_ref, ag_ref, cc_ref, ss_ref, s0, s1, s2,
             ob_ref, lse_ref, obg_ref, qsb_ref, ksb_ref, vsb_ref, f0, f1, f2,
             tmp, qs, ks, vx, og, mg, lg, o_t, m_t, l_t, o_acc, m_acc, l_acc, send_sems, recv_sems, local_sems):
        g = pl.program_id(1)
        step = pl.program_id(0) * 3 + g
        start, middle, end = _allgather_steps(GATHER_IDS, (s0, s1, s2), (f0, f1, f2), send_sems, recv_sems, local_sems)
        pl.when(step == 0)(start)
        pl.when(step == 6)(middle)
        first_half = _half_mask()
        prev_ok, cur_ok = _attn_masks()
        lane = lax.broadcasted_iota(jnp.int32, (1, LANES), 1)
        heads = (lane < 64, lane >= 64)
        nblk = _group_blocks(g)

        @pl.when(g == 0)
        def _():
            ks[0:ATT_PAD, :] = jnp.zeros((ATT_PAD, LANES), BF16)
            vx[0:ATT_PAD, 0:LANES] = jnp.zeros((ATT_PAD, LANES), BF16)
            vx[:, LANES:2 * LANES] = jnp.ones((ATT_PAD + S, LANES), BF16)

        tmp[...] = _rope(q_ref[...], cc_ref[...], ss_ref[...], first_half) * ATT_SCALE
        _to_residues_dyn(g, qs, tmp)
        tmp[...] = _rope(k_ref[...], cc_ref[...], ss_ref[...], first_half)
        _to_residues_dyn(g, ks, tmp, ATT_PAD, BF16)
        _to_residues_dyn(g, vx, v_ref, ATT_PAD, BF16)

        def unit(u, carry):
            start = pl.multiple_of(u * 128, 128)
            cur = pl.ds(start, 128)
            pm = prev_ok & ((u & (nblk - 1)) != 0)
            qu = qs[cur, :]
            kcat = ks[pl.ds(start, 256), :]
            vext = vx[pl.ds(start, 256), :]
            o_u = m_u = l_u = None
            for hh in range(2):
                s = _dot_nt(jnp.where(heads[hh], qu, 0.0).astype(BF16), kcat)
                sp = jnp.where(pm, s[:, 0:128], -jnp.inf)
                sc = jnp.where(cur_ok, s[:, 128:256], -jnp.inf)
                m = jnp.max(jnp.maximum(sp, sc), axis=-1, keepdims=True)
                p = jnp.concatenate([jnp.exp(sp - m), jnp.exp(sc - m)], axis=1).astype(BF16)
                ol = _dot(p, vext)
                mb = jnp.broadcast_to(m, (128, LANES))
                if hh == 0:
                    o_u, l_u, m_u = ol[:, 0:128], ol[:, 128:256], mb
                else:
                    o_u = jnp.where(heads[1], ol[:, 0:128], o_u)
                    l_u = jnp.where(heads[1], ol[:, 128:256], l_u)
                    m_u = jnp.where(heads[1], mb, m_u)
            og[cur, :] = o_u
            mg[cur, :] = m_u
            lg[cur, :] = l_u
            return carry

        lax.fori_loop(0, 16, unit, 0, unroll=16)
        qsb_ref[0] = qs[...].astype(BF16)
        ksb_ref[0] = ks[...]
        vsb_ref[0] = vx[:, 0:LANES]
        _from_residues_dyn(g, o_t, og)
        _from_residues_dyn(g, m_t, mg)
        _from_residues_dyn(g, l_t, lg)

        @pl.when(g == 0)
        def _():
            o_acc[...] = o_t[...]
            m_acc[...] = m_t[...]
            l_acc[...] = l_t[...]

        @pl.when(g > 0)
        def _():
            m_new = jnp.maximum(m_acc[...], m_t[...])
            wa, wb = jnp.exp(m_acc[...] - m_new), jnp.exp(m_t[...] - m_new)
            o_acc[...] = o_acc[...] * wa + o_t[...] * wb
            l_acc[...] = l_acc[...] * wa + l_t[...] * wb
            m_acc[...] = m_new

        @pl.when(g == 2)
        def _():
            ob = o_acc[...] / l_acc[...]
            ob_ref[...] = ob
            lse_ref[...] = m_acc[...] + jnp.log(l_acc[...])
            obg_ref[...] = (ob * _silu(ag_ref[...])).astype(BF16)

        pl.when(step == 11)(end)

    n = len(GATHER_IDS)
    any_spec = pl.BlockSpec(memory_space=pl.ANY)
    blk = pl.BlockSpec((S, LANES), lambda p, g: (0, p))
    buf = pltpu.VMEM((S, LANES), F32)
    return pl.pallas_call(
        body, name="attn_fwd", grid=(4, 3),
        out_shape=[jax.ShapeDtypeStruct((S, 512), F32), jax.ShapeDtypeStruct((S, 512), F32),
                   jax.ShapeDtypeStruct((S, 512), BF16), jax.ShapeDtypeStruct((3, S, 512), BF16),
                   jax.ShapeDtypeStruct((3, ATT_PAD + S, 512), BF16), jax.ShapeDtypeStruct((3, ATT_PAD + S, 512), BF16)]
        + [jax.ShapeDtypeStruct(FULL_SHAPES[a], BF16) for a in GATHER_IDS],
        in_specs=_attn_in_specs(0) + [any_spec] * n,
        out_specs=[blk, blk, blk, pl.BlockSpec((1, S, LANES), lambda p, g: (g, 0, p)),
                   pl.BlockSpec((1, ATT_PAD + S, LANES), lambda p, g: (g, 0, p)),
                   pl.BlockSpec((1, ATT_PAD + S, LANES), lambda p, g: (g, 0, p))] + [any_spec] * n,
        scratch_shapes=[buf, buf, pltpu.VMEM((ATT_PAD + S, LANES), BF16), pltpu.VMEM((ATT_PAD + S, 2 * LANES), BF16)]
        + [buf] * 9 + [pltpu.SemaphoreType.DMA((7 * n,)), pltpu.SemaphoreType.DMA((7 * n,)), pltpu.SemaphoreType.DMA((n,))],
        compiler_params=_cp(("arbitrary", "arbitrary")),
    )(z, z, z, z, cc, ss, *shards)


def _tail(x, o_a, o_bg, z, target, w_a, w_b, w_out, fnw):
    tm = 256

    def body(x_ref, oa_ref, ob_ref, gpa_ref, gpb_ref, t_ref, wa_ref, wb_ref, wo_ref, fnw_ref,
             dx2_ref, dx2b_ref, dz_hbm, doa_ref, dob_ref, mg_ref, dya_ref, dyb_ref, small_ref, dgp, dgp_sem):
        step = pl.program_id(0)
        slot = step % 2

        def dgp_copy(at_step, at_slot):
            return pltpu.make_async_copy(
                dgp.at[at_slot], dz_hbm.at[pl.ds(pl.multiple_of(at_step * tm, tm), tm), pl.ds(DZ_GATES, 2 * D)],
                dgp_sem.at[at_slot])

        @pl.when(step == 0)
        def _():
            small_ref[...] = jnp.zeros_like(small_ref)

        @pl.when(step >= 2)
        def _():
            dgp_copy(step - 2, slot).wait()

        wa, wb, wo = wa_ref[...], wb_ref[...], wo_ref[...]
        y_a = _dot(oa_ref[...], wa)
        y_b = _dot(ob_ref[...], wb)
        ga = jax.nn.sigmoid(gpa_ref[...])
        gb = jax.nn.sigmoid(gpb_ref[...])
        merged = (ga * y_a + gb * y_b).astype(BF16)
        x2 = x_ref[...] + _dot(merged, wo)
        r2 = lax.rsqrt(jnp.mean(x2 * x2, axis=-1, keepdims=True) + EPS)
        n2 = x2 * r2
        fw = fnw_ref[...]
        err = n2 * fw - t_ref[...]
        loss = 0.5 * jnp.sum(jnp.sum(err * err, axis=-1, keepdims=True), axis=0, keepdims=True) / D
        dy = err * (1.0 / D)
        g_fnw = jnp.sum(dy * n2, axis=0, keepdims=True)
        dn = dy * fw
        dx2 = r2 * (dn - n2 * jnp.mean(dn * n2, axis=-1, keepdims=True))
        dx2b = dx2.astype(BF16)
        dmerged = _dot_nt(dx2b, wo)
        dy_a = (dmerged * ga).astype(BF16)
        dy_b = (dmerged * gb).astype(BF16)
        dx2_ref[...] = dx2
        dx2b_ref[...] = dx2b
        dgp[slot, :, 0:D] = (dmerged * y_a * ga * (1.0 - ga)).astype(BF16)
        dgp[slot, :, D:2 * D] = (dmerged * y_b * gb * (1.0 - gb)).astype(BF16)
        dgp_copy(step, slot).start()
        doa_ref[...] = _dot_nt(dy_a, wa)
        dob_ref[...] = _dot_nt(dy_b, wb)
        mg_ref[...] = merged
        dya_ref[...] = dy_a
        dyb_ref[...] = dy_b
        small_ref[0:1, :] += g_fnw
        small_ref[1:2, :] += jnp.broadcast_to(loss, (1, D))

        @pl.when(step == S // tm - 1)
        def _():
            dgp_copy(step - 1, 1 - slot).wait()
            dgp_copy(step, slot).wait()

    def rows(cols, off=0):
        return pl.BlockSpec((tm, cols), lambda i: (i, off))

    def whole(shape):
        return pl.BlockSpec(shape, lambda i: (0, 0))

    return pl.pallas_call(
        body, name="tail", grid=(S // tm,),
        out_shape=[jax.ShapeDtypeStruct((S, D), F32), jax.ShapeDtypeStruct((S, D), BF16),
                   jax.ShapeDtypeStruct((S, IN_COLS), BF16), jax.ShapeDtypeStruct((S, D), F32),
                   jax.ShapeDtypeStruct((S, 512), F32), jax.ShapeDtypeStruct((S, D), BF16),
                   jax.ShapeDtypeStruct((S, D), BF16), jax.ShapeDtypeStruct((S, D), BF16),
                   jax.ShapeDtypeStruct((8, D), F32)],
        in_specs=[rows(D), rows(D), rows(512), rows(D, 9), rows(D, 10), rows(D),
                  whole((D, D)), whole((512, D)), whole((D, D)), whole((1, D))],
        out_specs=[rows(D), rows(D), pl.BlockSpec(memory_space=pl.ANY), rows(D), rows(512), rows(D), rows(D),
                   rows(D), whole((8, D))],
        scratch_shapes=[pltpu.VMEM((2, tm, 2 * D), BF16), pltpu.SemaphoreType.DMA((2,))],
        compiler_params=_cp(("arbitrary",)),
    )(x, o_a, o_bg, z, z, target, w_a, w_b, w_out, fnw)


def _tn_matmul(a, b, name):
    m, n = a.shape[1], b.shape[1]
    tn = 512

    def body(a_ref, b_ref, o_ref, ob_ref):
        acc = _dot_tn(a_ref[...], b_ref[...])
        o_ref[...] = acc
        ob_ref[...] = acc.astype(BF16)

    out_blk = pl.BlockSpec((m, tn), lambda j: (0, j))
    return pl.pallas_call(
        body, name=name, grid=(n // tn,),
        out_shape=[jax.ShapeDtypeStruct((m, n), F32), jax.ShapeDtypeStruct((m, n), BF16)],
        in_specs=[pl.BlockSpec((S, m), lambda j: (0, 0)), pl.BlockSpec((S, tn), lambda j: (0, j))],
        out_specs=[out_blk, out_blk],
        compiler_params=_cp(("parallel",)),
    )(a, b)


def _hgrn_bwd(z, o, do_a, states, lbv, hnw, partials, dz):
    ntb, nch = S // HBLK, HBLK // CHUNK
    n = len(GATHER_IDS)

    def body(hq_ref, hf_ref, hi_ref, hg_ref, o_ref, doa_ref, st_ref, lb_ref, hnw_ref, p0, p1, p2, dz_in,
             dz_ref, glb_ref, ghn_ref, e0, e1, e2, dstate, send_sems, recv_sems):
        dhq_ref, dhf_ref, dhi_ref, dhg_ref = (dz_ref.at[:, pl.ds(j * D, D)] for j in range(4))
        start, end = _exchange_chips_steps((p0, p1, p2), (e0, e1, e2), send_sems, recv_sems)

        @pl.when(pl.program_id(0) == 0)
        def _():
            dstate[...] = jnp.zeros_like(dstate)
            glb_ref[...] = jnp.zeros_like(glb_ref)
            ghn_ref[...] = jnp.zeros_like(ghn_ref)
            start()

        lb_a = lb_ref[...]
        hq_a, hg_a = hq_ref[...], hg_ref[...]
        q_a, k_a, g_a, v_a, sg_a, f_a, b_a = _hgrn_cols(hq_a, hf_ref[...], hi_ref[...], lb_a)
        bex_a = b_a - g_a
        eb_a = jnp.exp(b_a)
        w = hnw_ref[...]
        mask = _tril_mask(CHUNK)
        upper = _block_tri(CHUNK, CHUNK, upper=True)
        for h in range(HEADS):
            cols = slice(128 * h, 128 * h + 128)
            q, k, v, b, bex, eb = q_a[:, cols], k_a[:, cols], v_a[:, cols], b_a[:, cols], bex_a[:, cols], eb_a[:, cols]
            hq, hg, sg, f, lb = hq_a[:, cols], hg_a[:, cols], sg_a[:, cols], f_a[:, cols], lb_a[:, cols]
            ov, doa = o_ref[:, cols], doa_ref[:, cols]
            r = lax.rsqrt(jnp.mean(ov * ov, axis=-1, keepdims=True) + EPS)
            n = ov * r
            sil = _silu(hg)
            dhg_ref[:, cols] = (doa * n * w * _dsilu(hg)).astype(BF16)
            ghn_ref[h] += jnp.sum(doa * sil * n, axis=0, keepdims=True)
            dn = doa * sil * w
            do = r * (dn - n * jnp.mean(dn * n, axis=-1, keepdims=True))

            dst = dstate[h]
            dq_l, dk_l, dv_l, dg_l = [None] * nch, [None] * nch, [None] * nch, [None] * nch
            for c in reversed(range(nch)):
                r0 = c * CHUNK
                rows = slice(r0, r0 + CHUNK)
                st = st_ref[h, c]
                bc, kc, qc = b[rows], k[rows], q[rows]
                vb, dob = v[rows].astype(BF16), do[rows].astype(BF16)
                b_last = bc[CHUNK - 1:CHUNK]
                e_last = jnp.exp(b_last)
                ekl = jnp.exp(b_last - bc)
                dstb = dst.astype(BF16)
                a, qs_l, ks_l, ek_l, eq_l = _chunk_scores(q, k, b, bex, r0, mask)
                da = jnp.where(mask, _dot_nt(dob, vb), 0.0)
                dv_l[c] = _dot_tn(a.astype(BF16), dob) + _dot_nt((kc * ekl).astype(BF16), dstb)
                dq_inter = _dot(dob, st.astype(BF16)) * eb[rows]
                dk_state = _dot(vb, dstb) * ekl
                dq_parts, dk_intra = [], jnp.zeros((CHUNK, 128), F32)
                dab = da.astype(BF16)
                for i in range(CHUNK // SUB):
                    da_i = dab[SUB * i:SUB * (i + 1)]
                    ks_hi, ks_lo = _split2(ks_l[i])
                    qs_hi, qs_lo = _split2(qs_l[i])
                    dq_parts.append((_dot(da_i, ks_hi) + _dot(da_i, ks_lo)) * eq_l[i])
                    dk_intra = dk_intra + (_dot_tn(da_i, qs_hi) + _dot_tn(da_i, qs_lo)) * ek_l[i]
                dq = jnp.concatenate(dq_parts, axis=0) + dq_inter
                dk = dk_intra + dk_state
                last = (e_last * jnp.sum(st * dst, axis=0, keepdims=True)
                        + jnp.sum(kc * dk_state, axis=0, keepdims=True))
                dg_l[c] = _dot_ones(upper, qc * dq - kc * dk) + last
                dq_l[c], dk_l[c] = dq, dk
                dst = dst * e_last + _dot_tn(dob, (qc * eb[rows]).astype(BF16))
            dstate[h] = dst
            dq, dk = jnp.concatenate(dq_l, axis=0), jnp.concatenate(dk_l, axis=0)
            dg, dv = jnp.concatenate(dg_l, axis=0), jnp.concatenate(dv_l, axis=0)
            dhq_ref[:, cols] = (dq * _dsilu(hq)).astype(BF16)
            dhi_ref[:, cols] = dv.astype(BF16)
            df = dg / f - dk
            dhf_ref[:, cols] = (df * (1.0 - lb) * sg * (1.0 - sg)).astype(BF16)
            glb_ref[:, cols] += jnp.sum(df * (1.0 - sg), axis=0, keepdims=True)

        pl.when(pl.program_id(0) == ntb - 1)(end)

    def rev(t):
        return ntb - 1 - t

    def zcol(j):
        return pl.BlockSpec((HBLK, D), lambda t: (rev(t), j))

    blk = pl.BlockSpec((HBLK, D), lambda t: (rev(t), 0))
    any_spec = pl.BlockSpec(memory_space=pl.ANY)
    return pl.pallas_call(
        body, name="hgrn_bwd", grid=(ntb,),
        out_shape=[jax.ShapeDtypeStruct((S, IN_COLS), BF16)]
        + [jax.ShapeDtypeStruct((1, D), F32), jax.ShapeDtypeStruct((HEADS, 1, 128), F32)]
        + [jax.ShapeDtypeStruct((3,) + SHARD_SHAPES[a], BF16) for a in GATHER_IDS],
        in_specs=[zcol(0), zcol(1), zcol(2), zcol(3), blk, blk,
                  pl.BlockSpec((HEADS, nch, 128, 128), lambda t: (0, rev(t), 0, 0)),
                  pl.BlockSpec((1, D), lambda t: (0, 0)), pl.BlockSpec((1, 128), lambda t: (0, 0))]
        + [any_spec] * (n + 1),
        out_specs=[pl.BlockSpec((HBLK, DZ_ATT), lambda t: (rev(t), 0)), pl.BlockSpec((1, D), lambda t: (0, 0)),
                   pl.BlockSpec((HEADS, 1, 128), lambda t: (0, 0, 0))] + [any_spec] * n,
        scratch_shapes=[pltpu.VMEM((HEADS, 128, 128), F32), pltpu.SemaphoreType.DMA((3 * n,)),
                        pltpu.SemaphoreType.DMA((3 * n,))],
        input_output_aliases={9 + n: 0},
        compiler_params=_cp(("arbitrary",)),
    )(z, z, z, z, o, do_a, states, lbv, hnw, *partials, dz)


def _attn_bwd(z, qsb, ksb, vsb, cc, ss, ob, lse, do_bg, dz):
    def body(qs, ks, vs, ag_ref, cc_ref, ss_ref, ob_ref, lse_ref, dobg_ref, dz_in, dz_hbm,
             tmp, dos, dqs, dks, dvs, dkp, dvp, do_t, ls0_t, ls1_t, dl0_t, dl1_t, ls0, ls1, dl0, dl1,
             stage, stage_sem):
        pair, g = pl.program_id(0), pl.program_id(1)

        def out_copy(j):
            tile = DZ_ATT // LANES + (36 + pair if j == 3 else 12 * j + 4 * g + pair)
            return pltpu.make_async_copy(
                stage.at[j], dz_hbm.at[:, pl.ds(pl.multiple_of(tile * LANES, LANES), LANES)], stage_sem.at[j])

        def restage(j, value):
            pl.when(pair * 3 + g > 0)(lambda: out_copy(j).wait())
            stage[j] = value
            out_copy(j).start()

        pl.when(g == 2)(lambda: out_copy(3).wait())
        first_half = _half_mask()
        prev_ok, cur_ok = _attn_masks()
        lane = lax.broadcasted_iota(jnp.int32, (1, LANES), 1)
        heads = (lane < 64, lane >= 64)
        nblk = _group_blocks(g)
        cc_v, ss_v = cc_ref[...], ss_ref[...]

        @pl.when(g == 0)
        def _():
            ag, obv, dobg = ag_ref[...], ob_ref[...], dobg_ref[...]
            stage[3] = (dobg * obv * _dsilu(ag)).astype(BF16)
            out_copy(3).start()
            dob = dobg * _silu(ag)
            do_t[...] = dob
            prod = dob * obv
            dl = jnp.concatenate(
                [jnp.broadcast_to(jnp.sum(prod[:, 0:64], axis=-1, keepdims=True), (S, 64)),
                 jnp.broadcast_to(jnp.sum(prod[:, 64:128], axis=-1, keepdims=True), (S, 64))], axis=1)
            dl0_t[...] = dl

        _to_residues_dyn(g, dos, do_t)
        _to_residues_dyn(g, ls0, lse_ref)
        _to_residues_dyn(g, dl0, dl0_t)

        def unit(u, carry):
            start = pl.multiple_of(u * 128, 128)
            cur = pl.ds(start, 128)
            both = pl.ds(start, 256)
            pm = prev_ok & ((u & (nblk - 1)) != 0)
            qu, dou = qs[0, cur, :], dos[cur, :]
            kcat, vcat = ks[0, both, :], vs[0, both, :]
            dq_u = None
            q_l, do_l, ds_l, p_l = [], [], [], []
            ls_u, dl_u = ls0[cur, :], dl0[cur, :]
            ls_sw, dl_sw = pltpu.roll(ls_u, 64, 1), pltpu.roll(dl_u, 64, 1)
            for hh in range(2):
                q_h = jnp.where(heads[hh], qu, jnp.zeros((), BF16))
                do_h = jnp.where(heads[hh], dou, 0.0).astype(BF16)
                s = _dot_nt(q_h, kcat)
                dp = _dot_nt(do_h, vcat)
                lse_h = jnp.where(heads[hh], ls_u, ls_sw)
                dl_h = jnp.where(heads[hh], dl_u, dl_sw)
                pp = jnp.where(pm, jnp.exp(s[:, 0:128] - lse_h), 0.0)
                pc = jnp.where(cur_ok, jnp.exp(s[:, 128:256] - lse_h), 0.0)
                ds = jnp.concatenate([pp * (dp[:, 0:128] - dl_h), pc * (dp[:, 128:256] - dl_h)], axis=1).astype(BF16)
                dq = _dot(ds, kcat)
                dq_u = dq if hh == 0 else jnp.where(heads[1], dq, dq_u)
                q_l.append(q_h)
                do_l.append(do_h)
                ds_l.append(ds)
                p_l.append(jnp.concatenate([pp, pc], axis=1).astype(BF16))
            dkcat = _dot_tn(jnp.concatenate(ds_l, axis=0), jnp.concatenate(q_l, axis=0))
            dvcat = _dot_tn(jnp.concatenate(p_l, axis=0), jnp.concatenate(do_l, axis=0))
            dkp[cur, :] = dkcat[0:128]
            dks[cur, :] = dkcat[128:256]
            dvp[cur, :] = dvcat[0:128]
            dvs[cur, :] = dvcat[128:256]
            dqs[cur, :] = dq_u
            return carry

        lax.fori_loop(0, 16, unit, 0, unroll=ATT_UNROLL)
        dks[0:S - 128, :] += dkp[128:S, :]
        dvs[0:S - 128, :] += dvp[128:S, :]
        _from_residues_dyn(g, tmp, dqs)
        restage(0, (_rope(tmp[...], cc_v, -ss_v, first_half) * ATT_SCALE).astype(BF16))
        _from_residues_dyn(g, tmp, dks)
        restage(1, _rope(tmp[...], cc_v, -ss_v, first_half).astype(BF16))
        _from_residues_dyn(g, tmp, dvs)
        restage(2, tmp[...].astype(BF16))

        @pl.when(pair * 3 + g == 11)
        def _():
            for j in range(3):
                out_copy(j).wait()

    any_spec = pl.BlockSpec(memory_space=pl.ANY)
    buf = pltpu.VMEM((S, LANES), F32)
    padded_b = pltpu.VMEM((ATT_PAD + S, LANES), BF16)
    return pl.pallas_call(
        body, name="attn_bwd", grid=(4, 3),
        out_shape=jax.ShapeDtypeStruct((S, IN_COLS), BF16),
        in_specs=[pl.BlockSpec((1, S, LANES), lambda p, g: (g, 0, p)),
                  pl.BlockSpec((1, ATT_PAD + S, LANES), lambda p, g: (g, 0, p)),
                  pl.BlockSpec((1, ATT_PAD + S, LANES), lambda p, g: (g, 0, p))] + _attn_in_specs(3)[3:] + [any_spec],
        out_specs=any_spec,
        scratch_shapes=[buf] * 16 + [pltpu.VMEM((4, S, LANES), BF16), pltpu.SemaphoreType.DMA((4,))],
        input_output_aliases={9: 0},
        compiler_params=_cp(("arbitrary", "arbitrary")),
    )(qsb, ksb, vsb, z, cc, ss, ob, lse, do_bg, dz)


def _in_proj_bwd(dz, h, w_in):
    half = S // 2
    slab = (D, SHARD_COLS)

    def body(dz_hbm, h_hbm, w_hbm, dh_hbm, g_chip, r1_hbm, relay_hbm, r2_hbm,
             h_buf, dz_buf, stage_d, r1_buf, stage_i, acc,
             dz_sem, w_sem, h_sem, r1_sem, out_sem, send_d, recv_d, send_i, recv_i):
        x, y, c = _mesh_pos()
        sibling = (x, y, 1 - c)
        north = c == 1
        near = (jnp.where(north, 1 - x, x), jnp.where(north, y, 1 - y))
        far = (jnp.where(north, x, 1 - x), jnp.where(north, 1 - y, y))
        chips = [(1 - x, 1 - y), near, far, (x, y)]

        def cols(d):
            return pl.ds(pl.multiple_of(d * SHARD_COLS, LANES), SHARD_COLS)

        blocks = []
        for q_sib, q in zip([chips[0], far, near, chips[3]], chips):
            blocks += [4 * q_sib[0] + 2 * q_sib[1] + (1 - c), 4 * q[0] + 2 * q[1] + c]

        def dz_tile(t):
            return _SplitCopy(dz_hbm.at[pl.ds((t % 2) * half, half), cols(blocks[t // 2])],
                                         dz_buf.at[t % 2], dz_sem.at[t % 2])

        def to_sibling(i):
            return pltpu.make_async_remote_copy(
                src_ref=stage_d.at[i % 2], dst_ref=r1_hbm.at[i], send_sem=send_d.at[i], recv_sem=recv_d.at[i],
                device_id=sibling, device_id_type=MESH)

        def to_owner(i):
            dst = relay_hbm if i == 0 else r2_hbm.at[i - 1]
            return pltpu.make_async_remote_copy(
                src_ref=stage_i.at[i], dst_ref=dst, send_sem=send_i.at[i], recv_sem=recv_i.at[i],
                device_id=(*(far if i == 2 else near), c), device_id_type=MESH)

        h_copy = _SplitCopy(h_hbm, h_buf, h_sem)
        h_copy.start()
        dz_tile(0).start()
        h_copy.wait()
        for b in range(8):
            i = b // 2
            g = None
            for r in range(2):
                t = 2 * b + r
                if t + 1 < 16:
                    dz_tile(t + 1).start()
                dz_tile(t).wait()
                part = _dot_tn(h_buf[r * half:(r + 1) * half, :], dz_buf[t % 2])
                g = part if g is None else g + part
                if b % 2 == 1 and r == 0:
                    to_sibling(i).wait_recv()
                    r1_copy = _SplitCopy(r1_hbm.at[i], r1_buf, r1_sem)
                    r1_copy.start()
            if b % 2 == 0:
                if i >= 2:
                    to_sibling(i - 2).wait_send()
                stage_d[i % 2] = g.astype(BF16)
                to_sibling(i).start()
            else:
                r1_copy.wait()
                g = g + r1_buf[...].astype(F32)
                if i == 2:
                    to_owner(0).wait_recv()
                    relay_copy = _SplitCopy(relay_hbm, r1_buf, r1_sem)
                    relay_copy.start()
                    relay_copy.wait()
                    g = g + r1_buf[...].astype(F32)
                if i < 3:
                    stage_i[i] = g.astype(BF16)
                    to_owner(i).start()
                else:
                    g_chip[...] = g
        to_sibling(2).wait_send()
        to_sibling(3).wait_send()

        def dz2(t):
            return _SplitCopy(
                dz_hbm.at[pl.ds((t % 2) * half, half), pl.ds((t // 2) * SHARD_COLS, SHARD_COLS)],
                dz_buf.at[t % 2], dz_sem.at[t % 2])

        def w2(b):
            return _SplitCopy(w_hbm.at[:, pl.ds(b * SHARD_COLS, SHARD_COLS)],
                                         stage_d.at[b % 2], w_sem.at[b % 2])

        dz2(0).start()
        w2(0).start()
        for t in range(16):
            b, r = t // 2, t % 2
            if t + 1 < 16:
                dz2(t + 1).start()
            if r == 0:
                if b + 1 < 8:
                    w2(b + 1).start()
                w2(b).wait()
            dz2(t).wait()
            part = _dot_nt(dz_buf[t % 2], stage_d[b % 2])
            if b == 0:
                acc[r] = part
            else:
                acc[r] += part
        dh_out = [_SplitCopy(acc.at[r], dh_hbm.at[pl.ds(r * half, half), :], out_sem.at[r])
                  for r in range(2)]
        for cp in dh_out:
            cp.start()
        for cp in dh_out:
            cp.wait()
        for i in range(3):
            to_owner(i).wait_send()
        for i in (1, 2):
            to_owner(i).wait_recv()

    any_spec = pl.BlockSpec(memory_space=pl.ANY)
    return pl.pallas_call(
        body, name="in_proj_bwd",
        out_shape=[jax.ShapeDtypeStruct((S, D), F32), jax.ShapeDtypeStruct(slab, F32),
                   jax.ShapeDtypeStruct((4,) + slab, BF16), jax.ShapeDtypeStruct(slab, BF16),
                   jax.ShapeDtypeStruct((2,) + slab, BF16)],
        in_specs=[any_spec] * 3,
        out_specs=[any_spec, pl.BlockSpec(memory_space=pltpu.VMEM), any_spec, any_spec, any_spec],
        scratch_shapes=[pltpu.VMEM((S, D), BF16), pltpu.VMEM((2, half, SHARD_COLS), BF16),
                        pltpu.VMEM((2,) + slab, BF16), pltpu.VMEM(slab, BF16), pltpu.VMEM((3,) + slab, BF16),
                        pltpu.VMEM((2, half, D), F32),
                        pltpu.SemaphoreType.DMA((2,)), pltpu.SemaphoreType.DMA((2,)), pltpu.SemaphoreType.DMA,
                        pltpu.SemaphoreType.DMA, pltpu.SemaphoreType.DMA((2,)),
                        pltpu.SemaphoreType.DMA((4,)), pltpu.SemaphoreType.DMA((4,)),
                        pltpu.SemaphoreType.DMA((3,)), pltpu.SemaphoreType.DMA((3,))],
        compiler_params=_cp(),
    )(dz, h, w_in)


def _grad_x(x, norm_w, dh, dx2):
    tr = 256

    def body(x_ref, w_ref, dh_ref, dx2_ref, gx_ref, gnw_ref):
        @pl.when(pl.program_id(0) == 0)
        def _():
            gnw_ref[...] = jnp.zeros_like(gnw_ref)

        xv, dhv = x_ref[...], dh_ref[...]
        r = lax.rsqrt(jnp.mean(xv * xv, axis=-1, keepdims=True) + EPS)
        n = xv * r
        gnw_ref[...] += jnp.sum(dhv * n, axis=0, keepdims=True)
        dn = dhv * w_ref[...]
        gx_ref[...] = dx2_ref[...] + r * (dn - n * jnp.mean(dn * n, axis=-1, keepdims=True))

    row = pl.BlockSpec((tr, D), lambda i: (i, 0))
    vec = pl.BlockSpec((1, D), lambda i: (0, 0))
    return pl.pallas_call(
        body, name="grad_x", grid=(S // tr,),
        out_shape=[jax.ShapeDtypeStruct((S, D), F32), jax.ShapeDtypeStruct((1, D), F32)],
        in_specs=[row, vec, row, row], out_specs=[row, vec],
        compiler_params=_cp(("arbitrary",)),
    )(x, norm_w, dh, dx2)


def _rope_tables(positions):
    inv_freq = 10000.0 ** (-jnp.arange(0, 64, 2, dtype=F32) / 64)
    ang = positions.astype(F32)[:, None] * inv_freq[None, :]
    cos, sin = jnp.cos(ang), jnp.sin(ang)
    return jnp.tile(cos, (1, 4)), jnp.tile(jnp.concatenate([-sin, sin], axis=1), (1, 2))


def _local_step(x, positions, norm_w, lb_logits, hnw, fnw, target, w_in_shard, small_shards, core):
    cc, ss = _rope_tables(positions)
    lbv = jax.nn.sigmoid(lb_logits[0:1] - lb_logits[1:2])
    z, w_in, h = _in_proj_gather(x, norm_w, w_in_shard)
    o, o_a, states = _hgrn_fwd(z, lbv, hnw)
    ob, lse, o_bg, qsb, ksb, vsb, w_a, w_b, w_out = _attn_fwd(z, cc, ss, small_shards)
    dx2, dx2b, dz, do_a, do_bg, merged, dy_a, dy_b, tail_small = _tail(x, o_a, o_bg, z, target, w_a, w_b, w_out, fnw)
    g_out, gb_out = _tn_matmul(merged, dx2b, "grad_w_out")
    g_a, gb_a = _tn_matmul(o_a, dy_a, "grad_w_a")
    g_b, gb_b = _tn_matmul(o_bg, dy_b, "grad_w_b")
    grads, gb = (g_a, g_b, g_out), (gb_a, gb_b, gb_out)
    r1 = _exchange_sibling(GATHER_IDS, gb)
    pb = [_chip_partials(a, grads[i], r1[i], core) for i, a in enumerate(GATHER_IDS)]
    dz, glb, ghn, *r2 = _hgrn_bwd(z, o, do_a, states, lbv, hnw, pb, dz)
    dz = _attn_bwd(z, qsb, ksb, vsb, cc, ss, ob, lse, do_bg, dz)
    dh, g_chip_in, _, _, r2_in = _in_proj_bwd(dz, h, w_in)
    grad_x, gnw = _grad_x(x, norm_w, dh, dx2)
    ghn_row = jnp.pad(jnp.sum(ghn, axis=0), ((0, 0), (0, D - 128)))
    small = jnp.concatenate([gnw, glb, ghn_row, tail_small[0:2], jnp.zeros((3, D), F32)], axis=0)
    return grad_x, (g_chip_in, r2_in), grads, r1, r2, small


def kernel(x, positions, norm_w, w_in, lb_logits, hgrn_norm_w, w_branch_a, w_branch_b, w_out, final_norm_w, loss_target, m_norm_w, m_w_in, m_lb_logits, m_hgrn_norm_w, m_w_branch_a, m_w_branch_b, m_w_out, m_final_norm_w, v_norm_w, v_w_in, v_lb_logits, v_hgrn_norm_w, v_w_branch_a, v_w_branch_b, v_w_out, v_final_norm_w):
    ix, iy, ic = _mesh_pos()
    core = jnp.reshape(ic, (1,)).astype(jnp.int32)
    pos = jnp.stack([4 * ix + 2 * iy + ic, 2 * ix + iy]).astype(jnp.int32)

    shards = [w_in[0], w_branch_a[0], w_branch_b[0], w_out[0]]
    moments_m = [m_w_in[0], m_w_branch_a[0], m_w_branch_b[0], m_w_out[0]]
    moments_v = [v_w_in[0], v_w_branch_a[0], v_w_branch_b[0], v_w_out[0]]
    names = ("w_in", "w_a", "w_b", "w_out")
    ids = GATHER_IDS
    shards_b = [_cast_bf16(w, f"cast_{nm}") for w, nm in zip(shards, names)]

    fnw2 = final_norm_w.reshape(1, D)
    grad_x, (g_chip_in, r2_in), grads, r1, r2, small = _local_step(
        x[0], positions[0], norm_w, lb_logits, hgrn_norm_w, fnw2, loss_target[0], shards_b[0], shards_b[1:], core)

    gathered = _gather_small(small)
    big =[_reduce_own_and_update(shards[0], moments_m[0], moments_v[0], g_chip_in, r2_in)]
    big += [_reduce_and_update(a, shards[a], moments_m[a], moments_v[a], grads[i], r1[i], r2[i], pos)
            for i, a in enumerate(ids)]
    sm = _small_update(gathered, norm_w, lb_logits, hgrn_norm_w, fnw2,
                       (m_norm_w, m_lb_logits, m_hgrn_norm_w, m_final_norm_w.reshape(1, D),
                        v_norm_w, v_lb_logits, v_hgrn_norm_w, v_final_norm_w.reshape(1, D)))
    loss = sm[0][0, 0]
    outs = [loss, grad_x[None]]
    for kind in range(4):
        s_nw, s_lb, s_hn, s_fn = sm[1 + 4 * kind:5 + 4 * kind]
        outs += [s_nw, big[0][kind][None], s_lb, s_hn, big[1][kind][None], big[2][kind][None],
                 big[3][kind][None], s_fn.reshape(D)]
    return tuple(outs)
```

```python
import functools

import jax
import jax.numpy as jnp
from jax import lax
from jax.experimental import pallas as pl
from jax.experimental.pallas import tpu as pltpu

F32 = jnp.float32
BF16 = jnp.bfloat16
MESH = pl.DeviceIdType.MESH

S = 2048
D = 1024
NDEV = 8
HEADS = 8
CHUNK = 64
SUB = 16
HBLK = 256
ATT_PAD = 128
ATT_UNROLL = 16
COPY_PARTS = 4
EXP_CLAMP = 80.0
EPS = 1e-6
IN_COLS = 11264
SHARD_COLS = IN_COLS // NDEV
DZ_ATT = 4096
DZ_GATES = 9216
ATT_DILS = (1, 4, 16)
ATT_SCALE = 64 ** -0.5
LANES = 128

ADAM_LR, ADAM_B1, ADAM_B2, ADAM_EPS, ADAM_WD, ADAM_STEP = 0.001, 0.9, 0.999, 1e-08, 0.01, 10

VMEM_LIMIT = 56 * 1024 * 1024


def _cp(sem=None, **kw):
    return pltpu.CompilerParams(dimension_semantics=sem, vmem_limit_bytes=VMEM_LIMIT, **kw)


def _dot(a, b):
    return jnp.dot(a, b, preferred_element_type=F32)


def _dot_nt(a, b):
    return lax.dot_general(a, b, (((1,), (1,)), ((), ())), preferred_element_type=F32)


def _dot_tn(a, b):
    return lax.dot_general(a, b, (((0,), (0,)), ((), ())), preferred_element_type=F32)


def _split2(x):
    hi = x.astype(BF16)
    lo = (x - hi.astype(F32)).astype(BF16)
    return hi, lo


def _split3(x):
    hi = x.astype(BF16)
    r = x - hi.astype(F32)
    mid = r.astype(BF16)
    lo = (r - mid.astype(F32)).astype(BF16)
    return hi, mid, lo


def _dot_ones(ones_bf16, x):
    hi, mid, lo = _split3(x)
    return _dot(ones_bf16, hi) + _dot(ones_bf16, mid) + _dot(ones_bf16, lo)


def _silu(x):
    return x * jax.nn.sigmoid(x)


def _dsilu(x):
    s = jax.nn.sigmoid(x)
    return s * (1.0 + x * (1.0 - s))


def _mesh_pos():
    return lax.axis_index("x"), lax.axis_index("y"), lax.axis_index("c")


class _SplitCopy:
    def __init__(self, src, dst, sem):
        self.src, self.dst, self.sem = src, dst, sem

    def start(self):
        rows = self.src.shape[0] // COPY_PARTS
        for p in range(COPY_PARTS):
            chunk = pl.ds(p * rows, rows)
            pltpu.make_async_copy(self.src.at[chunk], self.dst.at[chunk], self.sem).start()

    def wait(self):
        pltpu.make_async_copy(self.src, self.dst, self.sem).wait()


def _shard_of(ref, a, d):
    if a == 0:
        return ref.at[:, pl.ds(pl.multiple_of(d * SHARD_COLS, LANES), SHARD_COLS)]
    if a == 2:
        return ref.at[:, pl.ds(pl.multiple_of(d * LANES, LANES), LANES)]
    return ref.at[pl.ds(pl.multiple_of(d * 128, 128), 128), :]


FULL_SHAPES = ((D, IN_COLS), (D, D), (512, D), (D, D))
SHARD_SHAPES = ((D, SHARD_COLS), (128, D), (512, 128), (128, D))


def _allgather_steps(ids, ins, outs, send_sems, recv_sems, local_sems):
    n = len(ids)
    x, y, c = _mesh_pos()
    me, sibling = (x, y, c), (x, y, 1 - c)
    chips = [(1 - x, y), (x, 1 - y), (1 - x, 1 - y)]

    def blk(a, p):
        return _shard_of(outs[a], ids[a], 4 * p[0] + 2 * p[1] + p[2])

    def copy(a, k, block, to, src=None):
        return pltpu.make_async_remote_copy(
            src_ref=blk(a, block) if src is None else src, dst_ref=blk(a, block),
            send_sem=send_sems.at[a * 7 + k], recv_sem=recv_sems.at[a * 7 + k],
            device_id=to, device_id_type=MESH)

    mine = [pltpu.make_async_copy(ins[a], blk(a, me), local_sems.at[a]) for a in range(n)]
    first = []
    for a in range(n):
        first += [copy(a, 1 + j, me, (*chip, c), src=ins[a]) for j, chip in enumerate(chips)]
    for a in range(n):
        first.append(copy(a, 0, me, sibling, src=ins[a]))
    passed = [copy(a, 4 + j, (*chip, c), sibling) for j, chip in enumerate(chips) for a in range(n)]

    def start():
        for cp in mine + first:
            cp.start()

    def middle():
        for j, chip in enumerate(chips):
            for a in range(n):
                copy(a, 1 + j, (*chip, c), me).wait_recv()
                passed[j * n + a].start()

    def end():
        for a in range(n):
            copy(a, 0, sibling, me).wait_recv()
        for j, chip in enumerate(chips):
            for a in range(n):
                copy(a, 4 + j, (*chip, 1 - c), me).wait_recv()
        for cp in first + passed:
            cp.wait_send()
        for cp in mine:
            cp.wait()

    return start, middle, end


def _in_proj_gather(x, norm_w, w_shard):
    half = S // 2
    slab = (D, SHARD_COLS)
    xt = 512

    def body(x_hbm, nw_ref, w_hbm, z_hbm, wfull_hbm, h_hbm, h_buf, land, zstage, xbuf,
             h_sem, own_sem, z_sem, wout_sem, x_sem, send_sems, recv_sems):
        x, y, c = _mesh_pos()
        sibling = (x, y, 1 - c)
        north = c == 1

        def chips_of(first_x):
            near = (jnp.where(first_x, 1 - x, x), jnp.where(first_x, y, 1 - y))
            far = (jnp.where(first_x, x, 1 - x), jnp.where(first_x, 1 - y, y))
            return [near, far, (1 - x, 1 - y)]

        mine, theirs = chips_of(north), chips_of(jnp.logical_not(north))

        def dev(chip, core):
            return 4 * chip[0] + 2 * chip[1] + core

        block_of = ([dev((x, y), c), dev((x, y), 1 - c)] + [dev(q, c) for q in mine]
                    + [dev(q, 1 - c) for q in theirs])

        def cols(d):
            if isinstance(d, int):
                return pl.ds(d * SHARD_COLS, SHARD_COLS)
            return pl.ds(pl.multiple_of(d * SHARD_COLS, LANES), SHARD_COLS)

        def send(k, src, dst_slot, to):
            return pltpu.make_async_remote_copy(
                src_ref=src, dst_ref=land.at[dst_slot], send_sem=send_sems.at[k], recv_sem=recv_sems.at[k],
                device_id=to, device_id_type=MESH)

        def to_sibling():
            return send(0, w_hbm, 1, sibling)

        def to_chip(j):
            if j == 2:
                return send(3, land.at[2], 4, (*mine[1], c))
            return send(1 + j, w_hbm, 2 + j, (*mine[j], c))

        def pass_on(j):
            return send(4 + j, land.at[2 + j], 5 + j, sibling)

        def x_tile(i):
            return _SplitCopy(x_hbm.at[pl.ds(i * xt, xt), :], xbuf.at[i % 2], x_sem.at[i % 2])

        own = _SplitCopy(w_hbm, land.at[0], own_sem)
        own.start()
        x_tile(0).start()
        to_sibling().start()
        to_chip(0).start()
        for i in range(S // xt):
            if i + 1 < S // xt:
                x_tile(i + 1).start()
            x_tile(i).wait()
            xv = xbuf[i % 2]
            r = lax.rsqrt(jnp.mean(xv * xv, axis=-1, keepdims=True) + EPS)
            h_buf[i * xt:(i + 1) * xt, :] = (xv * r * nw_ref[...]).astype(BF16)
        h_out = _SplitCopy(h_buf, h_hbm, h_sem)
        h_out.start()
        own.wait()

        def multiply(slot, n_done):
            d = block_of[slot]
            out = _SplitCopy(land.at[slot], wfull_hbm.at[:, cols(d)], wout_sem.at[slot])
            out.start()
            for r in range(2):
                rows = pl.ds(r * half, half)
                zc = _SplitCopy(zstage.at[r], z_hbm.at[rows, cols(d)], z_sem.at[r])
                if n_done > 0:
                    zc.wait()
                zstage[r] = _dot(h_buf[r * half:(r + 1) * half, :], land[slot])
                zc.start()
            return out

        outs = [multiply(0, 0)]
        to_sibling().wait_recv()
        outs.append(multiply(1, 1))
        done = 2
        for j in range(3):
            to_chip(j).wait_recv()
            pass_on(j).start()
            to_chip(j).wait_send()
            if j < 2:
                to_chip(j + 1).start()
            outs.append(multiply(2 + j, done))
            pass_on(j).wait_recv()
            outs.append(multiply(5 + j, done + 1))
            done += 2
        for r in range(2):
            _SplitCopy(zstage.at[r], z_hbm.at[pl.ds(r * half, half), cols(0)], z_sem.at[r]).wait()
        for out in outs:
            out.wait()
        h_out.wait()
        to_sibling().wait_send()
        for j in range(3):
            pass_on(j).wait_send()

    any_spec = pl.BlockSpec(memory_space=pl.ANY)
    return pl.pallas_call(
        body, name="in_proj_gather",
        out_shape=[jax.ShapeDtypeStruct((S, IN_COLS), F32), jax.ShapeDtypeStruct((D, IN_COLS), BF16),
                   jax.ShapeDtypeStruct((S, D), BF16)],
        in_specs=[any_spec, pl.BlockSpec(memory_space=pltpu.VMEM), any_spec], out_specs=[any_spec] * 3,
        scratch_shapes=[pltpu.VMEM((S, D), BF16), pltpu.VMEM((8,) + slab, BF16), pltpu.VMEM((2, half, SHARD_COLS), F32),
                        pltpu.VMEM((2, xt, D), F32),
                        pltpu.SemaphoreType.DMA, pltpu.SemaphoreType.DMA, pltpu.SemaphoreType.DMA((2,)),
                        pltpu.SemaphoreType.DMA((8,)), pltpu.SemaphoreType.DMA((2,)),
                        pltpu.SemaphoreType.DMA((7,)), pltpu.SemaphoreType.DMA((7,))],
        compiler_params=_cp(),
    )(x, norm_w, w_shard)


def _exchange_sibling(ids, gb):
    n = len(gb)

    def body(*refs):
        ins, outs = refs[:n], refs[n:2 * n]
        send_sems, recv_sems = refs[2 * n:]
        x, y, c = _mesh_pos()
        sibling = (x, y, 1 - c)
        copies = []
        for i, a in enumerate(ids):
            for q in range(4):
                copies.append(pltpu.make_async_remote_copy(
                    src_ref=_shard_of(ins[i], a, 2 * q + (1 - c)), dst_ref=outs[i].at[q],
                    send_sem=send_sems.at[i * 4 + q], recv_sem=recv_sems.at[i * 4 + q],
                    device_id=sibling, device_id_type=MESH))
        for cp in copies:
            cp.start()
        for cp in copies:
            cp.wait()

    any_spec = pl.BlockSpec(memory_space=pl.ANY)
    return pl.pallas_call(
        body, name="grads_to_sibling",
        out_shape=[jax.ShapeDtypeStruct((4,) + SHARD_SHAPES[a], BF16) for a in ids],
        in_specs=[any_spec] * n, out_specs=[any_spec] * n,
        scratch_shapes=[pltpu.SemaphoreType.DMA((4 * n,)), pltpu.SemaphoreType.DMA((4 * n,))],
    )(*gb)


def _exchange_chips_steps(ins, outs, send_sems, recv_sems):
    x, y, c = _mesh_pos()
    chips = [(1 - x, y), (x, 1 - y), (1 - x, 1 - y)]
    copies = []
    for a in range(len(ins)):
        for k, chip in enumerate(chips):
            copies.append(pltpu.make_async_remote_copy(
                src_ref=ins[a].at[2 * chip[0] + chip[1]], dst_ref=outs[a].at[k],
                send_sem=send_sems.at[a * 3 + k], recv_sem=recv_sems.at[a * 3 + k],
                device_id=(*chip, c), device_id_type=MESH))

    def start():
        for cp in copies:
            cp.start()

    def end():
        for cp in copies:
            cp.wait()

    return start, end


def _gather_small_steps(small_ref, small_out, ssend, srecv, local_sem):
    x, y, c = _mesh_pos()
    me = 4 * x + 2 * y + c
    copies = []
    for r in range(1, NDEV):
        peer = (1 - x if r & 4 else x, 1 - y if r & 2 else y, 1 - c if r & 1 else c)
        copies.append(pltpu.make_async_remote_copy(
            src_ref=small_ref, dst_ref=small_out.at[me],
            send_sem=ssend.at[r - 1], recv_sem=srecv.at[r - 1],
            device_id=peer, device_id_type=MESH))
    own = pltpu.make_async_copy(small_ref, small_out.at[me], local_sem)

    def start():
        own.start()
        for cp in copies:
            cp.start()

    def end():
        for cp in copies:
            cp.wait()
        own.wait()

    return start, end


def _gather_small(small):
    def body(small_ref, small_out, ssend, srecv, local_sem):
        start, end = _gather_small_steps(small_ref, small_out, ssend, srecv, local_sem)
        start()
        end()

    any_spec = pl.BlockSpec(memory_space=pl.ANY)
    return pl.pallas_call(
        body, name="gather_small",
        out_shape=jax.ShapeDtypeStruct((NDEV,) + small.shape, F32),
        in_specs=[any_spec], out_specs=any_spec,
        scratch_shapes=[pltpu.SemaphoreType.DMA((NDEV - 1,)), pltpu.SemaphoreType.DMA((NDEV - 1,)),
                        pltpu.SemaphoreType.DMA],
    )(small)


def _shard_tiles(a):
    rows, cols = SHARD_SHAPES[a]
    tr = min(rows, 256)
    return (tr, cols), rows // tr


def _full_index(a, d, i):
    (tr, _), nt = _shard_tiles(a)
    if a in (0, 2):
        return (i, d)
    return (d * nt + i, 0)


def _cast_bf16(x, name):
    rows, cols = x.shape
    tr = min(rows, 256)

    def body(x_ref, o_ref):
        o_ref[...] = x_ref[...].astype(BF16)

    return pl.pallas_call(
        body, name=name, out_shape=jax.ShapeDtypeStruct(x.shape, BF16), grid=(rows // tr,),
        in_specs=[pl.BlockSpec((tr, cols), lambda i: (i, 0))],
        out_specs=pl.BlockSpec((tr, cols), lambda i: (i, 0)),
        compiler_params=_cp(("parallel",)),
    )(x)


def _chip_partials(a, g_full, r1, core):
    tile, nt = _shard_tiles(a)

    def body(c_ref, g_ref, r_ref, o_ref):
        o_ref[0] = (g_ref[...] + r_ref[0].astype(F32)).astype(BF16)

    grid_spec = pltpu.PrefetchScalarGridSpec(
        num_scalar_prefetch=1, grid=(4, nt),
        in_specs=[pl.BlockSpec(tile, lambda q, i, c: _full_index(a, 2 * q + c[0], i)),
                  pl.BlockSpec((1,) + tile, lambda q, i, c: (q, i, 0))],
        out_specs=pl.BlockSpec((1,) + tile, lambda q, i, c: (q, i, 0)))
    return pl.pallas_call(
        body, name=f"chip_partials_{a}", grid_spec=grid_spec,
        out_shape=jax.ShapeDtypeStruct((4,) + SHARD_SHAPES[a], BF16),
        compiler_params=_cp(("parallel", "parallel")),
    )(core, g_full, r1)


def _adam(w, g, m, v):
    m = ADAM_B1 * m + (1.0 - ADAM_B1) * g
    v = ADAM_B2 * v + (1.0 - ADAM_B2) * (g * g)
    m_hat = m / (1.0 - ADAM_B1 ** ADAM_STEP)
    v_hat = v / (1.0 - ADAM_B2 ** ADAM_STEP)
    delta = -ADAM_LR * (m_hat / (jnp.sqrt(v_hat) + ADAM_EPS) + ADAM_WD * w)
    return delta, m, v


def _reduce_and_update(a, w, m, v, g_full, r1, r2, pos):
    tile, nt = _shard_tiles(a)

    def body(p_ref, w_ref, m_ref, v_ref, g_ref, r1_ref, r2_ref, go_ref, do_ref, mo_ref, vo_ref):
        g = g_ref[...] + r1_ref[0].astype(F32)
        g = g + r2_ref[0].astype(F32)
        g = g + r2_ref[1].astype(F32)
        g = g + r2_ref[2].astype(F32)
        delta, m_new, v_new = _adam(w_ref[...], g, m_ref[...], v_ref[...])
        go_ref[...] = g
        do_ref[...] = delta
        mo_ref[...] = m_new
        vo_ref[...] = v_new

    own = pl.BlockSpec(tile, lambda i, p: (i, 0))
    grid_spec = pltpu.PrefetchScalarGridSpec(
        num_scalar_prefetch=1, grid=(nt,),
        in_specs=[own, own, own,
                  pl.BlockSpec(tile, lambda i, p: _full_index(a, p[0], i)),
                  pl.BlockSpec((1,) + tile, lambda i, p: (p[1], i, 0)),
                  pl.BlockSpec((3,) + tile, lambda i, p: (0, i, 0))],
        out_specs=[own] * 4)
    shp = jax.ShapeDtypeStruct(w.shape, F32)
    return pl.pallas_call(
        body, name=f"reduce_update_{a}", grid_spec=grid_spec, out_shape=[shp] * 4,
        compiler_params=_cp(("parallel",)),
    )(pos, w, m, v, g_full, r1, r2)


def _reduce_own_and_update(w, m, v, g_chip, r2):
    tile, nt = _shard_tiles(0)

    def body(w_ref, m_ref, v_ref, g_ref, r2_ref, go_ref, do_ref, mo_ref, vo_ref):
        g = g_ref[...] + r2_ref[0].astype(F32)
        g = g + r2_ref[1].astype(F32)
        delta, m_new, v_new = _adam(w_ref[...], g, m_ref[...], v_ref[...])
        go_ref[...] = g
        do_ref[...] = delta
        mo_ref[...] = m_new
        vo_ref[...] = v_new

    own = pl.BlockSpec(tile, lambda i: (i, 0))
    shp = jax.ShapeDtypeStruct(w.shape, F32)
    return pl.pallas_call(
        body, name="reduce_update_0", grid=(nt,), out_shape=[shp] * 4,
        in_specs=[own, own, own, own, pl.BlockSpec((2,) + tile, lambda i: (0, i, 0))], out_specs=[own] * 4,
        compiler_params=_cp(("parallel",)),
    )(w, m, v, g_chip, r2)


def _small_update(gathered, norm_w, lb_logits, hnw, fnw, moments):
    m_nw, m_lb, m_hn, m_fn, v_nw, v_lb, v_hn, v_fn = moments

    def body(g_ref, nw, lb, hn, fn, mnw, mlb, mhn, mfn, vnw, vlb, vhn, vfn,
             loss_o, g_nw, g_lb, g_hn, g_fn, d_nw, d_lb, d_hn, d_fn,
             mo_nw, mo_lb, mo_hn, mo_fn, vo_nw, vo_lb, vo_hn, vo_fn):
        tot = g_ref[0]
        for d in range(1, NDEV):
            tot = tot + g_ref[d]
        loss_o[...] = tot[4:5, 0:LANES]
        logits = lb[...]
        lbv = jax.nn.sigmoid(logits[0:1] - logits[1:2])
        chain = tot[1:2] * lbv * (1.0 - lbv)
        grads = (tot[0:1], jnp.concatenate([chain, -chain], axis=0), tot[2:3, 0:LANES], tot[3:4])
        outs = ((nw, mnw, vnw, g_nw, d_nw, mo_nw, vo_nw), (lb, mlb, vlb, g_lb, d_lb, mo_lb, vo_lb),
                (hn, mhn, vhn, g_hn, d_hn, mo_hn, vo_hn), (fn, mfn, vfn, g_fn, d_fn, mo_fn, vo_fn))
        for g, (w_r, m_r, v_r, g_o, d_o, m_o, v_o) in zip(grads, outs):
            delta, m_new, v_new = _adam(w_r[...], g, m_r[...], v_r[...])
            g_o[...] = g
            d_o[...] = delta
            m_o[...] = m_new
            v_o[...] = v_new

    shapes = [norm_w.shape, lb_logits.shape, hnw.shape, fnw.shape]
    out_shape = [jax.ShapeDtypeStruct((1, LANES), F32)] + [jax.ShapeDtypeStruct(s, F32) for s in shapes] * 4
    return pl.pallas_call(body, name="small_update", out_shape=out_shape, compiler_params=_cp())(
        gathered, norm_w, lb_logits, hnw, fnw, m_nw, m_lb, m_hn, m_fn, v_nw, v_lb, v_hn, v_fn)


def _block_tri(n, block, upper=False):
    r = lax.broadcasted_iota(jnp.int32, (n, n), 0)
    c = lax.broadcasted_iota(jnp.int32, (n, n), 1)
    keep = (c >= r) if upper else (c <= r)
    return jnp.where(keep & ((r // block) == (c // block)), 1.0, 0.0).astype(BF16)


def _tril_mask(n):
    r = lax.broadcasted_iota(jnp.int32, (n, n), 0)
    c = lax.broadcasted_iota(jnp.int32, (n, n), 1)
    return c <= r


def _chunk_scores(q, k, b, bex, r0, mask):
    parts, qs_l, ks_l, ek_l, eq_l = [], [], [], [], []
    for i in range(CHUNK // SUB):
        ri = slice(r0 + SUB * i, r0 + SUB * (i + 1))
        seen = slice(r0, r0 + SUB * (i + 1))
        base = bex[r0 + SUB * i:r0 + SUB * i + 1]
        eq = jnp.exp(b[ri] - base)
        ek = jnp.exp(jnp.minimum(base - b[seen], EXP_CLAMP))
        ks = k[seen] * ek
        if i + 1 < CHUNK // SUB:
            rest = jnp.zeros((CHUNK - SUB * (i + 1), 128), F32)
            ek, ks = jnp.concatenate([ek, rest], axis=0), jnp.concatenate([ks, rest], axis=0)
        qs = q[ri] * eq
        parts.append(_dot_nt(qs.astype(BF16), ks.astype(BF16)))
        qs_l.append(qs)
        ks_l.append(ks)
        ek_l.append(ek)
        eq_l.append(eq)
    return jnp.where(mask, jnp.concatenate(parts, axis=0), 0.0), qs_l, ks_l, ek_l, eq_l


def _hgrn_cols(hq, hf, hi, lb):
    sg = jax.nn.sigmoid(hf)
    f = lb + (1.0 - lb) * sg
    g = jnp.log(f)
    b = _dot_ones(_block_tri(HBLK, CHUNK), g)
    return _silu(hq), 1.0 - f, g, hi, sg, f, b


GATHER_IDS = (1, 2, 3)


def _hgrn_fwd(z, lbv, hnw):
    ntb, nch = S // HBLK, HBLK // CHUNK

    def body(hq_ref, hf_ref, hi_ref, hg_ref, lb_ref, hnw_ref, o_ref, oa_ref, st_ref, state):
        @pl.when(pl.program_id(0) == 0)
        def _():
            state[...] = jnp.zeros_like(state)

        q_a, k_a, g_a, v_a, _, _, b_a = _hgrn_cols(hq_ref[...], hf_ref[...], hi_ref[...], lb_ref[...])
        bex_a = b_a - g_a
        eb_a = jnp.exp(b_a)
        mask = _tril_mask(CHUNK)
        hg = hg_ref[...]
        w = hnw_ref[...]
        for h in range(HEADS):
            cols = slice(128 * h, 128 * h + 128)
            q, k, v, b, bex, eb = q_a[:, cols], k_a[:, cols], v_a[:, cols], b_a[:, cols], bex_a[:, cols], eb_a[:, cols]
            st = state[h]
            outs = []
            for c in range(nch):
                r0 = c * CHUNK
                rows = slice(r0, r0 + CHUNK)
                a = _chunk_scores(q, k, b, bex, r0, mask)[0]
                vb = v[rows].astype(BF16)
                b_last = b[r0 + CHUNK - 1:r0 + CHUNK]
                qe = (q[rows] * eb[rows]).astype(BF16)
                outs.append(_dot(a.astype(BF16), vb) + _dot_nt(qe, st.astype(BF16)))
                st_ref[h, c] = st
                ke = (k[rows] * jnp.exp(b_last - b[rows])).astype(BF16)
                st = st * jnp.exp(b_last) + _dot_tn(vb, ke)
            state[h] = st
            o = jnp.concatenate(outs, axis=0)
            o_ref[:, cols] = o
            r = lax.rsqrt(jnp.mean(o * o, axis=-1, keepdims=True) + EPS)
            oa_ref[:, cols] = (o * r * w * _silu(hg[:, cols])).astype(BF16)

    def zcol(j):
        return pl.BlockSpec((HBLK, D), lambda t: (t, j))

    out_blk = pl.BlockSpec((HBLK, D), lambda t: (t, 0))
    return pl.pallas_call(
        body, name="hgrn_fwd", grid=(ntb,),
        out_shape=[jax.ShapeDtypeStruct((S, D), F32), jax.ShapeDtypeStruct((S, D), BF16),
                   jax.ShapeDtypeStruct((HEADS, S // CHUNK, 128, 128), F32)],
        in_specs=[zcol(0), zcol(1), zcol(2), zcol(3),
                  pl.BlockSpec((1, D), lambda t: (0, 0)), pl.BlockSpec((1, 128), lambda t: (0, 0))],
        out_specs=[out_blk, out_blk, pl.BlockSpec((HEADS, nch, 128, 128), lambda t: (0, t, 0, 0))],
        scratch_shapes=[pltpu.VMEM((HEADS, 128, 128), F32)],
        compiler_params=_cp(("arbitrary",)),
    )(z, z, z, z, lbv, hnw)


def _half_mask():
    lane = lax.broadcasted_iota(jnp.int32, (1, LANES), 1)
    return (lane % 64) < 32


def _rope(t, cc, ss, first_half):
    partner = jnp.where(first_half, pltpu.roll(t, 96, 1), pltpu.roll(t, 32, 1))
    return t * cc + partner * ss


def _attn_masks():
    i = lax.broadcasted_iota(jnp.int32, (128, 128), 0)
    j = lax.broadcasted_iota(jnp.int32, (128, 128), 1)
    return j >= i, j <= i


def _to_residues_dyn(g, dst, src, row0=0, dtype=None):
    for gi, dil in enumerate((1, 4, 16)):
        m = S // dil

        @pl.when(g == gi)
        def _(dil=dil, m=m):
            for r in range(dil):
                v = src[...] if dil == 1 else src[pl.ds(r, m, stride=dil), :]
                if dtype is not None:
                    v = v.astype(dtype)
                dst[row0 + r * m:row0 + (r + 1) * m, 0:LANES] = v


def _from_residues_dyn(g, dst, src, row0=0):
    for gi, dil in enumerate((1, 4, 16)):
        m = S // dil

        @pl.when(g == gi)
        def _(dil=dil, m=m):
            for r in range(dil):
                v = src[row0 + r * m:row0 + (r + 1) * m, :]
                if dil == 1:
                    dst[...] = v
                else:
                    dst[pl.ds(r, m, stride=dil), :] = v


def _group_blocks(g):
    return jnp.where(g == 0, 16, jnp.where(g == 1, 4, 1))


def _attn_in_specs(extra):
    def zcol(off):
        return pl.BlockSpec((S, LANES), lambda p, g: (0, off + 4 * g + p))

    per_pair = pl.BlockSpec((S, LANES), lambda p, g: (0, p))
    const = pl.BlockSpec((S, LANES), lambda p, g: (0, 0))
    return [zcol(32), zcol(44), zcol(56), pl.BlockSpec((S, LANES), lambda p, g: (0, 68 + p)), const, const] + [per_pair] * extra


def _attn_fwd(z, cc, ss, shards):
    def body(q_ref, k_ref, v_ref, ag_ref, cc_ref, ss_ref, s0, s1, s2,
             ob_ref, lse_ref, obg_ref, qsb_ref, ksb_ref, vsb_ref, f0, f1, f2,
             tmp, qs, ks, vx, og, mg, lg, o_t, m_t, l_t, o_acc, m_acc, l_acc, send_sems, recv_sems, local_sems):
        g = pl.program_id(1)
        step = pl.program_id(0) * 3 + g
        start, middle, end = _allgather_steps(GATHER_IDS, (s0, s1, s2), (f0, f1, f2), send_sems, recv_sems, local_sems)
        pl.when(step == 0)(start)
        pl.when(step == 9)(middle)
        first_half = _half_mask()
        prev_ok, cur_ok = _attn_masks()
        lane = lax.broadcasted_iota(jnp.int32, (1, LANES), 1)
        heads = (lane < 64, lane >= 64)
        nblk = _group_blocks(g)

        @pl.when(g == 0)
        def _():
            ks[0:ATT_PAD, :] = jnp.zeros((ATT_PAD, LANES), BF16)
            vx[0:ATT_PAD, 0:LANES] = jnp.zeros((ATT_PAD, LANES), BF16)
            vx[:, LANES:2 * LANES] = jnp.ones((ATT_PAD + S, LANES), BF16)

        tmp[...] = _rope(q_ref[...], cc_ref[...], ss_ref[...], first_half) * ATT_SCALE
        _to_residues_dyn(g, qs, tmp)
        tmp[...] = _rope(k_ref[...], cc_ref[...], ss_ref[...], first_half)
        _to_residues_dyn(g, ks, tmp, ATT_PAD, BF16)
        _to_residues_dyn(g, vx, v_ref, ATT_PAD, BF16)

        def unit(u, carry):
            start = pl.multiple_of(u * 128, 128)
            cur = pl.ds(start, 128)
            pm = prev_ok & ((u & (nblk - 1)) != 0)
            qu = qs[cur, :]
            kcat = ks[pl.ds(start, 256), :]
            vext = vx[pl.ds(start, 256), :]
            o_u = m_u = l_u = None
            for hh in range(2):
                s = _dot_nt(jnp.where(heads[hh], qu, 0.0).astype(BF16), kcat)
                sp = jnp.where(pm, s[:, 0:128], -jnp.inf)
                sc = jnp.where(cur_ok, s[:, 128:256], -jnp.inf)
                m = jnp.max(jnp.maximum(sp, sc), axis=-1, keepdims=True)
                p = jnp.concatenate([jnp.exp(sp - m), jnp.exp(sc - m)], axis=1).astype(BF16)
                ol = _dot(p, vext)
                mb = jnp.broadcast_to(m, (128, LANES))
                if hh == 0:
                    o_u, l_u, m_u = ol[:, 0:128], ol[:, 128:256], mb
                else:
                    o_u = jnp.where(heads[1], ol[:, 0:128], o_u)
                    l_u = jnp.where(heads[1], ol[:, 128:256], l_u)
                    m_u = jnp.where(heads[1], mb, m_u)
            og[cur, :] = o_u
            mg[cur, :] = m_u
            lg[cur, :] = l_u
            return carry

        lax.fori_loop(0, 16, unit, 0, unroll=16)
        qsb_ref[0] = qs[...].astype(BF16)
        ksb_ref[0] = ks[...]
        vsb_ref[0] = vx[:, 0:LANES]
        _from_residues_dyn(g, o_t, og)
        _from_residues_dyn(g, m_t, mg)
        _from_residues_dyn(g, l_t, lg)

        @pl.when(g == 0)
        def _():
            o_acc[...] = o_t[...]
            m_acc[...] = m_t[...]
            l_acc[...] = l_t[...]

        @pl.when(g > 0)
        def _():
            m_new = jnp.maximum(m_acc[...], m_t[...])
            wa, wb = jnp.exp(m_acc[...] - m_new), jnp.exp(m_t[...] - m_new)
            o_acc[...] = o_acc[...] * wa + o_t[...] * wb
            l_acc[...] = l_acc[...] * wa + l_t[...] * wb
            m_acc[...] = m_new

        @pl.when(g == 2)
        def _():
            ob = o_acc[...] / l_acc[...]
            ob_ref[...] = ob
            lse_ref[...] = m_acc[...] + jnp.log(l_acc[...])
            obg_ref[...] = (ob * _silu(ag_ref[...])).astype(BF16)

        pl.when(step == 11)(end)

    n = len(GATHER_IDS)
    any_spec = pl.BlockSpec(memory_space=pl.ANY)
    blk = pl.BlockSpec((S, LANES), lambda p, g: (0, p))
    buf = pltpu.VMEM((S, LANES), F32)
    return pl.pallas_call(
        body, name="attn_fwd", grid=(4, 3),
        out_shape=[jax.ShapeDtypeStruct((S, 512), F32), jax.ShapeDtypeStruct((S, 512), F32),
                   jax.ShapeDtypeStruct((S, 512), BF16), jax.ShapeDtypeStruct((3, S, 512), BF16),
                   jax.ShapeDtypeStruct((3, ATT_PAD + S, 512), BF16), jax.ShapeDtypeStruct((3, ATT_PAD + S, 512), BF16)]
        + [jax.ShapeDtypeStruct(FULL_SHAPES[a], BF16) for a in GATHER_IDS],
        in_specs=_attn_in_specs(0) + [any_spec] * n,
        out_specs=[blk, blk, blk, pl.BlockSpec((1, S, LANES), lambda p, g: (g, 0, p)),
                   pl.BlockSpec((1, ATT_PAD + S, LANES), lambda p, g: (g, 0, p)),
                   pl.BlockSpec((1, ATT_PAD + S, LANES), lambda p, g: (g, 0, p))] + [any_spec] * n,
        scratch_shapes=[buf, buf, pltpu.VMEM((ATT_PAD + S, LANES), BF16), pltpu.VMEM((ATT_PAD + S, 2 * LANES), BF16)]
        + [buf] * 9 + [pltpu.SemaphoreType.DMA((7 * n,)), pltpu.SemaphoreType.DMA((7 * n,)), pltpu.SemaphoreType.DMA((n,))],
        compiler_params=_cp(("arbitrary", "arbitrary")),
    )(z, z, z, z, cc, ss, *shards)


def _tail(x, o_a, o_bg, z, target, w_a, w_b, w_out, fnw):
    tm = 256

    def body(x_ref, oa_ref, ob_ref, gpa_ref, gpb_ref, t_ref, wa_ref, wb_ref, wo_ref, fnw_ref,
             dx2_ref, dx2b_ref, dz_hbm, doa_ref, dob_ref, mg_ref, dya_ref, dyb_ref, small_ref, dgp, dgp_sem):
        step = pl.program_id(0)
        slot = step % 2

        def dgp_copy(at_step, at_slot):
            return pltpu.make_async_copy(
                dgp.at[at_slot], dz_hbm.at[pl.ds(pl.multiple_of(at_step * tm, tm), tm), pl.ds(DZ_GATES, 2 * D)],
                dgp_sem.at[at_slot])

        @pl.when(step == 0)
        def _():
            small_ref[...] = jnp.zeros_like(small_ref)

        @pl.when(step >= 2)
        def _():
            dgp_copy(step - 2, slot).wait()

        wa, wb, wo = wa_ref[...], wb_ref[...], wo_ref[...]
        y_a = _dot(oa_ref[...], wa)
        y_b = _dot(ob_ref[...], wb)
        ga = jax.nn.sigmoid(gpa_ref[...])
        gb = jax.nn.sigmoid(gpb_ref[...])
        merged = (ga * y_a + gb * y_b).astype(BF16)
        x2 = x_ref[...] + _dot(merged, wo)
        r2 = lax.rsqrt(jnp.mean(x2 * x2, axis=-1, keepdims=True) + EPS)
        n2 = x2 * r2
        fw = fnw_ref[...]
        err = n2 * fw - t_ref[...]
        loss = 0.5 * jnp.sum(jnp.sum(err * err, axis=-1, keepdims=True), axis=0, keepdims=True) / D
        dy = err * (1.0 / D)
        g_fnw = jnp.sum(dy * n2, axis=0, keepdims=True)
        dn = dy * fw
        dx2 = r2 * (dn - n2 * jnp.mean(dn * n2, axis=-1, keepdims=True))
        dx2b = dx2.astype(BF16)
        dmerged = _dot_nt(dx2b, wo)
        dy_a = (dmerged * ga).astype(BF16)
        dy_b = (dmerged * gb).astype(BF16)
        dx2_ref[...] = dx2
        dx2b_ref[...] = dx2b
        dgp[slot, :, 0:D] = (dmerged * y_a * ga * (1.0 - ga)).astype(BF16)
        dgp[slot, :, D:2 * D] = (dmerged * y_b * gb * (1.0 - gb)).astype(BF16)
        dgp_copy(step, slot).start()
        doa_ref[...] = _dot_nt(dy_a, wa)
        dob_ref[...] = _dot_nt(dy_b, wb)
        mg_ref[...] = merged
        dya_ref[...] = dy_a
        dyb_ref[...] = dy_b
        small_ref[0:1, :] += g_fnw
        small_ref[1:2, :] += jnp.broadcast_to(loss, (1, D))

        @pl.when(step == S // tm - 1)
        def _():
            dgp_copy(step - 1, 1 - slot).wait()
            dgp_copy(step, slot).wait()

    def rows(cols, off=0):
        return pl.BlockSpec((tm, cols), lambda i: (i, off))

    def whole(shape):
        return pl.BlockSpec(shape, lambda i: (0, 0))

    return pl.pallas_call(
        body, name="tail", grid=(S // tm,),
        out_shape=[jax.ShapeDtypeStruct((S, D), F32), jax.ShapeDtypeStruct((S, D), BF16),
                   jax.ShapeDtypeStruct((S, IN_COLS), BF16), jax.ShapeDtypeStruct((S, D), F32),
                   jax.ShapeDtypeStruct((S, 512), F32), jax.ShapeDtypeStruct((S, D), BF16),
                   jax.ShapeDtypeStruct((S, D), BF16), jax.ShapeDtypeStruct((S, D), BF16),
                   jax.ShapeDtypeStruct((8, D), F32)],
        in_specs=[rows(D), rows(D), rows(512), rows(D, 9), rows(D, 10), rows(D),
                  whole((D, D)), whole((512, D)), whole((D, D)), whole((1, D))],
        out_specs=[rows(D), rows(D), pl.BlockSpec(memory_space=pl.ANY), rows(D), rows(512), rows(D), rows(D),
                   rows(D), whole((8, D))],
        scratch_shapes=[pltpu.VMEM((2, tm, 2 * D), BF16), pltpu.SemaphoreType.DMA((2,))],
        compiler_params=_cp(("arbitrary",)),
    )(x, o_a, o_bg, z, z, target, w_a, w_b, w_out, fnw)


def _tn_matmul(a, b, name):
    m, n = a.shape[1], b.shape[1]
    tn = 512

    def body(a_ref, b_ref, o_ref, ob_ref):
        acc = _dot_tn(a_ref[...], b_ref[...])
        o_ref[...] = acc
        ob_ref[...] = acc.astype(BF16)

    out_blk = pl.BlockSpec((m, tn), lambda j: (0, j))
    return pl.pallas_call(
        body, name=name, grid=(n // tn,),
        out_shape=[jax.ShapeDtypeStruct((m, n), F32), jax.ShapeDtypeStruct((m, n), BF16)],
        in_specs=[pl.BlockSpec((S, m), lambda j: (0, 0)), pl.BlockSpec((S, tn), lambda j: (0, j))],
        out_specs=[out_blk, out_blk],
        compiler_params=_cp(("parallel",)),
    )(a, b)


def _hgrn_bwd(z, o, do_a, states, lbv, hnw, partials, dz):
    ntb, nch = S // HBLK, HBLK // CHUNK
    n = len(GATHER_IDS)

    def body(hq_ref, hf_ref, hi_ref, hg_ref, o_ref, doa_ref, st_ref, lb_ref, hnw_ref, p0, p1, p2, dz_in,
             dz_ref, glb_ref, ghn_ref, e0, e1, e2, dstate, send_sems, recv_sems):
        dhq_ref, dhf_ref, dhi_ref, dhg_ref = (dz_ref.at[:, pl.ds(j * D, D)] for j in range(4))
        start, end = _exchange_chips_steps((p0, p1, p2), (e0, e1, e2), send_sems, recv_sems)

        @pl.when(pl.program_id(0) == 0)
        def _():
            dstate[...] = jnp.zeros_like(dstate)
            glb_ref[...] = jnp.zeros_like(glb_ref)
            ghn_ref[...] = jnp.zeros_like(ghn_ref)
            start()

        lb_a = lb_ref[...]
        hq_a, hg_a = hq_ref[...], hg_ref[...]
        q_a, k_a, g_a, v_a, sg_a, f_a, b_a = _hgrn_cols(hq_a, hf_ref[...], hi_ref[...], lb_a)
        bex_a = b_a - g_a
        eb_a = jnp.exp(b_a)
        w = hnw_ref[...]
        mask = _tril_mask(CHUNK)
        upper = _block_tri(CHUNK, CHUNK, upper=True)
        for h in range(HEADS):
            cols = slice(128 * h, 128 * h + 128)
            q, k, v, b, bex, eb = q_a[:, cols], k_a[:, cols], v_a[:, cols], b_a[:, cols], bex_a[:, cols], eb_a[:, cols]
            hq, hg, sg, f, lb = hq_a[:, cols], hg_a[:, cols], sg_a[:, cols], f_a[:, cols], lb_a[:, cols]
            ov, doa = o_ref[:, cols], doa_ref[:, cols]
            r = lax.rsqrt(jnp.mean(ov * ov, axis=-1, keepdims=True) + EPS)
            n = ov * r
            sil = _silu(hg)
            dhg_ref[:, cols] = (doa * n * w * _dsilu(hg)).astype(BF16)
            ghn_ref[h] += jnp.sum(doa * sil * n, axis=0, keepdims=True)
            dn = doa * sil * w
            do = r * (dn - n * jnp.mean(dn * n, axis=-1, keepdims=True))

            dst = dstate[h]
            dq_l, dk_l, dv_l, dg_l = [None] * nch, [None] * nch, [None] * nch, [None] * nch
            for c in reversed(range(nch)):
                r0 = c * CHUNK
                rows = slice(r0, r0 + CHUNK)
                st = st_ref[h, c]
                bc, kc, qc = b[rows], k[rows], q[rows]
                vb, dob = v[rows].astype(BF16), do[rows].astype(BF16)
                b_last = bc[CHUNK - 1:CHUNK]
                e_last = jnp.exp(b_last)
                ekl = jnp.exp(b_last - bc)
                dstb = dst.astype(BF16)
                a, qs_l, ks_l, ek_l, eq_l = _chunk_scores(q, k, b, bex, r0, mask)
                da = jnp.where(mask, _dot_nt(dob, vb), 0.0)
                dv_l[c] = _dot_tn(a.astype(BF16), dob) + _dot_nt((kc * ekl).astype(BF16), dstb)
                dq_inter = _dot(dob, st.astype(BF16)) * eb[rows]
                dk_state = _dot(vb, dstb) * ekl
                dq_parts, dk_intra = [], jnp.zeros((CHUNK, 128), F32)
                dab = da.astype(BF16)
                for i in range(CHUNK // SUB):
                    da_i = dab[SUB * i:SUB * (i + 1)]
                    ks_hi, ks_lo = _split2(ks_l[i])
                    qs_hi, qs_lo = _split2(qs_l[i])
                    dq_parts.append((_dot(da_i, ks_hi) + _dot(da_i, ks_lo)) * eq_l[i])
                    dk_intra = dk_intra + (_dot_tn(da_i, qs_hi) + _dot_tn(da_i, qs_lo)) * ek_l[i]
                dq = jnp.concatenate(dq_parts, axis=0) + dq_inter
                dk = dk_intra + dk_state
                last = (e_last * jnp.sum(st * dst, axis=0, keepdims=True)
                        + jnp.sum(kc * dk_state, axis=0, keepdims=True))
                dg_l[c] = _dot_ones(upper, qc * dq - kc * dk) + last
                dq_l[c], dk_l[c] = dq, dk
                dst = dst * e_last + _dot_tn(dob, (qc * eb[rows]).astype(BF16))
            dstate[h] = dst
            dq, dk = jnp.concatenate(dq_l, axis=0), jnp.concatenate(dk_l, axis=0)
            dg, dv = jnp.concatenate(dg_l, axis=0), jnp.concatenate(dv_l, axis=0)
            dhq_ref[:, cols] = (dq * _dsilu(hq)).astype(BF16)
            dhi_ref[:, cols] = dv.astype(BF16)
            df = dg / f - dk
            dhf_ref[:, cols] = (df * (1.0 - lb) * sg * (1.0 - sg)).astype(BF16)
            glb_ref[:, cols] += jnp.sum(df * (1.0 - sg), axis=0, keepdims=True)

        pl.when(pl.program_id(0) == ntb - 1)(end)

    def rev(t):
        return ntb - 1 - t

    def zcol(j):
        return pl.BlockSpec((HBLK, D), lambda t: (rev(t), j))

    blk = pl.BlockSpec((HBLK, D), lambda t: (rev(t), 0))
    any_spec = pl.BlockSpec(memory_space=pl.ANY)
    return pl.pallas_call(
        body, name="hgrn_bwd", grid=(ntb,),
        out_shape=[jax.ShapeDtypeStruct((S, IN_COLS), BF16)]
        + [jax.ShapeDtypeStruct((1, D), F32), jax.ShapeDtypeStruct((HEADS, 1, 128), F32)]
        + [jax.ShapeDtypeStruct((3,) + SHARD_SHAPES[a], BF16) for a in GATHER_IDS],
        in_specs=[zcol(0), zcol(1), zcol(2), zcol(3), blk, blk,
                  pl.BlockSpec((HEADS, nch, 128, 128), lambda t: (0, rev(t), 0, 0)),
                  pl.BlockSpec((1, D), lambda t: (0, 0)), pl.BlockSpec((1, 128), lambda t: (0, 0))]
        + [any_spec] * (n + 1),
        out_specs=[pl.BlockSpec((HBLK, DZ_ATT), lambda t: (rev(t), 0)), pl.BlockSpec((1, D), lambda t: (0, 0)),
                   pl.BlockSpec((HEADS, 1, 128), lambda t: (0, 0, 0))] + [any_spec] * n,
        scratch_shapes=[pltpu.VMEM((HEADS, 128, 128), F32), pltpu.SemaphoreType.DMA((3 * n,)),
                        pltpu.SemaphoreType.DMA((3 * n,))],
        input_output_aliases={9 + n: 0},
        compiler_params=_cp(("arbitrary",)),
    )(z, z, z, z, o, do_a, states, lbv, hnw, *partials, dz)


def _attn_bwd(z, qsb, ksb, vsb, cc, ss, ob, lse, do_bg, dz):
    def body(qs, ks, vs, ag_ref, cc_ref, ss_ref, ob_ref, lse_ref, dobg_ref, dz_in, dz_hbm,
             tmp, dos, dqs, dks, dvs, dkp, dvp, do_t, ls0_t, ls1_t, dl0_t, dl1_t, ls0, ls1, dl0, dl1,
             stage, stage_sem):
        pair, g = pl.program_id(0), pl.program_id(1)

        def out_copy(j):
            tile = DZ_ATT // LANES + (36 + pair if j == 3 else 12 * j + 4 * g + pair)
            return pltpu.make_async_copy(
                stage.at[j], dz_hbm.at[:, pl.ds(pl.multiple_of(tile * LANES, LANES), LANES)], stage_sem.at[j])

        def restage(j, value):
            pl.when(pair * 3 + g > 0)(lambda: out_copy(j).wait())
            stage[j] = value
            out_copy(j).start()

        pl.when(g == 2)(lambda: out_copy(3).wait())
        first_half = _half_mask()
        prev_ok, cur_ok = _attn_masks()
        lane = lax.broadcasted_iota(jnp.int32, (1, LANES), 1)
        heads = (lane < 64, lane >= 64)
        nblk = _group_blocks(g)
        cc_v, ss_v = cc_ref[...], ss_ref[...]

        @pl.when(g == 0)
        def _():
            ag, obv, dobg = ag_ref[...], ob_ref[...], dobg_ref[...]
            stage[3] = (dobg * obv * _dsilu(ag)).astype(BF16)
            out_copy(3).start()
            dob = dobg * _silu(ag)
            do_t[...] = dob
            prod = dob * obv
            dl = jnp.concatenate(
                [jnp.broadcast_to(jnp.sum(prod[:, 0:64], axis=-1, keepdims=True), (S, 64)),
                 jnp.broadcast_to(jnp.sum(prod[:, 64:128], axis=-1, keepdims=True), (S, 64))], axis=1)
            dl0_t[...] = dl

        _to_residues_dyn(g, dos, do_t)
        _to_residues_dyn(g, ls0, lse_ref)
        _to_residues_dyn(g, dl0, dl0_t)

        def unit(u, carry):
            start = pl.multiple_of(u * 128, 128)
            cur = pl.ds(start, 128)
            both = pl.ds(start, 256)
            pm = prev_ok & ((u & (nblk - 1)) != 0)
            qu, dou = qs[0, cur, :], dos[cur, :]
            kcat, vcat = ks[0, both, :], vs[0, both, :]
            dq_u = None
            q_l, do_l, ds_l, p_l = [], [], [], []
            ls_u, dl_u = ls0[cur, :], dl0[cur, :]
            ls_sw, dl_sw = pltpu.roll(ls_u, 64, 1), pltpu.roll(dl_u, 64, 1)
            for hh in range(2):
                q_h = jnp.where(heads[hh], qu, jnp.zeros((), BF16))
                do_h = jnp.where(heads[hh], dou, 0.0).astype(BF16)
                s = _dot_nt(q_h, kcat)
                dp = _dot_nt(do_h, vcat)
                lse_h = jnp.where(heads[hh], ls_u, ls_sw)
                dl_h = jnp.where(heads[hh], dl_u, dl_sw)
                pp = jnp.where(pm, jnp.exp(s[:, 0:128] - lse_h), 0.0)
                pc = jnp.where(cur_ok, jnp.exp(s[:, 128:256] - lse_h), 0.0)
                ds = jnp.concatenate([pp * (dp[:, 0:128] - dl_h), pc * (dp[:, 128:256] - dl_h)], axis=1).astype(BF16)
                dq = _dot(ds, kcat)
                dq_u = dq if hh == 0 else jnp.where(heads[1], dq, dq_u)
                q_l.append(q_h)
                do_l.append(do_h)
                ds_l.append(ds)
                p_l.append(jnp.concatenate([pp, pc], axis=1).astype(BF16))
            dkcat = _dot_tn(jnp.concatenate(ds_l, axis=0), jnp.concatenate(q_l, axis=0))
            dvcat = _dot_tn(jnp.concatenate(p_l, axis=0), jnp.concatenate(do_l, axis=0))
            dkp[cur, :] = dkcat[0:128]
            dks[cur, :] = dkcat[128:256]
            dvp[cur, :] = dvcat[0:128]
            dvs[cur, :] = dvcat[128:256]
            dqs[cur, :] = dq_u
            return carry

        lax.fori_loop(0, 16, unit, 0, unroll=ATT_UNROLL)
        dks[0:S - 128, :] += dkp[128:S, :]
        dvs[0:S - 128, :] += dvp[128:S, :]
        _from_residues_dyn(g, tmp, dqs)
        restage(0, (_rope(tmp[...], cc_v, -ss_v, first_half) * ATT_SCALE).astype(BF16))
        _from_residues_dyn(g, tmp, dks)
        restage(1, _rope(tmp[...], cc_v, -ss_v, first_half).astype(BF16))
        _from_residues_dyn(g, tmp, dvs)
        restage(2, tmp[...].astype(BF16))

        @pl.when(pair * 3 + g == 11)
        def _():
            for j in range(3):
                out_copy(j).wait()

    any_spec = pl.BlockSpec(memory_space=pl.ANY)
    buf = pltpu.VMEM((S, LANES), F32)
    padded_b = pltpu.VMEM((ATT_PAD + S, LANES), BF16)
    return pl.pallas_call(
        body, name="attn_bwd", grid=(4, 3),
        out_shape=jax.ShapeDtypeStruct((S, IN_COLS), BF16),
        in_specs=[pl.BlockSpec((1, S, LANES), lambda p, g: (g, 0, p)),
                  pl.BlockSpec((1, ATT_PAD + S, LANES), lambda p, g: (g, 0, p)),
                  pl.BlockSpec((1, ATT_PAD + S, LANES), lambda p, g: (g, 0, p))] + _attn_in_specs(3)[3:] + [any_spec],
        out_specs=any_spec,
        scratch_shapes=[buf] * 16 + [pltpu.VMEM((4, S, LANES), BF16), pltpu.SemaphoreType.DMA((4,))],
        input_output_aliases={9: 0},
        compiler_params=_cp(("arbitrary", "arbitrary")),
    )(qsb, ksb, vsb, z, cc, ss, ob, lse, do_bg, dz)


def _in_proj_bwd(dz, h, w_in):
    half = S // 2
    slab = (D, SHARD_COLS)

    def body(dz_hbm, h_hbm, w_hbm, dh_hbm, g_chip, r1_hbm, relay_hbm, r2_hbm,
             h_buf, dz_buf, stage_d, r1_buf, stage_i, acc,
             dz_sem, w_sem, h_sem, r1_sem, out_sem, send_d, recv_d, send_i, recv_i):
        x, y, c = _mesh_pos()
        sibling = (x, y, 1 - c)
        north = c == 1
        near = (jnp.where(north, 1 - x, x), jnp.where(north, y, 1 - y))
        far = (jnp.where(north, x, 1 - x), jnp.where(north, 1 - y, y))
        chips = [(1 - x, 1 - y), near, far, (x, y)]

        def cols(d):
            return pl.ds(pl.multiple_of(d * SHARD_COLS, LANES), SHARD_COLS)

        blocks = []
        for q_sib, q in zip([chips[0], far, near, chips[3]], chips):
            blocks += [4 * q_sib[0] + 2 * q_sib[1] + (1 - c), 4 * q[0] + 2 * q[1] + c]

        def dz_tile(t):
            return _SplitCopy(dz_hbm.at[pl.ds((t % 2) * half, half), cols(blocks[t // 2])],
                                         dz_buf.at[t % 2], dz_sem.at[t % 2])

        def to_sibling(i):
            return pltpu.make_async_remote_copy(
                src_ref=stage_d.at[i % 2], dst_ref=r1_hbm.at[i], send_sem=send_d.at[i], recv_sem=recv_d.at[i],
                device_id=sibling, device_id_type=MESH)

        def to_owner(i):
            dst = relay_hbm if i == 0 else r2_hbm.at[i - 1]
            return pltpu.make_async_remote_copy(
                src_ref=stage_i.at[i], dst_ref=dst, send_sem=send_i.at[i], recv_sem=recv_i.at[i],
                device_id=(*(far if i == 2 else near), c), device_id_type=MESH)

        h_copy = _SplitCopy(h_hbm, h_buf, h_sem)
        h_copy.start()
        dz_tile(0).start()
        h_copy.wait()
        for b in range(8):
            i = b // 2
            g = None
            for r in range(2):
                t = 2 * b + r
                if t + 1 < 16:
                    dz_tile(t + 1).start()
                dz_tile(t).wait()
                part = _dot_tn(h_buf[r * half:(r + 1) * half, :], dz_buf[t % 2])
                g = part if g is None else g + part
                if b % 2 == 1 and r == 0:
                    to_sibling(i).wait_recv()
                    r1_copy = _SplitCopy(r1_hbm.at[i], r1_buf, r1_sem)
                    r1_copy.start()
            if b % 2 == 0:
                if i >= 2:
                    to_sibling(i - 2).wait_send()
                stage_d[i % 2] = g.astype(BF16)
                to_sibling(i).start()
            else:
                r1_copy.wait()
                g = g + r1_buf[...].astype(F32)
                if i == 2:
                    to_owner(0).wait_recv()
                    relay_copy = _SplitCopy(relay_hbm, r1_buf, r1_sem)
                    relay_copy.start()
                    relay_copy.wait()
                    g = g + r1_buf[...].astype(F32)
                if i < 3:
                    stage_i[i] = g.astype(BF16)
                    to_owner(i).start()
                else:
                    g_chip[...] = g
        to_sibling(2).wait_send()
        to_sibling(3).wait_send()

        def dz2(t):
            return _SplitCopy(
                dz_hbm.at[pl.ds((t % 2) * half, half), pl.ds((t // 2) * SHARD_COLS, SHARD_COLS)],
                dz_buf.at[t % 2], dz_sem.at[t % 2])

        def w2(b):
            return _SplitCopy(w_hbm.at[:, pl.ds(b * SHARD_COLS, SHARD_COLS)],
                                         stage_d.at[b % 2], w_sem.at[b % 2])

        dz2(0).start()
        w2(0).start()
        for t in range(16):
            b, r = t // 2, t % 2
            if t + 1 < 16:
                dz2(t + 1).start()
            if r == 0:
                if b + 1 < 8:
                    w2(b + 1).start()
                w2(b).wait()
            dz2(t).wait()
            part = _dot_nt(dz_buf[t % 2], stage_d[b % 2])
            if b == 0:
                acc[r] = part
            else:
                acc[r] += part
        dh_out = [_SplitCopy(acc.at[r], dh_hbm.at[pl.ds(r * half, half), :], out_sem.at[r])
                  for r in range(2)]
        for cp in dh_out:
            cp.start()
        for cp in dh_out:
            cp.wait()
        for i in range(3):
            to_owner(i).wait_send()
        for i in (1, 2):
            to_owner(i).wait_recv()

    any_spec = pl.BlockSpec(memory_space=pl.ANY)
    return pl.pallas_call(
        body, name="in_proj_bwd",
        out_shape=[jax.ShapeDtypeStruct((S, D), F32), jax.ShapeDtypeStruct(slab, F32),
                   jax.ShapeDtypeStruct((4,) + slab, BF16), jax.ShapeDtypeStruct(slab, BF16),
                   jax.ShapeDtypeStruct((2,) + slab, BF16)],
        in_specs=[any_spec] * 3,
        out_specs=[any_spec, pl.BlockSpec(memory_space=pltpu.VMEM), any_spec, any_spec, any_spec],
        scratch_shapes=[pltpu.VMEM((S, D), BF16), pltpu.VMEM((2, half, SHARD_COLS), BF16),
                        pltpu.VMEM((2,) + slab, BF16), pltpu.VMEM(slab, BF16), pltpu.VMEM((3,) + slab, BF16),
                        pltpu.VMEM((2, half, D), F32),
                        pltpu.SemaphoreType.DMA((2,)), pltpu.SemaphoreType.DMA((2,)), pltpu.SemaphoreType.DMA,
                        pltpu.SemaphoreType.DMA, pltpu.SemaphoreType.DMA((2,)),
                        pltpu.SemaphoreType.DMA((4,)), pltpu.SemaphoreType.DMA((4,)),
                        pltpu.SemaphoreType.DMA((3,)), pltpu.SemaphoreType.DMA((3,))],
        compiler_params=_cp(),
    )(dz, h, w_in)


def _grad_x(x, norm_w, dh, dx2):
    tr = 256

    def body(x_ref, w_ref, dh_ref, dx2_ref, gx_ref, gnw_ref):
        @pl.when(pl.program_id(0) == 0)
        def _():
            gnw_ref[...] = jnp.zeros_like(gnw_ref)

        xv, dhv = x_ref[...], dh_ref[...]
        r = lax.rsqrt(jnp.mean(xv * xv, axis=-1, keepdims=True) + EPS)
        n = xv * r
        gnw_ref[...] += jnp.sum(dhv * n, axis=0, keepdims=True)
        dn = dhv * w_ref[...]
        gx_ref[...] = dx2_ref[...] + r * (dn - n * jnp.mean(dn * n, axis=-1, keepdims=True))

    row = pl.BlockSpec((tr, D), lambda i: (i, 0))
    vec = pl.BlockSpec((1, D), lambda i: (0, 0))
    return pl.pallas_call(
        body, name="grad_x", grid=(S // tr,),
        out_shape=[jax.ShapeDtypeStruct((S, D), F32), jax.ShapeDtypeStruct((1, D), F32)],
        in_specs=[row, vec, row, row], out_specs=[row, vec],
        compiler_params=_cp(("arbitrary",)),
    )(x, norm_w, dh, dx2)


def _rope_tables(positions):
    inv_freq = 10000.0 ** (-jnp.arange(0, 64, 2, dtype=F32) / 64)
    ang = positions.astype(F32)[:, None] * inv_freq[None, :]
    cos, sin = jnp.cos(ang), jnp.sin(ang)
    return jnp.tile(cos, (1, 4)), jnp.tile(jnp.concatenate([-sin, sin], axis=1), (1, 2))


def _local_step(x, positions, norm_w, lb_logits, hnw, fnw, target, w_in_shard, small_shards, core):
    cc, ss = _rope_tables(positions)
    lbv = jax.nn.sigmoid(lb_logits[0:1] - lb_logits[1:2])
    z, w_in, h = _in_proj_gather(x, norm_w, w_in_shard)
    o, o_a, states = _hgrn_fwd(z, lbv, hnw)
    ob, lse, o_bg, qsb, ksb, vsb, w_a, w_b, w_out = _attn_fwd(z, cc, ss, small_shards)
    dx2, dx2b, dz, do_a, do_bg, merged, dy_a, dy_b, tail_small = _tail(x, o_a, o_bg, z, target, w_a, w_b, w_out, fnw)
    g_out, gb_out = _tn_matmul(merged, dx2b, "grad_w_out")
    g_a, gb_a = _tn_matmul(o_a, dy_a, "grad_w_a")
    g_b, gb_b = _tn_matmul(o_bg, dy_b, "grad_w_b")
    grads, gb = (g_a, g_b, g_out), (gb_a, gb_b, gb_out)
    r1 = _exchange_sibling(GATHER_IDS, gb)
    pb = [_chip_partials(a, grads[i], r1[i], core) for i, a in enumerate(GATHER_IDS)]
    dz, glb, ghn, *r2 = _hgrn_bwd(z, o, do_a, states, lbv, hnw, pb, dz)
    dz = _attn_bwd(z, qsb, ksb, vsb, cc, ss, ob, lse, do_bg, dz)
    dh, g_chip_in, _, _, r2_in = _in_proj_bwd(dz, h, w_in)
    grad_x, gnw = _grad_x(x, norm_w, dh, dx2)
    ghn_row = jnp.pad(jnp.sum(ghn, axis=0), ((0, 0), (0, D - 128)))
    small = jnp.concatenate([gnw, glb, ghn_row, tail_small[0:2], jnp.zeros((3, D), F32)], axis=0)
    return grad_x, (g_chip_in, r2_in), grads, r1, r2, small


def kernel(x, positions, norm_w, w_in, lb_logits, hgrn_norm_w, w_branch_a, w_branch_b, w_out, final_norm_w, loss_target, m_norm_w, m_w_in, m_lb_logits, m_hgrn_norm_w, m_w_branch_a, m_w_branch_b, m_w_out, m_final_norm_w, v_norm_w, v_w_in, v_lb_logits, v_hgrn_norm_w, v_w_branch_a, v_w_branch_b, v_w_out, v_final_norm_w):
    ix, iy, ic = _mesh_pos()
    core = jnp.reshape(ic, (1,)).astype(jnp.int32)
    pos = jnp.stack([4 * ix + 2 * iy + ic, 2 * ix + iy]).astype(jnp.int32)

    shards = [w_in[0], w_branch_a[0], w_branch_b[0], w_out[0]]
    moments_m = [m_w_in[0], m_w_branch_a[0], m_w_branch_b[0], m_w_out[0]]
    moments_v = [v_w_in[0], v_w_branch_a[0], v_w_branch_b[0], v_w_out[0]]
    names = ("w_in", "w_a", "w_b", "w_out")
    ids = GATHER_IDS
    shards_b = [_cast_bf16(w, f"cast_{nm}") for w, nm in zip(shards, names)]

    fnw2 = final_norm_w.reshape(1, D)
    grad_x, (g_chip_in, r2_in), grads, r1, r2, small = _local_step(
        x[0], positions[0], norm_w, lb_logits, hgrn_norm_w, fnw2, loss_target[0], shards_b[0], shards_b[1:], core)

    gathered = _gather_small(small)
    big =[_reduce_own_and_update(shards[0], moments_m[0], moments_v[0], g_chip_in, r2_in)]
    big += [_reduce_and_update(a, shards[a], moments_m[a], moments_v[a], grads[i], r1[i], r2[i], pos)
            for i, a in enumerate(ids)]
    sm = _small_update(gathered, norm_w, lb_logits, hgrn_norm_w, fnw2,
                       (m_norm_w, m_lb_logits, m_hgrn_norm_w, m_final_norm_w.reshape(1, D),
                        v_norm_w, v_lb_logits, v_hgrn_norm_w, v_final_norm_w.reshape(1, D)))
    loss = sm[0][0, 0]
    outs = [loss, grad_x[None]]
    for kind in range(4):
        s_nw, s_lb, s_hn, s_fn = sm[1 + 4 * kind:5 + 4 * kind]
        outs += [s_nw, big[0][kind][None], s_lb, s_hn, big[1][kind][None], big[2][kind][None],
                 big[3][kind][None], s_fn.reshape(D)]
    return tuple(outs)
```

```python
import functools

import jax
import jax.numpy as jnp
from jax import lax
from jax.experimental import pallas as pl
from jax.experimental.pallas import tpu as pltpu

F32 = jnp.float32
BF16 = jnp.bfloat16
MESH = pl.DeviceIdType.MESH

S = 2048
D = 1024
NDEV = 8
HEADS = 8
CHUNK = 64
SUB = 16
HBLK = 256
ATT_PAD = 128
ATT_UNROLL = 16
COPY_PARTS = 4
EXP_CLAMP = 80.0
EPS = 1e-6
IN_COLS = 11264
SHARD_COLS = IN_COLS // NDEV
DZ_ATT = 4096
DZ_GATES = 9216
ATT_DILS = (1, 4, 16)
ATT_SCALE = 64 ** -0.5
LANES = 128

ADAM_LR, ADAM_B1, ADAM_B2, ADAM_EPS, ADAM_WD, ADAM_STEP = 0.001, 0.9, 0.999, 1e-08, 0.01, 10

VMEM_LIMIT = 56 * 1024 * 1024


def _cp(sem=None, **kw):
    return pltpu.CompilerParams(dimension_semantics=sem, vmem_limit_bytes=VMEM_LIMIT, **kw)


def _dot(a, b):
    return jnp.dot(a, b, preferred_element_type=F32)


def _dot_nt(a, b):
    return lax.dot_general(a, b, (((1,), (1,)), ((), ())), preferred_element_type=F32)


def _dot_tn(a, b):
    return lax.dot_general(a, b, (((0,), (0,)), ((), ())), preferred_element_type=F32)


def _split2(x):
    hi = x.astype(BF16)
    lo = (x - hi.astype(F32)).astype(BF16)
    return hi, lo


def _split3(x):
    hi = x.astype(BF16)
    r = x - hi.astype(F32)
    mid = r.astype(BF16)
    lo = (r - mid.astype(F32)).astype(BF16)
    return hi, mid, lo


def _dot_ones(ones_bf16, x):
    hi, mid, lo = _split3(x)
    return _dot(ones_bf16, hi) + _dot(ones_bf16, mid) + _dot(ones_bf16, lo)


def _silu(x):
    return x * jax.nn.sigmoid(x)


def _dsilu(x):
    s = jax.nn.sigmoid(x)
    return s * (1.0 + x * (1.0 - s))


def _mesh_pos():
    return lax.axis_index("x"), lax.axis_index("y"), lax.axis_index("c")


class _SplitCopy:
    def __init__(self, src, dst, sem):
        self.src, self.dst, self.sem = src, dst, sem

    def start(self):
        rows = self.src.shape[0] // COPY_PARTS
        for p in range(COPY_PARTS):
            chunk = pl.ds(p * rows, rows)
            pltpu.make_async_copy(self.src.at[chunk], self.dst.at[chunk], self.sem).start()

    def wait(self):
        pltpu.make_async_copy(self.src, self.dst, self.sem).wait()


def _shard_of(ref, a, d):
    if a == 0:
        return ref.at[:, pl.ds(pl.multiple_of(d * SHARD_COLS, LANES), SHARD_COLS)]
    if a == 2:
        return ref.at[:, pl.ds(pl.multiple_of(d * LANES, LANES), LANES)]
    return ref.at[pl.ds(pl.multiple_of(d * 128, 128), 128), :]


FULL_SHAPES = ((D, IN_COLS), (D, D), (512, D), (D, D))
SHARD_SHAPES = ((D, SHARD_COLS), (128, D), (512, 128), (128, D))


def _allgather_steps(ids, ins, outs, send_sems, recv_sems, local_sems):
    n = len(ids)
    x, y, c = _mesh_pos()
    me, sibling = (x, y, c), (x, y, 1 - c)
    chips = [(1 - x, y), (x, 1 - y), (1 - x, 1 - y)]

    def blk(a, p):
        return _shard_of(outs[a], ids[a], 4 * p[0] + 2 * p[1] + p[2])

    def copy(a, k, block, to, src=None):
        return pltpu.make_async_remote_copy(
            src_ref=blk(a, block) if src is None else src, dst_ref=blk(a, block),
            send_sem=send_sems.at[a * 7 + k], recv_sem=recv_sems.at[a * 7 + k],
            device_id=to, device_id_type=MESH)

    mine = [pltpu.make_async_copy(ins[a], blk(a, me), local_sems.at[a]) for a in range(n)]
    first = []
    for a in range(n):
        first += [copy(a, 1 + j, me, (*chip, c), src=ins[a]) for j, chip in enumerate(chips)]
    for a in range(n):
        first.append(copy(a, 0, me, sibling, src=ins[a]))
    passed = [copy(a, 4 + j, (*chip, c), sibling) for j, chip in enumerate(chips) for a in range(n)]

    def start():
        for cp in mine + first:
            cp.start()

    def middle():
        for j, chip in enumerate(chips):
            for a in range(n):
                copy(a, 1 + j, (*chip, c), me).wait_recv()
                passed[j * n + a].start()

    def end():
        for a in range(n):
            copy(a, 0, sibling, me).wait_recv()
        for j, chip in enumerate(chips):
            for a in range(n):
                copy(a, 4 + j, (*chip, 1 - c), me).wait_recv()
        for cp in first + passed:
            cp.wait_send()
        for cp in mine:
            cp.wait()

    return start, middle, end


def _in_proj_gather(x, norm_w, w_shard):
    half = S // 2
    slab = (D, SHARD_COLS)
    xt = 512

    def body(x_hbm, nw_ref, w_hbm, z_hbm, wfull_hbm, h_hbm, h_buf, land, zstage, xbuf,
             h_sem, own_sem, z_sem, wout_sem, x_sem, send_sems, recv_sems):
        x, y, c = _mesh_pos()
        sibling = (x, y, 1 - c)
        north = c == 1

        def chips_of(first_x):
            near = (jnp.where(first_x, 1 - x, x), jnp.where(first_x, y, 1 - y))
            far = (jnp.where(first_x, x, 1 - x), jnp.where(first_x, 1 - y, y))
            return [near, far, (1 - x, 1 - y)]

        mine, theirs = chips_of(north), chips_of(jnp.logical_not(north))

        def dev(chip, core):
            return 4 * chip[0] + 2 * chip[1] + core

        block_of = ([dev((x, y), c), dev((x, y), 1 - c)] + [dev(q, c) for q in mine]
                    + [dev(q, 1 - c) for q in theirs])

        def cols(d):
            if isinstance(d, int):
                return pl.ds(d * SHARD_COLS, SHARD_COLS)
            return pl.ds(pl.multiple_of(d * SHARD_COLS, LANES), SHARD_COLS)

        def send(k, src, dst_slot, to):
            return pltpu.make_async_remote_copy(
                src_ref=src, dst_ref=land.at[dst_slot], send_sem=send_sems.at[k], recv_sem=recv_sems.at[k],
                device_id=to, device_id_type=MESH)

        def to_sibling():
            return send(0, w_hbm, 1, sibling)

        def to_chip(j):
            if j == 2:
                return send(3, land.at[2], 4, (*mine[1], c))
            return send(1 + j, w_hbm, 2 + j, (*mine[j], c))

        def pass_on(j):
            return send(4 + j, land.at[2 + j], 5 + j, sibling)

        def x_tile(i):
            return _SplitCopy(x_hbm.at[pl.ds(i * xt, xt), :], xbuf.at[i % 2], x_sem.at[i % 2])

        own = _SplitCopy(w_hbm, land.at[0], own_sem)
        own.start()
        x_tile(0).start()
        to_sibling().start()
        to_chip(0).start()
        for i in range(S // xt):
            if i + 1 < S // xt:
                x_tile(i + 1).start()
            x_tile(i).wait()
            xv = xbuf[i % 2]
            r = lax.rsqrt(jnp.mean(xv * xv, axis=-1, keepdims=True) + EPS)
            h_buf[i * xt:(i + 1) * xt, :] = (xv * r * nw_ref[...]).astype(BF16)
        h_out = _SplitCopy(h_buf, h_hbm, h_sem)
        h_out.start()
        own.wait()

        def multiply(slot, n_done):
            d = block_of[slot]
            out = _SplitCopy(land.at[slot], wfull_hbm.at[:, cols(d)], wout_sem.at[slot])
            out.start()
            for r in range(2):
                rows = pl.ds(r * half, half)
                zc = _SplitCopy(zstage.at[r], z_hbm.at[rows, cols(d)], z_sem.at[r])
                if n_done > 0:
                    zc.wait()
                zstage[r] = _dot(h_buf[r * half:(r + 1) * half, :], land[slot])
                zc.start()
            return out

        outs = [multiply(0, 0)]
        to_sibling().wait_recv()
        outs.append(multiply(1, 1))
        done = 2
        for j in range(3):
            to_chip(j).wait_recv()
            pass_on(j).start()
            to_chip(j).wait_send()
            if j < 2:
                to_chip(j + 1).start()
            outs.append(multiply(2 + j, done))
            pass_on(j).wait_recv()
            outs.append(multiply(5 + j, done + 1))
            done += 2
        for r in range(2):
            _SplitCopy(zstage.at[r], z_hbm.at[pl.ds(r * half, half), cols(0)], z_sem.at[r]).wait()
        for out in outs:
            out.wait()
        h_out.wait()
        to_sibling().wait_send()
        for j in range(3):
            pass_on(j).wait_send()

    any_spec = pl.BlockSpec(memory_space=pl.ANY)
    return pl.pallas_call(
        body, name="in_proj_gather",
        out_shape=[jax.ShapeDtypeStruct((S, IN_COLS), F32), jax.ShapeDtypeStruct((D, IN_COLS), BF16),
                   jax.ShapeDtypeStruct((S, D), BF16)],
        in_specs=[any_spec, pl.BlockSpec(memory_space=pltpu.VMEM), any_spec], out_specs=[any_spec] * 3,
        scratch_shapes=[pltpu.VMEM((S, D), BF16), pltpu.VMEM((8,) + slab, BF16), pltpu.VMEM((2, half, SHARD_COLS), F32),
                        pltpu.VMEM((2, xt, D), F32),
                        pltpu.SemaphoreType.DMA, pltpu.SemaphoreType.DMA, pltpu.SemaphoreType.DMA((2,)),
                        pltpu.SemaphoreType.DMA((8,)), pltpu.SemaphoreType.DMA((2,)),
                        pltpu.SemaphoreType.DMA((7,)), pltpu.SemaphoreType.DMA((7,))],
        compiler_params=_cp(),
    )(x, norm_w, w_shard)


def _exchange_sibling(ids, gb):
    n = len(gb)

    def body(*refs):
        ins, outs = refs[:n], refs[n:2 * n]
        send_sems, recv_sems = refs[2 * n:]
        x, y, c = _mesh_pos()
        sibling = (x, y, 1 - c)
        copies = []
        for i, a in enumerate(ids):
            for q in range(4):
                copies.append(pltpu.make_async_remote_copy(
                    src_ref=_shard_of(ins[i], a, 2 * q + (1 - c)), dst_ref=outs[i].at[q],
                    send_sem=send_sems.at[i * 4 + q], recv_sem=recv_sems.at[i * 4 + q],
                    device_id=sibling, device_id_type=MESH))
        for cp in copies:
            cp.start()
        for cp in copies:
            cp.wait()

    any_spec = pl.BlockSpec(memory_space=pl.ANY)
    return pl.pallas_call(
        body, name="grads_to_sibling",
        out_shape=[jax.ShapeDtypeStruct((4,) + SHARD_SHAPES[a], BF16) for a in ids],
        in_specs=[any_spec] * n, out_specs=[any_spec] * n,
        scratch_shapes=[pltpu.SemaphoreType.DMA((4 * n,)), pltpu.SemaphoreType.DMA((4 * n,))],
    )(*gb)


def _exchange_chips_steps(ins, outs, send_sems, recv_sems):
    x, y, c = _mesh_pos()
    chips = [(1 - x, y), (x, 1 - y), (1 - x, 1 - y)]
    copies = []
    for a in range(len(ins)):
        for k, chip in enumerate(chips):
            copies.append(pltpu.make_async_remote_copy(
                src_ref=ins[a].at[2 * chip[0] + chip[1]], dst_ref=outs[a].at[k],
                send_sem=send_sems.at[a * 3 + k], recv_sem=recv_sems.at[a * 3 + k],
                device_id=(*chip, c), device_id_type=MESH))

    def start():
        for cp in copies:
            cp.start()

    def end():
        for cp in copies:
            cp.wait()

    return start, end


def _gather_small_steps(small_ref, small_out, ssend, srecv, local_sem):
    x, y, c = _mesh_pos()
    me = 4 * x + 2 * y + c
    copies = []
    for r in range(1, NDEV):
        peer = (1 - x if r & 4 else x, 1 - y if r & 2 else y, 1 - c if r & 1 else c)
        copies.append(pltpu.make_async_remote_copy(
            src_ref=small_ref, dst_ref=small_out.at[me],
            send_sem=ssend.at[r - 1], recv_sem=srecv.at[r - 1],
            device_id=peer, device_id_type=MESH))
    own = pltpu.make_async_copy(small_ref, small_out.at[me], local_sem)

    def start():
        own.start()
        for cp in copies:
            cp.start()

    def end():
        for cp in copies:
            cp.wait()
        own.wait()

    return start, end


def _gather_small(small):
    def body(small_ref, small_out, ssend, srecv, local_sem):
        start, end = _gather_small_steps(small_ref, small_out, ssend, srecv, local_sem)
        start()
        end()

    any_spec = pl.BlockSpec(memory_space=pl.ANY)
    return pl.pallas_call(
        body, name="gather_small",
        out_shape=jax.ShapeDtypeStruct((NDEV,) + small.shape, F32),
        in_specs=[any_spec], out_specs=any_spec,
        scratch_shapes=[pltpu.SemaphoreType.DMA((NDEV - 1,)), pltpu.SemaphoreType.DMA((NDEV - 1,)),
                        pltpu.SemaphoreType.DMA],
    )(small)


def _shard_tiles(a):
    rows, cols = SHARD_SHAPES[a]
    tr = min(rows, 256)
    return (tr, cols), rows // tr


def _full_index(a, d, i):
    (tr, _), nt = _shard_tiles(a)
    if a in (0, 2):
        return (i, d)
    return (d * nt + i, 0)


def _cast_bf16(x, name):
    rows, cols = x.shape
    tr = min(rows, 256)

    def body(x_ref, o_ref):
        o_ref[...] = x_ref[...].astype(BF16)

    return pl.pallas_call(
        body, name=name, out_shape=jax.ShapeDtypeStruct(x.shape, BF16), grid=(rows // tr,),
        in_specs=[pl.BlockSpec((tr, cols), lambda i: (i, 0))],
        out_specs=pl.BlockSpec((tr, cols), lambda i: (i, 0)),
        compiler_params=_cp(("parallel",)),
    )(x)


def _chip_partials(a, g_full, r1, core):
    tile, nt = _shard_tiles(a)

    def body(c_ref, g_ref, r_ref, o_ref):
        o_ref[0] = (g_ref[...] + r_ref[0].astype(F32)).astype(BF16)

    grid_spec = pltpu.PrefetchScalarGridSpec(
        num_scalar_prefetch=1, grid=(4, nt),
        in_specs=[pl.BlockSpec(tile, lambda q, i, c: _full_index(a, 2 * q + c[0], i)),
                  pl.BlockSpec((1,) + tile, lambda q, i, c: (q, i, 0))],
        out_specs=pl.BlockSpec((1,) + tile, lambda q, i, c: (q, i, 0)))
    return pl.pallas_call(
        body, name=f"chip_partials_{a}", grid_spec=grid_spec,
        out_shape=jax.ShapeDtypeStruct((4,) + SHARD_SHAPES[a], BF16),
        compiler_params=_cp(("parallel", "parallel")),
    )(core, g_full, r1)


def _adam(w, g, m, v):
    m = ADAM_B1 * m + (1.0 - ADAM_B1) * g
    v = ADAM_B2 * v + (1.0 - ADAM_B2) * (g * g)
    m_hat = m / (1.0 - ADAM_B1 ** ADAM_STEP)
    v_hat = v / (1.0 - ADAM_B2 ** ADAM_STEP)
    delta = -ADAM_LR * (m_hat / (jnp.sqrt(v_hat) + ADAM_EPS) + ADAM_WD * w)
    return delta, m, v


def _reduce_and_update(a, w, m, v, g_full, r1, r2, pos):
    tile, nt = _shard_tiles(a)

    def body(p_ref, w_ref, m_ref, v_ref, g_ref, r1_ref, r2_ref, go_ref, do_ref, mo_ref, vo_ref):
        g = g_ref[...] + r1_ref[0].astype(F32)
        g = g + r2_ref[0].astype(F32)
        g = g + r2_ref[1].astype(F32)
        g = g + r2_ref[2].astype(F32)
        delta, m_new, v_new = _adam(w_ref[...], g, m_ref[...], v_ref[...])
        go_ref[...] = g
        do_ref[...] = delta
        mo_ref[...] = m_new
        vo_ref[...] = v_new

    own = pl.BlockSpec(tile, lambda i, p: (i, 0))
    grid_spec = pltpu.PrefetchScalarGridSpec(
        num_scalar_prefetch=1, grid=(nt,),
        in_specs=[own, own, own,
                  pl.BlockSpec(tile, lambda i, p: _full_index(a, p[0], i)),
                  pl.BlockSpec((1,) + tile, lambda i, p: (p[1], i, 0)),
                  pl.BlockSpec((3,) + tile, lambda i, p: (0, i, 0))],
        out_specs=[own] * 4)
    shp = jax.ShapeDtypeStruct(w.shape, F32)
    return pl.pallas_call(
        body, name=f"reduce_update_{a}", grid_spec=grid_spec, out_shape=[shp] * 4,
        compiler_params=_cp(("parallel",)),
    )(pos, w, m, v, g_full, r1, r2)


def _reduce_own_and_update(w, m, v, g_chip, r2):
    tile, nt = _shard_tiles(0)

    def body(w_ref, m_ref, v_ref, g_ref, r2_ref, go_ref, do_ref, mo_ref, vo_ref):
        g = g_ref[...] + r2_ref[0].astype(F32)
        g = g + r2_ref[1].astype(F32)
        delta, m_new, v_new = _adam(w_ref[...], g, m_ref[...], v_ref[...])
        go_ref[...] = g
        do_ref[...] = delta
        mo_ref[...] = m_new
        vo_ref[...] = v_new

    own = pl.BlockSpec(tile, lambda i: (i, 0))
    shp = jax.ShapeDtypeStruct(w.shape, F32)
    return pl.pallas_call(
        body, name="reduce_update_0", grid=(nt,), out_shape=[shp] * 4,
        in_specs=[own, own, own, own, pl.BlockSpec((2,) + tile, lambda i: (0, i, 0))], out_specs=[own] * 4,
        compiler_params=_cp(("parallel",)),
    )(w, m, v, g_chip, r2)


def _small_update(gathered, norm_w, lb_logits, hnw, fnw, moments):
    m_nw, m_lb, m_hn, m_fn, v_nw, v_lb, v_hn, v_fn = moments

    def body(g_ref, nw, lb, hn, fn, mnw, mlb, mhn, mfn, vnw, vlb, vhn, vfn,
             loss_o, g_nw, g_lb, g_hn, g_fn, d_nw, d_lb, d_hn, d_fn,
             mo_nw, mo_lb, mo_hn, mo_fn, vo_nw, vo_lb, vo_hn, vo_fn):
        tot = g_ref[0]
        for d in range(1, NDEV):
            tot = tot + g_ref[d]
        loss_o[...] = tot[4:5, 0:LANES]
        logits = lb[...]
        lbv = jax.nn.sigmoid(logits[0:1] - logits[1:2])
        chain = tot[1:2] * lbv * (1.0 - lbv)
        grads = (tot[0:1], jnp.concatenate([chain, -chain], axis=0), tot[2:3, 0:LANES], tot[3:4])
        outs = ((nw, mnw, vnw, g_nw, d_nw, mo_nw, vo_nw), (lb, mlb, vlb, g_lb, d_lb, mo_lb, vo_lb),
                (hn, mhn, vhn, g_hn, d_hn, mo_hn, vo_hn), (fn, mfn, vfn, g_fn, d_fn, mo_fn, vo_fn))
        for g, (w_r, m_r, v_r, g_o, d_o, m_o, v_o) in zip(grads, outs):
            delta, m_new, v_new = _adam(w_r[...], g, m_r[...], v_r[...])
            g_o[...] = g
            d_o[...] = delta
            m_o[...] = m_new
            v_o[...] = v_new

    shapes = [norm_w.shape, lb_logits.shape, hnw.shape, fnw.shape]
    out_shape = [jax.ShapeDtypeStruct((1, LANES), F32)] + [jax.ShapeDtypeStruct(s, F32) for s in shapes] * 4
    return pl.pallas_call(body, name="small_update", out_shape=out_shape, compiler_params=_cp())(
        gathered, norm_w, lb_logits, hnw, fnw, m_nw, m_lb, m_hn, m_fn, v_nw, v_lb, v_hn, v_fn)


def _block_tri(n, block, upper=False):
    r = lax.broadcasted_iota(jnp.int32, (n, n), 0)
    c = lax.broadcasted_iota(jnp.int32, (n, n), 1)
    keep = (c >= r) if upper else (c <= r)
    return jnp.where(keep & ((r // block) == (c // block)), 1.0, 0.0).astype(BF16)


def _tril_mask(n):
    r = lax.broadcasted_iota(jnp.int32, (n, n), 0)
    c = lax.broadcasted_iota(jnp.int32, (n, n), 1)
    return c <= r


def _chunk_scores(q, k, b, bex, r0, mask):
    parts, qs_l, ks_l, ek_l, eq_l = [], [], [], [], []
    for i in range(CHUNK // SUB):
        ri = slice(r0 + SUB * i, r0 + SUB * (i + 1))
        seen = slice(r0, r0 + SUB * (i + 1))
        base = bex[r0 + SUB * i:r0 + SUB * i + 1]
        eq = jnp.exp(b[ri] - base)
        ek = jnp.exp(jnp.minimum(base - b[seen], EXP_CLAMP))
        ks = k[seen] * ek
        if i + 1 < CHUNK // SUB:
            rest = jnp.zeros((CHUNK - SUB * (i + 1), 128), F32)
            ek, ks = jnp.concatenate([ek, rest], axis=0), jnp.concatenate([ks, rest], axis=0)
        qs = q[ri] * eq
        parts.append(_dot_nt(qs.astype(BF16), ks.astype(BF16)))
        qs_l.append(qs)
        ks_l.append(ks)
        ek_l.append(ek)
        eq_l.append(eq)
    return jnp.where(mask, jnp.concatenate(parts, axis=0), 0.0), qs_l, ks_l, ek_l, eq_l


def _hgrn_cols(hq, hf, hi, lb):
    sg = jax.nn.sigmoid(hf)
    f = lb + (1.0 - lb) * sg
    g = jnp.log(f)
    b = _dot_ones(_block_tri(HBLK, CHUNK), g)
    return _silu(hq), 1.0 - f, g, hi, sg, f, b


GATHER_IDS = (1, 2, 3)


def _hgrn_fwd(z, lbv, hnw):
    ntb, nch = S // HBLK, HBLK // CHUNK

    def body(hq_ref, hf_ref, hi_ref, hg_ref, lb_ref, hnw_ref, o_ref, oa_ref, st_ref, state):
        @pl.when(pl.program_id(0) == 0)
        def _():
            state[...] = jnp.zeros_like(state)

        q_a, k_a, g_a, v_a, _, _, b_a = _hgrn_cols(hq_ref[...], hf_ref[...], hi_ref[...], lb_ref[...])
        bex_a = b_a - g_a
        eb_a = jnp.exp(b_a)
        mask = _tril_mask(CHUNK)
        hg = hg_ref[...]
        w = hnw_ref[...]
        for h in range(HEADS):
            cols = slice(128 * h, 128 * h + 128)
            q, k, v, b, bex, eb = q_a[:, cols], k_a[:, cols], v_a[:, cols], b_a[:, cols], bex_a[:, cols], eb_a[:, cols]
            st = state[h]
            outs = []
            for c in range(nch):
                r0 = c * CHUNK
                rows = slice(r0, r0 + CHUNK)
                a = _chunk_scores(q, k, b, bex, r0, mask)[0]
                vb = v[rows].astype(BF16)
                b_last = b[r0 + CHUNK - 1:r0 + CHUNK]
                qe = (q[rows] * eb[rows]).astype(BF16)
                outs.append(_dot(a.astype(BF16), vb) + _dot_nt(qe, st.astype(BF16)))
                st_ref[h, c] = st
                ke = (k[rows] * jnp.exp(b_last - b[rows])).astype(BF16)
                st = st * jnp.exp(b_last) + _dot_tn(vb, ke)
            state[h] = st
            o = jnp.concatenate(outs, axis=0)
            o_ref[:, cols] = o
            r = lax.rsqrt(jnp.mean(o * o, axis=-1, keepdims=True) + EPS)
            oa_ref[:, cols] = (o * r * w * _silu(hg[:, cols])).astype(BF16)

    def zcol(j):
        return pl.BlockSpec((HBLK, D), lambda t: (t, j))

    out_blk = pl.BlockSpec((HBLK, D), lambda t: (t, 0))
    return pl.pallas_call(
        body, name="hgrn_fwd", grid=(ntb,),
        out_shape=[jax.ShapeDtypeStruct((S, D), F32), jax.ShapeDtypeStruct((S, D), BF16),
                   jax.ShapeDtypeStruct((HEADS, S // CHUNK, 128, 128), F32)],
        in_specs=[zcol(0), zcol(1), zcol(2), zcol(3),
                  pl.BlockSpec((1, D), lambda t: (0, 0)), pl.BlockSpec((1, 128), lambda t: (0, 0))],
        out_specs=[out_blk, out_blk, pl.BlockSpec((HEADS, nch, 128, 128), lambda t: (0, t, 0, 0))],
        scratch_shapes=[pltpu.VMEM((HEADS, 128, 128), F32)],
        compiler_params=_cp(("arbitrary",)),
    )(z, z, z, z, lbv, hnw)


def _half_mask():
    lane = lax.broadcasted_iota(jnp.int32, (1, LANES), 1)
    return (lane % 64) < 32


def _rope(t, cc, ss, first_half):
    partner = jnp.where(first_half, pltpu.roll(t, 96, 1), pltpu.roll(t, 32, 1))
    return t * cc + partner * ss


def _attn_masks():
    i = lax.broadcasted_iota(jnp.int32, (128, 128), 0)
    j = lax.broadcasted_iota(jnp.int32, (128, 128), 1)
    return j >= i, j <= i


def _to_residues_dyn(g, dst, src, row0=0, dtype=None):
    for gi, dil in enumerate((1, 4, 16)):
        m = S // dil

        @pl.when(g == gi)
        def _(dil=dil, m=m):
            for r in range(dil):
                v = src[...] if dil == 1 else src[pl.ds(r, m, stride=dil), :]
                if dtype is not None:
                    v = v.astype(dtype)
                dst[row0 + r * m:row0 + (r + 1) * m, 0:LANES] = v


def _from_residues_dyn(g, dst, src, row0=0):
    for gi, dil in enumerate((1, 4, 16)):
        m = S // dil

        @pl.when(g == gi)
        def _(dil=dil, m=m):
            for r in range(dil):
                v = src[row0 + r * m:row0 + (r + 1) * m, :]
                if dil == 1:
                    dst[...] = v
                else:
                    dst[pl.ds(r, m, stride=dil), :] = v


def _group_blocks(g):
    return jnp.where(g == 0, 16, jnp.where(g == 1, 4, 1))


def _attn_in_specs(extra):
    def zcol(off):
        return pl.BlockSpec((S, LANES), lambda p, g: (0, off + 4 * g + p))

    per_pair = pl.BlockSpec((S, LANES), lambda p, g: (0, p))
    const = pl.BlockSpec((S, LANES), lambda p, g: (0, 0))
    return [zcol(32), zcol(44), zcol(56), pl.BlockSpec((S, LANES), lambda p, g: (0, 68 + p)), const, const] + [per_pair] * extra


def _attn_fwd(z, cc, ss, shards):
    def body(q_ref, k_ref, v_ref, ag_ref, cc_ref, ss_ref, s0, s1, s2,
             ob_ref, lse_ref, obg_ref, qsb_ref, ksb_ref, vsb_ref, f0, f1, f2,
             tmp, qs, ks, vx, og, mg, lg, o_t, m_t, l_t, o_acc, m_acc, l_acc, send_sems, recv_sems, local_sems):
        g = pl.program_id(1)
        step = pl.program_id(0) * 3 + g
        start, middle, end = _allgather_steps(GATHER_IDS, (s0, s1, s2), (f0, f1, f2), send_sems, recv_sems, local_sems)
        pl.when(step == 0)(start)
        pl.when(step == 10)(middle)
        first_half = _half_mask()
        prev_ok, cur_ok = _attn_masks()
        lane = lax.broadcasted_iota(jnp.int32, (1, LANES), 1)
        heads = (lane < 64, lane >= 64)
        nblk = _group_blocks(g)

        @pl.when(g == 0)
        def _():
            ks[0:ATT_PAD, :] = jnp.zeros((ATT_PAD, LANES), BF16)
            vx[0:ATT_PAD, 0:LANES] = jnp.zeros((ATT_PAD, LANES), BF16)
            vx[:, LANES:2 * LANES] = jnp.ones((ATT_PAD + S, LANES), BF16)

        tmp[...] = _rope(q_ref[...], cc_ref[...], ss_ref[...], first_half) * ATT_SCALE
        _to_residues_dyn(g, qs, tmp)
        tmp[...] = _rope(k_ref[...], cc_ref[...], ss_ref[...], first_half)
        _to_residues_dyn(g, ks, tmp, ATT_PAD, BF16)
        _to_residues_dyn(g, vx, v_ref, ATT_PAD, BF16)

        def unit(u, carry):
            start = pl.multiple_of(u * 128, 128)
            cur = pl.ds(start, 128)
            pm = prev_ok & ((u & (nblk - 1)) != 0)
            qu = qs[cur, :]
            kcat = ks[pl.ds(start, 256), :]
            vext = vx[pl.ds(start, 256), :]
            o_u = m_u = l_u = None
            for hh in range(2):
                s = _dot_nt(jnp.where(heads[hh], qu, 0.0).astype(BF16), kcat)
                sp = jnp.where(pm, s[:, 0:128], -jnp.inf)
                sc = jnp.where(cur_ok, s[:, 128:256], -jnp.inf)
                m = jnp.max(jnp.maximum(sp, sc), axis=-1, keepdims=True)
                p = jnp.concatenate([jnp.exp(sp - m), jnp.exp(sc - m)], axis=1).astype(BF16)
                ol = _dot(p, vext)
                mb = jnp.broadcast_to(m, (128, LANES))
                if hh == 0:
                    o_u, l_u, m_u = ol[:, 0:128], ol[:, 128:256], mb
                else:
                    o_u = jnp.where(heads[1], ol[:, 0:128], o_u)
                    l_u = jnp.where(heads[1], ol[:, 128:256], l_u)
                    m_u = jnp.where(heads[1], mb, m_u)
            og[cur, :] = o_u
            mg[cur, :] = m_u
            lg[cur, :] = l_u
            return carry

        lax.fori_loop(0, 16, unit, 0, unroll=16)
        qsb_ref[0] = qs[...].astype(BF16)
        ksb_ref[0] = ks[...]
        vsb_ref[0] = vx[:, 0:LANES]
        _from_residues_dyn(g, o_t, og)
        _from_residues_dyn(g, m_t, mg)
        _from_residues_dyn(g, l_t, lg)

        @pl.when(g == 0)
        def _():
            o_acc[...] = o_t[...]
            m_acc[...] = m_t[...]
            l_acc[...] = l_t[...]

        @pl.when(g > 0)
        def _():
            m_new = jnp.maximum(m_acc[...], m_t[...])
            wa, wb = jnp.exp(m_acc[...] - m_new), jnp.exp(m_t[...] - m_new)
            o_acc[...] = o_acc[...] * wa + o_t[...] * wb
            l_acc[...] = l_acc[...] * wa + l_t[...] * wb
            m_acc[...] = m_new

        @pl.when(g == 2)
        def _():
            ob = o_acc[...] / l_acc[...]
            ob_ref[...] = ob
            lse_ref[...] = m_acc[...] + jnp.log(l_acc[...])
            obg_ref[...] = (ob * _silu(ag_ref[...])).astype(BF16)

        pl.when(step == 11)(end)

    n = len(GATHER_IDS)
    any_spec = pl.BlockSpec(memory_space=pl.ANY)
    blk = pl.BlockSpec((S, LANES), lambda p, g: (0, p))
    buf = pltpu.VMEM((S, LANES), F32)
    return pl.pallas_call(
        body, name="attn_fwd", grid=(4, 3),
        out_shape=[jax.ShapeDtypeStruct((S, 512), F32), jax.ShapeDtypeStruct((S, 512), F32),
                   jax.ShapeDtypeStruct((S, 512), BF16), jax.ShapeDtypeStruct((3, S, 512), BF16),
                   jax.ShapeDtypeStruct((3, ATT_PAD + S, 512), BF16), jax.ShapeDtypeStruct((3, ATT_PAD + S, 512), BF16)]
        + [jax.ShapeDtypeStruct(FULL_SHAPES[a], BF16) for a in GATHER_IDS],
        in_specs=_attn_in_specs(0) + [any_spec] * n,
        out_specs=[blk, blk, blk, pl.BlockSpec((1, S, LANES), lambda p, g: (g, 0, p)),
                   pl.BlockSpec((1, ATT_PAD + S, LANES), lambda p, g: (g, 0, p)),
                   pl.BlockSpec((1, ATT_PAD + S, LANES), lambda p, g: (g, 0, p))] + [any_spec] * n,
        scratch_shapes=[buf, buf, pltpu.VMEM((ATT_PAD + S, LANES), BF16), pltpu.VMEM((ATT_PAD + S, 2 * LANES), BF16)]
        + [buf] * 9 + [pltpu.SemaphoreType.DMA((7 * n,)), pltpu.SemaphoreType.DMA((7 * n,)), pltpu.SemaphoreType.DMA((n,))],
        compiler_params=_cp(("arbitrary", "arbitrary")),
    )(z, z, z, z, cc, ss, *shards)


def _tail(x, o_a, o_bg, z, target, w_a, w_b, w_out, fnw):
    tm = 256

    def body(x_ref, oa_ref, ob_ref, gpa_ref, gpb_ref, t_ref, wa_ref, wb_ref, wo_ref, fnw_ref,
             dx2_ref, dx2b_ref, dz_hbm, doa_ref, dob_ref, mg_ref, dya_ref, dyb_ref, small_ref, dgp, dgp_sem):
        step = pl.program_id(0)
        slot = step % 2

        def dgp_copy(at_step, at_slot):
            return pltpu.make_async_copy(
                dgp.at[at_slot], dz_hbm.at[pl.ds(pl.multiple_of(at_step * tm, tm), tm), pl.ds(DZ_GATES, 2 * D)],
                dgp_sem.at[at_slot])

        @pl.when(step == 0)
        def _():
            small_ref[...] = jnp.zeros_like(small_ref)

        @pl.when(step >= 2)
        def _():
            dgp_copy(step - 2, slot).wait()

        wa, wb, wo = wa_ref[...], wb_ref[...], wo_ref[...]
        y_a = _dot(oa_ref[...], wa)
        y_b = _dot(ob_ref[...], wb)
        ga = jax.nn.sigmoid(gpa_ref[...])
        gb = jax.nn.sigmoid(gpb_ref[...])
        merged = (ga * y_a + gb * y_b).astype(BF16)
        x2 = x_ref[...] + _dot(merged, wo)
        r2 = lax.rsqrt(jnp.mean(x2 * x2, axis=-1, keepdims=True) + EPS)
        n2 = x2 * r2
        fw = fnw_ref[...]
        err = n2 * fw - t_ref[...]
        loss = 0.5 * jnp.sum(jnp.sum(err * err, axis=-1, keepdims=True), axis=0, keepdims=True) / D
        dy = err * (1.0 / D)
        g_fnw = jnp.sum(dy * n2, axis=0, keepdims=True)
        dn = dy * fw
        dx2 = r2 * (dn - n2 * jnp.mean(dn * n2, axis=-1, keepdims=True))
        dx2b = dx2.astype(BF16)
        dmerged = _dot_nt(dx2b, wo)
        dy_a = (dmerged * ga).astype(BF16)
        dy_b = (dmerged * gb).astype(BF16)
        dx2_ref[...] = dx2
        dx2b_ref[...] = dx2b
        dgp[slot, :, 0:D] = (dmerged * y_a * ga * (1.0 - ga)).astype(BF16)
        dgp[slot, :, D:2 * D] = (dmerged * y_b * gb * (1.0 - gb)).astype(BF16)
        dgp_copy(step, slot).start()
        doa_ref[...] = _dot_nt(dy_a, wa)
        dob_ref[...] = _dot_nt(dy_b, wb)
        mg_ref[...] = merged
        dya_ref[...] = dy_a
        dyb_ref[...] = dy_b
        small_ref[0:1, :] += g_fnw
        small_ref[1:2, :] += jnp.broadcast_to(loss, (1, D))

        @pl.when(step == S // tm - 1)
        def _():
            dgp_copy(step - 1, 1 - slot).wait()
            dgp_copy(step, slot).wait()

    def rows(cols, off=0):
        return pl.BlockSpec((tm, cols), lambda i: (i, off))

    def whole(shape):
        return pl.BlockSpec(shape, lambda i: (0, 0))

    return pl.pallas_call(
        body, name="tail", grid=(S // tm,),
        out_shape=[jax.ShapeDtypeStruct((S, D), F32), jax.ShapeDtypeStruct((S, D), BF16),
                   jax.ShapeDtypeStruct((S, IN_COLS), BF16), jax.ShapeDtypeStruct((S, D), F32),
                   jax.ShapeDtypeStruct((S, 512), F32), jax.ShapeDtypeStruct((S, D), BF16),
                   jax.ShapeDtypeStruct((S, D), BF16), jax.ShapeDtypeStruct((S, D), BF16),
                   jax.ShapeDtypeStruct((8, D), F32)],
        in_specs=[rows(D), rows(D), rows(512), rows(D, 9), rows(D, 10), rows(D),
                  whole((D, D)), whole((512, D)), whole((D, D)), whole((1, D))],
        out_specs=[rows(D), rows(D), pl.BlockSpec(memory_space=pl.ANY), rows(D), rows(512), rows(D), rows(D),
                   rows(D), whole((8, D))],
        scratch_shapes=[pltpu.VMEM((2, tm, 2 * D), BF16), pltpu.SemaphoreType.DMA((2,))],
        compiler_params=_cp(("arbitrary",)),
    )(x, o_a, o_bg, z, z, target, w_a, w_b, w_out, fnw)


def _tn_matmul(a, b, name):
    m, n = a.shape[1], b.shape[1]
    tn = 512

    def body(a_ref, b_ref, o_ref, ob_ref):
        acc = _dot_tn(a_ref[...], b_ref[...])
        o_ref[...] = acc
        ob_ref[...] = acc.astype(BF16)

    out_blk = pl.BlockSpec((m, tn), lambda j: (0, j))
    return pl.pallas_call(
        body, name=name, grid=(n // tn,),
        out_shape=[jax.ShapeDtypeStruct((m, n), F32), jax.ShapeDtypeStruct((m, n), BF16)],
        in_specs=[pl.BlockSpec((S, m), lambda j: (0, 0)), pl.BlockSpec((S, tn), lambda j: (0, j))],
        out_specs=[out_blk, out_blk],
        compiler_params=_cp(("parallel",)),
    )(a, b)


def _hgrn_bwd(z, o, do_a, states, lbv, hnw, partials, dz):
    ntb, nch = S // HBLK, HBLK // CHUNK
    n = len(GATHER_IDS)

    def body(hq_ref, hf_ref, hi_ref, hg_ref, o_ref, doa_ref, st_ref, lb_ref, hnw_ref, p0, p1, p2, dz_in,
             dz_ref, glb_ref, ghn_ref, e0, e1, e2, dstate, send_sems, recv_sems):
        dhq_ref, dhf_ref, dhi_ref, dhg_ref = (dz_ref.at[:, pl.ds(j * D, D)] for j in range(4))
        start, end = _exchange_chips_steps((p0, p1, p2), (e0, e1, e2), send_sems, recv_sems)

        @pl.when(pl.program_id(0) == 0)
        def _():
            dstate[...] = jnp.zeros_like(dstate)
            glb_ref[...] = jnp.zeros_like(glb_ref)
            ghn_ref[...] = jnp.zeros_like(ghn_ref)
            start()

        lb_a = lb_ref[...]
        hq_a, hg_a = hq_ref[...], hg_ref[...]
        q_a, k_a, g_a, v_a, sg_a, f_a, b_a = _hgrn_cols(hq_a, hf_ref[...], hi_ref[...], lb_a)
        bex_a = b_a - g_a
        eb_a = jnp.exp(b_a)
        w = hnw_ref[...]
        mask = _tril_mask(CHUNK)
        upper = _block_tri(CHUNK, CHUNK, upper=True)
        for h in range(HEADS):
            cols = slice(128 * h, 128 * h + 128)
            q, k, v, b, bex, eb = q_a[:, cols], k_a[:, cols], v_a[:, cols], b_a[:, cols], bex_a[:, cols], eb_a[:, cols]
            hq, hg, sg, f, lb = hq_a[:, cols], hg_a[:, cols], sg_a[:, cols], f_a[:, cols], lb_a[:, cols]
            ov, doa = o_ref[:, cols], doa_ref[:, cols]
            r = lax.rsqrt(jnp.mean(ov * ov, axis=-1, keepdims=True) + EPS)
            n = ov * r
            sil = _silu(hg)
            dhg_ref[:, cols] = (doa * n * w * _dsilu(hg)).astype(BF16)
            ghn_ref[h] += jnp.sum(doa * sil * n, axis=0, keepdims=True)
            dn = doa * sil * w
            do = r * (dn - n * jnp.mean(dn * n, axis=-1, keepdims=True))

            dst = dstate[h]
            dq_l, dk_l, dv_l, dg_l = [None] * nch, [None] * nch, [None] * nch, [None] * nch
            for c in reversed(range(nch)):
                r0 = c * CHUNK
                rows = slice(r0, r0 + CHUNK)
                st = st_ref[h, c]
                bc, kc, qc = b[rows], k[rows], q[rows]
                vb, dob = v[rows].astype(BF16), do[rows].astype(BF16)
                b_last = bc[CHUNK - 1:CHUNK]
                e_last = jnp.exp(b_last)
                ekl = jnp.exp(b_last - bc)
                dstb = dst.astype(BF16)
                a, qs_l, ks_l, ek_l, eq_l = _chunk_scores(q, k, b, bex, r0, mask)
                da = jnp.where(mask, _dot_nt(dob, vb), 0.0)
                dv_l[c] = _dot_tn(a.astype(BF16), dob) + _dot_nt((kc * ekl).astype(BF16), dstb)
                dq_inter = _dot(dob, st.astype(BF16)) * eb[rows]
                dk_state = _dot(vb, dstb) * ekl
                dq_parts, dk_intra = [], jnp.zeros((CHUNK, 128), F32)
                dab = da.astype(BF16)
                for i in range(CHUNK // SUB):
                    da_i = dab[SUB * i:SUB * (i + 1)]
                    ks_hi, ks_lo = _split2(ks_l[i])
                    qs_hi, qs_lo = _split2(qs_l[i])
                    dq_parts.append((_dot(da_i, ks_hi) + _dot(da_i, ks_lo)) * eq_l[i])
                    dk_intra = dk_intra + (_dot_tn(da_i, qs_hi) + _dot_tn(da_i, qs_lo)) * ek_l[i]
                dq = jnp.concatenate(dq_parts, axis=0) + dq_inter
                dk = dk_intra + dk_state
                last = (e_last * jnp.sum(st * dst, axis=0, keepdims=True)
                        + jnp.sum(kc * dk_state, axis=0, keepdims=True))
                dg_l[c] = _dot_ones(upper, qc * dq - kc * dk) + last
                dq_l[c], dk_l[c] = dq, dk
                dst = dst * e_last + _dot_tn(dob, (qc * eb[rows]).astype(BF16))
            dstate[h] = dst
            dq, dk = jnp.concatenate(dq_l, axis=0), jnp.concatenate(dk_l, axis=0)
            dg, dv = jnp.concatenate(dg_l, axis=0), jnp.concatenate(dv_l, axis=0)
            dhq_ref[:, cols] = (dq * _dsilu(hq)).astype(BF16)
            dhi_ref[:, cols] = dv.astype(BF16)
            df = dg / f - dk
            dhf_ref[:, cols] = (df * (1.0 - lb) * sg * (1.0 - sg)).astype(BF16)
            glb_ref[:, cols] += jnp.sum(df * (1.0 - sg), axis=0, keepdims=True)

        pl.when(pl.program_id(0) == ntb - 1)(end)

    def rev(t):
        return ntb - 1 - t

    def zcol(j):
        return pl.BlockSpec((HBLK, D), lambda t: (rev(t), j))

    blk = pl.BlockSpec((HBLK, D), lambda t: (rev(t), 0))
    any_spec = pl.BlockSpec(memory_space=pl.ANY)
    return pl.pallas_call(
        body, name="hgrn_bwd", grid=(ntb,),
        out_shape=[jax.ShapeDtypeStruct((S, IN_COLS), BF16)]
        + [jax.ShapeDtypeStruct((1, D), F32), jax.ShapeDtypeStruct((HEADS, 1, 128), F32)]
        + [jax.ShapeDtypeStruct((3,) + SHARD_SHAPES[a], BF16) for a in GATHER_IDS],
        in_specs=[zcol(0), zcol(1), zcol(2), zcol(3), blk, blk,
                  pl.BlockSpec((HEADS, nch, 128, 128), lambda t: (0, rev(t), 0, 0)),
                  pl.BlockSpec((1, D), lambda t: (0, 0)), pl.BlockSpec((1, 128), lambda t: (0, 0))]
        + [any_spec] * (n + 1),
        out_specs=[pl.BlockSpec((HBLK, DZ_ATT), lambda t: (rev(t), 0)), pl.BlockSpec((1, D), lambda t: (0, 0)),
                   pl.BlockSpec((HEADS, 1, 128), lambda t: (0, 0, 0))] + [any_spec] * n,
        scratch_shapes=[pltpu.VMEM((HEADS, 128, 128), F32), pltpu.SemaphoreType.DMA((3 * n,)),
                        pltpu.SemaphoreType.DMA((3 * n,))],
        input_output_aliases={9 + n: 0},
        compiler_params=_cp(("arbitrary",)),
    )(z, z, z, z, o, do_a, states, lbv, hnw, *partials, dz)


def _attn_bwd(z, qsb, ksb, vsb, cc, ss, ob, lse, do_bg, dz):
    def body(qs, ks, vs, ag_ref, cc_ref, ss_ref, ob_ref, lse_ref, dobg_ref, dz_in, dz_hbm,
             tmp, dos, dqs, dks, dvs, dkp, dvp, do_t, ls0_t, ls1_t, dl0_t, dl1_t, ls0, ls1, dl0, dl1,
             stage, stage_sem):
        pair, g = pl.program_id(0), pl.program_id(1)

        def out_copy(j):
            tile = DZ_ATT // LANES + (36 + pair if j == 3 else 12 * j + 4 * g + pair)
            return pltpu.make_async_copy(
                stage.at[j], dz_hbm.at[:, pl.ds(pl.multiple_of(tile * LANES, LANES), LANES)], stage_sem.at[j])

        def restage(j, value):
            pl.when(pair * 3 + g > 0)(lambda: out_copy(j).wait())
            stage[j] = value
            out_copy(j).start()

        pl.when(g == 2)(lambda: out_copy(3).wait())
        first_half = _half_mask()
        prev_ok, cur_ok = _attn_masks()
        lane = lax.broadcasted_iota(jnp.int32, (1, LANES), 1)
        heads = (lane < 64, lane >= 64)
        nblk = _group_blocks(g)
        cc_v, ss_v = cc_ref[...], ss_ref[...]

        @pl.when(g == 0)
        def _():
            ag, obv, dobg = ag_ref[...], ob_ref[...], dobg_ref[...]
            stage[3] = (dobg * obv * _dsilu(ag)).astype(BF16)
            out_copy(3).start()
            dob = dobg * _silu(ag)
            do_t[...] = dob
            prod = dob * obv
            dl = jnp.concatenate(
                [jnp.broadcast_to(jnp.sum(prod[:, 0:64], axis=-1, keepdims=True), (S, 64)),
                 jnp.broadcast_to(jnp.sum(prod[:, 64:128], axis=-1, keepdims=True), (S, 64))], axis=1)
            dl0_t[...] = dl

        _to_residues_dyn(g, dos, do_t)
        _to_residues_dyn(g, ls0, lse_ref)
        _to_residues_dyn(g, dl0, dl0_t)

        def unit(u, carry):
            start = pl.multiple_of(u * 128, 128)
            cur = pl.ds(start, 128)
            both = pl.ds(start, 256)
            pm = prev_ok & ((u & (nblk - 1)) != 0)
            qu, dou = qs[0, cur, :], dos[cur, :]
            kcat, vcat = ks[0, both, :], vs[0, both, :]
            dq_u = None
            q_l, do_l, ds_l, p_l = [], [], [], []
            ls_u, dl_u = ls0[cur, :], dl0[cur, :]
            ls_sw, dl_sw = pltpu.roll(ls_u, 64, 1), pltpu.roll(dl_u, 64, 1)
            for hh in range(2):
                q_h = jnp.where(heads[hh], qu, jnp.zeros((), BF16))
                do_h = jnp.where(heads[hh], dou, 0.0).astype(BF16)
                s = _dot_nt(q_h, kcat)
                dp = _dot_nt(do_h, vcat)
                lse_h = jnp.where(heads[hh], ls_u, ls_sw)
                dl_h = jnp.where(heads[hh], dl_u, dl_sw)
                pp = jnp.where(pm, jnp.exp(s[:, 0:128] - lse_h), 0.0)
                pc = jnp.where(cur_ok, jnp.exp(s[:, 128:256] - lse_h), 0.0)
                ds = jnp.concatenate([pp * (dp[:, 0:128] - dl_h), pc * (dp[:, 128:256] - dl_h)], axis=1).astype(BF16)
                dq = _dot(ds, kcat)
                dq_u = dq if hh == 0 else jnp.where(heads[1], dq, dq_u)
                q_l.append(q_h)
                do_l.append(do_h)
                ds_l.append(ds)
                p_l.append(jnp.concatenate([pp, pc], axis=1).astype(BF16))
            dkcat = _dot_tn(jnp.concatenate(ds_l, axis=0), jnp.concatenate(q_l, axis=0))
            dvcat = _dot_tn(jnp.concatenate(p_l, axis=0), jnp.concatenate(do_l, axis=0))
            dkp[cur, :] = dkcat[0:128]
            dks[cur, :] = dkcat[128:256]
            dvp[cur, :] = dvcat[0:128]
            dvs[cur, :] = dvcat[128:256]
            dqs[cur, :] = dq_u
            return carry

        lax.fori_loop(0, 16, unit, 0, unroll=ATT_UNROLL)
        dks[0:S - 128, :] += dkp[128:S, :]
        dvs[0:S - 128, :] += dvp[128:S, :]
        _from_residues_dyn(g, tmp, dqs)
        restage(0, (_rope(tmp[...], cc_v, -ss_v, first_half) * ATT_SCALE).astype(BF16))
        _from_residues_dyn(g, tmp, dks)
        restage(1, _rope(tmp[...], cc_v, -ss_v, first_half).astype(BF16))
        _from_residues_dyn(g, tmp, dvs)
        restage(2, tmp[...].astype(BF16))

        @pl.when(pair * 3 + g == 11)
        def _():
            for j in range(3):
                out_copy(j).wait()

    any_spec = pl.BlockSpec(memory_space=pl.ANY)
    buf = pltpu.VMEM((S, LANES), F32)
    padded_b = pltpu.VMEM((ATT_PAD + S, LANES), BF16)
    return pl.pallas_call(
        body, name="attn_bwd", grid=(4, 3),
        out_shape=jax.ShapeDtypeStruct((S, IN_COLS), BF16),
        in_specs=[pl.BlockSpec((1, S, LANES), lambda p, g: (g, 0, p)),
                  pl.BlockSpec((1, ATT_PAD + S, LANES), lambda p, g: (g, 0, p)),
                  pl.BlockSpec((1, ATT_PAD + S, LANES), lambda p, g: (g, 0, p))] + _attn_in_specs(3)[3:] + [any_spec],
        out_specs=any_spec,
        scratch_shapes=[buf] * 16 + [pltpu.VMEM((4, S, LANES), BF16), pltpu.SemaphoreType.DMA((4,))],
        input_output_aliases={9: 0},
        compiler_params=_cp(("arbitrary", "arbitrary")),
    )(qsb, ksb, vsb, z, cc, ss, ob, lse, do_bg, dz)


def _in_proj_bwd(dz, h, w_in):
    half = S // 2
    slab = (D, SHARD_COLS)

    def body(dz_hbm, h_hbm, w_hbm, dh_hbm, g_chip, r1_hbm, relay_hbm, r2_hbm,
             h_buf, dz_buf, stage_d, r1_buf, stage_i, acc,
             dz_sem, w_sem, h_sem, r1_sem, out_sem, send_d, recv_d, send_i, recv_i):
        x, y, c = _mesh_pos()
        sibling = (x, y, 1 - c)
        north = c == 1
        near = (jnp.where(north, 1 - x, x), jnp.where(north, y, 1 - y))
        far = (jnp.where(north, x, 1 - x), jnp.where(north, 1 - y, y))
        chips = [(1 - x, 1 - y), near, far, (x, y)]

        def cols(d):
            return pl.ds(pl.multiple_of(d * SHARD_COLS, LANES), SHARD_COLS)

        blocks = []
        for q_sib, q in zip([chips[0], far, near, chips[3]], chips):
            blocks += [4 * q_sib[0] + 2 * q_sib[1] + (1 - c), 4 * q[0] + 2 * q[1] + c]

        def dz_tile(t):
            return _SplitCopy(dz_hbm.at[pl.ds((t % 2) * half, half), cols(blocks[t // 2])],
                                         dz_buf.at[t % 2], dz_sem.at[t % 2])

        def to_sibling(i):
            return pltpu.make_async_remote_copy(
                src_ref=stage_d.at[i % 2], dst_ref=r1_hbm.at[i], send_sem=send_d.at[i], recv_sem=recv_d.at[i],
                device_id=sibling, device_id_type=MESH)

        def to_owner(i):
            dst = relay_hbm if i == 0 else r2_hbm.at[i - 1]
            return pltpu.make_async_remote_copy(
                src_ref=stage_i.at[i], dst_ref=dst, send_sem=send_i.at[i], recv_sem=recv_i.at[i],
                device_id=(*(far if i == 2 else near), c), device_id_type=MESH)

        h_copy = _SplitCopy(h_hbm, h_buf, h_sem)
        h_copy.start()
        dz_tile(0).start()
        h_copy.wait()
        for b in range(8):
            i = b // 2
            g = None
            for r in range(2):
                t = 2 * b + r
                if t + 1 < 16:
                    dz_tile(t + 1).start()
                dz_tile(t).wait()
                part = _dot_tn(h_buf[r * half:(r + 1) * half, :], dz_buf[t % 2])
                g = part if g is None else g + part
                if b % 2 == 1 and r == 0:
                    to_sibling(i).wait_recv()
                    r1_copy = _SplitCopy(r1_hbm.at[i], r1_buf, r1_sem)
                    r1_copy.start()
            if b % 2 == 0:
                if i >= 2:
                    to_sibling(i - 2).wait_send()
                stage_d[i % 2] = g.astype(BF16)
                to_sibling(i).start()
            else:
                r1_copy.wait()
                g = g + r1_buf[...].astype(F32)
                if i == 2:
                    to_owner(0).wait_recv()
                    relay_copy = _SplitCopy(relay_hbm, r1_buf, r1_sem)
                    relay_copy.start()
                    relay_copy.wait()
                    g = g + r1_buf[...].astype(F32)
                if i < 3:
                    stage_i[i] = g.astype(BF16)
                    to_owner(i).start()
                else:
                    g_chip[...] = g
        to_sibling(2).wait_send()
        to_sibling(3).wait_send()

        def dz2(t):
            return _SplitCopy(
                dz_hbm.at[pl.ds((t % 2) * half, half), pl.ds((t // 2) * SHARD_COLS, SHARD_COLS)],
                dz_buf.at[t % 2], dz_sem.at[t % 2])

        def w2(b):
            return _SplitCopy(w_hbm.at[:, pl.ds(b * SHARD_COLS, SHARD_COLS)],
                                         stage_d.at[b % 2], w_sem.at[b % 2])

        dz2(0).start()
        w2(0).start()
        for t in range(16):
            b, r = t // 2, t % 2
            if t + 1 < 16:
                dz2(t + 1).start()
            if r == 0:
                if b + 1 < 8:
                    w2(b + 1).start()
                w2(b).wait()
            dz2(t).wait()
            part = _dot_nt(dz_buf[t % 2], stage_d[b % 2])
            if b == 0:
                acc[r] = part
            else:
                acc[r] += part
        dh_out = [_SplitCopy(acc.at[r], dh_hbm.at[pl.ds(r * half, half), :], out_sem.at[r])
                  for r in range(2)]
        for cp in dh_out:
            cp.start()
        for cp in dh_out:
            cp.wait()
        for i in range(3):
            to_owner(i).wait_send()
        for i in (1, 2):
            to_owner(i).wait_recv()

    any_spec = pl.BlockSpec(memory_space=pl.ANY)
    return pl.pallas_call(
        body, name="in_proj_bwd",
        out_shape=[jax.ShapeDtypeStruct((S, D), F32), jax.ShapeDtypeStruct(slab, F32),
                   jax.ShapeDtypeStruct((4,) + slab, BF16), jax.ShapeDtypeStruct(slab, BF16),
                   jax.ShapeDtypeStruct((2,) + slab, BF16)],
        in_specs=[any_spec] * 3,
        out_specs=[any_spec, pl.BlockSpec(memory_space=pltpu.VMEM), any_spec, any_spec, any_spec],
        scratch_shapes=[pltpu.VMEM((S, D), BF16), pltpu.VMEM((2, half, SHARD_COLS), BF16),
                        pltpu.VMEM((2,) + slab, BF16), pltpu.VMEM(slab, BF16), pltpu.VMEM((3,) + slab, BF16),
                        pltpu.VMEM((2, half, D), F32),
                        pltpu.SemaphoreType.DMA((2,)), pltpu.SemaphoreType.DMA((2,)), pltpu.SemaphoreType.DMA,
                        pltpu.SemaphoreType.DMA, pltpu.SemaphoreType.DMA((2,)),
                        pltpu.SemaphoreType.DMA((4,)), pltpu.SemaphoreType.DMA((4,)),
                        pltpu.SemaphoreType.DMA((3,)), pltpu.SemaphoreType.DMA((3,))],
        compiler_params=_cp(),
    )(dz, h, w_in)


def _grad_x(x, norm_w, dh, dx2):
    tr = 256

    def body(x_ref, w_ref, dh_ref, dx2_ref, gx_ref, gnw_ref):
        @pl.when(pl.program_id(0) == 0)
        def _():
            gnw_ref[...] = jnp.zeros_like(gnw_ref)

        xv, dhv = x_ref[...], dh_ref[...]
        r = lax.rsqrt(jnp.mean(xv * xv, axis=-1, keepdims=True) + EPS)
        n = xv * r
        gnw_ref[...] += jnp.sum(dhv * n, axis=0, keepdims=True)
        dn = dhv * w_ref[...]
        gx_ref[...] = dx2_ref[...] + r * (dn - n * jnp.mean(dn * n, axis=-1, keepdims=True))

    row = pl.BlockSpec((tr, D), lambda i: (i, 0))
    vec = pl.BlockSpec((1, D), lambda i: (0, 0))
    return pl.pallas_call(
        body, name="grad_x", grid=(S // tr,),
        out_shape=[jax.ShapeDtypeStruct((S, D), F32), jax.ShapeDtypeStruct((1, D), F32)],
        in_specs=[row, vec, row, row], out_specs=[row, vec],
        compiler_params=_cp(("arbitrary",)),
    )(x, norm_w, dh, dx2)


def _rope_tables(positions):
    inv_freq = 10000.0 ** (-jnp.arange(0, 64, 2, dtype=F32) / 64)
    ang = positions.astype(F32)[:, None] * inv_freq[None, :]
    cos, sin = jnp.cos(ang), jnp.sin(ang)
    return jnp.tile(cos, (1, 4)), jnp.tile(jnp.concatenate([-sin, sin], axis=1), (1, 2))


def _local_step(x, positions, norm_w, lb_logits, hnw, fnw, target, w_in_shard, small_shards, core):
    cc, ss = _rope_tables(positions)
    lbv = jax.nn.sigmoid(lb_logits[0:1] - lb_logits[1:2])
    z, w_in, h = _in_proj_gather(x, norm_w, w_in_shard)
    o, o_a, states = _hgrn_fwd(z, lbv, hnw)
    ob, lse, o_bg, qsb, ksb, vsb, w_a, w_b, w_out = _attn_fwd(z, cc, ss, small_shards)
    dx2, dx2b, dz, do_a, do_bg, merged, dy_a, dy_b, tail_small = _tail(x, o_a, o_bg, z, target, w_a, w_b, w_out, fnw)
    g_out, gb_out = _tn_matmul(merged, dx2b, "grad_w_out")
    g_a, gb_a = _tn_matmul(o_a, dy_a, "grad_w_a")
    g_b, gb_b = _tn_matmul(o_bg, dy_b, "grad_w_b")
    grads, gb = (g_a, g_b, g_out), (gb_a, gb_b, gb_out)
    r1 = _exchange_sibling(GATHER_IDS, gb)
    pb = [_chip_partials(a, grads[i], r1[i], core) for i, a in enumerate(GATHER_IDS)]
    dz, glb, ghn, *r2 = _hgrn_bwd(z, o, do_a, states, lbv, hnw, pb, dz)
    dz = _attn_bwd(z, qsb, ksb, vsb, cc, ss, ob, lse, do_bg, dz)
    dh, g_chip_in, _, _, r2_in = _in_proj_bwd(dz, h, w_in)
    grad_x, gnw = _grad_x(x, norm_w, dh, dx2)
    ghn_row = jnp.pad(jnp.sum(ghn, axis=0), ((0, 0), (0, D - 128)))
    small = jnp.concatenate([gnw, glb, ghn_row, tail_small[0:2], jnp.zeros((3, D), F32)], axis=0)
    return grad_x, (g_chip_in, r2_in), grads, r1, r2, small


def kernel(x, positions, norm_w, w_in, lb_logits, hgrn_norm_w, w_branch_a, w_branch_b, w_out, final_norm_w, loss_target, m_norm_w, m_w_in, m_lb_logits, m_hgrn_norm_w, m_w_branch_a, m_w_branch_b, m_w_out, m_final_norm_w, v_norm_w, v_w_in, v_lb_logits, v_hgrn_norm_w, v_w_branch_a, v_w_branch_b, v_w_out, v_final_norm_w):
    ix, iy, ic = _mesh_pos()
    core = jnp.reshape(ic, (1,)).astype(jnp.int32)
    pos = jnp.stack([4 * ix + 2 * iy + ic, 2 * ix + iy]).astype(jnp.int32)

    shards = [w_in[0], w_branch_a[0], w_branch_b[0], w_out[0]]
    moments_m = [m_w_in[0], m_w_branch_a[0], m_w_branch_b[0], m_w_out[0]]
    moments_v = [v_w_in[0], v_w_branch_a[0], v_w_branch_b[0], v_w_out[0]]
    names = ("w_in", "w_a", "w_b", "w_out")
    ids = GATHER_IDS
    shards_b = [_cast_bf16(w, f"cast_{nm}") for w, nm in zip(shards, names)]

    fnw2 = final_norm_w.reshape(1, D)
    grad_x, (g_chip_in, r2_in), grads, r1, r2, small = _local_step(
        x[0], positions[0], norm_w, lb_logits, hgrn_norm_w, fnw2, loss_target[0], shards_b[0], shards_b[1:], core)

    gathered = _gather_small(small)
    big =[_reduce_own_and_update(shards[0], moments_m[0], moments_v[0], g_chip_in, r2_in)]
    big += [_reduce_and_update(a, shards[a], moments_m[a], moments_v[a], grads[i], r1[i], r2[i], pos)
            for i, a in enumerate(ids)]
    sm = _small_update(gathered, norm_w, lb_logits, hgrn_norm_w, fnw2,
                       (m_norm_w, m_lb_logits, m_hgrn_norm_w, m_final_norm_w.reshape(1, D),
                        v_norm_w, v_lb_logits, v_hgrn_norm_w, v_final_norm_w.reshape(1, D)))
    loss = sm[0][0, 0]
    outs = [loss, grad_x[None]]
    for kind in range(4):
        s_nw, s_lb, s_hn, s_fn = sm[1 + 4 * kind:5 + 4 * kind]
        outs += [s_nw, big[0][kind][None], s_lb, s_hn, big[1][kind][None], big[2][kind][None],
                 big[3][kind][None], s_fn.reshape(D)]
    return tuple(outs)
```

```python
import functools

import jax
import jax.numpy as jnp
from jax import lax
from jax.experimental import pallas as pl
from jax.experimental.pallas import tpu as pltpu

F32 = jnp.float32
BF16 = jnp.bfloat16
MESH = pl.DeviceIdType.MESH

S = 2048
D = 1024
NDEV = 8
HEADS = 8
CHUNK = 64
SUB = 16
HBLK = 256
ATT_PAD = 128
ATT_UNROLL = 16
COPY_PARTS = 4
EXP_CLAMP = 80.0
EPS = 1e-6
IN_COLS = 11264
SHARD_COLS = IN_COLS // NDEV
DZ_ATT = 4096
DZ_GATES = 9216
ATT_DILS = (1, 4, 16)
ATT_SCALE = 64 ** -0.5
LANES = 128

ADAM_LR, ADAM_B1, ADAM_B2, ADAM_EPS, ADAM_WD, ADAM_STEP = 0.001, 0.9, 0.999, 1e-08, 0.01, 10

VMEM_LIMIT = 56 * 1024 * 1024


def _cp(sem=None, **kw):
    return pltpu.CompilerParams(dimension_semantics=sem, vmem_limit_bytes=VMEM_LIMIT, **kw)


def _dot(a, b):
    return jnp.dot(a, b, preferred_element_type=F32)


def _dot_nt(a, b):
    return lax.dot_general(a, b, (((1,), (1,)), ((), ())), preferred_element_type=F32)


def _dot_tn(a, b):
    return lax.dot_general(a, b, (((0,), (0,)), ((), ())), preferred_element_type=F32)


def _split2(x):
    hi = x.astype(BF16)
    lo = (x - hi.astype(F32)).astype(BF16)
    return hi, lo


def _split3(x):
    hi = x.astype(BF16)
    r = x - hi.astype(F32)
    mid = r.astype(BF16)
    lo = (r - mid.astype(F32)).astype(BF16)
    return hi, mid, lo


def _dot_ones(ones_bf16, x):
    hi, mid, lo = _split3(x)
    return _dot(ones_bf16, hi) + _dot(ones_bf16, mid) + _dot(ones_bf16, lo)


def _silu(x):
    return x * jax.nn.sigmoid(x)


def _dsilu(x):
    s = jax.nn.sigmoid(x)
    return s * (1.0 + x * (1.0 - s))


def _mesh_pos():
    return lax.axis_index("x"), lax.axis_index("y"), lax.axis_index("c")


class _SplitCopy:
    def __init__(self, src, dst, sem):
        self.src, self.dst, self.sem = src, dst, sem

    def start(self):
        rows = self.src.shape[0] // COPY_PARTS
        for p in range(COPY_PARTS):
            chunk = pl.ds(p * rows, rows)
            pltpu.make_async_copy(self.src.at[chunk], self.dst.at[chunk], self.sem).start()

    def wait(self):
        pltpu.make_async_copy(self.src, self.dst, self.sem).wait()


def _shard_of(ref, a, d):
    if a == 0:
        return ref.at[:, pl.ds(pl.multiple_of(d * SHARD_COLS, LANES), SHARD_COLS)]
    if a == 2:
        return ref.at[:, pl.ds(pl.multiple_of(d * LANES, LANES), LANES)]
    return ref.at[pl.ds(pl.multiple_of(d * 128, 128), 128), :]


FULL_SHAPES = ((D, IN_COLS), (D, D), (512, D), (D, D))
SHARD_SHAPES = ((D, SHARD_COLS), (128, D), (512, 128), (128, D))


def _allgather_steps(ids, ins, outs, send_sems, recv_sems, local_sems):
    n = len(ids)
    x, y, c = _mesh_pos()
    me, sibling = (x, y, c), (x, y, 1 - c)
    chips = [(1 - x, y), (x, 1 - y), (1 - x, 1 - y)]

    def blk(a, p):
        return _shard_of(outs[a], ids[a], 4 * p[0] + 2 * p[1] + p[2])

    def copy(a, k, block, to, src=None):
        return pltpu.make_async_remote_copy(
            src_ref=blk(a, block) if src is None else src, dst_ref=blk(a, block),
            send_sem=send_sems.at[a * 7 + k], recv_sem=recv_sems.at[a * 7 + k],
            device_id=to, device_id_type=MESH)

    mine = [pltpu.make_async_copy(ins[a], blk(a, me), local_sems.at[a]) for a in range(n)]
    first = []
    for a in range(n):
        first += [copy(a, 1 + j, me, (*chip, c), src=ins[a]) for j, chip in enumerate(chips)]
    for a in range(n):
        first.append(copy(a, 0, me, sibling, src=ins[a]))
    passed = [copy(a, 4 + j, (*chip, c), sibling) for j, chip in enumerate(chips) for a in range(n)]

    def start():
        for cp in mine + first:
            cp.start()

    def middle():
        for j, chip in enumerate(chips):
            for a in range(n):
                copy(a, 1 + j, (*chip, c), me).wait_recv()
                passed[j * n + a].start()

    def end():
        for a in range(n):
            copy(a, 0, sibling, me).wait_recv()
        for j, chip in enumerate(chips):
            for a in range(n):
                copy(a, 4 + j, (*chip, 1 - c), me).wait_recv()
        for cp in first + passed:
            cp.wait_send()
        for cp in mine:
            cp.wait()

    return start, middle, end


def _in_proj_gather(x, norm_w, w_shard):
    half = S // 2
    slab = (D, SHARD_COLS)
    xt = 512

    def body(x_hbm, nw_ref, w_hbm, z_hbm, wfull_hbm, h_hbm, h_buf, land, zstage, xbuf,
             h_sem, own_sem, z_sem, wout_sem, x_sem, send_sems, recv_sems):
        x, y, c = _mesh_pos()
        sibling = (x, y, 1 - c)
        north = c == 1

        def chips_of(first_x):
            near = (jnp.where(first_x, 1 - x, x), jnp.where(first_x, y, 1 - y))
            far = (jnp.where(first_x, x, 1 - x), jnp.where(first_x, 1 - y, y))
            return [near, far, (1 - x, 1 - y)]

        mine, theirs = chips_of(north), chips_of(jnp.logical_not(north))

        def dev(chip, core):
            return 4 * chip[0] + 2 * chip[1] + core

        block_of = ([dev((x, y), c), dev((x, y), 1 - c)] + [dev(q, c) for q in mine]
                    + [dev(q, 1 - c) for q in theirs])

        def cols(d):
            if isinstance(d, int):
                return pl.ds(d * SHARD_COLS, SHARD_COLS)
            return pl.ds(pl.multiple_of(d * SHARD_COLS, LANES), SHARD_COLS)

        def send(k, src, dst_slot, to):
            return pltpu.make_async_remote_copy(
                src_ref=src, dst_ref=land.at[dst_slot], send_sem=send_sems.at[k], recv_sem=recv_sems.at[k],
                device_id=to, device_id_type=MESH)

        def to_sibling():
            return send(0, w_hbm, 1, sibling)

        def to_chip(j):
            if j == 2:
                return send(3, land.at[2], 4, (*mine[1], c))
            return send(1 + j, w_hbm, 2 + j, (*mine[j], c))

        def pass_on(j):
            return send(4 + j, land.at[2 + j], 5 + j, sibling)

        def x_tile(i):
            return _SplitCopy(x_hbm.at[pl.ds(i * xt, xt), :], xbuf.at[i % 2], x_sem.at[i % 2])

        own = _SplitCopy(w_hbm, land.at[0], own_sem)
        own.start()
        x_tile(0).start()
        to_sibling().start()
        to_chip(0).start()
        for i in range(S // xt):
            if i + 1 < S // xt:
                x_tile(i + 1).start()
            x_tile(i).wait()
            xv = xbuf[i % 2]
            r = lax.rsqrt(jnp.mean(xv * xv, axis=-1, keepdims=True) + EPS)
            h_buf[i * xt:(i + 1) * xt, :] = (xv * r * nw_ref[...]).astype(BF16)
        h_out = _SplitCopy(h_buf, h_hbm, h_sem)
        h_out.start()
        own.wait()

        def multiply(slot, n_done):
            d = block_of[slot]
            out = _SplitCopy(land.at[slot], wfull_hbm.at[:, cols(d)], wout_sem.at[slot])
            out.start()
            for r in range(2):
                rows = pl.ds(r * half, half)
                zc = _SplitCopy(zstage.at[r], z_hbm.at[rows, cols(d)], z_sem.at[r])
                if n_done > 0:
                    zc.wait()
                zstage[r] = _dot(h_buf[r * half:(r + 1) * half, :], land[slot])
                zc.start()
            return out

        outs = [multiply(0, 0)]
        to_sibling().wait_recv()
        outs.append(multiply(1, 1))
        done = 2
        for j in range(3):
            to_chip(j).wait_recv()
            pass_on(j).start()
            to_chip(j).wait_send()
            if j < 2:
                to_chip(j + 1).start()
            outs.append(multiply(2 + j, done))
            pass_on(j).wait_recv()
            outs.append(multiply(5 + j, done + 1))
            done += 2
        for r in range(2):
            _SplitCopy(zstage.at[r], z_hbm.at[pl.ds(r * half, half), cols(0)], z_sem.at[r]).wait()
        for out in outs:
            out.wait()
        h_out.wait()
        to_sibling().wait_send()
        for j in range(3):
            pass_on(j).wait_send()

    any_spec = pl.BlockSpec(memory_space=pl.ANY)
    return pl.pallas_call(
        body, name="in_proj_gather",
        out_shape=[jax.ShapeDtypeStruct((S, IN_COLS), F32), jax.ShapeDtypeStruct((D, IN_COLS), BF16),
                   jax.ShapeDtypeStruct((S, D), BF16)],
        in_specs=[any_spec, pl.BlockSpec(memory_space=pltpu.VMEM), any_spec], out_specs=[any_spec] * 3,
        scratch_shapes=[pltpu.VMEM((S, D), BF16), pltpu.VMEM((8,) + slab, BF16), pltpu.VMEM((2, half, SHARD_COLS), F32),
                        pltpu.VMEM((2, xt, D), F32),
                        pltpu.SemaphoreType.DMA, pltpu.SemaphoreType.DMA, pltpu.SemaphoreType.DMA((2,)),
                        pltpu.SemaphoreType.DMA((8,)), pltpu.SemaphoreType.DMA((2,)),
                        pltpu.SemaphoreType.DMA((7,)), pltpu.SemaphoreType.DMA((7,))],
        compiler_params=_cp(),
    )(x, norm_w, w_shard)


def _exchange_sibling(ids, gb):
    n = len(gb)

    def body(*refs):
        ins, outs = refs[:n], refs[n:2 * n]
        send_sems, recv_sems = refs[2 * n:]
        x, y, c = _mesh_pos()
        sibling = (x, y, 1 - c)
        copies = []
        for i, a in enumerate(ids):
            for q in range(4):
                copies.append(pltpu.make_async_remote_copy(
                    src_ref=_shard_of(ins[i], a, 2 * q + (1 - c)), dst_ref=outs[i].at[q],
                    send_sem=send_sems.at[i * 4 + q], recv_sem=recv_sems.at[i * 4 + q],
                    device_id=sibling, device_id_type=MESH))
        for cp in copies:
            cp.start()
        for cp in copies:
            cp.wait()

    any_spec = pl.BlockSpec(memory_space=pl.ANY)
    return pl.pallas_call(
        body, name="grads_to_sibling",
        out_shape=[jax.ShapeDtypeStruct((4,) + SHARD_SHAPES[a], BF16) for a in ids],
        in_specs=[any_spec] * n, out_specs=[any_spec] * n,
        scratch_shapes=[pltpu.SemaphoreType.DMA((4 * n,)), pltpu.SemaphoreType.DMA((4 * n,))],
    )(*gb)


def _exchange_chips_steps(ins, outs, send_sems, recv_sems):
    x, y, c = _mesh_pos()
    chips = [(1 - x, y), (x, 1 - y), (1 - x, 1 - y)]
    copies = []
    for a in range(len(ins)):
        for k, chip in enumerate(chips):
            copies.append(pltpu.make_async_remote_copy(
                src_ref=ins[a].at[2 * chip[0] + chip[1]], dst_ref=outs[a].at[k],
                send_sem=send_sems.at[a * 3 + k], recv_sem=recv_sems.at[a * 3 + k],
                device_id=(*chip, c), device_id_type=MESH))

    def start():
        for cp in copies:
            cp.start()

    def end():
        for cp in copies:
            cp.wait()

    return start, end


def _gather_small_steps(small_ref, small_out, ssend, srecv, local_sem):
    x, y, c = _mesh_pos()
    me = 4 * x + 2 * y + c
    copies = []
    for r in range(1, NDEV):
        peer = (1 - x if r & 4 else x, 1 - y if r & 2 else y, 1 - c if r & 1 else c)
        copies.append(pltpu.make_async_remote_copy(
            src_ref=small_ref, dst_ref=small_out.at[me],
            send_sem=ssend.at[r - 1], recv_sem=srecv.at[r - 1],
            device_id=peer, device_id_type=MESH))
    own = pltpu.make_async_copy(small_ref, small_out.at[me], local_sem)

    def start():
        own.start()
        for cp in copies:
            cp.start()

    def end():
        for cp in copies:
            cp.wait()
        own.wait()

    return start, end


def _gather_small(small):
    def body(small_ref, small_out, ssend, srecv, local_sem):
        start, end = _gather_small_steps(small_ref, small_out, ssend, srecv, local_sem)
        start()
        end()

    any_spec = pl.BlockSpec(memory_space=pl.ANY)
    return pl.pallas_call(
        body, name="gather_small",
        out_shape=jax.ShapeDtypeStruct((NDEV,) + small.shape, F32),
        in_specs=[any_spec], out_specs=any_spec,
        scratch_shapes=[pltpu.SemaphoreType.DMA((NDEV - 1,)), pltpu.SemaphoreType.DMA((NDEV - 1,)),
                        pltpu.SemaphoreType.DMA],
    )(small)


def _shard_tiles(a):
    rows, cols = SHARD_SHAPES[a]
    tr = min(rows, 256)
    return (tr, cols), rows // tr


def _full_index(a, d, i):
    (tr, _), nt = _shard_tiles(a)
    if a in (0, 2):
        return (i, d)
    return (d * nt + i, 0)


def _cast_shards(shards):
    n = len(shards)

    def body(*refs):
        for x_ref, o_ref in zip(refs[:n], refs[n:]):
            o_ref[...] = x_ref[...].astype(BF16)

    return pl.pallas_call(
        body, name="cast_shards", out_shape=[jax.ShapeDtypeStruct(w.shape, BF16) for w in shards],
        compiler_params=_cp(),
    )(*shards)


def _chip_partials(a, g_full, r1, core):
    tile, nt = _shard_tiles(a)

    def body(c_ref, g_ref, r_ref, o_ref):
        o_ref[0] = (g_ref[...] + r_ref[0].astype(F32)).astype(BF16)

    grid_spec = pltpu.PrefetchScalarGridSpec(
        num_scalar_prefetch=1, grid=(4, nt),
        in_specs=[pl.BlockSpec(tile, lambda q, i, c: _full_index(a, 2 * q + c[0], i)),
                  pl.BlockSpec((1,) + tile, lambda q, i, c: (q, i, 0))],
        out_specs=pl.BlockSpec((1,) + tile, lambda q, i, c: (q, i, 0)))
    return pl.pallas_call(
        body, name=f"chip_partials_{a}", grid_spec=grid_spec,
        out_shape=jax.ShapeDtypeStruct((4,) + SHARD_SHAPES[a], BF16),
        compiler_params=_cp(("parallel", "parallel")),
    )(core, g_full, r1)


def _adam(w, g, m, v):
    m = ADAM_B1 * m + (1.0 - ADAM_B1) * g
    v = ADAM_B2 * v + (1.0 - ADAM_B2) * (g * g)
    m_hat = m / (1.0 - ADAM_B1 ** ADAM_STEP)
    v_hat = v / (1.0 - ADAM_B2 ** ADAM_STEP)
    delta = -ADAM_LR * (m_hat / (jnp.sqrt(v_hat) + ADAM_EPS) + ADAM_WD * w)
    return delta, m, v


def _reduce_and_update(a, w, m, v, g_full, r1, r2, pos):
    tile, nt = _shard_tiles(a)

    def body(p_ref, w_ref, m_ref, v_ref, g_ref, r1_ref, r2_ref, go_ref, do_ref, mo_ref, vo_ref):
        g = g_ref[...] + r1_ref[0].astype(F32)
        g = g + r2_ref[0].astype(F32)
        g = g + r2_ref[1].astype(F32)
        g = g + r2_ref[2].astype(F32)
        delta, m_new, v_new = _adam(w_ref[...], g, m_ref[...], v_ref[...])
        go_ref[...] = g
        do_ref[...] = delta
        mo_ref[...] = m_new
        vo_ref[...] = v_new

    own = pl.BlockSpec(tile, lambda i, p: (i, 0))
    grid_spec = pltpu.PrefetchScalarGridSpec(
        num_scalar_prefetch=1, grid=(nt,),
        in_specs=[own, own, own,
                  pl.BlockSpec(tile, lambda i, p: _full_index(a, p[0], i)),
                  pl.BlockSpec((1,) + tile, lambda i, p: (p[1], i, 0)),
                  pl.BlockSpec((3,) + tile, lambda i, p: (0, i, 0))],
        out_specs=[own] * 4)
    shp = jax.ShapeDtypeStruct(w.shape, F32)
    return pl.pallas_call(
        body, name=f"reduce_update_{a}", grid_spec=grid_spec, out_shape=[shp] * 4,
        compiler_params=_cp(("parallel",)),
    )(pos, w, m, v, g_full, r1, r2)


def _reduce_own_and_update(w, m, v, g_chip, r2):
    tile, nt = _shard_tiles(0)

    def body(w_ref, m_ref, v_ref, g_ref, r2_ref, go_ref, do_ref, mo_ref, vo_ref):
        g = g_ref[...] + r2_ref[0].astype(F32)
        g = g + r2_ref[1].astype(F32)
        delta, m_new, v_new = _adam(w_ref[...], g, m_ref[...], v_ref[...])
        go_ref[...] = g
        do_ref[...] = delta
        mo_ref[...] = m_new
        vo_ref[...] = v_new

    own = pl.BlockSpec(tile, lambda i: (i, 0))
    shp = jax.ShapeDtypeStruct(w.shape, F32)
    return pl.pallas_call(
        body, name="reduce_update_0", grid=(nt,), out_shape=[shp] * 4,
        in_specs=[own, own, own, own, pl.BlockSpec((2,) + tile, lambda i: (0, i, 0))], out_specs=[own] * 4,
        compiler_params=_cp(("parallel",)),
    )(w, m, v, g_chip, r2)


def _small_update(gathered, norm_w, lb_logits, hnw, fnw, moments):
    m_nw, m_lb, m_hn, m_fn, v_nw, v_lb, v_hn, v_fn = moments

    def body(g_ref, nw, lb, hn, fn, mnw, mlb, mhn, mfn, vnw, vlb, vhn, vfn,
             loss_o, g_nw, g_lb, g_hn, g_fn, d_nw, d_lb, d_hn, d_fn,
             mo_nw, mo_lb, mo_hn, mo_fn, vo_nw, vo_lb, vo_hn, vo_fn):
        tot = g_ref[0]
        for d in range(1, NDEV):
            tot = tot + g_ref[d]
        loss_o[...] = tot[4:5, 0:LANES]
        logits = lb[...]
        lbv = jax.nn.sigmoid(logits[0:1] - logits[1:2])
        chain = tot[1:2] * lbv * (1.0 - lbv)
        grads = (tot[0:1], jnp.concatenate([chain, -chain], axis=0), tot[2:3, 0:LANES], tot[3:4])
        outs = ((nw, mnw, vnw, g_nw, d_nw, mo_nw, vo_nw), (lb, mlb, vlb, g_lb, d_lb, mo_lb, vo_lb),
                (hn, mhn, vhn, g_hn, d_hn, mo_hn, vo_hn), (fn, mfn, vfn, g_fn, d_fn, mo_fn, vo_fn))
        for g, (w_r, m_r, v_r, g_o, d_o, m_o, v_o) in zip(grads, outs):
            delta, m_new, v_new = _adam(w_r[...], g, m_r[...], v_r[...])
            g_o[...] = g
            d_o[...] = delta
            m_o[...] = m_new
            v_o[...] = v_new

    shapes = [norm_w.shape, lb_logits.shape, hnw.shape, fnw.shape]
    out_shape = [jax.ShapeDtypeStruct((1, LANES), F32)] + [jax.ShapeDtypeStruct(s, F32) for s in shapes] * 4
    return pl.pallas_call(body, name="small_update", out_shape=out_shape, compiler_params=_cp())(
        gathered, norm_w, lb_logits, hnw, fnw, m_nw, m_lb, m_hn, m_fn, v_nw, v_lb, v_hn, v_fn)


def _block_tri(n, block, upper=False):
    r = lax.broadcasted_iota(jnp.int32, (n, n), 0)
    c = lax.broadcasted_iota(jnp.int32, (n, n), 1)
    keep = (c >= r) if upper else (c <= r)
    return jnp.where(keep & ((r // block) == (c // block)), 1.0, 0.0).astype(BF16)


def _tril_mask(n):
    r = lax.broadcasted_iota(jnp.int32, (n, n), 0)
    c = lax.broadcasted_iota(jnp.int32, (n, n), 1)
    return c <= r


def _chunk_scores(q, k, b, bex, r0, mask):
    parts, qs_l, ks_l, ek_l, eq_l = [], [], [], [], []
    for i in range(CHUNK // SUB):
        ri = slice(r0 + SUB * i, r0 + SUB * (i + 1))
        seen = slice(r0, r0 + SUB * (i + 1))
        base = bex[r0 + SUB * i:r0 + SUB * i + 1]
        eq = jnp.exp(b[ri] - base)
        ek = jnp.exp(jnp.minimum(base - b[seen], EXP_CLAMP))
        ks = k[seen] * ek
        if i + 1 < CHUNK // SUB:
            rest = jnp.zeros((CHUNK - SUB * (i + 1), 128), F32)
            ek, ks = jnp.concatenate([ek, rest], axis=0), jnp.concatenate([ks, rest], axis=0)
        qs = q[ri] * eq
        parts.append(_dot_nt(qs.astype(BF16), ks.astype(BF16)))
        qs_l.append(qs)
        ks_l.append(ks)
        ek_l.append(ek)
        eq_l.append(eq)
    return jnp.where(mask, jnp.concatenate(parts, axis=0), 0.0), qs_l, ks_l, ek_l, eq_l


def _hgrn_cols(hq, hf, hi, lb):
    sg = jax.nn.sigmoid(hf)
    f = lb + (1.0 - lb) * sg
    g = jnp.log(f)
    b = _dot_ones(_block_tri(HBLK, CHUNK), g)
    return _silu(hq), 1.0 - f, g, hi, sg, f, b


GATHER_IDS = (1, 2, 3)


def _hgrn_fwd(z, lbv, hnw):
    ntb, nch = S // HBLK, HBLK // CHUNK

    def body(hq_ref, hf_ref, hi_ref, hg_ref, lb_ref, hnw_ref, o_ref, oa_ref, st_ref, state):
        @pl.when(pl.program_id(0) == 0)
        def _():
            state[...] = jnp.zeros_like(state)

        q_a, k_a, g_a, v_a, _, _, b_a = _hgrn_cols(hq_ref[...], hf_ref[...], hi_ref[...], lb_ref[...])
        bex_a = b_a - g_a
        eb_a = jnp.exp(b_a)
        mask = _tril_mask(CHUNK)
        hg = hg_ref[...]
        w = hnw_ref[...]
        for h in range(HEADS):
            cols = slice(128 * h, 128 * h + 128)
            q, k, v, b, bex, eb = q_a[:, cols], k_a[:, cols], v_a[:, cols], b_a[:, cols], bex_a[:, cols], eb_a[:, cols]
            st = state[h]
            outs = []
            for c in range(nch):
                r0 = c * CHUNK
                rows = slice(r0, r0 + CHUNK)
                a = _chunk_scores(q, k, b, bex, r0, mask)[0]
                vb = v[rows].astype(BF16)
                b_last = b[r0 + CHUNK - 1:r0 + CHUNK]
                qe = (q[rows] * eb[rows]).astype(BF16)
                outs.append(_dot(a.astype(BF16), vb) + _dot_nt(qe, st.astype(BF16)))
                st_ref[h, c] = st
                ke = (k[rows] * jnp.exp(b_last - b[rows])).astype(BF16)
                st = st * jnp.exp(b_last) + _dot_tn(vb, ke)
            state[h] = st
            o = jnp.concatenate(outs, axis=0)
            o_ref[:, cols] = o
            r = lax.rsqrt(jnp.mean(o * o, axis=-1, keepdims=True) + EPS)
            oa_ref[:, cols] = (o * r * w * _silu(hg[:, cols])).astype(BF16)

    def zcol(j):
        return pl.BlockSpec((HBLK, D), lambda t: (t, j))

    out_blk = pl.BlockSpec((HBLK, D), lambda t: (t, 0))
    return pl.pallas_call(
        body, name="hgrn_fwd", grid=(ntb,),
        out_shape=[jax.ShapeDtypeStruct((S, D), F32), jax.ShapeDtypeStruct((S, D), BF16),
                   jax.ShapeDtypeStruct((HEADS, S // CHUNK, 128, 128), F32)],
        in_specs=[zcol(0), zcol(1), zcol(2), zcol(3),
                  pl.BlockSpec((1, D), lambda t: (0, 0)), pl.BlockSpec((1, 128), lambda t: (0, 0))],
        out_specs=[out_blk, out_blk, pl.BlockSpec((HEADS, nch, 128, 128), lambda t: (0, t, 0, 0))],
        scratch_shapes=[pltpu.VMEM((HEADS, 128, 128), F32)],
        compiler_params=_cp(("arbitrary",)),
    )(z, z, z, z, lbv, hnw)


def _half_mask():
    lane = lax.broadcasted_iota(jnp.int32, (1, LANES), 1)
    return (lane % 64) < 32


def _rope(t, cc, ss, first_half):
    partner = jnp.where(first_half, pltpu.roll(t, 96, 1), pltpu.roll(t, 32, 1))
    return t * cc + partner * ss


def _attn_masks():
    i = lax.broadcasted_iota(jnp.int32, (128, 128), 0)
    j = lax.broadcasted_iota(jnp.int32, (128, 128), 1)
    return j >= i, j <= i


def _to_residues_dyn(g, dst, src, row0=0, dtype=None):
    for gi, dil in enumerate((1, 4, 16)):
        m = S // dil

        @pl.when(g == gi)
        def _(dil=dil, m=m):
            for r in range(dil):
                v = src[...] if dil == 1 else src[pl.ds(r, m, stride=dil), :]
                if dtype is not None:
                    v = v.astype(dtype)
                dst[row0 + r * m:row0 + (r + 1) * m, 0:LANES] = v


def _from_residues_dyn(g, dst, src, row0=0):
    for gi, dil in enumerate((1, 4, 16)):
        m = S // dil

        @pl.when(g == gi)
        def _(dil=dil, m=m):
            for r in range(dil):
                v = src[row0 + r * m:row0 + (r + 1) * m, :]
                if dil == 1:
                    dst[...] = v
                else:
                    dst[pl.ds(r, m, stride=dil), :] = v


def _group_blocks(g):
    return jnp.where(g == 0, 16, jnp.where(g == 1, 4, 1))


def _attn_in_specs(extra):
    def zcol(off):
        return pl.BlockSpec((S, LANES), lambda p, g: (0, off + 4 * g + p))

    per_pair = pl.BlockSpec((S, LANES), lambda p, g: (0, p))
    const = pl.BlockSpec((S, LANES), lambda p, g: (0, 0))
    return [zcol(32), zcol(44), zcol(56), pl.BlockSpec((S, LANES), lambda p, g: (0, 68 + p)), const, const] + [per_pair] * extra


def _attn_fwd(z, cc, ss, shards):
    def body(q_ref, k_ref, v_ref, ag_ref, cc_ref, ss_ref, s0, s1, s2,
             ob_ref, lse_ref, obg_ref, qsb_ref, ksb_ref, vsb_ref, f0, f1, f2,
             tmp, qs, ks, vx, og, mg, lg, o_t, m_t, l_t, o_acc, m_acc, l_acc, send_sems, recv_sems, local_sems):
        g = pl.program_id(1)
        step = pl.program_id(0) * 3 + g
        start, middle, end = _allgather_steps(GATHER_IDS, (s0, s1, s2), (f0, f1, f2), send_sems, recv_sems, local_sems)
        pl.when(step == 0)(start)
        pl.when(step == 9)(middle)
        first_half = _half_mask()
        prev_ok, cur_ok = _attn_masks()
        lane = lax.broadcasted_iota(jnp.int32, (1, LANES), 1)
        heads = (lane < 64, lane >= 64)
        nblk = _group_blocks(g)

        @pl.when(g == 0)
        def _():
            ks[0:ATT_PAD, :] = jnp.zeros((ATT_PAD, LANES), BF16)
            vx[0:ATT_PAD, 0:LANES] = jnp.zeros((ATT_PAD, LANES), BF16)
            vx[:, LANES:2 * LANES] = jnp.ones((ATT_PAD + S, LANES), BF16)

        tmp[...] = _rope(q_ref[...], cc_ref[...], ss_ref[...], first_half) * ATT_SCALE
        _to_residues_dyn(g, qs, tmp)
        tmp[...] = _rope(k_ref[...], cc_ref[...], ss_ref[...], first_half)
        _to_residues_dyn(g, ks, tmp, ATT_PAD, BF16)
        _to_residues_dyn(g, vx, v_ref, ATT_PAD, BF16)

        def unit(u, carry):
            start = pl.multiple_of(u * 128, 128)
            cur = pl.ds(start, 128)
            pm = prev_ok & ((u & (nblk - 1)) != 0)
            qu = qs[cur, :]
            kcat = ks[pl.ds(start, 256), :]
            vext = vx[pl.ds(start, 256), :]
            o_u = m_u = l_u = None
            for hh in range(2):
                s = _dot_nt(jnp.where(heads[hh], qu, 0.0).astype(BF16), kcat)
                sp = jnp.where(pm, s[:, 0:128], -jnp.inf)
                sc = jnp.where(cur_ok, s[:, 128:256], -jnp.inf)
                m = jnp.max(jnp.maximum(sp, sc), axis=-1, keepdims=True)
                p = jnp.concatenate([jnp.exp(sp - m), jnp.exp(sc - m)], axis=1).astype(BF16)
                ol = _dot(p, vext)
                mb = jnp.broadcast_to(m, (128, LANES))
                if hh == 0:
                    o_u, l_u, m_u = ol[:, 0:128], ol[:, 128:256], mb
                else:
                    o_u = jnp.where(heads[1], ol[:, 0:128], o_u)
                    l_u = jnp.where(heads[1], ol[:, 128:256], l_u)
                    m_u = jnp.where(heads[1], mb, m_u)
            og[cur, :] = o_u
            mg[cur, :] = m_u
            lg[cur, :] = l_u
            return carry

        lax.fori_loop(0, 16, unit, 0, unroll=16)
        qsb_ref[0] = qs[...].astype(BF16)
        ksb_ref[0] = ks[...]
        vsb_ref[0] = vx[:, 0:LANES]
        _from_residues_dyn(g, o_t, og)
        _from_residues_dyn(g, m_t, mg)
        _from_residues_dyn(g, l_t, lg)

        @pl.when(g == 0)
        def _():
            o_acc[...] = o_t[...]
            m_acc[...] = m_t[...]
            l_acc[...] = l_t[...]

        @pl.when(g > 0)
        def _():
            m_new = jnp.maximum(m_acc[...], m_t[...])
            wa, wb = jnp.exp(m_acc[...] - m_new), jnp.exp(m_t[...] - m_new)
            o_acc[...] = o_acc[...] * wa + o_t[...] * wb
            l_acc[...] = l_acc[...] * wa + l_t[...] * wb
            m_acc[...] = m_new

        @pl.when(g == 2)
        def _():
            ob = o_acc[...] / l_acc[...]
            ob_ref[...] = ob
            lse_ref[...] = m_acc[...] + jnp.log(l_acc[...])
            obg_ref[...] = (ob * _silu(ag_ref[...])).astype(BF16)

        pl.when(step == 11)(end)

    n = len(GATHER_IDS)
    any_spec = pl.BlockSpec(memory_space=pl.ANY)
    blk = pl.BlockSpec((S, LANES), lambda p, g: (0, p))
    buf = pltpu.VMEM((S, LANES), F32)
    return pl.pallas_call(
        body, name="attn_fwd", grid=(4, 3),
        out_shape=[jax.ShapeDtypeStruct((S, 512), F32), jax.ShapeDtypeStruct((S, 512), F32),
                   jax.ShapeDtypeStruct((S, 512), BF16), jax.ShapeDtypeStruct((3, S, 512), BF16),
                   jax.ShapeDtypeStruct((3, ATT_PAD + S, 512), BF16), jax.ShapeDtypeStruct((3, ATT_PAD + S, 512), BF16)]
        + [jax.ShapeDtypeStruct(FULL_SHAPES[a], BF16) for a in GATHER_IDS],
        in_specs=_attn_in_specs(0) + [any_spec] * n,
        out_specs=[blk, blk, blk, pl.BlockSpec((1, S, LANES), lambda p, g: (g, 0, p)),
                   pl.BlockSpec((1, ATT_PAD + S, LANES), lambda p, g: (g, 0, p)),
                   pl.BlockSpec((1, ATT_PAD + S, LANES), lambda p, g: (g, 0, p))] + [any_spec] * n,
        scratch_shapes=[buf, buf, pltpu.VMEM((ATT_PAD + S, LANES), BF16), pltpu.VMEM((ATT_PAD + S, 2 * LANES), BF16)]
        + [buf] * 9 + [pltpu.SemaphoreType.DMA((7 * n,)), pltpu.SemaphoreType.DMA((7 * n,)), pltpu.SemaphoreType.DMA((n,))],
        compiler_params=_cp(("arbitrary", "arbitrary")),
    )(z, z, z, z, cc, ss, *shards)


def _tail(x, o_a, o_bg, z, target, w_a, w_b, w_out, fnw):
    tm = 256

    def body(x_ref, oa_ref, ob_ref, gpa_ref, gpb_ref, t_ref, wa_ref, wb_ref, wo_ref, fnw_ref,
             dx2_ref, dx2b_ref, dz_hbm, doa_ref, dob_ref, mg_ref, dya_ref, dyb_ref, small_ref, dgp, dgp_sem):
        step = pl.program_id(0)
        slot = step % 2

        def dgp_copy(at_step, at_slot):
            return pltpu.make_async_copy(
                dgp.at[at_slot], dz_hbm.at[pl.ds(pl.multiple_of(at_step * tm, tm), tm), pl.ds(DZ_GATES, 2 * D)],
                dgp_sem.at[at_slot])

        @pl.when(step == 0)
        def _():
            small_ref[...] = jnp.zeros_like(small_ref)

        @pl.when(step >= 2)
        def _():
            dgp_copy(step - 2, slot).wait()

        wa, wb, wo = wa_ref[...], wb_ref[...], wo_ref[...]
        y_a = _dot(oa_ref[...], wa)
        y_b = _dot(ob_ref[...], wb)
        ga = jax.nn.sigmoid(gpa_ref[...])
        gb = jax.nn.sigmoid(gpb_ref[...])
        merged = (ga * y_a + gb * y_b).astype(BF16)
        x2 = x_ref[...] + _dot(merged, wo)
        r2 = lax.rsqrt(jnp.mean(x2 * x2, axis=-1, keepdims=True) + EPS)
        n2 = x2 * r2
        fw = fnw_ref[...]
        err = n2 * fw - t_ref[...]
        loss = 0.5 * jnp.sum(jnp.sum(err * err, axis=-1, keepdims=True), axis=0, keepdims=True) / D
        dy = err * (1.0 / D)
        g_fnw = jnp.sum(dy * n2, axis=0, keepdims=True)
        dn = dy * fw
        dx2 = r2 * (dn - n2 * jnp.mean(dn * n2, axis=-1, keepdims=True))
        dx2b = dx2.astype(BF16)
        dmerged = _dot_nt(dx2b, wo)
        dy_a = (dmerged * ga).astype(BF16)
        dy_b = (dmerged * gb).astype(BF16)
        dx2_ref[...] = dx2
        dx2b_ref[...] = dx2b
        dgp[slot, :, 0:D] = (dmerged * y_a * ga * (1.0 - ga)).astype(BF16)
        dgp[slot, :, D:2 * D] = (dmerged * y_b * gb * (1.0 - gb)).astype(BF16)
        dgp_copy(step, slot).start()
        doa_ref[...] = _dot_nt(dy_a, wa)
        dob_ref[...] = _dot_nt(dy_b, wb)
        mg_ref[...] = merged
        dya_ref[...] = dy_a
        dyb_ref[...] = dy_b
        small_ref[0:1, :] += g_fnw
        small_ref[1:2, :] += jnp.broadcast_to(loss, (1, D))

        @pl.when(step == S // tm - 1)
        def _():
            dgp_copy(step - 1, 1 - slot).wait()
            dgp_copy(step, slot).wait()

    def rows(cols, off=0):
        return pl.BlockSpec((tm, cols), lambda i: (i, off))

    def whole(shape):
        return pl.BlockSpec(shape, lambda i: (0, 0))

    return pl.pallas_call(
        body, name="tail", grid=(S // tm,),
        out_shape=[jax.ShapeDtypeStruct((S, D), F32), jax.ShapeDtypeStruct((S, D), BF16),
                   jax.ShapeDtypeStruct((S, IN_COLS), BF16), jax.ShapeDtypeStruct((S, D), F32),
                   jax.ShapeDtypeStruct((S, 512), F32), jax.ShapeDtypeStruct((S, D), BF16),
                   jax.ShapeDtypeStruct((S, D), BF16), jax.ShapeDtypeStruct((S, D), BF16),
                   jax.ShapeDtypeStruct((8, D), F32)],
        in_specs=[rows(D), rows(D), rows(512), rows(D, 9), rows(D, 10), rows(D),
                  whole((D, D)), whole((512, D)), whole((D, D)), whole((1, D))],
        out_specs=[rows(D), rows(D), pl.BlockSpec(memory_space=pl.ANY), rows(D), rows(512), rows(D), rows(D),
                   rows(D), whole((8, D))],
        scratch_shapes=[pltpu.VMEM((2, tm, 2 * D), BF16), pltpu.SemaphoreType.DMA((2,))],
        compiler_params=_cp(("arbitrary",)),
    )(x, o_a, o_bg, z, z, target, w_a, w_b, w_out, fnw)


def _tn_matmul(a, b, name):
    m, n = a.shape[1], b.shape[1]
    tn = 512

    def body(a_ref, b_ref, o_ref, ob_ref):
        acc = _dot_tn(a_ref[...], b_ref[...])
        o_ref[...] = acc
        ob_ref[...] = acc.astype(BF16)

    out_blk = pl.BlockSpec((m, tn), lambda j: (0, j))
    return pl.pallas_call(
        body, name=name, grid=(n // tn,),
        out_shape=[jax.ShapeDtypeStruct((m, n), F32), jax.ShapeDtypeStruct((m, n), BF16)],
        in_specs=[pl.BlockSpec((S, m), lambda j: (0, 0)), pl.BlockSpec((S, tn), lambda j: (0, j))],
        out_specs=[out_blk, out_blk],
        compiler_params=_cp(("parallel",)),
    )(a, b)


def _hgrn_bwd(z, o, do_a, states, lbv, hnw, partials, dz):
    ntb, nch = S // HBLK, HBLK // CHUNK
    n = len(GATHER_IDS)

    def body(hq_ref, hf_ref, hi_ref, hg_ref, o_ref, doa_ref, st_ref, lb_ref, hnw_ref, p0, p1, p2, dz_in,
             dz_ref, glb_ref, ghn_ref, e0, e1, e2, dstate, send_sems, recv_sems):
        dhq_ref, dhf_ref, dhi_ref, dhg_ref = (dz_ref.at[:, pl.ds(j * D, D)] for j in range(4))
        start, end = _exchange_chips_steps((p0, p1, p2), (e0, e1, e2), send_sems, recv_sems)

        @pl.when(pl.program_id(0) == 0)
        def _():
            dstate[...] = jnp.zeros_like(dstate)
            glb_ref[...] = jnp.zeros_like(glb_ref)
            ghn_ref[...] = jnp.zeros_like(ghn_ref)
            start()

        lb_a = lb_ref[...]
        hq_a, hg_a = hq_ref[...], hg_ref[...]
        q_a, k_a, g_a, v_a, sg_a, f_a, b_a = _hgrn_cols(hq_a, hf_ref[...], hi_ref[...], lb_a)
        bex_a = b_a - g_a
        eb_a = jnp.exp(b_a)
        w = hnw_ref[...]
        mask = _tril_mask(CHUNK)
        upper = _block_tri(CHUNK, CHUNK, upper=True)
        for h in range(HEADS):
            cols = slice(128 * h, 128 * h + 128)
            q, k, v, b, bex, eb = q_a[:, cols], k_a[:, cols], v_a[:, cols], b_a[:, cols], bex_a[:, cols], eb_a[:, cols]
            hq, hg, sg, f, lb = hq_a[:, cols], hg_a[:, cols], sg_a[:, cols], f_a[:, cols], lb_a[:, cols]
            ov, doa = o_ref[:, cols], doa_ref[:, cols]
            r = lax.rsqrt(jnp.mean(ov * ov, axis=-1, keepdims=True) + EPS)
            n = ov * r
            sil = _silu(hg)
            dhg_ref[:, cols] = (doa * n * w * _dsilu(hg)).astype(BF16)
            ghn_ref[h] += jnp.sum(doa * sil * n, axis=0, keepdims=True)
            dn = doa * sil * w
            do = r * (dn - n * jnp.mean(dn * n, axis=-1, keepdims=True))

            dst = dstate[h]
            dq_l, dk_l, dv_l, dg_l = [None] * nch, [None] * nch, [None] * nch, [None] * nch
            for c in reversed(range(nch)):
                r0 = c * CHUNK
                rows = slice(r0, r0 + CHUNK)
                st = st_ref[h, c]
                bc, kc, qc = b[rows], k[rows], q[rows]
                vb, dob = v[rows].astype(BF16), do[rows].astype(BF16)
                b_last = bc[CHUNK - 1:CHUNK]
                e_last = jnp.exp(b_last)
                ekl = jnp.exp(b_last - bc)
                dstb = dst.astype(BF16)
                a, qs_l, ks_l, ek_l, eq_l = _chunk_scores(q, k, b, bex, r0, mask)
                da = jnp.where(mask, _dot_nt(dob, vb), 0.0)
                dv_l[c] = _dot_tn(a.astype(BF16), dob) + _dot_nt((kc * ekl).astype(BF16), dstb)
                dq_inter = _dot(dob, st.astype(BF16)) * eb[rows]
                dk_state = _dot(vb, dstb) * ekl
                dq_parts, dk_intra = [], jnp.zeros((CHUNK, 128), F32)
                dab = da.astype(BF16)
                for i in range(CHUNK // SUB):
                    da_i = dab[SUB * i:SUB * (i + 1)]
                    ks_hi, ks_lo = _split2(ks_l[i])
                    qs_hi, qs_lo = _split2(qs_l[i])
                    dq_parts.append((_dot(da_i, ks_hi) + _dot(da_i, ks_lo)) * eq_l[i])
                    dk_intra = dk_intra + (_dot_tn(da_i, qs_hi) + _dot_tn(da_i, qs_lo)) * ek_l[i]
                dq = jnp.concatenate(dq_parts, axis=0) + dq_inter
                dk = dk_intra + dk_state
                last = (e_last * jnp.sum(st * dst, axis=0, keepdims=True)
                        + jnp.sum(kc * dk_state, axis=0, keepdims=True))
                dg_l[c] = _dot_ones(upper, qc * dq - kc * dk) + last
                dq_l[c], dk_l[c] = dq, dk
                dst = dst * e_last + _dot_tn(dob, (qc * eb[rows]).astype(BF16))
            dstate[h] = dst
            dq, dk = jnp.concatenate(dq_l, axis=0), jnp.concatenate(dk_l, axis=0)
            dg, dv = jnp.concatenate(dg_l, axis=0), jnp.concatenate(dv_l, axis=0)
            dhq_ref[:, cols] = (dq * _dsilu(hq)).astype(BF16)
            dhi_ref[:, cols] = dv.astype(BF16)
            df = dg / f - dk
            dhf_ref[:, cols] = (df * (1.0 - lb) * sg * (1.0 - sg)).astype(BF16)
            glb_ref[:, cols] += jnp.sum(df * (1.0 - sg), axis=0, keepdims=True)

        pl.when(pl.program_id(0) == ntb - 1)(end)

    def rev(t):
        return ntb - 1 - t

    def zcol(j):
        return pl.BlockSpec((HBLK, D), lambda t: (rev(t), j))

    blk = pl.BlockSpec((HBLK, D), lambda t: (rev(t), 0))
    any_spec = pl.BlockSpec(memory_space=pl.ANY)
    return pl.pallas_call(
        body, name="hgrn_bwd", grid=(ntb,),
        out_shape=[jax.ShapeDtypeStruct((S, IN_COLS), BF16)]
        + [jax.ShapeDtypeStruct((1, D), F32), jax.ShapeDtypeStruct((HEADS, 1, 128), F32)]
        + [jax.ShapeDtypeStruct((3,) + SHARD_SHAPES[a], BF16) for a in GATHER_IDS],
        in_specs=[zcol(0), zcol(1), zcol(2), zcol(3), blk, blk,
                  pl.BlockSpec((HEADS, nch, 128, 128), lambda t: (0, rev(t), 0, 0)),
                  pl.BlockSpec((1, D), lambda t: (0, 0)), pl.BlockSpec((1, 128), lambda t: (0, 0))]
        + [any_spec] * (n + 1),
        out_specs=[pl.BlockSpec((HBLK, DZ_ATT), lambda t: (rev(t), 0)), pl.BlockSpec((1, D), lambda t: (0, 0)),
                   pl.BlockSpec((HEADS, 1, 128), lambda t: (0, 0, 0))] + [any_spec] * n,
        scratch_shapes=[pltpu.VMEM((HEADS, 128, 128), F32), pltpu.SemaphoreType.DMA((3 * n,)),
                        pltpu.SemaphoreType.DMA((3 * n,))],
        input_output_aliases={9 + n: 0},
        compiler_params=_cp(("arbitrary",)),
    )(z, z, z, z, o, do_a, states, lbv, hnw, *partials, dz)


def _attn_bwd(z, qsb, ksb, vsb, cc, ss, ob, lse, do_bg, dz):
    def body(qs, ks, vs, ag_ref, cc_ref, ss_ref, ob_ref, lse_ref, dobg_ref, dz_in, dz_hbm,
             tmp, dos, dqs, dks, dvs, dkp, dvp, do_t, ls0_t, ls1_t, dl0_t, dl1_t, ls0, ls1, dl0, dl1,
             stage, stage_sem):
        pair, g = pl.program_id(0), pl.program_id(1)

        def out_copy(j):
            tile = DZ_ATT // LANES + (36 + pair if j == 3 else 12 * j + 4 * g + pair)
            return pltpu.make_async_copy(
                stage.at[j], dz_hbm.at[:, pl.ds(pl.multiple_of(tile * LANES, LANES), LANES)], stage_sem.at[j])

        def restage(j, value):
            pl.when(pair * 3 + g > 0)(lambda: out_copy(j).wait())
            stage[j] = value
            out_copy(j).start()

        pl.when(g == 2)(lambda: out_copy(3).wait())
        first_half = _half_mask()
        prev_ok, cur_ok = _attn_masks()
        lane = lax.broadcasted_iota(jnp.int32, (1, LANES), 1)
        heads = (lane < 64, lane >= 64)
        nblk = _group_blocks(g)
        cc_v, ss_v = cc_ref[...], ss_ref[...]

        @pl.when(g == 0)
        def _():
            ag, obv, dobg = ag_ref[...], ob_ref[...], dobg_ref[...]
            stage[3] = (dobg * obv * _dsilu(ag)).astype(BF16)
            out_copy(3).start()
            dob = dobg * _silu(ag)
            do_t[...] = dob
            prod = dob * obv
            dl = jnp.concatenate(
                [jnp.broadcast_to(jnp.sum(prod[:, 0:64], axis=-1, keepdims=True), (S, 64)),
                 jnp.broadcast_to(jnp.sum(prod[:, 64:128], axis=-1, keepdims=True), (S, 64))], axis=1)
            dl0_t[...] = dl

        _to_residues_dyn(g, dos, do_t)
        _to_residues_dyn(g, ls0, lse_ref)
        _to_residues_dyn(g, dl0, dl0_t)

        def unit(u, carry):
            start = pl.multiple_of(u * 128, 128)
            cur = pl.ds(start, 128)
            both = pl.ds(start, 256)
            pm = prev_ok & ((u & (nblk - 1)) != 0)
            qu, dou = qs[0, cur, :], dos[cur, :]
            kcat, vcat = ks[0, both, :], vs[0, both, :]
            dq_u = None
            q_l, do_l, ds_l, p_l = [], [], [], []
            ls_u, dl_u = ls0[cur, :], dl0[cur, :]
            ls_sw, dl_sw = pltpu.roll(ls_u, 64, 1), pltpu.roll(dl_u, 64, 1)
            for hh in range(2):
                q_h = jnp.where(heads[hh], qu, jnp.zeros((), BF16))
                do_h = jnp.where(heads[hh], dou, 0.0).astype(BF16)
                s = _dot_nt(q_h, kcat)
                dp = _dot_nt(do_h, vcat)
                lse_h = jnp.where(heads[hh], ls_u, ls_sw)
                dl_h = jnp.where(heads[hh], dl_u, dl_sw)
                pp = jnp.where(pm, jnp.exp(s[:, 0:128] - lse_h), 0.0)
                pc = jnp.where(cur_ok, jnp.exp(s[:, 128:256] - lse_h), 0.0)
                ds = jnp.concatenate([pp * (dp[:, 0:128] - dl_h), pc * (dp[:, 128:256] - dl_h)], axis=1).astype(BF16)
                dq = _dot(ds, kcat)
                dq_u = dq if hh == 0 else jnp.where(heads[1], dq, dq_u)
                q_l.append(q_h)
                do_l.append(do_h)
                ds_l.append(ds)
                p_l.append(jnp.concatenate([pp, pc], axis=1).astype(BF16))
            dkcat = _dot_tn(jnp.concatenate(ds_l, axis=0), jnp.concatenate(q_l, axis=0))
            dvcat = _dot_tn(jnp.concatenate(p_l, axis=0), jnp.concatenate(do_l, axis=0))
            dkp[cur, :] = dkcat[0:128]
            dks[cur, :] = dkcat[128:256]
            dvp[cur, :] = dvcat[0:128]
            dvs[cur, :] = dvcat[128:256]
            dqs[cur, :] = dq_u
            return carry

        lax.fori_loop(0, 16, unit, 0, unroll=ATT_UNROLL)
        dks[0:S - 128, :] += dkp[128:S, :]
        dvs[0:S - 128, :] += dvp[128:S, :]
        _from_residues_dyn(g, tmp, dqs)
        restage(0, (_rope(tmp[...], cc_v, -ss_v, first_half) * ATT_SCALE).astype(BF16))
        _from_residues_dyn(g, tmp, dks)
        restage(1, _rope(tmp[...], cc_v, -ss_v, first_half).astype(BF16))
        _from_residues_dyn(g, tmp, dvs)
        restage(2, tmp[...].astype(BF16))

        @pl.when(pair * 3 + g == 11)
        def _():
            for j in range(3):
                out_copy(j).wait()

    any_spec = pl.BlockSpec(memory_space=pl.ANY)
    buf = pltpu.VMEM((S, LANES), F32)
    padded_b = pltpu.VMEM((ATT_PAD + S, LANES), BF16)
    return pl.pallas_call(
        body, name="attn_bwd", grid=(4, 3),
        out_shape=jax.ShapeDtypeStruct((S, IN_COLS), BF16),
        in_specs=[pl.BlockSpec((1, S, LANES), lambda p, g: (g, 0, p)),
                  pl.BlockSpec((1, ATT_PAD + S, LANES), lambda p, g: (g, 0, p)),
                  pl.BlockSpec((1, ATT_PAD + S, LANES), lambda p, g: (g, 0, p))] + _attn_in_specs(3)[3:] + [any_spec],
        out_specs=any_spec,
        scratch_shapes=[buf] * 16 + [pltpu.VMEM((4, S, LANES), BF16), pltpu.SemaphoreType.DMA((4,))],
        input_output_aliases={9: 0},
        compiler_params=_cp(("arbitrary", "arbitrary")),
    )(qsb, ksb, vsb, z, cc, ss, ob, lse, do_bg, dz)


def _in_proj_bwd(dz, h, w_in):
    half = S // 2
    slab = (D, SHARD_COLS)

    def body(dz_hbm, h_hbm, w_hbm, dh_hbm, g_chip, r1_hbm, relay_hbm, r2_hbm,
             h_buf, dz_buf, stage_d, r1_buf, stage_i, acc,
             dz_sem, w_sem, h_sem, r1_sem, out_sem, send_d, recv_d, send_i, recv_i):
        x, y, c = _mesh_pos()
        sibling = (x, y, 1 - c)
        north = c == 1
        near = (jnp.where(north, 1 - x, x), jnp.where(north, y, 1 - y))
        far = (jnp.where(north, x, 1 - x), jnp.where(north, 1 - y, y))
        chips = [(1 - x, 1 - y), near, far, (x, y)]

        def cols(d):
            return pl.ds(pl.multiple_of(d * SHARD_COLS, LANES), SHARD_COLS)

        blocks = []
        for q_sib, q in zip([chips[0], far, near, chips[3]], chips):
            blocks += [4 * q_sib[0] + 2 * q_sib[1] + (1 - c), 4 * q[0] + 2 * q[1] + c]

        def dz_tile(t):
            return _SplitCopy(dz_hbm.at[pl.ds((t % 2) * half, half), cols(blocks[t // 2])],
                                         dz_buf.at[t % 2], dz_sem.at[t % 2])

        def to_sibling(i):
            return pltpu.make_async_remote_copy(
                src_ref=stage_d.at[i % 2], dst_ref=r1_hbm.at[i], send_sem=send_d.at[i], recv_sem=recv_d.at[i],
                device_id=sibling, device_id_type=MESH)

        def to_owner(i):
            dst = relay_hbm if i == 0 else r2_hbm.at[i - 1]
            return pltpu.make_async_remote_copy(
                src_ref=stage_i.at[i], dst_ref=dst, send_sem=send_i.at[i], recv_sem=recv_i.at[i],
                device_id=(*(far if i == 2 else near), c), device_id_type=MESH)

        h_copy = _SplitCopy(h_hbm, h_buf, h_sem)
        h_copy.start()
        dz_tile(0).start()
        h_copy.wait()
        for b in range(8):
            i = b // 2
            g = None
            for r in range(2):
                t = 2 * b + r
                if t + 1 < 16:
                    dz_tile(t + 1).start()
                dz_tile(t).wait()
                part = _dot_tn(h_buf[r * half:(r + 1) * half, :], dz_buf[t % 2])
                g = part if g is None else g + part
                if b % 2 == 1 and r == 0:
                    to_sibling(i).wait_recv()
                    r1_copy = _SplitCopy(r1_hbm.at[i], r1_buf, r1_sem)
                    r1_copy.start()
            if b % 2 == 0:
                if i >= 2:
                    to_sibling(i - 2).wait_send()
                stage_d[i % 2] = g.astype(BF16)
                to_sibling(i).start()
            else:
                r1_copy.wait()
                g = g + r1_buf[...].astype(F32)
                if i == 2:
                    to_owner(0).wait_recv()
                    relay_copy = _SplitCopy(relay_hbm, r1_buf, r1_sem)
                    relay_copy.start()
                    relay_copy.wait()
                    g = g + r1_buf[...].astype(F32)
                if i < 3:
                    stage_i[i] = g.astype(BF16)
                    to_owner(i).start()
                else:
                    g_chip[...] = g
        to_sibling(2).wait_send()
        to_sibling(3).wait_send()

        def dz2(t):
            return _SplitCopy(
                dz_hbm.at[pl.ds((t % 2) * half, half), pl.ds((t // 2) * SHARD_COLS, SHARD_COLS)],
                dz_buf.at[t % 2], dz_sem.at[t % 2])

        def w2(b):
            return _SplitCopy(w_hbm.at[:, pl.ds(b * SHARD_COLS, SHARD_COLS)],
                                         stage_d.at[b % 2], w_sem.at[b % 2])

        dz2(0).start()
        w2(0).start()
        for t in range(16):
            b, r = t // 2, t % 2
            if t + 1 < 16:
                dz2(t + 1).start()
            if r == 0:
                if b + 1 < 8:
                    w2(b + 1).start()
                w2(b).wait()
            dz2(t).wait()
            part = _dot_nt(dz_buf[t % 2], stage_d[b % 2])
            if b == 0:
                acc[r] = part
            else:
                acc[r] += part
        dh_out = [_SplitCopy(acc.at[r], dh_hbm.at[pl.ds(r * half, half), :], out_sem.at[r])
                  for r in range(2)]
        for cp in dh_out:
            cp.start()
        for cp in dh_out:
            cp.wait()
        for i in range(3):
            to_owner(i).wait_send()
        for i in (1, 2):
            to_owner(i).wait_recv()

    any_spec = pl.BlockSpec(memory_space=pl.ANY)
    return pl.pallas_call(
        body, name="in_proj_bwd",
        out_shape=[jax.ShapeDtypeStruct((S, D), F32), jax.ShapeDtypeStruct(slab, F32),
                   jax.ShapeDtypeStruct((4,) + slab, BF16), jax.ShapeDtypeStruct(slab, BF16),
                   jax.ShapeDtypeStruct((2,) + slab, BF16)],
        in_specs=[any_spec] * 3,
        out_specs=[any_spec, pl.BlockSpec(memory_space=pltpu.VMEM), any_spec, any_spec, any_spec],
        scratch_shapes=[pltpu.VMEM((S, D), BF16), pltpu.VMEM((2, half, SHARD_COLS), BF16),
                        pltpu.VMEM((2,) + slab, BF16), pltpu.VMEM(slab, BF16), pltpu.VMEM((3,) + slab, BF16),
                        pltpu.VMEM((2, half, D), F32),
                        pltpu.SemaphoreType.DMA((2,)), pltpu.SemaphoreType.DMA((2,)), pltpu.SemaphoreType.DMA,
                        pltpu.SemaphoreType.DMA, pltpu.SemaphoreType.DMA((2,)),
                        pltpu.SemaphoreType.DMA((4,)), pltpu.SemaphoreType.DMA((4,)),
                        pltpu.SemaphoreType.DMA((3,)), pltpu.SemaphoreType.DMA((3,))],
        compiler_params=_cp(),
    )(dz, h, w_in)


def _grad_x(x, norm_w, dh, dx2):
    tr = 256

    def body(x_ref, w_ref, dh_ref, dx2_ref, gx_ref, gnw_ref):
        @pl.when(pl.program_id(0) == 0)
        def _():
            gnw_ref[...] = jnp.zeros_like(gnw_ref)

        xv, dhv = x_ref[...], dh_ref[...]
        r = lax.rsqrt(jnp.mean(xv * xv, axis=-1, keepdims=True) + EPS)
        n = xv * r
        gnw_ref[...] += jnp.sum(dhv * n, axis=0, keepdims=True)
        dn = dhv * w_ref[...]
        gx_ref[...] = dx2_ref[...] + r * (dn - n * jnp.mean(dn * n, axis=-1, keepdims=True))

    row = pl.BlockSpec((tr, D), lambda i: (i, 0))
    vec = pl.BlockSpec((1, D), lambda i: (0, 0))
    return pl.pallas_call(
        body, name="grad_x", grid=(S // tr,),
        out_shape=[jax.ShapeDtypeStruct((S, D), F32), jax.ShapeDtypeStruct((1, D), F32)],
        in_specs=[row, vec, row, row], out_specs=[row, vec],
        compiler_params=_cp(("arbitrary",)),
    )(x, norm_w, dh, dx2)


def _rope_tables(positions):
    inv_freq = 10000.0 ** (-jnp.arange(0, 64, 2, dtype=F32) / 64)
    ang = positions.astype(F32)[:, None] * inv_freq[None, :]
    cos, sin = jnp.cos(ang), jnp.sin(ang)
    return jnp.tile(cos, (1, 4)), jnp.tile(jnp.concatenate([-sin, sin], axis=1), (1, 2))


def _local_step(x, positions, norm_w, lb_logits, hnw, fnw, target, w_in_shard, small_shards, core):
    cc, ss = _rope_tables(positions)
    lbv = jax.nn.sigmoid(lb_logits[0:1] - lb_logits[1:2])
    z, w_in, h = _in_proj_gather(x, norm_w, w_in_shard)
    o, o_a, states = _hgrn_fwd(z, lbv, hnw)
    ob, lse, o_bg, qsb, ksb, vsb, w_a, w_b, w_out = _attn_fwd(z, cc, ss, small_shards)
    dx2, dx2b, dz, do_a, do_bg, merged, dy_a, dy_b, tail_small = _tail(x, o_a, o_bg, z, target, w_a, w_b, w_out, fnw)
    g_out, gb_out = _tn_matmul(merged, dx2b, "grad_w_out")
    g_a, gb_a = _tn_matmul(o_a, dy_a, "grad_w_a")
    g_b, gb_b = _tn_matmul(o_bg, dy_b, "grad_w_b")
    grads, gb = (g_a, g_b, g_out), (gb_a, gb_b, gb_out)
    r1 = _exchange_sibling(GATHER_IDS, gb)
    pb = [_chip_partials(a, grads[i], r1[i], core) for i, a in enumerate(GATHER_IDS)]
    dz, glb, ghn, *r2 = _hgrn_bwd(z, o, do_a, states, lbv, hnw, pb, dz)
    dz = _attn_bwd(z, qsb, ksb, vsb, cc, ss, ob, lse, do_bg, dz)
    dh, g_chip_in, _, _, r2_in = _in_proj_bwd(dz, h, w_in)
    grad_x, gnw = _grad_x(x, norm_w, dh, dx2)
    ghn_row = jnp.pad(jnp.sum(ghn, axis=0), ((0, 0), (0, D - 128)))
    small = jnp.concatenate([gnw, glb, ghn_row, tail_small[0:2], jnp.zeros((3, D), F32)], axis=0)
    return grad_x, (g_chip_in, r2_in), grads, r1, r2, small


def kernel(x, positions, norm_w, w_in, lb_logits, hgrn_norm_w, w_branch_a, w_branch_b, w_out, final_norm_w, loss_target, m_norm_w, m_w_in, m_lb_logits, m_hgrn_norm_w, m_w_branch_a, m_w_branch_b, m_w_out, m_final_norm_w, v_norm_w, v_w_in, v_lb_logits, v_hgrn_norm_w, v_w_branch_a, v_w_branch_b, v_w_out, v_final_norm_w):
    ix, iy, ic = _mesh_pos()
    core = jnp.reshape(ic, (1,)).astype(jnp.int32)
    pos = jnp.stack([4 * ix + 2 * iy + ic, 2 * ix + iy]).astype(jnp.int32)

    shards = [w_in[0], w_branch_a[0], w_branch_b[0], w_out[0]]
    moments_m = [m_w_in[0], m_w_branch_a[0], m_w_branch_b[0], m_w_out[0]]
    moments_v = [v_w_in[0], v_w_branch_a[0], v_w_branch_b[0], v_w_out[0]]
    names = ("w_in", "w_a", "w_b", "w_out")
    ids = GATHER_IDS
    shards_b = _cast_shards(shards)

    fnw2 = final_norm_w.reshape(1, D)
    grad_x, (g_chip_in, r2_in), grads, r1, r2, small = _local_step(
        x[0], positions[0], norm_w, lb_logits, hgrn_norm_w, fnw2, loss_target[0], shards_b[0], shards_b[1:], core)

    gathered = _gather_small(small)
    big =[_reduce_own_and_update(shards[0], moments_m[0], moments_v[0], g_chip_in, r2_in)]
    big += [_reduce_and_update(a, shards[a], moments_m[a], moments_v[a], grads[i], r1[i], r2[i], pos)
            for i, a in enumerate(ids)]
    sm = _small_update(gathered, norm_w, lb_logits, hgrn_norm_w, fnw2,
                       (m_norm_w, m_lb_logits, m_hgrn_norm_w, m_final_norm_w.reshape(1, D),
                        v_norm_w, v_lb_logits, v_hgrn_norm_w, v_final_norm_w.reshape(1, D)))
    loss = sm[0][0, 0]
    outs = [loss, grad_x[None]]
    for kind in range(4):
        s_nw, s_lb, s_hn, s_fn = sm[1 + 4 * kind:5 + 4 * kind]
        outs += [s_nw, big[0][kind][None], s_lb, s_hn, big[1][kind][None], big[2][kind][None],
                 big[3][kind][None], s_fn.reshape(D)]
    return tuple(outs)
```

```python
import functools

import jax
import jax.numpy as jnp
from jax import lax
from jax.experimental import pallas as pl
from jax.experimental.pallas import tpu as pltpu

F32 = jnp.float32
BF16 = jnp.bfloat16
MESH = pl.DeviceIdType.MESH

S = 2048
D = 1024
NDEV = 8
HEADS = 8
CHUNK = 64
SUB = 16
HBLK = 256
ATT_PAD = 128
ATT_UNROLL = 16
COPY_PARTS = 4
EXP_CLAMP = 80.0
EPS = 1e-6
IN_COLS = 11264
SHARD_COLS = IN_COLS // NDEV
DZ_ATT = 4096
DZ_GATES = 9216
ATT_DILS = (1, 4, 16)
ATT_SCALE = 64 ** -0.5
LANES = 128

ADAM_LR, ADAM_B1, ADAM_B2, ADAM_EPS, ADAM_WD, ADAM_STEP = 0.001, 0.9, 0.999, 1e-08, 0.01, 10

VMEM_LIMIT = 56 * 1024 * 1024


def _cp(sem=None, **kw):
    return pltpu.CompilerParams(dimension_semantics=sem, vmem_limit_bytes=VMEM_LIMIT, **kw)


def _dot(a, b):
    return jnp.dot(a, b, preferred_element_type=F32)


def _dot_nt(a, b):
    return lax.dot_general(a, b, (((1,), (1,)), ((), ())), preferred_element_type=F32)


def _dot_tn(a, b):
    return lax.dot_general(a, b, (((0,), (0,)), ((), ())), preferred_element_type=F32)


def _split2(x):
    hi = x.astype(BF16)
    lo = (x - hi.astype(F32)).astype(BF16)
    return hi, lo


def _split3(x):
    hi = x.astype(BF16)
    r = x - hi.astype(F32)
    mid = r.astype(BF16)
    lo = (r - mid.astype(F32)).astype(BF16)
    return hi, mid, lo


def _dot_ones(ones_bf16, x):
    hi, mid, lo = _split3(x)
    return _dot(ones_bf16, hi) + _dot(ones_bf16, mid) + _dot(ones_bf16, lo)


def _silu(x):
    return x * jax.nn.sigmoid(x)


def _dsilu(x):
    s = jax.nn.sigmoid(x)
    return s * (1.0 + x * (1.0 - s))


def _mesh_pos():
    return lax.axis_index("x"), lax.axis_index("y"), lax.axis_index("c")


class _SplitCopy:
    def __init__(self, src, dst, sem):
        self.src, self.dst, self.sem = src, dst, sem

    def start(self):
        rows = self.src.shape[0] // COPY_PARTS
        for p in range(COPY_PARTS):
            chunk = pl.ds(p * rows, rows)
            pltpu.make_async_copy(self.src.at[chunk], self.dst.at[chunk], self.sem).start()

    def wait(self):
        pltpu.make_async_copy(self.src, self.dst, self.sem).wait()


def _shard_of(ref, a, d):
    if a == 0:
        return ref.at[:, pl.ds(pl.multiple_of(d * SHARD_COLS, LANES), SHARD_COLS)]
    if a == 2:
        return ref.at[:, pl.ds(pl.multiple_of(d * LANES, LANES), LANES)]
    return ref.at[pl.ds(pl.multiple_of(d * 128, 128), 128), :]


FULL_SHAPES = ((D, IN_COLS), (D, D), (512, D), (D, D))
SHARD_SHAPES = ((D, SHARD_COLS), (128, D), (512, 128), (128, D))


def _allgather_steps(ids, ins, outs, send_sems, recv_sems, local_sems):
    n = len(ids)
    x, y, c = _mesh_pos()
    me, sibling = (x, y, c), (x, y, 1 - c)
    chips = [(1 - x, y), (x, 1 - y), (1 - x, 1 - y)]

    def blk(a, p):
        return _shard_of(outs[a], ids[a], 4 * p[0] + 2 * p[1] + p[2])

    def copy(a, k, block, to, src=None):
        return pltpu.make_async_remote_copy(
            src_ref=blk(a, block) if src is None else src, dst_ref=blk(a, block),
            send_sem=send_sems.at[a * 7 + k], recv_sem=recv_sems.at[a * 7 + k],
            device_id=to, device_id_type=MESH)

    mine = [pltpu.make_async_copy(ins[a], blk(a, me), local_sems.at[a]) for a in range(n)]
    first = []
    for a in range(n):
        first += [copy(a, 1 + j, me, (*chip, c), src=ins[a]) for j, chip in enumerate(chips)]
    for a in range(n):
        first.append(copy(a, 0, me, sibling, src=ins[a]))
    passed = [copy(a, 4 + j, (*chip, c), sibling) for j, chip in enumerate(chips) for a in range(n)]

    def start():
        for cp in mine + first:
            cp.start()

    def middle():
        for j, chip in enumerate(chips):
            for a in range(n):
                copy(a, 1 + j, (*chip, c), me).wait_recv()
                passed[j * n + a].start()

    def end():
        for a in range(n):
            copy(a, 0, sibling, me).wait_recv()
        for j, chip in enumerate(chips):
            for a in range(n):
                copy(a, 4 + j, (*chip, 1 - c), me).wait_recv()
        for cp in first + passed:
            cp.wait_send()
        for cp in mine:
            cp.wait()

    return start, middle, end


def _in_proj_gather(x, norm_w, w_shard):
    half = S // 2
    slab = (D, SHARD_COLS)
    xt = 512

    def body(x_hbm, nw_ref, w_hbm, z_hbm, wfull_hbm, h_hbm, h_buf, land, zstage, xbuf,
             h_sem, own_sem, z_sem, wout_sem, x_sem, send_sems, recv_sems):
        x, y, c = _mesh_pos()
        sibling = (x, y, 1 - c)
        north = c == 1

        def chips_of(first_x):
            near = (jnp.where(first_x, 1 - x, x), jnp.where(first_x, y, 1 - y))
            far = (jnp.where(first_x, x, 1 - x), jnp.where(first_x, 1 - y, y))
            return [near, far, (1 - x, 1 - y)]

        mine, theirs = chips_of(north), chips_of(jnp.logical_not(north))

        def dev(chip, core):
            return 4 * chip[0] + 2 * chip[1] + core

        block_of = ([dev((x, y), c), dev((x, y), 1 - c)] + [dev(q, c) for q in mine]
                    + [dev(q, 1 - c) for q in theirs])

        def cols(d):
            if isinstance(d, int):
                return pl.ds(d * SHARD_COLS, SHARD_COLS)
            return pl.ds(pl.multiple_of(d * SHARD_COLS, LANES), SHARD_COLS)

        def send(k, src, dst_slot, to):
            return pltpu.make_async_remote_copy(
                src_ref=src, dst_ref=land.at[dst_slot], send_sem=send_sems.at[k], recv_sem=recv_sems.at[k],
                device_id=to, device_id_type=MESH)

        def to_sibling():
            return send(0, w_hbm, 1, sibling)

        def to_chip(j):
            if j == 2:
                return send(3, land.at[2], 4, (*mine[1], c))
            return send(1 + j, w_hbm, 2 + j, (*mine[j], c))

        def pass_on(j):
            return send(4 + j, land.at[2 + j], 5 + j, sibling)

        def x_tile(i):
            return _SplitCopy(x_hbm.at[pl.ds(i * xt, xt), :], xbuf.at[i % 2], x_sem.at[i % 2])

        own = _SplitCopy(w_hbm, land.at[0], own_sem)
        own.start()
        x_tile(0).start()
        to_sibling().start()
        to_chip(0).start()
        for i in range(S // xt):
            if i + 1 < S // xt:
                x_tile(i + 1).start()
            x_tile(i).wait()
            xv = xbuf[i % 2]
            r = lax.rsqrt(jnp.mean(xv * xv, axis=-1, keepdims=True) + EPS)
            h_buf[i * xt:(i + 1) * xt, :] = (xv * r * nw_ref[...]).astype(BF16)
        h_out = _SplitCopy(h_buf, h_hbm, h_sem)
        h_out.start()
        own.wait()

        def multiply(slot, n_done):
            d = block_of[slot]
            out = _SplitCopy(land.at[slot], wfull_hbm.at[:, cols(d)], wout_sem.at[slot])
            out.start()
            for r in range(2):
                rows = pl.ds(r * half, half)
                zc = _SplitCopy(zstage.at[r], z_hbm.at[rows, cols(d)], z_sem.at[r])
                if n_done > 0:
                    zc.wait()
                zstage[r] = _dot(h_buf[r * half:(r + 1) * half, :], land[slot])
                zc.start()
            return out

        outs = [multiply(0, 0)]
        to_sibling().wait_recv()
        outs.append(multiply(1, 1))
        done = 2
        for j in range(3):
            to_chip(j).wait_recv()
            pass_on(j).start()
            to_chip(j).wait_send()
            if j < 2:
                to_chip(j + 1).start()
            outs.append(multiply(2 + j, done))
            pass_on(j).wait_recv()
            outs.append(multiply(5 + j, done + 1))
            done += 2
        for r in range(2):
            _SplitCopy(zstage.at[r], z_hbm.at[pl.ds(r * half, half), cols(0)], z_sem.at[r]).wait()
        for out in outs:
            out.wait()
        h_out.wait()
        to_sibling().wait_send()
        for j in range(3):
            pass_on(j).wait_send()

    any_spec = pl.BlockSpec(memory_space=pl.ANY)
    return pl.pallas_call(
        body, name="in_proj_gather",
        out_shape=[jax.ShapeDtypeStruct((S, IN_COLS), F32), jax.ShapeDtypeStruct((D, IN_COLS), BF16),
                   jax.ShapeDtypeStruct((S, D), BF16)],
        in_specs=[any_spec, pl.BlockSpec(memory_space=pltpu.VMEM), any_spec], out_specs=[any_spec] * 3,
        scratch_shapes=[pltpu.VMEM((S, D), BF16), pltpu.VMEM((8,) + slab, BF16), pltpu.VMEM((2, half, SHARD_COLS), F32),
                        pltpu.VMEM((2, xt, D), F32),
                        pltpu.SemaphoreType.DMA, pltpu.SemaphoreType.DMA, pltpu.SemaphoreType.DMA((2,)),
                        pltpu.SemaphoreType.DMA((8,)), pltpu.SemaphoreType.DMA((2,)),
                        pltpu.SemaphoreType.DMA((7,)), pltpu.SemaphoreType.DMA((7,))],
        compiler_params=_cp(),
    )(x, norm_w, w_shard)


def _exchange_sibling(ids, gb):
    n = len(gb)

    def body(*refs):
        ins, outs = refs[:n], refs[n:2 * n]
        send_sems, recv_sems = refs[2 * n:]
        x, y, c = _mesh_pos()
        sibling = (x, y, 1 - c)
        copies = []
        for i, a in enumerate(ids):
            for q in range(4):
                copies.append(pltpu.make_async_remote_copy(
                    src_ref=_shard_of(ins[i], a, 2 * q + (1 - c)), dst_ref=outs[i].at[q],
                    send_sem=send_sems.at[i * 4 + q], recv_sem=recv_sems.at[i * 4 + q],
                    device_id=sibling, device_id_type=MESH))
        for cp in copies:
            cp.start()
        for cp in copies:
            cp.wait()

    any_spec = pl.BlockSpec(memory_space=pl.ANY)
    return pl.pallas_call(
        body, name="grads_to_sibling",
        out_shape=[jax.ShapeDtypeStruct((4,) + SHARD_SHAPES[a], BF16) for a in ids],
        in_specs=[any_spec] * n, out_specs=[any_spec] * n,
        scratch_shapes=[pltpu.SemaphoreType.DMA((4 * n,)), pltpu.SemaphoreType.DMA((4 * n,))],
    )(*gb)


def _exchange_chips_steps(ins, outs, send_sems, recv_sems):
    x, y, c = _mesh_pos()
    chips = [(1 - x, y), (x, 1 - y), (1 - x, 1 - y)]
    copies = []
    for a in range(len(ins)):
        for k, chip in enumerate(chips):
            copies.append(pltpu.make_async_remote_copy(
                src_ref=ins[a].at[2 * chip[0] + chip[1]], dst_ref=outs[a].at[k],
                send_sem=send_sems.at[a * 3 + k], recv_sem=recv_sems.at[a * 3 + k],
                device_id=(*chip, c), device_id_type=MESH))

    def start():
        for cp in copies:
            cp.start()

    def end():
        for cp in copies:
            cp.wait()

    return start, end


def _gather_small_steps(small_ref, small_out, ssend, srecv, local_sem):
    x, y, c = _mesh_pos()
    me = 4 * x + 2 * y + c
    copies = []
    for r in range(1, NDEV):
        peer = (1 - x if r & 4 else x, 1 - y if r & 2 else y, 1 - c if r & 1 else c)
        copies.append(pltpu.make_async_remote_copy(
            src_ref=small_ref, dst_ref=small_out.at[me],
            send_sem=ssend.at[r - 1], recv_sem=srecv.at[r - 1],
            device_id=peer, device_id_type=MESH))
    own = pltpu.make_async_copy(small_ref, small_out.at[me], local_sem)

    def start():
        own.start()
        for cp in copies:
            cp.start()

    def end():
        for cp in copies:
            cp.wait()
        own.wait()

    return start, end


def _gather_small(small):
    def body(small_ref, small_out, ssend, srecv, local_sem):
        start, end = _gather_small_steps(small_ref, small_out, ssend, srecv, local_sem)
        start()
        end()

    any_spec = pl.BlockSpec(memory_space=pl.ANY)
    return pl.pallas_call(
        body, name="gather_small",
        out_shape=jax.ShapeDtypeStruct((NDEV,) + small.shape, F32),
        in_specs=[any_spec], out_specs=any_spec,
        scratch_shapes=[pltpu.SemaphoreType.DMA((NDEV - 1,)), pltpu.SemaphoreType.DMA((NDEV - 1,)),
                        pltpu.SemaphoreType.DMA],
    )(small)


def _shard_tiles(a):
    rows, cols = SHARD_SHAPES[a]
    tr = min(rows, 256)
    return (tr, cols), rows // tr


def _full_index(a, d, i):
    (tr, _), nt = _shard_tiles(a)
    if a in (0, 2):
        return (i, d)
    return (d * nt + i, 0)


def _cast_shards(shards):
    n = len(shards)

    def body(*refs):
        for x_ref, o_ref in zip(refs[:n], refs[n:]):
            o_ref[...] = x_ref[...].astype(BF16)

    return pl.pallas_call(
        body, name="cast_shards", out_shape=[jax.ShapeDtypeStruct(w.shape, BF16) for w in shards],
        compiler_params=_cp(),
    )(*shards)


def _chip_partials(a, g_full, r1, core):
    tile, nt = _shard_tiles(a)

    def body(c_ref, g_ref, r_ref, o_ref):
        o_ref[0] = (g_ref[...] + r_ref[0].astype(F32)).astype(BF16)

    grid_spec = pltpu.PrefetchScalarGridSpec(
        num_scalar_prefetch=1, grid=(4, nt),
        in_specs=[pl.BlockSpec(tile, lambda q, i, c: _full_index(a, 2 * q + c[0], i)),
                  pl.BlockSpec((1,) + tile, lambda q, i, c: (q, i, 0))],
        out_specs=pl.BlockSpec((1,) + tile, lambda q, i, c: (q, i, 0)))
    return pl.pallas_call(
        body, name=f"chip_partials_{a}", grid_spec=grid_spec,
        out_shape=jax.ShapeDtypeStruct((4,) + SHARD_SHAPES[a], BF16),
        compiler_params=_cp(("parallel", "parallel")),
    )(core, g_full, r1)


def _adam(w, g, m, v):
    m = ADAM_B1 * m + (1.0 - ADAM_B1) * g
    v = ADAM_B2 * v + (1.0 - ADAM_B2) * (g * g)
    m_hat = m / (1.0 - ADAM_B1 ** ADAM_STEP)
    v_hat = v / (1.0 - ADAM_B2 ** ADAM_STEP)
    delta = -ADAM_LR * (m_hat / (jnp.sqrt(v_hat) + ADAM_EPS) + ADAM_WD * w)
    return delta, m, v


def _reduce_and_update(a, w, m, v, g_full, r1, r2, pos):
    tile, nt = _shard_tiles(a)

    def body(p_ref, w_ref, m_ref, v_ref, g_ref, r1_ref, r2_ref, go_ref, do_ref, mo_ref, vo_ref):
        g = g_ref[...] + r1_ref[0].astype(F32)
        g = g + r2_ref[0].astype(F32)
        g = g + r2_ref[1].astype(F32)
        g = g + r2_ref[2].astype(F32)
        delta, m_new, v_new = _adam(w_ref[...], g, m_ref[...], v_ref[...])
        go_ref[...] = g
        do_ref[...] = delta
        mo_ref[...] = m_new
        vo_ref[...] = v_new

    own = pl.BlockSpec(tile, lambda i, p: (i, 0))
    grid_spec = pltpu.PrefetchScalarGridSpec(
        num_scalar_prefetch=1, grid=(nt,),
        in_specs=[own, own, own,
                  pl.BlockSpec(tile, lambda i, p: _full_index(a, p[0], i)),
                  pl.BlockSpec((1,) + tile, lambda i, p: (p[1], i, 0)),
                  pl.BlockSpec((3,) + tile, lambda i, p: (0, i, 0))],
        out_specs=[own] * 4)
    shp = jax.ShapeDtypeStruct(w.shape, F32)
    return pl.pallas_call(
        body, name=f"reduce_update_{a}", grid_spec=grid_spec, out_shape=[shp] * 4,
        compiler_params=_cp(("parallel",)),
    )(pos, w, m, v, g_full, r1, r2)


def _reduce_own_and_update(w, m, v, g_chip, r2):
    tile, nt = _shard_tiles(0)

    def body(w_ref, m_ref, v_ref, g_ref, r2_ref, go_ref, do_ref, mo_ref, vo_ref):
        g = g_ref[...] + r2_ref[0].astype(F32)
        g = g + r2_ref[1].astype(F32)
        delta, m_new, v_new = _adam(w_ref[...], g, m_ref[...], v_ref[...])
        go_ref[...] = g
        do_ref[...] = delta
        mo_ref[...] = m_new
        vo_ref[...] = v_new

    own = pl.BlockSpec(tile, lambda i: (i, 0))
    shp = jax.ShapeDtypeStruct(w.shape, F32)
    return pl.pallas_call(
        body, name="reduce_update_0", grid=(nt,), out_shape=[shp] * 4,
        in_specs=[own, own, own, own, pl.BlockSpec((2,) + tile, lambda i: (0, i, 0))], out_specs=[own] * 4,
        compiler_params=_cp(("parallel",)),
    )(w, m, v, g_chip, r2)


def _small_update(gathered, norm_w, lb_logits, hnw, fnw, moments):
    m_nw, m_lb, m_hn, m_fn, v_nw, v_lb, v_hn, v_fn = moments

    def body(g_ref, nw, lb, hn, fn, mnw, mlb, mhn, mfn, vnw, vlb, vhn, vfn,
             loss_o, g_nw, g_lb, g_hn, g_fn, d_nw, d_lb, d_hn, d_fn,
             mo_nw, mo_lb, mo_hn, mo_fn, vo_nw, vo_lb, vo_hn, vo_fn):
        tot = g_ref[0]
        for d in range(1, NDEV):
            tot = tot + g_ref[d]
        loss_o[...] = tot[4:5, 0:LANES]
        logits = lb[...]
        lbv = jax.nn.sigmoid(logits[0:1] - logits[1:2])
        chain = tot[1:2] * lbv * (1.0 - lbv)
        grads = (tot[0:1], jnp.concatenate([chain, -chain], axis=0), tot[2:3, 0:LANES], tot[3:4])
        outs = ((nw, mnw, vnw, g_nw, d_nw, mo_nw, vo_nw), (lb, mlb, vlb, g_lb, d_lb, mo_lb, vo_lb),
                (hn, mhn, vhn, g_hn, d_hn, mo_hn, vo_hn), (fn, mfn, vfn, g_fn, d_fn, mo_fn, vo_fn))
        for g, (w_r, m_r, v_r, g_o, d_o, m_o, v_o) in zip(grads, outs):
            delta, m_new, v_new = _adam(w_r[...], g, m_r[...], v_r[...])
            g_o[...] = g
            d_o[...] = delta
            m_o[...] = m_new
            v_o[...] = v_new

    shapes = [norm_w.shape, lb_logits.shape, hnw.shape, fnw.shape]
    out_shape = [jax.ShapeDtypeStruct((1, LANES), F32)] + [jax.ShapeDtypeStruct(s, F32) for s in shapes] * 4
    return pl.pallas_call(body, name="small_update", out_shape=out_shape, compiler_params=_cp())(
        gathered, norm_w, lb_logits, hnw, fnw, m_nw, m_lb, m_hn, m_fn, v_nw, v_lb, v_hn, v_fn)


def _block_tri(n, block, upper=False):
    r = lax.broadcasted_iota(jnp.int32, (n, n), 0)
    c = lax.broadcasted_iota(jnp.int32, (n, n), 1)
    keep = (c >= r) if upper else (c <= r)
    return jnp.where(keep & ((r // block) == (c // block)), 1.0, 0.0).astype(BF16)


def _tril_mask(n):
    r = lax.broadcasted_iota(jnp.int32, (n, n), 0)
    c = lax.broadcasted_iota(jnp.int32, (n, n), 1)
    return c <= r


def _chunk_scores(q, k, b, bex, r0, mask):
    parts, qs_l, ks_l, ek_l, eq_l = [], [], [], [], []
    for i in range(CHUNK // SUB):
        ri = slice(r0 + SUB * i, r0 + SUB * (i + 1))
        seen = slice(r0, r0 + SUB * (i + 1))
        base = bex[r0 + SUB * i:r0 + SUB * i + 1]
        eq = jnp.exp(b[ri] - base)
        ek = jnp.exp(jnp.minimum(base - b[seen], EXP_CLAMP))
        ks = k[seen] * ek
        if i + 1 < CHUNK // SUB:
            rest = jnp.zeros((CHUNK - SUB * (i + 1), 128), F32)
            ek, ks = jnp.concatenate([ek, rest], axis=0), jnp.concatenate([ks, rest], axis=0)
        qs = q[ri] * eq
        parts.append(_dot_nt(qs.astype(BF16), ks.astype(BF16)))
        qs_l.append(qs)
        ks_l.append(ks)
        ek_l.append(ek)
        eq_l.append(eq)
    return jnp.where(mask, jnp.concatenate(parts, axis=0), 0.0), qs_l, ks_l, ek_l, eq_l


def _hgrn_cols(hq, hf, hi, lb):
    sg = jax.nn.sigmoid(hf)
    f = lb + (1.0 - lb) * sg
    g = jnp.log(f)
    b = _dot_ones(_block_tri(HBLK, CHUNK), g)
    return _silu(hq), 1.0 - f, g, hi, sg, f, b


GATHER_IDS = (1, 2, 3)


def _hgrn_fwd(z, lbv, hnw):
    ntb, nch = S // HBLK, HBLK // CHUNK

    def body(hq_ref, hf_ref, hi_ref, hg_ref, lb_ref, hnw_ref, o_ref, oa_ref, st_ref, state):
        @pl.when(pl.program_id(0) == 0)
        def _():
            state[...] = jnp.zeros_like(state)

        mask = _tril_mask(CHUNK)
        hg = hg_ref[...]
        w = hnw_ref[...]
        for h in range(HEADS):
            cols = slice(128 * h, 128 * h + 128)
            q, k, g, v, _, _, b = _hgrn_cols(hq_ref[:, cols], hf_ref[:, cols], hi_ref[:, cols], lb_ref[:, cols])
            bex = b - g
            eb = jnp.exp(b)
            st = state[h]
            outs = []
            for c in range(nch):
                r0 = c * CHUNK
                rows = slice(r0, r0 + CHUNK)
                a = _chunk_scores(q, k, b, bex, r0, mask)[0]
                vb = v[rows].astype(BF16)
                b_last = b[r0 + CHUNK - 1:r0 + CHUNK]
                qe = (q[rows] * eb[rows]).astype(BF16)
                outs.append(_dot(a.astype(BF16), vb) + _dot_nt(qe, st.astype(BF16)))
                st_ref[h, c] = st
                ke = (k[rows] * jnp.exp(b_last - b[rows])).astype(BF16)
                st = st * jnp.exp(b_last) + _dot_tn(vb, ke)
            state[h] = st
            o = jnp.concatenate(outs, axis=0)
            o_ref[:, cols] = o
            r = lax.rsqrt(jnp.mean(o * o, axis=-1, keepdims=True) + EPS)
            oa_ref[:, cols] = (o * r * w * _silu(hg[:, cols])).astype(BF16)

    def zcol(j):
        return pl.BlockSpec((HBLK, D), lambda t: (t, j))

    out_blk = pl.BlockSpec((HBLK, D), lambda t: (t, 0))
    return pl.pallas_call(
        body, name="hgrn_fwd", grid=(ntb,),
        out_shape=[jax.ShapeDtypeStruct((S, D), F32), jax.ShapeDtypeStruct((S, D), BF16),
                   jax.ShapeDtypeStruct((HEADS, S // CHUNK, 128, 128), F32)],
        in_specs=[zcol(0), zcol(1), zcol(2), zcol(3),
                  pl.BlockSpec((1, D), lambda t: (0, 0)), pl.BlockSpec((1, 128), lambda t: (0, 0))],
        out_specs=[out_blk, out_blk, pl.BlockSpec((HEADS, nch, 128, 128), lambda t: (0, t, 0, 0))],
        scratch_shapes=[pltpu.VMEM((HEADS, 128, 128), F32)],
        compiler_params=_cp(("arbitrary",)),
    )(z, z, z, z, lbv, hnw)


def _half_mask():
    lane = lax.broadcasted_iota(jnp.int32, (1, LANES), 1)
    return (lane % 64) < 32


def _rope(t, cc, ss, first_half):
    partner = jnp.where(first_half, pltpu.roll(t, 96, 1), pltpu.roll(t, 32, 1))
    return t * cc + partner * ss


def _attn_masks():
    i = lax.broadcasted_iota(jnp.int32, (128, 128), 0)
    j = lax.broadcasted_iota(jnp.int32, (128, 128), 1)
    return j >= i, j <= i


def _to_residues_dyn(g, dst, src, row0=0, dtype=None):
    for gi, dil in enumerate((1, 4, 16)):
        m = S // dil

        @pl.when(g == gi)
        def _(dil=dil, m=m):
            for r in range(dil):
                v = src[...] if dil == 1 else src[pl.ds(r, m, stride=dil), :]
                if dtype is not None:
                    v = v.astype(dtype)
                dst[row0 + r * m:row0 + (r + 1) * m, 0:LANES] = v


def _from_residues_dyn(g, dst, src, row0=0):
    for gi, dil in enumerate((1, 4, 16)):
        m = S // dil

        @pl.when(g == gi)
        def _(dil=dil, m=m):
            for r in range(dil):
                v = src[row0 + r * m:row0 + (r + 1) * m, :]
                if dil == 1:
                    dst[...] = v
                else:
                    dst[pl.ds(r, m, stride=dil), :] = v


def _group_blocks(g):
    return jnp.where(g == 0, 16, jnp.where(g == 1, 4, 1))


def _attn_in_specs(extra):
    def zcol(off):
        return pl.BlockSpec((S, LANES), lambda p, g: (0, off + 4 * g + p))

    per_pair = pl.BlockSpec((S, LANES), lambda p, g: (0, p))
    const = pl.BlockSpec((S, LANES), lambda p, g: (0, 0))
    return [zcol(32), zcol(44), zcol(56), pl.BlockSpec((S, LANES), lambda p, g: (0, 68 + p)), const, const] + [per_pair] * extra


def _attn_fwd(z, cc, ss, shards):
    def body(q_ref, k_ref, v_ref, ag_ref, cc_ref, ss_ref, s0, s1, s2,
             ob_ref, lse_ref, obg_ref, qsb_ref, ksb_ref, vsb_ref, f0, f1, f2,
             tmp, qs, ks, vx, og, mg, lg, o_t, m_t, l_t, o_acc, m_acc, l_acc, send_sems, recv_sems, local_sems):
        g = pl.program_id(1)
        step = pl.program_id(0) * 3 + g
        start, middle, end = _allgather_steps(GATHER_IDS, (s0, s1, s2), (f0, f1, f2), send_sems, recv_sems, local_sems)
        pl.when(step == 0)(start)
        pl.when(step == 9)(middle)
        first_half = _half_mask()
        prev_ok, cur_ok = _attn_masks()
        lane = lax.broadcasted_iota(jnp.int32, (1, LANES), 1)
        heads = (lane < 64, lane >= 64)
        nblk = _group_blocks(g)

        @pl.when(g == 0)
        def _():
            ks[0:ATT_PAD, :] = jnp.zeros((ATT_PAD, LANES), BF16)
            vx[0:ATT_PAD, 0:LANES] = jnp.zeros((ATT_PAD, LANES), BF16)
            vx[:, LANES:2 * LANES] = jnp.ones((ATT_PAD + S, LANES), BF16)

        tmp[...] = _rope(q_ref[...], cc_ref[...], ss_ref[...], first_half) * ATT_SCALE
        _to_residues_dyn(g, qs, tmp)
        tmp[...] = _rope(k_ref[...], cc_ref[...], ss_ref[...], first_half)
        _to_residues_dyn(g, ks, tmp, ATT_PAD, BF16)
        _to_residues_dyn(g, vx, v_ref, ATT_PAD, BF16)

        def unit(u, carry):
            start = pl.multiple_of(u * 128, 128)
            cur = pl.ds(start, 128)
            pm = prev_ok & ((u & (nblk - 1)) != 0)
            qu = qs[cur, :]
            kcat = ks[pl.ds(start, 256), :]
            vext = vx[pl.ds(start, 256), :]
            o_u = m_u = l_u = None
            for hh in range(2):
                s = _dot_nt(jnp.where(heads[hh], qu, 0.0).astype(BF16), kcat)
                sp = jnp.where(pm, s[:, 0:128], -jnp.inf)
                sc = jnp.where(cur_ok, s[:, 128:256], -jnp.inf)
                m = jnp.max(jnp.maximum(sp, sc), axis=-1, keepdims=True)
                p = jnp.concatenate([jnp.exp(sp - m), jnp.exp(sc - m)], axis=1).astype(BF16)
                ol = _dot(p, vext)
                mb = jnp.broadcast_to(m, (128, LANES))
                if hh == 0:
                    o_u, l_u, m_u = ol[:, 0:128], ol[:, 128:256], mb
                else:
                    o_u = jnp.where(heads[1], ol[:, 0:128], o_u)
                    l_u = jnp.where(heads[1], ol[:, 128:256], l_u)
                    m_u = jnp.where(heads[1], mb, m_u)
            og[cur, :] = o_u
            mg[cur, :] = m_u
            lg[cur, :] = l_u
            return carry

        lax.fori_loop(0, 16, unit, 0, unroll=16)
        qsb_ref[0] = qs[...].astype(BF16)
        ksb_ref[0] = ks[...]
        vsb_ref[0] = vx[:, 0:LANES]
        _from_residues_dyn(g, o_t, og)
        _from_residues_dyn(g, m_t, mg)
        _from_residues_dyn(g, l_t, lg)

        @pl.when(g == 0)
        def _():
            o_acc[...] = o_t[...]
            m_acc[...] = m_t[...]
            l_acc[...] = l_t[...]

        @pl.when(g > 0)
        def _():
            m_new = jnp.maximum(m_acc[...], m_t[...])
            wa, wb = jnp.exp(m_acc[...] - m_new), jnp.exp(m_t[...] - m_new)
            o_acc[...] = o_acc[...] * wa + o_t[...] * wb
            l_acc[...] = l_acc[...] * wa + l_t[...] * wb
            m_acc[...] = m_new

        @pl.when(g == 2)
        def _():
            ob = o_acc[...] / l_acc[...]
            ob_ref[...] = ob
            lse_ref[...] = m_acc[...] + jnp.log(l_acc[...])
            obg_ref[...] = (ob * _silu(ag_ref[...])).astype(BF16)

        pl.when(step == 11)(end)

    n = len(GATHER_IDS)
    any_spec = pl.BlockSpec(memory_space=pl.ANY)
    blk = pl.BlockSpec((S, LANES), lambda p, g: (0, p))
    buf = pltpu.VMEM((S, LANES), F32)
    return pl.pallas_call(
        body, name="attn_fwd", grid=(4, 3),
        out_shape=[jax.ShapeDtypeStruct((S, 512), F32), jax.ShapeDtypeStruct((S, 512), F32),
                   jax.ShapeDtypeStruct((S, 512), BF16), jax.ShapeDtypeStruct((3, S, 512), BF16),
                   jax.ShapeDtypeStruct((3, ATT_PAD + S, 512), BF16), jax.ShapeDtypeStruct((3, ATT_PAD + S, 512), BF16)]
        + [jax.ShapeDtypeStruct(FULL_SHAPES[a], BF16) for a in GATHER_IDS],
        in_specs=_attn_in_specs(0) + [any_spec] * n,
        out_specs=[blk, blk, blk, pl.BlockSpec((1, S, LANES), lambda p, g: (g, 0, p)),
                   pl.BlockSpec((1, ATT_PAD + S, LANES), lambda p, g: (g, 0, p)),
                   pl.BlockSpec((1, ATT_PAD + S, LANES), lambda p, g: (g, 0, p))] + [any_spec] * n,
        scratch_shapes=[buf, buf, pltpu.VMEM((ATT_PAD + S, LANES), BF16), pltpu.VMEM((ATT_PAD + S, 2 * LANES), BF16)]
        + [buf] * 9 + [pltpu.SemaphoreType.DMA((7 * n,)), pltpu.SemaphoreType.DMA((7 * n,)), pltpu.SemaphoreType.DMA((n,))],
        compiler_params=_cp(("arbitrary", "arbitrary")),
    )(z, z, z, z, cc, ss, *shards)


def _tail(x, o_a, o_bg, z, target, w_a, w_b, w_out, fnw):
    tm = 256

    def body(x_ref, oa_ref, ob_ref, gpa_ref, gpb_ref, t_ref, wa_ref, wb_ref, wo_ref, fnw_ref,
             dx2_ref, dx2b_ref, dz_hbm, doa_ref, dob_ref, mg_ref, dya_ref, dyb_ref, small_ref, dgp, dgp_sem):
        step = pl.program_id(0)
        slot = step % 2

        def dgp_copy(at_step, at_slot):
            return pltpu.make_async_copy(
                dgp.at[at_slot], dz_hbm.at[pl.ds(pl.multiple_of(at_step * tm, tm), tm), pl.ds(DZ_GATES, 2 * D)],
                dgp_sem.at[at_slot])

        @pl.when(step == 0)
        def _():
            small_ref[...] = jnp.zeros_like(small_ref)

        @pl.when(step >= 2)
        def _():
            dgp_copy(step - 2, slot).wait()

        wa, wb, wo = wa_ref[...], wb_ref[...], wo_ref[...]
        y_a = _dot(oa_ref[...], wa)
        y_b = _dot(ob_ref[...], wb)
        ga = jax.nn.sigmoid(gpa_ref[...])
        gb = jax.nn.sigmoid(gpb_ref[...])
        merged = (ga * y_a + gb * y_b).astype(BF16)
        x2 = x_ref[...] + _dot(merged, wo)
        r2 = lax.rsqrt(jnp.mean(x2 * x2, axis=-1, keepdims=True) + EPS)
        n2 = x2 * r2
        fw = fnw_ref[...]
        err = n2 * fw - t_ref[...]
        loss = 0.5 * jnp.sum(jnp.sum(err * err, axis=-1, keepdims=True), axis=0, keepdims=True) / D
        dy = err * (1.0 / D)
        g_fnw = jnp.sum(dy * n2, axis=0, keepdims=True)
        dn = dy * fw
        dx2 = r2 * (dn - n2 * jnp.mean(dn * n2, axis=-1, keepdims=True))
        dx2b = dx2.astype(BF16)
        dmerged = _dot_nt(dx2b, wo)
        dy_a = (dmerged * ga).astype(BF16)
        dy_b = (dmerged * gb).astype(BF16)
        dx2_ref[...] = dx2
        dx2b_ref[...] = dx2b
        dgp[slot, :, 0:D] = (dmerged * y_a * ga * (1.0 - ga)).astype(BF16)
        dgp[slot, :, D:2 * D] = (dmerged * y_b * gb * (1.0 - gb)).astype(BF16)
        dgp_copy(step, slot).start()
        doa_ref[...] = _dot_nt(dy_a, wa)
        dob_ref[...] = _dot_nt(dy_b, wb)
        mg_ref[...] = merged
        dya_ref[...] = dy_a
        dyb_ref[...] = dy_b
        small_ref[0:1, :] += g_fnw
        small_ref[1:2, :] += jnp.broadcast_to(loss, (1, D))

        @pl.when(step == S // tm - 1)
        def _():
            dgp_copy(step - 1, 1 - slot).wait()
            dgp_copy(step, slot).wait()

    def rows(cols, off=0):
        return pl.BlockSpec((tm, cols), lambda i: (i, off))

    def whole(shape):
        return pl.BlockSpec(shape, lambda i: (0, 0))

    return pl.pallas_call(
        body, name="tail", grid=(S // tm,),
        out_shape=[jax.ShapeDtypeStruct((S, D), F32), jax.ShapeDtypeStruct((S, D), BF16),
                   jax.ShapeDtypeStruct((S, IN_COLS), BF16), jax.ShapeDtypeStruct((S, D), F32),
                   jax.ShapeDtypeStruct((S, 512), F32), jax.ShapeDtypeStruct((S, D), BF16),
                   jax.ShapeDtypeStruct((S, D), BF16), jax.ShapeDtypeStruct((S, D), BF16),
                   jax.ShapeDtypeStruct((8, D), F32)],
        in_specs=[rows(D), rows(D), rows(512), rows(D, 9), rows(D, 10), rows(D),
                  whole((D, D)), whole((512, D)), whole((D, D)), whole((1, D))],
        out_specs=[rows(D), rows(D), pl.BlockSpec(memory_space=pl.ANY), rows(D), rows(512), rows(D), rows(D),
                   rows(D), whole((8, D))],
        scratch_shapes=[pltpu.VMEM((2, tm, 2 * D), BF16), pltpu.SemaphoreType.DMA((2,))],
        compiler_params=_cp(("arbitrary",)),
    )(x, o_a, o_bg, z, z, target, w_a, w_b, w_out, fnw)


def _tn_matmul(a, b, name):
    m, n = a.shape[1], b.shape[1]
    tn = 512

    def body(a_ref, b_ref, o_ref, ob_ref):
        acc = _dot_tn(a_ref[...], b_ref[...])
        o_ref[...] = acc
        ob_ref[...] = acc.astype(BF16)

    out_blk = pl.BlockSpec((m, tn), lambda j: (0, j))
    return pl.pallas_call(
        body, name=name, grid=(n // tn,),
        out_shape=[jax.ShapeDtypeStruct((m, n), F32), jax.ShapeDtypeStruct((m, n), BF16)],
        in_specs=[pl.BlockSpec((S, m), lambda j: (0, 0)), pl.BlockSpec((S, tn), lambda j: (0, j))],
        out_specs=[out_blk, out_blk],
        compiler_params=_cp(("parallel",)),
    )(a, b)


def _hgrn_bwd(z, o, do_a, states, lbv, hnw, partials, dz):
    ntb, nch = S // HBLK, HBLK // CHUNK
    n = len(GATHER_IDS)

    def body(hq_ref, hf_ref, hi_ref, hg_ref, o_ref, doa_ref, st_ref, lb_ref, hnw_ref, p0, p1, p2, dz_in,
             dz_ref, glb_ref, ghn_ref, e0, e1, e2, dstate, send_sems, recv_sems):
        dhq_ref, dhf_ref, dhi_ref, dhg_ref = (dz_ref.at[:, pl.ds(j * D, D)] for j in range(4))
        start, end = _exchange_chips_steps((p0, p1, p2), (e0, e1, e2), send_sems, recv_sems)

        @pl.when(pl.program_id(0) == 0)
        def _():
            dstate[...] = jnp.zeros_like(dstate)
            glb_ref[...] = jnp.zeros_like(glb_ref)
            ghn_ref[...] = jnp.zeros_like(ghn_ref)
            start()

        w = hnw_ref[...]
        mask = _tril_mask(CHUNK)
        upper = _block_tri(CHUNK, CHUNK, upper=True)
        for h in range(HEADS):
            cols = slice(128 * h, 128 * h + 128)
            hq, hg, lb = hq_ref[:, cols], hg_ref[:, cols], lb_ref[:, cols]
            q, k, g, v, sg, f, b = _hgrn_cols(hq, hf_ref[:, cols], hi_ref[:, cols], lb)
            bex = b - g
            eb = jnp.exp(b)
            ov, doa = o_ref[:, cols], doa_ref[:, cols]
            r = lax.rsqrt(jnp.mean(ov * ov, axis=-1, keepdims=True) + EPS)
            n = ov * r
            sil = _silu(hg)
            dhg_ref[:, cols] = (doa * n * w * _dsilu(hg)).astype(BF16)
            ghn_ref[h] += jnp.sum(doa * sil * n, axis=0, keepdims=True)
            dn = doa * sil * w
            do = r * (dn - n * jnp.mean(dn * n, axis=-1, keepdims=True))

            dst = dstate[h]
            dq_l, dk_l, dv_l, dg_l = [None] * nch, [None] * nch, [None] * nch, [None] * nch
            for c in reversed(range(nch)):
                r0 = c * CHUNK
                rows = slice(r0, r0 + CHUNK)
                st = st_ref[h, c]
                bc, kc, qc = b[rows], k[rows], q[rows]
                vb, dob = v[rows].astype(BF16), do[rows].astype(BF16)
                b_last = bc[CHUNK - 1:CHUNK]
                e_last = jnp.exp(b_last)
                ekl = jnp.exp(b_last - bc)
                dstb = dst.astype(BF16)
                a, qs_l, ks_l, ek_l, eq_l = _chunk_scores(q, k, b, bex, r0, mask)
                da = jnp.where(mask, _dot_nt(dob, vb), 0.0)
                dv_l[c] = _dot_tn(a.astype(BF16), dob) + _dot_nt((kc * ekl).astype(BF16), dstb)
                dq_inter = _dot(dob, st.astype(BF16)) * eb[rows]
                dk_state = _dot(vb, dstb) * ekl
                dq_parts, dk_intra = [], jnp.zeros((CHUNK, 128), F32)
                dab = da.astype(BF16)
                for i in range(CHUNK // SUB):
                    da_i = dab[SUB * i:SUB * (i + 1)]
                    ks_hi, ks_lo = _split2(ks_l[i])
                    qs_hi, qs_lo = _split2(qs_l[i])
                    dq_parts.append((_dot(da_i, ks_hi) + _dot(da_i, ks_lo)) * eq_l[i])
                    dk_intra = dk_intra + (_dot_tn(da_i, qs_hi) + _dot_tn(da_i, qs_lo)) * ek_l[i]
                dq = jnp.concatenate(dq_parts, axis=0) + dq_inter
                dk = dk_intra + dk_state
                last = (e_last * jnp.sum(st * dst, axis=0, keepdims=True)
                        + jnp.sum(kc * dk_state, axis=0, keepdims=True))
                dg_l[c] = _dot_ones(upper, qc * dq - kc * dk) + last
                dq_l[c], dk_l[c] = dq, dk
                dst = dst * e_last + _dot_tn(dob, (qc * eb[rows]).astype(BF16))
            dstate[h] = dst
            dq, dk = jnp.concatenate(dq_l, axis=0), jnp.concatenate(dk_l, axis=0)
            dg, dv = jnp.concatenate(dg_l, axis=0), jnp.concatenate(dv_l, axis=0)
            dhq_ref[:, cols] = (dq * _dsilu(hq)).astype(BF16)
            dhi_ref[:, cols] = dv.astype(BF16)
            df = dg / f - dk
            dhf_ref[:, cols] = (df * (1.0 - lb) * sg * (1.0 - sg)).astype(BF16)
            glb_ref[:, cols] += jnp.sum(df * (1.0 - sg), axis=0, keepdims=True)

        pl.when(pl.program_id(0) == ntb - 1)(end)

    def rev(t):
        return ntb - 1 - t

    def zcol(j):
        return pl.BlockSpec((HBLK, D), lambda t: (rev(t), j))

    blk = pl.BlockSpec((HBLK, D), lambda t: (rev(t), 0))
    any_spec = pl.BlockSpec(memory_space=pl.ANY)
    return pl.pallas_call(
        body, name="hgrn_bwd", grid=(ntb,),
        out_shape=[jax.ShapeDtypeStruct((S, IN_COLS), BF16)]
        + [jax.ShapeDtypeStruct((1, D), F32), jax.ShapeDtypeStruct((HEADS, 1, 128), F32)]
        + [jax.ShapeDtypeStruct((3,) + SHARD_SHAPES[a], BF16) for a in GATHER_IDS],
        in_specs=[zcol(0), zcol(1), zcol(2), zcol(3), blk, blk,
                  pl.BlockSpec((HEADS, nch, 128, 128), lambda t: (0, rev(t), 0, 0)),
                  pl.BlockSpec((1, D), lambda t: (0, 0)), pl.BlockSpec((1, 128), lambda t: (0, 0))]
        + [any_spec] * (n + 1),
        out_specs=[pl.BlockSpec((HBLK, DZ_ATT), lambda t: (rev(t), 0)), pl.BlockSpec((1, D), lambda t: (0, 0)),
                   pl.BlockSpec((HEADS, 1, 128), lambda t: (0, 0, 0))] + [any_spec] * n,
        scratch_shapes=[pltpu.VMEM((HEADS, 128, 128), F32), pltpu.SemaphoreType.DMA((3 * n,)),
                        pltpu.SemaphoreType.DMA((3 * n,))],
        input_output_aliases={9 + n: 0},
        compiler_params=_cp(("arbitrary",)),
    )(z, z, z, z, o, do_a, states, lbv, hnw, *partials, dz)


def _attn_bwd(z, qsb, ksb, vsb, cc, ss, ob, lse, do_bg, dz):
    def body(qs, ks, vs, ag_ref, cc_ref, ss_ref, ob_ref, lse_ref, dobg_ref, dz_in, dz_hbm,
             tmp, dos, dqs, dks, dvs, dkp, dvp, do_t, ls0_t, ls1_t, dl0_t, dl1_t, ls0, ls1, dl0, dl1,
             stage, stage_sem):
        pair, g = pl.program_id(0), pl.program_id(1)

        def out_copy(j):
            tile = DZ_ATT // LANES + (36 + pair if j == 3 else 12 * j + 4 * g + pair)
            return pltpu.make_async_copy(
                stage.at[j], dz_hbm.at[:, pl.ds(pl.multiple_of(tile * LANES, LANES), LANES)], stage_sem.at[j])

        def restage(j, value):
            pl.when(pair * 3 + g > 0)(lambda: out_copy(j).wait())
            stage[j] = value
            out_copy(j).start()

        pl.when(g == 2)(lambda: out_copy(3).wait())
        first_half = _half_mask()
        prev_ok, cur_ok = _attn_masks()
        lane = lax.broadcasted_iota(jnp.int32, (1, LANES), 1)
        heads = (lane < 64, lane >= 64)
        nblk = _group_blocks(g)
        cc_v, ss_v = cc_ref[...], ss_ref[...]

        @pl.when(g == 0)
        def _():
            ag, obv, dobg = ag_ref[...], ob_ref[...], dobg_ref[...]
            stage[3] = (dobg * obv * _dsilu(ag)).astype(BF16)
            out_copy(3).start()
            dob = dobg * _silu(ag)
            do_t[...] = dob
            prod = dob * obv
            dl = jnp.concatenate(
                [jnp.broadcast_to(jnp.sum(prod[:, 0:64], axis=-1, keepdims=True), (S, 64)),
                 jnp.broadcast_to(jnp.sum(prod[:, 64:128], axis=-1, keepdims=True), (S, 64))], axis=1)
            dl0_t[...] = dl

        _to_residues_dyn(g, dos, do_t)
        _to_residues_dyn(g, ls0, lse_ref)
        _to_residues_dyn(g, dl0, dl0_t)

        def unit(u, carry):
            start = pl.multiple_of(u * 128, 128)
            cur = pl.ds(start, 128)
            both = pl.ds(start, 256)
            pm = prev_ok & ((u & (nblk - 1)) != 0)
            qu, dou = qs[0, cur, :], dos[cur, :]
            kcat, vcat = ks[0, both, :], vs[0, both, :]
            dq_u = None
            q_l, do_l, ds_l, p_l = [], [], [], []
            ls_u, dl_u = ls0[cur, :], dl0[cur, :]
            ls_sw, dl_sw = pltpu.roll(ls_u, 64, 1), pltpu.roll(dl_u, 64, 1)
            for hh in range(2):
                q_h = jnp.where(heads[hh], qu, jnp.zeros((), BF16))
                do_h = jnp.where(heads[hh], dou, 0.0).astype(BF16)
                s = _dot_nt(q_h, kcat)
                dp = _dot_nt(do_h, vcat)
                lse_h = jnp.where(heads[hh], ls_u, ls_sw)
                dl_h = jnp.where(heads[hh], dl_u, dl_sw)
                pp = jnp.where(pm, jnp.exp(s[:, 0:128] - lse_h), 0.0)
                pc = jnp.where(cur_ok, jnp.exp(s[:, 128:256] - lse_h), 0.0)
                ds = jnp.concatenate([pp * (dp[:, 0:128] - dl_h), pc * (dp[:, 128:256] - dl_h)], axis=1).astype(BF16)
                dq = _dot(ds, kcat)
                dq_u = dq if hh == 0 else jnp.where(heads[1], dq, dq_u)
                q_l.append(q_h)
                do_l.append(do_h)
                ds_l.append(ds)
                p_l.append(jnp.concatenate([pp, pc], axis=1).astype(BF16))
            dkcat = _dot_tn(jnp.concatenate(ds_l, axis=0), jnp.concatenate(q_l, axis=0))
            dvcat = _dot_tn(jnp.concatenate(p_l, axis=0), jnp.concatenate(do_l, axis=0))
            dkp[cur, :] = dkcat[0:128]
            dks[cur, :] = dkcat[128:256]
            dvp[cur, :] = dvcat[0:128]
            dvs[cur, :] = dvcat[128:256]
            dqs[cur, :] = dq_u
            return carry

        lax.fori_loop(0, 16, unit, 0, unroll=ATT_UNROLL)
        dks[0:S - 128, :] += dkp[128:S, :]
        dvs[0:S - 128, :] += dvp[128:S, :]
        _from_residues_dyn(g, tmp, dqs)
        restage(0, (_rope(tmp[...], cc_v, -ss_v, first_half) * ATT_SCALE).astype(BF16))
        _from_residues_dyn(g, tmp, dks)
        restage(1, _rope(tmp[...], cc_v, -ss_v, first_half).astype(BF16))
        _from_residues_dyn(g, tmp, dvs)
        restage(2, tmp[...].astype(BF16))

        @pl.when(pair * 3 + g == 11)
        def _():
            for j in range(3):
                out_copy(j).wait()

    any_spec = pl.BlockSpec(memory_space=pl.ANY)
    buf = pltpu.VMEM((S, LANES), F32)
    padded_b = pltpu.VMEM((ATT_PAD + S, LANES), BF16)
    return pl.pallas_call(
        body, name="attn_bwd", grid=(4, 3),
        out_shape=jax.ShapeDtypeStruct((S, IN_COLS), BF16),
        in_specs=[pl.BlockSpec((1, S, LANES), lambda p, g: (g, 0, p)),
                  pl.BlockSpec((1, ATT_PAD + S, LANES), lambda p, g: (g, 0, p)),
                  pl.BlockSpec((1, ATT_PAD + S, LANES), lambda p, g: (g, 0, p))] + _attn_in_specs(3)[3:] + [any_spec],
        out_specs=any_spec,
        scratch_shapes=[buf] * 16 + [pltpu.VMEM((4, S, LANES), BF16), pltpu.SemaphoreType.DMA((4,))],
        input_output_aliases={9: 0},
        compiler_params=_cp(("arbitrary", "arbitrary")),
    )(qsb, ksb, vsb, z, cc, ss, ob, lse, do_bg, dz)


def _in_proj_bwd(dz, h, w_in):
    half = S // 2
    slab = (D, SHARD_COLS)

    def body(dz_hbm, h_hbm, w_hbm, dh_hbm, g_chip, r1_hbm, relay_hbm, r2_hbm,
             h_buf, dz_buf, stage_d, r1_buf, stage_i, acc,
             dz_sem, w_sem, h_sem, r1_sem, out_sem, send_d, recv_d, send_i, recv_i):
        x, y, c = _mesh_pos()
        sibling = (x, y, 1 - c)
        north = c == 1
        near = (jnp.where(north, 1 - x, x), jnp.where(north, y, 1 - y))
        far = (jnp.where(north, x, 1 - x), jnp.where(north, 1 - y, y))
        chips = [(1 - x, 1 - y), near, far, (x, y)]

        def cols(d):
            return pl.ds(pl.multiple_of(d * SHARD_COLS, LANES), SHARD_COLS)

        blocks = []
        for q_sib, q in zip([chips[0], far, near, chips[3]], chips):
            blocks += [4 * q_sib[0] + 2 * q_sib[1] + (1 - c), 4 * q[0] + 2 * q[1] + c]

        def dz_tile(t):
            return _SplitCopy(dz_hbm.at[pl.ds((t % 2) * half, half), cols(blocks[t // 2])],
                                         dz_buf.at[t % 2], dz_sem.at[t % 2])

        def to_sibling(i):
            return pltpu.make_async_remote_copy(
                src_ref=stage_d.at[i % 2], dst_ref=r1_hbm.at[i], send_sem=send_d.at[i], recv_sem=recv_d.at[i],
                device_id=sibling, device_id_type=MESH)

        def to_owner(i):
            dst = relay_hbm if i == 0 else r2_hbm.at[i - 1]
            return pltpu.make_async_remote_copy(
                src_ref=stage_i.at[i], dst_ref=dst, send_sem=send_i.at[i], recv_sem=recv_i.at[i],
                device_id=(*(far if i == 2 else near), c), device_id_type=MESH)

        h_copy = _SplitCopy(h_hbm, h_buf, h_sem)
        h_copy.start()
        dz_tile(0).start()
        h_copy.wait()
        for b in range(8):
            i = b // 2
            g = None
            for r in range(2):
                t = 2 * b + r
                if t + 1 < 16:
                    dz_tile(t + 1).start()
                dz_tile(t).wait()
                part = _dot_tn(h_buf[r * half:(r + 1) * half, :], dz_buf[t % 2])
                g = part if g is None else g + part
                if b % 2 == 1 and r == 0:
                    to_sibling(i).wait_recv()
                    r1_copy = _SplitCopy(r1_hbm.at[i], r1_buf, r1_sem)
                    r1_copy.start()
            if b % 2 == 0:
                if i >= 2:
                    to_sibling(i - 2).wait_send()
                stage_d[i % 2] = g.astype(BF16)
                to_sibling(i).start()
            else:
                r1_copy.wait()
                g = g + r1_buf[...].astype(F32)
                if i == 2:
                    to_owner(0).wait_recv()
                    relay_copy = _SplitCopy(relay_hbm, r1_buf, r1_sem)
                    relay_copy.start()
                    relay_copy.wait()
                    g = g + r1_buf[...].astype(F32)
                if i < 3:
                    stage_i[i] = g.astype(BF16)
                    to_owner(i).start()
                else:
                    g_chip[...] = g
        to_sibling(2).wait_send()
        to_sibling(3).wait_send()

        def dz2(t):
            return _SplitCopy(
                dz_hbm.at[pl.ds((t % 2) * half, half), pl.ds((t // 2) * SHARD_COLS, SHARD_COLS)],
                dz_buf.at[t % 2], dz_sem.at[t % 2])

        def w2(b):
            return _SplitCopy(w_hbm.at[:, pl.ds(b * SHARD_COLS, SHARD_COLS)],
                                         stage_d.at[b % 2], w_sem.at[b % 2])

        dz2(0).start()
        w2(0).start()
        for t in range(16):
            b, r = t // 2, t % 2
            if t + 1 < 16:
                dz2(t + 1).start()
            if r == 0:
                if b + 1 < 8:
                    w2(b + 1).start()
                w2(b).wait()
            dz2(t).wait()
            part = _dot_nt(dz_buf[t % 2], stage_d[b % 2])
            if b == 0:
                acc[r] = part
            else:
                acc[r] += part
        dh_out = [_SplitCopy(acc.at[r], dh_hbm.at[pl.ds(r * half, half), :], out_sem.at[r])
                  for r in range(2)]
        for cp in dh_out:
            cp.start()
        for cp in dh_out:
            cp.wait()
        for i in range(3):
            to_owner(i).wait_send()
        for i in (1, 2):
            to_owner(i).wait_recv()

    any_spec = pl.BlockSpec(memory_space=pl.ANY)
    return pl.pallas_call(
        body, name="in_proj_bwd",
        out_shape=[jax.ShapeDtypeStruct((S, D), F32), jax.ShapeDtypeStruct(slab, F32),
                   jax.ShapeDtypeStruct((4,) + slab, BF16), jax.ShapeDtypeStruct(slab, BF16),
                   jax.ShapeDtypeStruct((2,) + slab, BF16)],
        in_specs=[any_spec] * 3,
        out_specs=[any_spec, pl.BlockSpec(memory_space=pltpu.VMEM), any_spec, any_spec, any_spec],
        scratch_shapes=[pltpu.VMEM((S, D), BF16), pltpu.VMEM((2, half, SHARD_COLS), BF16),
                        pltpu.VMEM((2,) + slab, BF16), pltpu.VMEM(slab, BF16), pltpu.VMEM((3,) + slab, BF16),
                        pltpu.VMEM((2, half, D), F32),
                        pltpu.SemaphoreType.DMA((2,)), pltpu.SemaphoreType.DMA((2,)), pltpu.SemaphoreType.DMA,
                        pltpu.SemaphoreType.DMA, pltpu.SemaphoreType.DMA((2,)),
                        pltpu.SemaphoreType.DMA((4,)), pltpu.SemaphoreType.DMA((4,)),
                        pltpu.SemaphoreType.DMA((3,)), pltpu.SemaphoreType.DMA((3,))],
        compiler_params=_cp(),
    )(dz, h, w_in)


def _grad_x(x, norm_w, dh, dx2):
    tr = 256

    def body(x_ref, w_ref, dh_ref, dx2_ref, gx_ref, gnw_ref):
        @pl.when(pl.program_id(0) == 0)
        def _():
            gnw_ref[...] = jnp.zeros_like(gnw_ref)

        xv, dhv = x_ref[...], dh_ref[...]
        r = lax.rsqrt(jnp.mean(xv * xv, axis=-1, keepdims=True) + EPS)
        n = xv * r
        gnw_ref[...] += jnp.sum(dhv * n, axis=0, keepdims=True)
        dn = dhv * w_ref[...]
        gx_ref[...] = dx2_ref[...] + r * (dn - n * jnp.mean(dn * n, axis=-1, keepdims=True))

    row = pl.BlockSpec((tr, D), lambda i: (i, 0))
    vec = pl.BlockSpec((1, D), lambda i: (0, 0))
    return pl.pallas_call(
        body, name="grad_x", grid=(S // tr,),
        out_shape=[jax.ShapeDtypeStruct((S, D), F32), jax.ShapeDtypeStruct((1, D), F32)],
        in_specs=[row, vec, row, row], out_specs=[row, vec],
        compiler_params=_cp(("arbitrary",)),
    )(x, norm_w, dh, dx2)


def _rope_tables(positions):
    inv_freq = 10000.0 ** (-jnp.arange(0, 64, 2, dtype=F32) / 64)
    ang = positions.astype(F32)[:, None] * inv_freq[None, :]
    cos, sin = jnp.cos(ang), jnp.sin(ang)
    return jnp.tile(cos, (1, 4)), jnp.tile(jnp.concatenate([-sin, sin], axis=1), (1, 2))


def _local_step(x, positions, norm_w, lb_logits, hnw, fnw, target, w_in_shard, small_shards, core):
    cc, ss = _rope_tables(positions)
    lbv = jax.nn.sigmoid(lb_logits[0:1] - lb_logits[1:2])
    z, w_in, h = _in_proj_gather(x, norm_w, w_in_shard)
    o, o_a, states = _hgrn_fwd(z, lbv, hnw)
    ob, lse, o_bg, qsb, ksb, vsb, w_a, w_b, w_out = _attn_fwd(z, cc, ss, small_shards)
    dx2, dx2b, dz, do_a, do_bg, merged, dy_a, dy_b, tail_small = _tail(x, o_a, o_bg, z, target, w_a, w_b, w_out, fnw)
    g_out, gb_out = _tn_matmul(merged, dx2b, "grad_w_out")
    g_a, gb_a = _tn_matmul(o_a, dy_a, "grad_w_a")
    g_b, gb_b = _tn_matmul(o_bg, dy_b, "grad_w_b")
    grads, gb = (g_a, g_b, g_out), (gb_a, gb_b, gb_out)
    r1 = _exchange_sibling(GATHER_IDS, gb)
    pb = [_chip_partials(a, grads[i], r1[i], core) for i, a in enumerate(GATHER_IDS)]
    dz, glb, ghn, *r2 = _hgrn_bwd(z, o, do_a, states, lbv, hnw, pb, dz)
    dz = _attn_bwd(z, qsb, ksb, vsb, cc, ss, ob, lse, do_bg, dz)
    dh, g_chip_in, _, _, r2_in = _in_proj_bwd(dz, h, w_in)
    grad_x, gnw = _grad_x(x, norm_w, dh, dx2)
    ghn_row = jnp.pad(jnp.sum(ghn, axis=0), ((0, 0), (0, D - 128)))
    small = jnp.concatenate([gnw, glb, ghn_row, tail_small[0:2], jnp.zeros((3, D), F32)], axis=0)
    return grad_x, (g_chip_in, r2_in), grads, r1, r2, small


def kernel(x, positions, norm_w, w_in, lb_logits, hgrn_norm_w, w_branch_a, w_branch_b, w_out, final_norm_w, loss_target, m_norm_w, m_w_in, m_lb_logits, m_hgrn_norm_w, m_w_branch_a, m_w_branch_b, m_w_out, m_final_norm_w, v_norm_w, v_w_in, v_lb_logits, v_hgrn_norm_w, v_w_branch_a, v_w_branch_b, v_w_out, v_final_norm_w):
    ix, iy, ic = _mesh_pos()
    core = jnp.reshape(ic, (1,)).astype(jnp.int32)
    pos = jnp.stack([4 * ix + 2 * iy + ic, 2 * ix + iy]).astype(jnp.int32)

    shards = [w_in[0], w_branch_a[0], w_branch_b[0], w_out[0]]
    moments_m = [m_w_in[0], m_w_branch_a[0], m_w_branch_b[0], m_w_out[0]]
    moments_v = [v_w_in[0], v_w_branch_a[0], v_w_branch_b[0], v_w_out[0]]
    names = ("w_in", "w_a", "w_b", "w_out")
    ids = GATHER_IDS
    shards_b = _cast_shards(shards)

    fnw2 = final_norm_w.reshape(1, D)
    grad_x, (g_chip_in, r2_in), grads, r1, r2, small = _local_step(
        x[0], positions[0], norm_w, lb_logits, hgrn_norm_w, fnw2, loss_target[0], shards_b[0], shards_b[1:], core)

    gathered = _gather_small(small)
    big =[_reduce_own_and_update(shards[0], moments_m[0], moments_v[0], g_chip_in, r2_in)]
    big += [_reduce_and_update(a, shards[a], moments_m[a], moments_v[a], grads[i], r1[i], r2[i], pos)
            for i, a in enumerate(ids)]
    sm = _small_update(gathered, norm_w, lb_logits, hgrn_norm_w, fnw2,
                       (m_norm_w, m_lb_logits, m_hgrn_norm_w, m_final_norm_w.reshape(1, D),
                        v_norm_w, v_lb_logits, v_hgrn_norm_w, v_final_norm_w.reshape(1, D)))
    loss = sm[0][0, 0]
    outs = [loss, grad_x[None]]
    for kind in range(4):
        s_nw, s_lb, s_hn, s_fn = sm[1 + 4 * kind:5 + 4 * kind]
        outs += [s_nw, big[0][kind][None], s_lb, s_hn, big[1][kind][None], big[2][kind][None],
                 big[3][kind][None], s_fn.reshape(D)]
    return tuple(outs)
```
